```python
import math
import jax, jax.numpy as jnp
from jax import lax
import numpy as np

D_MODEL = 2048
BATCH = 8
SEQ = 8192
DEPTH = 4

CHUNK = 64
N_MIXERS = 3
EPS = 1e-6
MASK_VALUE = -1e30

ATT_HEADS = 16
ATT_HEAD_DIM = D_MODEL // ATT_HEADS
LEFT_CHUNKS = 8
BAND_LEFT = LEFT_CHUNKS * CHUNK
BAND = (LEFT_CHUNKS + 1) * CHUNK
MAX_REL = 256
NUM_REL = (CHUNK - 1) + MAX_REL + 1

POOL_WINDOWS = (2, 4, 8, 16)
POOL_GROUPS = len(POOL_WINDOWS)
POOL_DIM = D_MODEL // POOL_GROUPS

GDN_K_HEADS = 16
GDN_V_HEADS = 32
GDN_K_DIM = D_MODEL // GDN_K_HEADS
GDN_V_DIM = D_MODEL // GDN_K_HEADS
GDN_KEY = GDN_K_HEADS * GDN_K_DIM
GDN_VAL = GDN_V_HEADS * GDN_V_DIM
GDN_CONV = 4
GDN_CONV_CH = 2 * GDN_KEY + GDN_VAL
GDN_IN = GDN_CONV_CH + GDN_VAL + 2 * GDN_V_HEADS

D_FF = 128 * ((8 * D_MODEL // 3 + 127) // 128)
FFN_CONV = 3

N_ATT_LAYERS = (DEPTH + 2) // 3
N_POOL_LAYERS = (DEPTH + 1) // 3
N_GDN_LAYERS = DEPTH // 3

kernel_name = "hybrid_chunk_causal_encoder"


def rms_norm(x, gain):
    xf = x.astype(jnp.float32)
    y = xf * lax.rsqrt(jnp.mean(xf * xf, axis=-1, keepdims=True) + EPS)
    return (y * gain.astype(jnp.float32)).astype(x.dtype)


def l2_norm(x):
    xf = x.astype(jnp.float32)
    return xf * lax.rsqrt(jnp.sum(xf * xf, axis=-1, keepdims=True) + EPS)


def causal_depthwise_conv(x, w):
    k = w.shape[0]
    s = x.shape[1]
    xp = jnp.pad(x, ((0, 0), (k - 1, 0), (0, 0)))
    out = xp[:, 0:s] * w[0]
    for j in range(1, k):
        out = out + xp[:, j:j + s] * w[j]
    return out


def chunk_band_attention(h, w_qkv, q_gain, k_gain, rel_bias, w_o):
    b, s, d = h.shape
    nc = s // CHUNK
    qkv = h @ w_qkv
    q = qkv[..., :d].reshape(b, s, ATT_HEADS, ATT_HEAD_DIM)
    k = qkv[..., d:2 * d].reshape(b, s, ATT_HEADS, ATT_HEAD_DIM)
    v = qkv[..., 2 * d:].reshape(b, s, ATT_HEADS, ATT_HEAD_DIM)
    q = rms_norm(q, q_gain)
    k = rms_norm(k, k_gain)
    k_pad = jnp.pad(k, ((0, 0), (BAND_LEFT, 0), (0, 0), (0, 0)))
    v_pad = jnp.pad(v, ((0, 0), (BAND_LEFT, 0), (0, 0), (0, 0)))
    rel = BAND_LEFT + jnp.arange(CHUNK)[:, None] - jnp.arange(BAND)[None, :]
    rel_idx = jnp.clip(rel, -(CHUNK - 1), MAX_REL) + (CHUNK - 1)
    bias = rel_bias.astype(jnp.float32)[:, rel_idx]
    scale = ATT_HEAD_DIM ** -0.5
    q_chunks = q.reshape(b, nc, CHUNK, ATT_HEADS, ATT_HEAD_DIM).transpose(1, 0, 2, 3, 4)

    def per_chunk(args):
        c, qc = args
        kb = lax.dynamic_slice_in_dim(k_pad, c * CHUNK, BAND, axis=1)
        vb = lax.dynamic_slice_in_dim(v_pad, c * CHUNK, BAND, axis=1)
        sc = jnp.einsum('bqhd,bkhd->bhqk', qc, kb,
                        preferred_element_type=jnp.float32) * scale + bias
        valid = (c * CHUNK - BAND_LEFT + jnp.arange(BAND)) >= 0
        sc = jnp.where(valid[None, None, None, :], sc, MASK_VALUE)
        p = jax.nn.softmax(sc, axis=-1).astype(vb.dtype)
        return jnp.einsum('bhqk,bkhd->bqhd', p, vb)

    o = lax.map(per_chunk, (jnp.arange(nc, dtype=jnp.int32), q_chunks))
    o = o.transpose(1, 0, 2, 3, 4).reshape(b, s, d)
    return o @ w_o


def multiscale_pool_mixer(h, pool_w, pool_scale):
    b, s, d = h.shape
    hf = h.astype(jnp.float32)
    cs = jnp.cumsum(jnp.pad(hf, ((0, 0), (1, 0), (0, 0))), axis=1)
    pos = jnp.arange(s)
    groups = []
    for g, w in enumerate(POOL_WINDOWS):
        csg = cs[..., g * POOL_DIM:(g + 1) * POOL_DIM]
        upper = csg[:, 1:]
        lower = jnp.pad(csg[:, :s + 1 - w], ((0, 0), (w - 1, 0), (0, 0)))
        count = jnp.minimum(pos + 1, w).astype(jnp.float32)[None, :, None]
        groups.append((upper - lower) / count - hf[..., g * POOL_DIM:(g + 1) * POOL_DIM])
    pooled = jnp.stack(groups, axis=2)
    y = jnp.einsum('bsgc,gce->bsge', pooled, pool_w.astype(jnp.float32)).reshape(b, s, d)
    return (y * pool_scale.astype(jnp.float32)).astype(h.dtype)


def gated_delta_rule(q, k, v, g, beta):
    b, s, h, dk = q.shape
    dv = v.shape[-1]
    nc = s // CHUNK

    def to_chunks(t):
        return t.reshape(b, nc, CHUNK, h, -1).transpose(0, 3, 1, 2, 4)

    q, k, v = to_chunks(q), to_chunks(k), to_chunks(v)
    g = g.reshape(b, nc, CHUNK, h).transpose(0, 3, 1, 2)
    beta = beta.reshape(b, nc, CHUNK, h).transpose(0, 3, 1, 2)
    gc = jnp.cumsum(g, axis=-1)
    idx = jnp.arange(CHUNK)
    causal = idx[:, None] >= idx[None, :]
    strict = idx[:, None] > idx[None, :]
    diff = gc[..., :, None] - gc[..., None, :]
    decay = jnp.where(causal, jnp.exp(jnp.where(causal, diff, 0.0)), 0.0)
    kb = k * beta[..., None]
    vb = v * beta[..., None]
    a_strict = jnp.where(strict, jnp.einsum('bhncd,bhnjd->bhncj', kb, k) * decay, 0.0)
    eye = jnp.eye(CHUNK, dtype=jnp.float32)
    rhs = jnp.concatenate([vb, kb * jnp.exp(gc)[..., None]], axis=-1)
    sol = lax.linalg.triangular_solve(a_strict + eye, rhs, left_side=True, lower=True,
                                      unit_diagonal=True)
    u = sol[..., :dv]
    w = sol[..., dv:]
    attn = jnp.einsum('bhncd,bhnjd->bhncj', q, k) * decay
    qg = q * jnp.exp(gc)[..., None]
    k_state = k * jnp.exp(gc[..., -1:] - gc)[..., None]
    chunk_decay = jnp.exp(gc[..., -1])
    xs = tuple(jnp.moveaxis(t, 2, 0) for t in (qg, attn, u, w, k_state, chunk_decay))

    def step(state, inp):
        qg_c, attn_c, u_c, w_c, ks_c, dec_c = inp
        v_new = u_c - jnp.einsum('bhcd,bhde->bhce', w_c, state)
        o_c = (jnp.einsum('bhcd,bhde->bhce', qg_c, state)
               + jnp.einsum('bhcj,bhje->bhce', attn_c, v_new))
        state = state * dec_c[..., None, None] + jnp.einsum('bhcd,bhce->bhde', ks_c, v_new)
        return state, o_c

    state0 = jnp.zeros((b, h, dk, dv), jnp.float32)
    _, o = lax.scan(step, state0, xs)
    return o.transpose(1, 0, 3, 2, 4).reshape(b, s, h, dv)


def gated_deltanet_mixer(h, w_in, conv_w, a_log, dt_bias, o_gain, w_o):
    b, s, _ = h.shape
    proj = h @ w_in
    qkv = jax.nn.silu(causal_depthwise_conv(proj[..., :GDN_CONV_CH], conv_w))
    gate = proj[..., GDN_CONV_CH:GDN_CONV_CH + GDN_VAL]
    a = proj[..., GDN_CONV_CH + GDN_VAL:GDN_CONV_CH + GDN_VAL + GDN_V_HEADS]
    bt = proj[..., GDN_CONV_CH + GDN_VAL + GDN_V_HEADS:]
    q = qkv[..., :GDN_KEY].reshape(b, s, GDN_K_HEADS, GDN_K_DIM)
    k = qkv[..., GDN_KEY:2 * GDN_KEY].reshape(b, s, GDN_K_HEADS, GDN_K_DIM)
    v = qkv[..., 2 * GDN_KEY:].reshape(b, s, GDN_V_HEADS, GDN_V_DIM).astype(jnp.float32)
    q = l2_norm(q) * (GDN_K_DIM ** -0.5)
    k = l2_norm(k)
    rep = GDN_V_HEADS // GDN_K_HEADS
    q = jnp.repeat(q, rep, axis=2)
    k = jnp.repeat(k, rep, axis=2)
    beta = jax.nn.sigmoid(bt.astype(jnp.float32))
    g = -jnp.exp(a_log.astype(jnp.float32)) * jax.nn.softplus(
        a.astype(jnp.float32) + dt_bias.astype(jnp.float32))
    o = gated_delta_rule(q, k, v, g, beta)
    o = rms_norm(o, o_gain) * jax.nn.silu(
        gate.astype(jnp.float32).reshape(b, s, GDN_V_HEADS, GDN_V_DIM))
    return o.reshape(b, s, GDN_VAL).astype(h.dtype) @ w_o


def conv_ffn(h, w_up, conv_w, w_down):
    up = h @ w_up
    u = causal_depthwise_conv(up[..., :D_FF], conv_w)
    return (jax.nn.silu(u) * up[..., D_FF:]) @ w_down


def _normal(k, shape, scale):
    return scale * jax.random.normal(k, shape, jnp.float32)


def _fwd_setup_inputs(seed: int = 0) -> dict:
    key = jax.random.key(seed)
    ks = jax.random.split(key, 24)
    d = D_MODEL
    dt = jnp.exp(jax.random.uniform(ks[14], (N_GDN_LAYERS, GDN_V_HEADS), jnp.float32,
                                    minval=math.log(1e-3), maxval=math.log(1e-1)))
    return {
        "x": _normal(ks[0], (BATCH, SEQ, d), 1.0),
        "mix_norm": 1.0 + _normal(ks[1], (DEPTH, d), 0.02),
        "ffn_norm": 1.0 + _normal(ks[2], (DEPTH, d), 0.02),
        "att_w_qkv": _normal(ks[3], (N_ATT_LAYERS, d, 3 * d), d ** -0.5),
        "att_q_gain": 1.0 + _normal(ks[4], (N_ATT_LAYERS, ATT_HEAD_DIM), 0.02),
        "att_k_gain": 1.0 + _normal(ks[5], (N_ATT_LAYERS, ATT_HEAD_DIM), 0.02),
        "att_rel_bias": _normal(ks[6], (N_ATT_LAYERS, ATT_HEADS, NUM_REL), 0.2),
        "att_w_o": _normal(ks[7], (N_ATT_LAYERS, d, d), d ** -0.5),
        "pool_w": _normal(ks[8], (N_POOL_LAYERS, POOL_GROUPS, POOL_DIM, POOL_DIM),
                           POOL_DIM ** -0.5),
        "pool_scale": 1.0 + _normal(ks[9], (N_POOL_LAYERS, d), 0.1),
        "gdn_w_in": _normal(ks[10], (N_GDN_LAYERS, d, GDN_IN), d ** -0.5),
        "gdn_conv": _normal(ks[11], (N_GDN_LAYERS, GDN_CONV, GDN_CONV_CH), GDN_CONV ** -0.5),
        "gdn_a_log": jnp.log(jax.random.uniform(ks[12], (N_GDN_LAYERS, GDN_V_HEADS),
                                                 jnp.float32, minval=1.0, maxval=16.0)),
        "gdn_dt_bias": dt + jnp.log(-jnp.expm1(-dt)),
        "gdn_o_gain": 1.0 + _normal(ks[13], (N_GDN_LAYERS, GDN_V_DIM), 0.02),
        "gdn_w_o": _normal(ks[15], (N_GDN_LAYERS, GDN_VAL, d), GDN_VAL ** -0.5),
        "ffn_w_up": _normal(ks[16], (DEPTH, d, 2 * D_FF), d ** -0.5),
        "ffn_conv": _normal(ks[17], (DEPTH, FFN_CONV, D_FF), FFN_CONV ** -0.5),
        "ffn_w_down": _normal(ks[18], (DEPTH, D_FF, d), D_FF ** -0.5),
    }


def _fwd_reference(x, mix_norm, ffn_norm, att_w_qkv, att_q_gain, att_k_gain, att_rel_bias,
              att_w_o, pool_w, pool_scale, gdn_w_in, gdn_conv, gdn_a_log, gdn_dt_bias,
              gdn_o_gain, gdn_w_o, ffn_w_up, ffn_conv, ffn_w_down):
    for i in range(DEPTH):
        kind = i % N_MIXERS
        j = i // N_MIXERS
        h = rms_norm(x, mix_norm[i])
        if kind == 0:
            y = chunk_band_attention(h, att_w_qkv[j], att_q_gain[j], att_k_gain[j],
                                     att_rel_bias[j], att_w_o[j])
        elif kind == 1:
            y = multiscale_pool_mixer(h, pool_w[j], pool_scale[j])
        else:
            y = gated_deltanet_mixer(h, gdn_w_in[j], gdn_conv[j], gdn_a_log[j],
                                     gdn_dt_bias[j], gdn_o_gain[j], gdn_w_o[j])
        x = x + y
        h = rms_norm(x, ffn_norm[i])
        x = x + conv_ffn(h, ffn_w_up[i], ffn_conv[i], ffn_w_down[i])
    return x


import jax as _jax
import jax.numpy as _jnp

TWIN_FORMAT = 'train_step'
FWD_PARAMS = ['x', 'mix_norm', 'ffn_norm', 'att_w_qkv', 'att_q_gain', 'att_k_gain', 'att_rel_bias', 'att_w_o', 'pool_w', 'pool_scale', 'gdn_w_in', 'gdn_conv', 'gdn_a_log', 'gdn_dt_bias', 'gdn_o_gain', 'gdn_w_o', 'ffn_w_up', 'ffn_conv', 'ffn_w_down']
TWIN_WEIGHTS = ['mix_norm', 'ffn_norm', 'att_w_qkv', 'att_q_gain', 'att_k_gain', 'att_rel_bias', 'att_w_o', 'pool_w', 'pool_scale', 'gdn_w_in', 'gdn_conv', 'gdn_a_log', 'gdn_dt_bias', 'gdn_o_gain', 'gdn_w_o', 'ffn_w_up', 'ffn_conv', 'ffn_w_down']
TWIN_DIFF_INPUT = 'x'
TWIN_INPUTS = ['x', 'mix_norm', 'ffn_norm', 'att_w_qkv', 'att_q_gain', 'att_k_gain', 'att_rel_bias', 'att_w_o', 'pool_w', 'pool_scale', 'gdn_w_in', 'gdn_conv', 'gdn_a_log', 'gdn_dt_bias', 'gdn_o_gain', 'gdn_w_o', 'ffn_w_up', 'ffn_conv', 'ffn_w_down', 'loss_target', 'm_mix_norm', 'm_ffn_norm', 'm_att_w_qkv', 'm_att_q_gain', 'm_att_k_gain', 'm_att_rel_bias', 'm_att_w_o', 'm_pool_w', 'm_pool_scale', 'm_gdn_w_in', 'm_gdn_conv', 'm_gdn_a_log', 'm_gdn_dt_bias', 'm_gdn_o_gain', 'm_gdn_w_o', 'm_ffn_w_up', 'm_ffn_conv', 'm_ffn_w_down', 'v_mix_norm', 'v_ffn_norm', 'v_att_w_qkv', 'v_att_q_gain', 'v_att_k_gain', 'v_att_rel_bias', 'v_att_w_o', 'v_pool_w', 'v_pool_scale', 'v_gdn_w_in', 'v_gdn_conv', 'v_gdn_a_log', 'v_gdn_dt_bias', 'v_gdn_o_gain', 'v_gdn_w_o', 'v_ffn_w_up', 'v_ffn_conv', 'v_ffn_w_down']
TWIN_OUTPUTS = ['loss', 'grad_x', 'grad_mix_norm', 'grad_ffn_norm', 'grad_att_w_qkv', 'grad_att_q_gain', 'grad_att_k_gain', 'grad_att_rel_bias', 'grad_att_w_o', 'grad_pool_w', 'grad_pool_scale', 'grad_gdn_w_in', 'grad_gdn_conv', 'grad_gdn_a_log', 'grad_gdn_dt_bias', 'grad_gdn_o_gain', 'grad_gdn_w_o', 'grad_ffn_w_up', 'grad_ffn_conv', 'grad_ffn_w_down', 'delta_mix_norm', 'delta_ffn_norm', 'delta_att_w_qkv', 'delta_att_q_gain', 'delta_att_k_gain', 'delta_att_rel_bias', 'delta_att_w_o', 'delta_pool_w', 'delta_pool_scale', 'delta_gdn_w_in', 'delta_gdn_conv', 'delta_gdn_a_log', 'delta_gdn_dt_bias', 'delta_gdn_o_gain', 'delta_gdn_w_o', 'delta_ffn_w_up', 'delta_ffn_conv', 'delta_ffn_w_down', 'new_m_mix_norm', 'new_m_ffn_norm', 'new_m_att_w_qkv', 'new_m_att_q_gain', 'new_m_att_k_gain', 'new_m_att_rel_bias', 'new_m_att_w_o', 'new_m_pool_w', 'new_m_pool_scale', 'new_m_gdn_w_in', 'new_m_gdn_conv', 'new_m_gdn_a_log', 'new_m_gdn_dt_bias', 'new_m_gdn_o_gain', 'new_m_gdn_w_o', 'new_m_ffn_w_up', 'new_m_ffn_conv', 'new_m_ffn_w_down', 'new_v_mix_norm', 'new_v_ffn_norm', 'new_v_att_w_qkv', 'new_v_att_q_gain', 'new_v_att_k_gain', 'new_v_att_rel_bias', 'new_v_att_w_o', 'new_v_pool_w', 'new_v_pool_scale', 'new_v_gdn_w_in', 'new_v_gdn_conv', 'new_v_gdn_a_log', 'new_v_gdn_dt_bias', 'new_v_gdn_o_gain', 'new_v_gdn_w_o', 'new_v_ffn_w_up', 'new_v_ffn_conv', 'new_v_ffn_w_down']
TWIN_LEAF_KINDS = {'loss': 'loss', 'grad_x': 'grad_x', 'grad_mix_norm': 'grad_w', 'grad_ffn_norm': 'grad_w', 'grad_att_w_qkv': 'grad_w', 'grad_att_q_gain': 'grad_w', 'grad_att_k_gain': 'grad_w', 'grad_att_rel_bias': 'grad_w', 'grad_att_w_o': 'grad_w', 'grad_pool_w': 'grad_w', 'grad_pool_scale': 'grad_w', 'grad_gdn_w_in': 'grad_w', 'grad_gdn_conv': 'grad_w', 'grad_gdn_a_log': 'grad_w', 'grad_gdn_dt_bias': 'grad_w', 'grad_gdn_o_gain': 'grad_w', 'grad_gdn_w_o': 'grad_w', 'grad_ffn_w_up': 'grad_w', 'grad_ffn_conv': 'grad_w', 'grad_ffn_w_down': 'grad_w', 'delta_mix_norm': 'delta_w', 'delta_ffn_norm': 'delta_w', 'delta_att_w_qkv': 'delta_w', 'delta_att_q_gain': 'delta_w', 'delta_att_k_gain': 'delta_w', 'delta_att_rel_bias': 'delta_w', 'delta_att_w_o': 'delta_w', 'delta_pool_w': 'delta_w', 'delta_pool_scale': 'delta_w', 'delta_gdn_w_in': 'delta_w', 'delta_gdn_conv': 'delta_w', 'delta_gdn_a_log': 'delta_w', 'delta_gdn_dt_bias': 'delta_w', 'delta_gdn_o_gain': 'delta_w', 'delta_gdn_w_o': 'delta_w', 'delta_ffn_w_up': 'delta_w', 'delta_ffn_conv': 'delta_w', 'delta_ffn_w_down': 'delta_w', 'new_m_mix_norm': 'new_m', 'new_m_ffn_norm': 'new_m', 'new_m_att_w_qkv': 'new_m', 'new_m_att_q_gain': 'new_m', 'new_m_att_k_gain': 'new_m', 'new_m_att_rel_bias': 'new_m', 'new_m_att_w_o': 'new_m', 'new_m_pool_w': 'new_m', 'new_m_pool_scale': 'new_m', 'new_m_gdn_w_in': 'new_m', 'new_m_gdn_conv': 'new_m', 'new_m_gdn_a_log': 'new_m', 'new_m_gdn_dt_bias': 'new_m', 'new_m_gdn_o_gain': 'new_m', 'new_m_gdn_w_o': 'new_m', 'new_m_ffn_w_up': 'new_m', 'new_m_ffn_conv': 'new_m', 'new_m_ffn_w_down': 'new_m', 'new_v_mix_norm': 'new_v', 'new_v_ffn_norm': 'new_v', 'new_v_att_w_qkv': 'new_v', 'new_v_att_q_gain': 'new_v', 'new_v_att_k_gain': 'new_v', 'new_v_att_rel_bias': 'new_v', 'new_v_att_w_o': 'new_v', 'new_v_pool_w': 'new_v', 'new_v_pool_scale': 'new_v', 'new_v_gdn_w_in': 'new_v', 'new_v_gdn_conv': 'new_v', 'new_v_gdn_a_log': 'new_v', 'new_v_gdn_dt_bias': 'new_v', 'new_v_gdn_o_gain': 'new_v', 'new_v_gdn_w_o': 'new_v', 'new_v_ffn_w_up': 'new_v', 'new_v_ffn_conv': 'new_v', 'new_v_ffn_w_down': 'new_v'}


def _forward(args):
    return _fwd_reference(*[args[k] for k in FWD_PARAMS])


def _output_shape():
    def fwd():
        inp = _fwd_setup_inputs(0)
        return _fwd_reference(*[inp[k] for k in FWD_PARAMS])
    out = _jax.eval_shape(fwd)
    return out.shape, out.dtype

N_MICROBATCH = 1
ADAM_LR = 0.001
ADAM_B1 = 0.9
ADAM_B2 = 0.999
ADAM_EPS = 1e-08
ADAM_WD = 0.01
ADAM_STEP = 10
PER_EXAMPLE_BATCH_AXIS = {'x': 0, 'loss_target': 0}
SHARED_INPUTS = []
_WEIGHT_DTYPES = {'mix_norm': _jnp.float32, 'ffn_norm': _jnp.float32, 'att_w_qkv': _jnp.float32, 'att_q_gain': _jnp.float32, 'att_k_gain': _jnp.float32, 'att_rel_bias': _jnp.float32, 'att_w_o': _jnp.float32, 'pool_w': _jnp.float32, 'pool_scale': _jnp.float32, 'gdn_w_in': _jnp.float32, 'gdn_conv': _jnp.float32, 'gdn_a_log': _jnp.float32, 'gdn_dt_bias': _jnp.float32, 'gdn_o_gain': _jnp.float32, 'gdn_w_o': _jnp.float32, 'ffn_w_up': _jnp.float32, 'ffn_conv': _jnp.float32, 'ffn_w_down': _jnp.float32}
MOMENT_SCALE = {'mix_norm': 1.437766e+01, 'ffn_norm': 2.560217e+01, 'att_w_qkv': 1.099953e-01, 'att_q_gain': 1.656820e+00, 'att_k_gain': 1.653760e+00, 'att_rel_bias': 6.010707e-02, 'att_w_o': 1.856753e-01, 'pool_w': 1.330671e+00, 'pool_scale': 2.499533e+01, 'gdn_w_in': 2.132409e-01, 'gdn_conv': 3.981643e-01, 'gdn_a_log': 2.006074e+01, 'gdn_dt_bias': 1.910844e+01, 'gdn_o_gain': 1.780696e+02, 'gdn_w_o': 1.077660e+00, 'ffn_w_up': 2.495937e-01, 'ffn_conv': 2.854962e+00, 'ffn_w_down': 3.723133e-01}


def _to_microbatches(a, axis):
    t = _jnp.moveaxis(a, axis, 0)
    t = t.reshape((N_MICROBATCH, t.shape[0] // N_MICROBATCH) + t.shape[1:])
    return _jnp.moveaxis(t, 1, axis + 1)


def setup_inputs(seed: int = 0) -> dict:
    inp = _fwd_setup_inputs(seed)
    key = _jax.random.fold_in(_jax.random.key(seed), 7919)
    shape, _ = _output_shape()
    out = dict(inp)
    out["loss_target"] = _jax.random.normal(_jax.random.fold_in(key, 0), shape, _jnp.float32)
    for i, name in enumerate(TWIN_WEIGHTS):
        w = inp[name].astype(_jnp.float32)
        if MOMENT_SCALE is None:
            s = _jnp.sqrt(_jnp.mean(_jnp.square(w)) + 1e-30)
        else:
            s = MOMENT_SCALE[name]
        km, kv = _jax.random.split(_jax.random.fold_in(key, i + 1))
        out[name] = w
        out["m_" + name] = s * _jax.random.normal(km, w.shape, _jnp.float32)
        out["v_" + name] = (s * s) * _jax.random.uniform(kv, w.shape, _jnp.float32, 0.5, 1.5)
    if N_MICROBATCH > 1:
        for name, axis in PER_EXAMPLE_BATCH_AXIS.items():
            out[name] = _to_microbatches(out[name], axis)
    return {'x': out['x'], 'mix_norm': out['mix_norm'], 'ffn_norm': out['ffn_norm'], 'att_w_qkv': out['att_w_qkv'], 'att_q_gain': out['att_q_gain'], 'att_k_gain': out['att_k_gain'], 'att_rel_bias': out['att_rel_bias'], 'att_w_o': out['att_w_o'], 'pool_w': out['pool_w'], 'pool_scale': out['pool_scale'], 'gdn_w_in': out['gdn_w_in'], 'gdn_conv': out['gdn_conv'], 'gdn_a_log': out['gdn_a_log'], 'gdn_dt_bias': out['gdn_dt_bias'], 'gdn_o_gain': out['gdn_o_gain'], 'gdn_w_o': out['gdn_w_o'], 'ffn_w_up': out['ffn_w_up'], 'ffn_conv': out['ffn_conv'], 'ffn_w_down': out['ffn_w_down'], 'loss_target': out['loss_target'], 'm_mix_norm': out['m_mix_norm'], 'm_ffn_norm': out['m_ffn_norm'], 'm_att_w_qkv': out['m_att_w_qkv'], 'm_att_q_gain': out['m_att_q_gain'], 'm_att_k_gain': out['m_att_k_gain'], 'm_att_rel_bias': out['m_att_rel_bias'], 'm_att_w_o': out['m_att_w_o'], 'm_pool_w': out['m_pool_w'], 'm_pool_scale': out['m_pool_scale'], 'm_gdn_w_in': out['m_gdn_w_in'], 'm_gdn_conv': out['m_gdn_conv'], 'm_gdn_a_log': out['m_gdn_a_log'], 'm_gdn_dt_bias': out['m_gdn_dt_bias'], 'm_gdn_o_gain': out['m_gdn_o_gain'], 'm_gdn_w_o': out['m_gdn_w_o'], 'm_ffn_w_up': out['m_ffn_w_up'], 'm_ffn_conv': out['m_ffn_conv'], 'm_ffn_w_down': out['m_ffn_w_down'], 'v_mix_norm': out['v_mix_norm'], 'v_ffn_norm': out['v_ffn_norm'], 'v_att_w_qkv': out['v_att_w_qkv'], 'v_att_q_gain': out['v_att_q_gain'], 'v_att_k_gain': out['v_att_k_gain'], 'v_att_rel_bias': out['v_att_rel_bias'], 'v_att_w_o': out['v_att_w_o'], 'v_pool_w': out['v_pool_w'], 'v_pool_scale': out['v_pool_scale'], 'v_gdn_w_in': out['v_gdn_w_in'], 'v_gdn_conv': out['v_gdn_conv'], 'v_gdn_a_log': out['v_gdn_a_log'], 'v_gdn_dt_bias': out['v_gdn_dt_bias'], 'v_gdn_o_gain': out['v_gdn_o_gain'], 'v_gdn_w_o': out['v_gdn_w_o'], 'v_ffn_w_up': out['v_ffn_w_up'], 'v_ffn_conv': out['v_ffn_conv'], 'v_ffn_w_down': out['v_ffn_w_down']}


def _loss(weights, diff, rest, loss_target):
    with _jax.named_scope("forward"):
        args = {**rest, TWIN_DIFF_INPUT: diff, **{k: w.astype(_WEIGHT_DTYPES[k]) for k, w in weights.items()}}
        y = _forward(args)
    with _jax.named_scope("loss_head"):
        err = _jnp.square(y.astype(_jnp.float32) - loss_target)
        return 0.5 * _jnp.sum(_jnp.mean(err, axis=-1)) if err.ndim else 0.5 * err


def _adamw(w, g, m, v):
    m = ADAM_B1 * m + (1.0 - ADAM_B1) * g
    v = ADAM_B2 * v + (1.0 - ADAM_B2) * _jnp.square(g)
    m_hat = m / (1.0 - ADAM_B1 ** ADAM_STEP)
    v_hat = v / (1.0 - ADAM_B2 ** ADAM_STEP)
    delta = -ADAM_LR * (m_hat / (_jnp.sqrt(v_hat) + ADAM_EPS) + ADAM_WD * w)
    return delta, m, v


def reference(x, mix_norm, ffn_norm, att_w_qkv, att_q_gain, att_k_gain, att_rel_bias, att_w_o, pool_w, pool_scale, gdn_w_in, gdn_conv, gdn_a_log, gdn_dt_bias, gdn_o_gain, gdn_w_o, ffn_w_up, ffn_conv, ffn_w_down, loss_target, m_mix_norm, m_ffn_norm, m_att_w_qkv, m_att_q_gain, m_att_k_gain, m_att_rel_bias, m_att_w_o, m_pool_w, m_pool_scale, m_gdn_w_in, m_gdn_conv, m_gdn_a_log, m_gdn_dt_bias, m_gdn_o_gain, m_gdn_w_o, m_ffn_w_up, m_ffn_conv, m_ffn_w_down, v_mix_norm, v_ffn_norm, v_att_w_qkv, v_att_q_gain, v_att_k_gain, v_att_rel_bias, v_att_w_o, v_pool_w, v_pool_scale, v_gdn_w_in, v_gdn_conv, v_gdn_a_log, v_gdn_dt_bias, v_gdn_o_gain, v_gdn_w_o, v_ffn_w_up, v_ffn_conv, v_ffn_w_down):
    given = dict(x=x, mix_norm=mix_norm, ffn_norm=ffn_norm, att_w_qkv=att_w_qkv, att_q_gain=att_q_gain, att_k_gain=att_k_gain, att_rel_bias=att_rel_bias, att_w_o=att_w_o, pool_w=pool_w, pool_scale=pool_scale, gdn_w_in=gdn_w_in, gdn_conv=gdn_conv, gdn_a_log=gdn_a_log, gdn_dt_bias=gdn_dt_bias, gdn_o_gain=gdn_o_gain, gdn_w_o=gdn_w_o, ffn_w_up=ffn_w_up, ffn_conv=ffn_conv, ffn_w_down=ffn_w_down, loss_target=loss_target, m_mix_norm=m_mix_norm, m_ffn_norm=m_ffn_norm, m_att_w_qkv=m_att_w_qkv, m_att_q_gain=m_att_q_gain, m_att_k_gain=m_att_k_gain, m_att_rel_bias=m_att_rel_bias, m_att_w_o=m_att_w_o, m_pool_w=m_pool_w, m_pool_scale=m_pool_scale, m_gdn_w_in=m_gdn_w_in, m_gdn_conv=m_gdn_conv, m_gdn_a_log=m_gdn_a_log, m_gdn_dt_bias=m_gdn_dt_bias, m_gdn_o_gain=m_gdn_o_gain, m_gdn_w_o=m_gdn_w_o, m_ffn_w_up=m_ffn_w_up, m_ffn_conv=m_ffn_conv, m_ffn_w_down=m_ffn_w_down, v_mix_norm=v_mix_norm, v_ffn_norm=v_ffn_norm, v_att_w_qkv=v_att_w_qkv, v_att_q_gain=v_att_q_gain, v_att_k_gain=v_att_k_gain, v_att_rel_bias=v_att_rel_bias, v_att_w_o=v_att_w_o, v_pool_w=v_pool_w, v_pool_scale=v_pool_scale, v_gdn_w_in=v_gdn_w_in, v_gdn_conv=v_gdn_conv, v_gdn_a_log=v_gdn_a_log, v_gdn_dt_bias=v_gdn_dt_bias, v_gdn_o_gain=v_gdn_o_gain, v_gdn_w_o=v_gdn_w_o, v_ffn_w_up=v_ffn_w_up, v_ffn_conv=v_ffn_conv, v_ffn_w_down=v_ffn_w_down)
    weights = {n: given[n] for n in TWIN_WEIGHTS}
    shared = {n: given[n] for n in SHARED_INPUTS}
    per_example = {n: given[n] for n in ['x']}
    grad_fn = _jax.value_and_grad(_loss, argnums=(0, 1))

    def one_microbatch(ex, loss_target):
        ex = dict(ex)
        diff = ex.pop(TWIN_DIFF_INPUT)
        return grad_fn(weights, diff, {**shared, **ex}, loss_target)

    if N_MICROBATCH == 1:
        loss, (grad_w, grad_x) = one_microbatch(per_example, given["loss_target"])
    else:
        def body(carry, xs):
            loss_sum, grad_sum = carry
            l_k, (gw_k, gx_k) = one_microbatch(xs[0], xs[1])
            with _jax.named_scope("update"):
                return (loss_sum + l_k, _jax.tree.map(_jnp.add, grad_sum, gw_k)), gx_k

        init = (_jnp.zeros((), _jnp.float32), _jax.tree.map(_jnp.zeros_like, weights))
        (loss, grad_w), grad_x = _jax.lax.scan(body, init, (per_example, given["loss_target"]))
    with _jax.named_scope("update"):
        delta_w, new_m, new_v = {}, {}, {}
        for n in TWIN_WEIGHTS:
            delta_w[n], new_m[n], new_v[n] = _adamw(weights[n], grad_w[n], given["m_" + n], given["v_" + n])
    return (loss, grad_x, *[grad_w[n] for n in TWIN_WEIGHTS], *[delta_w[n] for n in TWIN_WEIGHTS],
            *[new_m[n] for n in TWIN_WEIGHTS], *[new_v[n] for n in TWIN_WEIGHTS])
```

```python
import functools

import numpy as np
import jax
import jax.numpy as jnp
from jax import lax
from jax.experimental import pallas as pl
from jax.experimental.pallas import tpu as pltpu

F32 = jnp.float32
BF = jnp.bfloat16
SDS = jax.ShapeDtypeStruct

EPS = 1e-6
MASK_VALUE = -1e30
CHUNK = 64
LEFT_CHUNKS = 8
BAND_LEFT = LEFT_CHUNKS * CHUNK
BAND = BAND_LEFT + CHUNK
MAX_REL = 256
NUM_REL = (CHUNK - 1) + MAX_REL + 1
HEAD = 128
LANE = 128
HALO = 16
POOL_WINDOWS = (2, 4, 8, 16)
GDN_CONV = 4
FFN_CONV = 3
FF_PAD = 512
N_DEV = 8
AXES = ("x", "y", "c")
VMEM_LIMIT = 56 * 1024 * 1024

ADAM_LR = 0.001
ADAM_B1 = 0.9
ADAM_B2 = 0.999
ADAM_EPS = 1e-08
ADAM_WD = 0.01
ADAM_STEP = 10

HI = lax.Precision.HIGHEST
NT = (((1,), (1,)), ((), ()))
TN = (((0,), (0,)), ((), ()))


def _tile(n, target, mult=LANE):
    if n <= target:
        return n
    t = (target // mult) * mult
    while t >= mult:
        if n % t == 0:
            return t
        t -= mult
    return n


def _params(*sem):
    return pltpu.CompilerParams(dimension_semantics=sem, vmem_limit_bytes=VMEM_LIMIT)


def _silu(x):
    return x / (1.0 + jnp.exp(-x))


@functools.partial(jax.custom_vjp, nondiff_argnums=(1,))
def _shift(x, k):
    return pltpu.roll(x, k % x.shape[0], axis=0)


def _shift_fwd(x, k):
    return _shift(x, k), None


def _shift_bwd(k, _, g):
    return (pltpu.roll(g, (-k) % g.shape[0], axis=0),)


_shift.defvjp(_shift_fwd, _shift_bwd)


def _mm(a, b, *, name, ta=False, tb=False, out_dtype=BF, res=None, tm=1024, tn=1024, tk=1024):
    m, k = (a.shape[1], a.shape[0]) if ta else a.shape
    n, kb = (b.shape[0], b.shape[1]) if tb else (b.shape[1], b.shape[0])
    assert k == kb, (a.shape, b.shape, ta, tb)
    tm, tn, tk = _tile(m, tm), _tile(n, tn), _tile(k, tk)
    nk = k // tk
    dims = (((0 if ta else 1,), (1 if tb else 0,)), ((), ()))

    def body(*refs):
        if res is None:
            a_ref, b_ref, o_ref, acc = refs
        else:
            a_ref, b_ref, r_ref, o_ref, acc = refs
        kk = pl.program_id(2)

        @pl.when(kk == 0)
        def _():
            acc[...] = jnp.zeros_like(acc)

        acc[...] += lax.dot_general(a_ref[...].astype(BF), b_ref[...].astype(BF), dims, preferred_element_type=F32)

        @pl.when(kk == nk - 1)
        def _():
            r = acc[...]
            if res is not None:
                r = r + r_ref[...].astype(F32)
            o_ref[...] = r.astype(out_dtype)

    a_spec = pl.BlockSpec((tk, tm), lambda i, j, q: (q, i)) if ta else pl.BlockSpec((tm, tk), lambda i, j, q: (i, q))
    b_spec = pl.BlockSpec((tn, tk), lambda i, j, q: (j, q)) if tb else pl.BlockSpec((tk, tn), lambda i, j, q: (q, j))
    o_spec = pl.BlockSpec((tm, tn), lambda i, j, q: (i, j))
    ins, specs = [a, b], [a_spec, b_spec]
    if res is not None:
        ins.append(res)
        specs.append(o_spec)
    return pl.pallas_call(
        body, name=name, grid=(m // tm, n // tn, nk), in_specs=specs, out_specs=o_spec,
        out_shape=SDS((m, n), out_dtype), scratch_shapes=[pltpu.VMEM((tm, tn), F32)],
        compiler_params=_params("parallel", "parallel", "arbitrary"))(*ins)


def _rms(x, gain):
    return x * lax.rsqrt(jnp.mean(x * x, axis=-1, keepdims=True) + EPS) * gain


def _rmsnorm_fwd(x, gain, name):
    s, d = x.shape
    ts = _tile(s, 256, 8)

    def body(x_ref, g_ref, o_ref):
        o_ref[...] = _rms(x_ref[...], g_ref[...]).astype(BF)

    return pl.pallas_call(
        body, name=name, grid=(s // ts,),
        in_specs=[pl.BlockSpec((ts, d), lambda i: (i, 0)), pl.BlockSpec((1, d), lambda i: (0, 0))],
        out_specs=pl.BlockSpec((ts, d), lambda i: (i, 0)), out_shape=SDS((s, d), BF),
        compiler_params=_params("parallel"))(x, gain)


def _rmsnorm_bwd(x, gain, dh, dres, name):
    s, d = x.shape
    ts = _tile(s, 256, 8)

    def body(x_ref, g_ref, dh_ref, dr_ref, dx_ref, dg_ref):
        i = pl.program_id(0)
        _, vjp = jax.vjp(_rms, x_ref[...], g_ref[...])
        dx, dg = vjp(dh_ref[...].astype(F32))
        dx_ref[...] = dr_ref[...] + dx

        @pl.when(i == 0)
        def _():
            dg_ref[...] = dg

        @pl.when(i > 0)
        def _():
            dg_ref[...] += dg

    row = pl.BlockSpec((ts, d), lambda i: (i, 0))
    vec = pl.BlockSpec((1, d), lambda i: (0, 0))
    return pl.pallas_call(
        body, name=name, grid=(s // ts,), in_specs=[row, vec, row, row], out_specs=[row, vec],
        out_shape=[SDS((s, d), F32), SDS((1, d), F32)], compiler_params=_params("arbitrary"))(x, gain, dh, dres)


def _ffn_act_tile(u_ext, g_ext, cw):
    acc = u_ext * cw[2:3] + _shift(u_ext, 1) * cw[1:2] + _shift(u_ext, 2) * cw[0:1]
    return _silu(acc) * g_ext


def _halo_index(rows_per_block):
    per = rows_per_block // HALO
    return lambda rb: jnp.maximum(rb * per - 1, 0)


def _ffn_act_fwd(u, g, cw, name):
    s, f = u.shape
    r, tc = _tile(s, 512, HALO), _tile(f, 512)
    hidx = _halo_index(r)

    def body(uc, uh, gc, cw_ref, o_ref):
        rb = pl.program_id(1)
        keep = jnp.where(rb == 0, 0.0, 1.0)
        u_ext = jnp.concatenate([uh[...].astype(F32) * keep, uc[...].astype(F32)], axis=0)
        g_ext = jnp.concatenate([jnp.zeros((HALO, tc), F32), gc[...].astype(F32)], axis=0)
        o_ref[...] = _ffn_act_tile(u_ext, g_ext, cw_ref[...])[HALO:].astype(BF)

    cur = pl.BlockSpec((r, tc), lambda j, rb: (rb, j))
    return pl.pallas_call(
        body, name=name, grid=(f // tc, s // r),
        in_specs=[cur, pl.BlockSpec((HALO, tc), lambda j, rb: (hidx(rb), j)), cur,
                  pl.BlockSpec((8, tc), lambda j, rb: (0, j))],
        out_specs=cur, out_shape=SDS((s, f), BF), compiler_params=_params("parallel", "parallel"))(u, u, g, cw)


def _ffn_act_bwd(u, g, cw, da, name):
    s, f = u.shape
    r, tc = _tile(s, 512, HALO), _tile(f, 512)
    nb = s // r
    hidx = _halo_index(r)

    def body(uc, uh, gc, da_ref, cw_ref, du_ref, dg_ref, dcw_ref, carry):
        step = pl.program_id(1)
        rb = nb - 1 - step
        keep = jnp.where(rb == 0, 0.0, 1.0)
        zero = jnp.zeros((HALO, tc), F32)
        u_ext = jnp.concatenate([uh[...].astype(F32) * keep, uc[...].astype(F32)], axis=0)
        g_ext = jnp.concatenate([zero, gc[...].astype(F32)], axis=0)
        ct = jnp.concatenate([zero, da_ref[...].astype(F32)], axis=0)
        _, vjp = jax.vjp(_ffn_act_tile, u_ext, g_ext, cw_ref[...])
        du_ext, dg_ext, dcw = vjp(ct)

        @pl.when(step == 0)
        def _():
            carry[...] = zero
            dcw_ref[...] = jnp.zeros_like(dcw_ref)

        du_ref[pl.ds(0, r - HALO), :] = du_ext[HALO:r].astype(BF)
        du_ref[pl.ds(r - HALO, HALO), :] = (du_ext[r:] + carry[...]).astype(BF)
        carry[...] = du_ext[:HALO]
        dg_ref[...] = dg_ext[HALO:].astype(BF)
        dcw_ref[...] += dcw

    cur = pl.BlockSpec((r, tc), lambda j, t: (nb - 1 - t, j))
    wspec = pl.BlockSpec((8, tc), lambda j, t: (0, j))
    return pl.pallas_call(
        body, name=name, grid=(f // tc, nb),
        in_specs=[cur, pl.BlockSpec((HALO, tc), lambda j, t: (hidx(nb - 1 - t), j)), cur, cur, wspec],
        out_specs=[cur, cur, wspec], out_shape=[SDS((s, f), BF), SDS((s, f), BF), SDS((8, f), F32)],
        scratch_shapes=[pltpu.VMEM((HALO, tc), F32)],
        compiler_params=_params("parallel", "arbitrary"))(u, u, g, da, cw)


def _loss_head(y, target, name):
    s, d = y.shape
    ts = _tile(s, 256, 8)

    def body(y_ref, t_ref, dy_ref, l_ref):
        i = pl.program_id(0)
        err = y_ref[...] - t_ref[...]
        dy_ref[...] = err * (1.0 / d)
        part = jnp.zeros((1, LANE), F32) + 0.5 * jnp.sum(jnp.sum(err * err, axis=1, keepdims=True), axis=0, keepdims=True) / d

        @pl.when(i == 0)
        def _():
            l_ref[...] = part

        @pl.when(i > 0)
        def _():
            l_ref[...] += part

    row = pl.BlockSpec((ts, d), lambda i: (i, 0))
    return pl.pallas_call(
        body, name=name, grid=(s // ts,), in_specs=[row, row],
        out_specs=[row, pl.BlockSpec((1, LANE), lambda i: (0, 0))],
        out_shape=[SDS((s, d), F32), SDS((1, LANE), F32)], compiler_params=_params("arbitrary"))(y, target)


def _rel_index():
    rel = BAND_LEFT + np.arange(CHUNK)[:, None] - np.arange(BAND)[None, :]
    return (np.clip(rel, -(CHUNK - 1), MAX_REL) + (CHUNK - 1)).reshape(1, CHUNK * BAND).astype(np.int32)


def _onehot(idx_row):
    rows = lax.broadcasted_iota(jnp.int32, (NUM_REL, idx_row.shape[1]), 0)
    return jnp.where(rows == idx_row, 1.0, 0.0).astype(F32)


def _bias_expand(rel_bias, name):
    h = rel_bias.shape[0]
    n = CHUNK * BAND
    tn = n // 8

    def body(rb_ref, idx_ref, o_ref):
        o_ref[...] = jnp.dot(rb_ref[...], _onehot(idx_ref[...]), precision=HI, preferred_element_type=F32)

    out = pl.pallas_call(
        body, name=name, grid=(n // tn,),
        in_specs=[pl.BlockSpec((h, NUM_REL), lambda j: (0, 0)), pl.BlockSpec((1, tn), lambda j: (0, j))],
        out_specs=pl.BlockSpec((h, tn), lambda j: (0, j)), out_shape=SDS((h, n), F32),
        compiler_params=_params("parallel"))(rel_bias, jnp.asarray(_rel_index()))
    return out.reshape(h, CHUNK, BAND)


def _bias_reduce(dbias, name):
    h = dbias.shape[0]
    n = CHUNK * BAND
    tn = n // 8

    def body(db_ref, idx_ref, o_ref):
        j = pl.program_id(0)
        part = lax.dot_general(db_ref[...], _onehot(idx_ref[...]), NT, precision=HI, preferred_element_type=F32)

        @pl.when(j == 0)
        def _():
            o_ref[...] = part

        @pl.when(j > 0)
        def _():
            o_ref[...] += part

    return pl.pallas_call(
        body, name=name, grid=(n // tn,),
        in_specs=[pl.BlockSpec((h, tn), lambda j: (0, j)), pl.BlockSpec((1, tn), lambda j: (0, j))],
        out_specs=pl.BlockSpec((h, NUM_REL), lambda j: (0, 0)), out_shape=SDS((h, NUM_REL), F32),
        compiler_params=_params("arbitrary"))(dbias.reshape(h, n), jnp.asarray(_rel_index()))


def _headnorm(x, gain):
    outs = []
    for hh in range(x.shape[1] // HEAD):
        xh = x[:, hh * HEAD:(hh + 1) * HEAD]
        outs.append(xh * lax.rsqrt(jnp.mean(xh * xh, axis=-1, keepdims=True) + EPS) * gain)
    return jnp.concatenate(outs, axis=1)


def _qkv_post_fwd(qkv, qg, kg, name):
    s, d3 = qkv.shape
    d = d3 // 3
    r = BAND_LEFT
    nb = s // r

    def body(x_ref, qg_ref, kg_ref, q_ref, k_ref, v_ref):
        i = pl.program_id(0)
        keep = jnp.where(i == 0, 0.0, 1.0)
        q_ref[...] = _headnorm(x_ref[:, 0:d].astype(F32), qg_ref[...]).astype(BF)
        k_ref[...] = (_headnorm(x_ref[:, d:2 * d].astype(F32), kg_ref[...]) * keep).astype(BF)
        v_ref[...] = (x_ref[:, 2 * d:].astype(F32) * keep).astype(BF)

    prev = lambda i: (jnp.maximum(i - 1, 0), 0)
    vec = pl.BlockSpec((1, HEAD), lambda i: (0, 0))
    return pl.pallas_call(
        body, name=name, grid=(nb + 1,),
        in_specs=[pl.BlockSpec((r, d3), prev), vec, vec],
        out_specs=[pl.BlockSpec((r, d), prev), pl.BlockSpec((r, d), lambda i: (i, 0)), pl.BlockSpec((r, d), lambda i: (i, 0))],
        out_shape=[SDS((s, d), BF), SDS((s + r, d), BF), SDS((s + r, d), BF)],
        compiler_params=_params("arbitrary"))(qkv, qg, kg)


def _qkv_post_bwd(qkv, qg, kg, dq, dkpad, dvpad, name):
    s, d3 = qkv.shape
    d = d3 // 3
    r = _tile(s, 256, 16)
    off = BAND_LEFT // r

    def body(x_ref, qg_ref, kg_ref, dq_ref, dk_ref, dv_ref, o_ref, dqg_ref, dkg_ref):
        i = pl.program_id(0)
        _, vq = jax.vjp(_headnorm, x_ref[:, 0:d].astype(F32), qg_ref[...])
        dxq, dqg = vq(dq_ref[...])
        _, vk = jax.vjp(_headnorm, x_ref[:, d:2 * d].astype(F32), kg_ref[...])
        dxk, dkg = vk(dk_ref[...])
        o_ref[:, 0:d] = dxq.astype(BF)
        o_ref[:, d:2 * d] = dxk.astype(BF)
        o_ref[:, 2 * d:] = dv_ref[...].astype(BF)

        @pl.when(i == 0)
        def _():
            dqg_ref[...] = dqg
            dkg_ref[...] = dkg

        @pl.when(i > 0)
        def _():
            dqg_ref[...] += dqg
            dkg_ref[...] += dkg

    vec = pl.BlockSpec((1, HEAD), lambda i: (0, 0))
    row3 = pl.BlockSpec((r, d3), lambda i: (i, 0))
    row = pl.BlockSpec((r, d), lambda i: (i, 0))
    padrow = pl.BlockSpec((r, d), lambda i: (i + off, 0))
    return pl.pallas_call(
        body, name=name, grid=(s // r,), in_specs=[row3, vec, vec, row, padrow, padrow],
        out_specs=[row3, vec, vec], out_shape=[SDS((s, d3), BF), SDS((1, HEAD), F32), SDS((1, HEAD), F32)],
        compiler_params=_params("arbitrary"))(qkv, qg, kg, dq, dkpad, dvpad)


ATT_QB = 256


def _att_probs(q, kw, bias, c):
    sc = lax.dot_general(q, kw, NT, preferred_element_type=F32) * (HEAD ** -0.5) + bias
    lane = lax.broadcasted_iota(jnp.int32, (CHUNK, BAND), 1)
    sc = jnp.where(lane + c * CHUNK >= BAND_LEFT, sc, MASK_VALUE)
    p = jnp.exp(sc - jnp.max(sc, axis=-1, keepdims=True))
    return p / jnp.sum(p, axis=-1, keepdims=True)


def _attn_fwd(q, kpad, vpad, bias, name):
    s, d = q.shape
    h = d // HEAD
    sp = kpad.shape[0]
    qb = _tile(s, ATT_QB, CHUNK)
    per = qb // CHUNK

    def body(q_ref, k_ref, v_ref, b_ref, o_ref):
        i = pl.program_id(1)
        bias_h = b_ref[0]
        for cc in range(per):
            c = i * per + cc
            start = pl.multiple_of(c * CHUNK, CHUNK)
            p = _att_probs(q_ref[pl.ds(cc * CHUNK, CHUNK), :], k_ref[pl.ds(start, BAND), :], bias_h, c)
            o_ref[pl.ds(cc * CHUNK, CHUNK), :] = jnp.dot(
                p.astype(BF), v_ref[pl.ds(start, BAND), :], preferred_element_type=F32).astype(BF)

    qspec = pl.BlockSpec((qb, HEAD), lambda hh, i: (i, hh))
    kspec = pl.BlockSpec((sp, HEAD), lambda hh, i: (0, hh))
    return pl.pallas_call(
        body, name=name, grid=(h, s // qb),
        in_specs=[qspec, kspec, kspec, pl.BlockSpec((1, CHUNK, BAND), lambda hh, i: (hh, 0, 0))],
        out_specs=qspec, out_shape=SDS((s, d), BF), compiler_params=_params("parallel", "arbitrary"))(q, kpad, vpad, bias)


def _attn_bwd(q, kpad, vpad, bias, do, name):
    s, d = q.shape
    h = d // HEAD
    sp = kpad.shape[0]
    qb = _tile(s, ATT_QB, CHUNK)
    per = qb // CHUNK
    scale = HEAD ** -0.5

    def body(q_ref, k_ref, v_ref, b_ref, do_ref, dq_ref, dk_ref, dv_ref, db_ref):
        i = pl.program_id(1)

        @pl.when(i == 0)
        def _():
            dk_ref[...] = jnp.zeros_like(dk_ref)
            dv_ref[...] = jnp.zeros_like(dv_ref)
            db_ref[...] = jnp.zeros_like(db_ref)

        bias_h = b_ref[0]
        for cc in range(per):
            c = i * per + cc
            start = pl.multiple_of(c * CHUNK, CHUNK)
            rows = pl.ds(cc * CHUNK, CHUNK)
            win = pl.ds(start, BAND)
            qc, kw, vw = q_ref[rows, :], k_ref[win, :], v_ref[win, :]
            doc = do_ref[rows, :].astype(BF)
            p = _att_probs(qc, kw, bias_h, c)
            dp = lax.dot_general(doc, vw, NT, preferred_element_type=F32)
            ds = p * (dp - jnp.sum(p * dp, axis=-1, keepdims=True))
            db_ref[0] += ds
            dsb = (ds * scale).astype(BF)
            dq_ref[rows, :] = jnp.dot(dsb, kw, preferred_element_type=F32)
            dk_ref[win, :] += lax.dot_general(dsb, qc, TN, preferred_element_type=F32)
            dv_ref[win, :] += lax.dot_general(p.astype(BF), doc, TN, preferred_element_type=F32)

    qspec = pl.BlockSpec((qb, HEAD), lambda hh, i: (i, hh))
    kspec = pl.BlockSpec((sp, HEAD), lambda hh, i: (0, hh))
    bspec = pl.BlockSpec((1, CHUNK, BAND), lambda hh, i: (hh, 0, 0))
    return pl.pallas_call(
        body, name=name, grid=(h, s // qb), in_specs=[qspec, kspec, kspec, bspec, qspec],
        out_specs=[qspec, kspec, kspec, bspec],
        out_shape=[SDS((s, d), F32), SDS((sp, d), F32), SDS((sp, d), F32), SDS((h, CHUNK, BAND), F32)],
        compiler_params=_params("parallel", "arbitrary"))(q, kpad, vpad, bias, do)


def _pool_tile(x_ext, gain, w4, scale, row0):
    n, d = x_ext.shape
    dg = d // len(POOL_WINDOWS)
    pos = lax.broadcasted_iota(jnp.int32, (n, 1), 0) + row0
    hn = _rms(x_ext, gain) * jnp.where(pos >= 0, 1.0, 0.0)
    outs = []
    for gi, w in enumerate(POOL_WINDOWS):
        hg = hn[:, gi * dg:(gi + 1) * dg]
        acc, k = hg, 1
        while k < w:
            acc = acc + _shift(acc, k)
            k *= 2
        inv = 1.0 / jnp.clip(pos + 1, 1, w).astype(F32)
        pooled = acc * inv - hg
        outs.append(jnp.dot(pooled.astype(BF), w4[gi].astype(BF), preferred_element_type=F32))
    return jnp.concatenate(outs, axis=1) * scale


POOL_ROWS = 128


def _pool_fwd(x, gain, w4, scale, name):
    s, d = x.shape
    r = _tile(s, POOL_ROWS, HALO)
    hidx = _halo_index(r)

    def body(xc, xh, g_ref, w_ref, s_ref, o_ref):
        rb = pl.program_id(0)
        x_ext = jnp.concatenate([xh[...], xc[...]], axis=0)
        y = _pool_tile(x_ext, g_ref[...], [w_ref[gi] for gi in range(len(POOL_WINDOWS))], s_ref[...], rb * r - HALO)
        o_ref[...] = xc[...] + y[HALO:]

    cur = pl.BlockSpec((r, d), lambda rb: (rb, 0))
    vec = pl.BlockSpec((1, d), lambda rb: (0, 0))
    return pl.pallas_call(
        body, name=name, grid=(s // r,),
        in_specs=[cur, pl.BlockSpec((HALO, d), lambda rb: (hidx(rb), 0)), vec,
                  pl.BlockSpec(w4.shape, lambda rb: (0, 0, 0)), vec],
        out_specs=cur, out_shape=SDS((s, d), F32), compiler_params=_params("parallel"))(x, x, gain, w4, scale)


def _pool_bwd(x, gain, w4, scale, dy, name):
    s, d = x.shape
    r = _tile(s, POOL_ROWS, HALO)
    nb = s // r
    hidx = _halo_index(r)

    def body(xc, xh, g_ref, w_ref, s_ref, dy_ref, dx_ref, dg_ref, dw_ref, ds_ref, carry):
        step = pl.program_id(0)
        rb = nb - 1 - step
        x_ext = jnp.concatenate([xh[...], xc[...]], axis=0)
        fn = functools.partial(_pool_tile, row0=rb * r - HALO)
        _, vjp = jax.vjp(fn, x_ext, g_ref[...], [w_ref[gi] for gi in range(len(POOL_WINDOWS))], s_ref[...])
        ct = jnp.concatenate([jnp.zeros((HALO, d), F32), dy_ref[...]], axis=0)
        dx_ext, dg, dws, dsc = vjp(ct)

        @pl.when(step == 0)
        def _():
            carry[...] = jnp.zeros_like(carry)
            dg_ref[...] = jnp.zeros_like(dg_ref)
            dw_ref[...] = jnp.zeros_like(dw_ref)
            ds_ref[...] = jnp.zeros_like(ds_ref)

        dx_ref[...] = dy_ref[...] + dx_ext[HALO:]
        dx_ref[pl.ds(r - HALO, HALO), :] += carry[...]
        carry[...] = dx_ext[:HALO]
        dg_ref[...] += dg
        for gi, dw in enumerate(dws):
            dw_ref[gi] += dw
        ds_ref[...] += dsc

    cur = pl.BlockSpec((r, d), lambda t: (nb - 1 - t, 0))
    vec = pl.BlockSpec((1, d), lambda t: (0, 0))
    wspec = pl.BlockSpec(w4.shape, lambda t: (0, 0, 0))
    return pl.pallas_call(
        body, name=name, grid=(nb,),
        in_specs=[cur, pl.BlockSpec((HALO, d), lambda t: (hidx(nb - 1 - t), 0)), vec, wspec, vec, cur],
        out_specs=[cur, vec, wspec, vec],
        out_shape=[SDS((s, d), F32), SDS((1, d), F32), SDS(w4.shape, F32), SDS((1, d), F32)],
        scratch_shapes=[pltpu.VMEM((HALO, d), F32)], compiler_params=_params("arbitrary"))(x, x, gain, w4, scale, dy)


def _gdn_conv_tile(u_ext, cw, kind):
    acc = (u_ext * cw[3:4] + _shift(u_ext, 1) * cw[2:3] + _shift(u_ext, 2) * cw[1:2] + _shift(u_ext, 3) * cw[0:1])
    y = _silu(acc)
    if kind != "v":
        y = y * lax.rsqrt(jnp.sum(y * y, axis=-1, keepdims=True) + EPS)
    if kind == "q":
        y = y * (HEAD ** -0.5)
    return y


def _gdn_conv_fwd(proj, cw, kind, head0, nheads, name):
    s = proj.shape[0]
    r = _tile(s, 512, HALO)
    hidx = _halo_index(r)

    def body(uc, uh, cw_ref, o_ref):
        rb = pl.program_id(1)
        keep = jnp.where(rb == 0, 0.0, 1.0)
        u_ext = jnp.concatenate([uh[...].astype(F32) * keep, uc[...].astype(F32)], axis=0)
        o_ref[...] = _gdn_conv_tile(u_ext, cw_ref[...], kind)[HALO:].astype(BF)

    return pl.pallas_call(
        body, name=name, grid=(nheads, s // r),
        in_specs=[pl.BlockSpec((r, HEAD), lambda j, rb: (rb, head0 + j)),
                  pl.BlockSpec((HALO, HEAD), lambda j, rb: (hidx(rb), head0 + j)),
                  pl.BlockSpec((8, HEAD), lambda j, rb: (0, head0 + j))],
        out_specs=pl.BlockSpec((r, HEAD), lambda j, rb: (rb, j)), out_shape=SDS((s, nheads * HEAD), BF),
        compiler_params=_params("parallel", "parallel"))(proj, proj, cw)


def _gdn_conv_bwd(proj, cw, dy, kind, head0, nheads, name):
    s = proj.shape[0]
    r = _tile(s, 512, HALO)
    nb = s // r
    hidx = _halo_index(r)
    rep = dy.shape[1] // (nheads * HEAD)

    def body(*refs):
        uc, uh, cw_ref = refs[:3]
        dys = refs[3:3 + rep]
        du_ref, dcw_ref, carry = refs[3 + rep:]
        step = pl.program_id(1)
        rb = nb - 1 - step
        keep = jnp.where(rb == 0, 0.0, 1.0)
        zero = jnp.zeros((HALO, HEAD), F32)
        u_ext = jnp.concatenate([uh[...].astype(F32) * keep, uc[...].astype(F32)], axis=0)
        dyc = dys[0][...]
        for extra in dys[1:]:
            dyc = dyc + extra[...]
        _, vjp = jax.vjp(functools.partial(_gdn_conv_tile, kind=kind), u_ext, cw_ref[...])
        du_ext, dcw = vjp(jnp.concatenate([zero, dyc], axis=0))

        @pl.when(step == 0)
        def _():
            carry[...] = zero
            dcw_ref[...] = jnp.zeros_like(dcw_ref)

        du_ref[pl.ds(0, r - HALO), :] = du_ext[HALO:r].astype(BF)
        du_ref[pl.ds(r - HALO, HALO), :] = (du_ext[r:] + carry[...]).astype(BF)
        carry[...] = du_ext[:HALO]
        dcw_ref[...] += dcw

    dy_specs = [pl.BlockSpec((r, HEAD), functools.partial(lambda j, t, e: (nb - 1 - t, rep * j + e), e=e)) for e in range(rep)]
    return pl.pallas_call(
        body, name=name, grid=(nheads, nb),
        in_specs=[pl.BlockSpec((r, HEAD), lambda j, t: (nb - 1 - t, head0 + j)),
                  pl.BlockSpec((HALO, HEAD), lambda j, t: (hidx(nb - 1 - t), head0 + j)),
                  pl.BlockSpec((8, HEAD), lambda j, t: (0, head0 + j))] + dy_specs,
        out_specs=[pl.BlockSpec((r, HEAD), lambda j, t: (nb - 1 - t, j)), pl.BlockSpec((8, HEAD), lambda j, t: (0, j))],
        out_shape=[SDS((s, nheads * HEAD), BF), SDS((8, nheads * HEAD), F32)],
        scratch_shapes=[pltpu.VMEM((HALO, HEAD), F32)],
        compiler_params=_params("parallel", "arbitrary"))(proj, proj, cw, *([dy] * rep))


GATE_ROWS = 256


def _gates_tile(a, bt, a_log, dt_bias, hv):
    r = a.shape[0]
    z = a + dt_bias
    softplus = jnp.maximum(z, 0.0) + jnp.log(1.0 + jnp.exp(-jnp.abs(z)))
    g = -jnp.exp(a_log) * softplus
    ri = lax.broadcasted_iota(jnp.int32, (r, r), 0)
    ci = lax.broadcasted_iota(jnp.int32, (r, r), 1)
    same_chunk = jnp.right_shift(ri, 6) == jnp.right_shift(ci, 6)
    tri = jnp.where(same_chunk, jnp.where(ri >= ci, 1.0, 0.0), 0.0).astype(F32)
    gc = jnp.dot(tri, g, precision=HI, preferred_element_type=F32)
    beta = 1.0 / (1.0 + jnp.exp(-bt))
    er = lax.broadcasted_iota(jnp.int32, (LANE, hv * HEAD), 0)
    ec = lax.broadcasted_iota(jnp.int32, (LANE, hv * HEAD), 1)
    expand = jnp.where(er == jnp.right_shift(ec, 7), 1.0, 0.0).astype(F32)
    return (jnp.dot(gc, expand, precision=HI, preferred_element_type=F32),
            jnp.dot(beta, expand, precision=HI, preferred_element_type=F32))


def _gates_fwd(ab, a_log, dt_bias, hv, name):
    s = ab.shape[0]
    r = _tile(s, GATE_ROWS, CHUNK)

    def body(a_ref, b_ref, al_ref, dt_ref, gc_ref, bb_ref):
        gcb, btb = _gates_tile(a_ref[...], b_ref[...], al_ref[...], dt_ref[...], hv)
        gc_ref[...] = gcb
        bb_ref[...] = btb

    vec = pl.BlockSpec((1, LANE), lambda i: (0, 0))
    wide = pl.BlockSpec((r, hv * HEAD), lambda i: (i, 0))
    return pl.pallas_call(
        body, name=name, grid=(s // r,),
        in_specs=[pl.BlockSpec((r, LANE), lambda i: (i, 0)), pl.BlockSpec((r, LANE), lambda i: (i, 1)), vec, vec],
        out_specs=[wide, wide], out_shape=[SDS((s, hv * HEAD), F32)] * 2,
        compiler_params=_params("parallel"))(ab, ab, a_log, dt_bias)


def _gates_bwd(ab, a_log, dt_bias, dgcb, dbtb, hv, name):
    s = ab.shape[0]
    r = _tile(s, GATE_ROWS, CHUNK)

    def body(a_ref, b_ref, al_ref, dt_ref, dgc_ref, dbb_ref, dab_ref, dal_ref, ddt_ref):
        i = pl.program_id(0)
        _, vjp = jax.vjp(functools.partial(_gates_tile, hv=hv), a_ref[...], b_ref[...], al_ref[...], dt_ref[...])
        da, dbt, dal, ddt = vjp((dgc_ref[...], dbb_ref[...]))
        dab_ref[:, 0:LANE] = da
        dab_ref[:, LANE:] = dbt

        @pl.when(i == 0)
        def _():
            dal_ref[...] = dal
            ddt_ref[...] = ddt

        @pl.when(i > 0)
        def _():
            dal_ref[...] += dal
            ddt_ref[...] += ddt

    vec = pl.BlockSpec((1, LANE), lambda i: (0, 0))
    wide = pl.BlockSpec((r, hv * HEAD), lambda i: (i, 0))
    return pl.pallas_call(
        body, name=name, grid=(s // r,),
        in_specs=[pl.BlockSpec((r, LANE), lambda i: (i, 0)), pl.BlockSpec((r, LANE), lambda i: (i, 1)), vec, vec, wide, wide],
        out_specs=[pl.BlockSpec((r, 2 * LANE), lambda i: (i, 0)), vec, vec],
        out_shape=[SDS((s, 2 * LANE), F32), SDS((1, LANE), F32), SDS((1, LANE), F32)],
        compiler_params=_params("arbitrary"))(ab, ab, a_log, dt_bias, dgcb, dbtb)


def _delta_chunk(q, k, v, gcb, btb, s_in):
    c = CHUNK
    ri = lax.broadcasted_iota(jnp.int32, (c, c), 0)
    ci = lax.broadcasted_iota(jnp.int32, (c, c), 1)
    causal = ri >= ci
    gcol = gcb[:, :c]
    grow = jnp.concatenate([gcb, gcb], axis=0).T[:c, :c]
    decay = jnp.where(causal, jnp.exp(jnp.where(causal, gcol - grow, 0.0)), 0.0)
    kb = k * btb
    vb = v * btb
    kbf = k.astype(BF)
    a = jnp.where(ri > ci, lax.dot_general(kb.astype(BF), kbf, NT, preferred_element_type=F32) * decay, 0.0)
    p = -a
    t = jnp.where(ri == ci, 1.0, 0.0) + p
    for _ in range(5):
        p = jnp.dot(p, p, precision=HI, preferred_element_type=F32)
        t = t + jnp.dot(t, p, precision=HI, preferred_element_type=F32)
    eg = jnp.exp(gcb)
    u = jnp.dot(t, vb, precision=HI, preferred_element_type=F32)
    w = jnp.dot(t, kb * eg, precision=HI, preferred_element_type=F32)
    attn = lax.dot_general(q.astype(BF), kbf, NT, preferred_element_type=F32) * decay
    glast = gcb[c - 1:c, :]
    ks = k * jnp.exp(glast - gcb)
    sb = s_in.astype(BF)
    v_new = u - jnp.dot(w.astype(BF), sb, preferred_element_type=F32)
    o = (jnp.dot((q * eg).astype(BF), sb, preferred_element_type=F32)
         + jnp.dot(attn.astype(BF), v_new.astype(BF), preferred_element_type=F32))
    s_out = s_in * jnp.exp(glast[:, 0:1]) + lax.dot_general(ks.astype(BF), v_new.astype(BF), TN, preferred_element_type=F32)
    return o, s_out


GDN_ROWS = 512


def _delta_fwd(q, k, v, gcb, btb, name):
    s, dv = v.shape
    hv = dv // HEAD
    rep = dv // q.shape[1]
    r = _tile(s, GDN_ROWS, CHUNK)
    per = r // CHUNK
    nc = s // CHUNK

    def body(q_ref, k_ref, v_ref, g_ref, b_ref, o_ref, st_ref, state):
        @pl.when(pl.program_id(1) == 0)
        def _():
            state[...] = jnp.zeros_like(state)

        def chunk(cc, carry):
            rows = pl.ds(pl.multiple_of(cc * CHUNK, CHUNK), CHUNK)
            st_ref[0, cc] = state[...]
            o, s_out = _delta_chunk(q_ref[rows, :].astype(F32), k_ref[rows, :].astype(F32), v_ref[rows, :].astype(F32),
                                    g_ref[rows, :], b_ref[rows, :], state[...])
            o_ref[rows, :] = o.astype(BF)
            state[...] = s_out
            return carry

        lax.fori_loop(0, per, chunk, 0)

    kq = pl.BlockSpec((r, HEAD), lambda h, i: (i, h // rep))
    vs = pl.BlockSpec((r, HEAD), lambda h, i: (i, h))
    return pl.pallas_call(
        body, name=name, grid=(hv, s // r), in_specs=[kq, kq, vs, vs, vs],
        out_specs=[vs, pl.BlockSpec((1, per, HEAD, HEAD), lambda h, i: (h, i, 0, 0))],
        out_shape=[SDS((s, dv), BF), SDS((hv, nc, HEAD, HEAD), F32)],
        scratch_shapes=[pltpu.VMEM((HEAD, HEAD), F32)],
        compiler_params=_params("parallel", "arbitrary"))(q, k, v, gcb, btb)


def _delta_bwd(q, k, v, gcb, btb, states, do, name):
    s, dv = v.shape
    hv = dv // HEAD
    rep = dv // q.shape[1]
    r = _tile(s, GDN_ROWS, CHUNK)
    per = r // CHUNK
    nb = s // r

    def body(q_ref, k_ref, v_ref, g_ref, b_ref, st_ref, do_ref, dq_ref, dk_ref, dv_ref, dg_ref, db_ref, dstate):
        @pl.when(pl.program_id(1) == 0)
        def _():
            dstate[...] = jnp.zeros_like(dstate)

        def chunk(t, carry):
            cc = per - 1 - t
            rows = pl.ds(pl.multiple_of(cc * CHUNK, CHUNK), CHUNK)
            _, vjp = jax.vjp(_delta_chunk, q_ref[rows, :].astype(F32), k_ref[rows, :].astype(F32),
                             v_ref[rows, :].astype(F32), g_ref[rows, :], b_ref[rows, :], st_ref[0, cc])
            dq, dk, dvv, dg, db, ds_in = vjp((do_ref[rows, :].astype(F32), dstate[...]))
            dq_ref[rows, :] = dq
            dk_ref[rows, :] = dk
            dv_ref[rows, :] = dvv
            dg_ref[rows, :] = dg
            db_ref[rows, :] = db
            dstate[...] = ds_in
            return carry

        lax.fori_loop(0, per, chunk, 0)

    kq = pl.BlockSpec((r, HEAD), lambda h, i: (nb - 1 - i, h // rep))
    vs = pl.BlockSpec((r, HEAD), lambda h, i: (nb - 1 - i, h))
    return pl.pallas_call(
        body, name=name, grid=(hv, nb),
        in_specs=[kq, kq, vs, vs, vs, pl.BlockSpec((1, per, HEAD, HEAD), lambda h, i: (h, nb - 1 - i, 0, 0)), vs],
        out_specs=[vs] * 5, out_shape=[SDS((s, dv), F32)] * 5,
        scratch_shapes=[pltpu.VMEM((HEAD, HEAD), F32)],
        compiler_params=_params("parallel", "arbitrary"))(q, k, v, gcb, btb, states, do)


def _gdn_out_tile(o, gate, gain):
    return _headnorm(o, gain) * _silu(gate)


def _gdn_out_fwd(o, proj, gate_col0, gain, name):
    s, dv = o.shape
    r = _tile(s, 128, 16)

    def body(o_ref, g_ref, gain_ref, y_ref):
        y_ref[...] = _gdn_out_tile(o_ref[...].astype(F32), g_ref[...].astype(F32), gain_ref[...]).astype(BF)

    row = pl.BlockSpec((r, dv), lambda i: (i, 0))
    return pl.pallas_call(
        body, name=name, grid=(s // r,),
        in_specs=[row, pl.BlockSpec((r, dv), lambda i: (i, gate_col0)), pl.BlockSpec((1, HEAD), lambda i: (0, 0))],
        out_specs=row, out_shape=SDS((s, dv), BF), compiler_params=_params("parallel"))(o, proj, gain)


def _gdn_out_bwd(o, proj, gate_col0, gain, dy, name):
    s, dv = o.shape
    r = _tile(s, 128, 16)

    def body(o_ref, g_ref, gain_ref, dy_ref, do_ref, dg_ref, dgain_ref):
        i = pl.program_id(0)
        _, vjp = jax.vjp(_gdn_out_tile, o_ref[...].astype(F32), g_ref[...].astype(F32), gain_ref[...])
        do, dg, dgain = vjp(dy_ref[...].astype(F32))
        do_ref[...] = do
        dg_ref[...] = dg.astype(BF)

        @pl.when(i == 0)
        def _():
            dgain_ref[...] = dgain

        @pl.when(i > 0)
        def _():
            dgain_ref[...] += dgain

    row = pl.BlockSpec((r, dv), lambda i: (i, 0))
    vec = pl.BlockSpec((1, HEAD), lambda i: (0, 0))
    return pl.pallas_call(
        body, name=name, grid=(s // r,),
        in_specs=[row, pl.BlockSpec((r, dv), lambda i: (i, gate_col0)), vec, row],
        out_specs=[row, row, vec], out_shape=[SDS((s, dv), F32), SDS((s, dv), BF), SDS((1, HEAD), F32)],
        compiler_params=_params("arbitrary"))(o, proj, gain, dy)


def _ffn_forward(x, gain, wu, wg, cw, wd, tag):
    h = _rmsnorm_fwd(x, gain, f"{tag}_norm")
    uu = _mm(h, wu, name=f"{tag}_up_u")
    ug = _mm(h, wg, name=f"{tag}_up_g")
    a = _ffn_act_fwd(uu, ug, cw, f"{tag}_act")
    y = _mm(a, wd, res=x, out_dtype=F32, name=f"{tag}_down")
    return y, (x, h, uu, ug, a)


def _ffn_backward(saved, dy, gain, wu, wg, cw, wd, tag):
    x, h, uu, ug, a = saved
    da = _mm(dy, wd, tb=True, name=f"{tag}_d_act")
    dwd = _mm(a, dy, ta=True, out_dtype=F32, name=f"{tag}_d_wd")
    duu, dug, dcw = _ffn_act_bwd(uu, ug, cw, da, f"{tag}_act_bwd")
    dh = _mm(duu, wu, tb=True, out_dtype=F32, name=f"{tag}_d_h_u")
    dh = _mm(dug, wg, tb=True, res=dh, out_dtype=F32, name=f"{tag}_d_h_g")
    dwu = _mm(h, duu, ta=True, out_dtype=F32, name=f"{tag}_d_wu")
    dwg = _mm(h, dug, ta=True, out_dtype=F32, name=f"{tag}_d_wg")
    dx, dgain = _rmsnorm_bwd(x, gain, dh, dy, f"{tag}_norm_bwd")
    return dx, dict(gain=dgain, wu=dwu, wg=dwg, cw=dcw, wd=dwd)


def _att_forward(x, gain, p, tag):
    h = _rmsnorm_fwd(x, gain, f"{tag}_norm")
    qkv = _mm(h, p["wqkv"], name=f"{tag}_qkv")
    q, kpad, vpad = _qkv_post_fwd(qkv, p["qg"], p["kg"], f"{tag}_qknorm")
    bias = _bias_expand(p["rel"], f"{tag}_bias")
    o = _attn_fwd(q, kpad, vpad, bias, f"{tag}_core")
    y = _mm(o, p["wo"], res=x, out_dtype=F32, name=f"{tag}_out")
    return y, (x, h, qkv, q, kpad, vpad, bias, o)


def _att_backward(saved, dy, gain, p, tag):
    x, h, qkv, q, kpad, vpad, bias, o = saved
    do = _mm(dy, p["wo"], tb=True, name=f"{tag}_d_o")
    dwo = _mm(o, dy, ta=True, out_dtype=F32, name=f"{tag}_d_wo")
    dq, dkpad, dvpad, dbias = _attn_bwd(q, kpad, vpad, bias, do, f"{tag}_core_bwd")
    drel = _bias_reduce(dbias, f"{tag}_bias_bwd")
    dqkv, dqg, dkg = _qkv_post_bwd(qkv, p["qg"], p["kg"], dq, dkpad, dvpad, f"{tag}_qknorm_bwd")
    dh = _mm(dqkv, p["wqkv"], tb=True, out_dtype=F32, name=f"{tag}_d_h")
    dwqkv = _mm(h, dqkv, ta=True, out_dtype=F32, name=f"{tag}_d_wqkv")
    dx, dgain = _rmsnorm_bwd(x, gain, dh, dy, f"{tag}_norm_bwd")
    return dx, dict(gain=dgain, wqkv=dwqkv, qg=dqg, kg=dkg, rel=drel, wo=dwo)


def _gdn_forward(x, gain, p, tag):
    d = x.shape[1]
    nk = d // HEAD
    hv = 2 * nk
    h = _rmsnorm_fwd(x, gain, f"{tag}_norm")
    proj = _mm(h, p["wmain"], name=f"{tag}_proj")
    ab = _mm(h, p["wab"], out_dtype=F32, name=f"{tag}_proj_ab")
    q = _gdn_conv_fwd(proj, p["cw"], "q", 0, nk, f"{tag}_conv_q")
    k = _gdn_conv_fwd(proj, p["cw"], "k", nk, nk, f"{tag}_conv_k")
    v = _gdn_conv_fwd(proj, p["cw"], "v", 2 * nk, hv, f"{tag}_conv_v")
    gcb, btb = _gates_fwd(ab, p["a_log"], p["dt_bias"], hv, f"{tag}_gates")
    o, states = _delta_fwd(q, k, v, gcb, btb, f"{tag}_delta")
    og = _gdn_out_fwd(o, proj, 2, p["o_gain"], f"{tag}_onorm")
    y = _mm(og, p["wo"], res=x, out_dtype=F32, name=f"{tag}_out")
    return y, (x, h, proj, ab, q, k, v, gcb, btb, o, states, og)


def _gdn_backward(saved, dy, gain, p, tag):
    x, h, proj, ab, q, k, v, gcb, btb, o, states, og = saved
    d = x.shape[1]
    nk = d // HEAD
    hv = 2 * nk
    dog = _mm(dy, p["wo"], tb=True, name=f"{tag}_d_og")
    dwo = _mm(og, dy, ta=True, out_dtype=F32, name=f"{tag}_d_wo")
    do, dgate, dogain = _gdn_out_bwd(o, proj, 2, p["o_gain"], dog, f"{tag}_onorm_bwd")
    dq, dk, dv, dgcb, dbtb = _delta_bwd(q, k, v, gcb, btb, states, do, f"{tag}_delta_bwd")
    dab, dalog, ddt = _gates_bwd(ab, p["a_log"], p["dt_bias"], dgcb, dbtb, hv, f"{tag}_gates_bwd")
    dpq, dcq = _gdn_conv_bwd(proj, p["cw"], dq, "q", 0, nk, f"{tag}_conv_q_bwd")
    dpk, dck = _gdn_conv_bwd(proj, p["cw"], dk, "k", nk, nk, f"{tag}_conv_k_bwd")
    dpv, dcv = _gdn_conv_bwd(proj, p["cw"], dv, "v", 2 * nk, hv, f"{tag}_conv_v_bwd")
    dproj = jnp.concatenate([dpq, dpk, dpv, dgate], axis=1)
    dcw = jnp.concatenate([dcq, dck, dcv], axis=1)
    dh = _mm(dproj, p["wmain"], tb=True, out_dtype=F32, name=f"{tag}_d_h_main")
    dh = _mm(dab, p["wab"], tb=True, res=dh, out_dtype=F32, name=f"{tag}_d_h_ab")
    dwmain = _mm(h, dproj, ta=True, out_dtype=F32, name=f"{tag}_d_wmain")
    dwab = _mm(h, dab, ta=True, out_dtype=F32, name=f"{tag}_d_wab")
    dx, dgain = _rmsnorm_bwd(x, gain, dh, dy, f"{tag}_norm_bwd")
    return dx, dict(gain=dgain, wmain=dwmain, wab=dwab, cw=dcw, a_log=dalog, dt_bias=ddt, o_gain=dogain, wo=dwo)


def _local_step(x, target, w):
    depth = len(w["ffn"])
    tape = []
    for i in range(depth):
        kind, j = i % 3, i // 3
        gain = w["mix_norm"][i:i + 1]
        if kind == 0:
            x, saved = _att_forward(x, gain, w["att"][j], f"l{i}_att")
        elif kind == 1:
            x_in = x
            x = _pool_fwd(x_in, gain, w["pool"][j]["w"], w["pool"][j]["scale"], f"l{i}_pool")
            saved = x_in
        else:
            x, saved = _gdn_forward(x, gain, w["gdn"][j], f"l{i}_gdn")
        f = w["ffn"][i]
        x, fsaved = _ffn_forward(x, w["ffn_norm"][i:i + 1], f["wu"], f["wg"], f["cw"], f["wd"], f"l{i}_ffn")
        tape.append((saved, fsaved))
    dy, loss_row = _loss_head(x, target, "loss_head")
    grads = dict(mix=[None] * depth, ffn=[None] * depth)
    for i in reversed(range(depth)):
        kind, j = i % 3, i // 3
        saved, fsaved = tape[i]
        f = w["ffn"][i]
        dy, grads["ffn"][i] = _ffn_backward(fsaved, dy, w["ffn_norm"][i:i + 1], f["wu"], f["wg"], f["cw"], f["wd"], f"l{i}_ffn")
        gain = w["mix_norm"][i:i + 1]
        if kind == 0:
            dy, grads["mix"][i] = _att_backward(saved, dy, gain, w["att"][j], f"l{i}_att")
        elif kind == 1:
            pw = w["pool"][j]
            dy, dgain, dw4, dscale = _pool_bwd(saved, gain, pw["w"], pw["scale"], dy, f"l{i}_pool_bwd")
            grads["mix"][i] = dict(gain=dgain, w=dw4, scale=dscale)
        else:
            dy, grads["mix"][i] = _gdn_backward(saved, dy, gain, w["gdn"][j], f"l{i}_gdn")
    return loss_row, dy, grads


MESH = pl.DeviceIdType.MESH
ANY = pl.BlockSpec(memory_space=pl.ANY)


def _position():
    return tuple(lax.axis_index(a) for a in AXES)


def _flip(pos, rel):
    return tuple(1 - p if (rel >> (2 - i)) & 1 else p for i, p in enumerate(pos))


def _index(pos):
    return 4 * pos[0] + 2 * pos[1] + pos[2]


def _all_gather(arr, name):
    def body(x_ref, o_ref, send, recv, local):
        me = _position()
        mine = pltpu.make_async_copy(x_ref, o_ref.at[_index(me)], local)
        mine.start()
        copies = []
        for rel in range(1, N_DEV):
            cp = pltpu.make_async_remote_copy(
                src_ref=x_ref, dst_ref=o_ref.at[_index(me)], send_sem=send.at[rel - 1], recv_sem=recv.at[rel - 1],
                device_id=_flip(me, rel), device_id_type=MESH)
            cp.start()
            copies.append(cp)
        for cp in copies:
            cp.wait()
        mine.wait()

    return pl.pallas_call(
        body, name=name, in_specs=[ANY], out_specs=ANY, out_shape=SDS((N_DEV,) + arr.shape, arr.dtype),
        scratch_shapes=[pltpu.SemaphoreType.DMA((N_DEV - 1,)), pltpu.SemaphoreType.DMA((N_DEV - 1,)),
                        pltpu.SemaphoreType.DMA(())])(arr)


def _exchange(arr, name):
    def body(x_ref, o_ref, send, recv, local):
        me = _position()
        mine = pltpu.make_async_copy(x_ref.at[_index(me)], o_ref.at[_index(me)], local)
        mine.start()
        copies = []
        for rel in range(1, N_DEV):
            peer = _flip(me, rel)
            cp = pltpu.make_async_remote_copy(
                src_ref=x_ref.at[_index(peer)], dst_ref=o_ref.at[_index(me)], send_sem=send.at[rel - 1],
                recv_sem=recv.at[rel - 1], device_id=peer, device_id_type=MESH)
            cp.start()
            copies.append(cp)
        for cp in copies:
            cp.wait()
        mine.wait()

    return pl.pallas_call(
        body, name=name, in_specs=[ANY], out_specs=ANY, out_shape=SDS(arr.shape, arr.dtype),
        scratch_shapes=[pltpu.SemaphoreType.DMA((N_DEV - 1,)), pltpu.SemaphoreType.DMA((N_DEV - 1,)),
                        pltpu.SemaphoreType.DMA(())])(arr)


def _adamw(parts, w, m, v, name):
    r, c = w.shape
    tr = _tile(r, 128, 16)
    c1 = 1.0 / (1.0 - ADAM_B1 ** ADAM_STEP)
    c2 = 1.0 / (1.0 - ADAM_B2 ** ADAM_STEP)

    def body(p_ref, w_ref, m_ref, v_ref, g_ref, d_ref, nm_ref, nv_ref):
        g = p_ref[0].astype(F32)
        for s in range(1, N_DEV):
            g = g + p_ref[s].astype(F32)
        nm = ADAM_B1 * m_ref[...] + (1.0 - ADAM_B1) * g
        nv = ADAM_B2 * v_ref[...] + (1.0 - ADAM_B2) * (g * g)
        g_ref[...] = g
        nm_ref[...] = nm
        nv_ref[...] = nv
        d_ref[...] = -ADAM_LR * ((nm * c1) / (jnp.sqrt(nv * c2) + ADAM_EPS) + ADAM_WD * w_ref[...])

    row = pl.BlockSpec((tr, c), lambda i: (i, 0))
    return pl.pallas_call(
        body, name=name, grid=(r // tr,), in_specs=[pl.BlockSpec((N_DEV, tr, c), lambda i: (0, i, 0)), row, row, row],
        out_specs=[row] * 4, out_shape=[SDS((r, c), F32)] * 4, compiler_params=_params("parallel"))(parts, w, m, v)


PACK = 8 * LANE


def _pack(arrs):
    flat = []
    for a in arrs:
        a = a.reshape(-1).astype(F32)
        flat.append(jnp.pad(a, (0, (-a.shape[0]) % PACK)))
    return jnp.concatenate(flat).reshape(-1, LANE)


def _unpack(packed, shapes, lead=()):
    flat = packed.reshape(lead + (-1,))
    out, off = [], 0
    for shp in shapes:
        n = int(np.prod(shp))
        out.append(flat[..., off:off + n].reshape(lead + tuple(shp)))
        off += n + (-n) % PACK
    return out


def _pad_to(a, axis, size):
    pad = [(0, 0)] * a.ndim
    pad[axis] = (0, size - a.shape[axis])
    return jnp.pad(a, pad)


def _cols_from_shards(g):
    return jnp.transpose(g, (1, 0, 2)).reshape(g.shape[1], -1)


def _cols_to_shards(a):
    c = a.shape[-1] // N_DEV
    a = a.reshape(a.shape[:-1] + (N_DEV, c))
    return jnp.moveaxis(a, -2, 0)


def _rows_to_shards(a):
    r = a.shape[-2] // N_DEV
    a = a.reshape(a.shape[:-2] + (N_DEV, r, a.shape[-1]))
    return jnp.moveaxis(a, -3, 0)


WEIGHTS = ("mix_norm", "ffn_norm", "att_w_qkv", "att_q_gain", "att_k_gain", "att_rel_bias", "att_w_o", "pool_w",
           "pool_scale", "gdn_w_in", "gdn_conv", "gdn_a_log", "gdn_dt_bias", "gdn_o_gain", "gdn_w_o", "ffn_w_up",
           "ffn_conv", "ffn_w_down")
REPLICATED = ("mix_norm", "ffn_norm", "att_q_gain", "att_k_gain", "pool_scale", "gdn_a_log", "gdn_dt_bias", "gdn_o_gain")
SMALL_SHARDED = ("att_rel_bias", "gdn_conv", "ffn_conv")
BIG = ("att_w_qkv", "att_w_o", "pool_w", "gdn_w_in", "gdn_w_o", "ffn_w_up", "ffn_w_down")
KEEP_F32 = ("pool_w",)


def _assemble_weights(w, gathered, small):
    d = w["mix_norm"].shape[1]
    nk = d // HEAD
    hv = 2 * nk
    f = gathered["ffn_w_down"].shape[2] * N_DEV
    fp = -(-f // FF_PAD) * FF_PAD
    out = dict(mix_norm=w["mix_norm"], ffn_norm=w["ffn_norm"], att=[], pool=[], gdn=[], ffn=[])
    for j in range(w["att_w_qkv"].shape[0]):
        out["att"].append(dict(
            wqkv=_cols_from_shards(gathered["att_w_qkv"][:, j]),
            wo=gathered["att_w_o"][:, j].reshape(d, d),
            qg=w["att_q_gain"][j:j + 1], kg=w["att_k_gain"][j:j + 1], rel=small["att_rel_bias"][j]))
    for j in range(w["pool_w"].shape[0]):
        g = gathered["pool_w"][:, j]
        out["pool"].append(dict(w=jnp.transpose(g, (1, 0, 2, 3)).reshape(g.shape[1], g.shape[3], g.shape[3]),
                                scale=w["pool_scale"][j:j + 1]))
    for j in range(w["gdn_w_in"].shape[0]):
        win = _cols_from_shards(gathered["gdn_w_in"][:, j])
        nm = 6 * d
        wab = jnp.concatenate([_pad_to(win[:, nm:nm + hv], 1, LANE), _pad_to(win[:, nm + hv:], 1, LANE)], axis=1)
        out["gdn"].append(dict(
            wmain=win[:, :nm], wab=wab, cw=_pad_to(small["gdn_conv"][j], 0, 8),
            a_log=_pad_to(w["gdn_a_log"][j:j + 1], 1, LANE), dt_bias=_pad_to(w["gdn_dt_bias"][j:j + 1], 1, LANE),
            o_gain=w["gdn_o_gain"][j:j + 1], wo=gathered["gdn_w_o"][:, j].reshape(2 * d, d)))
    for i in range(w["ffn_w_up"].shape[0]):
        wup = _cols_from_shards(gathered["ffn_w_up"][:, i])
        out["ffn"].append(dict(
            wu=_pad_to(wup[:, :f], 1, fp), wg=_pad_to(wup[:, f:], 1, fp),
            cw=_pad_to(_pad_to(small["ffn_conv"][i], 0, 8), 1, fp),
            wd=_pad_to(gathered["ffn_w_down"][:, i].reshape(f, d), 0, fp)))
    return out, f


def _full_gradients(grads, w, f):
    d = w["mix_norm"].shape[1]
    nk = d // HEAD
    hv = 2 * nk
    depth = len(grads["ffn"])
    att = [grads["mix"][i] for i in range(depth) if i % 3 == 0]
    pool = [grads["mix"][i] for i in range(depth) if i % 3 == 1]
    gdn = [grads["mix"][i] for i in range(depth) if i % 3 == 2]
    ffn = grads["ffn"]
    win = [jnp.concatenate([g["wmain"], g["wab"][:, :hv], g["wab"][:, LANE:LANE + hv]], axis=1) for g in gdn]
    return dict(
        mix_norm=jnp.concatenate([g["gain"] for g in grads["mix"]], axis=0),
        ffn_norm=jnp.concatenate([g["gain"] for g in ffn], axis=0),
        att_w_qkv=jnp.stack([g["wqkv"] for g in att]),
        att_q_gain=jnp.concatenate([g["qg"] for g in att], axis=0),
        att_k_gain=jnp.concatenate([g["kg"] for g in att], axis=0),
        att_rel_bias=jnp.stack([g["rel"] for g in att]),
        att_w_o=jnp.stack([g["wo"] for g in att]),
        pool_w=jnp.stack([g["w"] for g in pool]),
        pool_scale=jnp.concatenate([g["scale"] for g in pool], axis=0),
        gdn_w_in=jnp.stack(win),
        gdn_conv=jnp.stack([g["cw"][:GDN_CONV] for g in gdn]),
        gdn_a_log=jnp.concatenate([g["a_log"][:, :hv] for g in gdn], axis=0),
        gdn_dt_bias=jnp.concatenate([g["dt_bias"][:, :hv] for g in gdn], axis=0),
        gdn_o_gain=jnp.concatenate([g["o_gain"] for g in gdn], axis=0),
        gdn_w_o=jnp.stack([g["wo"] for g in gdn]),
        ffn_w_up=jnp.stack([jnp.concatenate([g["wu"][:, :f], g["wg"][:, :f]], axis=1) for g in ffn]),
        ffn_conv=jnp.stack([g["cw"][:FFN_CONV, :f] for g in ffn]),
        ffn_w_down=jnp.stack([g["wd"][:f] for g in ffn]))


ROW_SHARDED = ("att_w_o", "gdn_w_o", "ffn_w_down")


def _to_shards(name, full):
    if name == "pool_w":
        r = full.shape[2] // N_DEV
        a = full.reshape(full.shape[:2] + (N_DEV, r, full.shape[3]))
        return jnp.moveaxis(a, 2, 0)
    return _rows_to_shards(full) if name in ROW_SHARDED else _cols_to_shards(full)


def kernel(x, mix_norm, ffn_norm, att_w_qkv, att_q_gain, att_k_gain, att_rel_bias, att_w_o, pool_w, pool_scale, gdn_w_in, gdn_conv, gdn_a_log, gdn_dt_bias, gdn_o_gain, gdn_w_o, ffn_w_up, ffn_conv, ffn_w_down, loss_target, m_mix_norm, m_ffn_norm, m_att_w_qkv, m_att_q_gain, m_att_k_gain, m_att_rel_bias, m_att_w_o, m_pool_w, m_pool_scale, m_gdn_w_in, m_gdn_conv, m_gdn_a_log, m_gdn_dt_bias, m_gdn_o_gain, m_gdn_w_o, m_ffn_w_up, m_ffn_conv, m_ffn_w_down, v_mix_norm, v_ffn_norm, v_att_w_qkv, v_att_q_gain, v_att_k_gain, v_att_rel_bias, v_att_w_o, v_pool_w, v_pool_scale, v_gdn_w_in, v_gdn_conv, v_gdn_a_log, v_gdn_dt_bias, v_gdn_o_gain, v_gdn_w_o, v_ffn_w_up, v_ffn_conv, v_ffn_w_down):
    w = dict(zip(WEIGHTS, (mix_norm, ffn_norm, att_w_qkv, att_q_gain, att_k_gain, att_rel_bias, att_w_o, pool_w, pool_scale, gdn_w_in, gdn_conv, gdn_a_log, gdn_dt_bias, gdn_o_gain, gdn_w_o, ffn_w_up, ffn_conv, ffn_w_down)))
    m = dict(zip(WEIGHTS, (m_mix_norm, m_ffn_norm, m_att_w_qkv, m_att_q_gain, m_att_k_gain, m_att_rel_bias, m_att_w_o, m_pool_w, m_pool_scale, m_gdn_w_in, m_gdn_conv, m_gdn_a_log, m_gdn_dt_bias, m_gdn_o_gain, m_gdn_w_o, m_ffn_w_up, m_ffn_conv, m_ffn_w_down)))
    v = dict(zip(WEIGHTS, (v_mix_norm, v_ffn_norm, v_att_w_qkv, v_att_q_gain, v_att_k_gain, v_att_rel_bias, v_att_w_o, v_pool_w, v_pool_scale, v_gdn_w_in, v_gdn_conv, v_gdn_a_log, v_gdn_dt_bias, v_gdn_o_gain, v_gdn_w_o, v_ffn_w_up, v_ffn_conv, v_ffn_w_down)))

    gathered = {n: _all_gather(w[n] if n in KEEP_F32 else w[n].astype(BF), f"gather_{n}") for n in BIG}
    small_shapes = [w[n].shape for n in SMALL_SHARDED]
    small_g = _all_gather(_pack([w[n] for n in SMALL_SHARDED]), "gather_small")
    small = {}
    for n, a in zip(SMALL_SHARDED, _unpack(small_g, small_shapes, lead=(N_DEV,))):
        small[n] = jnp.moveaxis(a, 0, -2).reshape(a.shape[1:-1] + (N_DEV * a.shape[-1],))
    full, f = _assemble_weights(w, gathered, small)

    loss_row, dx, grads = _local_step(x[0], loss_target[0], full)
    loss = lax.psum(loss_row[0, 0], AXES)
    gfull = _full_gradients(grads, w, f)

    out = {}
    for n in BIG:
        shards = _to_shards(n, gfull[n])
        parts = _exchange(shards if n in KEEP_F32 else shards.astype(BF), f"exchange_{n}")
        c = w[n].shape[-1]
        res = _adamw(parts.reshape(N_DEV, -1, c), w[n].reshape(-1, c), m[n].reshape(-1, c), v[n].reshape(-1, c), f"adamw_{n}")
        out[n] = [a.reshape(w[n].shape) for a in res]
    sparts = _exchange(jnp.stack([_pack([_to_shards(n, gfull[n])[k] for n in SMALL_SHARDED]) for k in range(N_DEV)]),
                       "exchange_small")
    res = _adamw(sparts, *[_pack([t[n] for n in SMALL_SHARDED]) for t in (w, m, v)], "adamw_small")
    for n, *vals in zip(SMALL_SHARDED, *[_unpack(a, small_shapes) for a in res]):
        out[n] = vals
    rep_shapes = [w[n].shape for n in REPLICATED]
    rparts = _all_gather(_pack([gfull[n] for n in REPLICATED]), "gather_replicated_grads")
    res = _adamw(rparts, *[_pack([t[n] for n in REPLICATED]) for t in (w, m, v)], "adamw_replicated")
    for n, *vals in zip(REPLICATED, *[_unpack(a, rep_shapes) for a in res]):
        out[n] = vals
    return (loss, dx[None], *[out[n][0] for n in WEIGHTS], *[out[n][1] for n in WEIGHTS],
            *[out[n][2] for n in WEIGHTS], *[out[n][3] for n in WEIGHTS])
```

```python
import functools

import numpy as np
import jax
import jax.numpy as jnp
from jax import lax
from jax.experimental import pallas as pl
from jax.experimental.pallas import tpu as pltpu

F32 = jnp.float32
BF = jnp.bfloat16
SDS = jax.ShapeDtypeStruct

EPS = 1e-6
MASK_VALUE = -1e30
CHUNK = 64
LEFT_CHUNKS = 8
BAND_LEFT = LEFT_CHUNKS * CHUNK
BAND = BAND_LEFT + CHUNK
MAX_REL = 256
NUM_REL = (CHUNK - 1) + MAX_REL + 1
HEAD = 128
LANE = 128
HALO = 16
POOL_WINDOWS = (2, 4, 8, 16)
GDN_CONV = 4
FFN_CONV = 3
FF_PAD = 512
N_DEV = 8
AXES = ("x", "y", "c")
VMEM_LIMIT = 56 * 1024 * 1024

ADAM_LR = 0.001
ADAM_B1 = 0.9
ADAM_B2 = 0.999
ADAM_EPS = 1e-08
ADAM_WD = 0.01
ADAM_STEP = 10

HI = lax.Precision.HIGHEST
NT = (((1,), (1,)), ((), ()))
TN = (((0,), (0,)), ((), ()))


def _tile(n, target, mult=LANE):
    if n <= target:
        return n
    t = (target // mult) * mult
    while t >= mult:
        if n % t == 0:
            return t
        t -= mult
    return n


def _params(*sem):
    return pltpu.CompilerParams(dimension_semantics=sem, vmem_limit_bytes=VMEM_LIMIT)


def _silu(x):
    return x / (1.0 + jnp.exp(-x))


@functools.partial(jax.custom_vjp, nondiff_argnums=(1,))
def _shift(x, k):
    return pltpu.roll(x, k % x.shape[0], axis=0)


def _shift_fwd(x, k):
    return _shift(x, k), None


def _shift_bwd(k, _, g):
    return (pltpu.roll(g, (-k) % g.shape[0], axis=0),)


_shift.defvjp(_shift_fwd, _shift_bwd)


def _mm(a, b, *, name, ta=False, tb=False, out_dtype=BF, res=None, tm=1024, tn=1024, tk=1024):
    m, k = (a.shape[1], a.shape[0]) if ta else a.shape
    n, kb = (b.shape[0], b.shape[1]) if tb else (b.shape[1], b.shape[0])
    assert k == kb, (a.shape, b.shape, ta, tb)
    tm, tn, tk = _tile(m, tm), _tile(n, tn), _tile(k, tk)
    nk = k // tk
    dims = (((0 if ta else 1,), (1 if tb else 0,)), ((), ()))

    def body(*refs):
        if res is None:
            a_ref, b_ref, o_ref, acc = refs
        else:
            a_ref, b_ref, r_ref, o_ref, acc = refs
        kk = pl.program_id(2)

        @pl.when(kk == 0)
        def _():
            acc[...] = jnp.zeros_like(acc)

        acc[...] += lax.dot_general(a_ref[...].astype(BF), b_ref[...].astype(BF), dims, preferred_element_type=F32)

        @pl.when(kk == nk - 1)
        def _():
            r = acc[...]
            if res is not None:
                r = r + r_ref[...].astype(F32)
            o_ref[...] = r.astype(out_dtype)

    a_spec = pl.BlockSpec((tk, tm), lambda i, j, q: (q, i)) if ta else pl.BlockSpec((tm, tk), lambda i, j, q: (i, q))
    b_spec = pl.BlockSpec((tn, tk), lambda i, j, q: (j, q)) if tb else pl.BlockSpec((tk, tn), lambda i, j, q: (q, j))
    o_spec = pl.BlockSpec((tm, tn), lambda i, j, q: (i, j))
    ins, specs = [a, b], [a_spec, b_spec]
    if res is not None:
        ins.append(res)
        specs.append(o_spec)
    return pl.pallas_call(
        body, name=name, grid=(m // tm, n // tn, nk), in_specs=specs, out_specs=o_spec,
        out_shape=SDS((m, n), out_dtype), scratch_shapes=[pltpu.VMEM((tm, tn), F32)],
        compiler_params=_params("parallel", "parallel", "arbitrary"))(*ins)


def _rms(x, gain):
    return x * lax.rsqrt(jnp.mean(x * x, axis=-1, keepdims=True) + EPS) * gain


def _rmsnorm_fwd(x, gain, name):
    s, d = x.shape
    ts = _tile(s, 256, 8)

    def body(x_ref, g_ref, o_ref):
        o_ref[...] = _rms(x_ref[...], g_ref[...]).astype(BF)

    return pl.pallas_call(
        body, name=name, grid=(s // ts,),
        in_specs=[pl.BlockSpec((ts, d), lambda i: (i, 0)), pl.BlockSpec((1, d), lambda i: (0, 0))],
        out_specs=pl.BlockSpec((ts, d), lambda i: (i, 0)), out_shape=SDS((s, d), BF),
        compiler_params=_params("parallel"))(x, gain)


def _rmsnorm_bwd(x, gain, dh, dres, name):
    s, d = x.shape
    ts = _tile(s, 256, 8)

    def body(x_ref, g_ref, dh_ref, dr_ref, dx_ref, dg_ref):
        i = pl.program_id(0)
        _, vjp = jax.vjp(_rms, x_ref[...], g_ref[...])
        dx, dg = vjp(dh_ref[...].astype(F32))
        dx_ref[...] = dr_ref[...] + dx

        @pl.when(i == 0)
        def _():
            dg_ref[...] = dg

        @pl.when(i > 0)
        def _():
            dg_ref[...] += dg

    row = pl.BlockSpec((ts, d), lambda i: (i, 0))
    vec = pl.BlockSpec((1, d), lambda i: (0, 0))
    return pl.pallas_call(
        body, name=name, grid=(s // ts,), in_specs=[row, vec, row, row], out_specs=[row, vec],
        out_shape=[SDS((s, d), F32), SDS((1, d), F32)], compiler_params=_params("arbitrary"))(x, gain, dh, dres)


def _ffn_act_tile(u_ext, g_ext, cw):
    acc = u_ext * cw[2:3] + _shift(u_ext, 1) * cw[1:2] + _shift(u_ext, 2) * cw[0:1]
    return _silu(acc) * g_ext


def _halo_index(rows_per_block):
    per = rows_per_block // HALO
    return lambda rb: jnp.maximum(rb * per - 1, 0)


def _ffn_act_fwd(u, g, cw, name):
    s, f = u.shape
    r, tc = _tile(s, 512, HALO), _tile(f, 512)
    hidx = _halo_index(r)

    def body(uc, uh, gc, cw_ref, o_ref):
        rb = pl.program_id(1)
        keep = jnp.where(rb == 0, 0.0, 1.0)
        u_ext = jnp.concatenate([uh[...].astype(F32) * keep, uc[...].astype(F32)], axis=0)
        g_ext = jnp.concatenate([jnp.zeros((HALO, tc), F32), gc[...].astype(F32)], axis=0)
        o_ref[...] = _ffn_act_tile(u_ext, g_ext, cw_ref[...])[HALO:].astype(BF)

    cur = pl.BlockSpec((r, tc), lambda j, rb: (rb, j))
    return pl.pallas_call(
        body, name=name, grid=(f // tc, s // r),
        in_specs=[cur, pl.BlockSpec((HALO, tc), lambda j, rb: (hidx(rb), j)), cur,
                  pl.BlockSpec((8, tc), lambda j, rb: (0, j))],
        out_specs=cur, out_shape=SDS((s, f), BF), compiler_params=_params("parallel", "parallel"))(u, u, g, cw)


def _ffn_act_bwd(u, g, cw, da, name):
    s, f = u.shape
    r, tc = _tile(s, 512, HALO), _tile(f, 512)
    nb = s // r
    hidx = _halo_index(r)

    def body(uc, uh, gc, da_ref, cw_ref, du_ref, dg_ref, dcw_ref, carry):
        step = pl.program_id(1)
        rb = nb - 1 - step
        keep = jnp.where(rb == 0, 0.0, 1.0)
        zero = jnp.zeros((HALO, tc), F32)
        u_ext = jnp.concatenate([uh[...].astype(F32) * keep, uc[...].astype(F32)], axis=0)
        g_ext = jnp.concatenate([zero, gc[...].astype(F32)], axis=0)
        ct = jnp.concatenate([zero, da_ref[...].astype(F32)], axis=0)
        _, vjp = jax.vjp(_ffn_act_tile, u_ext, g_ext, cw_ref[...])
        du_ext, dg_ext, dcw = vjp(ct)

        @pl.when(step == 0)
        def _():
            carry[...] = zero
            dcw_ref[...] = jnp.zeros_like(dcw_ref)

        du_ref[pl.ds(0, r - HALO), :] = du_ext[HALO:r].astype(BF)
        du_ref[pl.ds(r - HALO, HALO), :] = (du_ext[r:] + carry[...]).astype(BF)
        carry[...] = du_ext[:HALO]
        dg_ref[...] = dg_ext[HALO:].astype(BF)
        dcw_ref[...] += dcw

    cur = pl.BlockSpec((r, tc), lambda j, t: (nb - 1 - t, j))
    wspec = pl.BlockSpec((8, tc), lambda j, t: (0, j))
    return pl.pallas_call(
        body, name=name, grid=(f // tc, nb),
        in_specs=[cur, pl.BlockSpec((HALO, tc), lambda j, t: (hidx(nb - 1 - t), j)), cur, cur, wspec],
        out_specs=[cur, cur, wspec], out_shape=[SDS((s, f), BF), SDS((s, f), BF), SDS((8, f), F32)],
        scratch_shapes=[pltpu.VMEM((HALO, tc), F32)],
        compiler_params=_params("parallel", "arbitrary"))(u, u, g, da, cw)


def _loss_head(y, target, name):
    s, d = y.shape
    ts = _tile(s, 256, 8)

    def body(y_ref, t_ref, dy_ref, l_ref):
        i = pl.program_id(0)
        err = y_ref[...] - t_ref[...]
        dy_ref[...] = err * (1.0 / d)
        part = jnp.zeros((1, LANE), F32) + 0.5 * jnp.sum(jnp.sum(err * err, axis=1, keepdims=True), axis=0, keepdims=True) / d

        @pl.when(i == 0)
        def _():
            l_ref[...] = part

        @pl.when(i > 0)
        def _():
            l_ref[...] += part

    row = pl.BlockSpec((ts, d), lambda i: (i, 0))
    return pl.pallas_call(
        body, name=name, grid=(s // ts,), in_specs=[row, row],
        out_specs=[row, pl.BlockSpec((1, LANE), lambda i: (0, 0))],
        out_shape=[SDS((s, d), F32), SDS((1, LANE), F32)], compiler_params=_params("arbitrary"))(y, target)


def _rel_index():
    rel = BAND_LEFT + np.arange(CHUNK)[:, None] - np.arange(BAND)[None, :]
    return (np.clip(rel, -(CHUNK - 1), MAX_REL) + (CHUNK - 1)).reshape(1, CHUNK * BAND).astype(np.int32)


def _onehot(idx_row):
    rows = lax.broadcasted_iota(jnp.int32, (NUM_REL, idx_row.shape[1]), 0)
    return jnp.where(rows == idx_row, 1.0, 0.0).astype(F32)


def _bias_expand(rel_bias, name):
    h = rel_bias.shape[0]
    n = CHUNK * BAND
    tn = n // 8

    def body(rb_ref, idx_ref, o_ref):
        o_ref[...] = jnp.dot(rb_ref[...], _onehot(idx_ref[...]), precision=HI, preferred_element_type=F32)

    out = pl.pallas_call(
        body, name=name, grid=(n // tn,),
        in_specs=[pl.BlockSpec((h, NUM_REL), lambda j: (0, 0)), pl.BlockSpec((1, tn), lambda j: (0, j))],
        out_specs=pl.BlockSpec((h, tn), lambda j: (0, j)), out_shape=SDS((h, n), F32),
        compiler_params=_params("parallel"))(rel_bias, jnp.asarray(_rel_index()))
    return out.reshape(h, CHUNK, BAND)


def _bias_reduce(dbias, name):
    h = dbias.shape[0]
    n = CHUNK * BAND
    tn = n // 8

    def body(db_ref, idx_ref, o_ref):
        j = pl.program_id(0)
        part = lax.dot_general(db_ref[...], _onehot(idx_ref[...]), NT, precision=HI, preferred_element_type=F32)

        @pl.when(j == 0)
        def _():
            o_ref[...] = part

        @pl.when(j > 0)
        def _():
            o_ref[...] += part

    return pl.pallas_call(
        body, name=name, grid=(n // tn,),
        in_specs=[pl.BlockSpec((h, tn), lambda j: (0, j)), pl.BlockSpec((1, tn), lambda j: (0, j))],
        out_specs=pl.BlockSpec((h, NUM_REL), lambda j: (0, 0)), out_shape=SDS((h, NUM_REL), F32),
        compiler_params=_params("arbitrary"))(dbias.reshape(h, n), jnp.asarray(_rel_index()))


def _headnorm(x, gain):
    outs = []
    for hh in range(x.shape[1] // HEAD):
        xh = x[:, hh * HEAD:(hh + 1) * HEAD]
        outs.append(xh * lax.rsqrt(jnp.mean(xh * xh, axis=-1, keepdims=True) + EPS) * gain)
    return jnp.concatenate(outs, axis=1)


def _qkv_post_fwd(qkv, qg, kg, name):
    s, d3 = qkv.shape
    d = d3 // 3
    r = BAND_LEFT
    nb = s // r

    def body(x_ref, qg_ref, kg_ref, q_ref, k_ref, v_ref):
        i = pl.program_id(0)
        keep = jnp.where(i == 0, 0.0, 1.0)
        q_ref[...] = _headnorm(x_ref[:, 0:d].astype(F32), qg_ref[...]).astype(BF)
        k_ref[...] = (_headnorm(x_ref[:, d:2 * d].astype(F32), kg_ref[...]) * keep).astype(BF)
        v_ref[...] = (x_ref[:, 2 * d:].astype(F32) * keep).astype(BF)

    prev = lambda i: (jnp.maximum(i - 1, 0), 0)
    vec = pl.BlockSpec((1, HEAD), lambda i: (0, 0))
    return pl.pallas_call(
        body, name=name, grid=(nb + 1,),
        in_specs=[pl.BlockSpec((r, d3), prev), vec, vec],
        out_specs=[pl.BlockSpec((r, d), prev), pl.BlockSpec((r, d), lambda i: (i, 0)), pl.BlockSpec((r, d), lambda i: (i, 0))],
        out_shape=[SDS((s, d), BF), SDS((s + r, d), BF), SDS((s + r, d), BF)],
        compiler_params=_params("arbitrary"))(qkv, qg, kg)


def _qkv_post_bwd(qkv, qg, kg, dq, dkpad, dvpad, name):
    s, d3 = qkv.shape
    d = d3 // 3
    r = _tile(s, 256, 16)
    off = BAND_LEFT // r

    def body(x_ref, qg_ref, kg_ref, dq_ref, dk_ref, dv_ref, o_ref, dqg_ref, dkg_ref):
        i = pl.program_id(0)
        _, vq = jax.vjp(_headnorm, x_ref[:, 0:d].astype(F32), qg_ref[...])
        dxq, dqg = vq(dq_ref[...])
        _, vk = jax.vjp(_headnorm, x_ref[:, d:2 * d].astype(F32), kg_ref[...])
        dxk, dkg = vk(dk_ref[...])
        o_ref[:, 0:d] = dxq.astype(BF)
        o_ref[:, d:2 * d] = dxk.astype(BF)
        o_ref[:, 2 * d:] = dv_ref[...].astype(BF)

        @pl.when(i == 0)
        def _():
            dqg_ref[...] = dqg
            dkg_ref[...] = dkg

        @pl.when(i > 0)
        def _():
            dqg_ref[...] += dqg
            dkg_ref[...] += dkg

    vec = pl.BlockSpec((1, HEAD), lambda i: (0, 0))
    row3 = pl.BlockSpec((r, d3), lambda i: (i, 0))
    row = pl.BlockSpec((r, d), lambda i: (i, 0))
    padrow = pl.BlockSpec((r, d), lambda i: (i + off, 0))
    return pl.pallas_call(
        body, name=name, grid=(s // r,), in_specs=[row3, vec, vec, row, padrow, padrow],
        out_specs=[row3, vec, vec], out_shape=[SDS((s, d3), BF), SDS((1, HEAD), F32), SDS((1, HEAD), F32)],
        compiler_params=_params("arbitrary"))(qkv, qg, kg, dq, dkpad, dvpad)


ATT_QB = 256


def _att_probs(q, kw, bias, c):
    sc = lax.dot_general(q, kw, NT, preferred_element_type=F32) * (HEAD ** -0.5) + bias
    lane = lax.broadcasted_iota(jnp.int32, (CHUNK, BAND), 1)
    sc = jnp.where(lane + c * CHUNK >= BAND_LEFT, sc, MASK_VALUE)
    p = jnp.exp(sc - jnp.max(sc, axis=-1, keepdims=True))
    return p / jnp.sum(p, axis=-1, keepdims=True)


def _attn_fwd(q, kpad, vpad, bias, name):
    s, d = q.shape
    h = d // HEAD
    sp = kpad.shape[0]
    qb = _tile(s, ATT_QB, CHUNK)
    per = qb // CHUNK

    def body(q_ref, k_ref, v_ref, b_ref, o_ref):
        i = pl.program_id(1)
        bias_h = b_ref[0]
        for cc in range(per):
            c = i * per + cc
            start = pl.multiple_of(c * CHUNK, CHUNK)
            p = _att_probs(q_ref[pl.ds(cc * CHUNK, CHUNK), :], k_ref[pl.ds(start, BAND), :], bias_h, c)
            o_ref[pl.ds(cc * CHUNK, CHUNK), :] = jnp.dot(
                p.astype(BF), v_ref[pl.ds(start, BAND), :], preferred_element_type=F32).astype(BF)

    qspec = pl.BlockSpec((qb, HEAD), lambda hh, i: (i, hh))
    kspec = pl.BlockSpec((sp, HEAD), lambda hh, i: (0, hh))
    return pl.pallas_call(
        body, name=name, grid=(h, s // qb),
        in_specs=[qspec, kspec, kspec, pl.BlockSpec((1, CHUNK, BAND), lambda hh, i: (hh, 0, 0))],
        out_specs=qspec, out_shape=SDS((s, d), BF), compiler_params=_params("parallel", "arbitrary"))(q, kpad, vpad, bias)


def _attn_bwd(q, kpad, vpad, bias, do, name):
    s, d = q.shape
    h = d // HEAD
    sp = kpad.shape[0]
    qb = _tile(s, ATT_QB, CHUNK)
    per = qb // CHUNK
    scale = HEAD ** -0.5

    def body(q_ref, k_ref, v_ref, b_ref, do_ref, dq_ref, dk_ref, dv_ref, db_ref):
        i = pl.program_id(1)

        @pl.when(i == 0)
        def _():
            dk_ref[...] = jnp.zeros_like(dk_ref)
            dv_ref[...] = jnp.zeros_like(dv_ref)
            db_ref[...] = jnp.zeros_like(db_ref)

        bias_h = b_ref[0]
        for cc in range(per):
            c = i * per + cc
            start = pl.multiple_of(c * CHUNK, CHUNK)
            rows = pl.ds(cc * CHUNK, CHUNK)
            win = pl.ds(start, BAND)
            qc, kw, vw = q_ref[rows, :], k_ref[win, :], v_ref[win, :]
            doc = do_ref[rows, :].astype(BF)
            p = _att_probs(qc, kw, bias_h, c)
            dp = lax.dot_general(doc, vw, NT, preferred_element_type=F32)
            ds = p * (dp - jnp.sum(p * dp, axis=-1, keepdims=True))
            db_ref[0] += ds
            dsb = (ds * scale).astype(BF)
            dq_ref[rows, :] = jnp.dot(dsb, kw, preferred_element_type=F32)
            dk_ref[win, :] += lax.dot_general(dsb, qc, TN, preferred_element_type=F32)
            dv_ref[win, :] += lax.dot_general(p.astype(BF), doc, TN, preferred_element_type=F32)

    qspec = pl.BlockSpec((qb, HEAD), lambda hh, i: (i, hh))
    kspec = pl.BlockSpec((sp, HEAD), lambda hh, i: (0, hh))
    bspec = pl.BlockSpec((1, CHUNK, BAND), lambda hh, i: (hh, 0, 0))
    return pl.pallas_call(
        body, name=name, grid=(h, s // qb), in_specs=[qspec, kspec, kspec, bspec, qspec],
        out_specs=[qspec, kspec, kspec, bspec],
        out_shape=[SDS((s, d), F32), SDS((sp, d), F32), SDS((sp, d), F32), SDS((h, CHUNK, BAND), F32)],
        compiler_params=_params("parallel", "arbitrary"))(q, kpad, vpad, bias, do)


def _pool_tile(x_ext, gain, w4, scale, row0):
    n, d = x_ext.shape
    dg = d // len(POOL_WINDOWS)
    pos = lax.broadcasted_iota(jnp.int32, (n, 1), 0) + row0
    hn = _rms(x_ext, gain) * jnp.where(pos >= 0, 1.0, 0.0)
    outs = []
    for gi, w in enumerate(POOL_WINDOWS):
        hg = hn[:, gi * dg:(gi + 1) * dg]
        acc, k = hg, 1
        while k < w:
            acc = acc + _shift(acc, k)
            k *= 2
        inv = 1.0 / jnp.clip(pos + 1, 1, w).astype(F32)
        pooled = acc * inv - hg
        outs.append(jnp.dot(pooled.astype(BF), w4[gi].astype(BF), preferred_element_type=F32))
    return jnp.concatenate(outs, axis=1) * scale


POOL_ROWS = 128


def _pool_fwd(x, gain, w4, scale, name):
    s, d = x.shape
    r = _tile(s, POOL_ROWS, HALO)
    hidx = _halo_index(r)

    def body(xc, xh, g_ref, w_ref, s_ref, o_ref):
        rb = pl.program_id(0)
        x_ext = jnp.concatenate([xh[...], xc[...]], axis=0)
        y = _pool_tile(x_ext, g_ref[...], [w_ref[gi] for gi in range(len(POOL_WINDOWS))], s_ref[...], rb * r - HALO)
        o_ref[...] = xc[...] + y[HALO:]

    cur = pl.BlockSpec((r, d), lambda rb: (rb, 0))
    vec = pl.BlockSpec((1, d), lambda rb: (0, 0))
    return pl.pallas_call(
        body, name=name, grid=(s // r,),
        in_specs=[cur, pl.BlockSpec((HALO, d), lambda rb: (hidx(rb), 0)), vec,
                  pl.BlockSpec(w4.shape, lambda rb: (0, 0, 0)), vec],
        out_specs=cur, out_shape=SDS((s, d), F32), compiler_params=_params("parallel"))(x, x, gain, w4, scale)


def _pool_bwd(x, gain, w4, scale, dy, name):
    s, d = x.shape
    r = _tile(s, POOL_ROWS, HALO)
    nb = s // r
    hidx = _halo_index(r)

    def body(xc, xh, g_ref, w_ref, s_ref, dy_ref, dx_ref, dg_ref, dw_ref, ds_ref, carry):
        step = pl.program_id(0)
        rb = nb - 1 - step
        x_ext = jnp.concatenate([xh[...], xc[...]], axis=0)
        fn = functools.partial(_pool_tile, row0=rb * r - HALO)
        _, vjp = jax.vjp(fn, x_ext, g_ref[...], [w_ref[gi] for gi in range(len(POOL_WINDOWS))], s_ref[...])
        ct = jnp.concatenate([jnp.zeros((HALO, d), F32), dy_ref[...]], axis=0)
        dx_ext, dg, dws, dsc = vjp(ct)

        @pl.when(step == 0)
        def _():
            carry[...] = jnp.zeros_like(carry)
            dg_ref[...] = jnp.zeros_like(dg_ref)
            dw_ref[...] = jnp.zeros_like(dw_ref)
            ds_ref[...] = jnp.zeros_like(ds_ref)

        dx_ref[...] = dy_ref[...] + dx_ext[HALO:]
        dx_ref[pl.ds(r - HALO, HALO), :] += carry[...]
        carry[...] = dx_ext[:HALO]
        dg_ref[...] += dg
        for gi, dw in enumerate(dws):
            dw_ref[gi] += dw
        ds_ref[...] += dsc

    cur = pl.BlockSpec((r, d), lambda t: (nb - 1 - t, 0))
    vec = pl.BlockSpec((1, d), lambda t: (0, 0))
    wspec = pl.BlockSpec(w4.shape, lambda t: (0, 0, 0))
    return pl.pallas_call(
        body, name=name, grid=(nb,),
        in_specs=[cur, pl.BlockSpec((HALO, d), lambda t: (hidx(nb - 1 - t), 0)), vec, wspec, vec, cur],
        out_specs=[cur, vec, wspec, vec],
        out_shape=[SDS((s, d), F32), SDS((1, d), F32), SDS(w4.shape, F32), SDS((1, d), F32)],
        scratch_shapes=[pltpu.VMEM((HALO, d), F32)], compiler_params=_params("arbitrary"))(x, x, gain, w4, scale, dy)


def _gdn_conv_tile(u_ext, cw, kind):
    acc = (u_ext * cw[3:4] + _shift(u_ext, 1) * cw[2:3] + _shift(u_ext, 2) * cw[1:2] + _shift(u_ext, 3) * cw[0:1])
    y = _silu(acc)
    if kind != "v":
        y = y * lax.rsqrt(jnp.sum(y * y, axis=-1, keepdims=True) + EPS)
    if kind == "q":
        y = y * (HEAD ** -0.5)
    return y


def _gdn_conv_fwd(proj, cw, kind, head0, nheads, name):
    s = proj.shape[0]
    r = _tile(s, 512, HALO)
    hidx = _halo_index(r)

    def body(uc, uh, cw_ref, o_ref):
        rb = pl.program_id(1)
        keep = jnp.where(rb == 0, 0.0, 1.0)
        u_ext = jnp.concatenate([uh[...].astype(F32) * keep, uc[...].astype(F32)], axis=0)
        o_ref[...] = _gdn_conv_tile(u_ext, cw_ref[...], kind)[HALO:].astype(BF)

    return pl.pallas_call(
        body, name=name, grid=(nheads, s // r),
        in_specs=[pl.BlockSpec((r, HEAD), lambda j, rb: (rb, head0 + j)),
                  pl.BlockSpec((HALO, HEAD), lambda j, rb: (hidx(rb), head0 + j)),
                  pl.BlockSpec((8, HEAD), lambda j, rb: (0, head0 + j))],
        out_specs=pl.BlockSpec((r, HEAD), lambda j, rb: (rb, j)), out_shape=SDS((s, nheads * HEAD), BF),
        compiler_params=_params("parallel", "parallel"))(proj, proj, cw)


def _gdn_conv_bwd(proj, cw, dy, kind, head0, nheads, name):
    s = proj.shape[0]
    r = _tile(s, 512, HALO)
    nb = s // r
    hidx = _halo_index(r)
    rep = dy.shape[1] // (nheads * HEAD)

    def body(*refs):
        uc, uh, cw_ref = refs[:3]
        dys = refs[3:3 + rep]
        du_ref, dcw_ref, carry = refs[3 + rep:]
        step = pl.program_id(1)
        rb = nb - 1 - step
        keep = jnp.where(rb == 0, 0.0, 1.0)
        zero = jnp.zeros((HALO, HEAD), F32)
        u_ext = jnp.concatenate([uh[...].astype(F32) * keep, uc[...].astype(F32)], axis=0)
        dyc = dys[0][...]
        for extra in dys[1:]:
            dyc = dyc + extra[...]
        _, vjp = jax.vjp(functools.partial(_gdn_conv_tile, kind=kind), u_ext, cw_ref[...])
        du_ext, dcw = vjp(jnp.concatenate([zero, dyc], axis=0))

        @pl.when(step == 0)
        def _():
            carry[...] = zero
            dcw_ref[...] = jnp.zeros_like(dcw_ref)

        du_ref[pl.ds(0, r - HALO), :] = du_ext[HALO:r].astype(BF)
        du_ref[pl.ds(r - HALO, HALO), :] = (du_ext[r:] + carry[...]).astype(BF)
        carry[...] = du_ext[:HALO]
        dcw_ref[...] += dcw

    dy_specs = [pl.BlockSpec((r, HEAD), functools.partial(lambda j, t, e: (nb - 1 - t, rep * j + e), e=e)) for e in range(rep)]
    return pl.pallas_call(
        body, name=name, grid=(nheads, nb),
        in_specs=[pl.BlockSpec((r, HEAD), lambda j, t: (nb - 1 - t, head0 + j)),
                  pl.BlockSpec((HALO, HEAD), lambda j, t: (hidx(nb - 1 - t), head0 + j)),
                  pl.BlockSpec((8, HEAD), lambda j, t: (0, head0 + j))] + dy_specs,
        out_specs=[pl.BlockSpec((r, HEAD), lambda j, t: (nb - 1 - t, j)), pl.BlockSpec((8, HEAD), lambda j, t: (0, j))],
        out_shape=[SDS((s, nheads * HEAD), BF), SDS((8, nheads * HEAD), F32)],
        scratch_shapes=[pltpu.VMEM((HALO, HEAD), F32)],
        compiler_params=_params("parallel", "arbitrary"))(proj, proj, cw, *([dy] * rep))


GATE_ROWS = 256


def _gates_tile(a, bt, a_log, dt_bias, hv):
    r = a.shape[0]
    z = a + dt_bias
    softplus = jnp.maximum(z, 0.0) + jnp.log(1.0 + jnp.exp(-jnp.abs(z)))
    g = -jnp.exp(a_log) * softplus
    ri = lax.broadcasted_iota(jnp.int32, (r, r), 0)
    ci = lax.broadcasted_iota(jnp.int32, (r, r), 1)
    same_chunk = jnp.right_shift(ri, 6) == jnp.right_shift(ci, 6)
    tri = jnp.where(same_chunk, jnp.where(ri >= ci, 1.0, 0.0), 0.0).astype(F32)
    gc = jnp.dot(tri, g, precision=HI, preferred_element_type=F32)
    beta = 1.0 / (1.0 + jnp.exp(-bt))
    er = lax.broadcasted_iota(jnp.int32, (LANE, hv * HEAD), 0)
    ec = lax.broadcasted_iota(jnp.int32, (LANE, hv * HEAD), 1)
    expand = jnp.where(er == jnp.right_shift(ec, 7), 1.0, 0.0).astype(F32)
    return (jnp.dot(gc, expand, precision=HI, preferred_element_type=F32),
            jnp.dot(beta, expand, precision=HI, preferred_element_type=F32))


def _gates_fwd(ab, a_log, dt_bias, hv, name):
    s = ab.shape[0]
    r = _tile(s, GATE_ROWS, CHUNK)

    def body(a_ref, b_ref, al_ref, dt_ref, gc_ref, bb_ref):
        gcb, btb = _gates_tile(a_ref[...], b_ref[...], al_ref[...], dt_ref[...], hv)
        gc_ref[...] = gcb
        bb_ref[...] = btb

    vec = pl.BlockSpec((1, LANE), lambda i: (0, 0))
    wide = pl.BlockSpec((r, hv * HEAD), lambda i: (i, 0))
    return pl.pallas_call(
        body, name=name, grid=(s // r,),
        in_specs=[pl.BlockSpec((r, LANE), lambda i: (i, 0)), pl.BlockSpec((r, LANE), lambda i: (i, 1)), vec, vec],
        out_specs=[wide, wide], out_shape=[SDS((s, hv * HEAD), F32)] * 2,
        compiler_params=_params("parallel"))(ab, ab, a_log, dt_bias)


def _gates_bwd(ab, a_log, dt_bias, dgcb, dbtb, hv, name):
    s = ab.shape[0]
    r = _tile(s, GATE_ROWS, CHUNK)

    def body(a_ref, b_ref, al_ref, dt_ref, dgc_ref, dbb_ref, dab_ref, dal_ref, ddt_ref):
        i = pl.program_id(0)
        _, vjp = jax.vjp(functools.partial(_gates_tile, hv=hv), a_ref[...], b_ref[...], al_ref[...], dt_ref[...])
        da, dbt, dal, ddt = vjp((dgc_ref[...], dbb_ref[...]))
        dab_ref[:, 0:LANE] = da
        dab_ref[:, LANE:] = dbt

        @pl.when(i == 0)
        def _():
            dal_ref[...] = dal
            ddt_ref[...] = ddt

        @pl.when(i > 0)
        def _():
            dal_ref[...] += dal
            ddt_ref[...] += ddt

    vec = pl.BlockSpec((1, LANE), lambda i: (0, 0))
    wide = pl.BlockSpec((r, hv * HEAD), lambda i: (i, 0))
    return pl.pallas_call(
        body, name=name, grid=(s // r,),
        in_specs=[pl.BlockSpec((r, LANE), lambda i: (i, 0)), pl.BlockSpec((r, LANE), lambda i: (i, 1)), vec, vec, wide, wide],
        out_specs=[pl.BlockSpec((r, 2 * LANE), lambda i: (i, 0)), vec, vec],
        out_shape=[SDS((s, 2 * LANE), F32), SDS((1, LANE), F32), SDS((1, LANE), F32)],
        compiler_params=_params("arbitrary"))(ab, ab, a_log, dt_bias, dgcb, dbtb)


def _split_bf16(a):
    hi = a.astype(BF)
    return hi, (a - hi.astype(F32)).astype(BF)


def _dot3(a, b, dims=(((1,), (0,)), ((), ()))):
    ah, al = _split_bf16(a)
    bh, bl = _split_bf16(b)
    d = lambda x, y: lax.dot_general(x, y, dims, preferred_element_type=F32)
    return d(ah, bh) + (d(ah, bl) + d(al, bh))


def _unit_lower_inverse(a):
    ri = lax.broadcasted_iota(jnp.int32, a.shape, 0)
    ci = lax.broadcasted_iota(jnp.int32, a.shape, 1)
    p = -a
    t = jnp.where(ri == ci, 1.0, 0.0) + p
    for _ in range(5):
        p = _dot3(p, p)
        t = t + _dot3(t, p)
    return t


@jax.custom_vjp
def _known_inverse(a, t):
    return t


def _known_inverse_fwd(a, t):
    return t, t


def _known_inverse_bwd(t, g):
    return -_dot3(_dot3(t, g, TN), t, NT), jnp.zeros_like(t)


_known_inverse.defvjp(_known_inverse_fwd, _known_inverse_bwd)


def _delta_decay(gcb):
    c = CHUNK
    ri = lax.broadcasted_iota(jnp.int32, (c, c), 0)
    ci = lax.broadcasted_iota(jnp.int32, (c, c), 1)
    causal = ri >= ci
    grow = jnp.concatenate([gcb, gcb], axis=0).T[:c, :c]
    return jnp.where(causal, jnp.exp(jnp.where(causal, gcb[:, :c] - grow, 0.0)), 0.0), ri > ci


def _delta_system(k, gcb, btb):
    decay, strict = _delta_decay(gcb)
    kk = lax.dot_general((k * btb).astype(BF), k.astype(BF), NT, preferred_element_type=F32)
    return jnp.where(strict, kk * decay, 0.0)


def _delta_prep(q, k, v, gcb, btb, tinv):
    decay, strict = _delta_decay(gcb)
    kb = k * btb
    kbf = k.astype(BF)
    a = jnp.where(strict, lax.dot_general(kb.astype(BF), kbf, NT, preferred_element_type=F32) * decay, 0.0)
    t = _known_inverse(a, tinv).astype(BF)
    u = jnp.dot(t, (v * btb).astype(BF), preferred_element_type=F32)
    w = jnp.dot(t, (kb * jnp.exp(gcb)).astype(BF), preferred_element_type=F32)
    attn = lax.dot_general(q.astype(BF), kbf, NT, preferred_element_type=F32) * decay
    return u, w, attn


def _delta_scan(u, w, attn, q, k, gcb, s_in):
    c = CHUNK
    glast = gcb[c - 1:c, :]
    sb = s_in.astype(BF)
    v_new = u - jnp.dot(w.astype(BF), sb, preferred_element_type=F32)
    vnb = v_new.astype(BF)
    o = (jnp.dot((q * jnp.exp(gcb)).astype(BF), sb, preferred_element_type=F32)
         + jnp.dot(attn.astype(BF), vnb, preferred_element_type=F32))
    ks = (k * jnp.exp(glast - gcb)).astype(BF)
    s_out = s_in * jnp.exp(glast[:, 0:1]) + lax.dot_general(ks, vnb, TN, preferred_element_type=F32)
    return o, s_out


PREP_ROWS = 256
PREP_HEADS = 2
SCAN_ROWS = 512
SCAN_HEADS = 4


def _delta_prep_fwd(q, k, v, gcb, btb, name):
    s, dv = v.shape
    hv = dv // HEAD
    g = PREP_HEADS
    assert dv // q.shape[1] == g
    r = _tile(s, PREP_ROWS, CHUNK)

    def body(q_ref, k_ref, v_ref, g_ref, b_ref, u_ref, w_ref, a_ref, t_ref):
        for cc in range(r // CHUNK):
            rows = pl.ds(cc * CHUNK, CHUNK)
            qc, kc = q_ref[rows, :].astype(F32), k_ref[rows, :].astype(F32)
            for hh in range(g):
                cols = pl.ds(hh * HEAD, HEAD)
                half = pl.ds(hh * CHUNK, CHUNK)
                gc, bc = g_ref[rows, cols], b_ref[rows, cols]
                tinv = _unit_lower_inverse(_delta_system(kc, gc, bc))
                u, w, attn = _delta_prep(qc, kc, v_ref[rows, cols].astype(F32), gc, bc, tinv)
                u_ref[rows, cols] = u
                w_ref[rows, cols] = w.astype(BF)
                a_ref[rows, half] = attn.astype(BF)
                t_ref[rows, half] = tinv

    kq = pl.BlockSpec((r, HEAD), lambda j, i: (i, j))
    vs = pl.BlockSpec((r, g * HEAD), lambda j, i: (i, j))
    sq = pl.BlockSpec((r, g * CHUNK), lambda j, i: (i, j))
    return pl.pallas_call(
        body, name=name, grid=(hv // g, s // r), in_specs=[kq, kq, vs, vs, vs], out_specs=[vs, vs, sq, sq],
        out_shape=[SDS((s, dv), F32), SDS((s, dv), BF), SDS((s, hv * CHUNK), BF), SDS((s, hv * CHUNK), F32)],
        compiler_params=_params("parallel", "parallel"))(q, k, v, gcb, btb)


def _delta_prep_bwd(q, k, v, gcb, btb, tinv, du, dw, dattn, dq_s, dk_s, dg_s, name):
    s, dv = v.shape
    hv = dv // HEAD
    g = PREP_HEADS
    r = _tile(s, PREP_ROWS, CHUNK)

    def body(q_ref, k_ref, v_ref, g_ref, b_ref, t_ref, du_ref, dw_ref, da_ref, dqs_ref, dks_ref, dgs_ref,
             dq_ref, dk_ref, dv_ref, dg_ref, db_ref):
        for cc in range(r // CHUNK):
            rows = pl.ds(cc * CHUNK, CHUNK)
            qc, kc = q_ref[rows, :].astype(F32), k_ref[rows, :].astype(F32)
            for hh in range(g):
                cols = pl.ds(hh * HEAD, HEAD)
                half = pl.ds(hh * CHUNK, CHUNK)
                fn = functools.partial(_delta_prep, tinv=t_ref[rows, half])
                _, vjp = jax.vjp(fn, qc, kc, v_ref[rows, cols].astype(F32), g_ref[rows, cols], b_ref[rows, cols])
                dq, dk, dvv, dg, db = vjp((du_ref[rows, cols].astype(F32), dw_ref[rows, cols].astype(F32),
                                           da_ref[rows, half].astype(F32)))
                dq_ref[rows, cols] = dq + dqs_ref[rows, cols]
                dk_ref[rows, cols] = dk + dks_ref[rows, cols]
                dv_ref[rows, cols] = dvv
                dg_ref[rows, cols] = dg + dgs_ref[rows, cols]
                db_ref[rows, cols] = db

    kq = pl.BlockSpec((r, HEAD), lambda j, i: (i, j))
    vs = pl.BlockSpec((r, g * HEAD), lambda j, i: (i, j))
    sq = pl.BlockSpec((r, g * CHUNK), lambda j, i: (i, j))
    return pl.pallas_call(
        body, name=name, grid=(hv // g, s // r), in_specs=[kq, kq, vs, vs, vs, sq, vs, vs, sq, vs, vs, vs],
        out_specs=[vs] * 5, out_shape=[SDS((s, dv), F32)] * 5,
        compiler_params=_params("parallel", "parallel"))(q, k, v, gcb, btb, tinv, du, dw, dattn, dq_s, dk_s, dg_s)


def _delta_scan_fwd(u, w, attn, q, k, gcb, name):
    s, dv = u.shape
    hv = dv // HEAD
    rep = dv // q.shape[1]
    g = min(SCAN_HEADS, hv)
    r = _tile(s, SCAN_ROWS, CHUNK)
    per = r // CHUNK

    def body(u_ref, w_ref, a_ref, q_ref, k_ref, g_ref, o_ref, st_ref, state):
        @pl.when(pl.program_id(1) == 0)
        def _():
            state[...] = jnp.zeros_like(state)

        def chunk(cc, carry):
            rows = pl.ds(pl.multiple_of(cc * CHUNK, CHUNK), CHUNK)
            for hh in range(g):
                cols = pl.ds(hh * HEAD, HEAD)
                kcols = pl.ds((hh // rep) * HEAD, HEAD)
                s_in = state[hh]
                st_ref[hh, cc] = s_in
                o, s_out = _delta_scan(u_ref[rows, cols], w_ref[rows, cols].astype(F32),
                                       a_ref[rows, pl.ds(hh * CHUNK, CHUNK)].astype(F32),
                                       q_ref[rows, kcols].astype(F32), k_ref[rows, kcols].astype(F32), g_ref[rows, cols], s_in)
                o_ref[rows, cols] = o.astype(BF)
                state[hh] = s_out
            return carry

        lax.fori_loop(0, per, chunk, 0)

    kq = pl.BlockSpec((r, g // rep * HEAD), lambda j, i: (i, j))
    vs = pl.BlockSpec((r, g * HEAD), lambda j, i: (i, j))
    sq = pl.BlockSpec((r, g * CHUNK), lambda j, i: (i, j))
    return pl.pallas_call(
        body, name=name, grid=(hv // g, s // r), in_specs=[vs, vs, sq, kq, kq, vs],
        out_specs=[vs, pl.BlockSpec((g, per, HEAD, HEAD), lambda j, i: (j, i, 0, 0))],
        out_shape=[SDS((s, dv), BF), SDS((hv, s // CHUNK, HEAD, HEAD), F32)],
        scratch_shapes=[pltpu.VMEM((g, HEAD, HEAD), F32)],
        compiler_params=_params("parallel", "arbitrary"))(u, w, attn, q, k, gcb)


def _delta_scan_bwd(u, w, attn, q, k, gcb, states, do, name):
    s, dv = u.shape
    hv = dv // HEAD
    rep = dv // q.shape[1]
    g = min(SCAN_HEADS, hv)
    r = _tile(s, SCAN_ROWS, CHUNK)
    per = r // CHUNK
    nb = s // r

    def body(u_ref, w_ref, a_ref, q_ref, k_ref, g_ref, st_ref, do_ref, du_ref, dw_ref, da_ref, dq_ref, dk_ref, dg_ref, dstate):
        @pl.when(pl.program_id(1) == 0)
        def _():
            dstate[...] = jnp.zeros_like(dstate)

        def chunk(t, carry):
            cc = per - 1 - t
            rows = pl.ds(pl.multiple_of(cc * CHUNK, CHUNK), CHUNK)
            for hh in range(g):
                cols = pl.ds(hh * HEAD, HEAD)
                kcols = pl.ds((hh // rep) * HEAD, HEAD)
                half = pl.ds(hh * CHUNK, CHUNK)
                _, vjp = jax.vjp(_delta_scan, u_ref[rows, cols], w_ref[rows, cols].astype(F32), a_ref[rows, half].astype(F32),
                                 q_ref[rows, kcols].astype(F32), k_ref[rows, kcols].astype(F32), g_ref[rows, cols], st_ref[hh, cc])
                du, dw, da, dq, dk, dg, ds_in = vjp((do_ref[rows, cols], dstate[hh]))
                du_ref[rows, cols] = du.astype(BF)
                dw_ref[rows, cols] = dw.astype(BF)
                da_ref[rows, half] = da.astype(BF)
                dq_ref[rows, cols] = dq
                dk_ref[rows, cols] = dk
                dg_ref[rows, cols] = dg
                dstate[hh] = ds_in
            return carry

        lax.fori_loop(0, per, chunk, 0)

    kq = pl.BlockSpec((r, g // rep * HEAD), lambda j, i: (nb - 1 - i, j))
    vs = pl.BlockSpec((r, g * HEAD), lambda j, i: (nb - 1 - i, j))
    sq = pl.BlockSpec((r, g * CHUNK), lambda j, i: (nb - 1 - i, j))
    return pl.pallas_call(
        body, name=name, grid=(hv // g, nb),
        in_specs=[vs, vs, sq, kq, kq, vs, pl.BlockSpec((g, per, HEAD, HEAD), lambda j, i: (j, nb - 1 - i, 0, 0)), vs],
        out_specs=[vs, vs, sq, vs, vs, vs],
        out_shape=[SDS((s, dv), BF), SDS((s, dv), BF), SDS((s, hv * CHUNK), BF)] + [SDS((s, dv), F32)] * 3,
        scratch_shapes=[pltpu.VMEM((g, HEAD, HEAD), F32)],
        compiler_params=_params("parallel", "arbitrary"))(u, w, attn, q, k, gcb, states, do)


def _delta_chunk(q, k, v, gcb, btb, s_in):
    c = CHUNK
    ri = lax.broadcasted_iota(jnp.int32, (c, c), 0)
    ci = lax.broadcasted_iota(jnp.int32, (c, c), 1)
    causal = ri >= ci
    gcol = gcb[:, :c]
    grow = jnp.concatenate([gcb, gcb], axis=0).T[:c, :c]
    decay = jnp.where(causal, jnp.exp(jnp.where(causal, gcol - grow, 0.0)), 0.0)
    kb = k * btb
    vb = v * btb
    kbf = k.astype(BF)
    a = jnp.where(ri > ci, lax.dot_general(kb.astype(BF), kbf, NT, preferred_element_type=F32) * decay, 0.0)
    p = -a
    t = jnp.where(ri == ci, 1.0, 0.0) + p
    for _ in range(5):
        p = jnp.dot(p, p, precision=HI, preferred_element_type=F32)
        t = t + jnp.dot(t, p, precision=HI, preferred_element_type=F32)
    eg = jnp.exp(gcb)
    u = jnp.dot(t, vb, precision=HI, preferred_element_type=F32)
    w = jnp.dot(t, kb * eg, precision=HI, preferred_element_type=F32)
    attn = lax.dot_general(q.astype(BF), kbf, NT, preferred_element_type=F32) * decay
    glast = gcb[c - 1:c, :]
    ks = k * jnp.exp(glast - gcb)
    sb = s_in.astype(BF)
    v_new = u - jnp.dot(w.astype(BF), sb, preferred_element_type=F32)
    o = (jnp.dot((q * eg).astype(BF), sb, preferred_element_type=F32)
         + jnp.dot(attn.astype(BF), v_new.astype(BF), preferred_element_type=F32))
    s_out = s_in * jnp.exp(glast[:, 0:1]) + lax.dot_general(ks.astype(BF), v_new.astype(BF), TN, preferred_element_type=F32)
    return o, s_out


GDN_ROWS = 512


def _delta_fwd(q, k, v, gcb, btb, name):
    s, dv = v.shape
    hv = dv // HEAD
    rep = dv // q.shape[1]
    r = _tile(s, GDN_ROWS, CHUNK)
    per = r // CHUNK
    nc = s // CHUNK

    def body(q_ref, k_ref, v_ref, g_ref, b_ref, o_ref, st_ref, state):
        @pl.when(pl.program_id(1) == 0)
        def _():
            state[...] = jnp.zeros_like(state)

        def chunk(cc, carry):
            rows = pl.ds(pl.multiple_of(cc * CHUNK, CHUNK), CHUNK)
            st_ref[0, cc] = state[...]
            o, s_out = _delta_chunk(q_ref[rows, :].astype(F32), k_ref[rows, :].astype(F32), v_ref[rows, :].astype(F32),
                                    g_ref[rows, :], b_ref[rows, :], state[...])
            o_ref[rows, :] = o.astype(BF)
            state[...] = s_out
            return carry

        lax.fori_loop(0, per, chunk, 0)

    kq = pl.BlockSpec((r, HEAD), lambda h, i: (i, h // rep))
    vs = pl.BlockSpec((r, HEAD), lambda h, i: (i, h))
    return pl.pallas_call(
        body, name=name, grid=(hv, s // r), in_specs=[kq, kq, vs, vs, vs],
        out_specs=[vs, pl.BlockSpec((1, per, HEAD, HEAD), lambda h, i: (h, i, 0, 0))],
        out_shape=[SDS((s, dv), BF), SDS((hv, nc, HEAD, HEAD), F32)],
        scratch_shapes=[pltpu.VMEM((HEAD, HEAD), F32)],
        compiler_params=_params("parallel", "arbitrary"))(q, k, v, gcb, btb)


def _delta_bwd(q, k, v, gcb, btb, states, do, name):
    s, dv = v.shape
    hv = dv // HEAD
    rep = dv // q.shape[1]
    r = _tile(s, GDN_ROWS, CHUNK)
    per = r // CHUNK
    nb = s // r

    def body(q_ref, k_ref, v_ref, g_ref, b_ref, st_ref, do_ref, dq_ref, dk_ref, dv_ref, dg_ref, db_ref, dstate):
        @pl.when(pl.program_id(1) == 0)
        def _():
            dstate[...] = jnp.zeros_like(dstate)

        def chunk(t, carry):
            cc = per - 1 - t
            rows = pl.ds(pl.multiple_of(cc * CHUNK, CHUNK), CHUNK)
            _, vjp = jax.vjp(_delta_chunk, q_ref[rows, :].astype(F32), k_ref[rows, :].astype(F32),
                             v_ref[rows, :].astype(F32), g_ref[rows, :], b_ref[rows, :], st_ref[0, cc])
            dq, dk, dvv, dg, db, ds_in = vjp((do_ref[rows, :].astype(F32), dstate[...]))
            dq_ref[rows, :] = dq
            dk_ref[rows, :] = dk
            dv_ref[rows, :] = dvv
            dg_ref[rows, :] = dg
            db_ref[rows, :] = db
            dstate[...] = ds_in
            return carry

        lax.fori_loop(0, per, chunk, 0)

    kq = pl.BlockSpec((r, HEAD), lambda h, i: (nb - 1 - i, h // rep))
    vs = pl.BlockSpec((r, HEAD), lambda h, i: (nb - 1 - i, h))
    return pl.pallas_call(
        body, name=name, grid=(hv, nb),
        in_specs=[kq, kq, vs, vs, vs, pl.BlockSpec((1, per, HEAD, HEAD), lambda h, i: (h, nb - 1 - i, 0, 0)), vs],
        out_specs=[vs] * 5, out_shape=[SDS((s, dv), F32)] * 5,
        scratch_shapes=[pltpu.VMEM((HEAD, HEAD), F32)],
        compiler_params=_params("parallel", "arbitrary"))(q, k, v, gcb, btb, states, do)


def _gdn_out_tile(o, gate, gain):
    return _headnorm(o, gain) * _silu(gate)


def _gdn_out_fwd(o, proj, gate_col0, gain, name):
    s, dv = o.shape
    r = _tile(s, 128, 16)

    def body(o_ref, g_ref, gain_ref, y_ref):
        y_ref[...] = _gdn_out_tile(o_ref[...].astype(F32), g_ref[...].astype(F32), gain_ref[...]).astype(BF)

    row = pl.BlockSpec((r, dv), lambda i: (i, 0))
    return pl.pallas_call(
        body, name=name, grid=(s // r,),
        in_specs=[row, pl.BlockSpec((r, dv), lambda i: (i, gate_col0)), pl.BlockSpec((1, HEAD), lambda i: (0, 0))],
        out_specs=row, out_shape=SDS((s, dv), BF), compiler_params=_params("parallel"))(o, proj, gain)


def _gdn_out_bwd(o, proj, gate_col0, gain, dy, name):
    s, dv = o.shape
    r = _tile(s, 128, 16)

    def body(o_ref, g_ref, gain_ref, dy_ref, do_ref, dg_ref, dgain_ref):
        i = pl.program_id(0)
        _, vjp = jax.vjp(_gdn_out_tile, o_ref[...].astype(F32), g_ref[...].astype(F32), gain_ref[...])
        do, dg, dgain = vjp(dy_ref[...].astype(F32))
        do_ref[...] = do
        dg_ref[...] = dg.astype(BF)

        @pl.when(i == 0)
        def _():
            dgain_ref[...] = dgain

        @pl.when(i > 0)
        def _():
            dgain_ref[...] += dgain

    row = pl.BlockSpec((r, dv), lambda i: (i, 0))
    vec = pl.BlockSpec((1, HEAD), lambda i: (0, 0))
    return pl.pallas_call(
        body, name=name, grid=(s // r,),
        in_specs=[row, pl.BlockSpec((r, dv), lambda i: (i, gate_col0)), vec, row],
        out_specs=[row, row, vec], out_shape=[SDS((s, dv), F32), SDS((s, dv), BF), SDS((1, HEAD), F32)],
        compiler_params=_params("arbitrary"))(o, proj, gain, dy)


def _ffn_forward(x, gain, wu, wg, cw, wd, tag):
    h = _rmsnorm_fwd(x, gain, f"{tag}_norm")
    uu = _mm(h, wu, name=f"{tag}_up_u")
    ug = _mm(h, wg, name=f"{tag}_up_g")
    a = _ffn_act_fwd(uu, ug, cw, f"{tag}_act")
    y = _mm(a, wd, res=x, out_dtype=F32, name=f"{tag}_down")
    return y, (x, h, uu, ug, a)


def _ffn_backward(saved, dy, gain, wu, wg, cw, wd, tag):
    x, h, uu, ug, a = saved
    da = _mm(dy, wd, tb=True, name=f"{tag}_d_act")
    dwd = _mm(a, dy, ta=True, out_dtype=F32, name=f"{tag}_d_wd")
    duu, dug, dcw = _ffn_act_bwd(uu, ug, cw, da, f"{tag}_act_bwd")
    dh = _mm(duu, wu, tb=True, out_dtype=F32, name=f"{tag}_d_h_u")
    dh = _mm(dug, wg, tb=True, res=dh, out_dtype=F32, name=f"{tag}_d_h_g")
    dwu = _mm(h, duu, ta=True, out_dtype=F32, name=f"{tag}_d_wu")
    dwg = _mm(h, dug, ta=True, out_dtype=F32, name=f"{tag}_d_wg")
    dx, dgain = _rmsnorm_bwd(x, gain, dh, dy, f"{tag}_norm_bwd")
    return dx, dict(gain=dgain, wu=dwu, wg=dwg, cw=dcw, wd=dwd)


def _att_forward(x, gain, p, tag):
    h = _rmsnorm_fwd(x, gain, f"{tag}_norm")
    qkv = _mm(h, p["wqkv"], name=f"{tag}_qkv")
    q, kpad, vpad = _qkv_post_fwd(qkv, p["qg"], p["kg"], f"{tag}_qknorm")
    bias = _bias_expand(p["rel"], f"{tag}_bias")
    o = _attn_fwd(q, kpad, vpad, bias, f"{tag}_core")
    y = _mm(o, p["wo"], res=x, out_dtype=F32, name=f"{tag}_out")
    return y, (x, h, qkv, q, kpad, vpad, bias, o)


def _att_backward(saved, dy, gain, p, tag):
    x, h, qkv, q, kpad, vpad, bias, o = saved
    do = _mm(dy, p["wo"], tb=True, name=f"{tag}_d_o")
    dwo = _mm(o, dy, ta=True, out_dtype=F32, name=f"{tag}_d_wo")
    dq, dkpad, dvpad, dbias = _attn_bwd(q, kpad, vpad, bias, do, f"{tag}_core_bwd")
    drel = _bias_reduce(dbias, f"{tag}_bias_bwd")
    dqkv, dqg, dkg = _qkv_post_bwd(qkv, p["qg"], p["kg"], dq, dkpad, dvpad, f"{tag}_qknorm_bwd")
    dh = _mm(dqkv, p["wqkv"], tb=True, out_dtype=F32, name=f"{tag}_d_h")
    dwqkv = _mm(h, dqkv, ta=True, out_dtype=F32, name=f"{tag}_d_wqkv")
    dx, dgain = _rmsnorm_bwd(x, gain, dh, dy, f"{tag}_norm_bwd")
    return dx, dict(gain=dgain, wqkv=dwqkv, qg=dqg, kg=dkg, rel=drel, wo=dwo)


def _gdn_forward(x, gain, p, tag):
    d = x.shape[1]
    nk = d // HEAD
    hv = 2 * nk
    h = _rmsnorm_fwd(x, gain, f"{tag}_norm")
    proj = _mm(h, p["wmain"], name=f"{tag}_proj")
    ab = _mm(h, p["wab"], out_dtype=F32, name=f"{tag}_proj_ab")
    q = _gdn_conv_fwd(proj, p["cw"], "q", 0, nk, f"{tag}_conv_q")
    k = _gdn_conv_fwd(proj, p["cw"], "k", nk, nk, f"{tag}_conv_k")
    v = _gdn_conv_fwd(proj, p["cw"], "v", 2 * nk, hv, f"{tag}_conv_v")
    gcb, btb = _gates_fwd(ab, p["a_log"], p["dt_bias"], hv, f"{tag}_gates")
    u, wd, attn, tinv = _delta_prep_fwd(q, k, v, gcb, btb, f"{tag}_delta_prep")
    o, states = _delta_scan_fwd(u, wd, attn, q, k, gcb, f"{tag}_delta_scan")
    og = _gdn_out_fwd(o, proj, 2, p["o_gain"], f"{tag}_onorm")
    y = _mm(og, p["wo"], res=x, out_dtype=F32, name=f"{tag}_out")
    return y, (x, h, proj, ab, q, k, v, gcb, btb, u, wd, attn, tinv, o, states, og)


def _gdn_backward(saved, dy, gain, p, tag):
    x, h, proj, ab, q, k, v, gcb, btb, u, wd, attn, tinv, o, states, og = saved
    d = x.shape[1]
    nk = d // HEAD
    hv = 2 * nk
    dog = _mm(dy, p["wo"], tb=True, name=f"{tag}_d_og")
    dwo = _mm(og, dy, ta=True, out_dtype=F32, name=f"{tag}_d_wo")
    do, dgate, dogain = _gdn_out_bwd(o, proj, 2, p["o_gain"], dog, f"{tag}_onorm_bwd")
    du, dw, dattn, dq_s, dk_s, dg_s = _delta_scan_bwd(u, wd, attn, q, k, gcb, states, do, f"{tag}_delta_scan_bwd")
    dq, dk, dv, dgcb, dbtb = _delta_prep_bwd(q, k, v, gcb, btb, tinv, du, dw, dattn, dq_s, dk_s, dg_s, f"{tag}_delta_prep_bwd")
    dab, dalog, ddt = _gates_bwd(ab, p["a_log"], p["dt_bias"], dgcb, dbtb, hv, f"{tag}_gates_bwd")
    dpq, dcq = _gdn_conv_bwd(proj, p["cw"], dq, "q", 0, nk, f"{tag}_conv_q_bwd")
    dpk, dck = _gdn_conv_bwd(proj, p["cw"], dk, "k", nk, nk, f"{tag}_conv_k_bwd")
    dpv, dcv = _gdn_conv_bwd(proj, p["cw"], dv, "v", 2 * nk, hv, f"{tag}_conv_v_bwd")
    dproj = jnp.concatenate([dpq, dpk, dpv, dgate], axis=1)
    dcw = jnp.concatenate([dcq, dck, dcv], axis=1)
    dh = _mm(dproj, p["wmain"], tb=True, out_dtype=F32, name=f"{tag}_d_h_main")
    dh = _mm(dab, p["wab"], tb=True, res=dh, out_dtype=F32, name=f"{tag}_d_h_ab")
    dwmain = _mm(h, dproj, ta=True, out_dtype=F32, name=f"{tag}_d_wmain")
    dwab = _mm(h, dab, ta=True, out_dtype=F32, name=f"{tag}_d_wab")
    dx, dgain = _rmsnorm_bwd(x, gain, dh, dy, f"{tag}_norm_bwd")
    return dx, dict(gain=dgain, wmain=dwmain, wab=dwab, cw=dcw, a_log=dalog, dt_bias=ddt, o_gain=dogain, wo=dwo)


def _resolve(entry, after):
    return entry(after) if callable(entry) else entry


def _local_step(x, target, w, sink=None):
    depth = len(w["ffn"])
    tape = []
    for i in range(depth):
        kind, j = i % 3, i // 3
        gain = w["mix_norm"][i:i + 1]
        if kind == 0:
            x, saved = _att_forward(x, gain, _resolve(w["att"][j], x), f"l{i}_att")
        elif kind == 1:
            x_in = x
            pw = _resolve(w["pool"][j], x)
            x = _pool_fwd(x_in, gain, pw["w"], pw["scale"], f"l{i}_pool")
            saved = x_in
        else:
            x, saved = _gdn_forward(x, gain, _resolve(w["gdn"][j], x), f"l{i}_gdn")
        f = _resolve(w["ffn"][i], x)
        x, fsaved = _ffn_forward(x, w["ffn_norm"][i:i + 1], f["wu"], f["wg"], f["cw"], f["wd"], f"l{i}_ffn")
        tape.append((saved, fsaved))
    dy, loss_row = _loss_head(x, target, "loss_head")
    grads = dict(mix=[None] * depth, ffn=[None] * depth)
    zero = 0.0
    for i in reversed(range(depth)):
        kind, j = i % 3, i // 3
        saved, fsaved = tape[i]
        f = _resolve(w["ffn"][i], dy)
        dy, grads["ffn"][i] = _ffn_backward(fsaved, dy, w["ffn_norm"][i:i + 1] + zero, f["wu"], f["wg"], f["cw"], f["wd"], f"l{i}_ffn")
        if sink is not None:
            zero = zero + sink("ffn", i, grads["ffn"][i])
        gain = w["mix_norm"][i:i + 1] + zero
        if kind == 0:
            dy, grads["mix"][i] = _att_backward(saved, dy, gain, _resolve(w["att"][j], dy), f"l{i}_att")
        elif kind == 1:
            pw = _resolve(w["pool"][j], dy)
            dy, dgain, dw4, dscale = _pool_bwd(saved, gain, pw["w"], pw["scale"], dy, f"l{i}_pool_bwd")
            grads["mix"][i] = dict(gain=dgain, w=dw4, scale=dscale)
        else:
            dy, grads["mix"][i] = _gdn_backward(saved, dy, gain, _resolve(w["gdn"][j], dy), f"l{i}_gdn")
        if sink is not None:
            zero = zero + sink(("att", "pool", "gdn")[kind], i, grads["mix"][i])
    return loss_row, dy, grads


MESH = pl.DeviceIdType.MESH
ANY = pl.BlockSpec(memory_space=pl.ANY)


def _position():
    return tuple(lax.axis_index(a) for a in AXES)


def _flip(pos, rel):
    return tuple(1 - p if (rel >> (2 - i)) & 1 else p for i, p in enumerate(pos))


def _index(pos):
    return 4 * pos[0] + 2 * pos[1] + pos[2]


def _all_gather(arr, name):
    def body(x_ref, o_ref, send, recv, local):
        me = _position()
        mine = pltpu.make_async_copy(x_ref, o_ref.at[_index(me)], local)
        mine.start()
        copies = []
        for rel in range(1, N_DEV):
            cp = pltpu.make_async_remote_copy(
                src_ref=x_ref, dst_ref=o_ref.at[_index(me)], send_sem=send.at[rel - 1], recv_sem=recv.at[rel - 1],
                device_id=_flip(me, rel), device_id_type=MESH)
            cp.start()
            copies.append(cp)
        for cp in copies:
            cp.wait()
        mine.wait()

    return pl.pallas_call(
        body, name=name, in_specs=[ANY], out_specs=ANY, out_shape=SDS((N_DEV,) + arr.shape, arr.dtype),
        scratch_shapes=[pltpu.SemaphoreType.DMA((N_DEV - 1,)), pltpu.SemaphoreType.DMA((N_DEV - 1,)),
                        pltpu.SemaphoreType.DMA(())])(arr)


def _exchange(arr, name):
    def body(x_ref, o_ref, send, recv, local):
        me = _position()
        mine = pltpu.make_async_copy(x_ref.at[_index(me)], o_ref.at[_index(me)], local)
        mine.start()
        copies = []
        for rel in range(1, N_DEV):
            peer = _flip(me, rel)
            cp = pltpu.make_async_remote_copy(
                src_ref=x_ref.at[_index(peer)], dst_ref=o_ref.at[_index(me)], send_sem=send.at[rel - 1],
                recv_sem=recv.at[rel - 1], device_id=peer, device_id_type=MESH)
            cp.start()
            copies.append(cp)
        for cp in copies:
            cp.wait()
        mine.wait()

    return pl.pallas_call(
        body, name=name, in_specs=[ANY], out_specs=ANY, out_shape=SDS(arr.shape, arr.dtype),
        scratch_shapes=[pltpu.SemaphoreType.DMA((N_DEV - 1,)), pltpu.SemaphoreType.DMA((N_DEV - 1,)),
                        pltpu.SemaphoreType.DMA(())])(arr)


HBM = pl.BlockSpec(memory_space=pltpu.HBM)
SEM = pl.BlockSpec(memory_space=pltpu.SEMAPHORE)
EFFECT = pltpu.SideEffectType.DATAFLOW_SIDE_EFFECTING


def _split_copies(x_ref, land_ref, send, recv, scatter):
    me = _position()
    copies = []
    for rel in range(1, N_DEV):
        peer = _flip(me, rel)
        copies.append(pltpu.make_async_remote_copy(
            src_ref=x_ref.at[_index(peer)] if scatter else x_ref, dst_ref=land_ref.at[_index(me)],
            send_sem=send.at[rel - 1], recv_sem=recv.at[rel - 1], device_id=peer, device_id_type=MESH))
    return copies


def _copies_start(arr, scatter, name):
    shape = arr.shape if scatter else (N_DEV,) + arr.shape

    def body(x_ref, land_ref, send, recv, x_thru, land_thru, token):
        for cp in _split_copies(x_ref, land_ref, send, recv, scatter):
            cp.start()
        token[...] = jnp.zeros_like(token)

    sems = pltpu.SemaphoreType.DMA((N_DEV - 1,))
    send, recv, x_thru, land_thru, token = pl.pallas_call(
        body, name=name,
        out_shape=(sems, sems, pltpu.HBM(arr.shape, arr.dtype), pltpu.HBM(shape, arr.dtype), SDS((8, LANE), F32)),
        in_specs=(HBM, HBM), out_specs=(SEM, SEM, HBM, HBM, pl.BlockSpec(memory_space=pltpu.VMEM)),
        input_output_aliases={0: 2, 1: 3}, compiler_params=pltpu.CompilerParams(has_side_effects=EFFECT),
    )(pltpu.with_memory_space_constraint(arr, pltpu.HBM), pltpu.with_memory_space_constraint(lax.empty(shape, arr.dtype), pltpu.HBM))
    return (send, recv, x_thru, land_thru, scatter, name), token[0, 0]


def _copies_wait(handle, after):
    send, recv, x_thru, land_thru, scatter, name = handle

    def body(x_ref, land_ref, send_ref, recv_ref, after_ref, x_dead, got_ref):
        for cp in _split_copies(x_ref, land_ref, send_ref, recv_ref, scatter):
            cp.wait_send()
            cp.wait_recv()

    return pl.pallas_call(
        body, name=name + "_wait",
        out_shape=(pltpu.HBM(x_thru.shape, x_thru.dtype), pltpu.HBM(land_thru.shape, land_thru.dtype)),
        in_specs=(HBM, HBM, SEM, SEM, ANY), out_specs=(HBM, HBM), input_output_aliases={0: 0, 1: 1},
        compiler_params=pltpu.CompilerParams(has_side_effects=EFFECT),
    )(x_thru, land_thru, send, recv, after)[1]


def _with_own(got, own):
    return lax.dynamic_update_index_in_dim(got, own.astype(got.dtype), _index(_position()), 0)


def _adamw(parts, w, m, v, name):
    if not isinstance(parts, (list, tuple)):
        parts = [parts]
    layers = len(parts)
    r, c = parts[0].shape[1:]
    assert w.shape == (layers * r, c), (w.shape, parts[0].shape, layers)
    tr = _tile(r, 128, 16)
    per = r // tr
    c1 = 1.0 / (1.0 - ADAM_B1 ** ADAM_STEP)
    c2 = 1.0 / (1.0 - ADAM_B2 ** ADAM_STEP)

    def body(*refs):
        p_refs = refs[:layers]
        w_ref, m_ref, v_ref, g_ref, d_ref, nm_ref, nv_ref = refs[layers:]

        def update(p_ref):
            g = p_ref[0].astype(F32)
            for s in range(1, N_DEV):
                g = g + p_ref[s].astype(F32)
            nm = ADAM_B1 * m_ref[...] + (1.0 - ADAM_B1) * g
            nv = ADAM_B2 * v_ref[...] + (1.0 - ADAM_B2) * (g * g)
            g_ref[...] = g
            nm_ref[...] = nm
            nv_ref[...] = nv
            d_ref[...] = -ADAM_LR * ((nm * c1) / (jnp.sqrt(nv * c2) + ADAM_EPS) + ADAM_WD * w_ref[...])

        if layers == 1:
            update(p_refs[0])
        else:
            for j in range(layers):
                pl.when(pl.program_id(0) == j)(functools.partial(update, p_refs[j]))

    p_specs = [pl.BlockSpec((N_DEV, tr, c), functools.partial(lambda l, i, j: (0, jnp.where(l == j, i, 0), 0), j=j))
               for j in range(layers)]
    row = pl.BlockSpec((tr, c), lambda l, i: (l * per + i, 0))
    return pl.pallas_call(
        body, name=name, grid=(layers, per), in_specs=p_specs + [row, row, row],
        out_specs=[row] * 4, out_shape=[SDS(w.shape, F32)] * 4, compiler_params=_params("arbitrary", "arbitrary"))(*parts, w, m, v)


PACK = 8 * LANE


def _pack(arrs):
    flat = []
    for a in arrs:
        a = a.reshape(-1).astype(F32)
        flat.append(jnp.pad(a, (0, (-a.shape[0]) % PACK)))
    return jnp.concatenate(flat).reshape(-1, LANE)


def _unpack(packed, shapes, lead=()):
    flat = packed.reshape(lead + (-1,))
    out, off = [], 0
    for shp in shapes:
        n = int(np.prod(shp))
        out.append(flat[..., off:off + n].reshape(lead + tuple(shp)))
        off += n + (-n) % PACK
    return out


def _pad_to(a, axis, size):
    pad = [(0, 0)] * a.ndim
    pad[axis] = (0, size - a.shape[axis])
    return jnp.pad(a, pad)


def _cols_from_shards(g):
    return jnp.transpose(g, (1, 0, 2)).reshape(g.shape[1], -1)


def _cols_to_shards(a):
    c = a.shape[-1] // N_DEV
    a = a.reshape(a.shape[:-1] + (N_DEV, c))
    return jnp.moveaxis(a, -2, 0)


def _rows_to_shards(a):
    r = a.shape[-2] // N_DEV
    a = a.reshape(a.shape[:-2] + (N_DEV, r, a.shape[-1]))
    return jnp.moveaxis(a, -3, 0)


WEIGHTS = ("mix_norm", "ffn_norm", "att_w_qkv", "att_q_gain", "att_k_gain", "att_rel_bias", "att_w_o", "pool_w",
           "pool_scale", "gdn_w_in", "gdn_conv", "gdn_a_log", "gdn_dt_bias", "gdn_o_gain", "gdn_w_o", "ffn_w_up",
           "ffn_conv", "ffn_w_down")
REPLICATED = ("mix_norm", "ffn_norm", "att_q_gain", "att_k_gain", "pool_scale", "gdn_a_log", "gdn_dt_bias", "gdn_o_gain")
SMALL_SHARDED = ("att_rel_bias", "gdn_conv", "ffn_conv")
BIG = ("att_w_qkv", "att_w_o", "pool_w", "gdn_w_in", "gdn_w_o", "ffn_w_up", "ffn_w_down")
KEEP_F32 = ("pool_w",)


def _memo(build):
    cache = []

    def entry(after):
        if not cache:
            cache.append(build(after))
        return cache[0]

    return entry


def _assemble_weights(w, get, small, f):
    d = w["mix_norm"].shape[1]
    nk = d // HEAD
    hv = 2 * nk
    fp = -(-f // FF_PAD) * FF_PAD
    out = dict(mix_norm=w["mix_norm"], ffn_norm=w["ffn_norm"], att=[], pool=[], gdn=[], ffn=[])

    def att(j, after):
        return dict(wqkv=_cols_from_shards(get("att_w_qkv", j, after)), wo=get("att_w_o", j, after).reshape(d, d),
                    qg=w["att_q_gain"][j:j + 1], kg=w["att_k_gain"][j:j + 1], rel=small["att_rel_bias"][j])

    def pool(j, after):
        g = get("pool_w", j, after)
        return dict(w=jnp.transpose(g, (1, 0, 2, 3)).reshape(g.shape[1], g.shape[3], g.shape[3]),
                    scale=w["pool_scale"][j:j + 1])

    def gdn(j, after):
        win = _cols_from_shards(get("gdn_w_in", j, after))
        nm = 6 * d
        wab = jnp.concatenate([_pad_to(win[:, nm:nm + hv], 1, LANE), _pad_to(win[:, nm + hv:], 1, LANE)], axis=1)
        return dict(wmain=win[:, :nm], wab=wab, cw=_pad_to(small["gdn_conv"][j], 0, 8),
                    a_log=_pad_to(w["gdn_a_log"][j:j + 1], 1, LANE), dt_bias=_pad_to(w["gdn_dt_bias"][j:j + 1], 1, LANE),
                    o_gain=w["gdn_o_gain"][j:j + 1], wo=get("gdn_w_o", j, after).reshape(2 * d, d))

    def ffn(i, after):
        wup = _cols_from_shards(get("ffn_w_up", i, after))
        return dict(wu=_pad_to(wup[:, :f], 1, fp), wg=_pad_to(wup[:, f:], 1, fp),
                    cw=_pad_to(_pad_to(small["ffn_conv"][i], 0, 8), 1, fp),
                    wd=_pad_to(get("ffn_w_down", i, after).reshape(f, d), 0, fp))

    for key, build, count in (("att", att, w["att_w_qkv"].shape[0]), ("pool", pool, w["pool_w"].shape[0]),
                              ("gdn", gdn, w["gdn_w_in"].shape[0]), ("ffn", ffn, w["ffn_w_up"].shape[0])):
        out[key] = [_memo(functools.partial(build, j)) for j in range(count)]
    return out


def _full_gradients(grads, w, f):
    d = w["mix_norm"].shape[1]
    nk = d // HEAD
    hv = 2 * nk
    depth = len(grads["ffn"])
    att = [grads["mix"][i] for i in range(depth) if i % 3 == 0]
    pool = [grads["mix"][i] for i in range(depth) if i % 3 == 1]
    gdn = [grads["mix"][i] for i in range(depth) if i % 3 == 2]
    ffn = grads["ffn"]
    win = [jnp.concatenate([g["wmain"], g["wab"][:, :hv], g["wab"][:, LANE:LANE + hv]], axis=1) for g in gdn]
    return dict(
        mix_norm=jnp.concatenate([g["gain"] for g in grads["mix"]], axis=0),
        ffn_norm=jnp.concatenate([g["gain"] for g in ffn], axis=0),
        att_w_qkv=jnp.stack([g["wqkv"] for g in att]),
        att_q_gain=jnp.concatenate([g["qg"] for g in att], axis=0),
        att_k_gain=jnp.concatenate([g["kg"] for g in att], axis=0),
        att_rel_bias=jnp.stack([g["rel"] for g in att]),
        att_w_o=jnp.stack([g["wo"] for g in att]),
        pool_w=jnp.stack([g["w"] for g in pool]),
        pool_scale=jnp.concatenate([g["scale"] for g in pool], axis=0),
        gdn_w_in=jnp.stack(win),
        gdn_conv=jnp.stack([g["cw"][:GDN_CONV] for g in gdn]),
        gdn_a_log=jnp.concatenate([g["a_log"][:, :hv] for g in gdn], axis=0),
        gdn_dt_bias=jnp.concatenate([g["dt_bias"][:, :hv] for g in gdn], axis=0),
        gdn_o_gain=jnp.concatenate([g["o_gain"] for g in gdn], axis=0),
        gdn_w_o=jnp.stack([g["wo"] for g in gdn]),
        ffn_w_up=jnp.stack([jnp.concatenate([g["wu"][:, :f], g["wg"][:, :f]], axis=1) for g in ffn]),
        ffn_conv=jnp.stack([g["cw"][:FFN_CONV, :f] for g in ffn]),
        ffn_w_down=jnp.stack([g["wd"][:f] for g in ffn]))


ROW_SHARDED = ("att_w_o", "gdn_w_o", "ffn_w_down")


def _to_shards(name, full):
    if name == "pool_w":
        r = full.shape[2] // N_DEV
        a = full.reshape(full.shape[:2] + (N_DEV, r, full.shape[3]))
        return jnp.moveaxis(a, 2, 0)
    return _rows_to_shards(full) if name in ROW_SHARDED else _cols_to_shards(full)


def kernel(x, mix_norm, ffn_norm, att_w_qkv, att_q_gain, att_k_gain, att_rel_bias, att_w_o, pool_w, pool_scale, gdn_w_in, gdn_conv, gdn_a_log, gdn_dt_bias, gdn_o_gain, gdn_w_o, ffn_w_up, ffn_conv, ffn_w_down, loss_target, m_mix_norm, m_ffn_norm, m_att_w_qkv, m_att_q_gain, m_att_k_gain, m_att_rel_bias, m_att_w_o, m_pool_w, m_pool_scale, m_gdn_w_in, m_gdn_conv, m_gdn_a_log, m_gdn_dt_bias, m_gdn_o_gain, m_gdn_w_o, m_ffn_w_up, m_ffn_conv, m_ffn_w_down, v_mix_norm, v_ffn_norm, v_att_w_qkv, v_att_q_gain, v_att_k_gain, v_att_rel_bias, v_att_w_o, v_pool_w, v_pool_scale, v_gdn_w_in, v_gdn_conv, v_gdn_a_log, v_gdn_dt_bias, v_gdn_o_gain, v_gdn_w_o, v_ffn_w_up, v_ffn_conv, v_ffn_w_down):
    w = dict(zip(WEIGHTS, (mix_norm, ffn_norm, att_w_qkv, att_q_gain, att_k_gain, att_rel_bias, att_w_o, pool_w, pool_scale, gdn_w_in, gdn_conv, gdn_a_log, gdn_dt_bias, gdn_o_gain, gdn_w_o, ffn_w_up, ffn_conv, ffn_w_down)))
    m = dict(zip(WEIGHTS, (m_mix_norm, m_ffn_norm, m_att_w_qkv, m_att_q_gain, m_att_k_gain, m_att_rel_bias, m_att_w_o, m_pool_w, m_pool_scale, m_gdn_w_in, m_gdn_conv, m_gdn_a_log, m_gdn_dt_bias, m_gdn_o_gain, m_gdn_w_o, m_ffn_w_up, m_ffn_conv, m_ffn_w_down)))
    v = dict(zip(WEIGHTS, (v_mix_norm, v_ffn_norm, v_att_w_qkv, v_att_q_gain, v_att_k_gain, v_att_rel_bias, v_att_w_o, v_pool_w, v_pool_scale, v_gdn_w_in, v_gdn_conv, v_gdn_a_log, v_gdn_dt_bias, v_gdn_o_gain, v_gdn_w_o, v_ffn_w_up, v_ffn_conv, v_ffn_w_down)))

    me = _index(_position())
    d = mix_norm.shape[1]
    hv = 2 * (d // HEAD)
    f = ffn_w_down.shape[1] * N_DEV
    depth = ffn_w_up.shape[0]

    small_shapes = [w[n].shape for n in SMALL_SHARDED]
    small_g = _all_gather(_pack([w[n] for n in SMALL_SHARDED]), "gather_small")
    small = {}
    for n, a in zip(SMALL_SHARDED, _unpack(small_g, small_shapes, lead=(N_DEV,))):
        small[n] = jnp.moveaxis(a, 0, -2).reshape(a.shape[1:-1] + (N_DEV * a.shape[-1],))
    order = []
    for i in range(depth):
        order += [[("att_w_qkv", i // 3), ("att_w_o", i // 3)], [("pool_w", i // 3)], [("gdn_w_in", i // 3), ("gdn_w_o", i // 3)]][i % 3]
        order += [("ffn_w_up", i), ("ffn_w_down", i)]
    local = {(n, j): (w[n][j] if n in KEEP_F32 else w[n][j].astype(BF)) for n, j in order}
    arriving, zero = {}, 0.0
    for n, j in order:
        arriving[(n, j)], tok = _copies_start(local[(n, j)], False, f"gather_{n}_{j}")
        zero = zero + tok

    def get(n, j, after):
        return _with_own(_copies_wait(arriving[(n, j)], after), local[(n, j)])

    ordered = dict(w, mix_norm=mix_norm + zero, ffn_norm=ffn_norm + zero)
    full = _assemble_weights(ordered, get, small, f)

    leaving = {}

    def sink(kind, i, g):
        j = i // 3
        if kind == "ffn":
            pieces = [("ffn_w_up", i, _cols_to_shards(jnp.concatenate([g["wu"][:, :f], g["wg"][:, :f]], axis=1))),
                      ("ffn_w_down", i, _rows_to_shards(g["wd"][:f]))]
        elif kind == "att":
            pieces = [("att_w_qkv", j, _cols_to_shards(g["wqkv"])), ("att_w_o", j, _rows_to_shards(g["wo"]))]
        elif kind == "pool":
            pieces = [("pool_w", j, _to_shards("pool_w", g["w"][None])[:, 0])]
        else:
            win = jnp.concatenate([g["wmain"], g["wab"][:, :hv], g["wab"][:, LANE:LANE + hv]], axis=1)
            pieces = [("gdn_w_in", j, _cols_to_shards(win)), ("gdn_w_o", j, _rows_to_shards(g["wo"]))]
        tok = 0.0
        for n, l, shards in pieces:
            shards = shards if n in KEEP_F32 else shards.astype(BF)
            handle, t = _copies_start(shards, True, f"exchange_{n}_{l}")
            leaving[(n, l)] = (handle, lax.dynamic_index_in_dim(shards, me, 0, keepdims=False))
            tok = tok + t
        return tok

    loss_row, dx, grads = _local_step(x[0], loss_target[0], full, sink)
    loss = lax.psum(loss_row[0, 0], AXES)
    gfull = _full_gradients(grads, w, f)

    out = {}
    for n in BIG:
        c = w[n].shape[-1]
        parts = []
        for l in range(w[n].shape[0]):
            handle, own = leaving[(n, l)]
            parts.append(_with_own(_copies_wait(handle, dx), own).reshape(N_DEV, -1, c))
        res = _adamw(parts, w[n].reshape(-1, c), m[n].reshape(-1, c), v[n].reshape(-1, c), f"adamw_{n}")
        out[n] = [a.reshape(w[n].shape) for a in res]
    sparts = _exchange(jnp.stack([_pack([_to_shards(n, gfull[n])[k] for n in SMALL_SHARDED]) for k in range(N_DEV)]),
                       "exchange_small")
    res = _adamw(sparts, *[_pack([t[n] for n in SMALL_SHARDED]) for t in (w, m, v)], "adamw_small")
    for n, *vals in zip(SMALL_SHARDED, *[_unpack(a, small_shapes) for a in res]):
        out[n] = vals
    rep_shapes = [w[n].shape for n in REPLICATED]
    rparts = _all_gather(_pack([gfull[n] for n in REPLICATED]), "gather_replicated_grads")
    res = _adamw(rparts, *[_pack([t[n] for n in REPLICATED]) for t in (w, m, v)], "adamw_replicated")
    for n, *vals in zip(REPLICATED, *[_unpack(a, rep_shapes) for a in res]):
        out[n] = vals
    return (loss, dx[None], *[out[n][0] for n in WEIGHTS], *[out[n][1] for n in WEIGHTS],
            *[out[n][2] for n in WEIGHTS], *[out[n][3] for n in WEIGHTS])
```

```python
import functools

import numpy as np
import jax
import jax.numpy as jnp
from jax import lax
from jax.experimental import pallas as pl
from jax.experimental.pallas import tpu as pltpu

F32 = jnp.float32
BF = jnp.bfloat16
SDS = jax.ShapeDtypeStruct

EPS = 1e-6
MASK_VALUE = -1e30
CHUNK = 64
LEFT_CHUNKS = 8
BAND_LEFT = LEFT_CHUNKS * CHUNK
BAND = BAND_LEFT + CHUNK
MAX_REL = 256
NUM_REL = (CHUNK - 1) + MAX_REL + 1
HEAD = 128
LANE = 128
HALO = 16
POOL_WINDOWS = (2, 4, 8, 16)
GDN_CONV = 4
FFN_CONV = 3
FF_PAD = 512
N_DEV = 8
AXES = ("x", "y", "c")
VMEM_LIMIT = 56 * 1024 * 1024

ADAM_LR = 0.001
ADAM_B1 = 0.9
ADAM_B2 = 0.999
ADAM_EPS = 1e-08
ADAM_WD = 0.01
ADAM_STEP = 10

HI = lax.Precision.HIGHEST
NT = (((1,), (1,)), ((), ()))
TN = (((0,), (0,)), ((), ()))


def _tile(n, target, mult=LANE):
    if n <= target:
        return n
    t = (target // mult) * mult
    while t >= mult:
        if n % t == 0:
            return t
        t -= mult
    return n


def _params(*sem):
    return pltpu.CompilerParams(dimension_semantics=sem, vmem_limit_bytes=VMEM_LIMIT)


def _silu(x):
    return x / (1.0 + jnp.exp(-x))


@functools.partial(jax.custom_vjp, nondiff_argnums=(1,))
def _shift(x, k):
    return pltpu.roll(x, k % x.shape[0], axis=0)


def _shift_fwd(x, k):
    return _shift(x, k), None


def _shift_bwd(k, _, g):
    return (pltpu.roll(g, (-k) % g.shape[0], axis=0),)


_shift.defvjp(_shift_fwd, _shift_bwd)


def _mm(a, b, *, name, ta=False, tb=False, out_dtype=BF, res=None, tm=1024, tn=1024, tk=2048):
    m, k = (a.shape[1], a.shape[0]) if ta else a.shape
    n, kb = (b.shape[0], b.shape[1]) if tb else (b.shape[1], b.shape[0])
    assert k == kb, (a.shape, b.shape, ta, tb)
    if ta or a.dtype != BF:
        tk = min(tk, 1024)
    tm, tn, tk = _tile(m, tm), _tile(n, tn), _tile(k, tk)
    nk = k // tk
    dims = (((0 if ta else 1,), (1 if tb else 0,)), ((), ()))

    def body(*refs):
        if res is None:
            a_ref, b_ref, o_ref, acc = refs
        else:
            a_ref, b_ref, r_ref, o_ref, acc = refs
        prod = lax.dot_general(a_ref[...].astype(BF), b_ref[...].astype(BF), dims, preferred_element_type=F32)
        if nk == 1:
            if res is not None:
                prod = prod + r_ref[...].astype(F32)
            o_ref[...] = prod.astype(out_dtype)
            return
        kk = pl.program_id(2)

        @pl.when(kk == 0)
        def _():
            acc[...] = prod

        @pl.when(kk > 0)
        def _():
            acc[...] += prod

        @pl.when(kk == nk - 1)
        def _():
            r = acc[...]
            if res is not None:
                r = r + r_ref[...].astype(F32)
            o_ref[...] = r.astype(out_dtype)

    a_spec = pl.BlockSpec((tk, tm), lambda i, j, q: (q, i)) if ta else pl.BlockSpec((tm, tk), lambda i, j, q: (i, q))
    b_spec = pl.BlockSpec((tn, tk), lambda i, j, q: (j, q)) if tb else pl.BlockSpec((tk, tn), lambda i, j, q: (q, j))
    o_spec = pl.BlockSpec((tm, tn), lambda i, j, q: (i, j))
    ins, specs = [a, b], [a_spec, b_spec]
    if res is not None:
        ins.append(res)
        specs.append(o_spec)
    return pl.pallas_call(
        body, name=name, grid=(m // tm, n // tn, nk), in_specs=specs, out_specs=o_spec,
        out_shape=SDS((m, n), out_dtype), scratch_shapes=[pltpu.VMEM((tm, tn), F32)],
        compiler_params=_params("parallel", "parallel", "arbitrary"))(*ins)


def _rms(x, gain):
    return x * lax.rsqrt(jnp.mean(x * x, axis=-1, keepdims=True) + EPS) * gain


def _rmsnorm_fwd(x, gain, name):
    s, d = x.shape
    ts = _tile(s, 256, 8)

    def body(x_ref, g_ref, o_ref):
        o_ref[...] = _rms(x_ref[...], g_ref[...]).astype(BF)

    return pl.pallas_call(
        body, name=name, grid=(s // ts,),
        in_specs=[pl.BlockSpec((ts, d), lambda i: (i, 0)), pl.BlockSpec((1, d), lambda i: (0, 0))],
        out_specs=pl.BlockSpec((ts, d), lambda i: (i, 0)), out_shape=SDS((s, d), BF),
        compiler_params=_params("parallel"))(x, gain)


def _rmsnorm_bwd(x, gain, dh, dres, name):
    s, d = x.shape
    ts = _tile(s, 256, 8)

    def body(x_ref, g_ref, dh_ref, dr_ref, dx_ref, dg_ref):
        i = pl.program_id(0)
        _, vjp = jax.vjp(_rms, x_ref[...], g_ref[...])
        dx, dg = vjp(dh_ref[...].astype(F32))
        dx_ref[...] = dr_ref[...] + dx

        @pl.when(i == 0)
        def _():
            dg_ref[...] = dg

        @pl.when(i > 0)
        def _():
            dg_ref[...] += dg

    row = pl.BlockSpec((ts, d), lambda i: (i, 0))
    vec = pl.BlockSpec((1, d), lambda i: (0, 0))
    return pl.pallas_call(
        body, name=name, grid=(s // ts,), in_specs=[row, vec, row, row], out_specs=[row, vec],
        out_shape=[SDS((s, d), F32), SDS((1, d), F32)], compiler_params=_params("arbitrary"))(x, gain, dh, dres)


def _ffn_act_tile(u_ext, g_ext, cw):
    acc = u_ext * cw[2:3] + _shift(u_ext, 1) * cw[1:2] + _shift(u_ext, 2) * cw[0:1]
    return _silu(acc) * g_ext


def _halo_index(rows_per_block):
    per = rows_per_block // HALO
    return lambda rb: jnp.maximum(rb * per - 1, 0)


def _ffn_act_fwd(u, g, cw, name):
    s, f = u.shape
    r, tc = _tile(s, 512, HALO), _tile(f, 512)
    hidx = _halo_index(r)

    def body(uc, uh, gc, cw_ref, o_ref):
        rb = pl.program_id(1)
        keep = jnp.where(rb == 0, 0.0, 1.0)
        u_ext = jnp.concatenate([uh[...].astype(F32) * keep, uc[...].astype(F32)], axis=0)
        g_ext = jnp.concatenate([jnp.zeros((HALO, tc), F32), gc[...].astype(F32)], axis=0)
        o_ref[...] = _ffn_act_tile(u_ext, g_ext, cw_ref[...])[HALO:].astype(BF)

    cur = pl.BlockSpec((r, tc), lambda j, rb: (rb, j))
    return pl.pallas_call(
        body, name=name, grid=(f // tc, s // r),
        in_specs=[cur, pl.BlockSpec((HALO, tc), lambda j, rb: (hidx(rb), j)), cur,
                  pl.BlockSpec((8, tc), lambda j, rb: (0, j))],
        out_specs=cur, out_shape=SDS((s, f), BF), compiler_params=_params("parallel", "parallel"))(u, u, g, cw)


def _ffn_act_bwd(u, g, cw, da, name):
    s, f = u.shape
    r, tc = _tile(s, 512, HALO), _tile(f, 512)
    nb = s // r
    hidx = _halo_index(r)

    def body(uc, uh, gc, da_ref, cw_ref, du_ref, dg_ref, dcw_ref, carry):
        step = pl.program_id(1)
        rb = nb - 1 - step
        keep = jnp.where(rb == 0, 0.0, 1.0)
        zero = jnp.zeros((HALO, tc), F32)
        u_ext = jnp.concatenate([uh[...].astype(F32) * keep, uc[...].astype(F32)], axis=0)
        g_ext = jnp.concatenate([zero, gc[...].astype(F32)], axis=0)
        ct = jnp.concatenate([zero, da_ref[...].astype(F32)], axis=0)
        _, vjp = jax.vjp(_ffn_act_tile, u_ext, g_ext, cw_ref[...])
        du_ext, dg_ext, dcw = vjp(ct)

        @pl.when(step == 0)
        def _():
            carry[...] = zero
            dcw_ref[...] = jnp.zeros_like(dcw_ref)

        du_ref[pl.ds(0, r - HALO), :] = du_ext[HALO:r].astype(BF)
        du_ref[pl.ds(r - HALO, HALO), :] = (du_ext[r:] + carry[...]).astype(BF)
        carry[...] = du_ext[:HALO]
        dg_ref[...] = dg_ext[HALO:].astype(BF)
        dcw_ref[...] += dcw

    cur = pl.BlockSpec((r, tc), lambda j, t: (nb - 1 - t, j))
    wspec = pl.BlockSpec((8, tc), lambda j, t: (0, j))
    return pl.pallas_call(
        body, name=name, grid=(f // tc, nb),
        in_specs=[cur, pl.BlockSpec((HALO, tc), lambda j, t: (hidx(nb - 1 - t), j)), cur, cur, wspec],
        out_specs=[cur, cur, wspec], out_shape=[SDS((s, f), BF), SDS((s, f), BF), SDS((8, f), F32)],
        scratch_shapes=[pltpu.VMEM((HALO, tc), F32)],
        compiler_params=_params("parallel", "arbitrary"))(u, u, g, da, cw)


def _loss_head(y, target, name):
    s, d = y.shape
    ts = _tile(s, 256, 8)

    def body(y_ref, t_ref, dy_ref, l_ref):
        i = pl.program_id(0)
        err = y_ref[...] - t_ref[...]
        dy_ref[...] = err * (1.0 / d)
        part = jnp.zeros((1, LANE), F32) + 0.5 * jnp.sum(jnp.sum(err * err, axis=1, keepdims=True), axis=0, keepdims=True) / d

        @pl.when(i == 0)
        def _():
            l_ref[...] = part

        @pl.when(i > 0)
        def _():
            l_ref[...] += part

    row = pl.BlockSpec((ts, d), lambda i: (i, 0))
    return pl.pallas_call(
        body, name=name, grid=(s // ts,), in_specs=[row, row],
        out_specs=[row, pl.BlockSpec((1, LANE), lambda i: (0, 0))],
        out_shape=[SDS((s, d), F32), SDS((1, LANE), F32)], compiler_params=_params("arbitrary"))(y, target)


def _rel_index():
    rel = BAND_LEFT + np.arange(CHUNK)[:, None] - np.arange(BAND)[None, :]
    return (np.clip(rel, -(CHUNK - 1), MAX_REL) + (CHUNK - 1)).reshape(1, CHUNK * BAND).astype(np.int32)


def _onehot(idx_row):
    rows = lax.broadcasted_iota(jnp.int32, (NUM_REL, idx_row.shape[1]), 0)
    return jnp.where(rows == idx_row, 1.0, 0.0).astype(F32)


def _bias_expand(rel_bias, name):
    h = rel_bias.shape[0]
    n = CHUNK * BAND
    tn = n // 8

    def body(rb_ref, idx_ref, o_ref):
        o_ref[...] = jnp.dot(rb_ref[...], _onehot(idx_ref[...]), precision=HI, preferred_element_type=F32)

    out = pl.pallas_call(
        body, name=name, grid=(n // tn,),
        in_specs=[pl.BlockSpec((h, NUM_REL), lambda j: (0, 0)), pl.BlockSpec((1, tn), lambda j: (0, j))],
        out_specs=pl.BlockSpec((h, tn), lambda j: (0, j)), out_shape=SDS((h, n), F32),
        compiler_params=_params("parallel"))(rel_bias, jnp.asarray(_rel_index()))
    return out.reshape(h, CHUNK, BAND)


def _bias_reduce(dbias, name):
    h = dbias.shape[0]
    n = CHUNK * BAND
    tn = n // 8

    def body(db_ref, idx_ref, o_ref):
        j = pl.program_id(0)
        part = lax.dot_general(db_ref[...], _onehot(idx_ref[...]), NT, precision=HI, preferred_element_type=F32)

        @pl.when(j == 0)
        def _():
            o_ref[...] = part

        @pl.when(j > 0)
        def _():
            o_ref[...] += part

    return pl.pallas_call(
        body, name=name, grid=(n // tn,),
        in_specs=[pl.BlockSpec((h, tn), lambda j: (0, j)), pl.BlockSpec((1, tn), lambda j: (0, j))],
        out_specs=pl.BlockSpec((h, NUM_REL), lambda j: (0, 0)), out_shape=SDS((h, NUM_REL), F32),
        compiler_params=_params("arbitrary"))(dbias.reshape(h, n), jnp.asarray(_rel_index()))


def _headnorm(x, gain):
    outs = []
    for hh in range(x.shape[1] // HEAD):
        xh = x[:, hh * HEAD:(hh + 1) * HEAD]
        outs.append(xh * lax.rsqrt(jnp.mean(xh * xh, axis=-1, keepdims=True) + EPS) * gain)
    return jnp.concatenate(outs, axis=1)


def _qkv_post_fwd(qkv, qg, kg, name):
    s, d3 = qkv.shape
    d = d3 // 3
    r = BAND_LEFT
    nb = s // r

    def body(x_ref, qg_ref, kg_ref, q_ref, k_ref, v_ref):
        i = pl.program_id(0)
        keep = jnp.where(i == 0, 0.0, 1.0)
        q_ref[...] = _headnorm(x_ref[:, 0:d].astype(F32), qg_ref[...]).astype(BF)
        k_ref[...] = (_headnorm(x_ref[:, d:2 * d].astype(F32), kg_ref[...]) * keep).astype(BF)
        v_ref[...] = (x_ref[:, 2 * d:].astype(F32) * keep).astype(BF)

    prev = lambda i: (jnp.maximum(i - 1, 0), 0)
    vec = pl.BlockSpec((1, HEAD), lambda i: (0, 0))
    return pl.pallas_call(
        body, name=name, grid=(nb + 1,),
        in_specs=[pl.BlockSpec((r, d3), prev), vec, vec],
        out_specs=[pl.BlockSpec((r, d), prev), pl.BlockSpec((r, d), lambda i: (i, 0)), pl.BlockSpec((r, d), lambda i: (i, 0))],
        out_shape=[SDS((s, d), BF), SDS((s + r, d), BF), SDS((s + r, d), BF)],
        compiler_params=_params("arbitrary"))(qkv, qg, kg)


def _qkv_post_bwd(qkv, qg, kg, dq, dkpad, dvpad, name):
    s, d3 = qkv.shape
    d = d3 // 3
    r = _tile(s, 256, 16)
    off = BAND_LEFT // r

    def body(x_ref, qg_ref, kg_ref, dq_ref, dk_ref, dv_ref, o_ref, dqg_ref, dkg_ref):
        i = pl.program_id(0)
        _, vq = jax.vjp(_headnorm, x_ref[:, 0:d].astype(F32), qg_ref[...])
        dxq, dqg = vq(dq_ref[...])
        _, vk = jax.vjp(_headnorm, x_ref[:, d:2 * d].astype(F32), kg_ref[...])
        dxk, dkg = vk(dk_ref[...])
        o_ref[:, 0:d] = dxq.astype(BF)
        o_ref[:, d:2 * d] = dxk.astype(BF)
        o_ref[:, 2 * d:] = dv_ref[...].astype(BF)

        @pl.when(i == 0)
        def _():
            dqg_ref[...] = dqg
            dkg_ref[...] = dkg

        @pl.when(i > 0)
        def _():
            dqg_ref[...] += dqg
            dkg_ref[...] += dkg

    vec = pl.BlockSpec((1, HEAD), lambda i: (0, 0))
    row3 = pl.BlockSpec((r, d3), lambda i: (i, 0))
    row = pl.BlockSpec((r, d), lambda i: (i, 0))
    padrow = pl.BlockSpec((r, d), lambda i: (i + off, 0))
    return pl.pallas_call(
        body, name=name, grid=(s // r,), in_specs=[row3, vec, vec, row, padrow, padrow],
        out_specs=[row3, vec, vec], out_shape=[SDS((s, d3), BF), SDS((1, HEAD), F32), SDS((1, HEAD), F32)],
        compiler_params=_params("arbitrary"))(qkv, qg, kg, dq, dkpad, dvpad)


ATT_QB = 256


def _att_probs(q, kw, bias, c):
    sc = lax.dot_general(q, kw, NT, preferred_element_type=F32) * (HEAD ** -0.5) + bias
    lane = lax.broadcasted_iota(jnp.int32, (CHUNK, BAND), 1)
    sc = jnp.where(lane + c * CHUNK >= BAND_LEFT, sc, MASK_VALUE)
    p = jnp.exp(sc - jnp.max(sc, axis=-1, keepdims=True))
    return p / jnp.sum(p, axis=-1, keepdims=True)


def _attn_fwd(q, kpad, vpad, bias, name):
    s, d = q.shape
    h = d // HEAD
    sp = kpad.shape[0]
    qb = _tile(s, ATT_QB, CHUNK)
    per = qb // CHUNK

    def body(q_ref, k_ref, v_ref, b_ref, o_ref):
        i = pl.program_id(1)
        bias_h = b_ref[0]
        for cc in range(per):
            c = i * per + cc
            start = pl.multiple_of(c * CHUNK, CHUNK)
            p = _att_probs(q_ref[pl.ds(cc * CHUNK, CHUNK), :], k_ref[pl.ds(start, BAND), :], bias_h, c)
            o_ref[pl.ds(cc * CHUNK, CHUNK), :] = jnp.dot(
                p.astype(BF), v_ref[pl.ds(start, BAND), :], preferred_element_type=F32).astype(BF)

    qspec = pl.BlockSpec((qb, HEAD), lambda hh, i: (i, hh))
    kspec = pl.BlockSpec((sp, HEAD), lambda hh, i: (0, hh))
    return pl.pallas_call(
        body, name=name, grid=(h, s // qb),
        in_specs=[qspec, kspec, kspec, pl.BlockSpec((1, CHUNK, BAND), lambda hh, i: (hh, 0, 0))],
        out_specs=qspec, out_shape=SDS((s, d), BF), compiler_params=_params("parallel", "arbitrary"))(q, kpad, vpad, bias)


def _attn_bwd(q, kpad, vpad, bias, do, name):
    s, d = q.shape
    h = d // HEAD
    sp = kpad.shape[0]
    qb = _tile(s, ATT_QB, CHUNK)
    per = qb // CHUNK
    scale = HEAD ** -0.5

    def body(q_ref, k_ref, v_ref, b_ref, do_ref, dq_ref, dk_ref, dv_ref, db_ref):
        i = pl.program_id(1)

        @pl.when(i == 0)
        def _():
            dk_ref[...] = jnp.zeros_like(dk_ref)
            dv_ref[...] = jnp.zeros_like(dv_ref)
            db_ref[...] = jnp.zeros_like(db_ref)

        bias_h = b_ref[0]
        for cc in range(per):
            c = i * per + cc
            start = pl.multiple_of(c * CHUNK, CHUNK)
            rows = pl.ds(cc * CHUNK, CHUNK)
            win = pl.ds(start, BAND)
            qc, kw, vw = q_ref[rows, :], k_ref[win, :], v_ref[win, :]
            doc = do_ref[rows, :].astype(BF)
            p = _att_probs(qc, kw, bias_h, c)
            dp = lax.dot_general(doc, vw, NT, preferred_element_type=F32)
            ds = p * (dp - jnp.sum(p * dp, axis=-1, keepdims=True))
            db_ref[0] += ds
            dsb = (ds * scale).astype(BF)
            dq_ref[rows, :] = jnp.dot(dsb, kw, preferred_element_type=F32)
            dk_ref[win, :] += lax.dot_general(dsb, qc, TN, preferred_element_type=F32)
            dv_ref[win, :] += lax.dot_general(p.astype(BF), doc, TN, preferred_element_type=F32)

    qspec = pl.BlockSpec((qb, HEAD), lambda hh, i: (i, hh))
    kspec = pl.BlockSpec((sp, HEAD), lambda hh, i: (0, hh))
    bspec = pl.BlockSpec((1, CHUNK, BAND), lambda hh, i: (hh, 0, 0))
    return pl.pallas_call(
        body, name=name, grid=(h, s // qb), in_specs=[qspec, kspec, kspec, bspec, qspec],
        out_specs=[qspec, kspec, kspec, bspec],
        out_shape=[SDS((s, d), F32), SDS((sp, d), F32), SDS((sp, d), F32), SDS((h, CHUNK, BAND), F32)],
        compiler_params=_params("parallel", "arbitrary"))(q, kpad, vpad, bias, do)


def _pool_tile(x_ext, gain, w4, scale, row0):
    n, d = x_ext.shape
    dg = d // len(POOL_WINDOWS)
    pos = lax.broadcasted_iota(jnp.int32, (n, 1), 0) + row0
    hn = _rms(x_ext, gain) * jnp.where(pos >= 0, 1.0, 0.0)
    outs = []
    for gi, w in enumerate(POOL_WINDOWS):
        hg = hn[:, gi * dg:(gi + 1) * dg]
        acc, k = hg, 1
        while k < w:
            acc = acc + _shift(acc, k)
            k *= 2
        inv = 1.0 / jnp.clip(pos + 1, 1, w).astype(F32)
        pooled = acc * inv - hg
        outs.append(jnp.dot(pooled.astype(BF), w4[gi].astype(BF), preferred_element_type=F32))
    return jnp.concatenate(outs, axis=1) * scale


POOL_ROWS = 128


def _pool_fwd(x, gain, w4, scale, name):
    s, d = x.shape
    r = _tile(s, POOL_ROWS, HALO)
    hidx = _halo_index(r)

    def body(xc, xh, g_ref, w_ref, s_ref, o_ref):
        rb = pl.program_id(0)
        x_ext = jnp.concatenate([xh[...], xc[...]], axis=0)
        y = _pool_tile(x_ext, g_ref[...], [w_ref[gi] for gi in range(len(POOL_WINDOWS))], s_ref[...], rb * r - HALO)
        o_ref[...] = xc[...] + y[HALO:]

    cur = pl.BlockSpec((r, d), lambda rb: (rb, 0))
    vec = pl.BlockSpec((1, d), lambda rb: (0, 0))
    return pl.pallas_call(
        body, name=name, grid=(s // r,),
        in_specs=[cur, pl.BlockSpec((HALO, d), lambda rb: (hidx(rb), 0)), vec,
                  pl.BlockSpec(w4.shape, lambda rb: (0, 0, 0)), vec],
        out_specs=cur, out_shape=SDS((s, d), F32), compiler_params=_params("parallel"))(x, x, gain, w4, scale)


def _pool_bwd(x, gain, w4, scale, dy, name):
    s, d = x.shape
    r = _tile(s, POOL_ROWS, HALO)
    nb = s // r
    hidx = _halo_index(r)

    def body(xc, xh, g_ref, w_ref, s_ref, dy_ref, dx_ref, dg_ref, dw_ref, ds_ref, carry):
        step = pl.program_id(0)
        rb = nb - 1 - step
        x_ext = jnp.concatenate([xh[...], xc[...]], axis=0)
        fn = functools.partial(_pool_tile, row0=rb * r - HALO)
        _, vjp = jax.vjp(fn, x_ext, g_ref[...], [w_ref[gi] for gi in range(len(POOL_WINDOWS))], s_ref[...])
        ct = jnp.concatenate([jnp.zeros((HALO, d), F32), dy_ref[...]], axis=0)
        dx_ext, dg, dws, dsc = vjp(ct)

        @pl.when(step == 0)
        def _():
            carry[...] = jnp.zeros_like(carry)
            dg_ref[...] = jnp.zeros_like(dg_ref)
            dw_ref[...] = jnp.zeros_like(dw_ref)
            ds_ref[...] = jnp.zeros_like(ds_ref)

        dx_ref[...] = dy_ref[...] + dx_ext[HALO:]
        dx_ref[pl.ds(r - HALO, HALO), :] += carry[...]
        carry[...] = dx_ext[:HALO]
        dg_ref[...] += dg
        for gi, dw in enumerate(dws):
            dw_ref[gi] += dw
        ds_ref[...] += dsc

    cur = pl.BlockSpec((r, d), lambda t: (nb - 1 - t, 0))
    vec = pl.BlockSpec((1, d), lambda t: (0, 0))
    wspec = pl.BlockSpec(w4.shape, lambda t: (0, 0, 0))
    return pl.pallas_call(
        body, name=name, grid=(nb,),
        in_specs=[cur, pl.BlockSpec((HALO, d), lambda t: (hidx(nb - 1 - t), 0)), vec, wspec, vec, cur],
        out_specs=[cur, vec, wspec, vec],
        out_shape=[SDS((s, d), F32), SDS((1, d), F32), SDS(w4.shape, F32), SDS((1, d), F32)],
        scratch_shapes=[pltpu.VMEM((HALO, d), F32)], compiler_params=_params("arbitrary"))(x, x, gain, w4, scale, dy)


def _gdn_conv_tile(u_ext, cw, kind):
    acc = (u_ext * cw[3:4] + _shift(u_ext, 1) * cw[2:3] + _shift(u_ext, 2) * cw[1:2] + _shift(u_ext, 3) * cw[0:1])
    y = _silu(acc)
    if kind != "v":
        y = y * lax.rsqrt(jnp.sum(y * y, axis=-1, keepdims=True) + EPS)
    if kind == "q":
        y = y * (HEAD ** -0.5)
    return y


def _gdn_conv_fwd(proj, cw, kind, head0, nheads, name):
    s = proj.shape[0]
    r = _tile(s, 512, HALO)
    hidx = _halo_index(r)

    def body(uc, uh, cw_ref, o_ref):
        rb = pl.program_id(1)
        keep = jnp.where(rb == 0, 0.0, 1.0)
        u_ext = jnp.concatenate([uh[...].astype(F32) * keep, uc[...].astype(F32)], axis=0)
        o_ref[...] = _gdn_conv_tile(u_ext, cw_ref[...], kind)[HALO:].astype(BF)

    return pl.pallas_call(
        body, name=name, grid=(nheads, s // r),
        in_specs=[pl.BlockSpec((r, HEAD), lambda j, rb: (rb, head0 + j)),
                  pl.BlockSpec((HALO, HEAD), lambda j, rb: (hidx(rb), head0 + j)),
                  pl.BlockSpec((8, HEAD), lambda j, rb: (0, head0 + j))],
        out_specs=pl.BlockSpec((r, HEAD), lambda j, rb: (rb, j)), out_shape=SDS((s, nheads * HEAD), BF),
        compiler_params=_params("parallel", "parallel"))(proj, proj, cw)


def _gdn_conv_bwd(proj, cw, dy, kind, head0, nheads, name):
    s = proj.shape[0]
    r = _tile(s, 512, HALO)
    nb = s // r
    hidx = _halo_index(r)
    rep = dy.shape[1] // (nheads * HEAD)

    def body(*refs):
        uc, uh, cw_ref = refs[:3]
        dys = refs[3:3 + rep]
        du_ref, dcw_ref, carry = refs[3 + rep:]
        step = pl.program_id(1)
        rb = nb - 1 - step
        keep = jnp.where(rb == 0, 0.0, 1.0)
        zero = jnp.zeros((HALO, HEAD), F32)
        u_ext = jnp.concatenate([uh[...].astype(F32) * keep, uc[...].astype(F32)], axis=0)
        dyc = dys[0][...]
        for extra in dys[1:]:
            dyc = dyc + extra[...]
        _, vjp = jax.vjp(functools.partial(_gdn_conv_tile, kind=kind), u_ext, cw_ref[...])
        du_ext, dcw = vjp(jnp.concatenate([zero, dyc], axis=0))

        @pl.when(step == 0)
        def _():
            carry[...] = zero
            dcw_ref[...] = jnp.zeros_like(dcw_ref)

        du_ref[pl.ds(0, r - HALO), :] = du_ext[HALO:r].astype(BF)
        du_ref[pl.ds(r - HALO, HALO), :] = (du_ext[r:] + carry[...]).astype(BF)
        carry[...] = du_ext[:HALO]
        dcw_ref[...] += dcw

    dy_specs = [pl.BlockSpec((r, HEAD), functools.partial(lambda j, t, e: (nb - 1 - t, rep * j + e), e=e)) for e in range(rep)]
    return pl.pallas_call(
        body, name=name, grid=(nheads, nb),
        in_specs=[pl.BlockSpec((r, HEAD), lambda j, t: (nb - 1 - t, head0 + j)),
                  pl.BlockSpec((HALO, HEAD), lambda j, t: (hidx(nb - 1 - t), head0 + j)),
                  pl.BlockSpec((8, HEAD), lambda j, t: (0, head0 + j))] + dy_specs,
        out_specs=[pl.BlockSpec((r, HEAD), lambda j, t: (nb - 1 - t, j)), pl.BlockSpec((8, HEAD), lambda j, t: (0, j))],
        out_shape=[SDS((s, nheads * HEAD), BF), SDS((8, nheads * HEAD), F32)],
        scratch_shapes=[pltpu.VMEM((HALO, HEAD), F32)],
        compiler_params=_params("parallel", "arbitrary"))(proj, proj, cw, *([dy] * rep))


GATE_ROWS = 256


def _gates_tile(a, bt, a_log, dt_bias, hv):
    r = a.shape[0]
    z = a + dt_bias
    softplus = jnp.maximum(z, 0.0) + jnp.log(1.0 + jnp.exp(-jnp.abs(z)))
    g = -jnp.exp(a_log) * softplus
    ri = lax.broadcasted_iota(jnp.int32, (r, r), 0)
    ci = lax.broadcasted_iota(jnp.int32, (r, r), 1)
    same_chunk = jnp.right_shift(ri, 6) == jnp.right_shift(ci, 6)
    tri = jnp.where(same_chunk, jnp.where(ri >= ci, 1.0, 0.0), 0.0).astype(F32)
    gc = jnp.dot(tri, g, precision=HI, preferred_element_type=F32)
    beta = 1.0 / (1.0 + jnp.exp(-bt))
    er = lax.broadcasted_iota(jnp.int32, (LANE, hv * HEAD), 0)
    ec = lax.broadcasted_iota(jnp.int32, (LANE, hv * HEAD), 1)
    expand = jnp.where(er == jnp.right_shift(ec, 7), 1.0, 0.0).astype(F32)
    return (jnp.dot(gc, expand, precision=HI, preferred_element_type=F32),
            jnp.dot(beta, expand, precision=HI, preferred_element_type=F32))


def _gates_fwd(ab, a_log, dt_bias, hv, name):
    s = ab.shape[0]
    r = _tile(s, GATE_ROWS, CHUNK)

    def body(a_ref, b_ref, al_ref, dt_ref, gc_ref, bb_ref):
        gcb, btb = _gates_tile(a_ref[...], b_ref[...], al_ref[...], dt_ref[...], hv)
        gc_ref[...] = gcb
        bb_ref[...] = btb

    vec = pl.BlockSpec((1, LANE), lambda i: (0, 0))
    wide = pl.BlockSpec((r, hv * HEAD), lambda i: (i, 0))
    return pl.pallas_call(
        body, name=name, grid=(s // r,),
        in_specs=[pl.BlockSpec((r, LANE), lambda i: (i, 0)), pl.BlockSpec((r, LANE), lambda i: (i, 1)), vec, vec],
        out_specs=[wide, wide], out_shape=[SDS((s, hv * HEAD), F32)] * 2,
        compiler_params=_params("parallel"))(ab, ab, a_log, dt_bias)


def _gates_bwd(ab, a_log, dt_bias, dgcb, dbtb, hv, name):
    s = ab.shape[0]
    r = _tile(s, GATE_ROWS, CHUNK)

    def body(a_ref, b_ref, al_ref, dt_ref, dgc_ref, dbb_ref, dab_ref, dal_ref, ddt_ref):
        i = pl.program_id(0)
        _, vjp = jax.vjp(functools.partial(_gates_tile, hv=hv), a_ref[...], b_ref[...], al_ref[...], dt_ref[...])
        da, dbt, dal, ddt = vjp((dgc_ref[...], dbb_ref[...]))
        dab_ref[:, 0:LANE] = da
        dab_ref[:, LANE:] = dbt

        @pl.when(i == 0)
        def _():
            dal_ref[...] = dal
            ddt_ref[...] = ddt

        @pl.when(i > 0)
        def _():
            dal_ref[...] += dal
            ddt_ref[...] += ddt

    vec = pl.BlockSpec((1, LANE), lambda i: (0, 0))
    wide = pl.BlockSpec((r, hv * HEAD), lambda i: (i, 0))
    return pl.pallas_call(
        body, name=name, grid=(s // r,),
        in_specs=[pl.BlockSpec((r, LANE), lambda i: (i, 0)), pl.BlockSpec((r, LANE), lambda i: (i, 1)), vec, vec, wide, wide],
        out_specs=[pl.BlockSpec((r, 2 * LANE), lambda i: (i, 0)), vec, vec],
        out_shape=[SDS((s, 2 * LANE), F32), SDS((1, LANE), F32), SDS((1, LANE), F32)],
        compiler_params=_params("arbitrary"))(ab, ab, a_log, dt_bias, dgcb, dbtb)


def _split_bf16(a):
    hi = a.astype(BF)
    return hi, (a - hi.astype(F32)).astype(BF)


def _dot3(a, b, dims=(((1,), (0,)), ((), ()))):
    ah, al = _split_bf16(a)
    bh, bl = _split_bf16(b)
    d = lambda x, y: lax.dot_general(x, y, dims, preferred_element_type=F32)
    return d(ah, bh) + (d(ah, bl) + d(al, bh))


NNB = (((2,), (1,)), ((0,), (0,)))
NTB = (((2,), (2,)), ((0,), (0,)))
TNB = (((1,), (1,)), ((0,), (0,)))


def _bdot(a, b, dims):
    return lax.dot_general(a, b, dims, preferred_element_type=F32)


def _unit_lower_inverse(a):
    ri = lax.broadcasted_iota(jnp.int32, a.shape, 1)
    ci = lax.broadcasted_iota(jnp.int32, a.shape, 2)
    p = -a
    t = jnp.where(ri == ci, 1.0, 0.0) + p
    for _ in range(5):
        p = _dot3(p, p, NNB)
        t = t + _dot3(t, p, NNB)
    return t


@jax.custom_vjp
def _known_inverse(a, t):
    return t


def _known_inverse_fwd(a, t):
    return t, t


def _known_inverse_bwd(t, g):
    return -_dot3(_dot3(t, g, TNB), t, NTB), jnp.zeros_like(t)


_known_inverse.defvjp(_known_inverse_fwd, _known_inverse_bwd)


def _delta_decay(gcb):
    c = CHUNK
    shape = (gcb.shape[0], c, c)
    ri = lax.broadcasted_iota(jnp.int32, shape, 1)
    ci = lax.broadcasted_iota(jnp.int32, shape, 2)
    causal = ri >= ci
    grow = jnp.stack([jnp.concatenate([gcb[b], gcb[b]], axis=0).T[:c, :c] for b in range(shape[0])])
    return jnp.where(causal, jnp.exp(jnp.where(causal, gcb[:, :, :c] - grow, 0.0)), 0.0), ri > ci


def _delta_system(k, gcb, btb):
    decay, strict = _delta_decay(gcb)
    return jnp.where(strict, _bdot((k * btb).astype(BF), k.astype(BF), NTB) * decay, 0.0)


def _delta_prep(q, k, v, gcb, btb, tinv):
    decay, strict = _delta_decay(gcb)
    kb = k * btb
    kbf = k.astype(BF)
    a = jnp.where(strict, _bdot(kb.astype(BF), kbf, NTB) * decay, 0.0)
    t = _known_inverse(a, tinv).astype(BF)
    u = _bdot(t, (v * btb).astype(BF), NNB)
    w = _bdot(t, (kb * jnp.exp(gcb)).astype(BF), NNB)
    attn = _bdot(q.astype(BF), kbf, NTB) * decay
    return u, w, attn


def _delta_scan(u, w, attn, q, k, gcb, s_in):
    c = CHUNK
    glast = gcb[c - 1:c, :]
    sb = s_in.astype(BF)
    v_new = u - jnp.dot(w.astype(BF), sb, preferred_element_type=F32)
    vnb = v_new.astype(BF)
    o = (jnp.dot((q * jnp.exp(gcb)).astype(BF), sb, preferred_element_type=F32)
         + jnp.dot(attn.astype(BF), vnb, preferred_element_type=F32))
    ks = (k * jnp.exp(glast - gcb)).astype(BF)
    s_out = s_in * jnp.exp(glast[:, 0:1]) + lax.dot_general(ks, vnb, TN, preferred_element_type=F32)
    return o, s_out


PREP_ROWS = 512
PREP_HEADS = 2
SCAN_ROWS = 512
SCAN_HEADS = 4


def _delta_prep_fwd(q, k, v, gcb, btb, name):
    s, dv = v.shape
    hv = dv // HEAD
    g = PREP_HEADS
    assert dv // q.shape[1] == g
    r = _tile(s, PREP_ROWS, CHUNK)

    def body(q_ref, k_ref, v_ref, g_ref, b_ref, u_ref, w_ref, a_ref, t_ref):
        nb = r // CHUNK
        qc = q_ref[...].astype(F32).reshape(nb, CHUNK, HEAD)
        kc = k_ref[...].astype(F32).reshape(nb, CHUNK, HEAD)
        for hh in range(g):
            cols = pl.ds(hh * HEAD, HEAD)
            half = pl.ds(hh * CHUNK, CHUNK)
            gc = g_ref[:, cols].reshape(nb, CHUNK, HEAD)
            bc = b_ref[:, cols].reshape(nb, CHUNK, HEAD)
            tinv = _unit_lower_inverse(_delta_system(kc, gc, bc))
            u, w, attn = _delta_prep(qc, kc, v_ref[:, cols].astype(F32).reshape(nb, CHUNK, HEAD), gc, bc, tinv)
            u_ref[:, cols] = u.reshape(r, HEAD)
            w_ref[:, cols] = w.reshape(r, HEAD).astype(BF)
            a_ref[:, half] = attn.reshape(r, CHUNK).astype(BF)
            t_ref[:, half] = tinv.reshape(r, CHUNK)

    kq = pl.BlockSpec((r, HEAD), lambda j, i: (i, j))
    vs = pl.BlockSpec((r, g * HEAD), lambda j, i: (i, j))
    sq = pl.BlockSpec((r, g * CHUNK), lambda j, i: (i, j))
    return pl.pallas_call(
        body, name=name, grid=(hv // g, s // r), in_specs=[kq, kq, vs, vs, vs], out_specs=[vs, vs, sq, sq],
        out_shape=[SDS((s, dv), F32), SDS((s, dv), BF), SDS((s, hv * CHUNK), BF), SDS((s, hv * CHUNK), F32)],
        compiler_params=_params("parallel", "parallel"))(q, k, v, gcb, btb)


def _delta_prep_bwd(q, k, v, gcb, btb, tinv, du, dw, dattn, dq_s, dk_s, dg_s, name):
    s, dv = v.shape
    hv = dv // HEAD
    g = PREP_HEADS
    r = _tile(s, PREP_ROWS, CHUNK)

    def body(q_ref, k_ref, v_ref, g_ref, b_ref, t_ref, du_ref, dw_ref, da_ref, dqs_ref, dks_ref, dgs_ref,
             dq_ref, dk_ref, dv_ref, dg_ref, db_ref):
        nb = r // CHUNK
        wide = lambda ref, cols: ref[:, cols].astype(F32).reshape(nb, CHUNK, HEAD)
        qc = q_ref[...].astype(F32).reshape(nb, CHUNK, HEAD)
        kc = k_ref[...].astype(F32).reshape(nb, CHUNK, HEAD)
        for hh in range(g):
            cols = pl.ds(hh * HEAD, HEAD)
            half = pl.ds(hh * CHUNK, CHUNK)
            fn = functools.partial(_delta_prep, tinv=t_ref[:, half].reshape(nb, CHUNK, CHUNK))
            _, vjp = jax.vjp(fn, qc, kc, wide(v_ref, cols), wide(g_ref, cols), wide(b_ref, cols))
            dq, dk, dvv, dg, db = vjp((wide(du_ref, cols), wide(dw_ref, cols),
                                       da_ref[:, half].astype(F32).reshape(nb, CHUNK, CHUNK)))
            dq_ref[:, cols] = dq.reshape(r, HEAD) + dqs_ref[:, cols]
            dk_ref[:, cols] = dk.reshape(r, HEAD) + dks_ref[:, cols]
            dv_ref[:, cols] = dvv.reshape(r, HEAD)
            dg_ref[:, cols] = dg.reshape(r, HEAD) + dgs_ref[:, cols]
            db_ref[:, cols] = db.reshape(r, HEAD)

    kq = pl.BlockSpec((r, HEAD), lambda j, i: (i, j))
    vs = pl.BlockSpec((r, g * HEAD), lambda j, i: (i, j))
    sq = pl.BlockSpec((r, g * CHUNK), lambda j, i: (i, j))
    return pl.pallas_call(
        body, name=name, grid=(hv // g, s // r), in_specs=[kq, kq, vs, vs, vs, sq, vs, vs, sq, vs, vs, vs],
        out_specs=[vs] * 5, out_shape=[SDS((s, dv), F32)] * 5,
        compiler_params=_params("parallel", "parallel"))(q, k, v, gcb, btb, tinv, du, dw, dattn, dq_s, dk_s, dg_s)


def _delta_scan_fwd(u, w, attn, q, k, gcb, name):
    s, dv = u.shape
    hv = dv // HEAD
    rep = dv // q.shape[1]
    g = min(SCAN_HEADS, hv)
    r = _tile(s, SCAN_ROWS, CHUNK)
    per = r // CHUNK

    def body(u_ref, w_ref, a_ref, q_ref, k_ref, g_ref, o_ref, st_ref, state):
        @pl.when(pl.program_id(1) == 0)
        def _():
            state[...] = jnp.zeros_like(state)

        def chunk(cc, carry):
            rows = pl.ds(pl.multiple_of(cc * CHUNK, CHUNK), CHUNK)
            for hh in range(g):
                cols = pl.ds(hh * HEAD, HEAD)
                kcols = pl.ds((hh // rep) * HEAD, HEAD)
                s_in = state[hh]
                st_ref[hh, cc] = s_in
                o, s_out = _delta_scan(u_ref[rows, cols], w_ref[rows, cols].astype(F32),
                                       a_ref[rows, pl.ds(hh * CHUNK, CHUNK)].astype(F32),
                                       q_ref[rows, kcols].astype(F32), k_ref[rows, kcols].astype(F32), g_ref[rows, cols], s_in)
                o_ref[rows, cols] = o.astype(BF)
                state[hh] = s_out
            return carry

        lax.fori_loop(0, per, chunk, 0)

    kq = pl.BlockSpec((r, g // rep * HEAD), lambda j, i: (i, j))
    vs = pl.BlockSpec((r, g * HEAD), lambda j, i: (i, j))
    sq = pl.BlockSpec((r, g * CHUNK), lambda j, i: (i, j))
    return pl.pallas_call(
        body, name=name, grid=(hv // g, s // r), in_specs=[vs, vs, sq, kq, kq, vs],
        out_specs=[vs, pl.BlockSpec((g, per, HEAD, HEAD), lambda j, i: (j, i, 0, 0))],
        out_shape=[SDS((s, dv), BF), SDS((hv, s // CHUNK, HEAD, HEAD), F32)],
        scratch_shapes=[pltpu.VMEM((g, HEAD, HEAD), F32)],
        compiler_params=_params("parallel", "arbitrary"))(u, w, attn, q, k, gcb)


def _delta_scan_bwd(u, w, attn, q, k, gcb, states, do, name):
    s, dv = u.shape
    hv = dv // HEAD
    rep = dv // q.shape[1]
    g = min(SCAN_HEADS, hv)
    r = _tile(s, SCAN_ROWS, CHUNK)
    per = r // CHUNK
    nb = s // r

    def body(u_ref, w_ref, a_ref, q_ref, k_ref, g_ref, st_ref, do_ref, du_ref, dw_ref, da_ref, dq_ref, dk_ref, dg_ref, dstate):
        @pl.when(pl.program_id(1) == 0)
        def _():
            dstate[...] = jnp.zeros_like(dstate)

        def chunk(t, carry):
            cc = per - 1 - t
            rows = pl.ds(pl.multiple_of(cc * CHUNK, CHUNK), CHUNK)
            for hh in range(g):
                cols = pl.ds(hh * HEAD, HEAD)
                kcols = pl.ds((hh // rep) * HEAD, HEAD)
                half = pl.ds(hh * CHUNK, CHUNK)
                _, vjp = jax.vjp(_delta_scan, u_ref[rows, cols], w_ref[rows, cols].astype(F32), a_ref[rows, half].astype(F32),
                                 q_ref[rows, kcols].astype(F32), k_ref[rows, kcols].astype(F32), g_ref[rows, cols], st_ref[hh, cc])
                du, dw, da, dq, dk, dg, ds_in = vjp((do_ref[rows, cols], dstate[hh]))
                du_ref[rows, cols] = du.astype(BF)
                dw_ref[rows, cols] = dw.astype(BF)
                da_ref[rows, half] = da.astype(BF)
                dq_ref[rows, cols] = dq
                dk_ref[rows, cols] = dk
                dg_ref[rows, cols] = dg
                dstate[hh] = ds_in
            return carry

        lax.fori_loop(0, per, chunk, 0)

    kq = pl.BlockSpec((r, g // rep * HEAD), lambda j, i: (nb - 1 - i, j))
    vs = pl.BlockSpec((r, g * HEAD), lambda j, i: (nb - 1 - i, j))
    sq = pl.BlockSpec((r, g * CHUNK), lambda j, i: (nb - 1 - i, j))
    return pl.pallas_call(
        body, name=name, grid=(hv // g, nb),
        in_specs=[vs, vs, sq, kq, kq, vs, pl.BlockSpec((g, per, HEAD, HEAD), lambda j, i: (j, nb - 1 - i, 0, 0)), vs],
        out_specs=[vs, vs, sq, vs, vs, vs],
        out_shape=[SDS((s, dv), BF), SDS((s, dv), BF), SDS((s, hv * CHUNK), BF)] + [SDS((s, dv), F32)] * 3,
        scratch_shapes=[pltpu.VMEM((g, HEAD, HEAD), F32)],
        compiler_params=_params("parallel", "arbitrary"))(u, w, attn, q, k, gcb, states, do)


def _delta_chunk(q, k, v, gcb, btb, s_in):
    c = CHUNK
    ri = lax.broadcasted_iota(jnp.int32, (c, c), 0)
    ci = lax.broadcasted_iota(jnp.int32, (c, c), 1)
    causal = ri >= ci
    gcol = gcb[:, :c]
    grow = jnp.concatenate([gcb, gcb], axis=0).T[:c, :c]
    decay = jnp.where(causal, jnp.exp(jnp.where(causal, gcol - grow, 0.0)), 0.0)
    kb = k * btb
    vb = v * btb
    kbf = k.astype(BF)
    a = jnp.where(ri > ci, lax.dot_general(kb.astype(BF), kbf, NT, preferred_element_type=F32) * decay, 0.0)
    p = -a
    t = jnp.where(ri == ci, 1.0, 0.0) + p
    for _ in range(5):
        p = jnp.dot(p, p, precision=HI, preferred_element_type=F32)
        t = t + jnp.dot(t, p, precision=HI, preferred_element_type=F32)
    eg = jnp.exp(gcb)
    u = jnp.dot(t, vb, precision=HI, preferred_element_type=F32)
    w = jnp.dot(t, kb * eg, precision=HI, preferred_element_type=F32)
    attn = lax.dot_general(q.astype(BF), kbf, NT, preferred_element_type=F32) * decay
    glast = gcb[c - 1:c, :]
    ks = k * jnp.exp(glast - gcb)
    sb = s_in.astype(BF)
    v_new = u - jnp.dot(w.astype(BF), sb, preferred_element_type=F32)
    o = (jnp.dot((q * eg).astype(BF), sb, preferred_element_type=F32)
         + jnp.dot(attn.astype(BF), v_new.astype(BF), preferred_element_type=F32))
    s_out = s_in * jnp.exp(glast[:, 0:1]) + lax.dot_general(ks.astype(BF), v_new.astype(BF), TN, preferred_element_type=F32)
    return o, s_out


GDN_ROWS = 512


def _delta_fwd(q, k, v, gcb, btb, name):
    s, dv = v.shape
    hv = dv // HEAD
    rep = dv // q.shape[1]
    r = _tile(s, GDN_ROWS, CHUNK)
    per = r // CHUNK
    nc = s // CHUNK

    def body(q_ref, k_ref, v_ref, g_ref, b_ref, o_ref, st_ref, state):
        @pl.when(pl.program_id(1) == 0)
        def _():
            state[...] = jnp.zeros_like(state)

        def chunk(cc, carry):
            rows = pl.ds(pl.multiple_of(cc * CHUNK, CHUNK), CHUNK)
            st_ref[0, cc] = state[...]
            o, s_out = _delta_chunk(q_ref[rows, :].astype(F32), k_ref[rows, :].astype(F32), v_ref[rows, :].astype(F32),
                                    g_ref[rows, :], b_ref[rows, :], state[...])
            o_ref[rows, :] = o.astype(BF)
            state[...] = s_out
            return carry

        lax.fori_loop(0, per, chunk, 0)

    kq = pl.BlockSpec((r, HEAD), lambda h, i: (i, h // rep))
    vs = pl.BlockSpec((r, HEAD), lambda h, i: (i, h))
    return pl.pallas_call(
        body, name=name, grid=(hv, s // r), in_specs=[kq, kq, vs, vs, vs],
        out_specs=[vs, pl.BlockSpec((1, per, HEAD, HEAD), lambda h, i: (h, i, 0, 0))],
        out_shape=[SDS((s, dv), BF), SDS((hv, nc, HEAD, HEAD), F32)],
        scratch_shapes=[pltpu.VMEM((HEAD, HEAD), F32)],
        compiler_params=_params("parallel", "arbitrary"))(q, k, v, gcb, btb)


def _delta_bwd(q, k, v, gcb, btb, states, do, name):
    s, dv = v.shape
    hv = dv // HEAD
    rep = dv // q.shape[1]
    r = _tile(s, GDN_ROWS, CHUNK)
    per = r // CHUNK
    nb = s // r

    def body(q_ref, k_ref, v_ref, g_ref, b_ref, st_ref, do_ref, dq_ref, dk_ref, dv_ref, dg_ref, db_ref, dstate):
        @pl.when(pl.program_id(1) == 0)
        def _():
            dstate[...] = jnp.zeros_like(dstate)

        def chunk(t, carry):
            cc = per - 1 - t
            rows = pl.ds(pl.multiple_of(cc * CHUNK, CHUNK), CHUNK)
            _, vjp = jax.vjp(_delta_chunk, q_ref[rows, :].astype(F32), k_ref[rows, :].astype(F32),
                             v_ref[rows, :].astype(F32), g_ref[rows, :], b_ref[rows, :], st_ref[0, cc])
            dq, dk, dvv, dg, db, ds_in = vjp((do_ref[rows, :].astype(F32), dstate[...]))
            dq_ref[rows, :] = dq
            dk_ref[rows, :] = dk
            dv_ref[rows, :] = dvv
            dg_ref[rows, :] = dg
            db_ref[rows, :] = db
            dstate[...] = ds_in
            return carry

        lax.fori_loop(0, per, chunk, 0)

    kq = pl.BlockSpec((r, HEAD), lambda h, i: (nb - 1 - i, h // rep))
    vs = pl.BlockSpec((r, HEAD), lambda h, i: (nb - 1 - i, h))
    return pl.pallas_call(
        body, name=name, grid=(hv, nb),
        in_specs=[kq, kq, vs, vs, vs, pl.BlockSpec((1, per, HEAD, HEAD), lambda h, i: (h, nb - 1 - i, 0, 0)), vs],
        out_specs=[vs] * 5, out_shape=[SDS((s, dv), F32)] * 5,
        scratch_shapes=[pltpu.VMEM((HEAD, HEAD), F32)],
        compiler_params=_params("parallel", "arbitrary"))(q, k, v, gcb, btb, states, do)


def _gdn_out_tile(o, gate, gain):
    return _headnorm(o, gain) * _silu(gate)


def _gdn_out_fwd(o, proj, gate_col0, gain, name):
    s, dv = o.shape
    r = _tile(s, 128, 16)

    def body(o_ref, g_ref, gain_ref, y_ref):
        y_ref[...] = _gdn_out_tile(o_ref[...].astype(F32), g_ref[...].astype(F32), gain_ref[...]).astype(BF)

    row = pl.BlockSpec((r, dv), lambda i: (i, 0))
    return pl.pallas_call(
        body, name=name, grid=(s // r,),
        in_specs=[row, pl.BlockSpec((r, dv), lambda i: (i, gate_col0)), pl.BlockSpec((1, HEAD), lambda i: (0, 0))],
        out_specs=row, out_shape=SDS((s, dv), BF), compiler_params=_params("parallel"))(o, proj, gain)


def _gdn_out_bwd(o, proj, gate_col0, gain, dy, name):
    s, dv = o.shape
    r = _tile(s, 128, 16)

    def body(o_ref, g_ref, gain_ref, dy_ref, do_ref, dg_ref, dgain_ref):
        i = pl.program_id(0)
        _, vjp = jax.vjp(_gdn_out_tile, o_ref[...].astype(F32), g_ref[...].astype(F32), gain_ref[...])
        do, dg, dgain = vjp(dy_ref[...].astype(F32))
        do_ref[...] = do
        dg_ref[...] = dg.astype(BF)

        @pl.when(i == 0)
        def _():
            dgain_ref[...] = dgain

        @pl.when(i > 0)
        def _():
            dgain_ref[...] += dgain

    row = pl.BlockSpec((r, dv), lambda i: (i, 0))
    vec = pl.BlockSpec((1, HEAD), lambda i: (0, 0))
    return pl.pallas_call(
        body, name=name, grid=(s // r,),
        in_specs=[row, pl.BlockSpec((r, dv), lambda i: (i, gate_col0)), vec, row],
        out_specs=[row, row, vec], out_shape=[SDS((s, dv), F32), SDS((s, dv), BF), SDS((1, HEAD), F32)],
        compiler_params=_params("arbitrary"))(o, proj, gain, dy)


def _ffn_forward(x, gain, wu, wg, cw, wd, tag):
    h = _rmsnorm_fwd(x, gain, f"{tag}_norm")
    uu = _mm(h, wu, name=f"{tag}_up_u")
    ug = _mm(h, wg, name=f"{tag}_up_g")
    a = _ffn_act_fwd(uu, ug, cw, f"{tag}_act")
    y = _mm(a, wd, res=x, out_dtype=F32, name=f"{tag}_down")
    return y, (x, h, uu, ug, a)


def _ffn_backward(saved, dy, gain, wu, wg, cw, wd, tag):
    x, h, uu, ug, a = saved
    da = _mm(dy, wd, tb=True, name=f"{tag}_d_act")
    dwd = _mm(a, dy, ta=True, out_dtype=F32, name=f"{tag}_d_wd")
    duu, dug, dcw = _ffn_act_bwd(uu, ug, cw, da, f"{tag}_act_bwd")
    dh = _mm(duu, wu, tb=True, out_dtype=F32, name=f"{tag}_d_h_u")
    dh = _mm(dug, wg, tb=True, res=dh, out_dtype=F32, name=f"{tag}_d_h_g")
    dwu = _mm(h, duu, ta=True, out_dtype=F32, name=f"{tag}_d_wu")
    dwg = _mm(h, dug, ta=True, out_dtype=F32, name=f"{tag}_d_wg")
    dx, dgain = _rmsnorm_bwd(x, gain, dh, dy, f"{tag}_norm_bwd")
    return dx, dict(gain=dgain, wu=dwu, wg=dwg, cw=dcw, wd=dwd)


def _att_forward(x, gain, p, tag):
    h = _rmsnorm_fwd(x, gain, f"{tag}_norm")
    qkv = _mm(h, p["wqkv"], name=f"{tag}_qkv")
    q, kpad, vpad = _qkv_post_fwd(qkv, p["qg"], p["kg"], f"{tag}_qknorm")
    bias = _bias_expand(p["rel"], f"{tag}_bias")
    o = _attn_fwd(q, kpad, vpad, bias, f"{tag}_core")
    y = _mm(o, p["wo"], res=x, out_dtype=F32, name=f"{tag}_out")
    return y, (x, h, qkv, q, kpad, vpad, bias, o)


def _att_backward(saved, dy, gain, p, tag):
    x, h, qkv, q, kpad, vpad, bias, o = saved
    do = _mm(dy, p["wo"], tb=True, name=f"{tag}_d_o")
    dwo = _mm(o, dy, ta=True, out_dtype=F32, name=f"{tag}_d_wo")
    dq, dkpad, dvpad, dbias = _attn_bwd(q, kpad, vpad, bias, do, f"{tag}_core_bwd")
    drel = _bias_reduce(dbias, f"{tag}_bias_bwd")
    dqkv, dqg, dkg = _qkv_post_bwd(qkv, p["qg"], p["kg"], dq, dkpad, dvpad, f"{tag}_qknorm_bwd")
    dh = _mm(dqkv, p["wqkv"], tb=True, out_dtype=F32, name=f"{tag}_d_h")
    dwqkv = _mm(h, dqkv, ta=True, out_dtype=F32, name=f"{tag}_d_wqkv")
    dx, dgain = _rmsnorm_bwd(x, gain, dh, dy, f"{tag}_norm_bwd")
    return dx, dict(gain=dgain, wqkv=dwqkv, qg=dqg, kg=dkg, rel=drel, wo=dwo)


def _gdn_forward(x, gain, p, tag):
    d = x.shape[1]
    nk = d // HEAD
    hv = 2 * nk
    h = _rmsnorm_fwd(x, gain, f"{tag}_norm")
    proj = _mm(h, p["wmain"], name=f"{tag}_proj")
    ab = _mm(h, p["wab"], out_dtype=F32, name=f"{tag}_proj_ab")
    q = _gdn_conv_fwd(proj, p["cw"], "q", 0, nk, f"{tag}_conv_q")
    k = _gdn_conv_fwd(proj, p["cw"], "k", nk, nk, f"{tag}_conv_k")
    v = _gdn_conv_fwd(proj, p["cw"], "v", 2 * nk, hv, f"{tag}_conv_v")
    gcb, btb = _gates_fwd(ab, p["a_log"], p["dt_bias"], hv, f"{tag}_gates")
    u, wd, attn, tinv = _delta_prep_fwd(q, k, v, gcb, btb, f"{tag}_delta_prep")
    o, states = _delta_scan_fwd(u, wd, attn, q, k, gcb, f"{tag}_delta_scan")
    og = _gdn_out_fwd(o, proj, 2, p["o_gain"], f"{tag}_onorm")
    y = _mm(og, p["wo"], res=x, out_dtype=F32, name=f"{tag}_out")
    return y, (x, h, proj, ab, q, k, v, gcb, btb, u, wd, attn, tinv, o, states, og)


def _gdn_backward(saved, dy, gain, p, tag):
    x, h, proj, ab, q, k, v, gcb, btb, u, wd, attn, tinv, o, states, og = saved
    d = x.shape[1]
    nk = d // HEAD
    hv = 2 * nk
    dog = _mm(dy, p["wo"], tb=True, name=f"{tag}_d_og")
    dwo = _mm(og, dy, ta=True, out_dtype=F32, name=f"{tag}_d_wo")
    do, dgate, dogain = _gdn_out_bwd(o, proj, 2, p["o_gain"], dog, f"{tag}_onorm_bwd")
    du, dw, dattn, dq_s, dk_s, dg_s = _delta_scan_bwd(u, wd, attn, q, k, gcb, states, do, f"{tag}_delta_scan_bwd")
    dq, dk, dv, dgcb, dbtb = _delta_prep_bwd(q, k, v, gcb, btb, tinv, du, dw, dattn, dq_s, dk_s, dg_s, f"{tag}_delta_prep_bwd")
    dab, dalog, ddt = _gates_bwd(ab, p["a_log"], p["dt_bias"], dgcb, dbtb, hv, f"{tag}_gates_bwd")
    dpq, dcq = _gdn_conv_bwd(proj, p["cw"], dq, "q", 0, nk, f"{tag}_conv_q_bwd")
    dpk, dck = _gdn_conv_bwd(proj, p["cw"], dk, "k", nk, nk, f"{tag}_conv_k_bwd")
    dpv, dcv = _gdn_conv_bwd(proj, p["cw"], dv, "v", 2 * nk, hv, f"{tag}_conv_v_bwd")
    dproj = jnp.concatenate([dpq, dpk, dpv, dgate], axis=1)
    dcw = jnp.concatenate([dcq, dck, dcv], axis=1)
    dh = _mm(dproj, p["wmain"], tb=True, out_dtype=F32, name=f"{tag}_d_h_main")
    dh = _mm(dab, p["wab"], tb=True, res=dh, out_dtype=F32, name=f"{tag}_d_h_ab")
    dwmain = _mm(h, dproj, ta=True, out_dtype=F32, name=f"{tag}_d_wmain")
    dwab = _mm(h, dab, ta=True, out_dtype=F32, name=f"{tag}_d_wab")
    dx, dgain = _rmsnorm_bwd(x, gain, dh, dy, f"{tag}_norm_bwd")
    return dx, dict(gain=dgain, wmain=dwmain, wab=dwab, cw=dcw, a_log=dalog, dt_bias=ddt, o_gain=dogain, wo=dwo)


def _resolve(entry, after):
    return entry(after) if callable(entry) else entry


def _local_step(x, target, w, sink=None):
    depth = len(w["ffn"])
    tape = []
    for i in range(depth):
        kind, j = i % 3, i // 3
        gain = w["mix_norm"][i:i + 1]
        if kind == 0:
            x, saved = _att_forward(x, gain, _resolve(w["att"][j], x), f"l{i}_att")
        elif kind == 1:
            x_in = x
            pw = _resolve(w["pool"][j], x)
            x = _pool_fwd(x_in, gain, pw["w"], pw["scale"], f"l{i}_pool")
            saved = x_in
        else:
            x, saved = _gdn_forward(x, gain, _resolve(w["gdn"][j], x), f"l{i}_gdn")
        f = _resolve(w["ffn"][i], x)
        x, fsaved = _ffn_forward(x, w["ffn_norm"][i:i + 1], f["wu"], f["wg"], f["cw"], f["wd"], f"l{i}_ffn")
        tape.append((saved, fsaved))
    dy, loss_row = _loss_head(x, target, "loss_head")
    grads = dict(mix=[None] * depth, ffn=[None] * depth)
    zero = 0.0
    for i in reversed(range(depth)):
        kind, j = i % 3, i // 3
        saved, fsaved = tape[i]
        f = _resolve(w["ffn"][i], dy)
        dy, grads["ffn"][i] = _ffn_backward(fsaved, dy, w["ffn_norm"][i:i + 1] + zero, f["wu"], f["wg"], f["cw"], f["wd"], f"l{i}_ffn")
        if sink is not None:
            zero = zero + sink("ffn", i, grads["ffn"][i])
        gain = w["mix_norm"][i:i + 1] + zero
        if kind == 0:
            dy, grads["mix"][i] = _att_backward(saved, dy, gain, _resolve(w["att"][j], dy), f"l{i}_att")
        elif kind == 1:
            pw = _resolve(w["pool"][j], dy)
            dy, dgain, dw4, dscale = _pool_bwd(saved, gain, pw["w"], pw["scale"], dy, f"l{i}_pool_bwd")
            grads["mix"][i] = dict(gain=dgain, w=dw4, scale=dscale)
        else:
            dy, grads["mix"][i] = _gdn_backward(saved, dy, gain, _resolve(w["gdn"][j], dy), f"l{i}_gdn")
        if sink is not None:
            zero = zero + sink(("att", "pool", "gdn")[kind], i, grads["mix"][i])
    return loss_row, dy, grads


MESH = pl.DeviceIdType.MESH
ANY = pl.BlockSpec(memory_space=pl.ANY)


def _position():
    return tuple(lax.axis_index(a) for a in AXES)


def _flip(pos, rel):
    return tuple(1 - p if (rel >> (2 - i)) & 1 else p for i, p in enumerate(pos))


def _index(pos):
    return 4 * pos[0] + 2 * pos[1] + pos[2]


def _all_gather(arr, name):
    def body(x_ref, o_ref, send, recv, local):
        me = _position()
        mine = pltpu.make_async_copy(x_ref, o_ref.at[_index(me)], local)
        mine.start()
        copies = []
        for rel in range(1, N_DEV):
            cp = pltpu.make_async_remote_copy(
                src_ref=x_ref, dst_ref=o_ref.at[_index(me)], send_sem=send.at[rel - 1], recv_sem=recv.at[rel - 1],
                device_id=_flip(me, rel), device_id_type=MESH)
            cp.start()
            copies.append(cp)
        for cp in copies:
            cp.wait()
        mine.wait()

    return pl.pallas_call(
        body, name=name, in_specs=[ANY], out_specs=ANY, out_shape=SDS((N_DEV,) + arr.shape, arr.dtype),
        scratch_shapes=[pltpu.SemaphoreType.DMA((N_DEV - 1,)), pltpu.SemaphoreType.DMA((N_DEV - 1,)),
                        pltpu.SemaphoreType.DMA(())])(arr)


def _exchange(arr, name):
    def body(x_ref, o_ref, send, recv, local):
        me = _position()
        mine = pltpu.make_async_copy(x_ref.at[_index(me)], o_ref.at[_index(me)], local)
        mine.start()
        copies = []
        for rel in range(1, N_DEV):
            peer = _flip(me, rel)
            cp = pltpu.make_async_remote_copy(
                src_ref=x_ref.at[_index(peer)], dst_ref=o_ref.at[_index(me)], send_sem=send.at[rel - 1],
                recv_sem=recv.at[rel - 1], device_id=peer, device_id_type=MESH)
            cp.start()
            copies.append(cp)
        for cp in copies:
            cp.wait()
        mine.wait()

    return pl.pallas_call(
        body, name=name, in_specs=[ANY], out_specs=ANY, out_shape=SDS(arr.shape, arr.dtype),
        scratch_shapes=[pltpu.SemaphoreType.DMA((N_DEV - 1,)), pltpu.SemaphoreType.DMA((N_DEV - 1,)),
                        pltpu.SemaphoreType.DMA(())])(arr)


HBM = pl.BlockSpec(memory_space=pltpu.HBM)
SEM = pl.BlockSpec(memory_space=pltpu.SEMAPHORE)
EFFECT = pltpu.SideEffectType.DATAFLOW_SIDE_EFFECTING


def _split_copies(x_ref, land_ref, send, recv, scatter):
    me = _position()
    copies = []
    for rel in range(1, N_DEV):
        peer = _flip(me, rel)
        copies.append(pltpu.make_async_remote_copy(
            src_ref=x_ref.at[_index(peer)] if scatter else x_ref, dst_ref=land_ref.at[_index(me)],
            send_sem=send.at[rel - 1], recv_sem=recv.at[rel - 1], device_id=peer, device_id_type=MESH))
    return copies


def _copies_start(arr, scatter, name):
    shape = arr.shape if scatter else (N_DEV,) + arr.shape

    def body(x_ref, land_ref, send, recv, x_thru, land_thru, token):
        for cp in _split_copies(x_ref, land_ref, send, recv, scatter):
            cp.start()
        token[...] = jnp.zeros_like(token)

    sems = pltpu.SemaphoreType.DMA((N_DEV - 1,))
    send, recv, x_thru, land_thru, token = pl.pallas_call(
        body, name=name,
        out_shape=(sems, sems, pltpu.HBM(arr.shape, arr.dtype), pltpu.HBM(shape, arr.dtype), SDS((8, LANE), F32)),
        in_specs=(HBM, HBM), out_specs=(SEM, SEM, HBM, HBM, pl.BlockSpec(memory_space=pltpu.VMEM)),
        input_output_aliases={0: 2, 1: 3}, compiler_params=pltpu.CompilerParams(has_side_effects=EFFECT),
    )(pltpu.with_memory_space_constraint(arr, pltpu.HBM), pltpu.with_memory_space_constraint(lax.empty(shape, arr.dtype), pltpu.HBM))
    return (send, recv, x_thru, land_thru, scatter, name), token[0, 0]


def _copies_wait(handle, after):
    send, recv, x_thru, land_thru, scatter, name = handle

    def body(x_ref, land_ref, send_ref, recv_ref, after_ref, x_dead, got_ref):
        for cp in _split_copies(x_ref, land_ref, send_ref, recv_ref, scatter):
            cp.wait_send()
            cp.wait_recv()

    return pl.pallas_call(
        body, name=name + "_wait",
        out_shape=(pltpu.HBM(x_thru.shape, x_thru.dtype), pltpu.HBM(land_thru.shape, land_thru.dtype)),
        in_specs=(HBM, HBM, SEM, SEM, ANY), out_specs=(HBM, HBM), input_output_aliases={0: 0, 1: 1},
        compiler_params=pltpu.CompilerParams(has_side_effects=EFFECT),
    )(x_thru, land_thru, send, recv, after)[1]


def _with_own(got, own):
    return lax.dynamic_update_index_in_dim(got, own.astype(got.dtype), _index(_position()), 0)


def _adamw(parts, w, m, v, name):
    if not isinstance(parts, (list, tuple)):
        parts = [parts]
    layers = len(parts)
    r, c = parts[0].shape[1:]
    assert w.shape == (layers * r, c), (w.shape, parts[0].shape, layers)
    tr = _tile(r, 128, 16)
    per = r // tr
    c1 = 1.0 / (1.0 - ADAM_B1 ** ADAM_STEP)
    c2 = 1.0 / (1.0 - ADAM_B2 ** ADAM_STEP)

    def body(*refs):
        p_refs = refs[:layers]
        w_ref, m_ref, v_ref, g_ref, d_ref, nm_ref, nv_ref = refs[layers:]

        def update(p_ref):
            g = p_ref[0].astype(F32)
            for s in range(1, N_DEV):
                g = g + p_ref[s].astype(F32)
            nm = ADAM_B1 * m_ref[...] + (1.0 - ADAM_B1) * g
            nv = ADAM_B2 * v_ref[...] + (1.0 - ADAM_B2) * (g * g)
            g_ref[...] = g
            nm_ref[...] = nm
            nv_ref[...] = nv
            d_ref[...] = -ADAM_LR * ((nm * c1) / (jnp.sqrt(nv * c2) + ADAM_EPS) + ADAM_WD * w_ref[...])

        if layers == 1:
            update(p_refs[0])
        else:
            for j in range(layers):
                pl.when(pl.program_id(0) == j)(functools.partial(update, p_refs[j]))

    p_specs = [pl.BlockSpec((N_DEV, tr, c), functools.partial(lambda l, i, j: (0, jnp.where(l == j, i, 0), 0), j=j))
               for j in range(layers)]
    row = pl.BlockSpec((tr, c), lambda l, i: (l * per + i, 0))
    return pl.pallas_call(
        body, name=name, grid=(layers, per), in_specs=p_specs + [row, row, row],
        out_specs=[row] * 4, out_shape=[SDS(w.shape, F32)] * 4, compiler_params=_params("arbitrary", "arbitrary"))(*parts, w, m, v)


PACK = 8 * LANE


def _pack(arrs):
    flat = []
    for a in arrs:
        a = a.reshape(-1).astype(F32)
        flat.append(jnp.pad(a, (0, (-a.shape[0]) % PACK)))
    return jnp.concatenate(flat).reshape(-1, LANE)


def _unpack(packed, shapes, lead=()):
    flat = packed.reshape(lead + (-1,))
    out, off = [], 0
    for shp in shapes:
        n = int(np.prod(shp))
        out.append(flat[..., off:off + n].reshape(lead + tuple(shp)))
        off += n + (-n) % PACK
    return out


def _pad_to(a, axis, size):
    pad = [(0, 0)] * a.ndim
    pad[axis] = (0, size - a.shape[axis])
    return jnp.pad(a, pad)


def _cols_from_shards(g):
    return jnp.transpose(g, (1, 0, 2)).reshape(g.shape[1], -1)


def _cols_to_shards(a):
    c = a.shape[-1] // N_DEV
    a = a.reshape(a.shape[:-1] + (N_DEV, c))
    return jnp.moveaxis(a, -2, 0)


def _rows_to_shards(a):
    r = a.shape[-2] // N_DEV
    a = a.reshape(a.shape[:-2] + (N_DEV, r, a.shape[-1]))
    return jnp.moveaxis(a, -3, 0)


WEIGHTS = ("mix_norm", "ffn_norm", "att_w_qkv", "att_q_gain", "att_k_gain", "att_rel_bias", "att_w_o", "pool_w",
           "pool_scale", "gdn_w_in", "gdn_conv", "gdn_a_log", "gdn_dt_bias", "gdn_o_gain", "gdn_w_o", "ffn_w_up",
           "ffn_conv", "ffn_w_down")
REPLICATED = ("mix_norm", "ffn_norm", "att_q_gain", "att_k_gain", "pool_scale", "gdn_a_log", "gdn_dt_bias", "gdn_o_gain")
SMALL_SHARDED = ("att_rel_bias", "gdn_conv", "ffn_conv")
BIG = ("att_w_qkv", "att_w_o", "pool_w", "gdn_w_in", "gdn_w_o", "ffn_w_up", "ffn_w_down")
KEEP_F32 = ("pool_w",)


def _memo(build):
    cache = []

    def entry(after):
        if not cache:
            cache.append(build(after))
        return cache[0]

    return entry


def _assemble_weights(w, get, small, f):
    d = w["mix_norm"].shape[1]
    nk = d // HEAD
    hv = 2 * nk
    fp = -(-f // FF_PAD) * FF_PAD
    out = dict(mix_norm=w["mix_norm"], ffn_norm=w["ffn_norm"], att=[], pool=[], gdn=[], ffn=[])

    def att(j, after):
        return dict(wqkv=_cols_from_shards(get("att_w_qkv", j, after)), wo=get("att_w_o", j, after).reshape(d, d),
                    qg=w["att_q_gain"][j:j + 1], kg=w["att_k_gain"][j:j + 1], rel=small["att_rel_bias"][j])

    def pool(j, after):
        g = get("pool_w", j, after)
        return dict(w=jnp.transpose(g, (1, 0, 2, 3)).reshape(g.shape[1], g.shape[3], g.shape[3]),
                    scale=w["pool_scale"][j:j + 1])

    def gdn(j, after):
        win = _cols_from_shards(get("gdn_w_in", j, after))
        nm = 6 * d
        wab = jnp.concatenate([_pad_to(win[:, nm:nm + hv], 1, LANE), _pad_to(win[:, nm + hv:], 1, LANE)], axis=1)
        return dict(wmain=win[:, :nm], wab=wab, cw=_pad_to(small["gdn_conv"][j], 0, 8),
                    a_log=_pad_to(w["gdn_a_log"][j:j + 1], 1, LANE), dt_bias=_pad_to(w["gdn_dt_bias"][j:j + 1], 1, LANE),
                    o_gain=w["gdn_o_gain"][j:j + 1], wo=get("gdn_w_o", j, after).reshape(2 * d, d))

    def ffn(i, after):
        wup = _cols_from_shards(get("ffn_w_up", i, after))
        return dict(wu=_pad_to(wup[:, :f], 1, fp), wg=_pad_to(wup[:, f:], 1, fp),
                    cw=_pad_to(_pad_to(small["ffn_conv"][i], 0, 8), 1, fp),
                    wd=_pad_to(get("ffn_w_down", i, after).reshape(f, d), 0, fp))

    for key, build, count in (("att", att, w["att_w_qkv"].shape[0]), ("pool", pool, w["pool_w"].shape[0]),
                              ("gdn", gdn, w["gdn_w_in"].shape[0]), ("ffn", ffn, w["ffn_w_up"].shape[0])):
        out[key] = [_memo(functools.partial(build, j)) for j in range(count)]
    return out


def _full_gradients(grads, w, f):
    d = w["mix_norm"].shape[1]
    nk = d // HEAD
    hv = 2 * nk
    depth = len(grads["ffn"])
    att = [grads["mix"][i] for i in range(depth) if i % 3 == 0]
    pool = [grads["mix"][i] for i in range(depth) if i % 3 == 1]
    gdn = [grads["mix"][i] for i in range(depth) if i % 3 == 2]
    ffn = grads["ffn"]
    win = [jnp.concatenate([g["wmain"], g["wab"][:, :hv], g["wab"][:, LANE:LANE + hv]], axis=1) for g in gdn]
    return dict(
        mix_norm=jnp.concatenate([g["gain"] for g in grads["mix"]], axis=0),
        ffn_norm=jnp.concatenate([g["gain"] for g in ffn], axis=0),
        att_w_qkv=jnp.stack([g["wqkv"] for g in att]),
        att_q_gain=jnp.concatenate([g["qg"] for g in att], axis=0),
        att_k_gain=jnp.concatenate([g["kg"] for g in att], axis=0),
        att_rel_bias=jnp.stack([g["rel"] for g in att]),
        att_w_o=jnp.stack([g["wo"] for g in att]),
        pool_w=jnp.stack([g["w"] for g in pool]),
        pool_scale=jnp.concatenate([g["scale"] for g in pool], axis=0),
        gdn_w_in=jnp.stack(win),
        gdn_conv=jnp.stack([g["cw"][:GDN_CONV] for g in gdn]),
        gdn_a_log=jnp.concatenate([g["a_log"][:, :hv] for g in gdn], axis=0),
        gdn_dt_bias=jnp.concatenate([g["dt_bias"][:, :hv] for g in gdn], axis=0),
        gdn_o_gain=jnp.concatenate([g["o_gain"] for g in gdn], axis=0),
        gdn_w_o=jnp.stack([g["wo"] for g in gdn]),
        ffn_w_up=jnp.stack([jnp.concatenate([g["wu"][:, :f], g["wg"][:, :f]], axis=1) for g in ffn]),
        ffn_conv=jnp.stack([g["cw"][:FFN_CONV, :f] for g in ffn]),
        ffn_w_down=jnp.stack([g["wd"][:f] for g in ffn]))


ROW_SHARDED = ("att_w_o", "gdn_w_o", "ffn_w_down")


def _to_shards(name, full):
    if name == "pool_w":
        r = full.shape[2] // N_DEV
        a = full.reshape(full.shape[:2] + (N_DEV, r, full.shape[3]))
        return jnp.moveaxis(a, 2, 0)
    return _rows_to_shards(full) if name in ROW_SHARDED else _cols_to_shards(full)


def kernel(x, mix_norm, ffn_norm, att_w_qkv, att_q_gain, att_k_gain, att_rel_bias, att_w_o, pool_w, pool_scale, gdn_w_in, gdn_conv, gdn_a_log, gdn_dt_bias, gdn_o_gain, gdn_w_o, ffn_w_up, ffn_conv, ffn_w_down, loss_target, m_mix_norm, m_ffn_norm, m_att_w_qkv, m_att_q_gain, m_att_k_gain, m_att_rel_bias, m_att_w_o, m_pool_w, m_pool_scale, m_gdn_w_in, m_gdn_conv, m_gdn_a_log, m_gdn_dt_bias, m_gdn_o_gain, m_gdn_w_o, m_ffn_w_up, m_ffn_conv, m_ffn_w_down, v_mix_norm, v_ffn_norm, v_att_w_qkv, v_att_q_gain, v_att_k_gain, v_att_rel_bias, v_att_w_o, v_pool_w, v_pool_scale, v_gdn_w_in, v_gdn_conv, v_gdn_a_log, v_gdn_dt_bias, v_gdn_o_gain, v_gdn_w_o, v_ffn_w_up, v_ffn_conv, v_ffn_w_down):
    w = dict(zip(WEIGHTS, (mix_norm, ffn_norm, att_w_qkv, att_q_gain, att_k_gain, att_rel_bias, att_w_o, pool_w, pool_scale, gdn_w_in, gdn_conv, gdn_a_log, gdn_dt_bias, gdn_o_gain, gdn_w_o, ffn_w_up, ffn_conv, ffn_w_down)))
    m = dict(zip(WEIGHTS, (m_mix_norm, m_ffn_norm, m_att_w_qkv, m_att_q_gain, m_att_k_gain, m_att_rel_bias, m_att_w_o, m_pool_w, m_pool_scale, m_gdn_w_in, m_gdn_conv, m_gdn_a_log, m_gdn_dt_bias, m_gdn_o_gain, m_gdn_w_o, m_ffn_w_up, m_ffn_conv, m_ffn_w_down)))
    v = dict(zip(WEIGHTS, (v_mix_norm, v_ffn_norm, v_att_w_qkv, v_att_q_gain, v_att_k_gain, v_att_rel_bias, v_att_w_o, v_pool_w, v_pool_scale, v_gdn_w_in, v_gdn_conv, v_gdn_a_log, v_gdn_dt_bias, v_gdn_o_gain, v_gdn_w_o, v_ffn_w_up, v_ffn_conv, v_ffn_w_down)))

    me = _index(_position())
    d = mix_norm.shape[1]
    hv = 2 * (d // HEAD)
    f = ffn_w_down.shape[1] * N_DEV
    depth = ffn_w_up.shape[0]

    small_shapes = [w[n].shape for n in SMALL_SHARDED]
    small_g = _all_gather(_pack([w[n] for n in SMALL_SHARDED]), "gather_small")
    small = {}
    for n, a in zip(SMALL_SHARDED, _unpack(small_g, small_shapes, lead=(N_DEV,))):
        small[n] = jnp.moveaxis(a, 0, -2).reshape(a.shape[1:-1] + (N_DEV * a.shape[-1],))
    order = []
    for i in range(depth):
        order += [[("att_w_qkv", i // 3), ("att_w_o", i // 3)], [("pool_w", i // 3)], [("gdn_w_in", i // 3), ("gdn_w_o", i // 3)]][i % 3]
        order += [("ffn_w_up", i), ("ffn_w_down", i)]
    after_small = small_g[0, 0, 0] * 0.0
    local = {(n, j): (w[n][j] + after_small if n in KEEP_F32 else (w[n][j] + after_small).astype(BF)) for n, j in order}
    arriving, zero = {}, 0.0
    for n, j in order:
        arriving[(n, j)], tok = _copies_start(local[(n, j)], False, f"gather_{n}_{j}")
        zero = zero + tok

    def get(n, j, after):
        return _with_own(_copies_wait(arriving[(n, j)], after), local[(n, j)])

    ordered = dict(w, mix_norm=mix_norm + zero, ffn_norm=ffn_norm + zero)
    full = _assemble_weights(ordered, get, small, f)

    leaving = {}

    def sink(kind, i, g):
        j = i // 3
        if kind == "ffn":
            pieces = [("ffn_w_up", i, _cols_to_shards(jnp.concatenate([g["wu"][:, :f], g["wg"][:, :f]], axis=1))),
                      ("ffn_w_down", i, _rows_to_shards(g["wd"][:f]))]
        elif kind == "att":
            pieces = [("att_w_qkv", j, _cols_to_shards(g["wqkv"])), ("att_w_o", j, _rows_to_shards(g["wo"]))]
        elif kind == "pool":
            pieces = [("pool_w", j, _to_shards("pool_w", g["w"][None])[:, 0])]
        else:
            win = jnp.concatenate([g["wmain"], g["wab"][:, :hv], g["wab"][:, LANE:LANE + hv]], axis=1)
            pieces = [("gdn_w_in", j, _cols_to_shards(win)), ("gdn_w_o", j, _rows_to_shards(g["wo"]))]
        tok = 0.0
        for n, l, shards in pieces:
            shards = shards if n in KEEP_F32 else shards.astype(BF)
            handle, t = _copies_start(shards, True, f"exchange_{n}_{l}")
            leaving[(n, l)] = (handle, lax.dynamic_index_in_dim(shards, me, 0, keepdims=False))
            tok = tok + t
        return tok

    loss_row, dx, grads = _local_step(x[0], loss_target[0], full, sink)
    gfull = _full_gradients(grads, w, f)

    out = {}
    for n in BIG:
        c = w[n].shape[-1]
        parts = []
        for l in range(w[n].shape[0]):
            handle, own = leaving[(n, l)]
            parts.append(_with_own(_copies_wait(handle, dx), own).reshape(N_DEV, -1, c))
        res = _adamw(parts, w[n].reshape(-1, c), m[n].reshape(-1, c), v[n].reshape(-1, c), f"adamw_{n}")
        out[n] = [a.reshape(w[n].shape) for a in res]
    sparts = _exchange(jnp.stack([_pack([_to_shards(n, gfull[n])[k] for n in SMALL_SHARDED]) for k in range(N_DEV)]),
                       "exchange_small")
    res = _adamw(sparts, *[_pack([t[n] for n in SMALL_SHARDED]) for t in (w, m, v)], "adamw_small")
    for n, *vals in zip(SMALL_SHARDED, *[_unpack(a, small_shapes) for a in res]):
        out[n] = vals
    rep_shapes = [w[n].shape for n in REPLICATED]
    rparts = _all_gather(_pack([gfull[n] for n in REPLICATED] + [loss_row[:, 0:1]]), "gather_replicated_grads")
    pad = jnp.zeros((1, 1), F32)
    res = _adamw(rparts, *[_pack([t[n] for n in REPLICATED] + [pad]) for t in (w, m, v)], "adamw_replicated")
    for n, *vals in zip(REPLICATED, *[_unpack(a, rep_shapes) for a in res]):
        out[n] = vals
    loss = jnp.sum(_unpack(rparts, rep_shapes + [(1, 1)], lead=(N_DEV,))[-1])
    return (loss, dx[None], *[out[n][0] for n in WEIGHTS], *[out[n][1] for n in WEIGHTS],
            *[out[n][2] for n in WEIGHTS], *[out[n][3] for n in WEIGHTS])
```

```python
import functools

import numpy as np
import jax
import jax.numpy as jnp
from jax import lax
from jax.experimental import pallas as pl
from jax.experimental.pallas import tpu as pltpu

F32 = jnp.float32
BF = jnp.bfloat16
SDS = jax.ShapeDtypeStruct

EPS = 1e-6
MASK_VALUE = -1e30
CHUNK = 64
LEFT_CHUNKS = 8
BAND_LEFT = LEFT_CHUNKS * CHUNK
BAND = BAND_LEFT + CHUNK
MAX_REL = 256
NUM_REL = (CHUNK - 1) + MAX_REL + 1
HEAD = 128
LANE = 128
HALO = 16
POOL_WINDOWS = (2, 4, 8, 16)
GDN_CONV = 4
FFN_CONV = 3
FF_PAD = 512
N_DEV = 8
AXES = ("x", "y", "c")
VMEM_LIMIT = 56 * 1024 * 1024

ADAM_LR = 0.001
ADAM_B1 = 0.9
ADAM_B2 = 0.999
ADAM_EPS = 1e-08
ADAM_WD = 0.01
ADAM_STEP = 10

HI = lax.Precision.HIGHEST
NT = (((1,), (1,)), ((), ()))
TN = (((0,), (0,)), ((), ()))


def _tile(n, target, mult=LANE):
    if n <= target:
        return n
    t = (target // mult) * mult
    while t >= mult:
        if n % t == 0:
            return t
        t -= mult
    return n


def _params(*sem):
    return pltpu.CompilerParams(dimension_semantics=sem, vmem_limit_bytes=VMEM_LIMIT)


def _silu(x):
    return x / (1.0 + jnp.exp(-x))


@functools.partial(jax.custom_vjp, nondiff_argnums=(1,))
def _shift(x, k):
    return pltpu.roll(x, k % x.shape[0], axis=0)


def _shift_fwd(x, k):
    return _shift(x, k), None


def _shift_bwd(k, _, g):
    return (pltpu.roll(g, (-k) % g.shape[0], axis=0),)


_shift.defvjp(_shift_fwd, _shift_bwd)


def _mm(a, b, *, name, ta=False, tb=False, out_dtype=BF, res=None, tm=1024, tn=1024, tk=2048):
    m, k = (a.shape[1], a.shape[0]) if ta else a.shape
    n, kb = (b.shape[0], b.shape[1]) if tb else (b.shape[1], b.shape[0])
    assert k == kb, (a.shape, b.shape, ta, tb)
    if ta or a.dtype != BF:
        tk = min(tk, 1024)
    tm, tn, tk = _tile(m, tm), _tile(n, tn), _tile(k, tk)
    nk = k // tk
    dims = (((0 if ta else 1,), (1 if tb else 0,)), ((), ()))

    def body(*refs):
        if res is None:
            a_ref, b_ref, o_ref, acc = refs
        else:
            a_ref, b_ref, r_ref, o_ref, acc = refs
        prod = lax.dot_general(a_ref[...].astype(BF), b_ref[...].astype(BF), dims, preferred_element_type=F32)
        if nk == 1:
            if res is not None:
                prod = prod + r_ref[...].astype(F32)
            o_ref[...] = prod.astype(out_dtype)
            return
        kk = pl.program_id(2)

        @pl.when(kk == 0)
        def _():
            acc[...] = prod

        @pl.when(kk > 0)
        def _():
            acc[...] += prod

        @pl.when(kk == nk - 1)
        def _():
            r = acc[...]
            if res is not None:
                r = r + r_ref[...].astype(F32)
            o_ref[...] = r.astype(out_dtype)

    a_spec = pl.BlockSpec((tk, tm), lambda i, j, q: (q, i)) if ta else pl.BlockSpec((tm, tk), lambda i, j, q: (i, q))
    b_spec = pl.BlockSpec((tn, tk), lambda i, j, q: (j, q)) if tb else pl.BlockSpec((tk, tn), lambda i, j, q: (q, j))
    o_spec = pl.BlockSpec((tm, tn), lambda i, j, q: (i, j))
    ins, specs = [a, b], [a_spec, b_spec]
    if res is not None:
        ins.append(res)
        specs.append(o_spec)
    return pl.pallas_call(
        body, name=name, grid=(m // tm, n // tn, nk), in_specs=specs, out_specs=o_spec,
        out_shape=SDS((m, n), out_dtype), scratch_shapes=[pltpu.VMEM((tm, tn), F32)],
        compiler_params=_params("parallel", "parallel", "arbitrary"))(*ins)


def _rms(x, gain):
    return x * lax.rsqrt(jnp.mean(x * x, axis=-1, keepdims=True) + EPS) * gain


def _rmsnorm_fwd(x, gain, name):
    s, d = x.shape
    ts = _tile(s, 256, LANE)

    def body(x_ref, g_ref, o_ref, t_ref):
        h = _rms(x_ref[...], g_ref[...])
        o_ref[...] = h.astype(BF)
        t_ref[...] = h.T.astype(BF)

    return pl.pallas_call(
        body, name=name, grid=(s // ts,),
        in_specs=[pl.BlockSpec((ts, d), lambda i: (i, 0)), pl.BlockSpec((1, d), lambda i: (0, 0))],
        out_specs=[pl.BlockSpec((ts, d), lambda i: (i, 0)), pl.BlockSpec((d, ts), lambda i: (0, i))],
        out_shape=[SDS((s, d), BF), SDS((d, s), BF)], compiler_params=_params("parallel"))(x, gain)


def _rmsnorm_bwd(x, gain, dh, dres, name):
    s, d = x.shape
    ts = _tile(s, 256, 16)

    def body(x_ref, g_ref, dh_ref, dr_ref, dx_ref, dxb_ref, dg_ref):
        i = pl.program_id(0)
        _, vjp = jax.vjp(_rms, x_ref[...], g_ref[...])
        dx, dg = vjp(dh_ref[...].astype(F32))
        dx = dr_ref[...] + dx
        dx_ref[...] = dx
        dxb_ref[...] = dx.astype(BF)

        @pl.when(i == 0)
        def _():
            dg_ref[...] = dg

        @pl.when(i > 0)
        def _():
            dg_ref[...] += dg

    row = pl.BlockSpec((ts, d), lambda i: (i, 0))
    vec = pl.BlockSpec((1, d), lambda i: (0, 0))
    return pl.pallas_call(
        body, name=name, grid=(s // ts,), in_specs=[row, vec, row, row], out_specs=[row, row, vec],
        out_shape=[SDS((s, d), F32), SDS((s, d), BF), SDS((1, d), F32)], compiler_params=_params("arbitrary"))(x, gain, dh, dres)


def _ffn_act_tile(u_ext, g_ext, cw):
    acc = u_ext * cw[2:3] + _shift(u_ext, 1) * cw[1:2] + _shift(u_ext, 2) * cw[0:1]
    return _silu(acc) * g_ext


def _halo_index(rows_per_block):
    per = rows_per_block // HALO
    return lambda rb: jnp.maximum(rb * per - 1, 0)


FFN_SUB = 256


def _ext_rows(cur, prev, nxt, r0, rows, tile, keep_prev=1.0, keep_next=1.0):
    n = cur.shape[0]
    parts = []
    if r0 > 0:
        parts.append(cur[pl.ds(r0 - HALO, HALO + rows), tile].astype(F32))
    else:
        head = jnp.zeros((HALO, LANE), F32) if prev is None else prev[:, tile].astype(F32) * keep_prev
        parts += [head, cur[pl.ds(0, rows), tile].astype(F32)]
    if nxt is not False:
        if r0 + rows < n:
            parts.append(cur[pl.ds(r0 + rows, HALO), tile].astype(F32))
        else:
            parts.append(jnp.zeros((HALO, LANE), F32) if nxt is None else nxt[:, tile].astype(F32) * keep_next)
    return jnp.concatenate(parts, axis=0)


def _ffn_act_fwd(u, g, cw, name):
    s, f = u.shape
    r, tc = _tile(s, 512, HALO), _tile(f, 512)
    sub = _tile(r, FFN_SUB, HALO)
    hidx = _halo_index(r)

    def body(uc, uh, gc, cw_ref, o_ref):
        keep = jnp.where(pl.program_id(1) == 0, 0.0, 1.0)
        for lt in range(tc // LANE):
            tile = pl.ds(lt * LANE, LANE)
            c0, c1, c2 = cw_ref[0:1, tile], cw_ref[1:2, tile], cw_ref[2:3, tile]
            for r0 in range(0, r, sub):
                u_ext = _ext_rows(uc, uh, False, r0, sub, tile, keep)
                acc = u_ext * c2 + pltpu.roll(u_ext, 1, axis=0) * c1 + pltpu.roll(u_ext, 2, axis=0) * c0
                o_ref[pl.ds(r0, sub), tile] = (_silu(acc)[HALO:] * gc[pl.ds(r0, sub), tile].astype(F32)).astype(BF)

    cur = pl.BlockSpec((r, tc), lambda j, rb: (rb, j))
    return pl.pallas_call(
        body, name=name, grid=(f // tc, s // r),
        in_specs=[cur, pl.BlockSpec((HALO, tc), lambda j, rb: (hidx(rb), j)), cur,
                  pl.BlockSpec((8, tc), lambda j, rb: (0, j))],
        out_specs=cur, out_shape=SDS((s, f), BF), compiler_params=_params("parallel", "parallel"))(u, u, g, cw)


def _ffn_act_bwd(u, g, cw, da, name):
    s, f = u.shape
    r, tc = _tile(s, 512, HALO), _tile(f, 512)
    sub = _tile(r, FFN_SUB, HALO)
    nb = s // r
    per = r // HALO
    hidx = _halo_index(r)
    nidx = lambda rb: jnp.minimum((rb + 1) * per, s // HALO - 1)
    n = sub + 2 * HALO

    def body(uc, uh, un, gc, gn, dac, dan, cw_ref, du_ref, dg_ref, dcw_ref):
        rb = pl.program_id(1)
        first = jnp.where(rb == 0, 0.0, 1.0)
        last = jnp.where(rb == nb - 1, 0.0, 1.0)
        rows = lax.broadcasted_iota(jnp.int32, (n, LANE), 0)
        own = jnp.where((rows >= HALO) & (rows < HALO + sub), 1.0, 0.0)
        wrow = lax.broadcasted_iota(jnp.int32, (8, LANE), 0)

        @pl.when(rb == 0)
        def _():
            dcw_ref[...] = jnp.zeros_like(dcw_ref)

        for lt in range(tc // LANE):
            tile = pl.ds(lt * LANE, LANE)
            c0, c1, c2 = cw_ref[0:1, tile], cw_ref[1:2, tile], cw_ref[2:3, tile]
            dcw = jnp.zeros((8, LANE), F32)
            for r0 in range(0, r, sub):
                u_ext = _ext_rows(uc, uh, un, r0, sub, tile, first, last)
                g_ext = _ext_rows(gc, None, gn, r0, sub, tile, 1.0, last)
                da_ext = _ext_rows(dac, None, dan, r0, sub, tile, 1.0, last)
                sh1, sh2 = pltpu.roll(u_ext, 1, axis=0), pltpu.roll(u_ext, 2, axis=0)
                acc = u_ext * c2 + sh1 * c1 + sh2 * c0
                sg = 1.0 / (1.0 + jnp.exp(-acc))
                dg_ref[pl.ds(r0, sub), tile] = (da_ext * (acc * sg))[HALO:HALO + sub].astype(BF)
                dacc = da_ext * g_ext * (sg * (1.0 + acc * (1.0 - sg)))
                du = dacc * c2 + pltpu.roll(dacc, n - 1, axis=0) * c1 + pltpu.roll(dacc, n - 2, axis=0) * c0
                du_ref[pl.ds(r0, sub), tile] = du[HALO:HALO + sub].astype(BF)
                dm = dacc * own
                for j, tap in enumerate((sh2, sh1, u_ext)):
                    dcw = dcw + jnp.where(wrow == j, jnp.sum(dm * tap, axis=0, keepdims=True), 0.0)
            dcw_ref[:, tile] += dcw

    cur = pl.BlockSpec((r, tc), lambda j, rb: (rb, j))
    prev = pl.BlockSpec((HALO, tc), lambda j, rb: (hidx(rb), j))
    nxt = pl.BlockSpec((HALO, tc), lambda j, rb: (nidx(rb), j))
    wspec = pl.BlockSpec((8, tc), lambda j, rb: (0, j))
    return pl.pallas_call(
        body, name=name, grid=(f // tc, nb),
        in_specs=[cur, prev, nxt, cur, nxt, cur, nxt, wspec],
        out_specs=[cur, cur, wspec], out_shape=[SDS((s, f), BF), SDS((s, f), BF), SDS((8, f), F32)],
        compiler_params=_params("parallel", "arbitrary"))(u, u, u, g, g, da, da, cw)


def _loss_head(y, target, name):
    s, d = y.shape
    ts = _tile(s, 256, 16)

    def body(y_ref, t_ref, dy_ref, dyb_ref, l_ref):
        i = pl.program_id(0)
        err = y_ref[...] - t_ref[...]
        dy_ref[...] = err * (1.0 / d)
        dyb_ref[...] = (err * (1.0 / d)).astype(BF)
        part = jnp.zeros((1, LANE), F32) + 0.5 * jnp.sum(jnp.sum(err * err, axis=1, keepdims=True), axis=0, keepdims=True) / d

        @pl.when(i == 0)
        def _():
            l_ref[...] = part

        @pl.when(i > 0)
        def _():
            l_ref[...] += part

    row = pl.BlockSpec((ts, d), lambda i: (i, 0))
    return pl.pallas_call(
        body, name=name, grid=(s // ts,), in_specs=[row, row],
        out_specs=[row, row, pl.BlockSpec((1, LANE), lambda i: (0, 0))],
        out_shape=[SDS((s, d), F32), SDS((s, d), BF), SDS((1, LANE), F32)], compiler_params=_params("arbitrary"))(y, target)


def _rel_index():
    rel = BAND_LEFT + np.arange(CHUNK)[:, None] - np.arange(BAND)[None, :]
    return (np.clip(rel, -(CHUNK - 1), MAX_REL) + (CHUNK - 1)).reshape(1, CHUNK * BAND).astype(np.int32)


def _onehot(idx_row):
    rows = lax.broadcasted_iota(jnp.int32, (NUM_REL, idx_row.shape[1]), 0)
    return jnp.where(rows == idx_row, 1.0, 0.0).astype(F32)


def _bias_expand(rel_bias, name):
    h = rel_bias.shape[0]
    n = CHUNK * BAND
    tn = n // 8

    def body(rb_ref, idx_ref, o_ref):
        o_ref[...] = jnp.dot(rb_ref[...], _onehot(idx_ref[...]), precision=HI, preferred_element_type=F32)

    out = pl.pallas_call(
        body, name=name, grid=(n // tn,),
        in_specs=[pl.BlockSpec((h, NUM_REL), lambda j: (0, 0)), pl.BlockSpec((1, tn), lambda j: (0, j))],
        out_specs=pl.BlockSpec((h, tn), lambda j: (0, j)), out_shape=SDS((h, n), F32),
        compiler_params=_params("parallel"))(rel_bias, jnp.asarray(_rel_index()))
    return out.reshape(h, CHUNK, BAND)


def _bias_reduce(dbias, name):
    h = dbias.shape[0]
    n = CHUNK * BAND
    tn = n // 8

    def body(db_ref, idx_ref, o_ref):
        j = pl.program_id(0)
        part = lax.dot_general(db_ref[...], _onehot(idx_ref[...]), NT, precision=HI, preferred_element_type=F32)

        @pl.when(j == 0)
        def _():
            o_ref[...] = part

        @pl.when(j > 0)
        def _():
            o_ref[...] += part

    return pl.pallas_call(
        body, name=name, grid=(n // tn,),
        in_specs=[pl.BlockSpec((h, tn), lambda j: (0, j)), pl.BlockSpec((1, tn), lambda j: (0, j))],
        out_specs=pl.BlockSpec((h, NUM_REL), lambda j: (0, 0)), out_shape=SDS((h, NUM_REL), F32),
        compiler_params=_params("arbitrary"))(dbias.reshape(h, n), jnp.asarray(_rel_index()))


def _headnorm(x, gain):
    outs = []
    for hh in range(x.shape[1] // HEAD):
        xh = x[:, hh * HEAD:(hh + 1) * HEAD]
        outs.append(xh * lax.rsqrt(jnp.mean(xh * xh, axis=-1, keepdims=True) + EPS) * gain)
    return jnp.concatenate(outs, axis=1)


def _qkv_post_fwd(qkv, qg, kg, name):
    s, d3 = qkv.shape
    d = d3 // 3
    r = BAND_LEFT
    nb = s // r

    def body(x_ref, qg_ref, kg_ref, q_ref, k_ref, v_ref):
        i = pl.program_id(0)
        keep = jnp.where(i == 0, 0.0, 1.0)
        q_ref[...] = _headnorm(x_ref[:, 0:d].astype(F32), qg_ref[...]).astype(BF)
        k_ref[...] = (_headnorm(x_ref[:, d:2 * d].astype(F32), kg_ref[...]) * keep).astype(BF)
        v_ref[...] = (x_ref[:, 2 * d:].astype(F32) * keep).astype(BF)

    prev = lambda i: (jnp.maximum(i - 1, 0), 0)
    vec = pl.BlockSpec((1, HEAD), lambda i: (0, 0))
    return pl.pallas_call(
        body, name=name, grid=(nb + 1,),
        in_specs=[pl.BlockSpec((r, d3), prev), vec, vec],
        out_specs=[pl.BlockSpec((r, d), prev), pl.BlockSpec((r, d), lambda i: (i, 0)), pl.BlockSpec((r, d), lambda i: (i, 0))],
        out_shape=[SDS((s, d), BF), SDS((s + r, d), BF), SDS((s + r, d), BF)],
        compiler_params=_params("arbitrary"))(qkv, qg, kg)


def _qkv_post_bwd(qkv, qg, kg, dq, dkpad, dvpad, name):
    s, d3 = qkv.shape
    d = d3 // 3
    r = _tile(s, 256, 16)
    off = BAND_LEFT // r

    def body(x_ref, qg_ref, kg_ref, dq_ref, dk_ref, dv_ref, o_ref, dqg_ref, dkg_ref):
        i = pl.program_id(0)
        _, vq = jax.vjp(_headnorm, x_ref[:, 0:d].astype(F32), qg_ref[...])
        dxq, dqg = vq(dq_ref[...])
        _, vk = jax.vjp(_headnorm, x_ref[:, d:2 * d].astype(F32), kg_ref[...])
        dxk, dkg = vk(dk_ref[...])
        o_ref[:, 0:d] = dxq.astype(BF)
        o_ref[:, d:2 * d] = dxk.astype(BF)
        o_ref[:, 2 * d:] = dv_ref[...].astype(BF)

        @pl.when(i == 0)
        def _():
            dqg_ref[...] = dqg
            dkg_ref[...] = dkg

        @pl.when(i > 0)
        def _():
            dqg_ref[...] += dqg
            dkg_ref[...] += dkg

    vec = pl.BlockSpec((1, HEAD), lambda i: (0, 0))
    row3 = pl.BlockSpec((r, d3), lambda i: (i, 0))
    row = pl.BlockSpec((r, d), lambda i: (i, 0))
    padrow = pl.BlockSpec((r, d), lambda i: (i + off, 0))
    return pl.pallas_call(
        body, name=name, grid=(s // r,), in_specs=[row3, vec, vec, row, padrow, padrow],
        out_specs=[row3, vec, vec], out_shape=[SDS((s, d3), BF), SDS((1, HEAD), F32), SDS((1, HEAD), F32)],
        compiler_params=_params("arbitrary"))(qkv, qg, kg, dq, dkpad, dvpad)


ATT_QB = 256


def _att_probs(q, kw, bias, c0):
    sc = _bdot(q, kw, NTB) * (HEAD ** -0.5) + bias
    lane = lax.broadcasted_iota(jnp.int32, sc.shape, 2)
    chunk = lax.broadcasted_iota(jnp.int32, sc.shape, 0) + c0
    sc = jnp.where(lane + chunk * CHUNK >= BAND_LEFT, sc, MASK_VALUE)
    p = jnp.exp(sc - jnp.max(sc, axis=-1, keepdims=True))
    return p / jnp.sum(p, axis=-1, keepdims=True)


def _attn_fwd(q, kpad, vpad, bias, name):
    s, d = q.shape
    h = d // HEAD
    sp = kpad.shape[0]
    qb = _tile(s, ATT_QB, CHUNK)
    per = qb // CHUNK

    def body(q_ref, k_ref, v_ref, b_ref, o_ref):
        c0 = pl.program_id(1) * per
        wins = [pl.ds(pl.multiple_of((c0 + cc) * CHUNK, CHUNK), BAND) for cc in range(per)]
        kw = jnp.stack([k_ref[w, :] for w in wins])
        vw = jnp.stack([v_ref[w, :] for w in wins])
        p = _att_probs(q_ref[...].reshape(per, CHUNK, HEAD), kw, b_ref[0], c0)
        o_ref[...] = _bdot(p.astype(BF), vw, NNB).reshape(qb, HEAD).astype(BF)

    qspec = pl.BlockSpec((qb, HEAD), lambda hh, i: (i, hh))
    kspec = pl.BlockSpec((sp, HEAD), lambda hh, i: (0, hh))
    return pl.pallas_call(
        body, name=name, grid=(h, s // qb),
        in_specs=[qspec, kspec, kspec, pl.BlockSpec((1, CHUNK, BAND), lambda hh, i: (hh, 0, 0))],
        out_specs=qspec, out_shape=SDS((s, d), BF), compiler_params=_params("parallel", "arbitrary"))(q, kpad, vpad, bias)


def _attn_bwd(q, kpad, vpad, bias, do, name):
    s, d = q.shape
    h = d // HEAD
    sp = kpad.shape[0]
    qb = _tile(s, ATT_QB, CHUNK)
    per = qb // CHUNK
    scale = HEAD ** -0.5

    def body(q_ref, k_ref, v_ref, b_ref, do_ref, dq_ref, dk_ref, dv_ref, db_ref):
        i = pl.program_id(1)

        @pl.when(i == 0)
        def _():
            dk_ref[...] = jnp.zeros_like(dk_ref)
            dv_ref[...] = jnp.zeros_like(dv_ref)
            db_ref[...] = jnp.zeros_like(db_ref)

        c0 = i * per
        wins = [pl.ds(pl.multiple_of((c0 + cc) * CHUNK, CHUNK), BAND) for cc in range(per)]
        kw = jnp.stack([k_ref[w, :] for w in wins])
        vw = jnp.stack([v_ref[w, :] for w in wins])
        qc = q_ref[...].reshape(per, CHUNK, HEAD)
        doc = do_ref[...].astype(BF).reshape(per, CHUNK, HEAD)
        p = _att_probs(qc, kw, b_ref[0], c0)
        dp = _bdot(doc, vw, NTB)
        ds = p * (dp - jnp.sum(p * dp, axis=-1, keepdims=True))
        db_ref[0] += jnp.sum(ds, axis=0)
        dsb = (ds * scale).astype(BF)
        dq_ref[...] = _bdot(dsb, kw, NNB).reshape(qb, HEAD)

        def union(x):
            tot = None
            for cc in range(per):
                parts = [x[cc]]
                if cc:
                    parts.insert(0, jnp.zeros((cc * CHUNK, HEAD), F32))
                if cc < per - 1:
                    parts.append(jnp.zeros(((per - 1 - cc) * CHUNK, HEAD), F32))
                piece = jnp.concatenate(parts, axis=0) if len(parts) > 1 else parts[0]
                tot = piece if tot is None else tot + piece
            return tot

        span = pl.ds(pl.multiple_of(c0 * CHUNK, CHUNK), BAND + (per - 1) * CHUNK)
        dk_ref[span, :] += union(_bdot(dsb, qc, TNB))
        dv_ref[span, :] += union(_bdot(p.astype(BF), doc, TNB))

    qspec = pl.BlockSpec((qb, HEAD), lambda hh, i: (i, hh))
    kspec = pl.BlockSpec((sp, HEAD), lambda hh, i: (0, hh))
    bspec = pl.BlockSpec((1, CHUNK, BAND), lambda hh, i: (hh, 0, 0))
    return pl.pallas_call(
        body, name=name, grid=(h, s // qb), in_specs=[qspec, kspec, kspec, bspec, qspec],
        out_specs=[qspec, kspec, kspec, bspec],
        out_shape=[SDS((s, d), F32), SDS((sp, d), F32), SDS((sp, d), F32), SDS((h, CHUNK, BAND), F32)],
        compiler_params=_params("parallel", "arbitrary"))(q, kpad, vpad, bias, do)


def _pool_tile(x_ext, gain, w4, scale, row0):
    n, d = x_ext.shape
    dg = d // len(POOL_WINDOWS)
    pos = lax.broadcasted_iota(jnp.int32, (n, 1), 0) + row0
    hn = _rms(x_ext, gain) * jnp.where(pos >= 0, 1.0, 0.0)
    outs = []
    for gi, w in enumerate(POOL_WINDOWS):
        hg = hn[:, gi * dg:(gi + 1) * dg]
        acc, k = hg, 1
        while k < w:
            acc = acc + _shift(acc, k)
            k *= 2
        inv = 1.0 / jnp.clip(pos + 1, 1, w).astype(F32)
        pooled = acc * inv - hg
        outs.append(jnp.dot(pooled.astype(BF), w4[gi].astype(BF), preferred_element_type=F32))
    return jnp.concatenate(outs, axis=1) * scale


POOL_ROWS = 128


def _pool_fwd(x, gain, w4, scale, name):
    s, d = x.shape
    r = _tile(s, POOL_ROWS, HALO)
    hidx = _halo_index(r)

    def body(xc, xh, g_ref, w_ref, s_ref, o_ref):
        rb = pl.program_id(0)
        x_ext = jnp.concatenate([xh[...], xc[...]], axis=0)
        y = _pool_tile(x_ext, g_ref[...], [w_ref[gi] for gi in range(len(POOL_WINDOWS))], s_ref[...], rb * r - HALO)
        o_ref[...] = xc[...] + y[HALO:]

    cur = pl.BlockSpec((r, d), lambda rb: (rb, 0))
    vec = pl.BlockSpec((1, d), lambda rb: (0, 0))
    return pl.pallas_call(
        body, name=name, grid=(s // r,),
        in_specs=[cur, pl.BlockSpec((HALO, d), lambda rb: (hidx(rb), 0)), vec,
                  pl.BlockSpec(w4.shape, lambda rb: (0, 0, 0)), vec],
        out_specs=cur, out_shape=SDS((s, d), F32), compiler_params=_params("parallel"))(x, x, gain, w4, scale)


def _pool_bwd(x, gain, w4, scale, dy, name):
    s, d = x.shape
    r = _tile(s, POOL_ROWS, HALO)
    nb = s // r
    hidx = _halo_index(r)

    def body(xc, xh, g_ref, w_ref, s_ref, dy_ref, dx_ref, dg_ref, dw_ref, ds_ref, carry):
        step = pl.program_id(0)
        rb = nb - 1 - step
        x_ext = jnp.concatenate([xh[...], xc[...]], axis=0)
        fn = functools.partial(_pool_tile, row0=rb * r - HALO)
        _, vjp = jax.vjp(fn, x_ext, g_ref[...], [w_ref[gi] for gi in range(len(POOL_WINDOWS))], s_ref[...])
        ct = jnp.concatenate([jnp.zeros((HALO, d), F32), dy_ref[...]], axis=0)
        dx_ext, dg, dws, dsc = vjp(ct)

        @pl.when(step == 0)
        def _():
            carry[...] = jnp.zeros_like(carry)
            dg_ref[...] = jnp.zeros_like(dg_ref)
            dw_ref[...] = jnp.zeros_like(dw_ref)
            ds_ref[...] = jnp.zeros_like(ds_ref)

        dx_ref[...] = dy_ref[...] + dx_ext[HALO:]
        dx_ref[pl.ds(r - HALO, HALO), :] += carry[...]
        carry[...] = dx_ext[:HALO]
        dg_ref[...] += dg
        for gi, dw in enumerate(dws):
            dw_ref[gi] += dw
        ds_ref[...] += dsc

    cur = pl.BlockSpec((r, d), lambda t: (nb - 1 - t, 0))
    vec = pl.BlockSpec((1, d), lambda t: (0, 0))
    wspec = pl.BlockSpec(w4.shape, lambda t: (0, 0, 0))
    return pl.pallas_call(
        body, name=name, grid=(nb,),
        in_specs=[cur, pl.BlockSpec((HALO, d), lambda t: (hidx(nb - 1 - t), 0)), vec, wspec, vec, cur],
        out_specs=[cur, vec, wspec, vec],
        out_shape=[SDS((s, d), F32), SDS((1, d), F32), SDS(w4.shape, F32), SDS((1, d), F32)],
        scratch_shapes=[pltpu.VMEM((HALO, d), F32)], compiler_params=_params("arbitrary"))(x, x, gain, w4, scale, dy)


def _gdn_conv_tile(u_ext, cw, kind):
    acc = (u_ext * cw[3:4] + _shift(u_ext, 1) * cw[2:3] + _shift(u_ext, 2) * cw[1:2] + _shift(u_ext, 3) * cw[0:1])
    y = _silu(acc)
    if kind != "v":
        y = y * lax.rsqrt(jnp.sum(y * y, axis=-1, keepdims=True) + EPS)
    if kind == "q":
        y = y * (HEAD ** -0.5)
    return y


def _gdn_conv_fwd(proj, cw, kind, head0, nheads, name):
    s = proj.shape[0]
    r = _tile(s, 512, HALO)
    hidx = _halo_index(r)

    def body(uc, uh, cw_ref, o_ref):
        rb = pl.program_id(1)
        keep = jnp.where(rb == 0, 0.0, 1.0)
        u_ext = jnp.concatenate([uh[...].astype(F32) * keep, uc[...].astype(F32)], axis=0)
        o_ref[...] = _gdn_conv_tile(u_ext, cw_ref[...], kind)[HALO:].astype(BF)

    return pl.pallas_call(
        body, name=name, grid=(nheads, s // r),
        in_specs=[pl.BlockSpec((r, HEAD), lambda j, rb: (rb, head0 + j)),
                  pl.BlockSpec((HALO, HEAD), lambda j, rb: (hidx(rb), head0 + j)),
                  pl.BlockSpec((8, HEAD), lambda j, rb: (0, head0 + j))],
        out_specs=pl.BlockSpec((r, HEAD), lambda j, rb: (rb, j)), out_shape=SDS((s, nheads * HEAD), BF),
        compiler_params=_params("parallel", "parallel"))(proj, proj, cw)


def _gdn_conv_bwd(proj, cw, dy, kind, head0, nheads, name):
    s = proj.shape[0]
    r = _tile(s, 512, HALO)
    nb = s // r
    hidx = _halo_index(r)
    rep = dy.shape[1] // (nheads * HEAD)

    def body(*refs):
        uc, uh, cw_ref = refs[:3]
        dys = refs[3:3 + rep]
        du_ref, dcw_ref, carry = refs[3 + rep:]
        step = pl.program_id(1)
        rb = nb - 1 - step
        keep = jnp.where(rb == 0, 0.0, 1.0)
        zero = jnp.zeros((HALO, HEAD), F32)
        u_ext = jnp.concatenate([uh[...].astype(F32) * keep, uc[...].astype(F32)], axis=0)
        dyc = dys[0][...]
        for extra in dys[1:]:
            dyc = dyc + extra[...]
        _, vjp = jax.vjp(functools.partial(_gdn_conv_tile, kind=kind), u_ext, cw_ref[...])
        du_ext, dcw = vjp(jnp.concatenate([zero, dyc], axis=0))

        @pl.when(step == 0)
        def _():
            carry[...] = zero
            dcw_ref[...] = jnp.zeros_like(dcw_ref)

        du_ref[pl.ds(0, r - HALO), :] = du_ext[HALO:r].astype(BF)
        du_ref[pl.ds(r - HALO, HALO), :] = (du_ext[r:] + carry[...]).astype(BF)
        carry[...] = du_ext[:HALO]
        dcw_ref[...] += dcw

    dy_specs = [pl.BlockSpec((r, HEAD), functools.partial(lambda j, t, e: (nb - 1 - t, rep * j + e), e=e)) for e in range(rep)]
    return pl.pallas_call(
        body, name=name, grid=(nheads, nb),
        in_specs=[pl.BlockSpec((r, HEAD), lambda j, t: (nb - 1 - t, head0 + j)),
                  pl.BlockSpec((HALO, HEAD), lambda j, t: (hidx(nb - 1 - t), head0 + j)),
                  pl.BlockSpec((8, HEAD), lambda j, t: (0, head0 + j))] + dy_specs,
        out_specs=[pl.BlockSpec((r, HEAD), lambda j, t: (nb - 1 - t, j)), pl.BlockSpec((8, HEAD), lambda j, t: (0, j))],
        out_shape=[SDS((s, nheads * HEAD), BF), SDS((8, nheads * HEAD), F32)],
        scratch_shapes=[pltpu.VMEM((HALO, HEAD), F32)],
        compiler_params=_params("parallel", "arbitrary"))(proj, proj, cw, *([dy] * rep))


GATE_ROWS = 256


def _gates_tile(a, bt, a_log, dt_bias, hv):
    r = a.shape[0]
    z = a + dt_bias
    softplus = jnp.maximum(z, 0.0) + jnp.log(1.0 + jnp.exp(-jnp.abs(z)))
    g = -jnp.exp(a_log) * softplus
    ri = lax.broadcasted_iota(jnp.int32, (r, r), 0)
    ci = lax.broadcasted_iota(jnp.int32, (r, r), 1)
    same_chunk = jnp.right_shift(ri, 6) == jnp.right_shift(ci, 6)
    tri = jnp.where(same_chunk, jnp.where(ri >= ci, 1.0, 0.0), 0.0).astype(F32)
    gc = jnp.dot(tri, g, precision=HI, preferred_element_type=F32)
    beta = 1.0 / (1.0 + jnp.exp(-bt))
    er = lax.broadcasted_iota(jnp.int32, (LANE, hv * HEAD), 0)
    ec = lax.broadcasted_iota(jnp.int32, (LANE, hv * HEAD), 1)
    expand = jnp.where(er == jnp.right_shift(ec, 7), 1.0, 0.0).astype(F32)
    return (jnp.dot(gc, expand, precision=HI, preferred_element_type=F32),
            jnp.dot(beta, expand, precision=HI, preferred_element_type=F32))


def _gates_fwd(ab, a_log, dt_bias, hv, name):
    s = ab.shape[0]
    r = _tile(s, GATE_ROWS, CHUNK)

    def body(a_ref, b_ref, al_ref, dt_ref, gc_ref, bb_ref):
        gcb, btb = _gates_tile(a_ref[...], b_ref[...], al_ref[...], dt_ref[...], hv)
        gc_ref[...] = gcb
        bb_ref[...] = btb

    vec = pl.BlockSpec((1, LANE), lambda i: (0, 0))
    wide = pl.BlockSpec((r, hv * HEAD), lambda i: (i, 0))
    return pl.pallas_call(
        body, name=name, grid=(s // r,),
        in_specs=[pl.BlockSpec((r, LANE), lambda i: (i, 0)), pl.BlockSpec((r, LANE), lambda i: (i, 1)), vec, vec],
        out_specs=[wide, wide], out_shape=[SDS((s, hv * HEAD), F32)] * 2,
        compiler_params=_params("parallel"))(ab, ab, a_log, dt_bias)


def _gates_bwd(ab, a_log, dt_bias, dgcb, dbtb, hv, name):
    s = ab.shape[0]
    r = _tile(s, GATE_ROWS, CHUNK)

    def body(a_ref, b_ref, al_ref, dt_ref, dgc_ref, dbb_ref, dab_ref, dal_ref, ddt_ref):
        i = pl.program_id(0)
        _, vjp = jax.vjp(functools.partial(_gates_tile, hv=hv), a_ref[...], b_ref[...], al_ref[...], dt_ref[...])
        da, dbt, dal, ddt = vjp((dgc_ref[...], dbb_ref[...]))
        dab_ref[:, 0:LANE] = da
        dab_ref[:, LANE:] = dbt

        @pl.when(i == 0)
        def _():
            dal_ref[...] = dal
            ddt_ref[...] = ddt

        @pl.when(i > 0)
        def _():
            dal_ref[...] += dal
            ddt_ref[...] += ddt

    vec = pl.BlockSpec((1, LANE), lambda i: (0, 0))
    wide = pl.BlockSpec((r, hv * HEAD), lambda i: (i, 0))
    return pl.pallas_call(
        body, name=name, grid=(s // r,),
        in_specs=[pl.BlockSpec((r, LANE), lambda i: (i, 0)), pl.BlockSpec((r, LANE), lambda i: (i, 1)), vec, vec, wide, wide],
        out_specs=[pl.BlockSpec((r, 2 * LANE), lambda i: (i, 0)), vec, vec],
        out_shape=[SDS((s, 2 * LANE), F32), SDS((1, LANE), F32), SDS((1, LANE), F32)],
        compiler_params=_params("arbitrary"))(ab, ab, a_log, dt_bias, dgcb, dbtb)


def _split_bf16(a):
    hi = a.astype(BF)
    return hi, (a - hi.astype(F32)).astype(BF)


def _dot3(a, b, dims=(((1,), (0,)), ((), ()))):
    ah, al = _split_bf16(a)
    bh, bl = _split_bf16(b)
    d = lambda x, y: lax.dot_general(x, y, dims, preferred_element_type=F32)
    return d(ah, bh) + (d(ah, bl) + d(al, bh))


NNB = (((2,), (1,)), ((0,), (0,)))
NTB = (((2,), (2,)), ((0,), (0,)))
TNB = (((1,), (1,)), ((0,), (0,)))


def _bdot(a, b, dims):
    return lax.dot_general(a, b, dims, preferred_element_type=F32)


def _unit_lower_inverse(a):
    ri = lax.broadcasted_iota(jnp.int32, a.shape, 1)
    ci = lax.broadcasted_iota(jnp.int32, a.shape, 2)
    p = -a
    t = jnp.where(ri == ci, 1.0, 0.0) + p
    for _ in range(5):
        p = _dot3(p, p, NNB)
        t = t + _dot3(t, p, NNB)
    return t


@jax.custom_vjp
def _known_inverse(a, t):
    return t


def _known_inverse_fwd(a, t):
    return t, t


def _known_inverse_bwd(t, g):
    return -_dot3(_dot3(t, g, TNB), t, NTB), jnp.zeros_like(t)


_known_inverse.defvjp(_known_inverse_fwd, _known_inverse_bwd)


def _delta_decay(gcb):
    c = CHUNK
    shape = (gcb.shape[0], c, c)
    ri = lax.broadcasted_iota(jnp.int32, shape, 1)
    ci = lax.broadcasted_iota(jnp.int32, shape, 2)
    causal = ri >= ci
    grow = jnp.stack([jnp.concatenate([gcb[b], gcb[b]], axis=0).T[:c, :c] for b in range(shape[0])])
    return jnp.where(causal, jnp.exp(jnp.where(causal, gcb[:, :, :c] - grow, 0.0)), 0.0), ri > ci


def _delta_system(k, gcb, btb):
    decay, strict = _delta_decay(gcb)
    return jnp.where(strict, _bdot((k * btb).astype(BF), k.astype(BF), NTB) * decay, 0.0)


def _delta_prep(q, k, v, gcb, btb, tinv):
    decay, strict = _delta_decay(gcb)
    kb = k * btb
    kbf = k.astype(BF)
    a = jnp.where(strict, _bdot(kb.astype(BF), kbf, NTB) * decay, 0.0)
    t = _known_inverse(a, tinv).astype(BF)
    u = _bdot(t, (v * btb).astype(BF), NNB)
    w = _bdot(t, (kb * jnp.exp(gcb)).astype(BF), NNB)
    attn = _bdot(q.astype(BF), kbf, NTB) * decay
    return u, w, attn


def _delta_scan(u, w, attn, q, k, gcb, s_in):
    c = CHUNK
    glast = gcb[c - 1:c, :]
    sb = s_in.astype(BF)
    v_new = u - jnp.dot(w.astype(BF), sb, preferred_element_type=F32)
    vnb = v_new.astype(BF)
    o = (jnp.dot((q * jnp.exp(gcb)).astype(BF), sb, preferred_element_type=F32)
         + jnp.dot(attn.astype(BF), vnb, preferred_element_type=F32))
    ks = (k * jnp.exp(glast - gcb)).astype(BF)
    s_out = s_in * jnp.exp(glast[:, 0:1]) + lax.dot_general(ks, vnb, TN, preferred_element_type=F32)
    return o, s_out


PREP_ROWS = 512
PREP_HEADS = 2
SCAN_ROWS = 512
SCAN_HEADS = 4


def _delta_prep_fwd(q, k, v, gcb, btb, name):
    s, dv = v.shape
    hv = dv // HEAD
    g = PREP_HEADS
    assert dv // q.shape[1] == g
    r = _tile(s, PREP_ROWS, CHUNK)

    def body(q_ref, k_ref, v_ref, g_ref, b_ref, u_ref, w_ref, a_ref, t_ref):
        nb = r // CHUNK
        qc = q_ref[...].astype(F32).reshape(nb, CHUNK, HEAD)
        kc = k_ref[...].astype(F32).reshape(nb, CHUNK, HEAD)
        for hh in range(g):
            cols = pl.ds(hh * HEAD, HEAD)
            half = pl.ds(hh * CHUNK, CHUNK)
            gc = g_ref[:, cols].reshape(nb, CHUNK, HEAD)
            bc = b_ref[:, cols].reshape(nb, CHUNK, HEAD)
            tinv = _unit_lower_inverse(_delta_system(kc, gc, bc))
            u, w, attn = _delta_prep(qc, kc, v_ref[:, cols].astype(F32).reshape(nb, CHUNK, HEAD), gc, bc, tinv)
            u_ref[:, cols] = u.reshape(r, HEAD)
            w_ref[:, cols] = w.reshape(r, HEAD).astype(BF)
            a_ref[:, half] = attn.reshape(r, CHUNK).astype(BF)
            t_ref[:, half] = tinv.reshape(r, CHUNK)

    kq = pl.BlockSpec((r, HEAD), lambda j, i: (i, j))
    vs = pl.BlockSpec((r, g * HEAD), lambda j, i: (i, j))
    sq = pl.BlockSpec((r, g * CHUNK), lambda j, i: (i, j))
    return pl.pallas_call(
        body, name=name, grid=(hv // g, s // r), in_specs=[kq, kq, vs, vs, vs], out_specs=[vs, vs, sq, sq],
        out_shape=[SDS((s, dv), F32), SDS((s, dv), BF), SDS((s, hv * CHUNK), BF), SDS((s, hv * CHUNK), F32)],
        compiler_params=_params("parallel", "parallel"))(q, k, v, gcb, btb)


def _delta_prep_bwd(q, k, v, gcb, btb, tinv, du, dw, dattn, dq_s, dk_s, dg_s, name):
    s, dv = v.shape
    hv = dv // HEAD
    g = PREP_HEADS
    r = _tile(s, PREP_ROWS, CHUNK)

    def body(q_ref, k_ref, v_ref, g_ref, b_ref, t_ref, du_ref, dw_ref, da_ref, dqs_ref, dks_ref, dgs_ref,
             dq_ref, dk_ref, dv_ref, dg_ref, db_ref):
        nb = r // CHUNK
        wide = lambda ref, cols: ref[:, cols].astype(F32).reshape(nb, CHUNK, HEAD)
        qc = q_ref[...].astype(F32).reshape(nb, CHUNK, HEAD)
        kc = k_ref[...].astype(F32).reshape(nb, CHUNK, HEAD)
        for hh in range(g):
            cols = pl.ds(hh * HEAD, HEAD)
            half = pl.ds(hh * CHUNK, CHUNK)
            fn = functools.partial(_delta_prep, tinv=t_ref[:, half].reshape(nb, CHUNK, CHUNK))
            _, vjp = jax.vjp(fn, qc, kc, wide(v_ref, cols), wide(g_ref, cols), wide(b_ref, cols))
            dq, dk, dvv, dg, db = vjp((wide(du_ref, cols), wide(dw_ref, cols),
                                       da_ref[:, half].astype(F32).reshape(nb, CHUNK, CHUNK)))
            dq_ref[:, cols] = dq.reshape(r, HEAD) + dqs_ref[:, cols]
            dk_ref[:, cols] = dk.reshape(r, HEAD) + dks_ref[:, cols]
            dv_ref[:, cols] = dvv.reshape(r, HEAD)
            dg_ref[:, cols] = dg.reshape(r, HEAD) + dgs_ref[:, cols]
            db_ref[:, cols] = db.reshape(r, HEAD)

    kq = pl.BlockSpec((r, HEAD), lambda j, i: (i, j))
    vs = pl.BlockSpec((r, g * HEAD), lambda j, i: (i, j))
    sq = pl.BlockSpec((r, g * CHUNK), lambda j, i: (i, j))
    return pl.pallas_call(
        body, name=name, grid=(hv // g, s // r), in_specs=[kq, kq, vs, vs, vs, sq, vs, vs, sq, vs, vs, vs],
        out_specs=[vs] * 5, out_shape=[SDS((s, dv), F32)] * 5,
        compiler_params=_params("parallel", "parallel"))(q, k, v, gcb, btb, tinv, du, dw, dattn, dq_s, dk_s, dg_s)


def _delta_scan_fwd(u, w, attn, q, k, gcb, name):
    s, dv = u.shape
    hv = dv // HEAD
    rep = dv // q.shape[1]
    g = min(SCAN_HEADS, hv)
    r = _tile(s, SCAN_ROWS, CHUNK)
    per = r // CHUNK

    def body(u_ref, w_ref, a_ref, q_ref, k_ref, g_ref, o_ref, st_ref, state):
        @pl.when(pl.program_id(1) == 0)
        def _():
            state[...] = jnp.zeros_like(state)

        def chunk(cc, carry):
            rows = pl.ds(pl.multiple_of(cc * CHUNK, CHUNK), CHUNK)
            for hh in range(g):
                cols = pl.ds(hh * HEAD, HEAD)
                kcols = pl.ds((hh // rep) * HEAD, HEAD)
                s_in = state[hh]
                st_ref[hh, cc] = s_in
                o, s_out = _delta_scan(u_ref[rows, cols], w_ref[rows, cols].astype(F32),
                                       a_ref[rows, pl.ds(hh * CHUNK, CHUNK)].astype(F32),
                                       q_ref[rows, kcols].astype(F32), k_ref[rows, kcols].astype(F32), g_ref[rows, cols], s_in)
                o_ref[rows, cols] = o.astype(BF)
                state[hh] = s_out
            return carry

        lax.fori_loop(0, per, chunk, 0)

    kq = pl.BlockSpec((r, g // rep * HEAD), lambda j, i: (i, j))
    vs = pl.BlockSpec((r, g * HEAD), lambda j, i: (i, j))
    sq = pl.BlockSpec((r, g * CHUNK), lambda j, i: (i, j))
    return pl.pallas_call(
        body, name=name, grid=(hv // g, s // r), in_specs=[vs, vs, sq, kq, kq, vs],
        out_specs=[vs, pl.BlockSpec((g, per, HEAD, HEAD), lambda j, i: (j, i, 0, 0))],
        out_shape=[SDS((s, dv), BF), SDS((hv, s // CHUNK, HEAD, HEAD), F32)],
        scratch_shapes=[pltpu.VMEM((g, HEAD, HEAD), F32)],
        compiler_params=_params("parallel", "arbitrary"))(u, w, attn, q, k, gcb)


def _delta_scan_bwd(u, w, attn, q, k, gcb, states, do, name):
    s, dv = u.shape
    hv = dv // HEAD
    rep = dv // q.shape[1]
    g = min(SCAN_HEADS, hv)
    r = _tile(s, SCAN_ROWS, CHUNK)
    per = r // CHUNK
    nb = s // r

    def body(u_ref, w_ref, a_ref, q_ref, k_ref, g_ref, st_ref, do_ref, du_ref, dw_ref, da_ref, dq_ref, dk_ref, dg_ref, dstate):
        @pl.when(pl.program_id(1) == 0)
        def _():
            dstate[...] = jnp.zeros_like(dstate)

        def chunk(t, carry):
            cc = per - 1 - t
            rows = pl.ds(pl.multiple_of(cc * CHUNK, CHUNK), CHUNK)
            for hh in range(g):
                cols = pl.ds(hh * HEAD, HEAD)
                kcols = pl.ds((hh // rep) * HEAD, HEAD)
                half = pl.ds(hh * CHUNK, CHUNK)
                _, vjp = jax.vjp(_delta_scan, u_ref[rows, cols], w_ref[rows, cols].astype(F32), a_ref[rows, half].astype(F32),
                                 q_ref[rows, kcols].astype(F32), k_ref[rows, kcols].astype(F32), g_ref[rows, cols], st_ref[hh, cc])
                du, dw, da, dq, dk, dg, ds_in = vjp((do_ref[rows, cols], dstate[hh]))
                du_ref[rows, cols] = du.astype(BF)
                dw_ref[rows, cols] = dw.astype(BF)
                da_ref[rows, half] = da.astype(BF)
                dq_ref[rows, cols] = dq
                dk_ref[rows, cols] = dk
                dg_ref[rows, cols] = dg
                dstate[hh] = ds_in
            return carry

        lax.fori_loop(0, per, chunk, 0)

    kq = pl.BlockSpec((r, g // rep * HEAD), lambda j, i: (nb - 1 - i, j))
    vs = pl.BlockSpec((r, g * HEAD), lambda j, i: (nb - 1 - i, j))
    sq = pl.BlockSpec((r, g * CHUNK), lambda j, i: (nb - 1 - i, j))
    return pl.pallas_call(
        body, name=name, grid=(hv // g, nb),
        in_specs=[vs, vs, sq, kq, kq, vs, pl.BlockSpec((g, per, HEAD, HEAD), lambda j, i: (j, nb - 1 - i, 0, 0)), vs],
        out_specs=[vs, vs, sq, vs, vs, vs],
        out_shape=[SDS((s, dv), BF), SDS((s, dv), BF), SDS((s, hv * CHUNK), BF)] + [SDS((s, dv), F32)] * 3,
        scratch_shapes=[pltpu.VMEM((g, HEAD, HEAD), F32)],
        compiler_params=_params("parallel", "arbitrary"))(u, w, attn, q, k, gcb, states, do)


def _delta_chunk(q, k, v, gcb, btb, s_in):
    c = CHUNK
    ri = lax.broadcasted_iota(jnp.int32, (c, c), 0)
    ci = lax.broadcasted_iota(jnp.int32, (c, c), 1)
    causal = ri >= ci
    gcol = gcb[:, :c]
    grow = jnp.concatenate([gcb, gcb], axis=0).T[:c, :c]
    decay = jnp.where(causal, jnp.exp(jnp.where(causal, gcol - grow, 0.0)), 0.0)
    kb = k * btb
    vb = v * btb
    kbf = k.astype(BF)
    a = jnp.where(ri > ci, lax.dot_general(kb.astype(BF), kbf, NT, preferred_element_type=F32) * decay, 0.0)
    p = -a
    t = jnp.where(ri == ci, 1.0, 0.0) + p
    for _ in range(5):
        p = jnp.dot(p, p, precision=HI, preferred_element_type=F32)
        t = t + jnp.dot(t, p, precision=HI, preferred_element_type=F32)
    eg = jnp.exp(gcb)
    u = jnp.dot(t, vb, precision=HI, preferred_element_type=F32)
    w = jnp.dot(t, kb * eg, precision=HI, preferred_element_type=F32)
    attn = lax.dot_general(q.astype(BF), kbf, NT, preferred_element_type=F32) * decay
    glast = gcb[c - 1:c, :]
    ks = k * jnp.exp(glast - gcb)
    sb = s_in.astype(BF)
    v_new = u - jnp.dot(w.astype(BF), sb, preferred_element_type=F32)
    o = (jnp.dot((q * eg).astype(BF), sb, preferred_element_type=F32)
         + jnp.dot(attn.astype(BF), v_new.astype(BF), preferred_element_type=F32))
    s_out = s_in * jnp.exp(glast[:, 0:1]) + lax.dot_general(ks.astype(BF), v_new.astype(BF), TN, preferred_element_type=F32)
    return o, s_out


GDN_ROWS = 512


def _delta_fwd(q, k, v, gcb, btb, name):
    s, dv = v.shape
    hv = dv // HEAD
    rep = dv // q.shape[1]
    r = _tile(s, GDN_ROWS, CHUNK)
    per = r // CHUNK
    nc = s // CHUNK

    def body(q_ref, k_ref, v_ref, g_ref, b_ref, o_ref, st_ref, state):
        @pl.when(pl.program_id(1) == 0)
        def _():
            state[...] = jnp.zeros_like(state)

        def chunk(cc, carry):
            rows = pl.ds(pl.multiple_of(cc * CHUNK, CHUNK), CHUNK)
            st_ref[0, cc] = state[...]
            o, s_out = _delta_chunk(q_ref[rows, :].astype(F32), k_ref[rows, :].astype(F32), v_ref[rows, :].astype(F32),
                                    g_ref[rows, :], b_ref[rows, :], state[...])
            o_ref[rows, :] = o.astype(BF)
            state[...] = s_out
            return carry

        lax.fori_loop(0, per, chunk, 0)

    kq = pl.BlockSpec((r, HEAD), lambda h, i: (i, h // rep))
    vs = pl.BlockSpec((r, HEAD), lambda h, i: (i, h))
    return pl.pallas_call(
        body, name=name, grid=(hv, s // r), in_specs=[kq, kq, vs, vs, vs],
        out_specs=[vs, pl.BlockSpec((1, per, HEAD, HEAD), lambda h, i: (h, i, 0, 0))],
        out_shape=[SDS((s, dv), BF), SDS((hv, nc, HEAD, HEAD), F32)],
        scratch_shapes=[pltpu.VMEM((HEAD, HEAD), F32)],
        compiler_params=_params("parallel", "arbitrary"))(q, k, v, gcb, btb)


def _delta_bwd(q, k, v, gcb, btb, states, do, name):
    s, dv = v.shape
    hv = dv // HEAD
    rep = dv // q.shape[1]
    r = _tile(s, GDN_ROWS, CHUNK)
    per = r // CHUNK
    nb = s // r

    def body(q_ref, k_ref, v_ref, g_ref, b_ref, st_ref, do_ref, dq_ref, dk_ref, dv_ref, dg_ref, db_ref, dstate):
        @pl.when(pl.program_id(1) == 0)
        def _():
            dstate[...] = jnp.zeros_like(dstate)

        def chunk(t, carry):
            cc = per - 1 - t
            rows = pl.ds(pl.multiple_of(cc * CHUNK, CHUNK), CHUNK)
            _, vjp = jax.vjp(_delta_chunk, q_ref[rows, :].astype(F32), k_ref[rows, :].astype(F32),
                             v_ref[rows, :].astype(F32), g_ref[rows, :], b_ref[rows, :], st_ref[0, cc])
            dq, dk, dvv, dg, db, ds_in = vjp((do_ref[rows, :].astype(F32), dstate[...]))
            dq_ref[rows, :] = dq
            dk_ref[rows, :] = dk
            dv_ref[rows, :] = dvv
            dg_ref[rows, :] = dg
            db_ref[rows, :] = db
            dstate[...] = ds_in
            return carry

        lax.fori_loop(0, per, chunk, 0)

    kq = pl.BlockSpec((r, HEAD), lambda h, i: (nb - 1 - i, h // rep))
    vs = pl.BlockSpec((r, HEAD), lambda h, i: (nb - 1 - i, h))
    return pl.pallas_call(
        body, name=name, grid=(hv, nb),
        in_specs=[kq, kq, vs, vs, vs, pl.BlockSpec((1, per, HEAD, HEAD), lambda h, i: (h, nb - 1 - i, 0, 0)), vs],
        out_specs=[vs] * 5, out_shape=[SDS((s, dv), F32)] * 5,
        scratch_shapes=[pltpu.VMEM((HEAD, HEAD), F32)],
        compiler_params=_params("parallel", "arbitrary"))(q, k, v, gcb, btb, states, do)


def _gdn_out_tile(o, gate, gain):
    return _headnorm(o, gain) * _silu(gate)


def _gdn_out_fwd(o, proj, gate_col0, gain, name):
    s, dv = o.shape
    r = _tile(s, 128, 16)

    def body(o_ref, g_ref, gain_ref, y_ref):
        y_ref[...] = _gdn_out_tile(o_ref[...].astype(F32), g_ref[...].astype(F32), gain_ref[...]).astype(BF)

    row = pl.BlockSpec((r, dv), lambda i: (i, 0))
    return pl.pallas_call(
        body, name=name, grid=(s // r,),
        in_specs=[row, pl.BlockSpec((r, dv), lambda i: (i, gate_col0)), pl.BlockSpec((1, HEAD), lambda i: (0, 0))],
        out_specs=row, out_shape=SDS((s, dv), BF), compiler_params=_params("parallel"))(o, proj, gain)


def _gdn_out_bwd(o, proj, gate_col0, gain, dy, name):
    s, dv = o.shape
    r = _tile(s, 128, 16)

    def body(o_ref, g_ref, gain_ref, dy_ref, do_ref, dg_ref, dgain_ref):
        i = pl.program_id(0)
        _, vjp = jax.vjp(_gdn_out_tile, o_ref[...].astype(F32), g_ref[...].astype(F32), gain_ref[...])
        do, dg, dgain = vjp(dy_ref[...].astype(F32))
        do_ref[...] = do
        dg_ref[...] = dg.astype(BF)

        @pl.when(i == 0)
        def _():
            dgain_ref[...] = dgain

        @pl.when(i > 0)
        def _():
            dgain_ref[...] += dgain

    row = pl.BlockSpec((r, dv), lambda i: (i, 0))
    vec = pl.BlockSpec((1, HEAD), lambda i: (0, 0))
    return pl.pallas_call(
        body, name=name, grid=(s // r,),
        in_specs=[row, pl.BlockSpec((r, dv), lambda i: (i, gate_col0)), vec, row],
        out_specs=[row, row, vec], out_shape=[SDS((s, dv), F32), SDS((s, dv), BF), SDS((1, HEAD), F32)],
        compiler_params=_params("arbitrary"))(o, proj, gain, dy)


def _ffn_forward(x, gain, wu, wg, cw, wd, tag):
    h, ht = _rmsnorm_fwd(x, gain, f"{tag}_norm")
    uu = _mm(h, wu, name=f"{tag}_up_u")
    ug = _mm(h, wg, name=f"{tag}_up_g")
    a = _ffn_act_fwd(uu, ug, cw, f"{tag}_act")
    y = _mm(a, wd, res=x, out_dtype=F32, name=f"{tag}_down")
    return y, (x, ht, uu, ug, a)


def _ffn_backward(saved, dys, gain, wu, wg, cw, wd, tag):
    x, ht, uu, ug, a = saved
    dy, dyb = dys
    da = _mm(dyb, wd, tb=True, name=f"{tag}_d_act")
    dwd = _mm(a, dyb, ta=True, out_dtype=F32, name=f"{tag}_d_wd")
    duu, dug, dcw = _ffn_act_bwd(uu, ug, cw, da, f"{tag}_act_bwd")
    dh = _mm(duu, wu, tb=True, out_dtype=F32, name=f"{tag}_d_h_u")
    dh = _mm(dug, wg, tb=True, res=dh, out_dtype=F32, name=f"{tag}_d_h_g")
    dwu = _mm(ht, duu, out_dtype=F32, name=f"{tag}_d_wu")
    dwg = _mm(ht, dug, out_dtype=F32, name=f"{tag}_d_wg")
    dx, dxb, dgain = _rmsnorm_bwd(x, gain, dh, dy, f"{tag}_norm_bwd")
    return (dx, dxb), dict(gain=dgain, wu=dwu, wg=dwg, cw=dcw, wd=dwd)


def _att_forward(x, gain, p, tag):
    h, ht = _rmsnorm_fwd(x, gain, f"{tag}_norm")
    qkv = _mm(h, p["wqkv"], name=f"{tag}_qkv")
    q, kpad, vpad = _qkv_post_fwd(qkv, p["qg"], p["kg"], f"{tag}_qknorm")
    bias = _bias_expand(p["rel"], f"{tag}_bias")
    o = _attn_fwd(q, kpad, vpad, bias, f"{tag}_core")
    y = _mm(o, p["wo"], res=x, out_dtype=F32, name=f"{tag}_out")
    return y, (x, ht, qkv, q, kpad, vpad, bias, o)


def _att_backward(saved, dys, gain, p, tag):
    x, ht, qkv, q, kpad, vpad, bias, o = saved
    dy, dyb = dys
    do = _mm(dyb, p["wo"], tb=True, name=f"{tag}_d_o")
    dwo = _mm(o, dyb, ta=True, out_dtype=F32, name=f"{tag}_d_wo")
    dq, dkpad, dvpad, dbias = _attn_bwd(q, kpad, vpad, bias, do, f"{tag}_core_bwd")
    drel = _bias_reduce(dbias, f"{tag}_bias_bwd")
    dqkv, dqg, dkg = _qkv_post_bwd(qkv, p["qg"], p["kg"], dq, dkpad, dvpad, f"{tag}_qknorm_bwd")
    dh = _mm(dqkv, p["wqkv"], tb=True, out_dtype=F32, name=f"{tag}_d_h")
    dwqkv = _mm(ht, dqkv, out_dtype=F32, name=f"{tag}_d_wqkv")
    dx, dxb, dgain = _rmsnorm_bwd(x, gain, dh, dy, f"{tag}_norm_bwd")
    return (dx, dxb), dict(gain=dgain, wqkv=dwqkv, qg=dqg, kg=dkg, rel=drel, wo=dwo)


def _gdn_forward(x, gain, p, tag):
    d = x.shape[1]
    nk = d // HEAD
    hv = 2 * nk
    h, ht = _rmsnorm_fwd(x, gain, f"{tag}_norm")
    proj = _mm(h, p["wmain"], name=f"{tag}_proj")
    ab = _mm(h, p["wab"], out_dtype=F32, name=f"{tag}_proj_ab")
    q = _gdn_conv_fwd(proj, p["cw"], "q", 0, nk, f"{tag}_conv_q")
    k = _gdn_conv_fwd(proj, p["cw"], "k", nk, nk, f"{tag}_conv_k")
    v = _gdn_conv_fwd(proj, p["cw"], "v", 2 * nk, hv, f"{tag}_conv_v")
    gcb, btb = _gates_fwd(ab, p["a_log"], p["dt_bias"], hv, f"{tag}_gates")
    u, wd, attn, tinv = _delta_prep_fwd(q, k, v, gcb, btb, f"{tag}_delta_prep")
    o, states = _delta_scan_fwd(u, wd, attn, q, k, gcb, f"{tag}_delta_scan")
    og = _gdn_out_fwd(o, proj, 2, p["o_gain"], f"{tag}_onorm")
    y = _mm(og, p["wo"], res=x, out_dtype=F32, name=f"{tag}_out")
    return y, (x, ht, proj, ab, q, k, v, gcb, btb, u, wd, attn, tinv, o, states, og)


def _gdn_backward(saved, dys, gain, p, tag):
    x, ht, proj, ab, q, k, v, gcb, btb, u, wd, attn, tinv, o, states, og = saved
    dy, dyb = dys
    d = x.shape[1]
    nk = d // HEAD
    hv = 2 * nk
    dog = _mm(dyb, p["wo"], tb=True, name=f"{tag}_d_og")
    dwo = _mm(og, dyb, ta=True, out_dtype=F32, name=f"{tag}_d_wo")
    do, dgate, dogain = _gdn_out_bwd(o, proj, 2, p["o_gain"], dog, f"{tag}_onorm_bwd")
    du, dw, dattn, dq_s, dk_s, dg_s = _delta_scan_bwd(u, wd, attn, q, k, gcb, states, do, f"{tag}_delta_scan_bwd")
    dq, dk, dv, dgcb, dbtb = _delta_prep_bwd(q, k, v, gcb, btb, tinv, du, dw, dattn, dq_s, dk_s, dg_s, f"{tag}_delta_prep_bwd")
    dab, dalog, ddt = _gates_bwd(ab, p["a_log"], p["dt_bias"], dgcb, dbtb, hv, f"{tag}_gates_bwd")
    dpq, dcq = _gdn_conv_bwd(proj, p["cw"], dq, "q", 0, nk, f"{tag}_conv_q_bwd")
    dpk, dck = _gdn_conv_bwd(proj, p["cw"], dk, "k", nk, nk, f"{tag}_conv_k_bwd")
    dpv, dcv = _gdn_conv_bwd(proj, p["cw"], dv, "v", 2 * nk, hv, f"{tag}_conv_v_bwd")
    dproj = jnp.concatenate([dpq, dpk, dpv, dgate], axis=1)
    dcw = jnp.concatenate([dcq, dck, dcv], axis=1)
    dh = _mm(dproj, p["wmain"], tb=True, out_dtype=F32, name=f"{tag}_d_h_main")
    dh = _mm(dab, p["wab"], tb=True, res=dh, out_dtype=F32, name=f"{tag}_d_h_ab")
    dwmain = _mm(ht, dproj, out_dtype=F32, name=f"{tag}_d_wmain")
    dwab = _mm(ht, dab, out_dtype=F32, name=f"{tag}_d_wab")
    dx, dxb, dgain = _rmsnorm_bwd(x, gain, dh, dy, f"{tag}_norm_bwd")
    return (dx, dxb), dict(gain=dgain, wmain=dwmain, wab=dwab, cw=dcw, a_log=dalog, dt_bias=ddt, o_gain=dogain, wo=dwo)


def _resolve(entry, after):
    return entry(after) if callable(entry) else entry


def _local_step(x, target, w, sink=None):
    depth = len(w["ffn"])
    tape = []
    for i in range(depth):
        kind, j = i % 3, i // 3
        gain = w["mix_norm"][i:i + 1]
        if kind == 0:
            x, saved = _att_forward(x, gain, _resolve(w["att"][j], x), f"l{i}_att")
        elif kind == 1:
            x_in = x
            pw = _resolve(w["pool"][j], x)
            x = _pool_fwd(x_in, gain, pw["w"], pw["scale"], f"l{i}_pool")
            saved = x_in
        else:
            x, saved = _gdn_forward(x, gain, _resolve(w["gdn"][j], x), f"l{i}_gdn")
        f = _resolve(w["ffn"][i], x)
        x, fsaved = _ffn_forward(x, w["ffn_norm"][i:i + 1], f["wu"], f["wg"], f["cw"], f["wd"], f"l{i}_ffn")
        tape.append((saved, fsaved))
    dy, dyb, loss_row = _loss_head(x, target, "loss_head")
    dy = (dy, dyb)
    grads = dict(mix=[None] * depth, ffn=[None] * depth)
    zero = 0.0
    for i in reversed(range(depth)):
        kind, j = i % 3, i // 3
        saved, fsaved = tape[i]
        f = _resolve(w["ffn"][i], dy[0])
        dy, grads["ffn"][i] = _ffn_backward(fsaved, dy, w["ffn_norm"][i:i + 1] + zero, f["wu"], f["wg"], f["cw"], f["wd"], f"l{i}_ffn")
        if sink is not None:
            zero = zero + sink("ffn", i, grads["ffn"][i])
        gain = w["mix_norm"][i:i + 1] + zero
        if kind == 0:
            dy, grads["mix"][i] = _att_backward(saved, dy, gain, _resolve(w["att"][j], dy[0]), f"l{i}_att")
        elif kind == 1:
            pw = _resolve(w["pool"][j], dy[0])
            dx, dgain, dw4, dscale = _pool_bwd(saved, gain, pw["w"], pw["scale"], dy[0], f"l{i}_pool_bwd")
            dy = (dx, dx.astype(BF))
            grads["mix"][i] = dict(gain=dgain, w=dw4, scale=dscale)
        else:
            dy, grads["mix"][i] = _gdn_backward(saved, dy, gain, _resolve(w["gdn"][j], dy[0]), f"l{i}_gdn")
        if sink is not None:
            zero = zero + sink(("att", "pool", "gdn")[kind], i, grads["mix"][i])
    return loss_row, dy[0], grads


MESH = pl.DeviceIdType.MESH
ANY = pl.BlockSpec(memory_space=pl.ANY)


def _position():
    return tuple(lax.axis_index(a) for a in AXES)


def _flip(pos, rel):
    return tuple(1 - p if (rel >> (2 - i)) & 1 else p for i, p in enumerate(pos))


def _index(pos):
    return 4 * pos[0] + 2 * pos[1] + pos[2]


def _all_gather(arr, name):
    def body(x_ref, o_ref, send, recv, local):
        me = _position()
        mine = pltpu.make_async_copy(x_ref, o_ref.at[_index(me)], local)
        mine.start()
        copies = []
        for rel in range(1, N_DEV):
            cp = pltpu.make_async_remote_copy(
                src_ref=x_ref, dst_ref=o_ref.at[_index(me)], send_sem=send.at[rel - 1], recv_sem=recv.at[rel - 1],
                device_id=_flip(me, rel), device_id_type=MESH)
            cp.start()
            copies.append(cp)
        for cp in copies:
            cp.wait()
        mine.wait()

    return pl.pallas_call(
        body, name=name, in_specs=[ANY], out_specs=ANY, out_shape=SDS((N_DEV,) + arr.shape, arr.dtype),
        scratch_shapes=[pltpu.SemaphoreType.DMA((N_DEV - 1,)), pltpu.SemaphoreType.DMA((N_DEV - 1,)),
                        pltpu.SemaphoreType.DMA(())])(arr)


def _exchange(arr, name):
    def body(x_ref, o_ref, send, recv, local):
        me = _position()
        mine = pltpu.make_async_copy(x_ref.at[_index(me)], o_ref.at[_index(me)], local)
        mine.start()
        copies = []
        for rel in range(1, N_DEV):
            peer = _flip(me, rel)
            cp = pltpu.make_async_remote_copy(
                src_ref=x_ref.at[_index(peer)], dst_ref=o_ref.at[_index(me)], send_sem=send.at[rel - 1],
                recv_sem=recv.at[rel - 1], device_id=peer, device_id_type=MESH)
            cp.start()
            copies.append(cp)
        for cp in copies:
            cp.wait()
        mine.wait()

    return pl.pallas_call(
        body, name=name, in_specs=[ANY], out_specs=ANY, out_shape=SDS(arr.shape, arr.dtype),
        scratch_shapes=[pltpu.SemaphoreType.DMA((N_DEV - 1,)), pltpu.SemaphoreType.DMA((N_DEV - 1,)),
                        pltpu.SemaphoreType.DMA(())])(arr)


HBM = pl.BlockSpec(memory_space=pltpu.HBM)
SEM = pl.BlockSpec(memory_space=pltpu.SEMAPHORE)
EFFECT = pltpu.SideEffectType.DATAFLOW_SIDE_EFFECTING


def _split_copies(x_ref, land_ref, send, recv, scatter):
    me = _position()
    copies = []
    for rel in range(1, N_DEV):
        peer = _flip(me, rel)
        copies.append(pltpu.make_async_remote_copy(
            src_ref=x_ref.at[_index(peer)] if scatter else x_ref, dst_ref=land_ref.at[_index(me)],
            send_sem=send.at[rel - 1], recv_sem=recv.at[rel - 1], device_id=peer, device_id_type=MESH))
    return copies


def _copies_start(arr, scatter, name):
    shape = arr.shape if scatter else (N_DEV,) + arr.shape

    def body(x_ref, land_ref, send, recv, x_thru, land_thru, token):
        for cp in _split_copies(x_ref, land_ref, send, recv, scatter):
            cp.start()
        token[...] = jnp.zeros_like(token)

    sems = pltpu.SemaphoreType.DMA((N_DEV - 1,))
    send, recv, x_thru, land_thru, token = pl.pallas_call(
        body, name=name,
        out_shape=(sems, sems, pltpu.HBM(arr.shape, arr.dtype), pltpu.HBM(shape, arr.dtype), SDS((8, LANE), F32)),
        in_specs=(HBM, HBM), out_specs=(SEM, SEM, HBM, HBM, pl.BlockSpec(memory_space=pltpu.VMEM)),
        input_output_aliases={0: 2, 1: 3}, compiler_params=pltpu.CompilerParams(has_side_effects=EFFECT),
    )(pltpu.with_memory_space_constraint(arr, pltpu.HBM), pltpu.with_memory_space_constraint(lax.empty(shape, arr.dtype), pltpu.HBM))
    return (send, recv, x_thru, land_thru, scatter, name), token[0, 0]


def _copies_wait(handle, after):
    send, recv, x_thru, land_thru, scatter, name = handle

    def body(x_ref, land_ref, send_ref, recv_ref, after_ref, x_dead, got_ref):
        for cp in _split_copies(x_ref, land_ref, send_ref, recv_ref, scatter):
            cp.wait_send()
            cp.wait_recv()

    return pl.pallas_call(
        body, name=name + "_wait",
        out_shape=(pltpu.HBM(x_thru.shape, x_thru.dtype), pltpu.HBM(land_thru.shape, land_thru.dtype)),
        in_specs=(HBM, HBM, SEM, SEM, ANY), out_specs=(HBM, HBM), input_output_aliases={0: 0, 1: 1},
        compiler_params=pltpu.CompilerParams(has_side_effects=EFFECT),
    )(x_thru, land_thru, send, recv, after)[1]


def _with_own(got, own):
    return lax.dynamic_update_index_in_dim(got, own.astype(got.dtype), _index(_position()), 0)


def _adamw(parts, w, m, v, name):
    if not isinstance(parts, (list, tuple)):
        parts = [parts]
    layers = len(parts)
    r, c = parts[0].shape[1:]
    assert w.shape == (layers * r, c), (w.shape, parts[0].shape, layers)
    tr = _tile(r, 128, 16)
    per = r // tr
    c1 = 1.0 / (1.0 - ADAM_B1 ** ADAM_STEP)
    c2 = 1.0 / (1.0 - ADAM_B2 ** ADAM_STEP)

    def body(*refs):
        p_refs = refs[:layers]
        w_ref, m_ref, v_ref, g_ref, d_ref, nm_ref, nv_ref = refs[layers:]

        def update(p_ref):
            g = p_ref[0].astype(F32)
            for s in range(1, N_DEV):
                g = g + p_ref[s].astype(F32)
            nm = ADAM_B1 * m_ref[...] + (1.0 - ADAM_B1) * g
            nv = ADAM_B2 * v_ref[...] + (1.0 - ADAM_B2) * (g * g)
            g_ref[...] = g
            nm_ref[...] = nm
            nv_ref[...] = nv
            d_ref[...] = -ADAM_LR * ((nm * c1) / (jnp.sqrt(nv * c2) + ADAM_EPS) + ADAM_WD * w_ref[...])

        if layers == 1:
            update(p_refs[0])
        else:
            for j in range(layers):
                pl.when(pl.program_id(0) == j)(functools.partial(update, p_refs[j]))

    p_specs = [pl.BlockSpec((N_DEV, tr, c), functools.partial(lambda l, i, j: (0, jnp.where(l == j, i, 0), 0), j=j))
               for j in range(layers)]
    row = pl.BlockSpec((tr, c), lambda l, i: (l * per + i, 0))
    return pl.pallas_call(
        body, name=name, grid=(layers, per), in_specs=p_specs + [row, row, row],
        out_specs=[row] * 4, out_shape=[SDS(w.shape, F32)] * 4, compiler_params=_params("arbitrary", "arbitrary"))(*parts, w, m, v)


PACK = 8 * LANE


def _pack(arrs):
    flat = []
    for a in arrs:
        a = a.reshape(-1).astype(F32)
        flat.append(jnp.pad(a, (0, (-a.shape[0]) % PACK)))
    return jnp.concatenate(flat).reshape(-1, LANE)


def _unpack(packed, shapes, lead=()):
    flat = packed.reshape(lead + (-1,))
    out, off = [], 0
    for shp in shapes:
        n = int(np.prod(shp))
        out.append(flat[..., off:off + n].reshape(lead + tuple(shp)))
        off += n + (-n) % PACK
    return out


def _pad_to(a, axis, size):
    pad = [(0, 0)] * a.ndim
    pad[axis] = (0, size - a.shape[axis])
    return jnp.pad(a, pad)


def _cols_from_shards(g):
    return jnp.transpose(g, (1, 0, 2)).reshape(g.shape[1], -1)


def _cols_to_shards(a):
    c = a.shape[-1] // N_DEV
    a = a.reshape(a.shape[:-1] + (N_DEV, c))
    return jnp.moveaxis(a, -2, 0)


def _rows_to_shards(a):
    r = a.shape[-2] // N_DEV
    a = a.reshape(a.shape[:-2] + (N_DEV, r, a.shape[-1]))
    return jnp.moveaxis(a, -3, 0)


WEIGHTS = ("mix_norm", "ffn_norm", "att_w_qkv", "att_q_gain", "att_k_gain", "att_rel_bias", "att_w_o", "pool_w",
           "pool_scale", "gdn_w_in", "gdn_conv", "gdn_a_log", "gdn_dt_bias", "gdn_o_gain", "gdn_w_o", "ffn_w_up",
           "ffn_conv", "ffn_w_down")
REPLICATED = ("mix_norm", "ffn_norm", "att_q_gain", "att_k_gain", "pool_scale", "gdn_a_log", "gdn_dt_bias", "gdn_o_gain")
SMALL_SHARDED = ("att_rel_bias", "gdn_conv", "ffn_conv")
BIG = ("att_w_qkv", "att_w_o", "pool_w", "gdn_w_in", "gdn_w_o", "ffn_w_up", "ffn_w_down")
KEEP_F32 = ("pool_w",)


def _memo(build):
    cache = []

    def entry(after):
        if not cache:
            cache.append(build(after))
        return cache[0]

    return entry


def _assemble_weights(w, get, small, f):
    d = w["mix_norm"].shape[1]
    nk = d // HEAD
    hv = 2 * nk
    fp = -(-f // FF_PAD) * FF_PAD
    out = dict(mix_norm=w["mix_norm"], ffn_norm=w["ffn_norm"], att=[], pool=[], gdn=[], ffn=[])

    def att(j, after):
        return dict(wqkv=_cols_from_shards(get("att_w_qkv", j, after)), wo=get("att_w_o", j, after).reshape(d, d),
                    qg=w["att_q_gain"][j:j + 1], kg=w["att_k_gain"][j:j + 1], rel=small["att_rel_bias"][j])

    def pool(j, after):
        g = get("pool_w", j, after)
        return dict(w=jnp.transpose(g, (1, 0, 2, 3)).reshape(g.shape[1], g.shape[3], g.shape[3]),
                    scale=w["pool_scale"][j:j + 1])

    def gdn(j, after):
        win = _cols_from_shards(get("gdn_w_in", j, after))
        nm = 6 * d
        wab = jnp.concatenate([_pad_to(win[:, nm:nm + hv], 1, LANE), _pad_to(win[:, nm + hv:], 1, LANE)], axis=1)
        return dict(wmain=win[:, :nm], wab=wab, cw=_pad_to(small["gdn_conv"][j], 0, 8),
                    a_log=_pad_to(w["gdn_a_log"][j:j + 1], 1, LANE), dt_bias=_pad_to(w["gdn_dt_bias"][j:j + 1], 1, LANE),
                    o_gain=w["gdn_o_gain"][j:j + 1], wo=get("gdn_w_o", j, after).reshape(2 * d, d))

    def ffn(i, after):
        wup = _cols_from_shards(get("ffn_w_up", i, after))
        return dict(wu=_pad_to(wup[:, :f], 1, fp), wg=_pad_to(wup[:, f:], 1, fp),
                    cw=_pad_to(_pad_to(small["ffn_conv"][i], 0, 8), 1, fp),
                    wd=_pad_to(get("ffn_w_down", i, after).reshape(f, d), 0, fp))

    for key, build, count in (("att", att, w["att_w_qkv"].shape[0]), ("pool", pool, w["pool_w"].shape[0]),
                              ("gdn", gdn, w["gdn_w_in"].shape[0]), ("ffn", ffn, w["ffn_w_up"].shape[0])):
        out[key] = [_memo(functools.partial(build, j)) for j in range(count)]
    return out


def _full_gradients(grads, w, f):
    d = w["mix_norm"].shape[1]
    nk = d // HEAD
    hv = 2 * nk
    depth = len(grads["ffn"])
    att = [grads["mix"][i] for i in range(depth) if i % 3 == 0]
    pool = [grads["mix"][i] for i in range(depth) if i % 3 == 1]
    gdn = [grads["mix"][i] for i in range(depth) if i % 3 == 2]
    ffn = grads["ffn"]
    win = [jnp.concatenate([g["wmain"], g["wab"][:, :hv], g["wab"][:, LANE:LANE + hv]], axis=1) for g in gdn]
    return dict(
        mix_norm=jnp.concatenate([g["gain"] for g in grads["mix"]], axis=0),
        ffn_norm=jnp.concatenate([g["gain"] for g in ffn], axis=0),
        att_w_qkv=jnp.stack([g["wqkv"] for g in att]),
        att_q_gain=jnp.concatenate([g["qg"] for g in att], axis=0),
        att_k_gain=jnp.concatenate([g["kg"] for g in att], axis=0),
        att_rel_bias=jnp.stack([g["rel"] for g in att]),
        att_w_o=jnp.stack([g["wo"] for g in att]),
        pool_w=jnp.stack([g["w"] for g in pool]),
        pool_scale=jnp.concatenate([g["scale"] for g in pool], axis=0),
        gdn_w_in=jnp.stack(win),
        gdn_conv=jnp.stack([g["cw"][:GDN_CONV] for g in gdn]),
        gdn_a_log=jnp.concatenate([g["a_log"][:, :hv] for g in gdn], axis=0),
        gdn_dt_bias=jnp.concatenate([g["dt_bias"][:, :hv] for g in gdn], axis=0),
        gdn_o_gain=jnp.concatenate([g["o_gain"] for g in gdn], axis=0),
        gdn_w_o=jnp.stack([g["wo"] for g in gdn]),
        ffn_w_up=jnp.stack([jnp.concatenate([g["wu"][:, :f], g["wg"][:, :f]], axis=1) for g in ffn]),
        ffn_conv=jnp.stack([g["cw"][:FFN_CONV, :f] for g in ffn]),
        ffn_w_down=jnp.stack([g["wd"][:f] for g in ffn]))


ROW_SHARDED = ("att_w_o", "gdn_w_o", "ffn_w_down")


def _to_shards(name, full):
    if name == "pool_w":
        r = full.shape[2] // N_DEV
        a = full.reshape(full.shape[:2] + (N_DEV, r, full.shape[3]))
        return jnp.moveaxis(a, 2, 0)
    return _rows_to_shards(full) if name in ROW_SHARDED else _cols_to_shards(full)


def kernel(x, mix_norm, ffn_norm, att_w_qkv, att_q_gain, att_k_gain, att_rel_bias, att_w_o, pool_w, pool_scale, gdn_w_in, gdn_conv, gdn_a_log, gdn_dt_bias, gdn_o_gain, gdn_w_o, ffn_w_up, ffn_conv, ffn_w_down, loss_target, m_mix_norm, m_ffn_norm, m_att_w_qkv, m_att_q_gain, m_att_k_gain, m_att_rel_bias, m_att_w_o, m_pool_w, m_pool_scale, m_gdn_w_in, m_gdn_conv, m_gdn_a_log, m_gdn_dt_bias, m_gdn_o_gain, m_gdn_w_o, m_ffn_w_up, m_ffn_conv, m_ffn_w_down, v_mix_norm, v_ffn_norm, v_att_w_qkv, v_att_q_gain, v_att_k_gain, v_att_rel_bias, v_att_w_o, v_pool_w, v_pool_scale, v_gdn_w_in, v_gdn_conv, v_gdn_a_log, v_gdn_dt_bias, v_gdn_o_gain, v_gdn_w_o, v_ffn_w_up, v_ffn_conv, v_ffn_w_down):
    w = dict(zip(WEIGHTS, (mix_norm, ffn_norm, att_w_qkv, att_q_gain, att_k_gain, att_rel_bias, att_w_o, pool_w, pool_scale, gdn_w_in, gdn_conv, gdn_a_log, gdn_dt_bias, gdn_o_gain, gdn_w_o, ffn_w_up, ffn_conv, ffn_w_down)))
    m = dict(zip(WEIGHTS, (m_mix_norm, m_ffn_norm, m_att_w_qkv, m_att_q_gain, m_att_k_gain, m_att_rel_bias, m_att_w_o, m_pool_w, m_pool_scale, m_gdn_w_in, m_gdn_conv, m_gdn_a_log, m_gdn_dt_bias, m_gdn_o_gain, m_gdn_w_o, m_ffn_w_up, m_ffn_conv, m_ffn_w_down)))
    v = dict(zip(WEIGHTS, (v_mix_norm, v_ffn_norm, v_att_w_qkv, v_att_q_gain, v_att_k_gain, v_att_rel_bias, v_att_w_o, v_pool_w, v_pool_scale, v_gdn_w_in, v_gdn_conv, v_gdn_a_log, v_gdn_dt_bias, v_gdn_o_gain, v_gdn_w_o, v_ffn_w_up, v_ffn_conv, v_ffn_w_down)))

    me = _index(_position())
    d = mix_norm.shape[1]
    hv = 2 * (d // HEAD)
    f = ffn_w_down.shape[1] * N_DEV
    depth = ffn_w_up.shape[0]

    small_shapes = [w[n].shape for n in SMALL_SHARDED]
    small_g = _all_gather(_pack([w[n] for n in SMALL_SHARDED]), "gather_small")
    small = {}
    for n, a in zip(SMALL_SHARDED, _unpack(small_g, small_shapes, lead=(N_DEV,))):
        small[n] = jnp.moveaxis(a, 0, -2).reshape(a.shape[1:-1] + (N_DEV * a.shape[-1],))
    order = []
    for i in range(depth):
        order += [[("att_w_qkv", i // 3), ("att_w_o", i // 3)], [("pool_w", i // 3)], [("gdn_w_in", i // 3), ("gdn_w_o", i // 3)]][i % 3]
        order += [("ffn_w_up", i), ("ffn_w_down", i)]
    after_small = small_g[0, 0, 0] * 0.0
    local = {(n, j): (w[n][j] + after_small if n in KEEP_F32 else (w[n][j] + after_small).astype(BF)) for n, j in order}
    arriving, zero = {}, 0.0
    for n, j in order:
        arriving[(n, j)], tok = _copies_start(local[(n, j)], False, f"gather_{n}_{j}")
        zero = zero + tok

    def get(n, j, after):
        return _with_own(_copies_wait(arriving[(n, j)], after), local[(n, j)])

    ordered = dict(w, mix_norm=mix_norm + zero, ffn_norm=ffn_norm + zero)
    full = _assemble_weights(ordered, get, small, f)

    leaving = {}

    def sink(kind, i, g):
        j = i // 3
        if kind == "ffn":
            pieces = [("ffn_w_up", i, _cols_to_shards(jnp.concatenate([g["wu"][:, :f], g["wg"][:, :f]], axis=1))),
                      ("ffn_w_down", i, _rows_to_shards(g["wd"][:f]))]
        elif kind == "att":
            pieces = [("att_w_qkv", j, _cols_to_shards(g["wqkv"])), ("att_w_o", j, _rows_to_shards(g["wo"]))]
        elif kind == "pool":
            pieces = [("pool_w", j, _to_shards("pool_w", g["w"][None])[:, 0])]
        else:
            win = jnp.concatenate([g["wmain"], g["wab"][:, :hv], g["wab"][:, LANE:LANE + hv]], axis=1)
            pieces = [("gdn_w_in", j, _cols_to_shards(win)), ("gdn_w_o", j, _rows_to_shards(g["wo"]))]
        tok = 0.0
        for n, l, shards in pieces:
            shards = shards if n in KEEP_F32 else shards.astype(BF)
            handle, t = _copies_start(shards, True, f"exchange_{n}_{l}")
            leaving[(n, l)] = (handle, lax.dynamic_index_in_dim(shards, me, 0, keepdims=False))
            tok = tok + t
        return tok

    loss_row, dx, grads = _local_step(x[0], loss_target[0], full, sink)
    gfull = _full_gradients(grads, w, f)

    out = {}
    for n in BIG:
        c = w[n].shape[-1]
        parts = []
        for l in range(w[n].shape[0]):
            handle, own = leaving[(n, l)]
            parts.append(_with_own(_copies_wait(handle, dx), own).reshape(N_DEV, -1, c))
        res = _adamw(parts, w[n].reshape(-1, c), m[n].reshape(-1, c), v[n].reshape(-1, c), f"adamw_{n}")
        out[n] = [a.reshape(w[n].shape) for a in res]
    sparts = _exchange(jnp.stack([_pack([_to_shards(n, gfull[n])[k] for n in SMALL_SHARDED]) for k in range(N_DEV)]),
                       "exchange_small")
    res = _adamw(sparts, *[_pack([t[n] for n in SMALL_SHARDED]) for t in (w, m, v)], "adamw_small")
    for n, *vals in zip(SMALL_SHARDED, *[_unpack(a, small_shapes) for a in res]):
        out[n] = vals
    rep_shapes = [w[n].shape for n in REPLICATED]
    rparts = _all_gather(_pack([gfull[n] for n in REPLICATED] + [loss_row[:, 0:1]]), "gather_replicated_grads")
    pad = jnp.zeros((1, 1), F32)
    res = _adamw(rparts, *[_pack([t[n] for n in REPLICATED] + [pad]) for t in (w, m, v)], "adamw_replicated")
    for n, *vals in zip(REPLICATED, *[_unpack(a, rep_shapes) for a in res]):
        out[n] = vals
    loss = jnp.sum(_unpack(rparts, rep_shapes + [(1, 1)], lead=(N_DEV,))[-1])
    return (loss, dx[None], *[out[n][0] for n in WEIGHTS], *[out[n][1] for n in WEIGHTS],
            *[out[n][2] for n in WEIGHTS], *[out[n][3] for n in WEIGHTS])
```

```python
import functools

import numpy as np
import jax
import jax.numpy as jnp
from jax import lax
from jax.experimental import pallas as pl
from jax.experimental.pallas import tpu as pltpu

F32 = jnp.float32
BF = jnp.bfloat16
SDS = jax.ShapeDtypeStruct

EPS = 1e-6
MASK_VALUE = -1e30
CHUNK = 64
LEFT_CHUNKS = 8
BAND_LEFT = LEFT_CHUNKS * CHUNK
BAND = BAND_LEFT + CHUNK
MAX_REL = 256
NUM_REL = (CHUNK - 1) + MAX_REL + 1
HEAD = 128
LANE = 128
HALO = 16
POOL_WINDOWS = (2, 4, 8, 16)
GDN_CONV = 4
FFN_CONV = 3
FF_PAD = 512
N_DEV = 8
AXES = ("x", "y", "c")
VMEM_LIMIT = 56 * 1024 * 1024

ADAM_LR = 0.001
ADAM_B1 = 0.9
ADAM_B2 = 0.999
ADAM_EPS = 1e-08
ADAM_WD = 0.01
ADAM_STEP = 10

HI = lax.Precision.HIGHEST
NT = (((1,), (1,)), ((), ()))
TN = (((0,), (0,)), ((), ()))


def _tile(n, target, mult=LANE):
    if n <= target:
        return n
    t = (target // mult) * mult
    while t >= mult:
        if n % t == 0:
            return t
        t -= mult
    return n


def _params(*sem):
    return pltpu.CompilerParams(dimension_semantics=sem, vmem_limit_bytes=VMEM_LIMIT)


def _silu(x):
    return x / (1.0 + jnp.exp(-x))


@functools.partial(jax.custom_vjp, nondiff_argnums=(1,))
def _shift(x, k):
    return pltpu.roll(x, k % x.shape[0], axis=0)


def _shift_fwd(x, k):
    return _shift(x, k), None


def _shift_bwd(k, _, g):
    return (pltpu.roll(g, (-k) % g.shape[0], axis=0),)


_shift.defvjp(_shift_fwd, _shift_bwd)


def _mm(a, b, *, name, ta=False, tb=False, out_dtype=BF, res=None, tm=1024, tn=1024, tk=2048):
    m, k = (a.shape[1], a.shape[0]) if ta else a.shape
    n, kb = (b.shape[0], b.shape[1]) if tb else (b.shape[1], b.shape[0])
    assert k == kb, (a.shape, b.shape, ta, tb)
    if ta or a.dtype != BF:
        tk = min(tk, 1024)
    tm, tn, tk = _tile(m, tm), _tile(n, tn), _tile(k, tk)
    nk = k // tk
    dims = (((0 if ta else 1,), (1 if tb else 0,)), ((), ()))

    def body(*refs):
        if res is None:
            a_ref, b_ref, o_ref, acc = refs
        else:
            a_ref, b_ref, r_ref, o_ref, acc = refs
        prod = lax.dot_general(a_ref[...].astype(BF), b_ref[...].astype(BF), dims, preferred_element_type=F32)
        if nk == 1:
            if res is not None:
                prod = prod + r_ref[...].astype(F32)
            o_ref[...] = prod.astype(out_dtype)
            return
        kk = pl.program_id(2)

        @pl.when(kk == 0)
        def _():
            acc[...] = prod

        @pl.when(kk > 0)
        def _():
            acc[...] += prod

        @pl.when(kk == nk - 1)
        def _():
            r = acc[...]
            if res is not None:
                r = r + r_ref[...].astype(F32)
            o_ref[...] = r.astype(out_dtype)

    a_spec = pl.BlockSpec((tk, tm), lambda i, j, q: (q, i)) if ta else pl.BlockSpec((tm, tk), lambda i, j, q: (i, q))
    b_spec = pl.BlockSpec((tn, tk), lambda i, j, q: (j, q)) if tb else pl.BlockSpec((tk, tn), lambda i, j, q: (q, j))
    o_spec = pl.BlockSpec((tm, tn), lambda i, j, q: (i, j))
    ins, specs = [a, b], [a_spec, b_spec]
    if res is not None:
        ins.append(res)
        specs.append(o_spec)
    return pl.pallas_call(
        body, name=name, grid=(m // tm, n // tn, nk), in_specs=specs, out_specs=o_spec,
        out_shape=SDS((m, n), out_dtype), scratch_shapes=[pltpu.VMEM((tm, tn), F32)],
        compiler_params=_params("parallel", "parallel", "arbitrary"))(*ins)


def _rms(x, gain):
    return x * lax.rsqrt(jnp.mean(x * x, axis=-1, keepdims=True) + EPS) * gain


def _rmsnorm_fwd(x, gain, name):
    s, d = x.shape
    ts = _tile(s, 256, LANE)

    def body(x_ref, g_ref, o_ref, t_ref):
        h = _rms(x_ref[...], g_ref[...])
        o_ref[...] = h.astype(BF)
        t_ref[...] = h.T.astype(BF)

    return pl.pallas_call(
        body, name=name, grid=(s // ts,),
        in_specs=[pl.BlockSpec((ts, d), lambda i: (i, 0)), pl.BlockSpec((1, d), lambda i: (0, 0))],
        out_specs=[pl.BlockSpec((ts, d), lambda i: (i, 0)), pl.BlockSpec((d, ts), lambda i: (0, i))],
        out_shape=[SDS((s, d), BF), SDS((d, s), BF)], compiler_params=_params("parallel"))(x, gain)


def _rmsnorm_bwd(x, gain, dh, dres, name):
    s, d = x.shape
    ts = _tile(s, 256, 16)

    def body(x_ref, g_ref, dh_ref, dr_ref, dx_ref, dxb_ref, dg_ref):
        i = pl.program_id(0)
        _, vjp = jax.vjp(_rms, x_ref[...], g_ref[...])
        dx, dg = vjp(dh_ref[...].astype(F32))
        dx = dr_ref[...] + dx
        dx_ref[...] = dx
        dxb_ref[...] = dx.astype(BF)

        @pl.when(i == 0)
        def _():
            dg_ref[...] = dg

        @pl.when(i > 0)
        def _():
            dg_ref[...] += dg

    row = pl.BlockSpec((ts, d), lambda i: (i, 0))
    vec = pl.BlockSpec((1, d), lambda i: (0, 0))
    return pl.pallas_call(
        body, name=name, grid=(s // ts,), in_specs=[row, vec, row, row], out_specs=[row, row, vec],
        out_shape=[SDS((s, d), F32), SDS((s, d), BF), SDS((1, d), F32)], compiler_params=_params("arbitrary"))(x, gain, dh, dres)


def _ffn_act_tile(u_ext, g_ext, cw):
    acc = u_ext * cw[2:3] + _shift(u_ext, 1) * cw[1:2] + _shift(u_ext, 2) * cw[0:1]
    return _silu(acc) * g_ext


def _halo_index(rows_per_block):
    per = rows_per_block // HALO
    return lambda rb: jnp.maximum(rb * per - 1, 0)


FFN_SUB = 256


def _ext_rows(cur, prev, nxt, r0, rows, tile, keep_prev=1.0, keep_next=1.0):
    n = cur.shape[0]
    parts = []
    if r0 > 0:
        parts.append(cur[pl.ds(r0 - HALO, HALO + rows), tile].astype(F32))
    else:
        head = jnp.zeros((HALO, LANE), F32) if prev is None else prev[:, tile].astype(F32) * keep_prev
        parts += [head, cur[pl.ds(0, rows), tile].astype(F32)]
    if nxt is not False:
        if r0 + rows < n:
            parts.append(cur[pl.ds(r0 + rows, HALO), tile].astype(F32))
        else:
            parts.append(jnp.zeros((HALO, LANE), F32) if nxt is None else nxt[:, tile].astype(F32) * keep_next)
    return jnp.concatenate(parts, axis=0)


def _ffn_act_fwd(u, g, cw, name):
    s, f = u.shape
    r, tc = _tile(s, 512, HALO), _tile(f, 512)
    sub = _tile(r, FFN_SUB, HALO)
    hidx = _halo_index(r)

    def body(uc, uh, gc, cw_ref, o_ref, t_ref):
        keep = jnp.where(pl.program_id(1) == 0, 0.0, 1.0)
        for lt in range(tc // LANE):
            tile = pl.ds(lt * LANE, LANE)
            c0, c1, c2 = cw_ref[0:1, tile], cw_ref[1:2, tile], cw_ref[2:3, tile]
            for r0 in range(0, r, sub):
                u_ext = _ext_rows(uc, uh, False, r0, sub, tile, keep)
                acc = u_ext * c2 + pltpu.roll(u_ext, 1, axis=0) * c1 + pltpu.roll(u_ext, 2, axis=0) * c0
                a = _silu(acc)[HALO:] * gc[pl.ds(r0, sub), tile].astype(F32)
                o_ref[pl.ds(r0, sub), tile] = a.astype(BF)
                t_ref[tile, pl.ds(r0, sub)] = a.T.astype(BF)

    cur = pl.BlockSpec((r, tc), lambda j, rb: (rb, j))
    return pl.pallas_call(
        body, name=name, grid=(f // tc, s // r),
        in_specs=[cur, pl.BlockSpec((HALO, tc), lambda j, rb: (hidx(rb), j)), cur,
                  pl.BlockSpec((8, tc), lambda j, rb: (0, j))],
        out_specs=[cur, pl.BlockSpec((tc, r), lambda j, rb: (j, rb))],
        out_shape=[SDS((s, f), BF), SDS((f, s), BF)], compiler_params=_params("parallel", "parallel"))(u, u, g, cw)


def _ffn_act_bwd(u, g, cw, da, name):
    s, f = u.shape
    r, tc = _tile(s, 512, HALO), _tile(f, 512)
    sub = _tile(r, FFN_SUB, HALO)
    nb = s // r
    per = r // HALO
    hidx = _halo_index(r)
    nidx = lambda rb: jnp.minimum((rb + 1) * per, s // HALO - 1)
    n = sub + 2 * HALO

    def body(uc, uh, un, gc, gn, dac, dan, cw_ref, du_ref, dg_ref, dcw_ref):
        rb = pl.program_id(1)
        first = jnp.where(rb == 0, 0.0, 1.0)
        last = jnp.where(rb == nb - 1, 0.0, 1.0)
        rows = lax.broadcasted_iota(jnp.int32, (n, LANE), 0)
        own = jnp.where((rows >= HALO) & (rows < HALO + sub), 1.0, 0.0)
        wrow = lax.broadcasted_iota(jnp.int32, (8, LANE), 0)

        @pl.when(rb == 0)
        def _():
            dcw_ref[...] = jnp.zeros_like(dcw_ref)

        for lt in range(tc // LANE):
            tile = pl.ds(lt * LANE, LANE)
            c0, c1, c2 = cw_ref[0:1, tile], cw_ref[1:2, tile], cw_ref[2:3, tile]
            dcw = jnp.zeros((8, LANE), F32)
            for r0 in range(0, r, sub):
                u_ext = _ext_rows(uc, uh, un, r0, sub, tile, first, last)
                g_ext = _ext_rows(gc, None, gn, r0, sub, tile, 1.0, last)
                da_ext = _ext_rows(dac, None, dan, r0, sub, tile, 1.0, last)
                sh1, sh2 = pltpu.roll(u_ext, 1, axis=0), pltpu.roll(u_ext, 2, axis=0)
                acc = u_ext * c2 + sh1 * c1 + sh2 * c0
                sg = 1.0 / (1.0 + jnp.exp(-acc))
                dg_ref[pl.ds(r0, sub), tile] = (da_ext * (acc * sg))[HALO:HALO + sub].astype(BF)
                dacc = da_ext * g_ext * (sg * (1.0 + acc * (1.0 - sg)))
                du = dacc * c2 + pltpu.roll(dacc, n - 1, axis=0) * c1 + pltpu.roll(dacc, n - 2, axis=0) * c0
                du_ref[pl.ds(r0, sub), tile] = du[HALO:HALO + sub].astype(BF)
                dm = dacc * own
                for j, tap in enumerate((sh2, sh1, u_ext)):
                    dcw = dcw + jnp.where(wrow == j, jnp.sum(dm * tap, axis=0, keepdims=True), 0.0)
            dcw_ref[:, tile] += dcw

    cur = pl.BlockSpec((r, tc), lambda j, rb: (rb, j))
    prev = pl.BlockSpec((HALO, tc), lambda j, rb: (hidx(rb), j))
    nxt = pl.BlockSpec((HALO, tc), lambda j, rb: (nidx(rb), j))
    wspec = pl.BlockSpec((8, tc), lambda j, rb: (0, j))
    return pl.pallas_call(
        body, name=name, grid=(f // tc, nb),
        in_specs=[cur, prev, nxt, cur, nxt, cur, nxt, wspec],
        out_specs=[cur, cur, wspec], out_shape=[SDS((s, f), BF), SDS((s, f), BF), SDS((8, f), F32)],
        compiler_params=_params("parallel", "arbitrary"))(u, u, u, g, g, da, da, cw)


def _loss_head(y, target, name):
    s, d = y.shape
    ts = _tile(s, 256, 16)

    def body(y_ref, t_ref, dy_ref, dyb_ref, l_ref):
        i = pl.program_id(0)
        err = y_ref[...] - t_ref[...]
        dy_ref[...] = err * (1.0 / d)
        dyb_ref[...] = (err * (1.0 / d)).astype(BF)
        part = jnp.zeros((1, LANE), F32) + 0.5 * jnp.sum(jnp.sum(err * err, axis=1, keepdims=True), axis=0, keepdims=True) / d

        @pl.when(i == 0)
        def _():
            l_ref[...] = part

        @pl.when(i > 0)
        def _():
            l_ref[...] += part

    row = pl.BlockSpec((ts, d), lambda i: (i, 0))
    return pl.pallas_call(
        body, name=name, grid=(s // ts,), in_specs=[row, row],
        out_specs=[row, row, pl.BlockSpec((1, LANE), lambda i: (0, 0))],
        out_shape=[SDS((s, d), F32), SDS((s, d), BF), SDS((1, LANE), F32)], compiler_params=_params("arbitrary"))(y, target)


def _rel_index():
    rel = BAND_LEFT + np.arange(CHUNK)[:, None] - np.arange(BAND)[None, :]
    return (np.clip(rel, -(CHUNK - 1), MAX_REL) + (CHUNK - 1)).reshape(1, CHUNK * BAND).astype(np.int32)


def _onehot(idx_row):
    rows = lax.broadcasted_iota(jnp.int32, (NUM_REL, idx_row.shape[1]), 0)
    return jnp.where(rows == idx_row, 1.0, 0.0).astype(F32)


def _bias_expand(rel_bias, name):
    h = rel_bias.shape[0]
    n = CHUNK * BAND
    tn = n // 8

    def body(rb_ref, idx_ref, o_ref):
        o_ref[...] = jnp.dot(rb_ref[...], _onehot(idx_ref[...]), precision=HI, preferred_element_type=F32)

    out = pl.pallas_call(
        body, name=name, grid=(n // tn,),
        in_specs=[pl.BlockSpec((h, NUM_REL), lambda j: (0, 0)), pl.BlockSpec((1, tn), lambda j: (0, j))],
        out_specs=pl.BlockSpec((h, tn), lambda j: (0, j)), out_shape=SDS((h, n), F32),
        compiler_params=_params("parallel"))(rel_bias, jnp.asarray(_rel_index()))
    return out.reshape(h, CHUNK, BAND)


def _bias_reduce(dbias, name):
    h = dbias.shape[0]
    n = CHUNK * BAND
    tn = n // 8

    def body(db_ref, idx_ref, o_ref):
        j = pl.program_id(0)
        part = lax.dot_general(db_ref[...], _onehot(idx_ref[...]), NT, precision=HI, preferred_element_type=F32)

        @pl.when(j == 0)
        def _():
            o_ref[...] = part

        @pl.when(j > 0)
        def _():
            o_ref[...] += part

    return pl.pallas_call(
        body, name=name, grid=(n // tn,),
        in_specs=[pl.BlockSpec((h, tn), lambda j: (0, j)), pl.BlockSpec((1, tn), lambda j: (0, j))],
        out_specs=pl.BlockSpec((h, NUM_REL), lambda j: (0, 0)), out_shape=SDS((h, NUM_REL), F32),
        compiler_params=_params("arbitrary"))(dbias.reshape(h, n), jnp.asarray(_rel_index()))


def _headnorm(x, gain):
    outs = []
    for hh in range(x.shape[1] // HEAD):
        xh = x[:, hh * HEAD:(hh + 1) * HEAD]
        outs.append(xh * lax.rsqrt(jnp.mean(xh * xh, axis=-1, keepdims=True) + EPS) * gain)
    return jnp.concatenate(outs, axis=1)


def _qkv_post_fwd(qkv, qg, kg, name):
    s, d3 = qkv.shape
    d = d3 // 3
    r = BAND_LEFT
    nb = s // r

    def body(x_ref, qg_ref, kg_ref, q_ref, k_ref, v_ref):
        i = pl.program_id(0)
        keep = jnp.where(i == 0, 0.0, 1.0)
        q_ref[...] = _headnorm(x_ref[:, 0:d].astype(F32), qg_ref[...]).astype(BF)
        k_ref[...] = (_headnorm(x_ref[:, d:2 * d].astype(F32), kg_ref[...]) * keep).astype(BF)
        v_ref[...] = (x_ref[:, 2 * d:].astype(F32) * keep).astype(BF)

    prev = lambda i: (jnp.maximum(i - 1, 0), 0)
    vec = pl.BlockSpec((1, HEAD), lambda i: (0, 0))
    return pl.pallas_call(
        body, name=name, grid=(nb + 1,),
        in_specs=[pl.BlockSpec((r, d3), prev), vec, vec],
        out_specs=[pl.BlockSpec((r, d), prev), pl.BlockSpec((r, d), lambda i: (i, 0)), pl.BlockSpec((r, d), lambda i: (i, 0))],
        out_shape=[SDS((s, d), BF), SDS((s + r, d), BF), SDS((s + r, d), BF)],
        compiler_params=_params("arbitrary"))(qkv, qg, kg)


def _qkv_post_bwd(qkv, qg, kg, dq, dkpad, dvpad, name):
    s, d3 = qkv.shape
    d = d3 // 3
    r = _tile(s, 256, 16)
    off = BAND_LEFT // r

    def body(x_ref, qg_ref, kg_ref, dq_ref, dk_ref, dv_ref, o_ref, dqg_ref, dkg_ref):
        i = pl.program_id(0)
        _, vq = jax.vjp(_headnorm, x_ref[:, 0:d].astype(F32), qg_ref[...])
        dxq, dqg = vq(dq_ref[...])
        _, vk = jax.vjp(_headnorm, x_ref[:, d:2 * d].astype(F32), kg_ref[...])
        dxk, dkg = vk(dk_ref[...])
        o_ref[:, 0:d] = dxq.astype(BF)
        o_ref[:, d:2 * d] = dxk.astype(BF)
        o_ref[:, 2 * d:] = dv_ref[...].astype(BF)

        @pl.when(i == 0)
        def _():
            dqg_ref[...] = dqg
            dkg_ref[...] = dkg

        @pl.when(i > 0)
        def _():
            dqg_ref[...] += dqg
            dkg_ref[...] += dkg

    vec = pl.BlockSpec((1, HEAD), lambda i: (0, 0))
    row3 = pl.BlockSpec((r, d3), lambda i: (i, 0))
    row = pl.BlockSpec((r, d), lambda i: (i, 0))
    padrow = pl.BlockSpec((r, d), lambda i: (i + off, 0))
    return pl.pallas_call(
        body, name=name, grid=(s // r,), in_specs=[row3, vec, vec, row, padrow, padrow],
        out_specs=[row3, vec, vec], out_shape=[SDS((s, d3), BF), SDS((1, HEAD), F32), SDS((1, HEAD), F32)],
        compiler_params=_params("arbitrary"))(qkv, qg, kg, dq, dkpad, dvpad)


ATT_QB = 256


def _att_probs(q, kw, bias, c0):
    sc = _bdot(q, kw, NTB) * (HEAD ** -0.5) + bias
    lane = lax.broadcasted_iota(jnp.int32, sc.shape, 2)
    chunk = lax.broadcasted_iota(jnp.int32, sc.shape, 0) + c0
    sc = jnp.where(lane + chunk * CHUNK >= BAND_LEFT, sc, MASK_VALUE)
    p = jnp.exp(sc - jnp.max(sc, axis=-1, keepdims=True))
    return p / jnp.sum(p, axis=-1, keepdims=True)


def _attn_fwd(q, kpad, vpad, bias, name):
    s, d = q.shape
    h = d // HEAD
    sp = kpad.shape[0]
    qb = _tile(s, ATT_QB, CHUNK)
    per = qb // CHUNK

    def body(q_ref, k_ref, v_ref, b_ref, o_ref):
        c0 = pl.program_id(1) * per
        wins = [pl.ds(pl.multiple_of((c0 + cc) * CHUNK, CHUNK), BAND) for cc in range(per)]
        kw = jnp.stack([k_ref[w, :] for w in wins])
        vw = jnp.stack([v_ref[w, :] for w in wins])
        p = _att_probs(q_ref[...].reshape(per, CHUNK, HEAD), kw, b_ref[0], c0)
        o_ref[...] = _bdot(p.astype(BF), vw, NNB).reshape(qb, HEAD).astype(BF)

    qspec = pl.BlockSpec((qb, HEAD), lambda hh, i: (i, hh))
    kspec = pl.BlockSpec((sp, HEAD), lambda hh, i: (0, hh))
    return pl.pallas_call(
        body, name=name, grid=(h, s // qb),
        in_specs=[qspec, kspec, kspec, pl.BlockSpec((1, CHUNK, BAND), lambda hh, i: (hh, 0, 0))],
        out_specs=qspec, out_shape=SDS((s, d), BF), compiler_params=_params("parallel", "arbitrary"))(q, kpad, vpad, bias)


def _attn_bwd(q, kpad, vpad, bias, do, name):
    s, d = q.shape
    h = d // HEAD
    sp = kpad.shape[0]
    qb = _tile(s, ATT_QB, CHUNK)
    per = qb // CHUNK
    scale = HEAD ** -0.5

    def body(q_ref, k_ref, v_ref, b_ref, do_ref, dq_ref, dk_ref, dv_ref, db_ref):
        i = pl.program_id(1)

        @pl.when(i == 0)
        def _():
            dk_ref[...] = jnp.zeros_like(dk_ref)
            dv_ref[...] = jnp.zeros_like(dv_ref)
            db_ref[...] = jnp.zeros_like(db_ref)

        c0 = i * per
        wins = [pl.ds(pl.multiple_of((c0 + cc) * CHUNK, CHUNK), BAND) for cc in range(per)]
        kw = jnp.stack([k_ref[w, :] for w in wins])
        vw = jnp.stack([v_ref[w, :] for w in wins])
        qc = q_ref[...].reshape(per, CHUNK, HEAD)
        doc = do_ref[...].astype(BF).reshape(per, CHUNK, HEAD)
        p = _att_probs(qc, kw, b_ref[0], c0)
        dp = _bdot(doc, vw, NTB)
        ds = p * (dp - jnp.sum(p * dp, axis=-1, keepdims=True))
        db_ref[0] += jnp.sum(ds, axis=0)
        dsb = (ds * scale).astype(BF)
        dq_ref[...] = _bdot(dsb, kw, NNB).reshape(qb, HEAD)

        def union(x):
            tot = None
            for cc in range(per):
                parts = [x[cc]]
                if cc:
                    parts.insert(0, jnp.zeros((cc * CHUNK, HEAD), F32))
                if cc < per - 1:
                    parts.append(jnp.zeros(((per - 1 - cc) * CHUNK, HEAD), F32))
                piece = jnp.concatenate(parts, axis=0) if len(parts) > 1 else parts[0]
                tot = piece if tot is None else tot + piece
            return tot

        span = pl.ds(pl.multiple_of(c0 * CHUNK, CHUNK), BAND + (per - 1) * CHUNK)
        dk_ref[span, :] += union(_bdot(dsb, qc, TNB))
        dv_ref[span, :] += union(_bdot(p.astype(BF), doc, TNB))

    qspec = pl.BlockSpec((qb, HEAD), lambda hh, i: (i, hh))
    kspec = pl.BlockSpec((sp, HEAD), lambda hh, i: (0, hh))
    bspec = pl.BlockSpec((1, CHUNK, BAND), lambda hh, i: (hh, 0, 0))
    return pl.pallas_call(
        body, name=name, grid=(h, s // qb), in_specs=[qspec, kspec, kspec, bspec, qspec],
        out_specs=[qspec, kspec, kspec, bspec],
        out_shape=[SDS((s, d), F32), SDS((sp, d), F32), SDS((sp, d), F32), SDS((h, CHUNK, BAND), F32)],
        compiler_params=_params("parallel", "arbitrary"))(q, kpad, vpad, bias, do)


def _pool_tile(x_ext, gain, w4, scale, row0):
    n, d = x_ext.shape
    dg = d // len(POOL_WINDOWS)
    pos = lax.broadcasted_iota(jnp.int32, (n, 1), 0) + row0
    hn = _rms(x_ext, gain) * jnp.where(pos >= 0, 1.0, 0.0)
    outs = []
    for gi, w in enumerate(POOL_WINDOWS):
        hg = hn[:, gi * dg:(gi + 1) * dg]
        acc, k = hg, 1
        while k < w:
            acc = acc + _shift(acc, k)
            k *= 2
        inv = 1.0 / jnp.clip(pos + 1, 1, w).astype(F32)
        pooled = acc * inv - hg
        outs.append(jnp.dot(pooled.astype(BF), w4[gi].astype(BF), preferred_element_type=F32))
    return jnp.concatenate(outs, axis=1) * scale


POOL_ROWS = 128


def _pool_fwd(x, gain, w4, scale, name):
    s, d = x.shape
    r = _tile(s, POOL_ROWS, HALO)
    hidx = _halo_index(r)

    def body(xc, xh, g_ref, w_ref, s_ref, o_ref):
        rb = pl.program_id(0)
        x_ext = jnp.concatenate([xh[...], xc[...]], axis=0)
        y = _pool_tile(x_ext, g_ref[...], [w_ref[gi] for gi in range(len(POOL_WINDOWS))], s_ref[...], rb * r - HALO)
        o_ref[...] = xc[...] + y[HALO:]

    cur = pl.BlockSpec((r, d), lambda rb: (rb, 0))
    vec = pl.BlockSpec((1, d), lambda rb: (0, 0))
    return pl.pallas_call(
        body, name=name, grid=(s // r,),
        in_specs=[cur, pl.BlockSpec((HALO, d), lambda rb: (hidx(rb), 0)), vec,
                  pl.BlockSpec(w4.shape, lambda rb: (0, 0, 0)), vec],
        out_specs=cur, out_shape=SDS((s, d), F32), compiler_params=_params("parallel"))(x, x, gain, w4, scale)


def _pool_bwd(x, gain, w4, scale, dy, name):
    s, d = x.shape
    r = _tile(s, POOL_ROWS, HALO)
    nb = s // r
    hidx = _halo_index(r)

    def body(xc, xh, g_ref, w_ref, s_ref, dy_ref, dx_ref, dg_ref, dw_ref, ds_ref, carry):
        step = pl.program_id(0)
        rb = nb - 1 - step
        x_ext = jnp.concatenate([xh[...], xc[...]], axis=0)
        fn = functools.partial(_pool_tile, row0=rb * r - HALO)
        _, vjp = jax.vjp(fn, x_ext, g_ref[...], [w_ref[gi] for gi in range(len(POOL_WINDOWS))], s_ref[...])
        ct = jnp.concatenate([jnp.zeros((HALO, d), F32), dy_ref[...]], axis=0)
        dx_ext, dg, dws, dsc = vjp(ct)

        @pl.when(step == 0)
        def _():
            carry[...] = jnp.zeros_like(carry)
            dg_ref[...] = jnp.zeros_like(dg_ref)
            dw_ref[...] = jnp.zeros_like(dw_ref)
            ds_ref[...] = jnp.zeros_like(ds_ref)

        dx_ref[...] = dy_ref[...] + dx_ext[HALO:]
        dx_ref[pl.ds(r - HALO, HALO), :] += carry[...]
        carry[...] = dx_ext[:HALO]
        dg_ref[...] += dg
        for gi, dw in enumerate(dws):
            dw_ref[gi] += dw
        ds_ref[...] += dsc

    cur = pl.BlockSpec((r, d), lambda t: (nb - 1 - t, 0))
    vec = pl.BlockSpec((1, d), lambda t: (0, 0))
    wspec = pl.BlockSpec(w4.shape, lambda t: (0, 0, 0))
    return pl.pallas_call(
        body, name=name, grid=(nb,),
        in_specs=[cur, pl.BlockSpec((HALO, d), lambda t: (hidx(nb - 1 - t), 0)), vec, wspec, vec, cur],
        out_specs=[cur, vec, wspec, vec],
        out_shape=[SDS((s, d), F32), SDS((1, d), F32), SDS(w4.shape, F32), SDS((1, d), F32)],
        scratch_shapes=[pltpu.VMEM((HALO, d), F32)], compiler_params=_params("arbitrary"))(x, x, gain, w4, scale, dy)


def _gdn_post(acc, kind):
    y = _silu(acc)
    if kind != "v":
        y = y * lax.rsqrt(jnp.sum(y * y, axis=-1, keepdims=True) + EPS)
    if kind == "q":
        y = y * (HEAD ** -0.5)
    return y


GDN_SUB = 128
GDN_CONV_HEADS = 4


def _gdn_conv_fwd(proj, cw, kind, head0, nheads, name):
    s = proj.shape[0]
    r = _tile(s, 512, HALO)
    sub = _tile(r, GDN_SUB, HALO)
    hb = min(GDN_CONV_HEADS, nheads)
    assert head0 % hb == 0 and nheads % hb == 0
    tc = hb * HEAD
    hidx = _halo_index(r)

    def body(uc, uh, cw_ref, o_ref):
        keep = jnp.where(pl.program_id(1) == 0, 0.0, 1.0)
        for hh in range(hb):
            tile = pl.ds(hh * HEAD, HEAD)
            taps = [cw_ref[j:j + 1, tile] for j in range(GDN_CONV)]
            for r0 in range(0, r, sub):
                u_ext = _ext_rows(uc, uh, False, r0, sub, tile, keep)
                acc = u_ext * taps[3]
                for j in range(1, GDN_CONV):
                    acc = acc + pltpu.roll(u_ext, j, axis=0) * taps[3 - j]
                o_ref[pl.ds(r0, sub), tile] = _gdn_post(acc, kind)[HALO:].astype(BF)

    return pl.pallas_call(
        body, name=name, grid=(nheads // hb, s // r),
        in_specs=[pl.BlockSpec((r, tc), lambda j, rb: (rb, head0 // hb + j)),
                  pl.BlockSpec((HALO, tc), lambda j, rb: (hidx(rb), head0 // hb + j)),
                  pl.BlockSpec((8, tc), lambda j, rb: (0, head0 // hb + j))],
        out_specs=pl.BlockSpec((r, tc), lambda j, rb: (rb, j)), out_shape=SDS((s, nheads * HEAD), BF),
        compiler_params=_params("parallel", "parallel"))(proj, proj, cw)


def _gdn_conv_bwd(proj, cw, dy, kind, head0, nheads, name):
    s = proj.shape[0]
    r = _tile(s, 512, HALO)
    sub = _tile(r, GDN_SUB, HALO)
    nb = s // r
    per = r // HALO
    hb = min(GDN_CONV_HEADS, nheads)
    tc = hb * HEAD
    hidx = _halo_index(r)
    nidx = lambda rb: jnp.minimum((rb + 1) * per, s // HALO - 1)
    rep = dy.shape[1] // (nheads * HEAD)
    n = sub + 2 * HALO

    def body(uc, uh, un, cw_ref, dyc, dyn, du_ref, dcw_ref):
        rb = pl.program_id(1)
        first = jnp.where(rb == 0, 0.0, 1.0)
        last = jnp.where(rb == nb - 1, 0.0, 1.0)
        rows = lax.broadcasted_iota(jnp.int32, (n, HEAD), 0)
        own = jnp.where((rows >= HALO) & (rows < HALO + sub), 1.0, 0.0)
        wrow = lax.broadcasted_iota(jnp.int32, (8, HEAD), 0)

        @pl.when(rb == 0)
        def _():
            dcw_ref[...] = jnp.zeros_like(dcw_ref)

        for hh in range(hb):
            tile = pl.ds(hh * HEAD, HEAD)
            taps = [cw_ref[j:j + 1, tile] for j in range(GDN_CONV)]
            dcw = jnp.zeros((8, HEAD), F32)
            for r0 in range(0, r, sub):
                u_ext = _ext_rows(uc, uh, un, r0, sub, tile, first, last)
                dy_ext = _ext_rows(dyc, None, dyn, r0, sub, pl.ds(hh * rep * HEAD, HEAD), 1.0, last)
                for e in range(1, rep):
                    dy_ext = dy_ext + _ext_rows(dyc, None, dyn, r0, sub, pl.ds((hh * rep + e) * HEAD, HEAD), 1.0, last)
                shifted = [u_ext] + [pltpu.roll(u_ext, j, axis=0) for j in range(1, GDN_CONV)]
                acc = shifted[0] * taps[3]
                for j in range(1, GDN_CONV):
                    acc = acc + shifted[j] * taps[3 - j]
                _, vjp = jax.vjp(functools.partial(_gdn_post, kind=kind), acc)
                dacc, = vjp(dy_ext)
                du = dacc * taps[3]
                for j in range(1, GDN_CONV):
                    du = du + pltpu.roll(dacc, n - j, axis=0) * taps[3 - j]
                du_ref[pl.ds(r0, sub), tile] = du[HALO:HALO + sub].astype(BF)
                dm = dacc * own
                for j in range(GDN_CONV):
                    dcw = dcw + jnp.where(wrow == j, jnp.sum(dm * shifted[3 - j], axis=0, keepdims=True), 0.0)
            dcw_ref[:, tile] += dcw

    ucol = lambda j: head0 // hb + j
    return pl.pallas_call(
        body, name=name, grid=(nheads // hb, nb),
        in_specs=[pl.BlockSpec((r, tc), lambda j, rb: (rb, ucol(j))),
                  pl.BlockSpec((HALO, tc), lambda j, rb: (hidx(rb), ucol(j))),
                  pl.BlockSpec((HALO, tc), lambda j, rb: (nidx(rb), ucol(j))),
                  pl.BlockSpec((8, tc), lambda j, rb: (0, ucol(j))),
                  pl.BlockSpec((r, rep * tc), lambda j, rb: (rb, j)),
                  pl.BlockSpec((HALO, rep * tc), lambda j, rb: (nidx(rb), j))],
        out_specs=[pl.BlockSpec((r, tc), lambda j, rb: (rb, j)), pl.BlockSpec((8, tc), lambda j, rb: (0, j))],
        out_shape=[SDS((s, nheads * HEAD), BF), SDS((8, nheads * HEAD), F32)],
        compiler_params=_params("parallel", "arbitrary"))(proj, proj, proj, cw, dy, dy)


GATE_ROWS = 256


def _gates_tile(a, bt, a_log, dt_bias, hv):
    r = a.shape[0]
    z = a + dt_bias
    softplus = jnp.maximum(z, 0.0) + jnp.log(1.0 + jnp.exp(-jnp.abs(z)))
    g = -jnp.exp(a_log) * softplus
    ri = lax.broadcasted_iota(jnp.int32, (r, r), 0)
    ci = lax.broadcasted_iota(jnp.int32, (r, r), 1)
    same_chunk = jnp.right_shift(ri, 6) == jnp.right_shift(ci, 6)
    tri = jnp.where(same_chunk, jnp.where(ri >= ci, 1.0, 0.0), 0.0).astype(F32)
    gc = jnp.dot(tri, g, precision=HI, preferred_element_type=F32)
    beta = 1.0 / (1.0 + jnp.exp(-bt))
    er = lax.broadcasted_iota(jnp.int32, (LANE, hv * HEAD), 0)
    ec = lax.broadcasted_iota(jnp.int32, (LANE, hv * HEAD), 1)
    expand = jnp.where(er == jnp.right_shift(ec, 7), 1.0, 0.0).astype(F32)
    return (jnp.dot(gc, expand, precision=HI, preferred_element_type=F32),
            jnp.dot(beta, expand, precision=HI, preferred_element_type=F32))


def _gates_fwd(ab, a_log, dt_bias, hv, name):
    s = ab.shape[0]
    r = _tile(s, GATE_ROWS, CHUNK)

    def body(a_ref, b_ref, al_ref, dt_ref, gc_ref, bb_ref):
        gcb, btb = _gates_tile(a_ref[...], b_ref[...], al_ref[...], dt_ref[...], hv)
        gc_ref[...] = gcb
        bb_ref[...] = btb

    vec = pl.BlockSpec((1, LANE), lambda i: (0, 0))
    wide = pl.BlockSpec((r, hv * HEAD), lambda i: (i, 0))
    return pl.pallas_call(
        body, name=name, grid=(s // r,),
        in_specs=[pl.BlockSpec((r, LANE), lambda i: (i, 0)), pl.BlockSpec((r, LANE), lambda i: (i, 1)), vec, vec],
        out_specs=[wide, wide], out_shape=[SDS((s, hv * HEAD), F32)] * 2,
        compiler_params=_params("parallel"))(ab, ab, a_log, dt_bias)


def _gates_bwd(ab, a_log, dt_bias, dgcb, dbtb, hv, name):
    s = ab.shape[0]
    r = _tile(s, GATE_ROWS, CHUNK)

    def body(a_ref, b_ref, al_ref, dt_ref, dgc_ref, dbb_ref, dab_ref, dal_ref, ddt_ref):
        i = pl.program_id(0)
        _, vjp = jax.vjp(functools.partial(_gates_tile, hv=hv), a_ref[...], b_ref[...], al_ref[...], dt_ref[...])
        da, dbt, dal, ddt = vjp((dgc_ref[...], dbb_ref[...]))
        dab_ref[:, 0:LANE] = da
        dab_ref[:, LANE:] = dbt

        @pl.when(i == 0)
        def _():
            dal_ref[...] = dal
            ddt_ref[...] = ddt

        @pl.when(i > 0)
        def _():
            dal_ref[...] += dal
            ddt_ref[...] += ddt

    vec = pl.BlockSpec((1, LANE), lambda i: (0, 0))
    wide = pl.BlockSpec((r, hv * HEAD), lambda i: (i, 0))
    return pl.pallas_call(
        body, name=name, grid=(s // r,),
        in_specs=[pl.BlockSpec((r, LANE), lambda i: (i, 0)), pl.BlockSpec((r, LANE), lambda i: (i, 1)), vec, vec, wide, wide],
        out_specs=[pl.BlockSpec((r, 2 * LANE), lambda i: (i, 0)), vec, vec],
        out_shape=[SDS((s, 2 * LANE), F32), SDS((1, LANE), F32), SDS((1, LANE), F32)],
        compiler_params=_params("arbitrary"))(ab, ab, a_log, dt_bias, dgcb, dbtb)


def _split_bf16(a):
    hi = a.astype(BF)
    return hi, (a - hi.astype(F32)).astype(BF)


def _dot3(a, b, dims=(((1,), (0,)), ((), ()))):
    ah, al = _split_bf16(a)
    bh, bl = _split_bf16(b)
    d = lambda x, y: lax.dot_general(x, y, dims, preferred_element_type=F32)
    return d(ah, bh) + (d(ah, bl) + d(al, bh))


NNB = (((2,), (1,)), ((0,), (0,)))
NTB = (((2,), (2,)), ((0,), (0,)))
TNB = (((1,), (1,)), ((0,), (0,)))


def _bdot(a, b, dims):
    return lax.dot_general(a, b, dims, preferred_element_type=F32)


def _unit_lower_inverse(a):
    ri = lax.broadcasted_iota(jnp.int32, a.shape, 1)
    ci = lax.broadcasted_iota(jnp.int32, a.shape, 2)
    p = -a
    t = jnp.where(ri == ci, 1.0, 0.0) + p
    for _ in range(5):
        p = _dot3(p, p, NNB)
        t = t + _dot3(t, p, NNB)
    return t


@jax.custom_vjp
def _known_inverse(a, t):
    return t


def _known_inverse_fwd(a, t):
    return t, t


def _known_inverse_bwd(t, g):
    return -_dot3(_dot3(t, g, TNB), t, NTB), jnp.zeros_like(t)


_known_inverse.defvjp(_known_inverse_fwd, _known_inverse_bwd)


def _delta_decay(gcb):
    c = CHUNK
    shape = (gcb.shape[0], c, c)
    ri = lax.broadcasted_iota(jnp.int32, shape, 1)
    ci = lax.broadcasted_iota(jnp.int32, shape, 2)
    causal = ri >= ci
    grow = jnp.stack([jnp.concatenate([gcb[b], gcb[b]], axis=0).T[:c, :c] for b in range(shape[0])])
    return jnp.where(causal, jnp.exp(jnp.where(causal, gcb[:, :, :c] - grow, 0.0)), 0.0), ri > ci


def _delta_system(k, gcb, btb):
    decay, strict = _delta_decay(gcb)
    return jnp.where(strict, _bdot((k * btb).astype(BF), k.astype(BF), NTB) * decay, 0.0)


def _delta_prep(q, k, v, gcb, btb, tinv):
    decay, strict = _delta_decay(gcb)
    kb = k * btb
    kbf = k.astype(BF)
    a = jnp.where(strict, _bdot(kb.astype(BF), kbf, NTB) * decay, 0.0)
    t = _known_inverse(a, tinv).astype(BF)
    u = _bdot(t, (v * btb).astype(BF), NNB)
    w = _bdot(t, (kb * jnp.exp(gcb)).astype(BF), NNB)
    attn = _bdot(q.astype(BF), kbf, NTB) * decay
    return u, w, attn


def _delta_scan(u, w, attn, q, k, gcb, s_in):
    c = CHUNK
    glast = gcb[:, c - 1:c, :]
    sb = s_in.astype(BF)
    v_new = u - _bdot(w.astype(BF), sb, NNB)
    vnb = v_new.astype(BF)
    o = _bdot((q * jnp.exp(gcb)).astype(BF), sb, NNB) + _bdot(attn.astype(BF), vnb, NNB)
    ks = (k * jnp.exp(glast - gcb)).astype(BF)
    s_out = s_in * jnp.exp(glast[:, :, 0:1]) + _bdot(ks, vnb, TNB)
    return o, s_out


def _head_stack(ref, rows, width, heads, rep=1):
    return jnp.stack([ref[rows, pl.ds((hh // rep) * width, width)].astype(F32) for hh in range(heads)])


PREP_ROWS = 512
PREP_HEADS = 2
SCAN_ROWS = 512
SCAN_HEADS = 4


def _delta_prep_fwd(q, k, v, gcb, btb, name):
    s, dv = v.shape
    hv = dv // HEAD
    g = PREP_HEADS
    assert dv // q.shape[1] == g
    r = _tile(s, PREP_ROWS, CHUNK)

    def body(q_ref, k_ref, v_ref, g_ref, b_ref, u_ref, w_ref, a_ref, t_ref):
        nb = r // CHUNK
        qc = q_ref[...].astype(F32).reshape(nb, CHUNK, HEAD)
        kc = k_ref[...].astype(F32).reshape(nb, CHUNK, HEAD)
        for hh in range(g):
            cols = pl.ds(hh * HEAD, HEAD)
            half = pl.ds(hh * CHUNK, CHUNK)
            gc = g_ref[:, cols].reshape(nb, CHUNK, HEAD)
            bc = b_ref[:, cols].reshape(nb, CHUNK, HEAD)
            tinv = _unit_lower_inverse(_delta_system(kc, gc, bc))
            u, w, attn = _delta_prep(qc, kc, v_ref[:, cols].astype(F32).reshape(nb, CHUNK, HEAD), gc, bc, tinv)
            u_ref[:, cols] = u.reshape(r, HEAD)
            w_ref[:, cols] = w.reshape(r, HEAD).astype(BF)
            a_ref[:, half] = attn.reshape(r, CHUNK).astype(BF)
            t_ref[:, half] = tinv.reshape(r, CHUNK)

    kq = pl.BlockSpec((r, HEAD), lambda j, i: (i, j))
    vs = pl.BlockSpec((r, g * HEAD), lambda j, i: (i, j))
    sq = pl.BlockSpec((r, g * CHUNK), lambda j, i: (i, j))
    return pl.pallas_call(
        body, name=name, grid=(hv // g, s // r), in_specs=[kq, kq, vs, vs, vs], out_specs=[vs, vs, sq, sq],
        out_shape=[SDS((s, dv), F32), SDS((s, dv), BF), SDS((s, hv * CHUNK), BF), SDS((s, hv * CHUNK), F32)],
        compiler_params=_params("parallel", "parallel"))(q, k, v, gcb, btb)


def _delta_prep_bwd(q, k, v, gcb, btb, tinv, du, dw, dattn, dq_s, dk_s, dg_s, name):
    s, dv = v.shape
    hv = dv // HEAD
    g = PREP_HEADS
    r = _tile(s, PREP_ROWS, CHUNK)

    def body(q_ref, k_ref, v_ref, g_ref, b_ref, t_ref, du_ref, dw_ref, da_ref, dqs_ref, dks_ref, dgs_ref,
             dq_ref, dk_ref, dv_ref, dg_ref, db_ref):
        nb = r // CHUNK
        wide = lambda ref, cols: ref[:, cols].astype(F32).reshape(nb, CHUNK, HEAD)
        qc = q_ref[...].astype(F32).reshape(nb, CHUNK, HEAD)
        kc = k_ref[...].astype(F32).reshape(nb, CHUNK, HEAD)
        for hh in range(g):
            cols = pl.ds(hh * HEAD, HEAD)
            half = pl.ds(hh * CHUNK, CHUNK)
            fn = functools.partial(_delta_prep, tinv=t_ref[:, half].reshape(nb, CHUNK, CHUNK))
            _, vjp = jax.vjp(fn, qc, kc, wide(v_ref, cols), wide(g_ref, cols), wide(b_ref, cols))
            dq, dk, dvv, dg, db = vjp((wide(du_ref, cols), wide(dw_ref, cols),
                                       da_ref[:, half].astype(F32).reshape(nb, CHUNK, CHUNK)))
            dq_ref[:, cols] = dq.reshape(r, HEAD) + dqs_ref[:, cols]
            dk_ref[:, cols] = dk.reshape(r, HEAD) + dks_ref[:, cols]
            dv_ref[:, cols] = dvv.reshape(r, HEAD)
            dg_ref[:, cols] = dg.reshape(r, HEAD) + dgs_ref[:, cols]
            db_ref[:, cols] = db.reshape(r, HEAD)

    kq = pl.BlockSpec((r, HEAD), lambda j, i: (i, j))
    vs = pl.BlockSpec((r, g * HEAD), lambda j, i: (i, j))
    sq = pl.BlockSpec((r, g * CHUNK), lambda j, i: (i, j))
    return pl.pallas_call(
        body, name=name, grid=(hv // g, s // r), in_specs=[kq, kq, vs, vs, vs, sq, vs, vs, sq, vs, vs, vs],
        out_specs=[vs] * 5, out_shape=[SDS((s, dv), F32)] * 5,
        compiler_params=_params("parallel", "parallel"))(q, k, v, gcb, btb, tinv, du, dw, dattn, dq_s, dk_s, dg_s)


def _delta_scan_fwd(u, w, attn, q, k, gcb, name):
    s, dv = u.shape
    hv = dv // HEAD
    rep = dv // q.shape[1]
    g = min(SCAN_HEADS, hv)
    r = _tile(s, SCAN_ROWS, CHUNK)
    per = r // CHUNK

    def body(u_ref, w_ref, a_ref, q_ref, k_ref, g_ref, o_ref, st_ref, state):
        @pl.when(pl.program_id(1) == 0)
        def _():
            state[...] = jnp.zeros_like(state)

        def chunk(cc, carry):
            rows = pl.ds(pl.multiple_of(cc * CHUNK, CHUNK), CHUNK)
            s_in = state[...]
            o, s_out = _delta_scan(_head_stack(u_ref, rows, HEAD, g), _head_stack(w_ref, rows, HEAD, g),
                                   _head_stack(a_ref, rows, CHUNK, g), _head_stack(q_ref, rows, HEAD, g, rep),
                                   _head_stack(k_ref, rows, HEAD, g, rep), _head_stack(g_ref, rows, HEAD, g), s_in)
            for hh in range(g):
                st_ref[hh, cc] = s_in[hh]
                o_ref[rows, pl.ds(hh * HEAD, HEAD)] = o[hh].astype(BF)
            state[...] = s_out
            return carry

        lax.fori_loop(0, per, chunk, 0)

    kq = pl.BlockSpec((r, g // rep * HEAD), lambda j, i: (i, j))
    vs = pl.BlockSpec((r, g * HEAD), lambda j, i: (i, j))
    sq = pl.BlockSpec((r, g * CHUNK), lambda j, i: (i, j))
    return pl.pallas_call(
        body, name=name, grid=(hv // g, s // r), in_specs=[vs, vs, sq, kq, kq, vs],
        out_specs=[vs, pl.BlockSpec((g, per, HEAD, HEAD), lambda j, i: (j, i, 0, 0))],
        out_shape=[SDS((s, dv), BF), SDS((hv, s // CHUNK, HEAD, HEAD), F32)],
        scratch_shapes=[pltpu.VMEM((g, HEAD, HEAD), F32)],
        compiler_params=_params("parallel", "arbitrary"))(u, w, attn, q, k, gcb)


def _delta_scan_bwd(u, w, attn, q, k, gcb, states, do, name):
    s, dv = u.shape
    hv = dv // HEAD
    rep = dv // q.shape[1]
    g = min(SCAN_HEADS, hv)
    r = _tile(s, SCAN_ROWS, CHUNK)
    per = r // CHUNK
    nb = s // r

    def body(u_ref, w_ref, a_ref, q_ref, k_ref, g_ref, st_ref, do_ref, du_ref, dw_ref, da_ref, dq_ref, dk_ref, dg_ref, dstate):
        @pl.when(pl.program_id(1) == 0)
        def _():
            dstate[...] = jnp.zeros_like(dstate)

        def chunk(t, carry):
            cc = per - 1 - t
            rows = pl.ds(pl.multiple_of(cc * CHUNK, CHUNK), CHUNK)
            s_in = jnp.stack([st_ref[hh, cc] for hh in range(g)])
            _, vjp = jax.vjp(_delta_scan, _head_stack(u_ref, rows, HEAD, g), _head_stack(w_ref, rows, HEAD, g),
                             _head_stack(a_ref, rows, CHUNK, g), _head_stack(q_ref, rows, HEAD, g, rep),
                             _head_stack(k_ref, rows, HEAD, g, rep), _head_stack(g_ref, rows, HEAD, g), s_in)
            du, dw, da, dq, dk, dg, ds_in = vjp((_head_stack(do_ref, rows, HEAD, g), dstate[...]))
            for hh in range(g):
                cols = pl.ds(hh * HEAD, HEAD)
                du_ref[rows, cols] = du[hh].astype(BF)
                dw_ref[rows, cols] = dw[hh].astype(BF)
                da_ref[rows, pl.ds(hh * CHUNK, CHUNK)] = da[hh].astype(BF)
                dq_ref[rows, cols] = dq[hh]
                dk_ref[rows, cols] = dk[hh]
                dg_ref[rows, cols] = dg[hh]
            dstate[...] = ds_in
            return carry

        lax.fori_loop(0, per, chunk, 0)

    kq = pl.BlockSpec((r, g // rep * HEAD), lambda j, i: (nb - 1 - i, j))
    vs = pl.BlockSpec((r, g * HEAD), lambda j, i: (nb - 1 - i, j))
    sq = pl.BlockSpec((r, g * CHUNK), lambda j, i: (nb - 1 - i, j))
    return pl.pallas_call(
        body, name=name, grid=(hv // g, nb),
        in_specs=[vs, vs, sq, kq, kq, vs, pl.BlockSpec((g, per, HEAD, HEAD), lambda j, i: (j, nb - 1 - i, 0, 0)), vs],
        out_specs=[vs, vs, sq, vs, vs, vs],
        out_shape=[SDS((s, dv), BF), SDS((s, dv), BF), SDS((s, hv * CHUNK), BF)] + [SDS((s, dv), F32)] * 3,
        scratch_shapes=[pltpu.VMEM((g, HEAD, HEAD), F32)],
        compiler_params=_params("parallel", "arbitrary"))(u, w, attn, q, k, gcb, states, do)


def _delta_chunk(q, k, v, gcb, btb, s_in):
    c = CHUNK
    ri = lax.broadcasted_iota(jnp.int32, (c, c), 0)
    ci = lax.broadcasted_iota(jnp.int32, (c, c), 1)
    causal = ri >= ci
    gcol = gcb[:, :c]
    grow = jnp.concatenate([gcb, gcb], axis=0).T[:c, :c]
    decay = jnp.where(causal, jnp.exp(jnp.where(causal, gcol - grow, 0.0)), 0.0)
    kb = k * btb
    vb = v * btb
    kbf = k.astype(BF)
    a = jnp.where(ri > ci, lax.dot_general(kb.astype(BF), kbf, NT, preferred_element_type=F32) * decay, 0.0)
    p = -a
    t = jnp.where(ri == ci, 1.0, 0.0) + p
    for _ in range(5):
        p = jnp.dot(p, p, precision=HI, preferred_element_type=F32)
        t = t + jnp.dot(t, p, precision=HI, preferred_element_type=F32)
    eg = jnp.exp(gcb)
    u = jnp.dot(t, vb, precision=HI, preferred_element_type=F32)
    w = jnp.dot(t, kb * eg, precision=HI, preferred_element_type=F32)
    attn = lax.dot_general(q.astype(BF), kbf, NT, preferred_element_type=F32) * decay
    glast = gcb[c - 1:c, :]
    ks = k * jnp.exp(glast - gcb)
    sb = s_in.astype(BF)
    v_new = u - jnp.dot(w.astype(BF), sb, preferred_element_type=F32)
    o = (jnp.dot((q * eg).astype(BF), sb, preferred_element_type=F32)
         + jnp.dot(attn.astype(BF), v_new.astype(BF), preferred_element_type=F32))
    s_out = s_in * jnp.exp(glast[:, 0:1]) + lax.dot_general(ks.astype(BF), v_new.astype(BF), TN, preferred_element_type=F32)
    return o, s_out


GDN_ROWS = 512


def _delta_fwd(q, k, v, gcb, btb, name):
    s, dv = v.shape
    hv = dv // HEAD
    rep = dv // q.shape[1]
    r = _tile(s, GDN_ROWS, CHUNK)
    per = r // CHUNK
    nc = s // CHUNK

    def body(q_ref, k_ref, v_ref, g_ref, b_ref, o_ref, st_ref, state):
        @pl.when(pl.program_id(1) == 0)
        def _():
            state[...] = jnp.zeros_like(state)

        def chunk(cc, carry):
            rows = pl.ds(pl.multiple_of(cc * CHUNK, CHUNK), CHUNK)
            st_ref[0, cc] = state[...]
            o, s_out = _delta_chunk(q_ref[rows, :].astype(F32), k_ref[rows, :].astype(F32), v_ref[rows, :].astype(F32),
                                    g_ref[rows, :], b_ref[rows, :], state[...])
            o_ref[rows, :] = o.astype(BF)
            state[...] = s_out
            return carry

        lax.fori_loop(0, per, chunk, 0)

    kq = pl.BlockSpec((r, HEAD), lambda h, i: (i, h // rep))
    vs = pl.BlockSpec((r, HEAD), lambda h, i: (i, h))
    return pl.pallas_call(
        body, name=name, grid=(hv, s // r), in_specs=[kq, kq, vs, vs, vs],
        out_specs=[vs, pl.BlockSpec((1, per, HEAD, HEAD), lambda h, i: (h, i, 0, 0))],
        out_shape=[SDS((s, dv), BF), SDS((hv, nc, HEAD, HEAD), F32)],
        scratch_shapes=[pltpu.VMEM((HEAD, HEAD), F32)],
        compiler_params=_params("parallel", "arbitrary"))(q, k, v, gcb, btb)


def _delta_bwd(q, k, v, gcb, btb, states, do, name):
    s, dv = v.shape
    hv = dv // HEAD
    rep = dv // q.shape[1]
    r = _tile(s, GDN_ROWS, CHUNK)
    per = r // CHUNK
    nb = s // r

    def body(q_ref, k_ref, v_ref, g_ref, b_ref, st_ref, do_ref, dq_ref, dk_ref, dv_ref, dg_ref, db_ref, dstate):
        @pl.when(pl.program_id(1) == 0)
        def _():
            dstate[...] = jnp.zeros_like(dstate)

        def chunk(t, carry):
            cc = per - 1 - t
            rows = pl.ds(pl.multiple_of(cc * CHUNK, CHUNK), CHUNK)
            _, vjp = jax.vjp(_delta_chunk, q_ref[rows, :].astype(F32), k_ref[rows, :].astype(F32),
                             v_ref[rows, :].astype(F32), g_ref[rows, :], b_ref[rows, :], st_ref[0, cc])
            dq, dk, dvv, dg, db, ds_in = vjp((do_ref[rows, :].astype(F32), dstate[...]))
            dq_ref[rows, :] = dq
            dk_ref[rows, :] = dk
            dv_ref[rows, :] = dvv
            dg_ref[rows, :] = dg
            db_ref[rows, :] = db
            dstate[...] = ds_in
            return carry

        lax.fori_loop(0, per, chunk, 0)

    kq = pl.BlockSpec((r, HEAD), lambda h, i: (nb - 1 - i, h // rep))
    vs = pl.BlockSpec((r, HEAD), lambda h, i: (nb - 1 - i, h))
    return pl.pallas_call(
        body, name=name, grid=(hv, nb),
        in_specs=[kq, kq, vs, vs, vs, pl.BlockSpec((1, per, HEAD, HEAD), lambda h, i: (h, nb - 1 - i, 0, 0)), vs],
        out_specs=[vs] * 5, out_shape=[SDS((s, dv), F32)] * 5,
        scratch_shapes=[pltpu.VMEM((HEAD, HEAD), F32)],
        compiler_params=_params("parallel", "arbitrary"))(q, k, v, gcb, btb, states, do)


def _gdn_out_tile(o, gate, gain):
    return _headnorm(o, gain) * _silu(gate)


def _gdn_out_fwd(o, proj, gate_col0, gain, name):
    s, dv = o.shape
    r = _tile(s, 128, 16)

    def body(o_ref, g_ref, gain_ref, y_ref):
        y_ref[...] = _gdn_out_tile(o_ref[...].astype(F32), g_ref[...].astype(F32), gain_ref[...]).astype(BF)

    row = pl.BlockSpec((r, dv), lambda i: (i, 0))
    return pl.pallas_call(
        body, name=name, grid=(s // r,),
        in_specs=[row, pl.BlockSpec((r, dv), lambda i: (i, gate_col0)), pl.BlockSpec((1, HEAD), lambda i: (0, 0))],
        out_specs=row, out_shape=SDS((s, dv), BF), compiler_params=_params("parallel"))(o, proj, gain)


def _gdn_out_bwd(o, proj, gate_col0, gain, dy, name):
    s, dv = o.shape
    r = _tile(s, 128, 16)

    def body(o_ref, g_ref, gain_ref, dy_ref, do_ref, dg_ref, dgain_ref):
        i = pl.program_id(0)
        _, vjp = jax.vjp(_gdn_out_tile, o_ref[...].astype(F32), g_ref[...].astype(F32), gain_ref[...])
        do, dg, dgain = vjp(dy_ref[...].astype(F32))
        do_ref[...] = do
        dg_ref[...] = dg.astype(BF)

        @pl.when(i == 0)
        def _():
            dgain_ref[...] = dgain

        @pl.when(i > 0)
        def _():
            dgain_ref[...] += dgain

    row = pl.BlockSpec((r, dv), lambda i: (i, 0))
    vec = pl.BlockSpec((1, HEAD), lambda i: (0, 0))
    return pl.pallas_call(
        body, name=name, grid=(s // r,),
        in_specs=[row, pl.BlockSpec((r, dv), lambda i: (i, gate_col0)), vec, row],
        out_specs=[row, row, vec], out_shape=[SDS((s, dv), F32), SDS((s, dv), BF), SDS((1, HEAD), F32)],
        compiler_params=_params("arbitrary"))(o, proj, gain, dy)


def _ffn_forward(x, gain, wu, wg, cw, wd, tag):
    h, ht = _rmsnorm_fwd(x, gain, f"{tag}_norm")
    uu = _mm(h, wu, name=f"{tag}_up_u")
    ug = _mm(h, wg, name=f"{tag}_up_g")
    a, at = _ffn_act_fwd(uu, ug, cw, f"{tag}_act")
    y = _mm(a, wd, res=x, out_dtype=F32, name=f"{tag}_down")
    return y, (x, ht, uu, ug, at)


def _ffn_backward(saved, dys, gain, wu, wg, cw, wd, tag):
    x, ht, uu, ug, at = saved
    dy, dyb = dys
    da = _mm(dyb, wd, tb=True, name=f"{tag}_d_act")
    dwd = _mm(at, dyb, out_dtype=F32, name=f"{tag}_d_wd")
    duu, dug, dcw = _ffn_act_bwd(uu, ug, cw, da, f"{tag}_act_bwd")
    dh = _mm(duu, wu, tb=True, out_dtype=F32, name=f"{tag}_d_h_u")
    dh = _mm(dug, wg, tb=True, res=dh, out_dtype=F32, name=f"{tag}_d_h_g")
    dwu = _mm(ht, duu, out_dtype=F32, name=f"{tag}_d_wu")
    dwg = _mm(ht, dug, out_dtype=F32, name=f"{tag}_d_wg")
    dx, dxb, dgain = _rmsnorm_bwd(x, gain, dh, dy, f"{tag}_norm_bwd")
    return (dx, dxb), dict(gain=dgain, wu=dwu, wg=dwg, cw=dcw, wd=dwd)


def _att_forward(x, gain, p, tag):
    h, ht = _rmsnorm_fwd(x, gain, f"{tag}_norm")
    qkv = _mm(h, p["wqkv"], name=f"{tag}_qkv")
    q, kpad, vpad = _qkv_post_fwd(qkv, p["qg"], p["kg"], f"{tag}_qknorm")
    bias = _bias_expand(p["rel"], f"{tag}_bias")
    o = _attn_fwd(q, kpad, vpad, bias, f"{tag}_core")
    y = _mm(o, p["wo"], res=x, out_dtype=F32, name=f"{tag}_out")
    return y, (x, ht, qkv, q, kpad, vpad, bias, o)


def _att_backward(saved, dys, gain, p, tag):
    x, ht, qkv, q, kpad, vpad, bias, o = saved
    dy, dyb = dys
    do = _mm(dyb, p["wo"], tb=True, name=f"{tag}_d_o")
    dwo = _mm(o, dyb, ta=True, out_dtype=F32, name=f"{tag}_d_wo")
    dq, dkpad, dvpad, dbias = _attn_bwd(q, kpad, vpad, bias, do, f"{tag}_core_bwd")
    drel = _bias_reduce(dbias, f"{tag}_bias_bwd")
    dqkv, dqg, dkg = _qkv_post_bwd(qkv, p["qg"], p["kg"], dq, dkpad, dvpad, f"{tag}_qknorm_bwd")
    dh = _mm(dqkv, p["wqkv"], tb=True, out_dtype=F32, name=f"{tag}_d_h")
    dwqkv = _mm(ht, dqkv, out_dtype=F32, name=f"{tag}_d_wqkv")
    dx, dxb, dgain = _rmsnorm_bwd(x, gain, dh, dy, f"{tag}_norm_bwd")
    return (dx, dxb), dict(gain=dgain, wqkv=dwqkv, qg=dqg, kg=dkg, rel=drel, wo=dwo)


def _gdn_forward(x, gain, p, tag):
    d = x.shape[1]
    nk = d // HEAD
    hv = 2 * nk
    h, ht = _rmsnorm_fwd(x, gain, f"{tag}_norm")
    proj = _mm(h, p["wmain"], name=f"{tag}_proj")
    ab = _mm(h, p["wab"], out_dtype=F32, name=f"{tag}_proj_ab")
    q = _gdn_conv_fwd(proj, p["cw"], "q", 0, nk, f"{tag}_conv_q")
    k = _gdn_conv_fwd(proj, p["cw"], "k", nk, nk, f"{tag}_conv_k")
    v = _gdn_conv_fwd(proj, p["cw"], "v", 2 * nk, hv, f"{tag}_conv_v")
    gcb, btb = _gates_fwd(ab, p["a_log"], p["dt_bias"], hv, f"{tag}_gates")
    u, wd, attn, tinv = _delta_prep_fwd(q, k, v, gcb, btb, f"{tag}_delta_prep")
    o, states = _delta_scan_fwd(u, wd, attn, q, k, gcb, f"{tag}_delta_scan")
    og = _gdn_out_fwd(o, proj, 2, p["o_gain"], f"{tag}_onorm")
    y = _mm(og, p["wo"], res=x, out_dtype=F32, name=f"{tag}_out")
    return y, (x, ht, proj, ab, q, k, v, gcb, btb, u, wd, attn, tinv, o, states, og)


def _gdn_backward(saved, dys, gain, p, tag):
    x, ht, proj, ab, q, k, v, gcb, btb, u, wd, attn, tinv, o, states, og = saved
    dy, dyb = dys
    d = x.shape[1]
    nk = d // HEAD
    hv = 2 * nk
    dog = _mm(dyb, p["wo"], tb=True, name=f"{tag}_d_og")
    dwo = _mm(og, dyb, ta=True, out_dtype=F32, name=f"{tag}_d_wo")
    do, dgate, dogain = _gdn_out_bwd(o, proj, 2, p["o_gain"], dog, f"{tag}_onorm_bwd")
    du, dw, dattn, dq_s, dk_s, dg_s = _delta_scan_bwd(u, wd, attn, q, k, gcb, states, do, f"{tag}_delta_scan_bwd")
    dq, dk, dv, dgcb, dbtb = _delta_prep_bwd(q, k, v, gcb, btb, tinv, du, dw, dattn, dq_s, dk_s, dg_s, f"{tag}_delta_prep_bwd")
    dab, dalog, ddt = _gates_bwd(ab, p["a_log"], p["dt_bias"], dgcb, dbtb, hv, f"{tag}_gates_bwd")
    dpq, dcq = _gdn_conv_bwd(proj, p["cw"], dq, "q", 0, nk, f"{tag}_conv_q_bwd")
    dpk, dck = _gdn_conv_bwd(proj, p["cw"], dk, "k", nk, nk, f"{tag}_conv_k_bwd")
    dpv, dcv = _gdn_conv_bwd(proj, p["cw"], dv, "v", 2 * nk, hv, f"{tag}_conv_v_bwd")
    dproj = jnp.concatenate([dpq, dpk, dpv, dgate], axis=1)
    dcw = jnp.concatenate([dcq, dck, dcv], axis=1)
    dh = _mm(dproj, p["wmain"], tb=True, out_dtype=F32, name=f"{tag}_d_h_main")
    dh = _mm(dab, p["wab"], tb=True, res=dh, out_dtype=F32, name=f"{tag}_d_h_ab")
    dwmain = _mm(ht, dproj, out_dtype=F32, name=f"{tag}_d_wmain")
    dwab = _mm(ht, dab, out_dtype=F32, name=f"{tag}_d_wab")
    dx, dxb, dgain = _rmsnorm_bwd(x, gain, dh, dy, f"{tag}_norm_bwd")
    return (dx, dxb), dict(gain=dgain, wmain=dwmain, wab=dwab, cw=dcw, a_log=dalog, dt_bias=ddt, o_gain=dogain, wo=dwo)


def _resolve(entry, after):
    return entry(after) if callable(entry) else entry


def _local_step(x, target, w, sink=None):
    depth = len(w["ffn"])
    tape = []
    for i in range(depth):
        kind, j = i % 3, i // 3
        gain = w["mix_norm"][i:i + 1]
        if kind == 0:
            x, saved = _att_forward(x, gain, _resolve(w["att"][j], x), f"l{i}_att")
        elif kind == 1:
            x_in = x
            pw = _resolve(w["pool"][j], x)
            x = _pool_fwd(x_in, gain, pw["w"], pw["scale"], f"l{i}_pool")
            saved = x_in
        else:
            x, saved = _gdn_forward(x, gain, _resolve(w["gdn"][j], x), f"l{i}_gdn")
        f = _resolve(w["ffn"][i], x)
        x, fsaved = _ffn_forward(x, w["ffn_norm"][i:i + 1], f["wu"], f["wg"], f["cw"], f["wd"], f"l{i}_ffn")
        tape.append((saved, fsaved))
    dy, dyb, loss_row = _loss_head(x, target, "loss_head")
    dy = (dy, dyb)
    grads = dict(mix=[None] * depth, ffn=[None] * depth)
    zero = 0.0
    for i in reversed(range(depth)):
        kind, j = i % 3, i // 3
        saved, fsaved = tape[i]
        f = _resolve(w["ffn"][i], dy[0])
        dy, grads["ffn"][i] = _ffn_backward(fsaved, dy, w["ffn_norm"][i:i + 1] + zero, f["wu"], f["wg"], f["cw"], f["wd"], f"l{i}_ffn")
        if sink is not None:
            zero = zero + sink("ffn", i, grads["ffn"][i])
        gain = w["mix_norm"][i:i + 1] + zero
        if kind == 0:
            dy, grads["mix"][i] = _att_backward(saved, dy, gain, _resolve(w["att"][j], dy[0]), f"l{i}_att")
        elif kind == 1:
            pw = _resolve(w["pool"][j], dy[0])
            dx, dgain, dw4, dscale = _pool_bwd(saved, gain, pw["w"], pw["scale"], dy[0], f"l{i}_pool_bwd")
            dy = (dx, dx.astype(BF))
            grads["mix"][i] = dict(gain=dgain, w=dw4, scale=dscale)
        else:
            dy, grads["mix"][i] = _gdn_backward(saved, dy, gain, _resolve(w["gdn"][j], dy[0]), f"l{i}_gdn")
        if sink is not None:
            zero = zero + sink(("att", "pool", "gdn")[kind], i, grads["mix"][i])
    return loss_row, dy[0], grads


MESH = pl.DeviceIdType.MESH
ANY = pl.BlockSpec(memory_space=pl.ANY)


def _position():
    return tuple(lax.axis_index(a) for a in AXES)


def _flip(pos, rel):
    return tuple(1 - p if (rel >> (2 - i)) & 1 else p for i, p in enumerate(pos))


def _index(pos):
    return 4 * pos[0] + 2 * pos[1] + pos[2]


def _all_gather(arr, name):
    def body(x_ref, o_ref, send, recv, local):
        me = _position()
        mine = pltpu.make_async_copy(x_ref, o_ref.at[_index(me)], local)
        mine.start()
        copies = []
        for rel in range(1, N_DEV):
            cp = pltpu.make_async_remote_copy(
                src_ref=x_ref, dst_ref=o_ref.at[_index(me)], send_sem=send.at[rel - 1], recv_sem=recv.at[rel - 1],
                device_id=_flip(me, rel), device_id_type=MESH)
            cp.start()
            copies.append(cp)
        for cp in copies:
            cp.wait()
        mine.wait()

    return pl.pallas_call(
        body, name=name, in_specs=[ANY], out_specs=ANY, out_shape=SDS((N_DEV,) + arr.shape, arr.dtype),
        scratch_shapes=[pltpu.SemaphoreType.DMA((N_DEV - 1,)), pltpu.SemaphoreType.DMA((N_DEV - 1,)),
                        pltpu.SemaphoreType.DMA(())])(arr)


def _exchange(arr, name):
    def body(x_ref, o_ref, send, recv, local):
        me = _position()
        mine = pltpu.make_async_copy(x_ref.at[_index(me)], o_ref.at[_index(me)], local)
        mine.start()
        copies = []
        for rel in range(1, N_DEV):
            peer = _flip(me, rel)
            cp = pltpu.make_async_remote_copy(
                src_ref=x_ref.at[_index(peer)], dst_ref=o_ref.at[_index(me)], send_sem=send.at[rel - 1],
                recv_sem=recv.at[rel - 1], device_id=peer, device_id_type=MESH)
            cp.start()
            copies.append(cp)
        for cp in copies:
            cp.wait()
        mine.wait()

    return pl.pallas_call(
        body, name=name, in_specs=[ANY], out_specs=ANY, out_shape=SDS(arr.shape, arr.dtype),
        scratch_shapes=[pltpu.SemaphoreType.DMA((N_DEV - 1,)), pltpu.SemaphoreType.DMA((N_DEV - 1,)),
                        pltpu.SemaphoreType.DMA(())])(arr)


HBM = pl.BlockSpec(memory_space=pltpu.HBM)
SEM = pl.BlockSpec(memory_space=pltpu.SEMAPHORE)
EFFECT = pltpu.SideEffectType.DATAFLOW_SIDE_EFFECTING


def _split_copies(x_ref, land_ref, send, recv, scatter):
    me = _position()
    copies = []
    for rel in range(1, N_DEV):
        peer = _flip(me, rel)
        copies.append(pltpu.make_async_remote_copy(
            src_ref=x_ref.at[_index(peer)] if scatter else x_ref, dst_ref=land_ref.at[_index(me)],
            send_sem=send.at[rel - 1], recv_sem=recv.at[rel - 1], device_id=peer, device_id_type=MESH))
    return copies


def _copies_start(arr, scatter, name):
    shape = arr.shape if scatter else (N_DEV,) + arr.shape

    def body(x_ref, land_ref, send, recv, x_thru, land_thru, token):
        for cp in _split_copies(x_ref, land_ref, send, recv, scatter):
            cp.start()
        token[...] = jnp.zeros_like(token)

    sems = pltpu.SemaphoreType.DMA((N_DEV - 1,))
    send, recv, x_thru, land_thru, token = pl.pallas_call(
        body, name=name,
        out_shape=(sems, sems, pltpu.HBM(arr.shape, arr.dtype), pltpu.HBM(shape, arr.dtype), SDS((8, LANE), F32)),
        in_specs=(HBM, HBM), out_specs=(SEM, SEM, HBM, HBM, pl.BlockSpec(memory_space=pltpu.VMEM)),
        input_output_aliases={0: 2, 1: 3}, compiler_params=pltpu.CompilerParams(has_side_effects=EFFECT),
    )(pltpu.with_memory_space_constraint(arr, pltpu.HBM), pltpu.with_memory_space_constraint(lax.empty(shape, arr.dtype), pltpu.HBM))
    return (send, recv, x_thru, land_thru, scatter, name), token[0, 0]


def _copies_wait(handle, after):
    send, recv, x_thru, land_thru, scatter, name = handle

    def body(x_ref, land_ref, send_ref, recv_ref, after_ref, x_dead, got_ref):
        for cp in _split_copies(x_ref, land_ref, send_ref, recv_ref, scatter):
            cp.wait_send()
            cp.wait_recv()

    return pl.pallas_call(
        body, name=name + "_wait",
        out_shape=(pltpu.HBM(x_thru.shape, x_thru.dtype), pltpu.HBM(land_thru.shape, land_thru.dtype)),
        in_specs=(HBM, HBM, SEM, SEM, ANY), out_specs=(HBM, HBM), input_output_aliases={0: 0, 1: 1},
        compiler_params=pltpu.CompilerParams(has_side_effects=EFFECT),
    )(x_thru, land_thru, send, recv, after)[1]


def _with_own(got, own):
    return lax.dynamic_update_index_in_dim(got, own.astype(got.dtype), _index(_position()), 0)


def _adamw(parts, w, m, v, name):
    if not isinstance(parts, (list, tuple)):
        parts = [parts]
    layers = len(parts)
    r, c = parts[0].shape[1:]
    assert w.shape == (layers * r, c), (w.shape, parts[0].shape, layers)
    tr = _tile(r, 128, 16)
    per = r // tr
    c1 = 1.0 / (1.0 - ADAM_B1 ** ADAM_STEP)
    c2 = 1.0 / (1.0 - ADAM_B2 ** ADAM_STEP)

    def body(*refs):
        p_refs = refs[:layers]
        w_ref, m_ref, v_ref, g_ref, d_ref, nm_ref, nv_ref = refs[layers:]

        def update(p_ref):
            g = p_ref[0].astype(F32)
            for s in range(1, N_DEV):
                g = g + p_ref[s].astype(F32)
            nm = ADAM_B1 * m_ref[...] + (1.0 - ADAM_B1) * g
            nv = ADAM_B2 * v_ref[...] + (1.0 - ADAM_B2) * (g * g)
            g_ref[...] = g
            nm_ref[...] = nm
            nv_ref[...] = nv
            d_ref[...] = -ADAM_LR * ((nm * c1) / (jnp.sqrt(nv * c2) + ADAM_EPS) + ADAM_WD * w_ref[...])

        if layers == 1:
            update(p_refs[0])
        else:
            for j in range(layers):
                pl.when(pl.program_id(0) == j)(functools.partial(update, p_refs[j]))

    p_specs = [pl.BlockSpec((N_DEV, tr, c), functools.partial(lambda l, i, j: (0, jnp.where(l == j, i, 0), 0), j=j))
               for j in range(layers)]
    row = pl.BlockSpec((tr, c), lambda l, i: (l * per + i, 0))
    return pl.pallas_call(
        body, name=name, grid=(layers, per), in_specs=p_specs + [row, row, row],
        out_specs=[row] * 4, out_shape=[SDS(w.shape, F32)] * 4, compiler_params=_params("arbitrary", "arbitrary"))(*parts, w, m, v)


PACK = 8 * LANE


def _pack(arrs):
    flat = []
    for a in arrs:
        a = a.reshape(-1).astype(F32)
        flat.append(jnp.pad(a, (0, (-a.shape[0]) % PACK)))
    return jnp.concatenate(flat).reshape(-1, LANE)


def _unpack(packed, shapes, lead=()):
    flat = packed.reshape(lead + (-1,))
    out, off = [], 0
    for shp in shapes:
        n = int(np.prod(shp))
        out.append(flat[..., off:off + n].reshape(lead + tuple(shp)))
        off += n + (-n) % PACK
    return out


def _pad_to(a, axis, size):
    pad = [(0, 0)] * a.ndim
    pad[axis] = (0, size - a.shape[axis])
    return jnp.pad(a, pad)


def _cols_from_shards(g):
    return jnp.transpose(g, (1, 0, 2)).reshape(g.shape[1], -1)


def _cols_to_shards(a):
    c = a.shape[-1] // N_DEV
    a = a.reshape(a.shape[:-1] + (N_DEV, c))
    return jnp.moveaxis(a, -2, 0)


def _rows_to_shards(a):
    r = a.shape[-2] // N_DEV
    a = a.reshape(a.shape[:-2] + (N_DEV, r, a.shape[-1]))
    return jnp.moveaxis(a, -3, 0)


WEIGHTS = ("mix_norm", "ffn_norm", "att_w_qkv", "att_q_gain", "att_k_gain", "att_rel_bias", "att_w_o", "pool_w",
           "pool_scale", "gdn_w_in", "gdn_conv", "gdn_a_log", "gdn_dt_bias", "gdn_o_gain", "gdn_w_o", "ffn_w_up",
           "ffn_conv", "ffn_w_down")
REPLICATED = ("mix_norm", "ffn_norm", "att_q_gain", "att_k_gain", "pool_scale", "gdn_a_log", "gdn_dt_bias", "gdn_o_gain")
SMALL_SHARDED = ("att_rel_bias", "gdn_conv", "ffn_conv")
BIG = ("att_w_qkv", "att_w_o", "pool_w", "gdn_w_in", "gdn_w_o", "ffn_w_up", "ffn_w_down")
KEEP_F32 = ("pool_w",)


def _memo(build):
    cache = []

    def entry(after):
        if not cache:
            cache.append(build(after))
        return cache[0]

    return entry


def _assemble_weights(w, get, small, f):
    d = w["mix_norm"].shape[1]
    nk = d // HEAD
    hv = 2 * nk
    fp = -(-f // FF_PAD) * FF_PAD
    out = dict(mix_norm=w["mix_norm"], ffn_norm=w["ffn_norm"], att=[], pool=[], gdn=[], ffn=[])

    def att(j, after):
        return dict(wqkv=_cols_from_shards(get("att_w_qkv", j, after)), wo=get("att_w_o", j, after).reshape(d, d),
                    qg=w["att_q_gain"][j:j + 1], kg=w["att_k_gain"][j:j + 1], rel=small["att_rel_bias"][j])

    def pool(j, after):
        g = get("pool_w", j, after)
        return dict(w=jnp.transpose(g, (1, 0, 2, 3)).reshape(g.shape[1], g.shape[3], g.shape[3]),
                    scale=w["pool_scale"][j:j + 1])

    def gdn(j, after):
        win = _cols_from_shards(get("gdn_w_in", j, after))
        nm = 6 * d
        wab = jnp.concatenate([_pad_to(win[:, nm:nm + hv], 1, LANE), _pad_to(win[:, nm + hv:], 1, LANE)], axis=1)
        return dict(wmain=win[:, :nm], wab=wab, cw=_pad_to(small["gdn_conv"][j], 0, 8),
                    a_log=_pad_to(w["gdn_a_log"][j:j + 1], 1, LANE), dt_bias=_pad_to(w["gdn_dt_bias"][j:j + 1], 1, LANE),
                    o_gain=w["gdn_o_gain"][j:j + 1], wo=get("gdn_w_o", j, after).reshape(2 * d, d))

    def ffn(i, after):
        wup = _cols_from_shards(get("ffn_w_up", i, after))
        return dict(wu=_pad_to(wup[:, :f], 1, fp), wg=_pad_to(wup[:, f:], 1, fp),
                    cw=_pad_to(_pad_to(small["ffn_conv"][i], 0, 8), 1, fp),
                    wd=_pad_to(get("ffn_w_down", i, after).reshape(f, d), 0, fp))

    for key, build, count in (("att", att, w["att_w_qkv"].shape[0]), ("pool", pool, w["pool_w"].shape[0]),
                              ("gdn", gdn, w["gdn_w_in"].shape[0]), ("ffn", ffn, w["ffn_w_up"].shape[0])):
        out[key] = [_memo(functools.partial(build, j)) for j in range(count)]
    return out


def _full_gradients(grads, w, f):
    d = w["mix_norm"].shape[1]
    nk = d // HEAD
    hv = 2 * nk
    depth = len(grads["ffn"])
    att = [grads["mix"][i] for i in range(depth) if i % 3 == 0]
    pool = [grads["mix"][i] for i in range(depth) if i % 3 == 1]
    gdn = [grads["mix"][i] for i in range(depth) if i % 3 == 2]
    ffn = grads["ffn"]
    win = [jnp.concatenate([g["wmain"], g["wab"][:, :hv], g["wab"][:, LANE:LANE + hv]], axis=1) for g in gdn]
    return dict(
        mix_norm=jnp.concatenate([g["gain"] for g in grads["mix"]], axis=0),
        ffn_norm=jnp.concatenate([g["gain"] for g in ffn], axis=0),
        att_w_qkv=jnp.stack([g["wqkv"] for g in att]),
        att_q_gain=jnp.concatenate([g["qg"] for g in att], axis=0),
        att_k_gain=jnp.concatenate([g["kg"] for g in att], axis=0),
        att_rel_bias=jnp.stack([g["rel"] for g in att]),
        att_w_o=jnp.stack([g["wo"] for g in att]),
        pool_w=jnp.stack([g["w"] for g in pool]),
        pool_scale=jnp.concatenate([g["scale"] for g in pool], axis=0),
        gdn_w_in=jnp.stack(win),
        gdn_conv=jnp.stack([g["cw"][:GDN_CONV] for g in gdn]),
        gdn_a_log=jnp.concatenate([g["a_log"][:, :hv] for g in gdn], axis=0),
        gdn_dt_bias=jnp.concatenate([g["dt_bias"][:, :hv] for g in gdn], axis=0),
        gdn_o_gain=jnp.concatenate([g["o_gain"] for g in gdn], axis=0),
        gdn_w_o=jnp.stack([g["wo"] for g in gdn]),
        ffn_w_up=jnp.stack([jnp.concatenate([g["wu"][:, :f], g["wg"][:, :f]], axis=1) for g in ffn]),
        ffn_conv=jnp.stack([g["cw"][:FFN_CONV, :f] for g in ffn]),
        ffn_w_down=jnp.stack([g["wd"][:f] for g in ffn]))


ROW_SHARDED = ("att_w_o", "gdn_w_o", "ffn_w_down")


def _to_shards(name, full):
    if name == "pool_w":
        r = full.shape[2] // N_DEV
        a = full.reshape(full.shape[:2] + (N_DEV, r, full.shape[3]))
        return jnp.moveaxis(a, 2, 0)
    return _rows_to_shards(full) if name in ROW_SHARDED else _cols_to_shards(full)


def kernel(x, mix_norm, ffn_norm, att_w_qkv, att_q_gain, att_k_gain, att_rel_bias, att_w_o, pool_w, pool_scale, gdn_w_in, gdn_conv, gdn_a_log, gdn_dt_bias, gdn_o_gain, gdn_w_o, ffn_w_up, ffn_conv, ffn_w_down, loss_target, m_mix_norm, m_ffn_norm, m_att_w_qkv, m_att_q_gain, m_att_k_gain, m_att_rel_bias, m_att_w_o, m_pool_w, m_pool_scale, m_gdn_w_in, m_gdn_conv, m_gdn_a_log, m_gdn_dt_bias, m_gdn_o_gain, m_gdn_w_o, m_ffn_w_up, m_ffn_conv, m_ffn_w_down, v_mix_norm, v_ffn_norm, v_att_w_qkv, v_att_q_gain, v_att_k_gain, v_att_rel_bias, v_att_w_o, v_pool_w, v_pool_scale, v_gdn_w_in, v_gdn_conv, v_gdn_a_log, v_gdn_dt_bias, v_gdn_o_gain, v_gdn_w_o, v_ffn_w_up, v_ffn_conv, v_ffn_w_down):
    w = dict(zip(WEIGHTS, (mix_norm, ffn_norm, att_w_qkv, att_q_gain, att_k_gain, att_rel_bias, att_w_o, pool_w, pool_scale, gdn_w_in, gdn_conv, gdn_a_log, gdn_dt_bias, gdn_o_gain, gdn_w_o, ffn_w_up, ffn_conv, ffn_w_down)))
    m = dict(zip(WEIGHTS, (m_mix_norm, m_ffn_norm, m_att_w_qkv, m_att_q_gain, m_att_k_gain, m_att_rel_bias, m_att_w_o, m_pool_w, m_pool_scale, m_gdn_w_in, m_gdn_conv, m_gdn_a_log, m_gdn_dt_bias, m_gdn_o_gain, m_gdn_w_o, m_ffn_w_up, m_ffn_conv, m_ffn_w_down)))
    v = dict(zip(WEIGHTS, (v_mix_norm, v_ffn_norm, v_att_w_qkv, v_att_q_gain, v_att_k_gain, v_att_rel_bias, v_att_w_o, v_pool_w, v_pool_scale, v_gdn_w_in, v_gdn_conv, v_gdn_a_log, v_gdn_dt_bias, v_gdn_o_gain, v_gdn_w_o, v_ffn_w_up, v_ffn_conv, v_ffn_w_down)))

    me = _index(_position())
    d = mix_norm.shape[1]
    hv = 2 * (d // HEAD)
    f = ffn_w_down.shape[1] * N_DEV
    depth = ffn_w_up.shape[0]

    small_shapes = [w[n].shape for n in SMALL_SHARDED]
    small_g = _all_gather(_pack([w[n] for n in SMALL_SHARDED]), "gather_small")
    small = {}
    for n, a in zip(SMALL_SHARDED, _unpack(small_g, small_shapes, lead=(N_DEV,))):
        small[n] = jnp.moveaxis(a, 0, -2).reshape(a.shape[1:-1] + (N_DEV * a.shape[-1],))
    order = []
    for i in range(depth):
        order += [[("att_w_qkv", i // 3), ("att_w_o", i // 3)], [("pool_w", i // 3)], [("gdn_w_in", i // 3), ("gdn_w_o", i // 3)]][i % 3]
        order += [("ffn_w_up", i), ("ffn_w_down", i)]
    after_small = small_g[0, 0, 0] * 0.0
    local = {(n, j): (w[n][j] + after_small if n in KEEP_F32 else (w[n][j] + after_small).astype(BF)) for n, j in order}
    arriving, zero = {}, 0.0
    for n, j in order:
        arriving[(n, j)], tok = _copies_start(local[(n, j)], False, f"gather_{n}_{j}")
        zero = zero + tok

    def get(n, j, after):
        return _with_own(_copies_wait(arriving[(n, j)], after), local[(n, j)])

    ordered = dict(w, mix_norm=mix_norm + zero, ffn_norm=ffn_norm + zero)
    full = _assemble_weights(ordered, get, small, f)

    leaving = {}

    def sink(kind, i, g):
        j = i // 3
        if kind == "ffn":
            pieces = [("ffn_w_up", i, _cols_to_shards(jnp.concatenate([g["wu"][:, :f], g["wg"][:, :f]], axis=1))),
                      ("ffn_w_down", i, _rows_to_shards(g["wd"][:f]))]
        elif kind == "att":
            pieces = [("att_w_qkv", j, _cols_to_shards(g["wqkv"])), ("att_w_o", j, _rows_to_shards(g["wo"]))]
        elif kind == "pool":
            pieces = [("pool_w", j, _to_shards("pool_w", g["w"][None])[:, 0])]
        else:
            win = jnp.concatenate([g["wmain"], g["wab"][:, :hv], g["wab"][:, LANE:LANE + hv]], axis=1)
            pieces = [("gdn_w_in", j, _cols_to_shards(win)), ("gdn_w_o", j, _rows_to_shards(g["wo"]))]
        tok = 0.0
        for n, l, shards in pieces:
            shards = shards if n in KEEP_F32 else shards.astype(BF)
            handle, t = _copies_start(shards, True, f"exchange_{n}_{l}")
            leaving[(n, l)] = (handle, lax.dynamic_index_in_dim(shards, me, 0, keepdims=False))
            tok = tok + t
        return tok

    loss_row, dx, grads = _local_step(x[0], loss_target[0], full, sink)
    gfull = _full_gradients(grads, w, f)

    out = {}
    for n in BIG:
        c = w[n].shape[-1]
        parts = []
        for l in range(w[n].shape[0]):
            handle, own = leaving[(n, l)]
            parts.append(_with_own(_copies_wait(handle, dx), own).reshape(N_DEV, -1, c))
        res = _adamw(parts, w[n].reshape(-1, c), m[n].reshape(-1, c), v[n].reshape(-1, c), f"adamw_{n}")
        out[n] = [a.reshape(w[n].shape) for a in res]
    sparts = _exchange(jnp.stack([_pack([_to_shards(n, gfull[n])[k] for n in SMALL_SHARDED]) for k in range(N_DEV)]),
                       "exchange_small")
    res = _adamw(sparts, *[_pack([t[n] for n in SMALL_SHARDED]) for t in (w, m, v)], "adamw_small")
    for n, *vals in zip(SMALL_SHARDED, *[_unpack(a, small_shapes) for a in res]):
        out[n] = vals
    rep_shapes = [w[n].shape for n in REPLICATED]
    rparts = _all_gather(_pack([gfull[n] for n in REPLICATED] + [loss_row[:, 0:1]]), "gather_replicated_grads")
    pad = jnp.zeros((1, 1), F32)
    res = _adamw(rparts, *[_pack([t[n] for n in REPLICATED] + [pad]) for t in (w, m, v)], "adamw_replicated")
    for n, *vals in zip(REPLICATED, *[_unpack(a, rep_shapes) for a in res]):
        out[n] = vals
    loss = jnp.sum(_unpack(rparts, rep_shapes + [(1, 1)], lead=(N_DEV,))[-1])
    return (loss, dx[None], *[out[n][0] for n in WEIGHTS], *[out[n][1] for n in WEIGHTS],
            *[out[n][2] for n in WEIGHTS], *[out[n][3] for n in WEIGHTS])
```

```python
import functools

import numpy as np
import jax
import jax.numpy as jnp
from jax import lax
from jax.experimental import pallas as pl
from jax.experimental.pallas import tpu as pltpu

F32 = jnp.float32
BF = jnp.bfloat16
SDS = jax.ShapeDtypeStruct

EPS = 1e-6
MASK_VALUE = -1e30
CHUNK = 64
LEFT_CHUNKS = 8
BAND_LEFT = LEFT_CHUNKS * CHUNK
BAND = BAND_LEFT + CHUNK
MAX_REL = 256
NUM_REL = (CHUNK - 1) + MAX_REL + 1
HEAD = 128
LANE = 128
HALO = 16
POOL_WINDOWS = (2, 4, 8, 16)
GDN_CONV = 4
FFN_CONV = 3
FF_PAD = 512
N_DEV = 8
AXES = ("x", "y", "c")
VMEM_LIMIT = 56 * 1024 * 1024

ADAM_LR = 0.001
ADAM_B1 = 0.9
ADAM_B2 = 0.999
ADAM_EPS = 1e-08
ADAM_WD = 0.01
ADAM_STEP = 10

HI = lax.Precision.HIGHEST
NT = (((1,), (1,)), ((), ()))
TN = (((0,), (0,)), ((), ()))


def _tile(n, target, mult=LANE):
    if n <= target:
        return n
    t = (target // mult) * mult
    while t >= mult:
        if n % t == 0:
            return t
        t -= mult
    return n


def _params(*sem):
    return pltpu.CompilerParams(dimension_semantics=sem, vmem_limit_bytes=VMEM_LIMIT)


def _silu(x):
    return x / (1.0 + jnp.exp(-x))


@functools.partial(jax.custom_vjp, nondiff_argnums=(1,))
def _shift(x, k):
    return pltpu.roll(x, k % x.shape[0], axis=0)


def _shift_fwd(x, k):
    return _shift(x, k), None


def _shift_bwd(k, _, g):
    return (pltpu.roll(g, (-k) % g.shape[0], axis=0),)


_shift.defvjp(_shift_fwd, _shift_bwd)


def _mm(a, b, *, name, ta=False, tb=False, out_dtype=BF, res=None, tm=1024, tn=1024, tk=2048, n_outer=False):
    m, k = (a.shape[1], a.shape[0]) if ta else a.shape
    n, kb = (b.shape[0], b.shape[1]) if tb else (b.shape[1], b.shape[0])
    assert k == kb, (a.shape, b.shape, ta, tb)
    if ta or a.dtype != BF:
        tk = min(tk, 1024)
    tm, tn, tk = _tile(m, tm), _tile(n, tn), _tile(k, tk)
    nk = k // tk
    dims = (((0 if ta else 1,), (1 if tb else 0,)), ((), ()))

    def body(*refs):
        if res is None:
            a_ref, b_ref, o_ref, acc = refs
        else:
            a_ref, b_ref, r_ref, o_ref, acc = refs
        prod = lax.dot_general(a_ref[...].astype(BF), b_ref[...].astype(BF), dims, preferred_element_type=F32)
        if nk == 1:
            if res is not None:
                prod = prod + r_ref[...].astype(F32)
            o_ref[...] = prod.astype(out_dtype)
            return
        kk = pl.program_id(2)

        @pl.when(kk == 0)
        def _():
            acc[...] = prod

        @pl.when(kk > 0)
        def _():
            acc[...] += prod

        @pl.when(kk == nk - 1)
        def _():
            r = acc[...]
            if res is not None:
                r = r + r_ref[...].astype(F32)
            o_ref[...] = r.astype(out_dtype)

    def order(fn):
        return (lambda j, i, q: fn(i, j, q)) if n_outer else fn

    a_spec = pl.BlockSpec((tk, tm), order(lambda i, j, q: (q, i))) if ta else pl.BlockSpec((tm, tk), order(lambda i, j, q: (i, q)))
    b_spec = pl.BlockSpec((tn, tk), order(lambda i, j, q: (j, q))) if tb else pl.BlockSpec((tk, tn), order(lambda i, j, q: (q, j)))
    o_spec = pl.BlockSpec((tm, tn), order(lambda i, j, q: (i, j)))
    ins, specs = [a, b], [a_spec, b_spec]
    if res is not None:
        ins.append(res)
        specs.append(o_spec)
    return pl.pallas_call(
        body, name=name, grid=(n // tn, m // tm, nk) if n_outer else (m // tm, n // tn, nk), in_specs=specs, out_specs=o_spec,
        out_shape=SDS((m, n), out_dtype), scratch_shapes=[pltpu.VMEM((tm, tn), F32)],
        compiler_params=_params("parallel", "parallel", "arbitrary"))(*ins)


def _rms(x, gain):
    return x * lax.rsqrt(jnp.mean(x * x, axis=-1, keepdims=True) + EPS) * gain


def _rmsnorm_fwd(x, gain, name):
    s, d = x.shape
    ts = _tile(s, 256, LANE)

    def body(x_ref, g_ref, o_ref, t_ref):
        h = _rms(x_ref[...], g_ref[...])
        o_ref[...] = h.astype(BF)
        t_ref[...] = h.T.astype(BF)

    return pl.pallas_call(
        body, name=name, grid=(s // ts,),
        in_specs=[pl.BlockSpec((ts, d), lambda i: (i, 0)), pl.BlockSpec((1, d), lambda i: (0, 0))],
        out_specs=[pl.BlockSpec((ts, d), lambda i: (i, 0)), pl.BlockSpec((d, ts), lambda i: (0, i))],
        out_shape=[SDS((s, d), BF), SDS((d, s), BF)], compiler_params=_params("parallel"))(x, gain)


def _rmsnorm_bwd(x, gain, dh, dres, name):
    s, d = x.shape
    ts = _tile(s, 256, 16)

    def body(x_ref, g_ref, dh_ref, dr_ref, dx_ref, dxb_ref, dg_ref):
        i = pl.program_id(0)
        _, vjp = jax.vjp(_rms, x_ref[...], g_ref[...])
        dx, dg = vjp(dh_ref[...].astype(F32))
        dx = dr_ref[...] + dx
        dx_ref[...] = dx
        dxb_ref[...] = dx.astype(BF)

        @pl.when(i == 0)
        def _():
            dg_ref[...] = dg

        @pl.when(i > 0)
        def _():
            dg_ref[...] += dg

    row = pl.BlockSpec((ts, d), lambda i: (i, 0))
    vec = pl.BlockSpec((1, d), lambda i: (0, 0))
    return pl.pallas_call(
        body, name=name, grid=(s // ts,), in_specs=[row, vec, row, row], out_specs=[row, row, vec],
        out_shape=[SDS((s, d), F32), SDS((s, d), BF), SDS((1, d), F32)], compiler_params=_params("arbitrary"))(x, gain, dh, dres)


def _ffn_act_tile(u_ext, g_ext, cw):
    acc = u_ext * cw[2:3] + _shift(u_ext, 1) * cw[1:2] + _shift(u_ext, 2) * cw[0:1]
    return _silu(acc) * g_ext


def _halo_index(rows_per_block):
    per = rows_per_block // HALO
    return lambda rb: jnp.maximum(rb * per - 1, 0)


FFN_SUB = 256


def _ext_rows(cur, prev, nxt, r0, rows, tile, keep_prev=1.0, keep_next=1.0):
    n = cur.shape[0]
    parts = []
    if r0 > 0:
        parts.append(cur[pl.ds(r0 - HALO, HALO + rows), tile].astype(F32))
    else:
        head = jnp.zeros((HALO, LANE), F32) if prev is None else prev[:, tile].astype(F32) * keep_prev
        parts += [head, cur[pl.ds(0, rows), tile].astype(F32)]
    if nxt is not False:
        if r0 + rows < n:
            parts.append(cur[pl.ds(r0 + rows, HALO), tile].astype(F32))
        else:
            parts.append(jnp.zeros((HALO, LANE), F32) if nxt is None else nxt[:, tile].astype(F32) * keep_next)
    return jnp.concatenate(parts, axis=0)


def _ffn_act_fwd(u, g, cw, name):
    s, f = u.shape
    r, tc = _tile(s, 512, HALO), _tile(f, 512)
    sub = _tile(r, FFN_SUB, HALO)
    hidx = _halo_index(r)

    def body(uc, uh, gc, cw_ref, o_ref, t_ref):
        keep = jnp.where(pl.program_id(1) == 0, 0.0, 1.0)
        for lt in range(tc // LANE):
            tile = pl.ds(lt * LANE, LANE)
            c0, c1, c2 = cw_ref[0:1, tile], cw_ref[1:2, tile], cw_ref[2:3, tile]
            for r0 in range(0, r, sub):
                u_ext = _ext_rows(uc, uh, False, r0, sub, tile, keep)
                acc = u_ext * c2 + pltpu.roll(u_ext, 1, axis=0) * c1 + pltpu.roll(u_ext, 2, axis=0) * c0
                a = _silu(acc)[HALO:] * gc[pl.ds(r0, sub), tile].astype(F32)
                o_ref[pl.ds(r0, sub), tile] = a.astype(BF)
                t_ref[tile, pl.ds(r0, sub)] = a.T.astype(BF)

    cur = pl.BlockSpec((r, tc), lambda j, rb: (rb, j))
    return pl.pallas_call(
        body, name=name, grid=(f // tc, s // r),
        in_specs=[cur, pl.BlockSpec((HALO, tc), lambda j, rb: (hidx(rb), j)), cur,
                  pl.BlockSpec((8, tc), lambda j, rb: (0, j))],
        out_specs=[cur, pl.BlockSpec((tc, r), lambda j, rb: (j, rb))],
        out_shape=[SDS((s, f), BF), SDS((f, s), BF)], compiler_params=_params("parallel", "parallel"))(u, u, g, cw)


def _ffn_act_bwd(u, g, cw, da, name):
    s, f = u.shape
    r, tc = _tile(s, 512, HALO), _tile(f, 512)
    sub = _tile(r, FFN_SUB, HALO)
    nb = s // r
    per = r // HALO
    hidx = _halo_index(r)
    nidx = lambda rb: jnp.minimum((rb + 1) * per, s // HALO - 1)
    n = sub + 2 * HALO

    def body(uc, uh, un, gc, gn, dac, dan, cw_ref, du_ref, dg_ref, dcw_ref):
        rb = pl.program_id(1)
        first = jnp.where(rb == 0, 0.0, 1.0)
        last = jnp.where(rb == nb - 1, 0.0, 1.0)
        rows = lax.broadcasted_iota(jnp.int32, (n, LANE), 0)
        own = jnp.where((rows >= HALO) & (rows < HALO + sub), 1.0, 0.0)
        wrow = lax.broadcasted_iota(jnp.int32, (8, LANE), 0)

        @pl.when(rb == 0)
        def _():
            dcw_ref[...] = jnp.zeros_like(dcw_ref)

        for lt in range(tc // LANE):
            tile = pl.ds(lt * LANE, LANE)
            c0, c1, c2 = cw_ref[0:1, tile], cw_ref[1:2, tile], cw_ref[2:3, tile]
            dcw = jnp.zeros((8, LANE), F32)
            for r0 in range(0, r, sub):
                u_ext = _ext_rows(uc, uh, un, r0, sub, tile, first, last)
                g_ext = _ext_rows(gc, None, gn, r0, sub, tile, 1.0, last)
                da_ext = _ext_rows(dac, None, dan, r0, sub, tile, 1.0, last)
                sh1, sh2 = pltpu.roll(u_ext, 1, axis=0), pltpu.roll(u_ext, 2, axis=0)
                acc = u_ext * c2 + sh1 * c1 + sh2 * c0
                sg = 1.0 / (1.0 + jnp.exp(-acc))
                dg_ref[pl.ds(r0, sub), tile] = (da_ext * (acc * sg))[HALO:HALO + sub].astype(BF)
                dacc = da_ext * g_ext * (sg * (1.0 + acc * (1.0 - sg)))
                du = dacc * c2 + pltpu.roll(dacc, n - 1, axis=0) * c1 + pltpu.roll(dacc, n - 2, axis=0) * c0
                du_ref[pl.ds(r0, sub), tile] = du[HALO:HALO + sub].astype(BF)
                dm = dacc * own
                for j, tap in enumerate((sh2, sh1, u_ext)):
                    dcw = dcw + jnp.where(wrow == j, jnp.sum(dm * tap, axis=0, keepdims=True), 0.0)
            dcw_ref[:, tile] += dcw

    cur = pl.BlockSpec((r, tc), lambda j, rb: (rb, j))
    prev = pl.BlockSpec((HALO, tc), lambda j, rb: (hidx(rb), j))
    nxt = pl.BlockSpec((HALO, tc), lambda j, rb: (nidx(rb), j))
    wspec = pl.BlockSpec((8, tc), lambda j, rb: (0, j))
    return pl.pallas_call(
        body, name=name, grid=(f // tc, nb),
        in_specs=[cur, prev, nxt, cur, nxt, cur, nxt, wspec],
        out_specs=[cur, cur, wspec], out_shape=[SDS((s, f), BF), SDS((s, f), BF), SDS((8, f), F32)],
        compiler_params=_params("parallel", "arbitrary"))(u, u, u, g, g, da, da, cw)


def _loss_head(y, target, name):
    s, d = y.shape
    ts = _tile(s, 256, 16)

    def body(y_ref, t_ref, dy_ref, dyb_ref, l_ref):
        i = pl.program_id(0)
        err = y_ref[...] - t_ref[...]
        dy_ref[...] = err * (1.0 / d)
        dyb_ref[...] = (err * (1.0 / d)).astype(BF)
        part = jnp.zeros((1, LANE), F32) + 0.5 * jnp.sum(jnp.sum(err * err, axis=1, keepdims=True), axis=0, keepdims=True) / d

        @pl.when(i == 0)
        def _():
            l_ref[...] = part

        @pl.when(i > 0)
        def _():
            l_ref[...] += part

    row = pl.BlockSpec((ts, d), lambda i: (i, 0))
    return pl.pallas_call(
        body, name=name, grid=(s // ts,), in_specs=[row, row],
        out_specs=[row, row, pl.BlockSpec((1, LANE), lambda i: (0, 0))],
        out_shape=[SDS((s, d), F32), SDS((s, d), BF), SDS((1, LANE), F32)], compiler_params=_params("arbitrary"))(y, target)


def _rel_index():
    rel = BAND_LEFT + np.arange(CHUNK)[:, None] - np.arange(BAND)[None, :]
    return (np.clip(rel, -(CHUNK - 1), MAX_REL) + (CHUNK - 1)).reshape(1, CHUNK * BAND).astype(np.int32)


def _onehot(idx_row):
    rows = lax.broadcasted_iota(jnp.int32, (NUM_REL, idx_row.shape[1]), 0)
    return jnp.where(rows == idx_row, 1.0, 0.0).astype(F32)


def _bias_expand(rel_bias, name):
    h = rel_bias.shape[0]
    n = CHUNK * BAND
    tn = n // 8

    def body(rb_ref, idx_ref, o_ref):
        o_ref[...] = jnp.dot(rb_ref[...], _onehot(idx_ref[...]), precision=HI, preferred_element_type=F32)

    out = pl.pallas_call(
        body, name=name, grid=(n // tn,),
        in_specs=[pl.BlockSpec((h, NUM_REL), lambda j: (0, 0)), pl.BlockSpec((1, tn), lambda j: (0, j))],
        out_specs=pl.BlockSpec((h, tn), lambda j: (0, j)), out_shape=SDS((h, n), F32),
        compiler_params=_params("parallel"))(rel_bias, jnp.asarray(_rel_index()))
    return out.reshape(h, CHUNK, BAND)


def _bias_reduce(dbias, name):
    h = dbias.shape[0]
    n = CHUNK * BAND
    tn = n // 8

    def body(db_ref, idx_ref, o_ref):
        j = pl.program_id(0)
        part = lax.dot_general(db_ref[...], _onehot(idx_ref[...]), NT, precision=HI, preferred_element_type=F32)

        @pl.when(j == 0)
        def _():
            o_ref[...] = part

        @pl.when(j > 0)
        def _():
            o_ref[...] += part

    return pl.pallas_call(
        body, name=name, grid=(n // tn,),
        in_specs=[pl.BlockSpec((h, tn), lambda j: (0, j)), pl.BlockSpec((1, tn), lambda j: (0, j))],
        out_specs=pl.BlockSpec((h, NUM_REL), lambda j: (0, 0)), out_shape=SDS((h, NUM_REL), F32),
        compiler_params=_params("arbitrary"))(dbias.reshape(h, n), jnp.asarray(_rel_index()))


def _headnorm(x, gain):
    outs = []
    for hh in range(x.shape[1] // HEAD):
        xh = x[:, hh * HEAD:(hh + 1) * HEAD]
        outs.append(xh * lax.rsqrt(jnp.mean(xh * xh, axis=-1, keepdims=True) + EPS) * gain)
    return jnp.concatenate(outs, axis=1)


def _qkv_post_fwd(qkv, qg, kg, name):
    s, d3 = qkv.shape
    d = d3 // 3
    r = BAND_LEFT
    nb = s // r

    def body(x_ref, qg_ref, kg_ref, q_ref, k_ref, v_ref):
        i = pl.program_id(0)
        keep = jnp.where(i == 0, 0.0, 1.0)
        q_ref[...] = _headnorm(x_ref[:, 0:d].astype(F32), qg_ref[...]).astype(BF)
        k_ref[...] = (_headnorm(x_ref[:, d:2 * d].astype(F32), kg_ref[...]) * keep).astype(BF)
        v_ref[...] = (x_ref[:, 2 * d:].astype(F32) * keep).astype(BF)

    prev = lambda i: (jnp.maximum(i - 1, 0), 0)
    vec = pl.BlockSpec((1, HEAD), lambda i: (0, 0))
    return pl.pallas_call(
        body, name=name, grid=(nb + 1,),
        in_specs=[pl.BlockSpec((r, d3), prev), vec, vec],
        out_specs=[pl.BlockSpec((r, d), prev), pl.BlockSpec((r, d), lambda i: (i, 0)), pl.BlockSpec((r, d), lambda i: (i, 0))],
        out_shape=[SDS((s, d), BF), SDS((s + r, d), BF), SDS((s + r, d), BF)],
        compiler_params=_params("arbitrary"))(qkv, qg, kg)


def _qkv_post_bwd(qkv, qg, kg, dq, dkpad, dvpad, name):
    s, d3 = qkv.shape
    d = d3 // 3
    r = _tile(s, 256, 16)
    off = BAND_LEFT // r

    def body(x_ref, qg_ref, kg_ref, dq_ref, dk_ref, dv_ref, o_ref, dqg_ref, dkg_ref):
        i = pl.program_id(0)
        _, vq = jax.vjp(_headnorm, x_ref[:, 0:d].astype(F32), qg_ref[...])
        dxq, dqg = vq(dq_ref[...])
        _, vk = jax.vjp(_headnorm, x_ref[:, d:2 * d].astype(F32), kg_ref[...])
        dxk, dkg = vk(dk_ref[...])
        o_ref[:, 0:d] = dxq.astype(BF)
        o_ref[:, d:2 * d] = dxk.astype(BF)
        o_ref[:, 2 * d:] = dv_ref[...].astype(BF)

        @pl.when(i == 0)
        def _():
            dqg_ref[...] = dqg
            dkg_ref[...] = dkg

        @pl.when(i > 0)
        def _():
            dqg_ref[...] += dqg
            dkg_ref[...] += dkg

    vec = pl.BlockSpec((1, HEAD), lambda i: (0, 0))
    row3 = pl.BlockSpec((r, d3), lambda i: (i, 0))
    row = pl.BlockSpec((r, d), lambda i: (i, 0))
    padrow = pl.BlockSpec((r, d), lambda i: (i + off, 0))
    return pl.pallas_call(
        body, name=name, grid=(s // r,), in_specs=[row3, vec, vec, row, padrow, padrow],
        out_specs=[row3, vec, vec], out_shape=[SDS((s, d3), BF), SDS((1, HEAD), F32), SDS((1, HEAD), F32)],
        compiler_params=_params("arbitrary"))(qkv, qg, kg, dq, dkpad, dvpad)


ATT_QB = 256


def _att_probs(q, kw, bias, c0):
    sc = _bdot(q, kw, NTB) * (HEAD ** -0.5) + bias
    lane = lax.broadcasted_iota(jnp.int32, sc.shape, 2)
    chunk = lax.broadcasted_iota(jnp.int32, sc.shape, 0) + c0
    sc = jnp.where(lane + chunk * CHUNK >= BAND_LEFT, sc, MASK_VALUE)
    p = jnp.exp(sc - jnp.max(sc, axis=-1, keepdims=True))
    return p / jnp.sum(p, axis=-1, keepdims=True)


def _attn_fwd(q, kpad, vpad, bias, name):
    s, d = q.shape
    h = d // HEAD
    sp = kpad.shape[0]
    qb = _tile(s, ATT_QB, CHUNK)
    per = qb // CHUNK

    def body(q_ref, k_ref, v_ref, b_ref, o_ref):
        c0 = pl.program_id(1) * per
        wins = [pl.ds(pl.multiple_of((c0 + cc) * CHUNK, CHUNK), BAND) for cc in range(per)]
        kw = jnp.stack([k_ref[w, :] for w in wins])
        vw = jnp.stack([v_ref[w, :] for w in wins])
        p = _att_probs(q_ref[...].reshape(per, CHUNK, HEAD), kw, b_ref[0], c0)
        o_ref[...] = _bdot(p.astype(BF), vw, NNB).reshape(qb, HEAD).astype(BF)

    qspec = pl.BlockSpec((qb, HEAD), lambda hh, i: (i, hh))
    kspec = pl.BlockSpec((sp, HEAD), lambda hh, i: (0, hh))
    return pl.pallas_call(
        body, name=name, grid=(h, s // qb),
        in_specs=[qspec, kspec, kspec, pl.BlockSpec((1, CHUNK, BAND), lambda hh, i: (hh, 0, 0))],
        out_specs=qspec, out_shape=SDS((s, d), BF), compiler_params=_params("parallel", "arbitrary"))(q, kpad, vpad, bias)


def _attn_bwd(q, kpad, vpad, bias, do, name):
    s, d = q.shape
    h = d // HEAD
    sp = kpad.shape[0]
    qb = _tile(s, ATT_QB, CHUNK)
    per = qb // CHUNK
    scale = HEAD ** -0.5

    def body(q_ref, k_ref, v_ref, b_ref, do_ref, dq_ref, dk_ref, dv_ref, db_ref):
        i = pl.program_id(1)

        @pl.when(i == 0)
        def _():
            dk_ref[...] = jnp.zeros_like(dk_ref)
            dv_ref[...] = jnp.zeros_like(dv_ref)
            db_ref[...] = jnp.zeros_like(db_ref)

        c0 = i * per
        wins = [pl.ds(pl.multiple_of((c0 + cc) * CHUNK, CHUNK), BAND) for cc in range(per)]
        kw = jnp.stack([k_ref[w, :] for w in wins])
        vw = jnp.stack([v_ref[w, :] for w in wins])
        qc = q_ref[...].reshape(per, CHUNK, HEAD)
        doc = do_ref[...].astype(BF).reshape(per, CHUNK, HEAD)
        p = _att_probs(qc, kw, b_ref[0], c0)
        dp = _bdot(doc, vw, NTB)
        ds = p * (dp - jnp.sum(p * dp, axis=-1, keepdims=True))
        db_ref[0] += jnp.sum(ds, axis=0)
        dsb = (ds * scale).astype(BF)
        dq_ref[...] = _bdot(dsb, kw, NNB).reshape(qb, HEAD)

        def union(x):
            tot = None
            for cc in range(per):
                parts = [x[cc]]
                if cc:
                    parts.insert(0, jnp.zeros((cc * CHUNK, HEAD), F32))
                if cc < per - 1:
                    parts.append(jnp.zeros(((per - 1 - cc) * CHUNK, HEAD), F32))
                piece = jnp.concatenate(parts, axis=0) if len(parts) > 1 else parts[0]
                tot = piece if tot is None else tot + piece
            return tot

        span = pl.ds(pl.multiple_of(c0 * CHUNK, CHUNK), BAND + (per - 1) * CHUNK)
        dk_ref[span, :] += union(_bdot(dsb, qc, TNB))
        dv_ref[span, :] += union(_bdot(p.astype(BF), doc, TNB))

    qspec = pl.BlockSpec((qb, HEAD), lambda hh, i: (i, hh))
    kspec = pl.BlockSpec((sp, HEAD), lambda hh, i: (0, hh))
    bspec = pl.BlockSpec((1, CHUNK, BAND), lambda hh, i: (hh, 0, 0))
    return pl.pallas_call(
        body, name=name, grid=(h, s // qb), in_specs=[qspec, kspec, kspec, bspec, qspec],
        out_specs=[qspec, kspec, kspec, bspec],
        out_shape=[SDS((s, d), F32), SDS((sp, d), F32), SDS((sp, d), F32), SDS((h, CHUNK, BAND), F32)],
        compiler_params=_params("parallel", "arbitrary"))(q, kpad, vpad, bias, do)


def _pool_tile(x_ext, gain, w4, scale, row0):
    n, d = x_ext.shape
    dg = d // len(POOL_WINDOWS)
    pos = lax.broadcasted_iota(jnp.int32, (n, 1), 0) + row0
    hn = _rms(x_ext, gain) * jnp.where(pos >= 0, 1.0, 0.0)
    outs = []
    for gi, w in enumerate(POOL_WINDOWS):
        hg = hn[:, gi * dg:(gi + 1) * dg]
        acc, k = hg, 1
        while k < w:
            acc = acc + _shift(acc, k)
            k *= 2
        inv = 1.0 / jnp.clip(pos + 1, 1, w).astype(F32)
        pooled = acc * inv - hg
        outs.append(jnp.dot(pooled.astype(BF), w4[gi].astype(BF), preferred_element_type=F32))
    return jnp.concatenate(outs, axis=1) * scale


POOL_ROWS = 128


def _pool_fwd(x, gain, w4, scale, name):
    s, d = x.shape
    r = _tile(s, POOL_ROWS, HALO)
    hidx = _halo_index(r)

    def body(xc, xh, g_ref, w_ref, s_ref, o_ref):
        rb = pl.program_id(0)
        x_ext = jnp.concatenate([xh[...], xc[...]], axis=0)
        y = _pool_tile(x_ext, g_ref[...], [w_ref[gi] for gi in range(len(POOL_WINDOWS))], s_ref[...], rb * r - HALO)
        o_ref[...] = xc[...] + y[HALO:]

    cur = pl.BlockSpec((r, d), lambda rb: (rb, 0))
    vec = pl.BlockSpec((1, d), lambda rb: (0, 0))
    return pl.pallas_call(
        body, name=name, grid=(s // r,),
        in_specs=[cur, pl.BlockSpec((HALO, d), lambda rb: (hidx(rb), 0)), vec,
                  pl.BlockSpec(w4.shape, lambda rb: (0, 0, 0)), vec],
        out_specs=cur, out_shape=SDS((s, d), F32), compiler_params=_params("parallel"))(x, x, gain, w4, scale)


def _pool_bwd(x, gain, w4, scale, dy, name):
    s, d = x.shape
    r = _tile(s, POOL_ROWS, HALO)
    nb = s // r
    hidx = _halo_index(r)

    def body(xc, xh, g_ref, w_ref, s_ref, dy_ref, dx_ref, dg_ref, dw_ref, ds_ref, carry):
        step = pl.program_id(0)
        rb = nb - 1 - step
        x_ext = jnp.concatenate([xh[...], xc[...]], axis=0)
        fn = functools.partial(_pool_tile, row0=rb * r - HALO)
        _, vjp = jax.vjp(fn, x_ext, g_ref[...], [w_ref[gi] for gi in range(len(POOL_WINDOWS))], s_ref[...])
        ct = jnp.concatenate([jnp.zeros((HALO, d), F32), dy_ref[...]], axis=0)
        dx_ext, dg, dws, dsc = vjp(ct)

        @pl.when(step == 0)
        def _():
            carry[...] = jnp.zeros_like(carry)
            dg_ref[...] = jnp.zeros_like(dg_ref)
            dw_ref[...] = jnp.zeros_like(dw_ref)
            ds_ref[...] = jnp.zeros_like(ds_ref)

        dx_ref[...] = dy_ref[...] + dx_ext[HALO:]
        dx_ref[pl.ds(r - HALO, HALO), :] += carry[...]
        carry[...] = dx_ext[:HALO]
        dg_ref[...] += dg
        for gi, dw in enumerate(dws):
            dw_ref[gi] += dw
        ds_ref[...] += dsc

    cur = pl.BlockSpec((r, d), lambda t: (nb - 1 - t, 0))
    vec = pl.BlockSpec((1, d), lambda t: (0, 0))
    wspec = pl.BlockSpec(w4.shape, lambda t: (0, 0, 0))
    return pl.pallas_call(
        body, name=name, grid=(nb,),
        in_specs=[cur, pl.BlockSpec((HALO, d), lambda t: (hidx(nb - 1 - t), 0)), vec, wspec, vec, cur],
        out_specs=[cur, vec, wspec, vec],
        out_shape=[SDS((s, d), F32), SDS((1, d), F32), SDS(w4.shape, F32), SDS((1, d), F32)],
        scratch_shapes=[pltpu.VMEM((HALO, d), F32)], compiler_params=_params("arbitrary"))(x, x, gain, w4, scale, dy)


def _gdn_post(acc, kind):
    y = _silu(acc)
    if kind != "v":
        y = y * lax.rsqrt(jnp.sum(y * y, axis=-1, keepdims=True) + EPS)
    if kind == "q":
        y = y * (HEAD ** -0.5)
    return y


GDN_SUB = 128
GDN_CONV_HEADS = 4


def _gdn_conv_fwd(proj, cw, kind, head0, nheads, name):
    s = proj.shape[0]
    r = _tile(s, 512, HALO)
    sub = _tile(r, GDN_SUB, HALO)
    hb = min(GDN_CONV_HEADS, nheads)
    assert head0 % hb == 0 and nheads % hb == 0
    tc = hb * HEAD
    hidx = _halo_index(r)

    def body(uc, uh, cw_ref, o_ref):
        keep = jnp.where(pl.program_id(1) == 0, 0.0, 1.0)
        for hh in range(hb):
            tile = pl.ds(hh * HEAD, HEAD)
            taps = [cw_ref[j:j + 1, tile] for j in range(GDN_CONV)]
            for r0 in range(0, r, sub):
                u_ext = _ext_rows(uc, uh, False, r0, sub, tile, keep)
                acc = u_ext * taps[3]
                for j in range(1, GDN_CONV):
                    acc = acc + pltpu.roll(u_ext, j, axis=0) * taps[3 - j]
                o_ref[pl.ds(r0, sub), tile] = _gdn_post(acc, kind)[HALO:].astype(BF)

    return pl.pallas_call(
        body, name=name, grid=(nheads // hb, s // r),
        in_specs=[pl.BlockSpec((r, tc), lambda j, rb: (rb, head0 // hb + j)),
                  pl.BlockSpec((HALO, tc), lambda j, rb: (hidx(rb), head0 // hb + j)),
                  pl.BlockSpec((8, tc), lambda j, rb: (0, head0 // hb + j))],
        out_specs=pl.BlockSpec((r, tc), lambda j, rb: (rb, j)), out_shape=SDS((s, nheads * HEAD), BF),
        compiler_params=_params("parallel", "parallel"))(proj, proj, cw)


def _gdn_conv_bwd(proj, cw, dy, kind, head0, nheads, name):
    s = proj.shape[0]
    r = _tile(s, 512, HALO)
    sub = _tile(r, GDN_SUB, HALO)
    nb = s // r
    per = r // HALO
    hb = min(GDN_CONV_HEADS, nheads)
    tc = hb * HEAD
    hidx = _halo_index(r)
    nidx = lambda rb: jnp.minimum((rb + 1) * per, s // HALO - 1)
    rep = dy.shape[1] // (nheads * HEAD)
    n = sub + 2 * HALO

    def body(uc, uh, un, cw_ref, dyc, dyn, du_ref, dcw_ref):
        rb = pl.program_id(1)
        first = jnp.where(rb == 0, 0.0, 1.0)
        last = jnp.where(rb == nb - 1, 0.0, 1.0)
        rows = lax.broadcasted_iota(jnp.int32, (n, HEAD), 0)
        own = jnp.where((rows >= HALO) & (rows < HALO + sub), 1.0, 0.0)
        wrow = lax.broadcasted_iota(jnp.int32, (8, HEAD), 0)

        @pl.when(rb == 0)
        def _():
            dcw_ref[...] = jnp.zeros_like(dcw_ref)

        for hh in range(hb):
            tile = pl.ds(hh * HEAD, HEAD)
            taps = [cw_ref[j:j + 1, tile] for j in range(GDN_CONV)]
            dcw = jnp.zeros((8, HEAD), F32)
            for r0 in range(0, r, sub):
                u_ext = _ext_rows(uc, uh, un, r0, sub, tile, first, last)
                dy_ext = _ext_rows(dyc, None, dyn, r0, sub, pl.ds(hh * rep * HEAD, HEAD), 1.0, last)
                for e in range(1, rep):
                    dy_ext = dy_ext + _ext_rows(dyc, None, dyn, r0, sub, pl.ds((hh * rep + e) * HEAD, HEAD), 1.0, last)
                shifted = [u_ext] + [pltpu.roll(u_ext, j, axis=0) for j in range(1, GDN_CONV)]
                acc = shifted[0] * taps[3]
                for j in range(1, GDN_CONV):
                    acc = acc + shifted[j] * taps[3 - j]
                _, vjp = jax.vjp(functools.partial(_gdn_post, kind=kind), acc)
                dacc, = vjp(dy_ext)
                du = dacc * taps[3]
                for j in range(1, GDN_CONV):
                    du = du + pltpu.roll(dacc, n - j, axis=0) * taps[3 - j]
                du_ref[pl.ds(r0, sub), tile] = du[HALO:HALO + sub].astype(BF)
                dm = dacc * own
                for j in range(GDN_CONV):
                    dcw = dcw + jnp.where(wrow == j, jnp.sum(dm * shifted[3 - j], axis=0, keepdims=True), 0.0)
            dcw_ref[:, tile] += dcw

    ucol = lambda j: head0 // hb + j
    return pl.pallas_call(
        body, name=name, grid=(nheads // hb, nb),
        in_specs=[pl.BlockSpec((r, tc), lambda j, rb: (rb, ucol(j))),
                  pl.BlockSpec((HALO, tc), lambda j, rb: (hidx(rb), ucol(j))),
                  pl.BlockSpec((HALO, tc), lambda j, rb: (nidx(rb), ucol(j))),
                  pl.BlockSpec((8, tc), lambda j, rb: (0, ucol(j))),
                  pl.BlockSpec((r, rep * tc), lambda j, rb: (rb, j)),
                  pl.BlockSpec((HALO, rep * tc), lambda j, rb: (nidx(rb), j))],
        out_specs=[pl.BlockSpec((r, tc), lambda j, rb: (rb, j)), pl.BlockSpec((8, tc), lambda j, rb: (0, j))],
        out_shape=[SDS((s, nheads * HEAD), BF), SDS((8, nheads * HEAD), F32)],
        compiler_params=_params("parallel", "arbitrary"))(proj, proj, proj, cw, dy, dy)


GATE_ROWS = 256


def _gates_tile(a, bt, a_log, dt_bias, hv):
    r = a.shape[0]
    z = a + dt_bias
    softplus = jnp.maximum(z, 0.0) + jnp.log(1.0 + jnp.exp(-jnp.abs(z)))
    g = -jnp.exp(a_log) * softplus
    ri = lax.broadcasted_iota(jnp.int32, (r, r), 0)
    ci = lax.broadcasted_iota(jnp.int32, (r, r), 1)
    same_chunk = jnp.right_shift(ri, 6) == jnp.right_shift(ci, 6)
    tri = jnp.where(same_chunk, jnp.where(ri >= ci, 1.0, 0.0), 0.0).astype(F32)
    gc = jnp.dot(tri, g, precision=HI, preferred_element_type=F32)
    beta = 1.0 / (1.0 + jnp.exp(-bt))
    er = lax.broadcasted_iota(jnp.int32, (LANE, hv * HEAD), 0)
    ec = lax.broadcasted_iota(jnp.int32, (LANE, hv * HEAD), 1)
    expand = jnp.where(er == jnp.right_shift(ec, 7), 1.0, 0.0).astype(F32)
    return (jnp.dot(gc, expand, precision=HI, preferred_element_type=F32),
            jnp.dot(beta, expand, precision=HI, preferred_element_type=F32))


def _gates_fwd(ab, a_log, dt_bias, hv, name):
    s = ab.shape[0]
    r = _tile(s, GATE_ROWS, CHUNK)

    def body(a_ref, b_ref, al_ref, dt_ref, gc_ref, bb_ref):
        gcb, btb = _gates_tile(a_ref[...], b_ref[...], al_ref[...], dt_ref[...], hv)
        gc_ref[...] = gcb
        bb_ref[...] = btb

    vec = pl.BlockSpec((1, LANE), lambda i: (0, 0))
    wide = pl.BlockSpec((r, hv * HEAD), lambda i: (i, 0))
    return pl.pallas_call(
        body, name=name, grid=(s // r,),
        in_specs=[pl.BlockSpec((r, LANE), lambda i: (i, 0)), pl.BlockSpec((r, LANE), lambda i: (i, 1)), vec, vec],
        out_specs=[wide, wide], out_shape=[SDS((s, hv * HEAD), F32)] * 2,
        compiler_params=_params("parallel"))(ab, ab, a_log, dt_bias)


def _gates_bwd(ab, a_log, dt_bias, dgcb, dbtb, hv, name):
    s = ab.shape[0]
    r = _tile(s, GATE_ROWS, CHUNK)

    def body(a_ref, b_ref, al_ref, dt_ref, dgc_ref, dbb_ref, dab_ref, dal_ref, ddt_ref):
        i = pl.program_id(0)
        _, vjp = jax.vjp(functools.partial(_gates_tile, hv=hv), a_ref[...], b_ref[...], al_ref[...], dt_ref[...])
        da, dbt, dal, ddt = vjp((dgc_ref[...], dbb_ref[...]))
        dab_ref[:, 0:LANE] = da
        dab_ref[:, LANE:] = dbt

        @pl.when(i == 0)
        def _():
            dal_ref[...] = dal
            ddt_ref[...] = ddt

        @pl.when(i > 0)
        def _():
            dal_ref[...] += dal
            ddt_ref[...] += ddt

    vec = pl.BlockSpec((1, LANE), lambda i: (0, 0))
    wide = pl.BlockSpec((r, hv * HEAD), lambda i: (i, 0))
    return pl.pallas_call(
        body, name=name, grid=(s // r,),
        in_specs=[pl.BlockSpec((r, LANE), lambda i: (i, 0)), pl.BlockSpec((r, LANE), lambda i: (i, 1)), vec, vec, wide, wide],
        out_specs=[pl.BlockSpec((r, 2 * LANE), lambda i: (i, 0)), vec, vec],
        out_shape=[SDS((s, 2 * LANE), F32), SDS((1, LANE), F32), SDS((1, LANE), F32)],
        compiler_params=_params("arbitrary"))(ab, ab, a_log, dt_bias, dgcb, dbtb)


def _split_bf16(a):
    hi = a.astype(BF)
    return hi, (a - hi.astype(F32)).astype(BF)


def _dot3(a, b, dims=(((1,), (0,)), ((), ()))):
    ah, al = _split_bf16(a)
    bh, bl = _split_bf16(b)
    d = lambda x, y: lax.dot_general(x, y, dims, preferred_element_type=F32)
    return d(ah, bh) + (d(ah, bl) + d(al, bh))


NNB = (((2,), (1,)), ((0,), (0,)))
NTB = (((2,), (2,)), ((0,), (0,)))
TNB = (((1,), (1,)), ((0,), (0,)))


def _bdot(a, b, dims):
    return lax.dot_general(a, b, dims, preferred_element_type=F32)


def _unit_lower_inverse(a):
    ri = lax.broadcasted_iota(jnp.int32, a.shape, 1)
    ci = lax.broadcasted_iota(jnp.int32, a.shape, 2)
    p = -a
    t = jnp.where(ri == ci, 1.0, 0.0) + p
    for _ in range(5):
        p = _dot3(p, p, NNB)
        t = t + _dot3(t, p, NNB)
    return t


@jax.custom_vjp
def _known_inverse(a, t):
    return t


def _known_inverse_fwd(a, t):
    return t, t


def _known_inverse_bwd(t, g):
    return -_dot3(_dot3(t, g, TNB), t, NTB), jnp.zeros_like(t)


_known_inverse.defvjp(_known_inverse_fwd, _known_inverse_bwd)


def _delta_decay(gcb):
    c = CHUNK
    shape = (gcb.shape[0], c, c)
    ri = lax.broadcasted_iota(jnp.int32, shape, 1)
    ci = lax.broadcasted_iota(jnp.int32, shape, 2)
    causal = ri >= ci
    grow = jnp.stack([jnp.concatenate([gcb[b], gcb[b]], axis=0).T[:c, :c] for b in range(shape[0])])
    return jnp.where(causal, jnp.exp(jnp.where(causal, gcb[:, :, :c] - grow, 0.0)), 0.0), ri > ci


def _delta_system(k, gcb, btb):
    decay, strict = _delta_decay(gcb)
    return jnp.where(strict, _bdot((k * btb).astype(BF), k.astype(BF), NTB) * decay, 0.0)


def _delta_prep(q, k, v, gcb, btb, tinv):
    decay, strict = _delta_decay(gcb)
    kb = k * btb
    kbf = k.astype(BF)
    a = jnp.where(strict, _bdot(kb.astype(BF), kbf, NTB) * decay, 0.0)
    t = _known_inverse(a, tinv).astype(BF)
    u = _bdot(t, (v * btb).astype(BF), NNB)
    w = _bdot(t, (kb * jnp.exp(gcb)).astype(BF), NNB)
    attn = _bdot(q.astype(BF), kbf, NTB) * decay
    return u, w, attn


def _delta_scan(u, w, attn, q, k, gcb, s_in):
    c = CHUNK
    glast = gcb[:, c - 1:c, :]
    sb = s_in.astype(BF)
    v_new = u - _bdot(w.astype(BF), sb, NNB)
    vnb = v_new.astype(BF)
    o = _bdot((q * jnp.exp(gcb)).astype(BF), sb, NNB) + _bdot(attn.astype(BF), vnb, NNB)
    ks = (k * jnp.exp(glast - gcb)).astype(BF)
    s_out = s_in * jnp.exp(glast[:, :, 0:1]) + _bdot(ks, vnb, TNB)
    return o, s_out


def _head_stack(ref, rows, width, heads, rep=1):
    return jnp.stack([ref[rows, pl.ds((hh // rep) * width, width)].astype(F32) for hh in range(heads)])


PREP_ROWS = 512
PREP_HEADS = 2
SCAN_ROWS = 512
SCAN_HEADS = 4


def _delta_prep_fwd(q, k, v, gcb, btb, name):
    s, dv = v.shape
    hv = dv // HEAD
    g = PREP_HEADS
    assert dv // q.shape[1] == g
    r = _tile(s, PREP_ROWS, CHUNK)

    def body(q_ref, k_ref, v_ref, g_ref, b_ref, u_ref, w_ref, a_ref, t_ref):
        nb = r // CHUNK
        qc = q_ref[...].astype(F32).reshape(nb, CHUNK, HEAD)
        kc = k_ref[...].astype(F32).reshape(nb, CHUNK, HEAD)
        for hh in range(g):
            cols = pl.ds(hh * HEAD, HEAD)
            half = pl.ds(hh * CHUNK, CHUNK)
            gc = g_ref[:, cols].reshape(nb, CHUNK, HEAD)
            bc = b_ref[:, cols].reshape(nb, CHUNK, HEAD)
            tinv = _unit_lower_inverse(_delta_system(kc, gc, bc))
            u, w, attn = _delta_prep(qc, kc, v_ref[:, cols].astype(F32).reshape(nb, CHUNK, HEAD), gc, bc, tinv)
            u_ref[:, cols] = u.reshape(r, HEAD)
            w_ref[:, cols] = w.reshape(r, HEAD).astype(BF)
            a_ref[:, half] = attn.reshape(r, CHUNK).astype(BF)
            t_ref[:, half] = tinv.reshape(r, CHUNK)

    kq = pl.BlockSpec((r, HEAD), lambda j, i: (i, j))
    vs = pl.BlockSpec((r, g * HEAD), lambda j, i: (i, j))
    sq = pl.BlockSpec((r, g * CHUNK), lambda j, i: (i, j))
    return pl.pallas_call(
        body, name=name, grid=(hv // g, s // r), in_specs=[kq, kq, vs, vs, vs], out_specs=[vs, vs, sq, sq],
        out_shape=[SDS((s, dv), F32), SDS((s, dv), BF), SDS((s, hv * CHUNK), BF), SDS((s, hv * CHUNK), F32)],
        compiler_params=_params("parallel", "parallel"))(q, k, v, gcb, btb)


def _delta_prep_bwd(q, k, v, gcb, btb, tinv, du, dw, dattn, dq_s, dk_s, dg_s, name):
    s, dv = v.shape
    hv = dv // HEAD
    g = PREP_HEADS
    r = _tile(s, PREP_ROWS, CHUNK)

    def body(q_ref, k_ref, v_ref, g_ref, b_ref, t_ref, du_ref, dw_ref, da_ref, dqs_ref, dks_ref, dgs_ref,
             dq_ref, dk_ref, dv_ref, dg_ref, db_ref):
        nb = r // CHUNK
        wide = lambda ref, cols: ref[:, cols].astype(F32).reshape(nb, CHUNK, HEAD)
        qc = q_ref[...].astype(F32).reshape(nb, CHUNK, HEAD)
        kc = k_ref[...].astype(F32).reshape(nb, CHUNK, HEAD)
        for hh in range(g):
            cols = pl.ds(hh * HEAD, HEAD)
            half = pl.ds(hh * CHUNK, CHUNK)
            fn = functools.partial(_delta_prep, tinv=t_ref[:, half].reshape(nb, CHUNK, CHUNK))
            _, vjp = jax.vjp(fn, qc, kc, wide(v_ref, cols), wide(g_ref, cols), wide(b_ref, cols))
            dq, dk, dvv, dg, db = vjp((wide(du_ref, cols), wide(dw_ref, cols),
                                       da_ref[:, half].astype(F32).reshape(nb, CHUNK, CHUNK)))
            dq_ref[:, cols] = dq.reshape(r, HEAD) + dqs_ref[:, cols]
            dk_ref[:, cols] = dk.reshape(r, HEAD) + dks_ref[:, cols]
            dv_ref[:, cols] = dvv.reshape(r, HEAD)
            dg_ref[:, cols] = dg.reshape(r, HEAD) + dgs_ref[:, cols]
            db_ref[:, cols] = db.reshape(r, HEAD)

    kq = pl.BlockSpec((r, HEAD), lambda j, i: (i, j))
    vs = pl.BlockSpec((r, g * HEAD), lambda j, i: (i, j))
    sq = pl.BlockSpec((r, g * CHUNK), lambda j, i: (i, j))
    return pl.pallas_call(
        body, name=name, grid=(hv // g, s // r), in_specs=[kq, kq, vs, vs, vs, sq, vs, vs, sq, vs, vs, vs],
        out_specs=[vs] * 5, out_shape=[SDS((s, dv), F32)] * 5,
        compiler_params=_params("parallel", "parallel"))(q, k, v, gcb, btb, tinv, du, dw, dattn, dq_s, dk_s, dg_s)


def _delta_scan_fwd(u, w, attn, q, k, gcb, name):
    s, dv = u.shape
    hv = dv // HEAD
    rep = dv // q.shape[1]
    g = min(SCAN_HEADS, hv)
    r = _tile(s, SCAN_ROWS, CHUNK)
    per = r // CHUNK

    def body(u_ref, w_ref, a_ref, q_ref, k_ref, g_ref, o_ref, st_ref, state):
        @pl.when(pl.program_id(1) == 0)
        def _():
            state[...] = jnp.zeros_like(state)

        def chunk(cc, carry):
            rows = pl.ds(pl.multiple_of(cc * CHUNK, CHUNK), CHUNK)
            s_in = state[...]
            o, s_out = _delta_scan(_head_stack(u_ref, rows, HEAD, g), _head_stack(w_ref, rows, HEAD, g),
                                   _head_stack(a_ref, rows, CHUNK, g), _head_stack(q_ref, rows, HEAD, g, rep),
                                   _head_stack(k_ref, rows, HEAD, g, rep), _head_stack(g_ref, rows, HEAD, g), s_in)
            for hh in range(g):
                st_ref[hh, cc] = s_in[hh]
                o_ref[rows, pl.ds(hh * HEAD, HEAD)] = o[hh].astype(BF)
            state[...] = s_out
            return carry

        lax.fori_loop(0, per, chunk, 0)

    kq = pl.BlockSpec((r, g // rep * HEAD), lambda j, i: (i, j))
    vs = pl.BlockSpec((r, g * HEAD), lambda j, i: (i, j))
    sq = pl.BlockSpec((r, g * CHUNK), lambda j, i: (i, j))
    return pl.pallas_call(
        body, name=name, grid=(hv // g, s // r), in_specs=[vs, vs, sq, kq, kq, vs],
        out_specs=[vs, pl.BlockSpec((g, per, HEAD, HEAD), lambda j, i: (j, i, 0, 0))],
        out_shape=[SDS((s, dv), BF), SDS((hv, s // CHUNK, HEAD, HEAD), F32)],
        scratch_shapes=[pltpu.VMEM((g, HEAD, HEAD), F32)],
        compiler_params=_params("parallel", "arbitrary"))(u, w, attn, q, k, gcb)


def _delta_scan_bwd(u, w, attn, q, k, gcb, states, do, name):
    s, dv = u.shape
    hv = dv // HEAD
    rep = dv // q.shape[1]
    g = min(SCAN_HEADS, hv)
    r = _tile(s, SCAN_ROWS, CHUNK)
    per = r // CHUNK
    nb = s // r

    def body(u_ref, w_ref, a_ref, q_ref, k_ref, g_ref, st_ref, do_ref, du_ref, dw_ref, da_ref, dq_ref, dk_ref, dg_ref, dstate):
        @pl.when(pl.program_id(1) == 0)
        def _():
            dstate[...] = jnp.zeros_like(dstate)

        def chunk(t, carry):
            cc = per - 1 - t
            rows = pl.ds(pl.multiple_of(cc * CHUNK, CHUNK), CHUNK)
            s_in = jnp.stack([st_ref[hh, cc] for hh in range(g)])
            _, vjp = jax.vjp(_delta_scan, _head_stack(u_ref, rows, HEAD, g), _head_stack(w_ref, rows, HEAD, g),
                             _head_stack(a_ref, rows, CHUNK, g), _head_stack(q_ref, rows, HEAD, g, rep),
                             _head_stack(k_ref, rows, HEAD, g, rep), _head_stack(g_ref, rows, HEAD, g), s_in)
            du, dw, da, dq, dk, dg, ds_in = vjp((_head_stack(do_ref, rows, HEAD, g), dstate[...]))
            for hh in range(g):
                cols = pl.ds(hh * HEAD, HEAD)
                du_ref[rows, cols] = du[hh].astype(BF)
                dw_ref[rows, cols] = dw[hh].astype(BF)
                da_ref[rows, pl.ds(hh * CHUNK, CHUNK)] = da[hh].astype(BF)
                dq_ref[rows, cols] = dq[hh]
                dk_ref[rows, cols] = dk[hh]
                dg_ref[rows, cols] = dg[hh]
            dstate[...] = ds_in
            return carry

        lax.fori_loop(0, per, chunk, 0)

    kq = pl.BlockSpec((r, g // rep * HEAD), lambda j, i: (nb - 1 - i, j))
    vs = pl.BlockSpec((r, g * HEAD), lambda j, i: (nb - 1 - i, j))
    sq = pl.BlockSpec((r, g * CHUNK), lambda j, i: (nb - 1 - i, j))
    return pl.pallas_call(
        body, name=name, grid=(hv // g, nb),
        in_specs=[vs, vs, sq, kq, kq, vs, pl.BlockSpec((g, per, HEAD, HEAD), lambda j, i: (j, nb - 1 - i, 0, 0)), vs],
        out_specs=[vs, vs, sq, vs, vs, vs],
        out_shape=[SDS((s, dv), BF), SDS((s, dv), BF), SDS((s, hv * CHUNK), BF)] + [SDS((s, dv), F32)] * 3,
        scratch_shapes=[pltpu.VMEM((g, HEAD, HEAD), F32)],
        compiler_params=_params("parallel", "arbitrary"))(u, w, attn, q, k, gcb, states, do)


def _delta_chunk(q, k, v, gcb, btb, s_in):
    c = CHUNK
    ri = lax.broadcasted_iota(jnp.int32, (c, c), 0)
    ci = lax.broadcasted_iota(jnp.int32, (c, c), 1)
    causal = ri >= ci
    gcol = gcb[:, :c]
    grow = jnp.concatenate([gcb, gcb], axis=0).T[:c, :c]
    decay = jnp.where(causal, jnp.exp(jnp.where(causal, gcol - grow, 0.0)), 0.0)
    kb = k * btb
    vb = v * btb
    kbf = k.astype(BF)
    a = jnp.where(ri > ci, lax.dot_general(kb.astype(BF), kbf, NT, preferred_element_type=F32) * decay, 0.0)
    p = -a
    t = jnp.where(ri == ci, 1.0, 0.0) + p
    for _ in range(5):
        p = jnp.dot(p, p, precision=HI, preferred_element_type=F32)
        t = t + jnp.dot(t, p, precision=HI, preferred_element_type=F32)
    eg = jnp.exp(gcb)
    u = jnp.dot(t, vb, precision=HI, preferred_element_type=F32)
    w = jnp.dot(t, kb * eg, precision=HI, preferred_element_type=F32)
    attn = lax.dot_general(q.astype(BF), kbf, NT, preferred_element_type=F32) * decay
    glast = gcb[c - 1:c, :]
    ks = k * jnp.exp(glast - gcb)
    sb = s_in.astype(BF)
    v_new = u - jnp.dot(w.astype(BF), sb, preferred_element_type=F32)
    o = (jnp.dot((q * eg).astype(BF), sb, preferred_element_type=F32)
         + jnp.dot(attn.astype(BF), v_new.astype(BF), preferred_element_type=F32))
    s_out = s_in * jnp.exp(glast[:, 0:1]) + lax.dot_general(ks.astype(BF), v_new.astype(BF), TN, preferred_element_type=F32)
    return o, s_out


GDN_ROWS = 512


def _delta_fwd(q, k, v, gcb, btb, name):
    s, dv = v.shape
    hv = dv // HEAD
    rep = dv // q.shape[1]
    r = _tile(s, GDN_ROWS, CHUNK)
    per = r // CHUNK
    nc = s // CHUNK

    def body(q_ref, k_ref, v_ref, g_ref, b_ref, o_ref, st_ref, state):
        @pl.when(pl.program_id(1) == 0)
        def _():
            state[...] = jnp.zeros_like(state)

        def chunk(cc, carry):
            rows = pl.ds(pl.multiple_of(cc * CHUNK, CHUNK), CHUNK)
            st_ref[0, cc] = state[...]
            o, s_out = _delta_chunk(q_ref[rows, :].astype(F32), k_ref[rows, :].astype(F32), v_ref[rows, :].astype(F32),
                                    g_ref[rows, :], b_ref[rows, :], state[...])
            o_ref[rows, :] = o.astype(BF)
            state[...] = s_out
            return carry

        lax.fori_loop(0, per, chunk, 0)

    kq = pl.BlockSpec((r, HEAD), lambda h, i: (i, h // rep))
    vs = pl.BlockSpec((r, HEAD), lambda h, i: (i, h))
    return pl.pallas_call(
        body, name=name, grid=(hv, s // r), in_specs=[kq, kq, vs, vs, vs],
        out_specs=[vs, pl.BlockSpec((1, per, HEAD, HEAD), lambda h, i: (h, i, 0, 0))],
        out_shape=[SDS((s, dv), BF), SDS((hv, nc, HEAD, HEAD), F32)],
        scratch_shapes=[pltpu.VMEM((HEAD, HEAD), F32)],
        compiler_params=_params("parallel", "arbitrary"))(q, k, v, gcb, btb)


def _delta_bwd(q, k, v, gcb, btb, states, do, name):
    s, dv = v.shape
    hv = dv // HEAD
    rep = dv // q.shape[1]
    r = _tile(s, GDN_ROWS, CHUNK)
    per = r // CHUNK
    nb = s // r

    def body(q_ref, k_ref, v_ref, g_ref, b_ref, st_ref, do_ref, dq_ref, dk_ref, dv_ref, dg_ref, db_ref, dstate):
        @pl.when(pl.program_id(1) == 0)
        def _():
            dstate[...] = jnp.zeros_like(dstate)

        def chunk(t, carry):
            cc = per - 1 - t
            rows = pl.ds(pl.multiple_of(cc * CHUNK, CHUNK), CHUNK)
            _, vjp = jax.vjp(_delta_chunk, q_ref[rows, :].astype(F32), k_ref[rows, :].astype(F32),
                             v_ref[rows, :].astype(F32), g_ref[rows, :], b_ref[rows, :], st_ref[0, cc])
            dq, dk, dvv, dg, db, ds_in = vjp((do_ref[rows, :].astype(F32), dstate[...]))
            dq_ref[rows, :] = dq
            dk_ref[rows, :] = dk
            dv_ref[rows, :] = dvv
            dg_ref[rows, :] = dg
            db_ref[rows, :] = db
            dstate[...] = ds_in
            return carry

        lax.fori_loop(0, per, chunk, 0)

    kq = pl.BlockSpec((r, HEAD), lambda h, i: (nb - 1 - i, h // rep))
    vs = pl.BlockSpec((r, HEAD), lambda h, i: (nb - 1 - i, h))
    return pl.pallas_call(
        body, name=name, grid=(hv, nb),
        in_specs=[kq, kq, vs, vs, vs, pl.BlockSpec((1, per, HEAD, HEAD), lambda h, i: (h, nb - 1 - i, 0, 0)), vs],
        out_specs=[vs] * 5, out_shape=[SDS((s, dv), F32)] * 5,
        scratch_shapes=[pltpu.VMEM((HEAD, HEAD), F32)],
        compiler_params=_params("parallel", "arbitrary"))(q, k, v, gcb, btb, states, do)


def _gdn_out_tile(o, gate, gain):
    return _headnorm(o, gain) * _silu(gate)


def _gdn_out_fwd(o, proj, gate_col0, gain, name):
    s, dv = o.shape
    r = _tile(s, 128, 16)

    def body(o_ref, g_ref, gain_ref, y_ref):
        y_ref[...] = _gdn_out_tile(o_ref[...].astype(F32), g_ref[...].astype(F32), gain_ref[...]).astype(BF)

    row = pl.BlockSpec((r, dv), lambda i: (i, 0))
    return pl.pallas_call(
        body, name=name, grid=(s // r,),
        in_specs=[row, pl.BlockSpec((r, dv), lambda i: (i, gate_col0)), pl.BlockSpec((1, HEAD), lambda i: (0, 0))],
        out_specs=row, out_shape=SDS((s, dv), BF), compiler_params=_params("parallel"))(o, proj, gain)


def _gdn_out_bwd(o, proj, gate_col0, gain, dy, name):
    s, dv = o.shape
    r = _tile(s, 128, 16)

    def body(o_ref, g_ref, gain_ref, dy_ref, do_ref, dg_ref, dgain_ref):
        i = pl.program_id(0)
        _, vjp = jax.vjp(_gdn_out_tile, o_ref[...].astype(F32), g_ref[...].astype(F32), gain_ref[...])
        do, dg, dgain = vjp(dy_ref[...].astype(F32))
        do_ref[...] = do
        dg_ref[...] = dg.astype(BF)

        @pl.when(i == 0)
        def _():
            dgain_ref[...] = dgain

        @pl.when(i > 0)
        def _():
            dgain_ref[...] += dgain

    row = pl.BlockSpec((r, dv), lambda i: (i, 0))
    vec = pl.BlockSpec((1, HEAD), lambda i: (0, 0))
    return pl.pallas_call(
        body, name=name, grid=(s // r,),
        in_specs=[row, pl.BlockSpec((r, dv), lambda i: (i, gate_col0)), vec, row],
        out_specs=[row, row, vec], out_shape=[SDS((s, dv), F32), SDS((s, dv), BF), SDS((1, HEAD), F32)],
        compiler_params=_params("arbitrary"))(o, proj, gain, dy)


WIDE_K = dict(tm=512, tn=1024, tk=8192, n_outer=True)
DEEP_K = 4096


def _ffn_forward(x, gain, wu, wg, cw, wd, tag):
    h, ht = _rmsnorm_fwd(x, gain, f"{tag}_norm")
    uu = _mm(h, wu, name=f"{tag}_up_u")
    ug = _mm(h, wg, name=f"{tag}_up_g")
    a, at = _ffn_act_fwd(uu, ug, cw, f"{tag}_act")
    y = _mm(a, wd, res=x, out_dtype=F32, name=f"{tag}_down", **WIDE_K)
    return y, (x, ht, uu, ug, at)


def _ffn_backward(saved, dys, gain, wu, wg, cw, wd, tag, early=None):
    x, ht, uu, ug, at = saved
    dy, dyb = dys
    da = _mm(dyb, wd, tb=True, name=f"{tag}_d_act")
    dwd = _mm(at, dyb, out_dtype=F32, name=f"{tag}_d_wd", tk=DEEP_K)
    duu, dug, dcw = _ffn_act_bwd(uu, ug, cw, da, f"{tag}_act_bwd")
    dwu = _mm(ht, duu, out_dtype=F32, name=f"{tag}_d_wu", tk=DEEP_K)
    dwg = _mm(ht, dug, out_dtype=F32, name=f"{tag}_d_wg", tk=DEEP_K)
    grads = dict(wu=dwu, wg=dwg, cw=dcw, wd=dwd)
    zero = early(grads) if early is not None else 0.0
    dh = _mm(duu, wu, tb=True, out_dtype=F32, name=f"{tag}_d_h_u", **WIDE_K)
    dh = _mm(dug, wg, tb=True, res=dh, out_dtype=F32, name=f"{tag}_d_h_g", **WIDE_K)
    dx, dxb, dgain = _rmsnorm_bwd(x, gain + zero, dh, dy, f"{tag}_norm_bwd")
    return (dx, dxb), dict(grads, gain=dgain)


def _att_forward(x, gain, p, tag):
    h, ht = _rmsnorm_fwd(x, gain, f"{tag}_norm")
    qkv = _mm(h, p["wqkv"], name=f"{tag}_qkv")
    q, kpad, vpad = _qkv_post_fwd(qkv, p["qg"], p["kg"], f"{tag}_qknorm")
    bias = _bias_expand(p["rel"], f"{tag}_bias")
    o = _attn_fwd(q, kpad, vpad, bias, f"{tag}_core")
    y = _mm(o, p["wo"], res=x, out_dtype=F32, name=f"{tag}_out")
    return y, (x, ht, qkv, q, kpad, vpad, bias, o)


def _att_backward(saved, dys, gain, p, tag, early=None):
    x, ht, qkv, q, kpad, vpad, bias, o = saved
    dy, dyb = dys
    do = _mm(dyb, p["wo"], tb=True, name=f"{tag}_d_o")
    dwo = _mm(o, dyb, ta=True, out_dtype=F32, name=f"{tag}_d_wo")
    dq, dkpad, dvpad, dbias = _attn_bwd(q, kpad, vpad, bias, do, f"{tag}_core_bwd")
    drel = _bias_reduce(dbias, f"{tag}_bias_bwd")
    dqkv, dqg, dkg = _qkv_post_bwd(qkv, p["qg"], p["kg"], dq, dkpad, dvpad, f"{tag}_qknorm_bwd")
    dwqkv = _mm(ht, dqkv, out_dtype=F32, name=f"{tag}_d_wqkv", tk=DEEP_K)
    grads = dict(wqkv=dwqkv, qg=dqg, kg=dkg, rel=drel, wo=dwo)
    zero = early(grads) if early is not None else 0.0
    dh = _mm(dqkv, p["wqkv"], tb=True, out_dtype=F32, name=f"{tag}_d_h", **WIDE_K)
    dx, dxb, dgain = _rmsnorm_bwd(x, gain + zero, dh, dy, f"{tag}_norm_bwd")
    return (dx, dxb), dict(grads, gain=dgain)


def _gdn_forward(x, gain, p, tag):
    d = x.shape[1]
    nk = d // HEAD
    hv = 2 * nk
    h, ht = _rmsnorm_fwd(x, gain, f"{tag}_norm")
    proj = _mm(h, p["wmain"], name=f"{tag}_proj")
    ab = _mm(h, p["wab"], out_dtype=F32, name=f"{tag}_proj_ab")
    q = _gdn_conv_fwd(proj, p["cw"], "q", 0, nk, f"{tag}_conv_q")
    k = _gdn_conv_fwd(proj, p["cw"], "k", nk, nk, f"{tag}_conv_k")
    v = _gdn_conv_fwd(proj, p["cw"], "v", 2 * nk, hv, f"{tag}_conv_v")
    gcb, btb = _gates_fwd(ab, p["a_log"], p["dt_bias"], hv, f"{tag}_gates")
    u, wd, attn, tinv = _delta_prep_fwd(q, k, v, gcb, btb, f"{tag}_delta_prep")
    o, states = _delta_scan_fwd(u, wd, attn, q, k, gcb, f"{tag}_delta_scan")
    og = _gdn_out_fwd(o, proj, 2, p["o_gain"], f"{tag}_onorm")
    y = _mm(og, p["wo"], res=x, out_dtype=F32, name=f"{tag}_out")
    return y, (x, ht, proj, ab, q, k, v, gcb, btb, u, wd, attn, tinv, o, states, og)


def _gdn_backward(saved, dys, gain, p, tag, early=None):
    x, ht, proj, ab, q, k, v, gcb, btb, u, wd, attn, tinv, o, states, og = saved
    dy, dyb = dys
    d = x.shape[1]
    nk = d // HEAD
    hv = 2 * nk
    dog = _mm(dyb, p["wo"], tb=True, name=f"{tag}_d_og")
    dwo = _mm(og, dyb, ta=True, out_dtype=F32, name=f"{tag}_d_wo")
    do, dgate, dogain = _gdn_out_bwd(o, proj, 2, p["o_gain"], dog, f"{tag}_onorm_bwd")
    du, dw, dattn, dq_s, dk_s, dg_s = _delta_scan_bwd(u, wd, attn, q, k, gcb, states, do, f"{tag}_delta_scan_bwd")
    dq, dk, dv, dgcb, dbtb = _delta_prep_bwd(q, k, v, gcb, btb, tinv, du, dw, dattn, dq_s, dk_s, dg_s, f"{tag}_delta_prep_bwd")
    dab, dalog, ddt = _gates_bwd(ab, p["a_log"], p["dt_bias"], dgcb, dbtb, hv, f"{tag}_gates_bwd")
    dpq, dcq = _gdn_conv_bwd(proj, p["cw"], dq, "q", 0, nk, f"{tag}_conv_q_bwd")
    dpk, dck = _gdn_conv_bwd(proj, p["cw"], dk, "k", nk, nk, f"{tag}_conv_k_bwd")
    dpv, dcv = _gdn_conv_bwd(proj, p["cw"], dv, "v", 2 * nk, hv, f"{tag}_conv_v_bwd")
    dproj = jnp.concatenate([dpq, dpk, dpv, dgate], axis=1)
    dcw = jnp.concatenate([dcq, dck, dcv], axis=1)
    dwmain = _mm(ht, dproj, out_dtype=F32, name=f"{tag}_d_wmain", tk=DEEP_K)
    dwab = _mm(ht, dab, out_dtype=F32, name=f"{tag}_d_wab")
    grads = dict(wmain=dwmain, wab=dwab, cw=dcw, a_log=dalog, dt_bias=ddt, o_gain=dogain, wo=dwo)
    zero = early(grads) if early is not None else 0.0
    dh = _mm(dproj, p["wmain"], tb=True, out_dtype=F32, name=f"{tag}_d_h_main", **WIDE_K)
    dh = _mm(dab, p["wab"], tb=True, res=dh, out_dtype=F32, name=f"{tag}_d_h_ab")
    dx, dxb, dgain = _rmsnorm_bwd(x, gain + zero, dh, dy, f"{tag}_norm_bwd")
    return (dx, dxb), dict(grads, gain=dgain)


def _resolve(entry, after):
    return entry(after) if callable(entry) else entry


def _local_step(x, target, w, sink=None):
    depth = len(w["ffn"])
    tape = []
    for i in range(depth):
        kind, j = i % 3, i // 3
        gain = w["mix_norm"][i:i + 1]
        if kind == 0:
            x, saved = _att_forward(x, gain, _resolve(w["att"][j], x), f"l{i}_att")
        elif kind == 1:
            x_in = x
            pw = _resolve(w["pool"][j], x)
            x = _pool_fwd(x_in, gain, pw["w"], pw["scale"], f"l{i}_pool")
            saved = x_in
        else:
            x, saved = _gdn_forward(x, gain, _resolve(w["gdn"][j], x), f"l{i}_gdn")
        f = _resolve(w["ffn"][i], x)
        x, fsaved = _ffn_forward(x, w["ffn_norm"][i:i + 1], f["wu"], f["wg"], f["cw"], f["wd"], f"l{i}_ffn")
        tape.append((saved, fsaved))
    dy, dyb, loss_row = _loss_head(x, target, "loss_head")
    dy = (dy, dyb)
    grads = dict(mix=[None] * depth, ffn=[None] * depth)
    zero = 0.0
    for i in reversed(range(depth)):
        kind, j = i % 3, i // 3
        saved, fsaved = tape[i]
        early = (lambda name, layer: functools.partial(sink, name, layer)) if sink is not None else (lambda name, layer: None)
        f = _resolve(w["ffn"][i], dy[0])
        dy, grads["ffn"][i] = _ffn_backward(fsaved, dy, w["ffn_norm"][i:i + 1] + zero, f["wu"], f["wg"], f["cw"], f["wd"],
                                            f"l{i}_ffn", early("ffn", i))
        gain = w["mix_norm"][i:i + 1] + zero
        if kind == 0:
            dy, grads["mix"][i] = _att_backward(saved, dy, gain, _resolve(w["att"][j], dy[0]), f"l{i}_att", early("att", i))
        elif kind == 1:
            pw = _resolve(w["pool"][j], dy[0])
            dx, dgain, dw4, dscale = _pool_bwd(saved, gain, pw["w"], pw["scale"], dy[0], f"l{i}_pool_bwd")
            dy = (dx, dx.astype(BF))
            grads["mix"][i] = dict(gain=dgain, w=dw4, scale=dscale)
            if sink is not None:
                zero = zero + sink("pool", i, grads["mix"][i])
        else:
            dy, grads["mix"][i] = _gdn_backward(saved, dy, gain, _resolve(w["gdn"][j], dy[0]), f"l{i}_gdn", early("gdn", i))
    return loss_row, dy[0], grads


MESH = pl.DeviceIdType.MESH
ANY = pl.BlockSpec(memory_space=pl.ANY)


def _position():
    return tuple(lax.axis_index(a) for a in AXES)


def _flip(pos, rel):
    return tuple(1 - p if (rel >> (2 - i)) & 1 else p for i, p in enumerate(pos))


def _index(pos):
    return 4 * pos[0] + 2 * pos[1] + pos[2]


def _all_gather(arr, name):
    def body(x_ref, o_ref, send, recv, local):
        me = _position()
        mine = pltpu.make_async_copy(x_ref, o_ref.at[_index(me)], local)
        mine.start()
        copies = []
        for rel in range(1, N_DEV):
            cp = pltpu.make_async_remote_copy(
                src_ref=x_ref, dst_ref=o_ref.at[_index(me)], send_sem=send.at[rel - 1], recv_sem=recv.at[rel - 1],
                device_id=_flip(me, rel), device_id_type=MESH)
            cp.start()
            copies.append(cp)
        for cp in copies:
            cp.wait()
        mine.wait()

    return pl.pallas_call(
        body, name=name, in_specs=[ANY], out_specs=ANY, out_shape=SDS((N_DEV,) + arr.shape, arr.dtype),
        scratch_shapes=[pltpu.SemaphoreType.DMA((N_DEV - 1,)), pltpu.SemaphoreType.DMA((N_DEV - 1,)),
                        pltpu.SemaphoreType.DMA(())])(arr)


def _exchange(arr, name):
    def body(x_ref, o_ref, send, recv, local):
        me = _position()
        mine = pltpu.make_async_copy(x_ref.at[_index(me)], o_ref.at[_index(me)], local)
        mine.start()
        copies = []
        for rel in range(1, N_DEV):
            peer = _flip(me, rel)
            cp = pltpu.make_async_remote_copy(
                src_ref=x_ref.at[_index(peer)], dst_ref=o_ref.at[_index(me)], send_sem=send.at[rel - 1],
                recv_sem=recv.at[rel - 1], device_id=peer, device_id_type=MESH)
            cp.start()
            copies.append(cp)
        for cp in copies:
            cp.wait()
        mine.wait()

    return pl.pallas_call(
        body, name=name, in_specs=[ANY], out_specs=ANY, out_shape=SDS(arr.shape, arr.dtype),
        scratch_shapes=[pltpu.SemaphoreType.DMA((N_DEV - 1,)), pltpu.SemaphoreType.DMA((N_DEV - 1,)),
                        pltpu.SemaphoreType.DMA(())])(arr)


HBM = pl.BlockSpec(memory_space=pltpu.HBM)
SEM = pl.BlockSpec(memory_space=pltpu.SEMAPHORE)
EFFECT = pltpu.SideEffectType.DATAFLOW_SIDE_EFFECTING


def _split_copies(x_ref, land_ref, send, recv, scatter):
    me = _position()
    copies = []
    for rel in range(1, N_DEV):
        peer = _flip(me, rel)
        copies.append(pltpu.make_async_remote_copy(
            src_ref=x_ref.at[_index(peer)] if scatter else x_ref, dst_ref=land_ref.at[_index(me)],
            send_sem=send.at[rel - 1], recv_sem=recv.at[rel - 1], device_id=peer, device_id_type=MESH))
    return copies


def _copies_start(arr, scatter, name):
    shape = arr.shape if scatter else (N_DEV,) + arr.shape

    def body(x_ref, land_ref, send, recv, x_thru, land_thru, token):
        for cp in _split_copies(x_ref, land_ref, send, recv, scatter):
            cp.start()
        token[...] = jnp.zeros_like(token)

    sems = pltpu.SemaphoreType.DMA((N_DEV - 1,))
    send, recv, x_thru, land_thru, token = pl.pallas_call(
        body, name=name,
        out_shape=(sems, sems, pltpu.HBM(arr.shape, arr.dtype), pltpu.HBM(shape, arr.dtype), SDS((8, LANE), F32)),
        in_specs=(HBM, HBM), out_specs=(SEM, SEM, HBM, HBM, pl.BlockSpec(memory_space=pltpu.VMEM)),
        input_output_aliases={0: 2, 1: 3}, compiler_params=pltpu.CompilerParams(has_side_effects=EFFECT),
    )(pltpu.with_memory_space_constraint(arr, pltpu.HBM), pltpu.with_memory_space_constraint(lax.empty(shape, arr.dtype), pltpu.HBM))
    return (send, recv, x_thru, land_thru, scatter, name), token[0, 0]


def _copies_wait(handle, after):
    send, recv, x_thru, land_thru, scatter, name = handle

    def body(x_ref, land_ref, send_ref, recv_ref, after_ref, x_dead, got_ref):
        for cp in _split_copies(x_ref, land_ref, send_ref, recv_ref, scatter):
            cp.wait_send()
            cp.wait_recv()

    return pl.pallas_call(
        body, name=name + "_wait",
        out_shape=(pltpu.HBM(x_thru.shape, x_thru.dtype), pltpu.HBM(land_thru.shape, land_thru.dtype)),
        in_specs=(HBM, HBM, SEM, SEM, ANY), out_specs=(HBM, HBM), input_output_aliases={0: 0, 1: 1},
        compiler_params=pltpu.CompilerParams(has_side_effects=EFFECT),
    )(x_thru, land_thru, send, recv, after)[1]


def _with_own(got, own):
    return lax.dynamic_update_index_in_dim(got, own.astype(got.dtype), _index(_position()), 0)


def _adamw(parts, w, m, v, name):
    if not isinstance(parts, (list, tuple)):
        parts = [parts]
    layers = len(parts)
    r, c = parts[0].shape[1:]
    assert w.shape == (layers * r, c), (w.shape, parts[0].shape, layers)
    tr = _tile(r, 128, 16)
    per = r // tr
    c1 = 1.0 / (1.0 - ADAM_B1 ** ADAM_STEP)
    c2 = 1.0 / (1.0 - ADAM_B2 ** ADAM_STEP)

    def body(*refs):
        p_refs = refs[:layers]
        w_ref, m_ref, v_ref, g_ref, d_ref, nm_ref, nv_ref = refs[layers:]

        def update(p_ref):
            g = p_ref[0].astype(F32)
            for s in range(1, N_DEV):
                g = g + p_ref[s].astype(F32)
            nm = ADAM_B1 * m_ref[...] + (1.0 - ADAM_B1) * g
            nv = ADAM_B2 * v_ref[...] + (1.0 - ADAM_B2) * (g * g)
            g_ref[...] = g
            nm_ref[...] = nm
            nv_ref[...] = nv
            d_ref[...] = -ADAM_LR * ((nm * c1) / (jnp.sqrt(nv * c2) + ADAM_EPS) + ADAM_WD * w_ref[...])

        if layers == 1:
            update(p_refs[0])
        else:
            for j in range(layers):
                pl.when(pl.program_id(0) == j)(functools.partial(update, p_refs[j]))

    p_specs = [pl.BlockSpec((N_DEV, tr, c), functools.partial(lambda l, i, j: (0, jnp.where(l == j, i, 0), 0), j=j))
               for j in range(layers)]
    row = pl.BlockSpec((tr, c), lambda l, i: (l * per + i, 0))
    return pl.pallas_call(
        body, name=name, grid=(layers, per), in_specs=p_specs + [row, row, row],
        out_specs=[row] * 4, out_shape=[SDS(w.shape, F32)] * 4, compiler_params=_params("arbitrary", "arbitrary"))(*parts, w, m, v)


PACK = 8 * LANE


def _pack(arrs):
    flat = []
    for a in arrs:
        a = a.reshape(-1).astype(F32)
        flat.append(jnp.pad(a, (0, (-a.shape[0]) % PACK)))
    return jnp.concatenate(flat).reshape(-1, LANE)


def _unpack(packed, shapes, lead=()):
    flat = packed.reshape(lead + (-1,))
    out, off = [], 0
    for shp in shapes:
        n = int(np.prod(shp))
        out.append(flat[..., off:off + n].reshape(lead + tuple(shp)))
        off += n + (-n) % PACK
    return out


def _pad_to(a, axis, size):
    pad = [(0, 0)] * a.ndim
    pad[axis] = (0, size - a.shape[axis])
    return jnp.pad(a, pad)


def _cols_from_shards(g):
    return jnp.transpose(g, (1, 0, 2)).reshape(g.shape[1], -1)


def _cols_to_shards(a):
    c = a.shape[-1] // N_DEV
    a = a.reshape(a.shape[:-1] + (N_DEV, c))
    return jnp.moveaxis(a, -2, 0)


def _rows_to_shards(a):
    r = a.shape[-2] // N_DEV
    a = a.reshape(a.shape[:-2] + (N_DEV, r, a.shape[-1]))
    return jnp.moveaxis(a, -3, 0)


WEIGHTS = ("mix_norm", "ffn_norm", "att_w_qkv", "att_q_gain", "att_k_gain", "att_rel_bias", "att_w_o", "pool_w",
           "pool_scale", "gdn_w_in", "gdn_conv", "gdn_a_log", "gdn_dt_bias", "gdn_o_gain", "gdn_w_o", "ffn_w_up",
           "ffn_conv", "ffn_w_down")
REPLICATED = ("mix_norm", "ffn_norm", "att_q_gain", "att_k_gain", "pool_scale", "gdn_a_log", "gdn_dt_bias", "gdn_o_gain")
SMALL_SHARDED = ("att_rel_bias", "gdn_conv", "ffn_conv")
BIG = ("att_w_qkv", "att_w_o", "pool_w", "gdn_w_in", "gdn_w_o", "ffn_w_up", "ffn_w_down")
KEEP_F32 = ("pool_w",)


def _memo(build):
    cache = []

    def entry(after):
        if not cache:
            cache.append(build(after))
        return cache[0]

    return entry


def _assemble_weights(w, get, small, f):
    d = w["mix_norm"].shape[1]
    nk = d // HEAD
    hv = 2 * nk
    fp = -(-f // FF_PAD) * FF_PAD
    out = dict(mix_norm=w["mix_norm"], ffn_norm=w["ffn_norm"], att=[], pool=[], gdn=[], ffn=[])

    def att(j, after):
        return dict(wqkv=_cols_from_shards(get("att_w_qkv", j, after)), wo=get("att_w_o", j, after).reshape(d, d),
                    qg=w["att_q_gain"][j:j + 1], kg=w["att_k_gain"][j:j + 1], rel=small["att_rel_bias"][j])

    def pool(j, after):
        g = get("pool_w", j, after)
        return dict(w=jnp.transpose(g, (1, 0, 2, 3)).reshape(g.shape[1], g.shape[3], g.shape[3]),
                    scale=w["pool_scale"][j:j + 1])

    def gdn(j, after):
        win = _cols_from_shards(get("gdn_w_in", j, after))
        nm = 6 * d
        wab = jnp.concatenate([_pad_to(win[:, nm:nm + hv], 1, LANE), _pad_to(win[:, nm + hv:], 1, LANE)], axis=1)
        return dict(wmain=win[:, :nm], wab=wab, cw=_pad_to(small["gdn_conv"][j], 0, 8),
                    a_log=_pad_to(w["gdn_a_log"][j:j + 1], 1, LANE), dt_bias=_pad_to(w["gdn_dt_bias"][j:j + 1], 1, LANE),
                    o_gain=w["gdn_o_gain"][j:j + 1], wo=get("gdn_w_o", j, after).reshape(2 * d, d))

    def ffn(i, after):
        wup = _cols_from_shards(get("ffn_w_up", i, after))
        return dict(wu=_pad_to(wup[:, :f], 1, fp), wg=_pad_to(wup[:, f:], 1, fp),
                    cw=_pad_to(_pad_to(small["ffn_conv"][i], 0, 8), 1, fp),
                    wd=_pad_to(get("ffn_w_down", i, after).reshape(f, d), 0, fp))

    for key, build, count in (("att", att, w["att_w_qkv"].shape[0]), ("pool", pool, w["pool_w"].shape[0]),
                              ("gdn", gdn, w["gdn_w_in"].shape[0]), ("ffn", ffn, w["ffn_w_up"].shape[0])):
        out[key] = [_memo(functools.partial(build, j)) for j in range(count)]
    return out


def _full_gradients(grads, w, f):
    d = w["mix_norm"].shape[1]
    nk = d // HEAD
    hv = 2 * nk
    depth = len(grads["ffn"])
    att = [grads["mix"][i] for i in range(depth) if i % 3 == 0]
    pool = [grads["mix"][i] for i in range(depth) if i % 3 == 1]
    gdn = [grads["mix"][i] for i in range(depth) if i % 3 == 2]
    ffn = grads["ffn"]
    win = [jnp.concatenate([g["wmain"], g["wab"][:, :hv], g["wab"][:, LANE:LANE + hv]], axis=1) for g in gdn]
    return dict(
        mix_norm=jnp.concatenate([g["gain"] for g in grads["mix"]], axis=0),
        ffn_norm=jnp.concatenate([g["gain"] for g in ffn], axis=0),
        att_w_qkv=jnp.stack([g["wqkv"] for g in att]),
        att_q_gain=jnp.concatenate([g["qg"] for g in att], axis=0),
        att_k_gain=jnp.concatenate([g["kg"] for g in att], axis=0),
        att_rel_bias=jnp.stack([g["rel"] for g in att]),
        att_w_o=jnp.stack([g["wo"] for g in att]),
        pool_w=jnp.stack([g["w"] for g in pool]),
        pool_scale=jnp.concatenate([g["scale"] for g in pool], axis=0),
        gdn_w_in=jnp.stack(win),
        gdn_conv=jnp.stack([g["cw"][:GDN_CONV] for g in gdn]),
        gdn_a_log=jnp.concatenate([g["a_log"][:, :hv] for g in gdn], axis=0),
        gdn_dt_bias=jnp.concatenate([g["dt_bias"][:, :hv] for g in gdn], axis=0),
        gdn_o_gain=jnp.concatenate([g["o_gain"] for g in gdn], axis=0),
        gdn_w_o=jnp.stack([g["wo"] for g in gdn]),
        ffn_w_up=jnp.stack([jnp.concatenate([g["wu"][:, :f], g["wg"][:, :f]], axis=1) for g in ffn]),
        ffn_conv=jnp.stack([g["cw"][:FFN_CONV, :f] for g in ffn]),
        ffn_w_down=jnp.stack([g["wd"][:f] for g in ffn]))


ROW_SHARDED = ("att_w_o", "gdn_w_o", "ffn_w_down")


def _to_shards(name, full):
    if name == "pool_w":
        r = full.shape[2] // N_DEV
        a = full.reshape(full.shape[:2] + (N_DEV, r, full.shape[3]))
        return jnp.moveaxis(a, 2, 0)
    return _rows_to_shards(full) if name in ROW_SHARDED else _cols_to_shards(full)


def kernel(x, mix_norm, ffn_norm, att_w_qkv, att_q_gain, att_k_gain, att_rel_bias, att_w_o, pool_w, pool_scale, gdn_w_in, gdn_conv, gdn_a_log, gdn_dt_bias, gdn_o_gain, gdn_w_o, ffn_w_up, ffn_conv, ffn_w_down, loss_target, m_mix_norm, m_ffn_norm, m_att_w_qkv, m_att_q_gain, m_att_k_gain, m_att_rel_bias, m_att_w_o, m_pool_w, m_pool_scale, m_gdn_w_in, m_gdn_conv, m_gdn_a_log, m_gdn_dt_bias, m_gdn_o_gain, m_gdn_w_o, m_ffn_w_up, m_ffn_conv, m_ffn_w_down, v_mix_norm, v_ffn_norm, v_att_w_qkv, v_att_q_gain, v_att_k_gain, v_att_rel_bias, v_att_w_o, v_pool_w, v_pool_scale, v_gdn_w_in, v_gdn_conv, v_gdn_a_log, v_gdn_dt_bias, v_gdn_o_gain, v_gdn_w_o, v_ffn_w_up, v_ffn_conv, v_ffn_w_down):
    w = dict(zip(WEIGHTS, (mix_norm, ffn_norm, att_w_qkv, att_q_gain, att_k_gain, att_rel_bias, att_w_o, pool_w, pool_scale, gdn_w_in, gdn_conv, gdn_a_log, gdn_dt_bias, gdn_o_gain, gdn_w_o, ffn_w_up, ffn_conv, ffn_w_down)))
    m = dict(zip(WEIGHTS, (m_mix_norm, m_ffn_norm, m_att_w_qkv, m_att_q_gain, m_att_k_gain, m_att_rel_bias, m_att_w_o, m_pool_w, m_pool_scale, m_gdn_w_in, m_gdn_conv, m_gdn_a_log, m_gdn_dt_bias, m_gdn_o_gain, m_gdn_w_o, m_ffn_w_up, m_ffn_conv, m_ffn_w_down)))
    v = dict(zip(WEIGHTS, (v_mix_norm, v_ffn_norm, v_att_w_qkv, v_att_q_gain, v_att_k_gain, v_att_rel_bias, v_att_w_o, v_pool_w, v_pool_scale, v_gdn_w_in, v_gdn_conv, v_gdn_a_log, v_gdn_dt_bias, v_gdn_o_gain, v_gdn_w_o, v_ffn_w_up, v_ffn_conv, v_ffn_w_down)))

    me = _index(_position())
    d = mix_norm.shape[1]
    hv = 2 * (d // HEAD)
    f = ffn_w_down.shape[1] * N_DEV
    depth = ffn_w_up.shape[0]

    small_shapes = [w[n].shape for n in SMALL_SHARDED]
    small_g = _all_gather(_pack([w[n] for n in SMALL_SHARDED]), "gather_small")
    small = {}
    for n, a in zip(SMALL_SHARDED, _unpack(small_g, small_shapes, lead=(N_DEV,))):
        small[n] = jnp.moveaxis(a, 0, -2).reshape(a.shape[1:-1] + (N_DEV * a.shape[-1],))
    order = []
    for i in range(depth):
        order += [[("att_w_qkv", i // 3), ("att_w_o", i // 3)], [("pool_w", i // 3)], [("gdn_w_in", i // 3), ("gdn_w_o", i // 3)]][i % 3]
        order += [("ffn_w_up", i), ("ffn_w_down", i)]
    after_small = small_g[0, 0, 0] * 0.0
    local = {(n, j): (w[n][j] + after_small if n in KEEP_F32 else (w[n][j] + after_small).astype(BF)) for n, j in order}
    arriving, zero = {}, 0.0
    for n, j in order:
        arriving[(n, j)], tok = _copies_start(local[(n, j)], False, f"gather_{n}_{j}")
        zero = zero + tok

    def get(n, j, after):
        return _with_own(_copies_wait(arriving[(n, j)], after), local[(n, j)])

    ordered = dict(w, mix_norm=mix_norm + zero, ffn_norm=ffn_norm + zero)
    full = _assemble_weights(ordered, get, small, f)

    leaving = {}

    def sink(kind, i, g):
        j = i // 3
        if kind == "ffn":
            pieces = [("ffn_w_up", i, _cols_to_shards(jnp.concatenate([g["wu"][:, :f], g["wg"][:, :f]], axis=1))),
                      ("ffn_w_down", i, _rows_to_shards(g["wd"][:f]))]
        elif kind == "att":
            pieces = [("att_w_qkv", j, _cols_to_shards(g["wqkv"])), ("att_w_o", j, _rows_to_shards(g["wo"]))]
        elif kind == "pool":
            pieces = [("pool_w", j, _to_shards("pool_w", g["w"][None])[:, 0])]
        else:
            win = jnp.concatenate([g["wmain"], g["wab"][:, :hv], g["wab"][:, LANE:LANE + hv]], axis=1)
            pieces = [("gdn_w_in", j, _cols_to_shards(win)), ("gdn_w_o", j, _rows_to_shards(g["wo"]))]
        tok = 0.0
        for n, l, shards in pieces:
            shards = shards if n in KEEP_F32 else shards.astype(BF)
            handle, t = _copies_start(shards, True, f"exchange_{n}_{l}")
            leaving[(n, l)] = (handle, lax.dynamic_index_in_dim(shards, me, 0, keepdims=False))
            tok = tok + t
        return tok

    loss_row, dx, grads = _local_step(x[0], loss_target[0], full, sink)
    gfull = _full_gradients(grads, w, f)

    out = {}
    for n in BIG:
        c = w[n].shape[-1]
        parts = []
        for l in range(w[n].shape[0]):
            handle, own = leaving[(n, l)]
            parts.append(_with_own(_copies_wait(handle, dx), own).reshape(N_DEV, -1, c))
        res = _adamw(parts, w[n].reshape(-1, c), m[n].reshape(-1, c), v[n].reshape(-1, c), f"adamw_{n}")
        out[n] = [a.reshape(w[n].shape) for a in res]
    sparts = _exchange(jnp.stack([_pack([_to_shards(n, gfull[n])[k] for n in SMALL_SHARDED]) for k in range(N_DEV)]),
                       "exchange_small")
    res = _adamw(sparts, *[_pack([t[n] for n in SMALL_SHARDED]) for t in (w, m, v)], "adamw_small")
    for n, *vals in zip(SMALL_SHARDED, *[_unpack(a, small_shapes) for a in res]):
        out[n] = vals
    rep_shapes = [w[n].shape for n in REPLICATED]
    rparts = _all_gather(_pack([gfull[n] for n in REPLICATED] + [loss_row[:, 0:1]]), "gather_replicated_grads")
    pad = jnp.zeros((1, 1), F32)
    res = _adamw(rparts, *[_pack([t[n] for n in REPLICATED] + [pad]) for t in (w, m, v)], "adamw_replicated")
    for n, *vals in zip(REPLICATED, *[_unpack(a, rep_shapes) for a in res]):
        out[n] = vals
    loss = jnp.sum(_unpack(rparts, rep_shapes + [(1, 1)], lead=(N_DEV,))[-1])
    return (loss, dx[None], *[out[n][0] for n in WEIGHTS], *[out[n][1] for n in WEIGHTS],
            *[out[n][2] for n in WEIGHTS], *[out[n][3] for n in WEIGHTS])
```

```python
import functools

import numpy as np
import jax
import jax.numpy as jnp
from jax import lax
from jax.experimental import pallas as pl
from jax.experimental.pallas import tpu as pltpu

F32 = jnp.float32
BF = jnp.bfloat16
SDS = jax.ShapeDtypeStruct

EPS = 1e-6
MASK_VALUE = -1e30
CHUNK = 64
LEFT_CHUNKS = 8
BAND_LEFT = LEFT_CHUNKS * CHUNK
BAND = BAND_LEFT + CHUNK
MAX_REL = 256
NUM_REL = (CHUNK - 1) + MAX_REL + 1
HEAD = 128
LANE = 128
HALO = 16
POOL_WINDOWS = (2, 4, 8, 16)
GDN_CONV = 4
FFN_CONV = 3
FF_PAD = 512
N_DEV = 8
AXES = ("x", "y", "c")
VMEM_LIMIT = 56 * 1024 * 1024

ADAM_LR = 0.001
ADAM_B1 = 0.9
ADAM_B2 = 0.999
ADAM_EPS = 1e-08
ADAM_WD = 0.01
ADAM_STEP = 10

HI = lax.Precision.HIGHEST
NT = (((1,), (1,)), ((), ()))
TN = (((0,), (0,)), ((), ()))


def _tile(n, target, mult=LANE):
    if n <= target:
        return n
    t = (target // mult) * mult
    while t >= mult:
        if n % t == 0:
            return t
        t -= mult
    return n


def _params(*sem):
    return pltpu.CompilerParams(dimension_semantics=sem, vmem_limit_bytes=VMEM_LIMIT)


def _silu(x):
    return x / (1.0 + jnp.exp(-x))


@functools.partial(jax.custom_vjp, nondiff_argnums=(1,))
def _shift(x, k):
    return pltpu.roll(x, k % x.shape[0], axis=0)


def _shift_fwd(x, k):
    return _shift(x, k), None


def _shift_bwd(k, _, g):
    return (pltpu.roll(g, (-k) % g.shape[0], axis=0),)


_shift.defvjp(_shift_fwd, _shift_bwd)


def _mm(a, b, *, name, ta=False, tb=False, out_dtype=BF, res=None, tm=1024, tn=1024, tk=2048, n_outer=False):
    m, k = (a.shape[1], a.shape[0]) if ta else a.shape
    n, kb = (b.shape[0], b.shape[1]) if tb else (b.shape[1], b.shape[0])
    assert k == kb, (a.shape, b.shape, ta, tb)
    if ta or a.dtype != BF:
        tk = min(tk, 1024)
    tm, tn, tk = _tile(m, tm), _tile(n, tn), _tile(k, tk)
    nk = k // tk
    dims = (((0 if ta else 1,), (1 if tb else 0,)), ((), ()))

    def body(*refs):
        if res is None:
            a_ref, b_ref, o_ref, acc = refs
        else:
            a_ref, b_ref, r_ref, o_ref, acc = refs
        prod = lax.dot_general(a_ref[...].astype(BF), b_ref[...].astype(BF), dims, preferred_element_type=F32)
        if nk == 1:
            if res is not None:
                prod = prod + r_ref[...].astype(F32)
            o_ref[...] = prod.astype(out_dtype)
            return
        kk = pl.program_id(2)

        @pl.when(kk == 0)
        def _():
            acc[...] = prod

        @pl.when(kk > 0)
        def _():
            acc[...] += prod

        @pl.when(kk == nk - 1)
        def _():
            r = acc[...]
            if res is not None:
                r = r + r_ref[...].astype(F32)
            o_ref[...] = r.astype(out_dtype)

    def order(fn):
        return (lambda j, i, q: fn(i, j, q)) if n_outer else fn

    a_spec = pl.BlockSpec((tk, tm), order(lambda i, j, q: (q, i))) if ta else pl.BlockSpec((tm, tk), order(lambda i, j, q: (i, q)))
    b_spec = pl.BlockSpec((tn, tk), order(lambda i, j, q: (j, q))) if tb else pl.BlockSpec((tk, tn), order(lambda i, j, q: (q, j)))
    o_spec = pl.BlockSpec((tm, tn), order(lambda i, j, q: (i, j)))
    ins, specs = [a, b], [a_spec, b_spec]
    if res is not None:
        ins.append(res)
        specs.append(o_spec)
    return pl.pallas_call(
        body, name=name, grid=(n // tn, m // tm, nk) if n_outer else (m // tm, n // tn, nk), in_specs=specs, out_specs=o_spec,
        out_shape=SDS((m, n), out_dtype), scratch_shapes=[pltpu.VMEM((tm, tn), F32)],
        compiler_params=_params("parallel", "parallel", "arbitrary"))(*ins)


def _rms(x, gain):
    return x * lax.rsqrt(jnp.mean(x * x, axis=-1, keepdims=True) + EPS) * gain


def _rmsnorm_fwd(x, gain, name):
    s, d = x.shape
    ts = _tile(s, 256, LANE)

    def body(x_ref, g_ref, o_ref, t_ref):
        h = _rms(x_ref[...], g_ref[...])
        o_ref[...] = h.astype(BF)
        t_ref[...] = h.T.astype(BF)

    return pl.pallas_call(
        body, name=name, grid=(s // ts,),
        in_specs=[pl.BlockSpec((ts, d), lambda i: (i, 0)), pl.BlockSpec((1, d), lambda i: (0, 0))],
        out_specs=[pl.BlockSpec((ts, d), lambda i: (i, 0)), pl.BlockSpec((d, ts), lambda i: (0, i))],
        out_shape=[SDS((s, d), BF), SDS((d, s), BF)], compiler_params=_params("parallel"))(x, gain)


def _rmsnorm_bwd(x, gain, dh, dres, name):
    s, d = x.shape
    ts = _tile(s, 256, 16)

    def body(x_ref, g_ref, dh_ref, dr_ref, dx_ref, dxb_ref, dg_ref):
        i = pl.program_id(0)
        _, vjp = jax.vjp(_rms, x_ref[...], g_ref[...])
        dx, dg = vjp(dh_ref[...].astype(F32))
        dx = dr_ref[...] + dx
        dx_ref[...] = dx
        dxb_ref[...] = dx.astype(BF)

        @pl.when(i == 0)
        def _():
            dg_ref[...] = dg

        @pl.when(i > 0)
        def _():
            dg_ref[...] += dg

    row = pl.BlockSpec((ts, d), lambda i: (i, 0))
    vec = pl.BlockSpec((1, d), lambda i: (0, 0))
    return pl.pallas_call(
        body, name=name, grid=(s // ts,), in_specs=[row, vec, row, row], out_specs=[row, row, vec],
        out_shape=[SDS((s, d), F32), SDS((s, d), BF), SDS((1, d), F32)], compiler_params=_params("arbitrary"))(x, gain, dh, dres)


def _ffn_act_tile(u_ext, g_ext, cw):
    acc = u_ext * cw[2:3] + _shift(u_ext, 1) * cw[1:2] + _shift(u_ext, 2) * cw[0:1]
    return _silu(acc) * g_ext


def _halo_index(rows_per_block):
    per = rows_per_block // HALO
    return lambda rb: jnp.maximum(rb * per - 1, 0)


FFN_SUB = 256


def _ext_rows(cur, prev, nxt, r0, rows, tile, keep_prev=1.0, keep_next=1.0):
    n = cur.shape[0]
    parts = []
    if r0 > 0:
        parts.append(cur[pl.ds(r0 - HALO, HALO + rows), tile].astype(F32))
    else:
        head = jnp.zeros((HALO, LANE), F32) if prev is None else prev[:, tile].astype(F32) * keep_prev
        parts += [head, cur[pl.ds(0, rows), tile].astype(F32)]
    if nxt is not False:
        if r0 + rows < n:
            parts.append(cur[pl.ds(r0 + rows, HALO), tile].astype(F32))
        else:
            parts.append(jnp.zeros((HALO, LANE), F32) if nxt is None else nxt[:, tile].astype(F32) * keep_next)
    return jnp.concatenate(parts, axis=0)


def _ffn_act_fwd(u, g, cw, name):
    s, f = u.shape
    r, tc = _tile(s, 512, HALO), _tile(f, 512)
    sub = _tile(r, FFN_SUB, HALO)
    hidx = _halo_index(r)

    def body(uc, uh, gc, cw_ref, o_ref, t_ref):
        keep = jnp.where(pl.program_id(1) == 0, 0.0, 1.0)
        for lt in range(tc // LANE):
            tile = pl.ds(lt * LANE, LANE)
            c0, c1, c2 = cw_ref[0:1, tile], cw_ref[1:2, tile], cw_ref[2:3, tile]
            for r0 in range(0, r, sub):
                u_ext = _ext_rows(uc, uh, False, r0, sub, tile, keep)
                acc = u_ext * c2 + pltpu.roll(u_ext, 1, axis=0) * c1 + pltpu.roll(u_ext, 2, axis=0) * c0
                a = _silu(acc)[HALO:] * gc[pl.ds(r0, sub), tile].astype(F32)
                o_ref[pl.ds(r0, sub), tile] = a.astype(BF)
                t_ref[tile, pl.ds(r0, sub)] = a.T.astype(BF)

    cur = pl.BlockSpec((r, tc), lambda j, rb: (rb, j))
    return pl.pallas_call(
        body, name=name, grid=(f // tc, s // r),
        in_specs=[cur, pl.BlockSpec((HALO, tc), lambda j, rb: (hidx(rb), j)), cur,
                  pl.BlockSpec((8, tc), lambda j, rb: (0, j))],
        out_specs=[cur, pl.BlockSpec((tc, r), lambda j, rb: (j, rb))],
        out_shape=[SDS((s, f), BF), SDS((f, s), BF)], compiler_params=_params("parallel", "parallel"))(u, u, g, cw)


def _ffn_act_bwd(u, g, cw, da, name):
    s, f = u.shape
    r, tc = _tile(s, 512, HALO), _tile(f, 512)
    sub = _tile(r, FFN_SUB, HALO)
    nb = s // r
    per = r // HALO
    hidx = _halo_index(r)
    nidx = lambda rb: jnp.minimum((rb + 1) * per, s // HALO - 1)
    n = sub + 2 * HALO

    def body(uc, uh, un, gc, gn, dac, dan, cw_ref, du_ref, dg_ref, dcw_ref):
        rb = pl.program_id(1)
        first = jnp.where(rb == 0, 0.0, 1.0)
        last = jnp.where(rb == nb - 1, 0.0, 1.0)
        rows = lax.broadcasted_iota(jnp.int32, (n, LANE), 0)
        own = jnp.where((rows >= HALO) & (rows < HALO + sub), 1.0, 0.0)
        wrow = lax.broadcasted_iota(jnp.int32, (8, LANE), 0)

        @pl.when(rb == 0)
        def _():
            dcw_ref[...] = jnp.zeros_like(dcw_ref)

        for lt in range(tc // LANE):
            tile = pl.ds(lt * LANE, LANE)
            c0, c1, c2 = cw_ref[0:1, tile], cw_ref[1:2, tile], cw_ref[2:3, tile]
            dcw = jnp.zeros((8, LANE), F32)
            for r0 in range(0, r, sub):
                u_ext = _ext_rows(uc, uh, un, r0, sub, tile, first, last)
                g_ext = _ext_rows(gc, None, gn, r0, sub, tile, 1.0, last)
                da_ext = _ext_rows(dac, None, dan, r0, sub, tile, 1.0, last)
                sh1, sh2 = pltpu.roll(u_ext, 1, axis=0), pltpu.roll(u_ext, 2, axis=0)
                acc = u_ext * c2 + sh1 * c1 + sh2 * c0
                sg = 1.0 / (1.0 + jnp.exp(-acc))
                dg_ref[pl.ds(r0, sub), tile] = (da_ext * (acc * sg))[HALO:HALO + sub].astype(BF)
                dacc = da_ext * g_ext * (sg * (1.0 + acc * (1.0 - sg)))
                du = dacc * c2 + pltpu.roll(dacc, n - 1, axis=0) * c1 + pltpu.roll(dacc, n - 2, axis=0) * c0
                du_ref[pl.ds(r0, sub), tile] = du[HALO:HALO + sub].astype(BF)
                dm = dacc * own
                for j, tap in enumerate((sh2, sh1, u_ext)):
                    dcw = dcw + jnp.where(wrow == j, jnp.sum(dm * tap, axis=0, keepdims=True), 0.0)
            dcw_ref[:, tile] += dcw

    cur = pl.BlockSpec((r, tc), lambda j, rb: (rb, j))
    prev = pl.BlockSpec((HALO, tc), lambda j, rb: (hidx(rb), j))
    nxt = pl.BlockSpec((HALO, tc), lambda j, rb: (nidx(rb), j))
    wspec = pl.BlockSpec((8, tc), lambda j, rb: (0, j))
    return pl.pallas_call(
        body, name=name, grid=(f // tc, nb),
        in_specs=[cur, prev, nxt, cur, nxt, cur, nxt, wspec],
        out_specs=[cur, cur, wspec], out_shape=[SDS((s, f), BF), SDS((s, f), BF), SDS((8, f), F32)],
        compiler_params=_params("parallel", "arbitrary"))(u, u, u, g, g, da, da, cw)


def _loss_head(y, target, name):
    s, d = y.shape
    ts = _tile(s, 256, 16)

    def body(y_ref, t_ref, dy_ref, dyb_ref, l_ref):
        i = pl.program_id(0)
        err = y_ref[...] - t_ref[...]
        dy_ref[...] = err * (1.0 / d)
        dyb_ref[...] = (err * (1.0 / d)).astype(BF)
        part = jnp.zeros((1, LANE), F32) + 0.5 * jnp.sum(jnp.sum(err * err, axis=1, keepdims=True), axis=0, keepdims=True) / d

        @pl.when(i == 0)
        def _():
            l_ref[...] = part

        @pl.when(i > 0)
        def _():
            l_ref[...] += part

    row = pl.BlockSpec((ts, d), lambda i: (i, 0))
    return pl.pallas_call(
        body, name=name, grid=(s // ts,), in_specs=[row, row],
        out_specs=[row, row, pl.BlockSpec((1, LANE), lambda i: (0, 0))],
        out_shape=[SDS((s, d), F32), SDS((s, d), BF), SDS((1, LANE), F32)], compiler_params=_params("arbitrary"))(y, target)


def _rel_index():
    rel = BAND_LEFT + np.arange(CHUNK)[:, None] - np.arange(BAND)[None, :]
    return (np.clip(rel, -(CHUNK - 1), MAX_REL) + (CHUNK - 1)).reshape(1, CHUNK * BAND).astype(np.int32)


def _onehot(idx_row):
    rows = lax.broadcasted_iota(jnp.int32, (NUM_REL, idx_row.shape[1]), 0)
    return jnp.where(rows == idx_row, 1.0, 0.0).astype(F32)


def _bias_expand(rel_bias, name):
    h = rel_bias.shape[0]
    n = CHUNK * BAND
    tn = n // 8

    def body(rb_ref, idx_ref, o_ref):
        o_ref[...] = jnp.dot(rb_ref[...], _onehot(idx_ref[...]), precision=HI, preferred_element_type=F32)

    out = pl.pallas_call(
        body, name=name, grid=(n // tn,),
        in_specs=[pl.BlockSpec((h, NUM_REL), lambda j: (0, 0)), pl.BlockSpec((1, tn), lambda j: (0, j))],
        out_specs=pl.BlockSpec((h, tn), lambda j: (0, j)), out_shape=SDS((h, n), F32),
        compiler_params=_params("parallel"))(rel_bias, jnp.asarray(_rel_index()))
    return out.reshape(h, CHUNK, BAND)


def _bias_reduce(dbias, name):
    h = dbias.shape[0]
    n = CHUNK * BAND
    tn = n // 8

    def body(db_ref, idx_ref, o_ref):
        j = pl.program_id(0)
        part = lax.dot_general(db_ref[...], _onehot(idx_ref[...]), NT, precision=HI, preferred_element_type=F32)

        @pl.when(j == 0)
        def _():
            o_ref[...] = part

        @pl.when(j > 0)
        def _():
            o_ref[...] += part

    return pl.pallas_call(
        body, name=name, grid=(n // tn,),
        in_specs=[pl.BlockSpec((h, tn), lambda j: (0, j)), pl.BlockSpec((1, tn), lambda j: (0, j))],
        out_specs=pl.BlockSpec((h, NUM_REL), lambda j: (0, 0)), out_shape=SDS((h, NUM_REL), F32),
        compiler_params=_params("arbitrary"))(dbias.reshape(h, n), jnp.asarray(_rel_index()))


def _headnorm(x, gain):
    outs = []
    for hh in range(x.shape[1] // HEAD):
        xh = x[:, hh * HEAD:(hh + 1) * HEAD]
        outs.append(xh * lax.rsqrt(jnp.mean(xh * xh, axis=-1, keepdims=True) + EPS) * gain)
    return jnp.concatenate(outs, axis=1)


def _qkv_post_fwd(qkv, qg, kg, name):
    s, d3 = qkv.shape
    d = d3 // 3
    r = BAND_LEFT
    nb = s // r

    def body(x_ref, qg_ref, kg_ref, q_ref, k_ref, v_ref):
        i = pl.program_id(0)
        keep = jnp.where(i == 0, 0.0, 1.0)
        q_ref[...] = _headnorm(x_ref[:, 0:d].astype(F32), qg_ref[...]).astype(BF)
        k_ref[...] = (_headnorm(x_ref[:, d:2 * d].astype(F32), kg_ref[...]) * keep).astype(BF)
        v_ref[...] = (x_ref[:, 2 * d:].astype(F32) * keep).astype(BF)

    prev = lambda i: (jnp.maximum(i - 1, 0), 0)
    vec = pl.BlockSpec((1, HEAD), lambda i: (0, 0))
    return pl.pallas_call(
        body, name=name, grid=(nb + 1,),
        in_specs=[pl.BlockSpec((r, d3), prev), vec, vec],
        out_specs=[pl.BlockSpec((r, d), prev), pl.BlockSpec((r, d), lambda i: (i, 0)), pl.BlockSpec((r, d), lambda i: (i, 0))],
        out_shape=[SDS((s, d), BF), SDS((s + r, d), BF), SDS((s + r, d), BF)],
        compiler_params=_params("arbitrary"))(qkv, qg, kg)


def _qkv_post_bwd(qkv, qg, kg, dq, dkpad, dvpad, name):
    s, d3 = qkv.shape
    d = d3 // 3
    r = _tile(s, 256, 16)
    off = BAND_LEFT // r

    def body(x_ref, qg_ref, kg_ref, dq_ref, dk_ref, dv_ref, o_ref, dqg_ref, dkg_ref):
        i = pl.program_id(0)
        _, vq = jax.vjp(_headnorm, x_ref[:, 0:d].astype(F32), qg_ref[...])
        dxq, dqg = vq(dq_ref[...])
        _, vk = jax.vjp(_headnorm, x_ref[:, d:2 * d].astype(F32), kg_ref[...])
        dxk, dkg = vk(dk_ref[...])
        o_ref[:, 0:d] = dxq.astype(BF)
        o_ref[:, d:2 * d] = dxk.astype(BF)
        o_ref[:, 2 * d:] = dv_ref[...].astype(BF)

        @pl.when(i == 0)
        def _():
            dqg_ref[...] = dqg
            dkg_ref[...] = dkg

        @pl.when(i > 0)
        def _():
            dqg_ref[...] += dqg
            dkg_ref[...] += dkg

    vec = pl.BlockSpec((1, HEAD), lambda i: (0, 0))
    row3 = pl.BlockSpec((r, d3), lambda i: (i, 0))
    row = pl.BlockSpec((r, d), lambda i: (i, 0))
    padrow = pl.BlockSpec((r, d), lambda i: (i + off, 0))
    return pl.pallas_call(
        body, name=name, grid=(s // r,), in_specs=[row3, vec, vec, row, padrow, padrow],
        out_specs=[row3, vec, vec], out_shape=[SDS((s, d3), BF), SDS((1, HEAD), F32), SDS((1, HEAD), F32)],
        compiler_params=_params("arbitrary"))(qkv, qg, kg, dq, dkpad, dvpad)


ATT_QB = 512


def _att_probs(q, kw, bias, c0):
    sc = _bdot(q, kw, NTB) * (HEAD ** -0.5) + bias
    lane = lax.broadcasted_iota(jnp.int32, sc.shape, 2)
    chunk = lax.broadcasted_iota(jnp.int32, sc.shape, 0) + c0
    sc = jnp.where(lane + chunk * CHUNK >= BAND_LEFT, sc, MASK_VALUE)
    p = jnp.exp(sc - jnp.max(sc, axis=-1, keepdims=True))
    return p / jnp.sum(p, axis=-1, keepdims=True)


def _attn_fwd(q, kpad, vpad, bias, name):
    s, d = q.shape
    h = d // HEAD
    sp = kpad.shape[0]
    qb = _tile(s, ATT_QB, CHUNK)
    per = qb // CHUNK

    def body(q_ref, k_ref, v_ref, b_ref, o_ref):
        c0 = pl.program_id(1) * per
        wins = [pl.ds(pl.multiple_of((c0 + cc) * CHUNK, CHUNK), BAND) for cc in range(per)]
        kw = jnp.stack([k_ref[w, :] for w in wins])
        vw = jnp.stack([v_ref[w, :] for w in wins])
        p = _att_probs(q_ref[...].reshape(per, CHUNK, HEAD), kw, b_ref[0], c0)
        o_ref[...] = _bdot(p.astype(BF), vw, NNB).reshape(qb, HEAD).astype(BF)

    qspec = pl.BlockSpec((qb, HEAD), lambda hh, i: (i, hh))
    kspec = pl.BlockSpec((sp, HEAD), lambda hh, i: (0, hh))
    return pl.pallas_call(
        body, name=name, grid=(h, s // qb),
        in_specs=[qspec, kspec, kspec, pl.BlockSpec((1, CHUNK, BAND), lambda hh, i: (hh, 0, 0))],
        out_specs=qspec, out_shape=SDS((s, d), BF), compiler_params=_params("parallel", "arbitrary"))(q, kpad, vpad, bias)


def _attn_bwd(q, kpad, vpad, bias, do, name):
    s, d = q.shape
    h = d // HEAD
    sp = kpad.shape[0]
    qb = _tile(s, ATT_QB, CHUNK)
    per = qb // CHUNK
    scale = HEAD ** -0.5

    def body(q_ref, k_ref, v_ref, b_ref, do_ref, dq_ref, dk_ref, dv_ref, db_ref):
        i = pl.program_id(1)

        @pl.when(i == 0)
        def _():
            dk_ref[...] = jnp.zeros_like(dk_ref)
            dv_ref[...] = jnp.zeros_like(dv_ref)
            db_ref[...] = jnp.zeros_like(db_ref)

        c0 = i * per
        wins = [pl.ds(pl.multiple_of((c0 + cc) * CHUNK, CHUNK), BAND) for cc in range(per)]
        kw = jnp.stack([k_ref[w, :] for w in wins])
        vw = jnp.stack([v_ref[w, :] for w in wins])
        qc = q_ref[...].reshape(per, CHUNK, HEAD)
        doc = do_ref[...].astype(BF).reshape(per, CHUNK, HEAD)
        p = _att_probs(qc, kw, b_ref[0], c0)
        dp = _bdot(doc, vw, NTB)
        ds = p * (dp - jnp.sum(p * dp, axis=-1, keepdims=True))
        db_ref[0] += jnp.sum(ds, axis=0)
        dsb = (ds * scale).astype(BF)
        dq_ref[...] = _bdot(dsb, kw, NNB).reshape(qb, HEAD)

        def union(x):
            tot = None
            for cc in range(per):
                parts = [x[cc]]
                if cc:
                    parts.insert(0, jnp.zeros((cc * CHUNK, HEAD), F32))
                if cc < per - 1:
                    parts.append(jnp.zeros(((per - 1 - cc) * CHUNK, HEAD), F32))
                piece = jnp.concatenate(parts, axis=0) if len(parts) > 1 else parts[0]
                tot = piece if tot is None else tot + piece
            return tot

        span = pl.ds(pl.multiple_of(c0 * CHUNK, CHUNK), BAND + (per - 1) * CHUNK)
        dk_ref[span, :] += union(_bdot(dsb, qc, TNB))
        dv_ref[span, :] += union(_bdot(p.astype(BF), doc, TNB))

    qspec = pl.BlockSpec((qb, HEAD), lambda hh, i: (i, hh))
    kspec = pl.BlockSpec((sp, HEAD), lambda hh, i: (0, hh))
    bspec = pl.BlockSpec((1, CHUNK, BAND), lambda hh, i: (hh, 0, 0))
    return pl.pallas_call(
        body, name=name, grid=(h, s // qb), in_specs=[qspec, kspec, kspec, bspec, qspec],
        out_specs=[qspec, kspec, kspec, bspec],
        out_shape=[SDS((s, d), F32), SDS((sp, d), F32), SDS((sp, d), F32), SDS((h, CHUNK, BAND), F32)],
        compiler_params=_params("parallel", "arbitrary"))(q, kpad, vpad, bias, do)


def _pool_tile(x_ext, gain, w4, scale, row0):
    n, d = x_ext.shape
    dg = d // len(POOL_WINDOWS)
    pos = lax.broadcasted_iota(jnp.int32, (n, 1), 0) + row0
    hn = _rms(x_ext, gain) * jnp.where(pos >= 0, 1.0, 0.0)
    outs = []
    for gi, w in enumerate(POOL_WINDOWS):
        hg = hn[:, gi * dg:(gi + 1) * dg]
        acc, k = hg, 1
        while k < w:
            acc = acc + _shift(acc, k)
            k *= 2
        inv = 1.0 / jnp.clip(pos + 1, 1, w).astype(F32)
        pooled = acc * inv - hg
        outs.append(jnp.dot(pooled.astype(BF), w4[gi].astype(BF), preferred_element_type=F32))
    return jnp.concatenate(outs, axis=1) * scale


POOL_ROWS = 128


def _pool_fwd(x, gain, w4, scale, name):
    s, d = x.shape
    r = _tile(s, POOL_ROWS, HALO)
    hidx = _halo_index(r)

    def body(xc, xh, g_ref, w_ref, s_ref, o_ref):
        rb = pl.program_id(0)
        x_ext = jnp.concatenate([xh[...], xc[...]], axis=0)
        y = _pool_tile(x_ext, g_ref[...], [w_ref[gi] for gi in range(len(POOL_WINDOWS))], s_ref[...], rb * r - HALO)
        o_ref[...] = xc[...] + y[HALO:]

    cur = pl.BlockSpec((r, d), lambda rb: (rb, 0))
    vec = pl.BlockSpec((1, d), lambda rb: (0, 0))
    return pl.pallas_call(
        body, name=name, grid=(s // r,),
        in_specs=[cur, pl.BlockSpec((HALO, d), lambda rb: (hidx(rb), 0)), vec,
                  pl.BlockSpec(w4.shape, lambda rb: (0, 0, 0)), vec],
        out_specs=cur, out_shape=SDS((s, d), F32), compiler_params=_params("parallel"))(x, x, gain, w4, scale)


def _pool_bwd(x, gain, w4, scale, dy, name):
    s, d = x.shape
    r = _tile(s, POOL_ROWS, HALO)
    nb = s // r
    hidx = _halo_index(r)

    def body(xc, xh, g_ref, w_ref, s_ref, dy_ref, dx_ref, dg_ref, dw_ref, ds_ref, carry):
        step = pl.program_id(0)
        rb = nb - 1 - step
        x_ext = jnp.concatenate([xh[...], xc[...]], axis=0)
        fn = functools.partial(_pool_tile, row0=rb * r - HALO)
        _, vjp = jax.vjp(fn, x_ext, g_ref[...], [w_ref[gi] for gi in range(len(POOL_WINDOWS))], s_ref[...])
        ct = jnp.concatenate([jnp.zeros((HALO, d), F32), dy_ref[...]], axis=0)
        dx_ext, dg, dws, dsc = vjp(ct)

        @pl.when(step == 0)
        def _():
            carry[...] = jnp.zeros_like(carry)
            dg_ref[...] = jnp.zeros_like(dg_ref)
            dw_ref[...] = jnp.zeros_like(dw_ref)
            ds_ref[...] = jnp.zeros_like(ds_ref)

        dx_ref[...] = dy_ref[...] + dx_ext[HALO:]
        dx_ref[pl.ds(r - HALO, HALO), :] += carry[...]
        carry[...] = dx_ext[:HALO]
        dg_ref[...] += dg
        for gi, dw in enumerate(dws):
            dw_ref[gi] += dw
        ds_ref[...] += dsc

    cur = pl.BlockSpec((r, d), lambda t: (nb - 1 - t, 0))
    vec = pl.BlockSpec((1, d), lambda t: (0, 0))
    wspec = pl.BlockSpec(w4.shape, lambda t: (0, 0, 0))
    return pl.pallas_call(
        body, name=name, grid=(nb,),
        in_specs=[cur, pl.BlockSpec((HALO, d), lambda t: (hidx(nb - 1 - t), 0)), vec, wspec, vec, cur],
        out_specs=[cur, vec, wspec, vec],
        out_shape=[SDS((s, d), F32), SDS((1, d), F32), SDS(w4.shape, F32), SDS((1, d), F32)],
        scratch_shapes=[pltpu.VMEM((HALO, d), F32)], compiler_params=_params("arbitrary"))(x, x, gain, w4, scale, dy)


def _gdn_post(acc, kind):
    y = _silu(acc)
    if kind != "v":
        y = y * lax.rsqrt(jnp.sum(y * y, axis=-1, keepdims=True) + EPS)
    if kind == "q":
        y = y * (HEAD ** -0.5)
    return y


GDN_SUB = 128
GDN_CONV_HEADS = 4


def _gdn_conv_fwd(proj, cw, kind, head0, nheads, name):
    s = proj.shape[0]
    r = _tile(s, 512, HALO)
    sub = _tile(r, GDN_SUB, HALO)
    hb = min(GDN_CONV_HEADS, nheads)
    assert head0 % hb == 0 and nheads % hb == 0
    tc = hb * HEAD
    hidx = _halo_index(r)

    def body(uc, uh, cw_ref, o_ref):
        keep = jnp.where(pl.program_id(1) == 0, 0.0, 1.0)
        for hh in range(hb):
            tile = pl.ds(hh * HEAD, HEAD)
            taps = [cw_ref[j:j + 1, tile] for j in range(GDN_CONV)]
            for r0 in range(0, r, sub):
                u_ext = _ext_rows(uc, uh, False, r0, sub, tile, keep)
                acc = u_ext * taps[3]
                for j in range(1, GDN_CONV):
                    acc = acc + pltpu.roll(u_ext, j, axis=0) * taps[3 - j]
                o_ref[pl.ds(r0, sub), tile] = _gdn_post(acc, kind)[HALO:].astype(BF)

    return pl.pallas_call(
        body, name=name, grid=(nheads // hb, s // r),
        in_specs=[pl.BlockSpec((r, tc), lambda j, rb: (rb, head0 // hb + j)),
                  pl.BlockSpec((HALO, tc), lambda j, rb: (hidx(rb), head0 // hb + j)),
                  pl.BlockSpec((8, tc), lambda j, rb: (0, head0 // hb + j))],
        out_specs=pl.BlockSpec((r, tc), lambda j, rb: (rb, j)), out_shape=SDS((s, nheads * HEAD), BF),
        compiler_params=_params("parallel", "parallel"))(proj, proj, cw)


def _gdn_conv_bwd(proj, cw, dy, kind, head0, nheads, name):
    s = proj.shape[0]
    r = _tile(s, 512, HALO)
    sub = _tile(r, GDN_SUB, HALO)
    nb = s // r
    per = r // HALO
    hb = min(GDN_CONV_HEADS, nheads)
    tc = hb * HEAD
    hidx = _halo_index(r)
    nidx = lambda rb: jnp.minimum((rb + 1) * per, s // HALO - 1)
    rep = dy.shape[1] // (nheads * HEAD)
    n = sub + 2 * HALO

    def body(uc, uh, un, cw_ref, dyc, dyn, du_ref, dcw_ref):
        rb = pl.program_id(1)
        first = jnp.where(rb == 0, 0.0, 1.0)
        last = jnp.where(rb == nb - 1, 0.0, 1.0)
        rows = lax.broadcasted_iota(jnp.int32, (n, HEAD), 0)
        own = jnp.where((rows >= HALO) & (rows < HALO + sub), 1.0, 0.0)
        wrow = lax.broadcasted_iota(jnp.int32, (8, HEAD), 0)

        @pl.when(rb == 0)
        def _():
            dcw_ref[...] = jnp.zeros_like(dcw_ref)

        for hh in range(hb):
            tile = pl.ds(hh * HEAD, HEAD)
            taps = [cw_ref[j:j + 1, tile] for j in range(GDN_CONV)]
            dcw = jnp.zeros((8, HEAD), F32)
            for r0 in range(0, r, sub):
                u_ext = _ext_rows(uc, uh, un, r0, sub, tile, first, last)
                dy_ext = _ext_rows(dyc, None, dyn, r0, sub, pl.ds(hh * rep * HEAD, HEAD), 1.0, last)
                for e in range(1, rep):
                    dy_ext = dy_ext + _ext_rows(dyc, None, dyn, r0, sub, pl.ds((hh * rep + e) * HEAD, HEAD), 1.0, last)
                shifted = [u_ext] + [pltpu.roll(u_ext, j, axis=0) for j in range(1, GDN_CONV)]
                acc = shifted[0] * taps[3]
                for j in range(1, GDN_CONV):
                    acc = acc + shifted[j] * taps[3 - j]
                _, vjp = jax.vjp(functools.partial(_gdn_post, kind=kind), acc)
                dacc, = vjp(dy_ext)
                du = dacc * taps[3]
                for j in range(1, GDN_CONV):
                    du = du + pltpu.roll(dacc, n - j, axis=0) * taps[3 - j]
                du_ref[pl.ds(r0, sub), tile] = du[HALO:HALO + sub].astype(BF)
                dm = dacc * own
                for j in range(GDN_CONV):
                    dcw = dcw + jnp.where(wrow == j, jnp.sum(dm * shifted[3 - j], axis=0, keepdims=True), 0.0)
            dcw_ref[:, tile] += dcw

    ucol = lambda j: head0 // hb + j
    return pl.pallas_call(
        body, name=name, grid=(nheads // hb, nb),
        in_specs=[pl.BlockSpec((r, tc), lambda j, rb: (rb, ucol(j))),
                  pl.BlockSpec((HALO, tc), lambda j, rb: (hidx(rb), ucol(j))),
                  pl.BlockSpec((HALO, tc), lambda j, rb: (nidx(rb), ucol(j))),
                  pl.BlockSpec((8, tc), lambda j, rb: (0, ucol(j))),
                  pl.BlockSpec((r, rep * tc), lambda j, rb: (rb, j)),
                  pl.BlockSpec((HALO, rep * tc), lambda j, rb: (nidx(rb), j))],
        out_specs=[pl.BlockSpec((r, tc), lambda j, rb: (rb, j)), pl.BlockSpec((8, tc), lambda j, rb: (0, j))],
        out_shape=[SDS((s, nheads * HEAD), BF), SDS((8, nheads * HEAD), F32)],
        compiler_params=_params("parallel", "arbitrary"))(proj, proj, proj, cw, dy, dy)


GATE_ROWS = 256


def _gates_tile(a, bt, a_log, dt_bias, hv):
    r = a.shape[0]
    z = a + dt_bias
    softplus = jnp.maximum(z, 0.0) + jnp.log(1.0 + jnp.exp(-jnp.abs(z)))
    g = -jnp.exp(a_log) * softplus
    ri = lax.broadcasted_iota(jnp.int32, (r, r), 0)
    ci = lax.broadcasted_iota(jnp.int32, (r, r), 1)
    same_chunk = jnp.right_shift(ri, 6) == jnp.right_shift(ci, 6)
    tri = jnp.where(same_chunk, jnp.where(ri >= ci, 1.0, 0.0), 0.0).astype(F32)
    gc = jnp.dot(tri, g, precision=HI, preferred_element_type=F32)
    beta = 1.0 / (1.0 + jnp.exp(-bt))
    er = lax.broadcasted_iota(jnp.int32, (LANE, hv * HEAD), 0)
    ec = lax.broadcasted_iota(jnp.int32, (LANE, hv * HEAD), 1)
    expand = jnp.where(er == jnp.right_shift(ec, 7), 1.0, 0.0).astype(F32)
    return (jnp.dot(gc, expand, precision=HI, preferred_element_type=F32),
            jnp.dot(beta, expand, precision=HI, preferred_element_type=F32))


def _gates_fwd(ab, a_log, dt_bias, hv, name):
    s = ab.shape[0]
    r = _tile(s, GATE_ROWS, CHUNK)

    def body(a_ref, b_ref, al_ref, dt_ref, gc_ref, bb_ref):
        gcb, btb = _gates_tile(a_ref[...], b_ref[...], al_ref[...], dt_ref[...], hv)
        gc_ref[...] = gcb
        bb_ref[...] = btb

    vec = pl.BlockSpec((1, LANE), lambda i: (0, 0))
    wide = pl.BlockSpec((r, hv * HEAD), lambda i: (i, 0))
    return pl.pallas_call(
        body, name=name, grid=(s // r,),
        in_specs=[pl.BlockSpec((r, LANE), lambda i: (i, 0)), pl.BlockSpec((r, LANE), lambda i: (i, 1)), vec, vec],
        out_specs=[wide, wide], out_shape=[SDS((s, hv * HEAD), F32)] * 2,
        compiler_params=_params("parallel"))(ab, ab, a_log, dt_bias)


def _gates_bwd(ab, a_log, dt_bias, dgcb, dbtb, hv, name):
    s = ab.shape[0]
    r = _tile(s, GATE_ROWS, CHUNK)

    def body(a_ref, b_ref, al_ref, dt_ref, dgc_ref, dbb_ref, dab_ref, dal_ref, ddt_ref):
        i = pl.program_id(0)
        _, vjp = jax.vjp(functools.partial(_gates_tile, hv=hv), a_ref[...], b_ref[...], al_ref[...], dt_ref[...])
        da, dbt, dal, ddt = vjp((dgc_ref[...], dbb_ref[...]))
        dab_ref[:, 0:LANE] = da
        dab_ref[:, LANE:] = dbt

        @pl.when(i == 0)
        def _():
            dal_ref[...] = dal
            ddt_ref[...] = ddt

        @pl.when(i > 0)
        def _():
            dal_ref[...] += dal
            ddt_ref[...] += ddt

    vec = pl.BlockSpec((1, LANE), lambda i: (0, 0))
    wide = pl.BlockSpec((r, hv * HEAD), lambda i: (i, 0))
    return pl.pallas_call(
        body, name=name, grid=(s // r,),
        in_specs=[pl.BlockSpec((r, LANE), lambda i: (i, 0)), pl.BlockSpec((r, LANE), lambda i: (i, 1)), vec, vec, wide, wide],
        out_specs=[pl.BlockSpec((r, 2 * LANE), lambda i: (i, 0)), vec, vec],
        out_shape=[SDS((s, 2 * LANE), F32), SDS((1, LANE), F32), SDS((1, LANE), F32)],
        compiler_params=_params("arbitrary"))(ab, ab, a_log, dt_bias, dgcb, dbtb)


def _split_bf16(a):
    hi = a.astype(BF)
    return hi, (a - hi.astype(F32)).astype(BF)


def _dot3(a, b, dims=(((1,), (0,)), ((), ()))):
    ah, al = _split_bf16(a)
    bh, bl = _split_bf16(b)
    d = lambda x, y: lax.dot_general(x, y, dims, preferred_element_type=F32)
    return d(ah, bh) + (d(ah, bl) + d(al, bh))


NNB = (((2,), (1,)), ((0,), (0,)))
NTB = (((2,), (2,)), ((0,), (0,)))
TNB = (((1,), (1,)), ((0,), (0,)))


def _bdot(a, b, dims):
    return lax.dot_general(a, b, dims, preferred_element_type=F32)


def _unit_lower_inverse(a):
    ri = lax.broadcasted_iota(jnp.int32, a.shape, 1)
    ci = lax.broadcasted_iota(jnp.int32, a.shape, 2)
    p = -a
    t = jnp.where(ri == ci, 1.0, 0.0) + p
    for _ in range(5):
        p = _dot3(p, p, NNB)
        t = t + _dot3(t, p, NNB)
    return t


@jax.custom_vjp
def _known_inverse(a, t):
    return t


def _known_inverse_fwd(a, t):
    return t, t


def _known_inverse_bwd(t, g):
    return -_dot3(_dot3(t, g, TNB), t, NTB), jnp.zeros_like(t)


_known_inverse.defvjp(_known_inverse_fwd, _known_inverse_bwd)


def _delta_decay(gcb):
    c = CHUNK
    shape = (gcb.shape[0], c, c)
    ri = lax.broadcasted_iota(jnp.int32, shape, 1)
    ci = lax.broadcasted_iota(jnp.int32, shape, 2)
    causal = ri >= ci
    grow = jnp.stack([jnp.concatenate([gcb[b], gcb[b]], axis=0).T[:c, :c] for b in range(shape[0])])
    return jnp.where(causal, jnp.exp(jnp.where(causal, gcb[:, :, :c] - grow, 0.0)), 0.0), ri > ci


def _delta_system(k, gcb, btb):
    decay, strict = _delta_decay(gcb)
    return jnp.where(strict, _bdot((k * btb).astype(BF), k.astype(BF), NTB) * decay, 0.0)


def _delta_prep(q, k, v, gcb, btb, tinv):
    decay, strict = _delta_decay(gcb)
    kb = k * btb
    kbf = k.astype(BF)
    a = jnp.where(strict, _bdot(kb.astype(BF), kbf, NTB) * decay, 0.0)
    t = _known_inverse(a, tinv).astype(BF)
    u = _bdot(t, (v * btb).astype(BF), NNB)
    w = _bdot(t, (kb * jnp.exp(gcb)).astype(BF), NNB)
    attn = _bdot(q.astype(BF), kbf, NTB) * decay
    return u, w, attn


def _delta_scan(u, w, attn, q, k, gcb, s_in):
    c = CHUNK
    glast = gcb[:, c - 1:c, :]
    sb = s_in.astype(BF)
    v_new = u - _bdot(w.astype(BF), sb, NNB)
    vnb = v_new.astype(BF)
    o = _bdot((q * jnp.exp(gcb)).astype(BF), sb, NNB) + _bdot(attn.astype(BF), vnb, NNB)
    ks = (k * jnp.exp(glast - gcb)).astype(BF)
    s_out = s_in * jnp.exp(glast[:, :, 0:1]) + _bdot(ks, vnb, TNB)
    return o, s_out


def _head_stack(ref, rows, width, heads, rep=1):
    return jnp.stack([ref[rows, pl.ds((hh // rep) * width, width)].astype(F32) for hh in range(heads)])


PREP_ROWS = 1024
PREP_HEADS = 2
SCAN_ROWS = 256
SCAN_HEADS = 8


def _delta_prep_fwd(q, k, v, gcb, btb, name):
    s, dv = v.shape
    hv = dv // HEAD
    g = PREP_HEADS
    assert dv // q.shape[1] == g
    r = _tile(s, PREP_ROWS, CHUNK)

    def body(q_ref, k_ref, v_ref, g_ref, b_ref, u_ref, w_ref, a_ref, t_ref):
        nb = r // CHUNK
        qc = q_ref[...].astype(F32).reshape(nb, CHUNK, HEAD)
        kc = k_ref[...].astype(F32).reshape(nb, CHUNK, HEAD)
        for hh in range(g):
            cols = pl.ds(hh * HEAD, HEAD)
            half = pl.ds(hh * CHUNK, CHUNK)
            gc = g_ref[:, cols].reshape(nb, CHUNK, HEAD)
            bc = b_ref[:, cols].reshape(nb, CHUNK, HEAD)
            tinv = _unit_lower_inverse(_delta_system(kc, gc, bc))
            u, w, attn = _delta_prep(qc, kc, v_ref[:, cols].astype(F32).reshape(nb, CHUNK, HEAD), gc, bc, tinv)
            u_ref[:, cols] = u.reshape(r, HEAD)
            w_ref[:, cols] = w.reshape(r, HEAD).astype(BF)
            a_ref[:, half] = attn.reshape(r, CHUNK).astype(BF)
            t_ref[:, half] = tinv.reshape(r, CHUNK)

    kq = pl.BlockSpec((r, HEAD), lambda j, i: (i, j))
    vs = pl.BlockSpec((r, g * HEAD), lambda j, i: (i, j))
    sq = pl.BlockSpec((r, g * CHUNK), lambda j, i: (i, j))
    return pl.pallas_call(
        body, name=name, grid=(hv // g, s // r), in_specs=[kq, kq, vs, vs, vs], out_specs=[vs, vs, sq, sq],
        out_shape=[SDS((s, dv), F32), SDS((s, dv), BF), SDS((s, hv * CHUNK), BF), SDS((s, hv * CHUNK), F32)],
        compiler_params=_params("parallel", "parallel"))(q, k, v, gcb, btb)


def _delta_prep_bwd(q, k, v, gcb, btb, tinv, du, dw, dattn, dq_s, dk_s, dg_s, name):
    s, dv = v.shape
    hv = dv // HEAD
    g = PREP_HEADS
    r = _tile(s, PREP_ROWS, CHUNK)

    def body(q_ref, k_ref, v_ref, g_ref, b_ref, t_ref, du_ref, dw_ref, da_ref, dqs_ref, dks_ref, dgs_ref,
             dq_ref, dk_ref, dv_ref, dg_ref, db_ref):
        nb = r // CHUNK
        wide = lambda ref, cols: ref[:, cols].astype(F32).reshape(nb, CHUNK, HEAD)
        qc = q_ref[...].astype(F32).reshape(nb, CHUNK, HEAD)
        kc = k_ref[...].astype(F32).reshape(nb, CHUNK, HEAD)
        for hh in range(g):
            cols = pl.ds(hh * HEAD, HEAD)
            half = pl.ds(hh * CHUNK, CHUNK)
            fn = functools.partial(_delta_prep, tinv=t_ref[:, half].reshape(nb, CHUNK, CHUNK))
            _, vjp = jax.vjp(fn, qc, kc, wide(v_ref, cols), wide(g_ref, cols), wide(b_ref, cols))
            dq, dk, dvv, dg, db = vjp((wide(du_ref, cols), wide(dw_ref, cols),
                                       da_ref[:, half].astype(F32).reshape(nb, CHUNK, CHUNK)))
            dq_ref[:, cols] = dq.reshape(r, HEAD) + dqs_ref[:, cols]
            dk_ref[:, cols] = dk.reshape(r, HEAD) + dks_ref[:, cols]
            dv_ref[:, cols] = dvv.reshape(r, HEAD)
            dg_ref[:, cols] = dg.reshape(r, HEAD) + dgs_ref[:, cols]
            db_ref[:, cols] = db.reshape(r, HEAD)

    kq = pl.BlockSpec((r, HEAD), lambda j, i: (i, j))
    vs = pl.BlockSpec((r, g * HEAD), lambda j, i: (i, j))
    sq = pl.BlockSpec((r, g * CHUNK), lambda j, i: (i, j))
    return pl.pallas_call(
        body, name=name, grid=(hv // g, s // r), in_specs=[kq, kq, vs, vs, vs, sq, vs, vs, sq, vs, vs, vs],
        out_specs=[vs] * 5, out_shape=[SDS((s, dv), F32)] * 5,
        compiler_params=_params("parallel", "parallel"))(q, k, v, gcb, btb, tinv, du, dw, dattn, dq_s, dk_s, dg_s)


def _delta_scan_fwd(u, w, attn, q, k, gcb, name):
    s, dv = u.shape
    hv = dv // HEAD
    rep = dv // q.shape[1]
    g = min(SCAN_HEADS, hv)
    r = _tile(s, SCAN_ROWS, CHUNK)
    per = r // CHUNK

    def body(u_ref, w_ref, a_ref, q_ref, k_ref, g_ref, o_ref, st_ref, state):
        @pl.when(pl.program_id(1) == 0)
        def _():
            state[...] = jnp.zeros_like(state)

        def chunk(cc, carry):
            rows = pl.ds(pl.multiple_of(cc * CHUNK, CHUNK), CHUNK)
            s_in = state[...]
            o, s_out = _delta_scan(_head_stack(u_ref, rows, HEAD, g), _head_stack(w_ref, rows, HEAD, g),
                                   _head_stack(a_ref, rows, CHUNK, g), _head_stack(q_ref, rows, HEAD, g, rep),
                                   _head_stack(k_ref, rows, HEAD, g, rep), _head_stack(g_ref, rows, HEAD, g), s_in)
            for hh in range(g):
                st_ref[hh, cc] = s_in[hh]
                o_ref[rows, pl.ds(hh * HEAD, HEAD)] = o[hh].astype(BF)
            state[...] = s_out
            return carry

        lax.fori_loop(0, per, chunk, 0)

    kq = pl.BlockSpec((r, g // rep * HEAD), lambda j, i: (i, j))
    vs = pl.BlockSpec((r, g * HEAD), lambda j, i: (i, j))
    sq = pl.BlockSpec((r, g * CHUNK), lambda j, i: (i, j))
    return pl.pallas_call(
        body, name=name, grid=(hv // g, s // r), in_specs=[vs, vs, sq, kq, kq, vs],
        out_specs=[vs, pl.BlockSpec((g, per, HEAD, HEAD), lambda j, i: (j, i, 0, 0))],
        out_shape=[SDS((s, dv), BF), SDS((hv, s // CHUNK, HEAD, HEAD), F32)],
        scratch_shapes=[pltpu.VMEM((g, HEAD, HEAD), F32)],
        compiler_params=_params("parallel", "arbitrary"))(u, w, attn, q, k, gcb)


def _delta_scan_bwd(u, w, attn, q, k, gcb, states, do, name):
    s, dv = u.shape
    hv = dv // HEAD
    rep = dv // q.shape[1]
    g = min(SCAN_HEADS, hv)
    r = _tile(s, SCAN_ROWS, CHUNK)
    per = r // CHUNK
    nb = s // r

    def body(u_ref, w_ref, a_ref, q_ref, k_ref, g_ref, st_ref, do_ref, du_ref, dw_ref, da_ref, dq_ref, dk_ref, dg_ref, dstate):
        @pl.when(pl.program_id(1) == 0)
        def _():
            dstate[...] = jnp.zeros_like(dstate)

        def chunk(t, carry):
            cc = per - 1 - t
            rows = pl.ds(pl.multiple_of(cc * CHUNK, CHUNK), CHUNK)
            s_in = jnp.stack([st_ref[hh, cc] for hh in range(g)])
            _, vjp = jax.vjp(_delta_scan, _head_stack(u_ref, rows, HEAD, g), _head_stack(w_ref, rows, HEAD, g),
                             _head_stack(a_ref, rows, CHUNK, g), _head_stack(q_ref, rows, HEAD, g, rep),
                             _head_stack(k_ref, rows, HEAD, g, rep), _head_stack(g_ref, rows, HEAD, g), s_in)
            du, dw, da, dq, dk, dg, ds_in = vjp((_head_stack(do_ref, rows, HEAD, g), dstate[...]))
            for hh in range(g):
                cols = pl.ds(hh * HEAD, HEAD)
                du_ref[rows, cols] = du[hh].astype(BF)
                dw_ref[rows, cols] = dw[hh].astype(BF)
                da_ref[rows, pl.ds(hh * CHUNK, CHUNK)] = da[hh].astype(BF)
                dq_ref[rows, cols] = dq[hh]
                dk_ref[rows, cols] = dk[hh]
                dg_ref[rows, cols] = dg[hh]
            dstate[...] = ds_in
            return carry

        lax.fori_loop(0, per, chunk, 0)

    kq = pl.BlockSpec((r, g // rep * HEAD), lambda j, i: (nb - 1 - i, j))
    vs = pl.BlockSpec((r, g * HEAD), lambda j, i: (nb - 1 - i, j))
    sq = pl.BlockSpec((r, g * CHUNK), lambda j, i: (nb - 1 - i, j))
    return pl.pallas_call(
        body, name=name, grid=(hv // g, nb),
        in_specs=[vs, vs, sq, kq, kq, vs, pl.BlockSpec((g, per, HEAD, HEAD), lambda j, i: (j, nb - 1 - i, 0, 0)), vs],
        out_specs=[vs, vs, sq, vs, vs, vs],
        out_shape=[SDS((s, dv), BF), SDS((s, dv), BF), SDS((s, hv * CHUNK), BF)] + [SDS((s, dv), F32)] * 3,
        scratch_shapes=[pltpu.VMEM((g, HEAD, HEAD), F32)],
        compiler_params=_params("parallel", "arbitrary"))(u, w, attn, q, k, gcb, states, do)


def _delta_chunk(q, k, v, gcb, btb, s_in):
    c = CHUNK
    ri = lax.broadcasted_iota(jnp.int32, (c, c), 0)
    ci = lax.broadcasted_iota(jnp.int32, (c, c), 1)
    causal = ri >= ci
    gcol = gcb[:, :c]
    grow = jnp.concatenate([gcb, gcb], axis=0).T[:c, :c]
    decay = jnp.where(causal, jnp.exp(jnp.where(causal, gcol - grow, 0.0)), 0.0)
    kb = k * btb
    vb = v * btb
    kbf = k.astype(BF)
    a = jnp.where(ri > ci, lax.dot_general(kb.astype(BF), kbf, NT, preferred_element_type=F32) * decay, 0.0)
    p = -a
    t = jnp.where(ri == ci, 1.0, 0.0) + p
    for _ in range(5):
        p = jnp.dot(p, p, precision=HI, preferred_element_type=F32)
        t = t + jnp.dot(t, p, precision=HI, preferred_element_type=F32)
    eg = jnp.exp(gcb)
    u = jnp.dot(t, vb, precision=HI, preferred_element_type=F32)
    w = jnp.dot(t, kb * eg, precision=HI, preferred_element_type=F32)
    attn = lax.dot_general(q.astype(BF), kbf, NT, preferred_element_type=F32) * decay
    glast = gcb[c - 1:c, :]
    ks = k * jnp.exp(glast - gcb)
    sb = s_in.astype(BF)
    v_new = u - jnp.dot(w.astype(BF), sb, preferred_element_type=F32)
    o = (jnp.dot((q * eg).astype(BF), sb, preferred_element_type=F32)
         + jnp.dot(attn.astype(BF), v_new.astype(BF), preferred_element_type=F32))
    s_out = s_in * jnp.exp(glast[:, 0:1]) + lax.dot_general(ks.astype(BF), v_new.astype(BF), TN, preferred_element_type=F32)
    return o, s_out


GDN_ROWS = 512


def _delta_fwd(q, k, v, gcb, btb, name):
    s, dv = v.shape
    hv = dv // HEAD
    rep = dv // q.shape[1]
    r = _tile(s, GDN_ROWS, CHUNK)
    per = r // CHUNK
    nc = s // CHUNK

    def body(q_ref, k_ref, v_ref, g_ref, b_ref, o_ref, st_ref, state):
        @pl.when(pl.program_id(1) == 0)
        def _():
            state[...] = jnp.zeros_like(state)

        def chunk(cc, carry):
            rows = pl.ds(pl.multiple_of(cc * CHUNK, CHUNK), CHUNK)
            st_ref[0, cc] = state[...]
            o, s_out = _delta_chunk(q_ref[rows, :].astype(F32), k_ref[rows, :].astype(F32), v_ref[rows, :].astype(F32),
                                    g_ref[rows, :], b_ref[rows, :], state[...])
            o_ref[rows, :] = o.astype(BF)
            state[...] = s_out
            return carry

        lax.fori_loop(0, per, chunk, 0)

    kq = pl.BlockSpec((r, HEAD), lambda h, i: (i, h // rep))
    vs = pl.BlockSpec((r, HEAD), lambda h, i: (i, h))
    return pl.pallas_call(
        body, name=name, grid=(hv, s // r), in_specs=[kq, kq, vs, vs, vs],
        out_specs=[vs, pl.BlockSpec((1, per, HEAD, HEAD), lambda h, i: (h, i, 0, 0))],
        out_shape=[SDS((s, dv), BF), SDS((hv, nc, HEAD, HEAD), F32)],
        scratch_shapes=[pltpu.VMEM((HEAD, HEAD), F32)],
        compiler_params=_params("parallel", "arbitrary"))(q, k, v, gcb, btb)


def _delta_bwd(q, k, v, gcb, btb, states, do, name):
    s, dv = v.shape
    hv = dv // HEAD
    rep = dv // q.shape[1]
    r = _tile(s, GDN_ROWS, CHUNK)
    per = r // CHUNK
    nb = s // r

    def body(q_ref, k_ref, v_ref, g_ref, b_ref, st_ref, do_ref, dq_ref, dk_ref, dv_ref, dg_ref, db_ref, dstate):
        @pl.when(pl.program_id(1) == 0)
        def _():
            dstate[...] = jnp.zeros_like(dstate)

        def chunk(t, carry):
            cc = per - 1 - t
            rows = pl.ds(pl.multiple_of(cc * CHUNK, CHUNK), CHUNK)
            _, vjp = jax.vjp(_delta_chunk, q_ref[rows, :].astype(F32), k_ref[rows, :].astype(F32),
                             v_ref[rows, :].astype(F32), g_ref[rows, :], b_ref[rows, :], st_ref[0, cc])
            dq, dk, dvv, dg, db, ds_in = vjp((do_ref[rows, :].astype(F32), dstate[...]))
            dq_ref[rows, :] = dq
            dk_ref[rows, :] = dk
            dv_ref[rows, :] = dvv
            dg_ref[rows, :] = dg
            db_ref[rows, :] = db
            dstate[...] = ds_in
            return carry

        lax.fori_loop(0, per, chunk, 0)

    kq = pl.BlockSpec((r, HEAD), lambda h, i: (nb - 1 - i, h // rep))
    vs = pl.BlockSpec((r, HEAD), lambda h, i: (nb - 1 - i, h))
    return pl.pallas_call(
        body, name=name, grid=(hv, nb),
        in_specs=[kq, kq, vs, vs, vs, pl.BlockSpec((1, per, HEAD, HEAD), lambda h, i: (h, nb - 1 - i, 0, 0)), vs],
        out_specs=[vs] * 5, out_shape=[SDS((s, dv), F32)] * 5,
        scratch_shapes=[pltpu.VMEM((HEAD, HEAD), F32)],
        compiler_params=_params("parallel", "arbitrary"))(q, k, v, gcb, btb, states, do)


def _gdn_out_tile(o, gate, gain):
    return _headnorm(o, gain) * _silu(gate)


def _gdn_out_fwd(o, proj, gate_col0, gain, name):
    s, dv = o.shape
    r = _tile(s, 128, 16)

    def body(o_ref, g_ref, gain_ref, y_ref):
        y_ref[...] = _gdn_out_tile(o_ref[...].astype(F32), g_ref[...].astype(F32), gain_ref[...]).astype(BF)

    row = pl.BlockSpec((r, dv), lambda i: (i, 0))
    return pl.pallas_call(
        body, name=name, grid=(s // r,),
        in_specs=[row, pl.BlockSpec((r, dv), lambda i: (i, gate_col0)), pl.BlockSpec((1, HEAD), lambda i: (0, 0))],
        out_specs=row, out_shape=SDS((s, dv), BF), compiler_params=_params("parallel"))(o, proj, gain)


def _gdn_out_bwd(o, proj, gate_col0, gain, dy, name):
    s, dv = o.shape
    r = _tile(s, 128, 16)

    def body(o_ref, g_ref, gain_ref, dy_ref, do_ref, dg_ref, dgain_ref):
        i = pl.program_id(0)
        _, vjp = jax.vjp(_gdn_out_tile, o_ref[...].astype(F32), g_ref[...].astype(F32), gain_ref[...])
        do, dg, dgain = vjp(dy_ref[...].astype(F32))
        do_ref[...] = do
        dg_ref[...] = dg.astype(BF)

        @pl.when(i == 0)
        def _():
            dgain_ref[...] = dgain

        @pl.when(i > 0)
        def _():
            dgain_ref[...] += dgain

    row = pl.BlockSpec((r, dv), lambda i: (i, 0))
    vec = pl.BlockSpec((1, HEAD), lambda i: (0, 0))
    return pl.pallas_call(
        body, name=name, grid=(s // r,),
        in_specs=[row, pl.BlockSpec((r, dv), lambda i: (i, gate_col0)), vec, row],
        out_specs=[row, row, vec], out_shape=[SDS((s, dv), F32), SDS((s, dv), BF), SDS((1, HEAD), F32)],
        compiler_params=_params("arbitrary"))(o, proj, gain, dy)


WIDE_K = dict(tm=512, tn=1024, tk=8192, n_outer=True)
DEEP_K = 4096


def _ffn_forward(x, gain, wu, wg, cw, wd, tag):
    h, ht = _rmsnorm_fwd(x, gain, f"{tag}_norm")
    uu = _mm(h, wu, name=f"{tag}_up_u")
    ug = _mm(h, wg, name=f"{tag}_up_g")
    a, at = _ffn_act_fwd(uu, ug, cw, f"{tag}_act")
    y = _mm(a, wd, res=x, out_dtype=F32, name=f"{tag}_down", **WIDE_K)
    return y, (x, ht, uu, ug, at)


def _ffn_backward(saved, dys, gain, wu, wg, cw, wd, tag, early=None):
    x, ht, uu, ug, at = saved
    dy, dyb = dys
    da = _mm(dyb, wd, tb=True, name=f"{tag}_d_act")
    dwd = _mm(at, dyb, out_dtype=F32, name=f"{tag}_d_wd", tk=DEEP_K)
    duu, dug, dcw = _ffn_act_bwd(uu, ug, cw, da, f"{tag}_act_bwd")
    dwu = _mm(ht, duu, out_dtype=F32, name=f"{tag}_d_wu", tk=DEEP_K)
    dwg = _mm(ht, dug, out_dtype=F32, name=f"{tag}_d_wg", tk=DEEP_K)
    grads = dict(wu=dwu, wg=dwg, cw=dcw, wd=dwd)
    zero = early(grads) if early is not None else 0.0
    dh = _mm(duu, wu, tb=True, out_dtype=F32, name=f"{tag}_d_h_u", **WIDE_K)
    dh = _mm(dug, wg, tb=True, res=dh, out_dtype=F32, name=f"{tag}_d_h_g", **WIDE_K)
    dx, dxb, dgain = _rmsnorm_bwd(x, gain + zero, dh, dy, f"{tag}_norm_bwd")
    return (dx, dxb), dict(grads, gain=dgain)


def _att_forward(x, gain, p, tag):
    h, ht = _rmsnorm_fwd(x, gain, f"{tag}_norm")
    qkv = _mm(h, p["wqkv"], name=f"{tag}_qkv")
    q, kpad, vpad = _qkv_post_fwd(qkv, p["qg"], p["kg"], f"{tag}_qknorm")
    bias = _bias_expand(p["rel"], f"{tag}_bias")
    o = _attn_fwd(q, kpad, vpad, bias, f"{tag}_core")
    y = _mm(o, p["wo"], res=x, out_dtype=F32, name=f"{tag}_out")
    return y, (x, ht, qkv, q, kpad, vpad, bias, o)


def _att_backward(saved, dys, gain, p, tag, early=None):
    x, ht, qkv, q, kpad, vpad, bias, o = saved
    dy, dyb = dys
    do = _mm(dyb, p["wo"], tb=True, name=f"{tag}_d_o")
    dwo = _mm(o, dyb, ta=True, out_dtype=F32, name=f"{tag}_d_wo")
    dq, dkpad, dvpad, dbias = _attn_bwd(q, kpad, vpad, bias, do, f"{tag}_core_bwd")
    drel = _bias_reduce(dbias, f"{tag}_bias_bwd")
    dqkv, dqg, dkg = _qkv_post_bwd(qkv, p["qg"], p["kg"], dq, dkpad, dvpad, f"{tag}_qknorm_bwd")
    dwqkv = _mm(ht, dqkv, out_dtype=F32, name=f"{tag}_d_wqkv", tk=DEEP_K)
    grads = dict(wqkv=dwqkv, qg=dqg, kg=dkg, rel=drel, wo=dwo)
    zero = early(grads) if early is not None else 0.0
    dh = _mm(dqkv, p["wqkv"], tb=True, out_dtype=F32, name=f"{tag}_d_h", **WIDE_K)
    dx, dxb, dgain = _rmsnorm_bwd(x, gain + zero, dh, dy, f"{tag}_norm_bwd")
    return (dx, dxb), dict(grads, gain=dgain)


def _gdn_forward(x, gain, p, tag):
    d = x.shape[1]
    nk = d // HEAD
    hv = 2 * nk
    h, ht = _rmsnorm_fwd(x, gain, f"{tag}_norm")
    proj = _mm(h, p["wmain"], name=f"{tag}_proj")
    ab = _mm(h, p["wab"], out_dtype=F32, name=f"{tag}_proj_ab")
    q = _gdn_conv_fwd(proj, p["cw"], "q", 0, nk, f"{tag}_conv_q")
    k = _gdn_conv_fwd(proj, p["cw"], "k", nk, nk, f"{tag}_conv_k")
    v = _gdn_conv_fwd(proj, p["cw"], "v", 2 * nk, hv, f"{tag}_conv_v")
    gcb, btb = _gates_fwd(ab, p["a_log"], p["dt_bias"], hv, f"{tag}_gates")
    u, wd, attn, tinv = _delta_prep_fwd(q, k, v, gcb, btb, f"{tag}_delta_prep")
    o, states = _delta_scan_fwd(u, wd, attn, q, k, gcb, f"{tag}_delta_scan")
    og = _gdn_out_fwd(o, proj, 2, p["o_gain"], f"{tag}_onorm")
    y = _mm(og, p["wo"], res=x, out_dtype=F32, name=f"{tag}_out")
    return y, (x, ht, proj, ab, q, k, v, gcb, btb, u, wd, attn, tinv, o, states, og)


def _gdn_backward(saved, dys, gain, p, tag, early=None):
    x, ht, proj, ab, q, k, v, gcb, btb, u, wd, attn, tinv, o, states, og = saved
    dy, dyb = dys
    d = x.shape[1]
    nk = d // HEAD
    hv = 2 * nk
    dog = _mm(dyb, p["wo"], tb=True, name=f"{tag}_d_og")
    dwo = _mm(og, dyb, ta=True, out_dtype=F32, name=f"{tag}_d_wo")
    do, dgate, dogain = _gdn_out_bwd(o, proj, 2, p["o_gain"], dog, f"{tag}_onorm_bwd")
    du, dw, dattn, dq_s, dk_s, dg_s = _delta_scan_bwd(u, wd, attn, q, k, gcb, states, do, f"{tag}_delta_scan_bwd")
    dq, dk, dv, dgcb, dbtb = _delta_prep_bwd(q, k, v, gcb, btb, tinv, du, dw, dattn, dq_s, dk_s, dg_s, f"{tag}_delta_prep_bwd")
    dab, dalog, ddt = _gates_bwd(ab, p["a_log"], p["dt_bias"], dgcb, dbtb, hv, f"{tag}_gates_bwd")
    dpq, dcq = _gdn_conv_bwd(proj, p["cw"], dq, "q", 0, nk, f"{tag}_conv_q_bwd")
    dpk, dck = _gdn_conv_bwd(proj, p["cw"], dk, "k", nk, nk, f"{tag}_conv_k_bwd")
    dpv, dcv = _gdn_conv_bwd(proj, p["cw"], dv, "v", 2 * nk, hv, f"{tag}_conv_v_bwd")
    dproj = jnp.concatenate([dpq, dpk, dpv, dgate], axis=1)
    dcw = jnp.concatenate([dcq, dck, dcv], axis=1)
    dwmain = _mm(ht, dproj, out_dtype=F32, name=f"{tag}_d_wmain", tk=DEEP_K)
    dwab = _mm(ht, dab, out_dtype=F32, name=f"{tag}_d_wab")
    grads = dict(wmain=dwmain, wab=dwab, cw=dcw, a_log=dalog, dt_bias=ddt, o_gain=dogain, wo=dwo)
    zero = early(grads) if early is not None else 0.0
    dh = _mm(dproj, p["wmain"], tb=True, out_dtype=F32, name=f"{tag}_d_h_main", **WIDE_K)
    dh = _mm(dab, p["wab"], tb=True, res=dh, out_dtype=F32, name=f"{tag}_d_h_ab")
    dx, dxb, dgain = _rmsnorm_bwd(x, gain + zero, dh, dy, f"{tag}_norm_bwd")
    return (dx, dxb), dict(grads, gain=dgain)


def _resolve(entry, after):
    return entry(after) if callable(entry) else entry


def _local_step(x, target, w, sink=None):
    depth = len(w["ffn"])
    tape = []
    for i in range(depth):
        kind, j = i % 3, i // 3
        gain = w["mix_norm"][i:i + 1]
        if kind == 0:
            x, saved = _att_forward(x, gain, _resolve(w["att"][j], x), f"l{i}_att")
        elif kind == 1:
            x_in = x
            pw = _resolve(w["pool"][j], x)
            x = _pool_fwd(x_in, gain, pw["w"], pw["scale"], f"l{i}_pool")
            saved = x_in
        else:
            x, saved = _gdn_forward(x, gain, _resolve(w["gdn"][j], x), f"l{i}_gdn")
        f = _resolve(w["ffn"][i], x)
        x, fsaved = _ffn_forward(x, w["ffn_norm"][i:i + 1], f["wu"], f["wg"], f["cw"], f["wd"], f"l{i}_ffn")
        tape.append((saved, fsaved))
    dy, dyb, loss_row = _loss_head(x, target, "loss_head")
    dy = (dy, dyb)
    grads = dict(mix=[None] * depth, ffn=[None] * depth)
    zero = 0.0
    for i in reversed(range(depth)):
        kind, j = i % 3, i // 3
        saved, fsaved = tape[i]
        early = (lambda name, layer: functools.partial(sink, name, layer)) if sink is not None else (lambda name, layer: None)
        f = _resolve(w["ffn"][i], dy[0])
        dy, grads["ffn"][i] = _ffn_backward(fsaved, dy, w["ffn_norm"][i:i + 1] + zero, f["wu"], f["wg"], f["cw"], f["wd"],
                                            f"l{i}_ffn", early("ffn", i))
        gain = w["mix_norm"][i:i + 1] + zero
        if kind == 0:
            dy, grads["mix"][i] = _att_backward(saved, dy, gain, _resolve(w["att"][j], dy[0]), f"l{i}_att", early("att", i))
        elif kind == 1:
            pw = _resolve(w["pool"][j], dy[0])
            dx, dgain, dw4, dscale = _pool_bwd(saved, gain, pw["w"], pw["scale"], dy[0], f"l{i}_pool_bwd")
            dy = (dx, dx.astype(BF))
            grads["mix"][i] = dict(gain=dgain, w=dw4, scale=dscale)
            if sink is not None:
                zero = zero + sink("pool", i, grads["mix"][i])
        else:
            dy, grads["mix"][i] = _gdn_backward(saved, dy, gain, _resolve(w["gdn"][j], dy[0]), f"l{i}_gdn", early("gdn", i))
    return loss_row, dy[0], grads


MESH = pl.DeviceIdType.MESH
ANY = pl.BlockSpec(memory_space=pl.ANY)


def _position():
    return tuple(lax.axis_index(a) for a in AXES)


def _flip(pos, rel):
    return tuple(1 - p if (rel >> (2 - i)) & 1 else p for i, p in enumerate(pos))


def _index(pos):
    return 4 * pos[0] + 2 * pos[1] + pos[2]


def _all_gather(arr, name):
    def body(x_ref, o_ref, send, recv, local):
        me = _position()
        mine = pltpu.make_async_copy(x_ref, o_ref.at[_index(me)], local)
        mine.start()
        copies = []
        for rel in range(1, N_DEV):
            cp = pltpu.make_async_remote_copy(
                src_ref=x_ref, dst_ref=o_ref.at[_index(me)], send_sem=send.at[rel - 1], recv_sem=recv.at[rel - 1],
                device_id=_flip(me, rel), device_id_type=MESH)
            cp.start()
            copies.append(cp)
        for cp in copies:
            cp.wait()
        mine.wait()

    return pl.pallas_call(
        body, name=name, in_specs=[ANY], out_specs=ANY, out_shape=SDS((N_DEV,) + arr.shape, arr.dtype),
        scratch_shapes=[pltpu.SemaphoreType.DMA((N_DEV - 1,)), pltpu.SemaphoreType.DMA((N_DEV - 1,)),
                        pltpu.SemaphoreType.DMA(())])(arr)


def _exchange(arr, name):
    def body(x_ref, o_ref, send, recv, local):
        me = _position()
        mine = pltpu.make_async_copy(x_ref.at[_index(me)], o_ref.at[_index(me)], local)
        mine.start()
        copies = []
        for rel in range(1, N_DEV):
            peer = _flip(me, rel)
            cp = pltpu.make_async_remote_copy(
                src_ref=x_ref.at[_index(peer)], dst_ref=o_ref.at[_index(me)], send_sem=send.at[rel - 1],
                recv_sem=recv.at[rel - 1], device_id=peer, device_id_type=MESH)
            cp.start()
            copies.append(cp)
        for cp in copies:
            cp.wait()
        mine.wait()

    return pl.pallas_call(
        body, name=name, in_specs=[ANY], out_specs=ANY, out_shape=SDS(arr.shape, arr.dtype),
        scratch_shapes=[pltpu.SemaphoreType.DMA((N_DEV - 1,)), pltpu.SemaphoreType.DMA((N_DEV - 1,)),
                        pltpu.SemaphoreType.DMA(())])(arr)


HBM = pl.BlockSpec(memory_space=pltpu.HBM)
SEM = pl.BlockSpec(memory_space=pltpu.SEMAPHORE)
EFFECT = pltpu.SideEffectType.DATAFLOW_SIDE_EFFECTING


def _split_copies(x_ref, land_ref, send, recv, scatter):
    me = _position()
    copies = []
    for rel in range(1, N_DEV):
        peer = _flip(me, rel)
        copies.append(pltpu.make_async_remote_copy(
            src_ref=x_ref.at[_index(peer)] if scatter else x_ref, dst_ref=land_ref.at[_index(me)],
            send_sem=send.at[rel - 1], recv_sem=recv.at[rel - 1], device_id=peer, device_id_type=MESH))
    return copies


def _copies_start(arr, scatter, name):
    shape = arr.shape if scatter else (N_DEV,) + arr.shape

    def body(x_ref, land_ref, send, recv, x_thru, land_thru, token):
        for cp in _split_copies(x_ref, land_ref, send, recv, scatter):
            cp.start()
        token[...] = jnp.zeros_like(token)

    sems = pltpu.SemaphoreType.DMA((N_DEV - 1,))
    send, recv, x_thru, land_thru, token = pl.pallas_call(
        body, name=name,
        out_shape=(sems, sems, pltpu.HBM(arr.shape, arr.dtype), pltpu.HBM(shape, arr.dtype), SDS((8, LANE), F32)),
        in_specs=(HBM, HBM), out_specs=(SEM, SEM, HBM, HBM, pl.BlockSpec(memory_space=pltpu.VMEM)),
        input_output_aliases={0: 2, 1: 3}, compiler_params=pltpu.CompilerParams(has_side_effects=EFFECT),
    )(pltpu.with_memory_space_constraint(arr, pltpu.HBM), pltpu.with_memory_space_constraint(lax.empty(shape, arr.dtype), pltpu.HBM))
    return (send, recv, x_thru, land_thru, scatter, name), token[0, 0]


def _copies_wait(handle, after):
    send, recv, x_thru, land_thru, scatter, name = handle

    def body(x_ref, land_ref, send_ref, recv_ref, after_ref, x_dead, got_ref):
        for cp in _split_copies(x_ref, land_ref, send_ref, recv_ref, scatter):
            cp.wait_send()
            cp.wait_recv()

    return pl.pallas_call(
        body, name=name + "_wait",
        out_shape=(pltpu.HBM(x_thru.shape, x_thru.dtype), pltpu.HBM(land_thru.shape, land_thru.dtype)),
        in_specs=(HBM, HBM, SEM, SEM, ANY), out_specs=(HBM, HBM), input_output_aliases={0: 0, 1: 1},
        compiler_params=pltpu.CompilerParams(has_side_effects=EFFECT),
    )(x_thru, land_thru, send, recv, after)[1]


def _with_own(got, own):
    return lax.dynamic_update_index_in_dim(got, own.astype(got.dtype), _index(_position()), 0)


def _adamw(parts, w, m, v, name):
    if not isinstance(parts, (list, tuple)):
        parts = [parts]
    layers = len(parts)
    r, c = parts[0].shape[1:]
    assert w.shape == (layers * r, c), (w.shape, parts[0].shape, layers)
    tr = _tile(r, 128, 16)
    per = r // tr
    c1 = 1.0 / (1.0 - ADAM_B1 ** ADAM_STEP)
    c2 = 1.0 / (1.0 - ADAM_B2 ** ADAM_STEP)

    def body(*refs):
        p_refs = refs[:layers]
        w_ref, m_ref, v_ref, g_ref, d_ref, nm_ref, nv_ref = refs[layers:]

        def update(p_ref):
            g = p_ref[0].astype(F32)
            for s in range(1, N_DEV):
                g = g + p_ref[s].astype(F32)
            nm = ADAM_B1 * m_ref[...] + (1.0 - ADAM_B1) * g
            nv = ADAM_B2 * v_ref[...] + (1.0 - ADAM_B2) * (g * g)
            g_ref[...] = g
            nm_ref[...] = nm
            nv_ref[...] = nv
            d_ref[...] = -ADAM_LR * ((nm * c1) / (jnp.sqrt(nv * c2) + ADAM_EPS) + ADAM_WD * w_ref[...])

        if layers == 1:
            update(p_refs[0])
        else:
            for j in range(layers):
                pl.when(pl.program_id(0) == j)(functools.partial(update, p_refs[j]))

    p_specs = [pl.BlockSpec((N_DEV, tr, c), functools.partial(lambda l, i, j: (0, jnp.where(l == j, i, 0), 0), j=j))
               for j in range(layers)]
    row = pl.BlockSpec((tr, c), lambda l, i: (l * per + i, 0))
    return pl.pallas_call(
        body, name=name, grid=(layers, per), in_specs=p_specs + [row, row, row],
        out_specs=[row] * 4, out_shape=[SDS(w.shape, F32)] * 4, compiler_params=_params("arbitrary", "arbitrary"))(*parts, w, m, v)


PACK = 8 * LANE


def _pack(arrs):
    flat = []
    for a in arrs:
        a = a.reshape(-1).astype(F32)
        flat.append(jnp.pad(a, (0, (-a.shape[0]) % PACK)))
    return jnp.concatenate(flat).reshape(-1, LANE)


def _unpack(packed, shapes, lead=()):
    flat = packed.reshape(lead + (-1,))
    out, off = [], 0
    for shp in shapes:
        n = int(np.prod(shp))
        out.append(flat[..., off:off + n].reshape(lead + tuple(shp)))
        off += n + (-n) % PACK
    return out


def _pad_to(a, axis, size):
    pad = [(0, 0)] * a.ndim
    pad[axis] = (0, size - a.shape[axis])
    return jnp.pad(a, pad)


def _cols_from_shards(g):
    return jnp.transpose(g, (1, 0, 2)).reshape(g.shape[1], -1)


def _cols_to_shards(a):
    c = a.shape[-1] // N_DEV
    a = a.reshape(a.shape[:-1] + (N_DEV, c))
    return jnp.moveaxis(a, -2, 0)


def _rows_to_shards(a):
    r = a.shape[-2] // N_DEV
    a = a.reshape(a.shape[:-2] + (N_DEV, r, a.shape[-1]))
    return jnp.moveaxis(a, -3, 0)


WEIGHTS = ("mix_norm", "ffn_norm", "att_w_qkv", "att_q_gain", "att_k_gain", "att_rel_bias", "att_w_o", "pool_w",
           "pool_scale", "gdn_w_in", "gdn_conv", "gdn_a_log", "gdn_dt_bias", "gdn_o_gain", "gdn_w_o", "ffn_w_up",
           "ffn_conv", "ffn_w_down")
REPLICATED = ("mix_norm", "ffn_norm", "att_q_gain", "att_k_gain", "pool_scale", "gdn_a_log", "gdn_dt_bias", "gdn_o_gain")
SMALL_SHARDED = ("att_rel_bias", "gdn_conv", "ffn_conv")
BIG = ("att_w_qkv", "att_w_o", "pool_w", "gdn_w_in", "gdn_w_o", "ffn_w_up", "ffn_w_down")
KEEP_F32 = ("pool_w",)


def _memo(build):
    cache = []

    def entry(after):
        if not cache:
            cache.append(build(after))
        return cache[0]

    return entry


def _assemble_weights(w, get, small, f):
    d = w["mix_norm"].shape[1]
    nk = d // HEAD
    hv = 2 * nk
    fp = -(-f // FF_PAD) * FF_PAD
    out = dict(mix_norm=w["mix_norm"], ffn_norm=w["ffn_norm"], att=[], pool=[], gdn=[], ffn=[])

    def att(j, after):
        return dict(wqkv=_cols_from_shards(get("att_w_qkv", j, after)), wo=get("att_w_o", j, after).reshape(d, d),
                    qg=w["att_q_gain"][j:j + 1], kg=w["att_k_gain"][j:j + 1], rel=small["att_rel_bias"][j])

    def pool(j, after):
        g = get("pool_w", j, after)
        return dict(w=jnp.transpose(g, (1, 0, 2, 3)).reshape(g.shape[1], g.shape[3], g.shape[3]),
                    scale=w["pool_scale"][j:j + 1])

    def gdn(j, after):
        win = _cols_from_shards(get("gdn_w_in", j, after))
        nm = 6 * d
        wab = jnp.concatenate([_pad_to(win[:, nm:nm + hv], 1, LANE), _pad_to(win[:, nm + hv:], 1, LANE)], axis=1)
        return dict(wmain=win[:, :nm], wab=wab, cw=_pad_to(small["gdn_conv"][j], 0, 8),
                    a_log=_pad_to(w["gdn_a_log"][j:j + 1], 1, LANE), dt_bias=_pad_to(w["gdn_dt_bias"][j:j + 1], 1, LANE),
                    o_gain=w["gdn_o_gain"][j:j + 1], wo=get("gdn_w_o", j, after).reshape(2 * d, d))

    def ffn(i, after):
        g = get("ffn_w_up", i, after)
        half = N_DEV // 2
        tail = [jnp.zeros((d, fp - f), g.dtype)] if fp > f else []
        return dict(wu=jnp.concatenate([g[k] for k in range(half)] + tail, axis=1),
                    wg=jnp.concatenate([g[k] for k in range(half, N_DEV)] + tail, axis=1),
                    cw=_pad_to(_pad_to(small["ffn_conv"][i], 0, 8), 1, fp),
                    wd=_pad_to(get("ffn_w_down", i, after).reshape(f, d), 0, fp))

    for key, build, count in (("att", att, w["att_w_qkv"].shape[0]), ("pool", pool, w["pool_w"].shape[0]),
                              ("gdn", gdn, w["gdn_w_in"].shape[0]), ("ffn", ffn, w["ffn_w_up"].shape[0])):
        out[key] = [_memo(functools.partial(build, j)) for j in range(count)]
    return out


def _full_gradients(grads, w, f):
    d = w["mix_norm"].shape[1]
    nk = d // HEAD
    hv = 2 * nk
    depth = len(grads["ffn"])
    att = [grads["mix"][i] for i in range(depth) if i % 3 == 0]
    pool = [grads["mix"][i] for i in range(depth) if i % 3 == 1]
    gdn = [grads["mix"][i] for i in range(depth) if i % 3 == 2]
    ffn = grads["ffn"]
    win = [jnp.concatenate([g["wmain"], g["wab"][:, :hv], g["wab"][:, LANE:LANE + hv]], axis=1) for g in gdn]
    return dict(
        mix_norm=jnp.concatenate([g["gain"] for g in grads["mix"]], axis=0),
        ffn_norm=jnp.concatenate([g["gain"] for g in ffn], axis=0),
        att_w_qkv=jnp.stack([g["wqkv"] for g in att]),
        att_q_gain=jnp.concatenate([g["qg"] for g in att], axis=0),
        att_k_gain=jnp.concatenate([g["kg"] for g in att], axis=0),
        att_rel_bias=jnp.stack([g["rel"] for g in att]),
        att_w_o=jnp.stack([g["wo"] for g in att]),
        pool_w=jnp.stack([g["w"] for g in pool]),
        pool_scale=jnp.concatenate([g["scale"] for g in pool], axis=0),
        gdn_w_in=jnp.stack(win),
        gdn_conv=jnp.stack([g["cw"][:GDN_CONV] for g in gdn]),
        gdn_a_log=jnp.concatenate([g["a_log"][:, :hv] for g in gdn], axis=0),
        gdn_dt_bias=jnp.concatenate([g["dt_bias"][:, :hv] for g in gdn], axis=0),
        gdn_o_gain=jnp.concatenate([g["o_gain"] for g in gdn], axis=0),
        gdn_w_o=jnp.stack([g["wo"] for g in gdn]),
        ffn_w_up=jnp.stack([jnp.concatenate([g["wu"][:, :f], g["wg"][:, :f]], axis=1) for g in ffn]),
        ffn_conv=jnp.stack([g["cw"][:FFN_CONV, :f] for g in ffn]),
        ffn_w_down=jnp.stack([g["wd"][:f] for g in ffn]))


ROW_SHARDED = ("att_w_o", "gdn_w_o", "ffn_w_down")


def _to_shards(name, full):
    if name == "pool_w":
        r = full.shape[2] // N_DEV
        a = full.reshape(full.shape[:2] + (N_DEV, r, full.shape[3]))
        return jnp.moveaxis(a, 2, 0)
    return _rows_to_shards(full) if name in ROW_SHARDED else _cols_to_shards(full)


def kernel(x, mix_norm, ffn_norm, att_w_qkv, att_q_gain, att_k_gain, att_rel_bias, att_w_o, pool_w, pool_scale, gdn_w_in, gdn_conv, gdn_a_log, gdn_dt_bias, gdn_o_gain, gdn_w_o, ffn_w_up, ffn_conv, ffn_w_down, loss_target, m_mix_norm, m_ffn_norm, m_att_w_qkv, m_att_q_gain, m_att_k_gain, m_att_rel_bias, m_att_w_o, m_pool_w, m_pool_scale, m_gdn_w_in, m_gdn_conv, m_gdn_a_log, m_gdn_dt_bias, m_gdn_o_gain, m_gdn_w_o, m_ffn_w_up, m_ffn_conv, m_ffn_w_down, v_mix_norm, v_ffn_norm, v_att_w_qkv, v_att_q_gain, v_att_k_gain, v_att_rel_bias, v_att_w_o, v_pool_w, v_pool_scale, v_gdn_w_in, v_gdn_conv, v_gdn_a_log, v_gdn_dt_bias, v_gdn_o_gain, v_gdn_w_o, v_ffn_w_up, v_ffn_conv, v_ffn_w_down):
    w = dict(zip(WEIGHTS, (mix_norm, ffn_norm, att_w_qkv, att_q_gain, att_k_gain, att_rel_bias, att_w_o, pool_w, pool_scale, gdn_w_in, gdn_conv, gdn_a_log, gdn_dt_bias, gdn_o_gain, gdn_w_o, ffn_w_up, ffn_conv, ffn_w_down)))
    m = dict(zip(WEIGHTS, (m_mix_norm, m_ffn_norm, m_att_w_qkv, m_att_q_gain, m_att_k_gain, m_att_rel_bias, m_att_w_o, m_pool_w, m_pool_scale, m_gdn_w_in, m_gdn_conv, m_gdn_a_log, m_gdn_dt_bias, m_gdn_o_gain, m_gdn_w_o, m_ffn_w_up, m_ffn_conv, m_ffn_w_down)))
    v = dict(zip(WEIGHTS, (v_mix_norm, v_ffn_norm, v_att_w_qkv, v_att_q_gain, v_att_k_gain, v_att_rel_bias, v_att_w_o, v_pool_w, v_pool_scale, v_gdn_w_in, v_gdn_conv, v_gdn_a_log, v_gdn_dt_bias, v_gdn_o_gain, v_gdn_w_o, v_ffn_w_up, v_ffn_conv, v_ffn_w_down)))

    me = _index(_position())
    d = mix_norm.shape[1]
    hv = 2 * (d // HEAD)
    f = ffn_w_down.shape[1] * N_DEV
    depth = ffn_w_up.shape[0]

    small_shapes = [w[n].shape for n in SMALL_SHARDED]
    small_g = _all_gather(_pack([w[n] for n in SMALL_SHARDED]), "gather_small")
    small = {}
    for n, a in zip(SMALL_SHARDED, _unpack(small_g, small_shapes, lead=(N_DEV,))):
        small[n] = jnp.moveaxis(a, 0, -2).reshape(a.shape[1:-1] + (N_DEV * a.shape[-1],))
    order = []
    for i in range(depth):
        order += [[("att_w_qkv", i // 3), ("att_w_o", i // 3)], [("pool_w", i // 3)], [("gdn_w_in", i // 3), ("gdn_w_o", i // 3)]][i % 3]
        order += [("ffn_w_up", i), ("ffn_w_down", i)]
    after_small = small_g[0, 0, 0] * 0.0
    local = {(n, j): (w[n][j] + after_small if n in KEEP_F32 else (w[n][j] + after_small).astype(BF)) for n, j in order}
    arriving, zero = {}, 0.0
    for n, j in order:
        arriving[(n, j)], tok = _copies_start(local[(n, j)], False, f"gather_{n}_{j}")
        zero = zero + tok

    def get(n, j, after):
        return _with_own(_copies_wait(arriving[(n, j)], after), local[(n, j)])

    ordered = dict(w, mix_norm=mix_norm + zero, ffn_norm=ffn_norm + zero)
    full = _assemble_weights(ordered, get, small, f)

    leaving = {}

    def sink(kind, i, g):
        j = i // 3
        if kind == "ffn":
            c = 2 * f // N_DEV
            up = [g[key][:, k * c:(k + 1) * c] for key in ("wu", "wg") for k in range(N_DEV // 2)]
            pieces = [("ffn_w_up", i, jnp.stack(up)), ("ffn_w_down", i, _rows_to_shards(g["wd"][:f]))]
        elif kind == "att":
            pieces = [("att_w_qkv", j, _cols_to_shards(g["wqkv"])), ("att_w_o", j, _rows_to_shards(g["wo"]))]
        elif kind == "pool":
            pieces = [("pool_w", j, _to_shards("pool_w", g["w"][None])[:, 0])]
        else:
            win = jnp.concatenate([g["wmain"], g["wab"][:, :hv], g["wab"][:, LANE:LANE + hv]], axis=1)
            pieces = [("gdn_w_in", j, _cols_to_shards(win)), ("gdn_w_o", j, _rows_to_shards(g["wo"]))]
        tok = 0.0
        for n, l, shards in pieces:
            shards = shards if n in KEEP_F32 else shards.astype(BF)
            handle, t = _copies_start(shards, True, f"exchange_{n}_{l}")
            leaving[(n, l)] = (handle, lax.dynamic_index_in_dim(shards, me, 0, keepdims=False))
            tok = tok + t
        return tok

    loss_row, dx, grads = _local_step(x[0], loss_target[0], full, sink)
    gfull = _full_gradients(grads, w, f)

    out = {}
    for n in BIG:
        c = w[n].shape[-1]
        parts = []
        for l in range(w[n].shape[0]):
            handle, own = leaving[(n, l)]
            parts.append(_with_own(_copies_wait(handle, dx), own).reshape(N_DEV, -1, c))
        res = _adamw(parts, w[n].reshape(-1, c), m[n].reshape(-1, c), v[n].reshape(-1, c), f"adamw_{n}")
        out[n] = [a.reshape(w[n].shape) for a in res]
    sparts = _exchange(jnp.stack([_pack([_to_shards(n, gfull[n])[k] for n in SMALL_SHARDED]) for k in range(N_DEV)]),
                       "exchange_small")
    res = _adamw(sparts, *[_pack([t[n] for n in SMALL_SHARDED]) for t in (w, m, v)], "adamw_small")
    for n, *vals in zip(SMALL_SHARDED, *[_unpack(a, small_shapes) for a in res]):
        out[n] = vals
    rep_shapes = [w[n].shape for n in REPLICATED]
    rparts = _all_gather(_pack([gfull[n] for n in REPLICATED] + [loss_row[:, 0:1]]), "gather_replicated_grads")
    pad = jnp.zeros((1, 1), F32)
    res = _adamw(rparts, *[_pack([t[n] for n in REPLICATED] + [pad]) for t in (w, m, v)], "adamw_replicated")
    for n, *vals in zip(REPLICATED, *[_unpack(a, rep_shapes) for a in res]):
        out[n] = vals
    loss = jnp.sum(_unpack(rparts, rep_shapes + [(1, 1)], lead=(N_DEV,))[-1])
    return (loss, dx[None], *[out[n][0] for n in WEIGHTS], *[out[n][1] for n in WEIGHTS],
            *[out[n][2] for n in WEIGHTS], *[out[n][3] for n in WEIGHTS])
```

```python
import functools

import numpy as np
import jax
import jax.numpy as jnp
from jax import lax
from jax.experimental import pallas as pl
from jax.experimental.pallas import tpu as pltpu

F32 = jnp.float32
BF = jnp.bfloat16
SDS = jax.ShapeDtypeStruct

EPS = 1e-6
MASK_VALUE = -1e30
CHUNK = 64
LEFT_CHUNKS = 8
BAND_LEFT = LEFT_CHUNKS * CHUNK
BAND = BAND_LEFT + CHUNK
MAX_REL = 256
NUM_REL = (CHUNK - 1) + MAX_REL + 1
HEAD = 128
LANE = 128
HALO = 16
POOL_WINDOWS = (2, 4, 8, 16)
GDN_CONV = 4
FFN_CONV = 3
FF_PAD = 512
N_DEV = 8
AXES = ("x", "y", "c")
VMEM_LIMIT = 56 * 1024 * 1024

ADAM_LR = 0.001
ADAM_B1 = 0.9
ADAM_B2 = 0.999
ADAM_EPS = 1e-08
ADAM_WD = 0.01
ADAM_STEP = 10

HI = lax.Precision.HIGHEST
NT = (((1,), (1,)), ((), ()))
TN = (((0,), (0,)), ((), ()))


def _tile(n, target, mult=LANE):
    if n <= target:
        return n
    t = (target // mult) * mult
    while t >= mult:
        if n % t == 0:
            return t
        t -= mult
    return n


def _params(*sem):
    return pltpu.CompilerParams(dimension_semantics=sem, vmem_limit_bytes=VMEM_LIMIT)


def _silu(x):
    return x / (1.0 + jnp.exp(-x))


@functools.partial(jax.custom_vjp, nondiff_argnums=(1,))
def _shift(x, k):
    return pltpu.roll(x, k % x.shape[0], axis=0)


def _shift_fwd(x, k):
    return _shift(x, k), None


def _shift_bwd(k, _, g):
    return (pltpu.roll(g, (-k) % g.shape[0], axis=0),)


_shift.defvjp(_shift_fwd, _shift_bwd)


def _mm(a, b, *, name, ta=False, tb=False, out_dtype=BF, res=None, tm=1024, tn=1024, tk=2048, n_outer=False):
    m, k = (a.shape[1], a.shape[0]) if ta else a.shape
    n, kb = (b.shape[0], b.shape[1]) if tb else (b.shape[1], b.shape[0])
    assert k == kb, (a.shape, b.shape, ta, tb)
    if ta or a.dtype != BF:
        tk = min(tk, 1024)
    tm, tn, tk = _tile(m, tm), _tile(n, tn), _tile(k, tk)
    nk = k // tk
    dims = (((0 if ta else 1,), (1 if tb else 0,)), ((), ()))

    def body(*refs):
        if res is None:
            a_ref, b_ref, o_ref, acc = refs
        else:
            a_ref, b_ref, r_ref, o_ref, acc = refs
        prod = lax.dot_general(a_ref[...].astype(BF), b_ref[...].astype(BF), dims, preferred_element_type=F32)
        if nk == 1:
            if res is not None:
                prod = prod + r_ref[...].astype(F32)
            o_ref[...] = prod.astype(out_dtype)
            return
        kk = pl.program_id(2)

        @pl.when(kk == 0)
        def _():
            acc[...] = prod

        @pl.when(kk > 0)
        def _():
            acc[...] += prod

        @pl.when(kk == nk - 1)
        def _():
            r = acc[...]
            if res is not None:
                r = r + r_ref[...].astype(F32)
            o_ref[...] = r.astype(out_dtype)

    def order(fn):
        return (lambda j, i, q: fn(i, j, q)) if n_outer else fn

    a_spec = pl.BlockSpec((tk, tm), order(lambda i, j, q: (q, i))) if ta else pl.BlockSpec((tm, tk), order(lambda i, j, q: (i, q)))
    b_spec = pl.BlockSpec((tn, tk), order(lambda i, j, q: (j, q))) if tb else pl.BlockSpec((tk, tn), order(lambda i, j, q: (q, j)))
    o_spec = pl.BlockSpec((tm, tn), order(lambda i, j, q: (i, j)))
    ins, specs = [a, b], [a_spec, b_spec]
    if res is not None:
        ins.append(res)
        specs.append(o_spec)
    return pl.pallas_call(
        body, name=name, grid=(n // tn, m // tm, nk) if n_outer else (m // tm, n // tn, nk), in_specs=specs, out_specs=o_spec,
        out_shape=SDS((m, n), out_dtype), scratch_shapes=[pltpu.VMEM((tm, tn), F32)],
        compiler_params=_params("parallel", "parallel", "arbitrary"))(*ins)


def _rms(x, gain):
    return x * lax.rsqrt(jnp.mean(x * x, axis=-1, keepdims=True) + EPS) * gain


def _rmsnorm_fwd(x, gain, name):
    s, d = x.shape
    ts = _tile(s, 256, LANE)

    def body(x_ref, g_ref, o_ref, t_ref):
        h = _rms(x_ref[...], g_ref[...])
        o_ref[...] = h.astype(BF)
        t_ref[...] = h.T.astype(BF)

    return pl.pallas_call(
        body, name=name, grid=(s // ts,),
        in_specs=[pl.BlockSpec((ts, d), lambda i: (i, 0)), pl.BlockSpec((1, d), lambda i: (0, 0))],
        out_specs=[pl.BlockSpec((ts, d), lambda i: (i, 0)), pl.BlockSpec((d, ts), lambda i: (0, i))],
        out_shape=[SDS((s, d), BF), SDS((d, s), BF)], compiler_params=_params("parallel"))(x, gain)


def _rmsnorm_bwd(x, gain, dh, dres, name):
    s, d = x.shape
    ts = _tile(s, 256, 16)

    grp = 16

    def body(x_ref, g_ref, dh_ref, dr_ref, dx_ref, dxb_ref, dg_ref):
        i = pl.program_id(0)
        gain_row = g_ref[...]

        def piece(t, dg_rows):
            rows = pl.ds(pl.multiple_of(t * grp, grp), grp)
            xv = x_ref[rows, :]
            dh = dh_ref[rows, :].astype(F32)
            r = lax.rsqrt(jnp.mean(xv * xv, axis=-1, keepdims=True) + EPS)
            xh = xv * r
            dxh = dh * gain_row
            dx = r * (dxh - xh * jnp.mean(dxh * xh, axis=-1, keepdims=True)) + dr_ref[rows, :]
            dx_ref[rows, :] = dx
            dxb_ref[rows, :] = dx.astype(BF)
            return dg_rows + dh * xh

        def pieces(t, dg_rows):
            for q in range(together):
                dg_rows = piece(t * together + q, dg_rows)
            return dg_rows

        together = 4
        dg = jnp.sum(lax.fori_loop(0, ts // (grp * together), pieces, jnp.zeros((grp, d), F32)), axis=0, keepdims=True)

        @pl.when(i == 0)
        def _():
            dg_ref[...] = dg

        @pl.when(i > 0)
        def _():
            dg_ref[...] += dg

    row = pl.BlockSpec((ts, d), lambda i: (i, 0))
    vec = pl.BlockSpec((1, d), lambda i: (0, 0))
    return pl.pallas_call(
        body, name=name, grid=(s // ts,), in_specs=[row, vec, row, row], out_specs=[row, row, vec],
        out_shape=[SDS((s, d), F32), SDS((s, d), BF), SDS((1, d), F32)], compiler_params=_params("arbitrary"))(x, gain, dh, dres)


def _ffn_act_tile(u_ext, g_ext, cw):
    acc = u_ext * cw[2:3] + _shift(u_ext, 1) * cw[1:2] + _shift(u_ext, 2) * cw[0:1]
    return _silu(acc) * g_ext


def _halo_index(rows_per_block):
    per = rows_per_block // HALO
    return lambda rb: jnp.maximum(rb * per - 1, 0)


FFN_SUB = 256


def _ext_rows(cur, prev, nxt, r0, rows, tile, keep_prev=1.0, keep_next=1.0):
    n = cur.shape[0]
    parts = []
    if r0 > 0:
        parts.append(cur[pl.ds(r0 - HALO, HALO + rows), tile].astype(F32))
    else:
        head = jnp.zeros((HALO, LANE), F32) if prev is None else prev[:, tile].astype(F32) * keep_prev
        parts += [head, cur[pl.ds(0, rows), tile].astype(F32)]
    if nxt is not False:
        if r0 + rows < n:
            parts.append(cur[pl.ds(r0 + rows, HALO), tile].astype(F32))
        else:
            parts.append(jnp.zeros((HALO, LANE), F32) if nxt is None else nxt[:, tile].astype(F32) * keep_next)
    return jnp.concatenate(parts, axis=0)


def _ffn_act_fwd(u, g, cw, name):
    s, f = u.shape
    r, tc = _tile(s, 512, HALO), _tile(f, 512)
    sub = _tile(r, FFN_SUB, HALO)
    hidx = _halo_index(r)

    def body(uc, uh, gc, cw_ref, o_ref, t_ref):
        keep = jnp.where(pl.program_id(1) == 0, 0.0, 1.0)
        for lt in range(tc // LANE):
            tile = pl.ds(lt * LANE, LANE)
            c0, c1, c2 = cw_ref[0:1, tile], cw_ref[1:2, tile], cw_ref[2:3, tile]
            for r0 in range(0, r, sub):
                u_ext = _ext_rows(uc, uh, False, r0, sub, tile, keep)
                acc = u_ext * c2 + pltpu.roll(u_ext, 1, axis=0) * c1 + pltpu.roll(u_ext, 2, axis=0) * c0
                a = _silu(acc)[HALO:] * gc[pl.ds(r0, sub), tile].astype(F32)
                o_ref[pl.ds(r0, sub), tile] = a.astype(BF)
                t_ref[tile, pl.ds(r0, sub)] = a.T.astype(BF)

    cur = pl.BlockSpec((r, tc), lambda j, rb: (rb, j))
    return pl.pallas_call(
        body, name=name, grid=(f // tc, s // r),
        in_specs=[cur, pl.BlockSpec((HALO, tc), lambda j, rb: (hidx(rb), j)), cur,
                  pl.BlockSpec((8, tc), lambda j, rb: (0, j))],
        out_specs=[cur, pl.BlockSpec((tc, r), lambda j, rb: (j, rb))],
        out_shape=[SDS((s, f), BF), SDS((f, s), BF)], compiler_params=_params("parallel", "parallel"))(u, u, g, cw)


def _ffn_act_bwd(u, g, cw, da, name):
    s, f = u.shape
    r, tc = _tile(s, 512, HALO), _tile(f, 512)
    sub = _tile(r, FFN_SUB, HALO)
    nb = s // r
    per = r // HALO
    hidx = _halo_index(r)
    nidx = lambda rb: jnp.minimum((rb + 1) * per, s // HALO - 1)
    n = sub + 2 * HALO

    def body(uc, uh, un, gc, gn, dac, dan, cw_ref, du_ref, dg_ref, dcw_ref):
        rb = pl.program_id(1)
        first = jnp.where(rb == 0, 0.0, 1.0)
        last = jnp.where(rb == nb - 1, 0.0, 1.0)
        rows = lax.broadcasted_iota(jnp.int32, (n, LANE), 0)
        own = jnp.where((rows >= HALO) & (rows < HALO + sub), 1.0, 0.0)
        wrow = lax.broadcasted_iota(jnp.int32, (8, LANE), 0)

        @pl.when(rb == 0)
        def _():
            dcw_ref[...] = jnp.zeros_like(dcw_ref)

        for lt in range(tc // LANE):
            tile = pl.ds(lt * LANE, LANE)
            c0, c1, c2 = cw_ref[0:1, tile], cw_ref[1:2, tile], cw_ref[2:3, tile]
            dcw = jnp.zeros((8, LANE), F32)
            for r0 in range(0, r, sub):
                u_ext = _ext_rows(uc, uh, un, r0, sub, tile, first, last)
                g_ext = _ext_rows(gc, None, gn, r0, sub, tile, 1.0, last)
                da_ext = _ext_rows(dac, None, dan, r0, sub, tile, 1.0, last)
                sh1, sh2 = pltpu.roll(u_ext, 1, axis=0), pltpu.roll(u_ext, 2, axis=0)
                acc = u_ext * c2 + sh1 * c1 + sh2 * c0
                sg = 1.0 / (1.0 + jnp.exp(-acc))
                dg_ref[pl.ds(r0, sub), tile] = (da_ext * (acc * sg))[HALO:HALO + sub].astype(BF)
                dacc = da_ext * g_ext * (sg * (1.0 + acc * (1.0 - sg)))
                du = dacc * c2 + pltpu.roll(dacc, n - 1, axis=0) * c1 + pltpu.roll(dacc, n - 2, axis=0) * c0
                du_ref[pl.ds(r0, sub), tile] = du[HALO:HALO + sub].astype(BF)
                dm = dacc * own
                for j, tap in enumerate((sh2, sh1, u_ext)):
                    dcw = dcw + jnp.where(wrow == j, jnp.sum(dm * tap, axis=0, keepdims=True), 0.0)
            dcw_ref[:, tile] += dcw

    cur = pl.BlockSpec((r, tc), lambda j, rb: (rb, j))
    prev = pl.BlockSpec((HALO, tc), lambda j, rb: (hidx(rb), j))
    nxt = pl.BlockSpec((HALO, tc), lambda j, rb: (nidx(rb), j))
    wspec = pl.BlockSpec((8, tc), lambda j, rb: (0, j))
    return pl.pallas_call(
        body, name=name, grid=(f // tc, nb),
        in_specs=[cur, prev, nxt, cur, nxt, cur, nxt, wspec],
        out_specs=[cur, cur, wspec], out_shape=[SDS((s, f), BF), SDS((s, f), BF), SDS((8, f), F32)],
        compiler_params=_params("parallel", "arbitrary"))(u, u, u, g, g, da, da, cw)


def _loss_head(y, target, name):
    s, d = y.shape
    ts = _tile(s, 256, 16)

    def body(y_ref, t_ref, dy_ref, dyb_ref, l_ref):
        i = pl.program_id(0)
        err = y_ref[...] - t_ref[...]
        dy_ref[...] = err * (1.0 / d)
        dyb_ref[...] = (err * (1.0 / d)).astype(BF)
        part = jnp.zeros((1, LANE), F32) + 0.5 * jnp.sum(jnp.sum(err * err, axis=1, keepdims=True), axis=0, keepdims=True) / d

        @pl.when(i == 0)
        def _():
            l_ref[...] = part

        @pl.when(i > 0)
        def _():
            l_ref[...] += part

    row = pl.BlockSpec((ts, d), lambda i: (i, 0))
    return pl.pallas_call(
        body, name=name, grid=(s // ts,), in_specs=[row, row],
        out_specs=[row, row, pl.BlockSpec((1, LANE), lambda i: (0, 0))],
        out_shape=[SDS((s, d), F32), SDS((s, d), BF), SDS((1, LANE), F32)], compiler_params=_params("arbitrary"))(y, target)


def _rel_index():
    rel = BAND_LEFT + np.arange(CHUNK)[:, None] - np.arange(BAND)[None, :]
    return (np.clip(rel, -(CHUNK - 1), MAX_REL) + (CHUNK - 1)).reshape(1, CHUNK * BAND).astype(np.int32)


def _onehot(idx_row):
    rows = lax.broadcasted_iota(jnp.int32, (NUM_REL, idx_row.shape[1]), 0)
    return jnp.where(rows == idx_row, 1.0, 0.0).astype(F32)


def _bias_expand(rel_bias, name):
    h = rel_bias.shape[0]
    n = CHUNK * BAND
    tn = n // 8

    def body(rb_ref, idx_ref, o_ref):
        o_ref[...] = jnp.dot(rb_ref[...], _onehot(idx_ref[...]), precision=HI, preferred_element_type=F32)

    out = pl.pallas_call(
        body, name=name, grid=(n // tn,),
        in_specs=[pl.BlockSpec((h, NUM_REL), lambda j: (0, 0)), pl.BlockSpec((1, tn), lambda j: (0, j))],
        out_specs=pl.BlockSpec((h, tn), lambda j: (0, j)), out_shape=SDS((h, n), F32),
        compiler_params=_params("parallel"))(rel_bias, jnp.asarray(_rel_index()))
    return out.reshape(h, CHUNK, BAND)


def _bias_reduce(dbias, name):
    h = dbias.shape[0]
    n = CHUNK * BAND
    tn = n // 8

    def body(db_ref, idx_ref, o_ref):
        j = pl.program_id(0)
        part = lax.dot_general(db_ref[...], _onehot(idx_ref[...]), NT, precision=HI, preferred_element_type=F32)

        @pl.when(j == 0)
        def _():
            o_ref[...] = part

        @pl.when(j > 0)
        def _():
            o_ref[...] += part

    return pl.pallas_call(
        body, name=name, grid=(n // tn,),
        in_specs=[pl.BlockSpec((h, tn), lambda j: (0, j)), pl.BlockSpec((1, tn), lambda j: (0, j))],
        out_specs=pl.BlockSpec((h, NUM_REL), lambda j: (0, 0)), out_shape=SDS((h, NUM_REL), F32),
        compiler_params=_params("arbitrary"))(dbias.reshape(h, n), jnp.asarray(_rel_index()))


def _headnorm(x, gain):
    outs = []
    for hh in range(x.shape[1] // HEAD):
        xh = x[:, hh * HEAD:(hh + 1) * HEAD]
        outs.append(xh * lax.rsqrt(jnp.mean(xh * xh, axis=-1, keepdims=True) + EPS) * gain)
    return jnp.concatenate(outs, axis=1)


def _qkv_post_fwd(qkv, qg, kg, name):
    s, d3 = qkv.shape
    d = d3 // 3
    r = BAND_LEFT
    nb = s // r

    def body(x_ref, qg_ref, kg_ref, q_ref, k_ref, v_ref):
        i = pl.program_id(0)
        keep = jnp.where(i == 0, 0.0, 1.0)
        q_ref[...] = _headnorm(x_ref[:, 0:d].astype(F32), qg_ref[...]).astype(BF)
        k_ref[...] = (_headnorm(x_ref[:, d:2 * d].astype(F32), kg_ref[...]) * keep).astype(BF)
        v_ref[...] = (x_ref[:, 2 * d:].astype(F32) * keep).astype(BF)

    prev = lambda i: (jnp.maximum(i - 1, 0), 0)
    vec = pl.BlockSpec((1, HEAD), lambda i: (0, 0))
    return pl.pallas_call(
        body, name=name, grid=(nb + 1,),
        in_specs=[pl.BlockSpec((r, d3), prev), vec, vec],
        out_specs=[pl.BlockSpec((r, d), prev), pl.BlockSpec((r, d), lambda i: (i, 0)), pl.BlockSpec((r, d), lambda i: (i, 0))],
        out_shape=[SDS((s, d), BF), SDS((s + r, d), BF), SDS((s + r, d), BF)],
        compiler_params=_params("arbitrary"))(qkv, qg, kg)


def _qkv_post_bwd(qkv, qg, kg, dq, dkpad, dvpad, name):
    s, d3 = qkv.shape
    d = d3 // 3
    r = _tile(s, 256, 16)
    off = BAND_LEFT // r

    def body(x_ref, qg_ref, kg_ref, dq_ref, dk_ref, dv_ref, o_ref, dqg_ref, dkg_ref):
        i = pl.program_id(0)
        _, vq = jax.vjp(_headnorm, x_ref[:, 0:d].astype(F32), qg_ref[...])
        dxq, dqg = vq(dq_ref[...])
        _, vk = jax.vjp(_headnorm, x_ref[:, d:2 * d].astype(F32), kg_ref[...])
        dxk, dkg = vk(dk_ref[...])
        o_ref[:, 0:d] = dxq.astype(BF)
        o_ref[:, d:2 * d] = dxk.astype(BF)
        o_ref[:, 2 * d:] = dv_ref[...].astype(BF)

        @pl.when(i == 0)
        def _():
            dqg_ref[...] = dqg
            dkg_ref[...] = dkg

        @pl.when(i > 0)
        def _():
            dqg_ref[...] += dqg
            dkg_ref[...] += dkg

    vec = pl.BlockSpec((1, HEAD), lambda i: (0, 0))
    row3 = pl.BlockSpec((r, d3), lambda i: (i, 0))
    row = pl.BlockSpec((r, d), lambda i: (i, 0))
    padrow = pl.BlockSpec((r, d), lambda i: (i + off, 0))
    return pl.pallas_call(
        body, name=name, grid=(s // r,), in_specs=[row3, vec, vec, row, padrow, padrow],
        out_specs=[row3, vec, vec], out_shape=[SDS((s, d3), BF), SDS((1, HEAD), F32), SDS((1, HEAD), F32)],
        compiler_params=_params("arbitrary"))(qkv, qg, kg, dq, dkpad, dvpad)


ATT_QB = 1024


def _att_probs(q, kw, bias, c0):
    sc = _bdot(q, kw, NTB) * (HEAD ** -0.5) + bias
    lane = lax.broadcasted_iota(jnp.int32, sc.shape, 2)
    chunk = lax.broadcasted_iota(jnp.int32, sc.shape, 0) + c0
    sc = jnp.where(lane + chunk * CHUNK >= BAND_LEFT, sc, MASK_VALUE)
    p = jnp.exp(sc - jnp.max(sc, axis=-1, keepdims=True))
    return p / jnp.sum(p, axis=-1, keepdims=True)


def _attn_fwd(q, kpad, vpad, bias, name):
    s, d = q.shape
    h = d // HEAD
    sp = kpad.shape[0]
    qb = _tile(s, ATT_QB, CHUNK)
    per = qb // CHUNK

    def body(q_ref, k_ref, v_ref, b_ref, o_ref):
        c0 = pl.program_id(1) * per
        wins = [pl.ds(pl.multiple_of((c0 + cc) * CHUNK, CHUNK), BAND) for cc in range(per)]
        kw = jnp.stack([k_ref[w, :] for w in wins])
        vw = jnp.stack([v_ref[w, :] for w in wins])
        p = _att_probs(q_ref[...].reshape(per, CHUNK, HEAD), kw, b_ref[0], c0)
        o_ref[...] = _bdot(p.astype(BF), vw, NNB).reshape(qb, HEAD).astype(BF)

    qspec = pl.BlockSpec((qb, HEAD), lambda hh, i: (i, hh))
    kspec = pl.BlockSpec((sp, HEAD), lambda hh, i: (0, hh))
    return pl.pallas_call(
        body, name=name, grid=(h, s // qb),
        in_specs=[qspec, kspec, kspec, pl.BlockSpec((1, CHUNK, BAND), lambda hh, i: (hh, 0, 0))],
        out_specs=qspec, out_shape=SDS((s, d), BF), compiler_params=_params("parallel", "arbitrary"))(q, kpad, vpad, bias)


def _attn_bwd(q, kpad, vpad, bias, do, name):
    s, d = q.shape
    h = d // HEAD
    sp = kpad.shape[0]
    qb = _tile(s, ATT_QB, CHUNK)
    per = qb // CHUNK
    scale = HEAD ** -0.5

    def body(q_ref, k_ref, v_ref, b_ref, do_ref, dq_ref, dk_ref, dv_ref, db_ref):
        i = pl.program_id(1)

        @pl.when(i == 0)
        def _():
            dk_ref[...] = jnp.zeros_like(dk_ref)
            dv_ref[...] = jnp.zeros_like(dv_ref)
            db_ref[...] = jnp.zeros_like(db_ref)

        c0 = i * per
        wins = [pl.ds(pl.multiple_of((c0 + cc) * CHUNK, CHUNK), BAND) for cc in range(per)]
        kw = jnp.stack([k_ref[w, :] for w in wins])
        vw = jnp.stack([v_ref[w, :] for w in wins])
        qc = q_ref[...].reshape(per, CHUNK, HEAD)
        doc = do_ref[...].astype(BF).reshape(per, CHUNK, HEAD)
        p = _att_probs(qc, kw, b_ref[0], c0)
        dp = _bdot(doc, vw, NTB)
        ds = p * (dp - jnp.sum(p * dp, axis=-1, keepdims=True))
        db_ref[0] += jnp.sum(ds, axis=0)
        dsb = (ds * scale).astype(BF)
        dq_ref[...] = _bdot(dsb, kw, NNB).reshape(qb, HEAD)

        def union(x):
            tot = None
            for cc in range(per):
                parts = [x[cc]]
                if cc:
                    parts.insert(0, jnp.zeros((cc * CHUNK, HEAD), F32))
                if cc < per - 1:
                    parts.append(jnp.zeros(((per - 1 - cc) * CHUNK, HEAD), F32))
                piece = jnp.concatenate(parts, axis=0) if len(parts) > 1 else parts[0]
                tot = piece if tot is None else tot + piece
            return tot

        span = pl.ds(pl.multiple_of(c0 * CHUNK, CHUNK), BAND + (per - 1) * CHUNK)
        dk_ref[span, :] += union(_bdot(dsb, qc, TNB))
        dv_ref[span, :] += union(_bdot(p.astype(BF), doc, TNB))

    qspec = pl.BlockSpec((qb, HEAD), lambda hh, i: (i, hh))
    kspec = pl.BlockSpec((sp, HEAD), lambda hh, i: (0, hh))
    bspec = pl.BlockSpec((1, CHUNK, BAND), lambda hh, i: (hh, 0, 0))
    return pl.pallas_call(
        body, name=name, grid=(h, s // qb), in_specs=[qspec, kspec, kspec, bspec, qspec],
        out_specs=[qspec, kspec, kspec, bspec],
        out_shape=[SDS((s, d), F32), SDS((sp, d), F32), SDS((sp, d), F32), SDS((h, CHUNK, BAND), F32)],
        compiler_params=_params("parallel", "arbitrary"))(q, kpad, vpad, bias, do)


def _pool_tile(x_ext, gain, w4, scale, row0):
    n, d = x_ext.shape
    dg = d // len(POOL_WINDOWS)
    pos = lax.broadcasted_iota(jnp.int32, (n, 1), 0) + row0
    hn = _rms(x_ext, gain) * jnp.where(pos >= 0, 1.0, 0.0)
    outs = []
    for gi, w in enumerate(POOL_WINDOWS):
        hg = hn[:, gi * dg:(gi + 1) * dg]
        acc, k = hg, 1
        while k < w:
            acc = acc + _shift(acc, k)
            k *= 2
        inv = 1.0 / jnp.clip(pos + 1, 1, w).astype(F32)
        pooled = acc * inv - hg
        outs.append(jnp.dot(pooled.astype(BF), w4[gi].astype(BF), preferred_element_type=F32))
    return jnp.concatenate(outs, axis=1) * scale


POOL_ROWS = 128


def _pool_fwd(x, gain, w4, scale, name):
    s, d = x.shape
    r = _tile(s, POOL_ROWS, HALO)
    hidx = _halo_index(r)

    def body(xc, xh, g_ref, w_ref, s_ref, o_ref):
        rb = pl.program_id(0)
        x_ext = jnp.concatenate([xh[...], xc[...]], axis=0)
        y = _pool_tile(x_ext, g_ref[...], [w_ref[gi] for gi in range(len(POOL_WINDOWS))], s_ref[...], rb * r - HALO)
        o_ref[...] = xc[...] + y[HALO:]

    cur = pl.BlockSpec((r, d), lambda rb: (rb, 0))
    vec = pl.BlockSpec((1, d), lambda rb: (0, 0))
    return pl.pallas_call(
        body, name=name, grid=(s // r,),
        in_specs=[cur, pl.BlockSpec((HALO, d), lambda rb: (hidx(rb), 0)), vec,
                  pl.BlockSpec(w4.shape, lambda rb: (0, 0, 0)), vec],
        out_specs=cur, out_shape=SDS((s, d), F32), compiler_params=_params("parallel"))(x, x, gain, w4, scale)


def _pool_bwd(x, gain, w4, scale, dy, name):
    s, d = x.shape
    r = _tile(s, POOL_ROWS, HALO)
    nb = s // r
    hidx = _halo_index(r)

    def body(xc, xh, g_ref, w_ref, s_ref, dy_ref, dx_ref, dg_ref, dw_ref, ds_ref, carry):
        step = pl.program_id(0)
        rb = nb - 1 - step
        x_ext = jnp.concatenate([xh[...], xc[...]], axis=0)
        fn = functools.partial(_pool_tile, row0=rb * r - HALO)
        _, vjp = jax.vjp(fn, x_ext, g_ref[...], [w_ref[gi] for gi in range(len(POOL_WINDOWS))], s_ref[...])
        ct = jnp.concatenate([jnp.zeros((HALO, d), F32), dy_ref[...]], axis=0)
        dx_ext, dg, dws, dsc = vjp(ct)

        @pl.when(step == 0)
        def _():
            carry[...] = jnp.zeros_like(carry)
            dg_ref[...] = jnp.zeros_like(dg_ref)
            dw_ref[...] = jnp.zeros_like(dw_ref)
            ds_ref[...] = jnp.zeros_like(ds_ref)

        dx_ref[...] = dy_ref[...] + dx_ext[HALO:]
        dx_ref[pl.ds(r - HALO, HALO), :] += carry[...]
        carry[...] = dx_ext[:HALO]
        dg_ref[...] += dg
        for gi, dw in enumerate(dws):
            dw_ref[gi] += dw
        ds_ref[...] += dsc

    cur = pl.BlockSpec((r, d), lambda t: (nb - 1 - t, 0))
    vec = pl.BlockSpec((1, d), lambda t: (0, 0))
    wspec = pl.BlockSpec(w4.shape, lambda t: (0, 0, 0))
    return pl.pallas_call(
        body, name=name, grid=(nb,),
        in_specs=[cur, pl.BlockSpec((HALO, d), lambda t: (hidx(nb - 1 - t), 0)), vec, wspec, vec, cur],
        out_specs=[cur, vec, wspec, vec],
        out_shape=[SDS((s, d), F32), SDS((1, d), F32), SDS(w4.shape, F32), SDS((1, d), F32)],
        scratch_shapes=[pltpu.VMEM((HALO, d), F32)], compiler_params=_params("arbitrary"))(x, x, gain, w4, scale, dy)


def _gdn_post(acc, kind):
    y = _silu(acc)
    if kind != "v":
        y = y * lax.rsqrt(jnp.sum(y * y, axis=-1, keepdims=True) + EPS)
    if kind == "q":
        y = y * (HEAD ** -0.5)
    return y


GDN_SUB = 128
GDN_CONV_HEADS = 4


def _gdn_conv_fwd(proj, cw, kind, head0, nheads, name):
    s = proj.shape[0]
    r = _tile(s, 512, HALO)
    sub = _tile(r, GDN_SUB, HALO)
    hb = min(GDN_CONV_HEADS, nheads)
    assert head0 % hb == 0 and nheads % hb == 0
    tc = hb * HEAD
    hidx = _halo_index(r)

    def body(uc, uh, cw_ref, o_ref):
        keep = jnp.where(pl.program_id(1) == 0, 0.0, 1.0)
        for hh in range(hb):
            tile = pl.ds(hh * HEAD, HEAD)
            taps = [cw_ref[j:j + 1, tile] for j in range(GDN_CONV)]
            for r0 in range(0, r, sub):
                u_ext = _ext_rows(uc, uh, False, r0, sub, tile, keep)
                acc = u_ext * taps[3]
                for j in range(1, GDN_CONV):
                    acc = acc + pltpu.roll(u_ext, j, axis=0) * taps[3 - j]
                o_ref[pl.ds(r0, sub), tile] = _gdn_post(acc, kind)[HALO:].astype(BF)

    return pl.pallas_call(
        body, name=name, grid=(nheads // hb, s // r),
        in_specs=[pl.BlockSpec((r, tc), lambda j, rb: (rb, head0 // hb + j)),
                  pl.BlockSpec((HALO, tc), lambda j, rb: (hidx(rb), head0 // hb + j)),
                  pl.BlockSpec((8, tc), lambda j, rb: (0, head0 // hb + j))],
        out_specs=pl.BlockSpec((r, tc), lambda j, rb: (rb, j)), out_shape=SDS((s, nheads * HEAD), BF),
        compiler_params=_params("parallel", "parallel"))(proj, proj, cw)


def _gdn_conv_bwd(proj, cw, dy, kind, head0, nheads, name):
    s = proj.shape[0]
    r = _tile(s, 512, HALO)
    sub = _tile(r, GDN_SUB, HALO)
    nb = s // r
    per = r // HALO
    hb = min(GDN_CONV_HEADS, nheads)
    tc = hb * HEAD
    hidx = _halo_index(r)
    nidx = lambda rb: jnp.minimum((rb + 1) * per, s // HALO - 1)
    rep = dy.shape[1] // (nheads * HEAD)
    n = sub + 2 * HALO

    def body(uc, uh, un, cw_ref, dyc, dyn, du_ref, dcw_ref):
        rb = pl.program_id(1)
        first = jnp.where(rb == 0, 0.0, 1.0)
        last = jnp.where(rb == nb - 1, 0.0, 1.0)
        rows = lax.broadcasted_iota(jnp.int32, (n, HEAD), 0)
        own = jnp.where((rows >= HALO) & (rows < HALO + sub), 1.0, 0.0)
        wrow = lax.broadcasted_iota(jnp.int32, (8, HEAD), 0)

        @pl.when(rb == 0)
        def _():
            dcw_ref[...] = jnp.zeros_like(dcw_ref)

        for hh in range(hb):
            tile = pl.ds(hh * HEAD, HEAD)
            taps = [cw_ref[j:j + 1, tile] for j in range(GDN_CONV)]
            dcw = jnp.zeros((8, HEAD), F32)
            for r0 in range(0, r, sub):
                u_ext = _ext_rows(uc, uh, un, r0, sub, tile, first, last)
                dy_ext = _ext_rows(dyc, None, dyn, r0, sub, pl.ds(hh * rep * HEAD, HEAD), 1.0, last)
                for e in range(1, rep):
                    dy_ext = dy_ext + _ext_rows(dyc, None, dyn, r0, sub, pl.ds((hh * rep + e) * HEAD, HEAD), 1.0, last)
                shifted = [u_ext] + [pltpu.roll(u_ext, j, axis=0) for j in range(1, GDN_CONV)]
                acc = shifted[0] * taps[3]
                for j in range(1, GDN_CONV):
                    acc = acc + shifted[j] * taps[3 - j]
                _, vjp = jax.vjp(functools.partial(_gdn_post, kind=kind), acc)
                dacc, = vjp(dy_ext)
                du = dacc * taps[3]
                for j in range(1, GDN_CONV):
                    du = du + pltpu.roll(dacc, n - j, axis=0) * taps[3 - j]
                du_ref[pl.ds(r0, sub), tile] = du[HALO:HALO + sub].astype(BF)
                dm = dacc * own
                for j in range(GDN_CONV):
                    dcw = dcw + jnp.where(wrow == j, jnp.sum(dm * shifted[3 - j], axis=0, keepdims=True), 0.0)
            dcw_ref[:, tile] += dcw

    ucol = lambda j: head0 // hb + j
    return pl.pallas_call(
        body, name=name, grid=(nheads // hb, nb),
        in_specs=[pl.BlockSpec((r, tc), lambda j, rb: (rb, ucol(j))),
                  pl.BlockSpec((HALO, tc), lambda j, rb: (hidx(rb), ucol(j))),
                  pl.BlockSpec((HALO, tc), lambda j, rb: (nidx(rb), ucol(j))),
                  pl.BlockSpec((8, tc), lambda j, rb: (0, ucol(j))),
                  pl.BlockSpec((r, rep * tc), lambda j, rb: (rb, j)),
                  pl.BlockSpec((HALO, rep * tc), lambda j, rb: (nidx(rb), j))],
        out_specs=[pl.BlockSpec((r, tc), lambda j, rb: (rb, j)), pl.BlockSpec((8, tc), lambda j, rb: (0, j))],
        out_shape=[SDS((s, nheads * HEAD), BF), SDS((8, nheads * HEAD), F32)],
        compiler_params=_params("parallel", "arbitrary"))(proj, proj, proj, cw, dy, dy)


GATE_ROWS = 256


def _gates_tile(a, bt, a_log, dt_bias, hv):
    r = a.shape[0]
    z = a + dt_bias
    softplus = jnp.maximum(z, 0.0) + jnp.log(1.0 + jnp.exp(-jnp.abs(z)))
    g = -jnp.exp(a_log) * softplus
    ri = lax.broadcasted_iota(jnp.int32, (r, r), 0)
    ci = lax.broadcasted_iota(jnp.int32, (r, r), 1)
    same_chunk = jnp.right_shift(ri, 6) == jnp.right_shift(ci, 6)
    tri = jnp.where(same_chunk, jnp.where(ri >= ci, 1.0, 0.0), 0.0).astype(F32)
    gc = jnp.dot(tri, g, precision=HI, preferred_element_type=F32)
    beta = 1.0 / (1.0 + jnp.exp(-bt))
    er = lax.broadcasted_iota(jnp.int32, (LANE, hv * HEAD), 0)
    ec = lax.broadcasted_iota(jnp.int32, (LANE, hv * HEAD), 1)
    expand = jnp.where(er == jnp.right_shift(ec, 7), 1.0, 0.0).astype(F32)
    return (jnp.dot(gc, expand, precision=HI, preferred_element_type=F32),
            jnp.dot(beta, expand, precision=HI, preferred_element_type=F32))


def _gates_fwd(ab, a_log, dt_bias, hv, name):
    s = ab.shape[0]
    r = _tile(s, GATE_ROWS, CHUNK)

    def body(a_ref, b_ref, al_ref, dt_ref, gc_ref, bb_ref):
        gcb, btb = _gates_tile(a_ref[...], b_ref[...], al_ref[...], dt_ref[...], hv)
        gc_ref[...] = gcb
        bb_ref[...] = btb

    vec = pl.BlockSpec((1, LANE), lambda i: (0, 0))
    wide = pl.BlockSpec((r, hv * HEAD), lambda i: (i, 0))
    return pl.pallas_call(
        body, name=name, grid=(s // r,),
        in_specs=[pl.BlockSpec((r, LANE), lambda i: (i, 0)), pl.BlockSpec((r, LANE), lambda i: (i, 1)), vec, vec],
        out_specs=[wide, wide], out_shape=[SDS((s, hv * HEAD), F32)] * 2,
        compiler_params=_params("parallel"))(ab, ab, a_log, dt_bias)


def _gates_bwd(ab, a_log, dt_bias, dgcb, dbtb, hv, name):
    s = ab.shape[0]
    r = _tile(s, GATE_ROWS, CHUNK)

    def body(a_ref, b_ref, al_ref, dt_ref, dgc_ref, dbb_ref, dab_ref, dal_ref, ddt_ref):
        i = pl.program_id(0)
        _, vjp = jax.vjp(functools.partial(_gates_tile, hv=hv), a_ref[...], b_ref[...], al_ref[...], dt_ref[...])
        da, dbt, dal, ddt = vjp((dgc_ref[...], dbb_ref[...]))
        dab_ref[:, 0:LANE] = da
        dab_ref[:, LANE:] = dbt

        @pl.when(i == 0)
        def _():
            dal_ref[...] = dal
            ddt_ref[...] = ddt

        @pl.when(i > 0)
        def _():
            dal_ref[...] += dal
            ddt_ref[...] += ddt

    vec = pl.BlockSpec((1, LANE), lambda i: (0, 0))
    wide = pl.BlockSpec((r, hv * HEAD), lambda i: (i, 0))
    return pl.pallas_call(
        body, name=name, grid=(s // r,),
        in_specs=[pl.BlockSpec((r, LANE), lambda i: (i, 0)), pl.BlockSpec((r, LANE), lambda i: (i, 1)), vec, vec, wide, wide],
        out_specs=[pl.BlockSpec((r, 2 * LANE), lambda i: (i, 0)), vec, vec],
        out_shape=[SDS((s, 2 * LANE), F32), SDS((1, LANE), F32), SDS((1, LANE), F32)],
        compiler_params=_params("arbitrary"))(ab, ab, a_log, dt_bias, dgcb, dbtb)


def _split_bf16(a):
    hi = a.astype(BF)
    return hi, (a - hi.astype(F32)).astype(BF)


def _dot3(a, b, dims=(((1,), (0,)), ((), ()))):
    ah, al = _split_bf16(a)
    bh, bl = _split_bf16(b)
    d = lambda x, y: lax.dot_general(x, y, dims, preferred_element_type=F32)
    return d(ah, bh) + (d(ah, bl) + d(al, bh))


NNB = (((2,), (1,)), ((0,), (0,)))
NTB = (((2,), (2,)), ((0,), (0,)))
TNB = (((1,), (1,)), ((0,), (0,)))


def _bdot(a, b, dims):
    return lax.dot_general(a, b, dims, preferred_element_type=F32)


def _unit_lower_inverse(a):
    ri = lax.broadcasted_iota(jnp.int32, a.shape, 1)
    ci = lax.broadcasted_iota(jnp.int32, a.shape, 2)
    p = -a
    t = jnp.where(ri == ci, 1.0, 0.0) + p
    for _ in range(5):
        p = _dot3(p, p, NNB)
        t = t + _dot3(t, p, NNB)
    return t


@jax.custom_vjp
def _known_inverse(a, t):
    return t


def _known_inverse_fwd(a, t):
    return t, t


def _known_inverse_bwd(t, g):
    return -_dot3(_dot3(t, g, TNB), t, NTB), jnp.zeros_like(t)


_known_inverse.defvjp(_known_inverse_fwd, _known_inverse_bwd)


def _delta_decay(gcb):
    c = CHUNK
    shape = (gcb.shape[0], c, c)
    ri = lax.broadcasted_iota(jnp.int32, shape, 1)
    ci = lax.broadcasted_iota(jnp.int32, shape, 2)
    causal = ri >= ci
    grow = jnp.stack([jnp.concatenate([gcb[b], gcb[b]], axis=0).T[:c, :c] for b in range(shape[0])])
    return jnp.where(causal, jnp.exp(jnp.where(causal, gcb[:, :, :c] - grow, 0.0)), 0.0), ri > ci


def _delta_system(k, gcb, btb):
    decay, strict = _delta_decay(gcb)
    return jnp.where(strict, _bdot((k * btb).astype(BF), k.astype(BF), NTB) * decay, 0.0)


def _delta_prep(q, k, v, gcb, btb, tinv):
    decay, strict = _delta_decay(gcb)
    kb = k * btb
    kbf = k.astype(BF)
    a = jnp.where(strict, _bdot(kb.astype(BF), kbf, NTB) * decay, 0.0)
    t = _known_inverse(a, tinv).astype(BF)
    u = _bdot(t, (v * btb).astype(BF), NNB)
    w = _bdot(t, (kb * jnp.exp(gcb)).astype(BF), NNB)
    attn = _bdot(q.astype(BF), kbf, NTB) * decay
    return u, w, attn


def _delta_scan(u, w, attn, q, k, gcb, s_in):
    c = CHUNK
    glast = gcb[:, c - 1:c, :]
    sb = s_in.astype(BF)
    v_new = u - _bdot(w.astype(BF), sb, NNB)
    vnb = v_new.astype(BF)
    o = _bdot((q * jnp.exp(gcb)).astype(BF), sb, NNB) + _bdot(attn.astype(BF), vnb, NNB)
    ks = (k * jnp.exp(glast - gcb)).astype(BF)
    s_out = s_in * jnp.exp(glast[:, :, 0:1]) + _bdot(ks, vnb, TNB)
    return o, s_out


def _head_stack(ref, rows, width, heads, rep=1):
    return jnp.stack([ref[rows, pl.ds((hh // rep) * width, width)].astype(F32) for hh in range(heads)])


PREP_ROWS = 1024
PREP_HEADS = 2
SCAN_ROWS = 128
SCAN_HEADS = 16


def _delta_prep_fwd(q, k, v, gcb, btb, name):
    s, dv = v.shape
    hv = dv // HEAD
    g = PREP_HEADS
    assert dv // q.shape[1] == g
    r = _tile(s, PREP_ROWS, CHUNK)

    def body(q_ref, k_ref, v_ref, g_ref, b_ref, u_ref, w_ref, a_ref, t_ref):
        nb = r // CHUNK
        qc = q_ref[...].astype(F32).reshape(nb, CHUNK, HEAD)
        kc = k_ref[...].astype(F32).reshape(nb, CHUNK, HEAD)
        for hh in range(g):
            cols = pl.ds(hh * HEAD, HEAD)
            half = pl.ds(hh * CHUNK, CHUNK)
            gc = g_ref[:, cols].reshape(nb, CHUNK, HEAD)
            bc = b_ref[:, cols].reshape(nb, CHUNK, HEAD)
            tinv = _unit_lower_inverse(_delta_system(kc, gc, bc))
            u, w, attn = _delta_prep(qc, kc, v_ref[:, cols].astype(F32).reshape(nb, CHUNK, HEAD), gc, bc, tinv)
            u_ref[:, cols] = u.reshape(r, HEAD)
            w_ref[:, cols] = w.reshape(r, HEAD).astype(BF)
            a_ref[:, half] = attn.reshape(r, CHUNK).astype(BF)
            t_ref[:, half] = tinv.reshape(r, CHUNK)

    kq = pl.BlockSpec((r, HEAD), lambda j, i: (i, j))
    vs = pl.BlockSpec((r, g * HEAD), lambda j, i: (i, j))
    sq = pl.BlockSpec((r, g * CHUNK), lambda j, i: (i, j))
    return pl.pallas_call(
        body, name=name, grid=(hv // g, s // r), in_specs=[kq, kq, vs, vs, vs], out_specs=[vs, vs, sq, sq],
        out_shape=[SDS((s, dv), F32), SDS((s, dv), BF), SDS((s, hv * CHUNK), BF), SDS((s, hv * CHUNK), F32)],
        compiler_params=_params("parallel", "parallel"))(q, k, v, gcb, btb)


def _delta_prep_bwd(q, k, v, gcb, btb, tinv, du, dw, dattn, dq_s, dk_s, dg_s, name):
    s, dv = v.shape
    hv = dv // HEAD
    g = PREP_HEADS
    r = _tile(s, PREP_ROWS, CHUNK)

    def body(q_ref, k_ref, v_ref, g_ref, b_ref, t_ref, du_ref, dw_ref, da_ref, dqs_ref, dks_ref, dgs_ref,
             dq_ref, dk_ref, dv_ref, dg_ref, db_ref):
        nb = r // CHUNK
        wide = lambda ref, cols: ref[:, cols].astype(F32).reshape(nb, CHUNK, HEAD)
        qc = q_ref[...].astype(F32).reshape(nb, CHUNK, HEAD)
        kc = k_ref[...].astype(F32).reshape(nb, CHUNK, HEAD)
        for hh in range(g):
            cols = pl.ds(hh * HEAD, HEAD)
            half = pl.ds(hh * CHUNK, CHUNK)
            fn = functools.partial(_delta_prep, tinv=t_ref[:, half].reshape(nb, CHUNK, CHUNK))
            _, vjp = jax.vjp(fn, qc, kc, wide(v_ref, cols), wide(g_ref, cols), wide(b_ref, cols))
            dq, dk, dvv, dg, db = vjp((wide(du_ref, cols), wide(dw_ref, cols),
                                       da_ref[:, half].astype(F32).reshape(nb, CHUNK, CHUNK)))
            dq_ref[:, cols] = dq.reshape(r, HEAD) + dqs_ref[:, cols]
            dk_ref[:, cols] = dk.reshape(r, HEAD) + dks_ref[:, cols]
            dv_ref[:, cols] = dvv.reshape(r, HEAD)
            dg_ref[:, cols] = dg.reshape(r, HEAD) + dgs_ref[:, cols]
            db_ref[:, cols] = db.reshape(r, HEAD)

    kq = pl.BlockSpec((r, HEAD), lambda j, i: (i, j))
    vs = pl.BlockSpec((r, g * HEAD), lambda j, i: (i, j))
    sq = pl.BlockSpec((r, g * CHUNK), lambda j, i: (i, j))
    return pl.pallas_call(
        body, name=name, grid=(hv // g, s // r), in_specs=[kq, kq, vs, vs, vs, sq, vs, vs, sq, vs, vs, vs],
        out_specs=[vs] * 5, out_shape=[SDS((s, dv), F32)] * 5,
        compiler_params=_params("parallel", "parallel"))(q, k, v, gcb, btb, tinv, du, dw, dattn, dq_s, dk_s, dg_s)


def _delta_scan_fwd(u, w, attn, q, k, gcb, name):
    s, dv = u.shape
    hv = dv // HEAD
    rep = dv // q.shape[1]
    g = min(SCAN_HEADS, hv)
    r = _tile(s, SCAN_ROWS, CHUNK)
    per = r // CHUNK

    def body(u_ref, w_ref, a_ref, q_ref, k_ref, g_ref, o_ref, st_ref, state):
        @pl.when(pl.program_id(1) == 0)
        def _():
            state[...] = jnp.zeros_like(state)

        def chunk(cc, carry):
            rows = pl.ds(pl.multiple_of(cc * CHUNK, CHUNK), CHUNK)
            s_in = state[...]
            o, s_out = _delta_scan(_head_stack(u_ref, rows, HEAD, g), _head_stack(w_ref, rows, HEAD, g),
                                   _head_stack(a_ref, rows, CHUNK, g), _head_stack(q_ref, rows, HEAD, g, rep),
                                   _head_stack(k_ref, rows, HEAD, g, rep), _head_stack(g_ref, rows, HEAD, g), s_in)
            for hh in range(g):
                st_ref[hh, cc] = s_in[hh]
                o_ref[rows, pl.ds(hh * HEAD, HEAD)] = o[hh].astype(BF)
            state[...] = s_out
            return carry

        lax.fori_loop(0, per, chunk, 0)

    kq = pl.BlockSpec((r, g // rep * HEAD), lambda j, i: (i, j))
    vs = pl.BlockSpec((r, g * HEAD), lambda j, i: (i, j))
    sq = pl.BlockSpec((r, g * CHUNK), lambda j, i: (i, j))
    return pl.pallas_call(
        body, name=name, grid=(hv // g, s // r), in_specs=[vs, vs, sq, kq, kq, vs],
        out_specs=[vs, pl.BlockSpec((g, per, HEAD, HEAD), lambda j, i: (j, i, 0, 0))],
        out_shape=[SDS((s, dv), BF), SDS((hv, s // CHUNK, HEAD, HEAD), F32)],
        scratch_shapes=[pltpu.VMEM((g, HEAD, HEAD), F32)],
        compiler_params=_params("parallel", "arbitrary"))(u, w, attn, q, k, gcb)


def _delta_scan_bwd(u, w, attn, q, k, gcb, states, do, name):
    s, dv = u.shape
    hv = dv // HEAD
    rep = dv // q.shape[1]
    g = min(SCAN_HEADS, hv)
    r = _tile(s, SCAN_ROWS, CHUNK)
    per = r // CHUNK
    nb = s // r

    def body(u_ref, w_ref, a_ref, q_ref, k_ref, g_ref, st_ref, do_ref, du_ref, dw_ref, da_ref, dq_ref, dk_ref, dg_ref, dstate):
        @pl.when(pl.program_id(1) == 0)
        def _():
            dstate[...] = jnp.zeros_like(dstate)

        def chunk(t, carry):
            cc = per - 1 - t
            rows = pl.ds(pl.multiple_of(cc * CHUNK, CHUNK), CHUNK)
            s_in = jnp.stack([st_ref[hh, cc] for hh in range(g)])
            _, vjp = jax.vjp(_delta_scan, _head_stack(u_ref, rows, HEAD, g), _head_stack(w_ref, rows, HEAD, g),
                             _head_stack(a_ref, rows, CHUNK, g), _head_stack(q_ref, rows, HEAD, g, rep),
                             _head_stack(k_ref, rows, HEAD, g, rep), _head_stack(g_ref, rows, HEAD, g), s_in)
            du, dw, da, dq, dk, dg, ds_in = vjp((_head_stack(do_ref, rows, HEAD, g), dstate[...]))
            for hh in range(g):
                cols = pl.ds(hh * HEAD, HEAD)
                du_ref[rows, cols] = du[hh].astype(BF)
                dw_ref[rows, cols] = dw[hh].astype(BF)
                da_ref[rows, pl.ds(hh * CHUNK, CHUNK)] = da[hh].astype(BF)
                dq_ref[rows, cols] = dq[hh]
                dk_ref[rows, cols] = dk[hh]
                dg_ref[rows, cols] = dg[hh]
            dstate[...] = ds_in
            return carry

        lax.fori_loop(0, per, chunk, 0)

    kq = pl.BlockSpec((r, g // rep * HEAD), lambda j, i: (nb - 1 - i, j))
    vs = pl.BlockSpec((r, g * HEAD), lambda j, i: (nb - 1 - i, j))
    sq = pl.BlockSpec((r, g * CHUNK), lambda j, i: (nb - 1 - i, j))
    return pl.pallas_call(
        body, name=name, grid=(hv // g, nb),
        in_specs=[vs, vs, sq, kq, kq, vs, pl.BlockSpec((g, per, HEAD, HEAD), lambda j, i: (j, nb - 1 - i, 0, 0)), vs],
        out_specs=[vs, vs, sq, vs, vs, vs],
        out_shape=[SDS((s, dv), BF), SDS((s, dv), BF), SDS((s, hv * CHUNK), BF)] + [SDS((s, dv), F32)] * 3,
        scratch_shapes=[pltpu.VMEM((g, HEAD, HEAD), F32)],
        compiler_params=_params("parallel", "arbitrary"))(u, w, attn, q, k, gcb, states, do)


def _delta_chunk(q, k, v, gcb, btb, s_in):
    c = CHUNK
    ri = lax.broadcasted_iota(jnp.int32, (c, c), 0)
    ci = lax.broadcasted_iota(jnp.int32, (c, c), 1)
    causal = ri >= ci
    gcol = gcb[:, :c]
    grow = jnp.concatenate([gcb, gcb], axis=0).T[:c, :c]
    decay = jnp.where(causal, jnp.exp(jnp.where(causal, gcol - grow, 0.0)), 0.0)
    kb = k * btb
    vb = v * btb
    kbf = k.astype(BF)
    a = jnp.where(ri > ci, lax.dot_general(kb.astype(BF), kbf, NT, preferred_element_type=F32) * decay, 0.0)
    p = -a
    t = jnp.where(ri == ci, 1.0, 0.0) + p
    for _ in range(5):
        p = jnp.dot(p, p, precision=HI, preferred_element_type=F32)
        t = t + jnp.dot(t, p, precision=HI, preferred_element_type=F32)
    eg = jnp.exp(gcb)
    u = jnp.dot(t, vb, precision=HI, preferred_element_type=F32)
    w = jnp.dot(t, kb * eg, precision=HI, preferred_element_type=F32)
    attn = lax.dot_general(q.astype(BF), kbf, NT, preferred_element_type=F32) * decay
    glast = gcb[c - 1:c, :]
    ks = k * jnp.exp(glast - gcb)
    sb = s_in.astype(BF)
    v_new = u - jnp.dot(w.astype(BF), sb, preferred_element_type=F32)
    o = (jnp.dot((q * eg).astype(BF), sb, preferred_element_type=F32)
         + jnp.dot(attn.astype(BF), v_new.astype(BF), preferred_element_type=F32))
    s_out = s_in * jnp.exp(glast[:, 0:1]) + lax.dot_general(ks.astype(BF), v_new.astype(BF), TN, preferred_element_type=F32)
    return o, s_out


GDN_ROWS = 512


def _delta_fwd(q, k, v, gcb, btb, name):
    s, dv = v.shape
    hv = dv // HEAD
    rep = dv // q.shape[1]
    r = _tile(s, GDN_ROWS, CHUNK)
    per = r // CHUNK
    nc = s // CHUNK

    def body(q_ref, k_ref, v_ref, g_ref, b_ref, o_ref, st_ref, state):
        @pl.when(pl.program_id(1) == 0)
        def _():
            state[...] = jnp.zeros_like(state)

        def chunk(cc, carry):
            rows = pl.ds(pl.multiple_of(cc * CHUNK, CHUNK), CHUNK)
            st_ref[0, cc] = state[...]
            o, s_out = _delta_chunk(q_ref[rows, :].astype(F32), k_ref[rows, :].astype(F32), v_ref[rows, :].astype(F32),
                                    g_ref[rows, :], b_ref[rows, :], state[...])
            o_ref[rows, :] = o.astype(BF)
            state[...] = s_out
            return carry

        lax.fori_loop(0, per, chunk, 0)

    kq = pl.BlockSpec((r, HEAD), lambda h, i: (i, h // rep))
    vs = pl.BlockSpec((r, HEAD), lambda h, i: (i, h))
    return pl.pallas_call(
        body, name=name, grid=(hv, s // r), in_specs=[kq, kq, vs, vs, vs],
        out_specs=[vs, pl.BlockSpec((1, per, HEAD, HEAD), lambda h, i: (h, i, 0, 0))],
        out_shape=[SDS((s, dv), BF), SDS((hv, nc, HEAD, HEAD), F32)],
        scratch_shapes=[pltpu.VMEM((HEAD, HEAD), F32)],
        compiler_params=_params("parallel", "arbitrary"))(q, k, v, gcb, btb)


def _delta_bwd(q, k, v, gcb, btb, states, do, name):
    s, dv = v.shape
    hv = dv // HEAD
    rep = dv // q.shape[1]
    r = _tile(s, GDN_ROWS, CHUNK)
    per = r // CHUNK
    nb = s // r

    def body(q_ref, k_ref, v_ref, g_ref, b_ref, st_ref, do_ref, dq_ref, dk_ref, dv_ref, dg_ref, db_ref, dstate):
        @pl.when(pl.program_id(1) == 0)
        def _():
            dstate[...] = jnp.zeros_like(dstate)

        def chunk(t, carry):
            cc = per - 1 - t
            rows = pl.ds(pl.multiple_of(cc * CHUNK, CHUNK), CHUNK)
            _, vjp = jax.vjp(_delta_chunk, q_ref[rows, :].astype(F32), k_ref[rows, :].astype(F32),
                             v_ref[rows, :].astype(F32), g_ref[rows, :], b_ref[rows, :], st_ref[0, cc])
            dq, dk, dvv, dg, db, ds_in = vjp((do_ref[rows, :].astype(F32), dstate[...]))
            dq_ref[rows, :] = dq
            dk_ref[rows, :] = dk
            dv_ref[rows, :] = dvv
            dg_ref[rows, :] = dg
            db_ref[rows, :] = db
            dstate[...] = ds_in
            return carry

        lax.fori_loop(0, per, chunk, 0)

    kq = pl.BlockSpec((r, HEAD), lambda h, i: (nb - 1 - i, h // rep))
    vs = pl.BlockSpec((r, HEAD), lambda h, i: (nb - 1 - i, h))
    return pl.pallas_call(
        body, name=name, grid=(hv, nb),
        in_specs=[kq, kq, vs, vs, vs, pl.BlockSpec((1, per, HEAD, HEAD), lambda h, i: (h, nb - 1 - i, 0, 0)), vs],
        out_specs=[vs] * 5, out_shape=[SDS((s, dv), F32)] * 5,
        scratch_shapes=[pltpu.VMEM((HEAD, HEAD), F32)],
        compiler_params=_params("parallel", "arbitrary"))(q, k, v, gcb, btb, states, do)


def _gdn_out_tile(o, gate, gain):
    return _headnorm(o, gain) * _silu(gate)


def _gdn_out_fwd(o, proj, gate_col0, gain, name):
    s, dv = o.shape
    r = _tile(s, 128, 16)

    def body(o_ref, g_ref, gain_ref, y_ref):
        y_ref[...] = _gdn_out_tile(o_ref[...].astype(F32), g_ref[...].astype(F32), gain_ref[...]).astype(BF)

    row = pl.BlockSpec((r, dv), lambda i: (i, 0))
    return pl.pallas_call(
        body, name=name, grid=(s // r,),
        in_specs=[row, pl.BlockSpec((r, dv), lambda i: (i, gate_col0)), pl.BlockSpec((1, HEAD), lambda i: (0, 0))],
        out_specs=row, out_shape=SDS((s, dv), BF), compiler_params=_params("parallel"))(o, proj, gain)


def _gdn_out_bwd(o, proj, gate_col0, gain, dy, name):
    s, dv = o.shape
    r = _tile(s, 128, 16)

    def body(o_ref, g_ref, gain_ref, dy_ref, do_ref, dg_ref, dgain_ref):
        i = pl.program_id(0)
        _, vjp = jax.vjp(_gdn_out_tile, o_ref[...].astype(F32), g_ref[...].astype(F32), gain_ref[...])
        do, dg, dgain = vjp(dy_ref[...].astype(F32))
        do_ref[...] = do
        dg_ref[...] = dg.astype(BF)

        @pl.when(i == 0)
        def _():
            dgain_ref[...] = dgain

        @pl.when(i > 0)
        def _():
            dgain_ref[...] += dgain

    row = pl.BlockSpec((r, dv), lambda i: (i, 0))
    vec = pl.BlockSpec((1, HEAD), lambda i: (0, 0))
    return pl.pallas_call(
        body, name=name, grid=(s // r,),
        in_specs=[row, pl.BlockSpec((r, dv), lambda i: (i, gate_col0)), vec, row],
        out_specs=[row, row, vec], out_shape=[SDS((s, dv), F32), SDS((s, dv), BF), SDS((1, HEAD), F32)],
        compiler_params=_params("arbitrary"))(o, proj, gain, dy)


WIDE_K = dict(tm=512, tn=1024, tk=8192, n_outer=True)
DEEP_K = 4096


def _ffn_forward(x, gain, wu, wg, cw, wd, tag):
    h, ht = _rmsnorm_fwd(x, gain, f"{tag}_norm")
    uu = _mm(h, wu, name=f"{tag}_up_u")
    ug = _mm(h, wg, name=f"{tag}_up_g")
    a, at = _ffn_act_fwd(uu, ug, cw, f"{tag}_act")
    y = _mm(a, wd, res=x, out_dtype=F32, name=f"{tag}_down", **WIDE_K)
    return y, (x, ht, uu, ug, at)


def _ffn_backward(saved, dys, gain, wu, wg, cw, wd, tag, early=None):
    x, ht, uu, ug, at = saved
    dy, dyb = dys
    da = _mm(dyb, wd, tb=True, name=f"{tag}_d_act")
    dwd = _mm(at, dyb, out_dtype=F32, name=f"{tag}_d_wd", tk=DEEP_K)
    duu, dug, dcw = _ffn_act_bwd(uu, ug, cw, da, f"{tag}_act_bwd")
    dwu = _mm(ht, duu, out_dtype=F32, name=f"{tag}_d_wu", tk=DEEP_K)
    dwg = _mm(ht, dug, out_dtype=F32, name=f"{tag}_d_wg", tk=DEEP_K)
    grads = dict(wu=dwu, wg=dwg, cw=dcw, wd=dwd)
    zero = early(grads) if early is not None else 0.0
    dh = _mm(duu, wu, tb=True, out_dtype=F32, name=f"{tag}_d_h_u", **WIDE_K)
    dh = _mm(dug, wg, tb=True, res=dh, out_dtype=F32, name=f"{tag}_d_h_g", **WIDE_K)
    dx, dxb, dgain = _rmsnorm_bwd(x, gain + zero, dh, dy, f"{tag}_norm_bwd")
    return (dx, dxb), dict(grads, gain=dgain)


def _att_forward(x, gain, p, tag):
    h, ht = _rmsnorm_fwd(x, gain, f"{tag}_norm")
    qkv = _mm(h, p["wqkv"], name=f"{tag}_qkv")
    q, kpad, vpad = _qkv_post_fwd(qkv, p["qg"], p["kg"], f"{tag}_qknorm")
    bias = _bias_expand(p["rel"], f"{tag}_bias")
    o = _attn_fwd(q, kpad, vpad, bias, f"{tag}_core")
    y = _mm(o, p["wo"], res=x, out_dtype=F32, name=f"{tag}_out")
    return y, (x, ht, qkv, q, kpad, vpad, bias, o)


def _att_backward(saved, dys, gain, p, tag, early=None):
    x, ht, qkv, q, kpad, vpad, bias, o = saved
    dy, dyb = dys
    do = _mm(dyb, p["wo"], tb=True, name=f"{tag}_d_o")
    dwo = _mm(o, dyb, ta=True, out_dtype=F32, name=f"{tag}_d_wo")
    dq, dkpad, dvpad, dbias = _attn_bwd(q, kpad, vpad, bias, do, f"{tag}_core_bwd")
    drel = _bias_reduce(dbias, f"{tag}_bias_bwd")
    dqkv, dqg, dkg = _qkv_post_bwd(qkv, p["qg"], p["kg"], dq, dkpad, dvpad, f"{tag}_qknorm_bwd")
    dwqkv = _mm(ht, dqkv, out_dtype=F32, name=f"{tag}_d_wqkv", tk=DEEP_K)
    grads = dict(wqkv=dwqkv, qg=dqg, kg=dkg, rel=drel, wo=dwo)
    zero = early(grads) if early is not None else 0.0
    dh = _mm(dqkv, p["wqkv"], tb=True, out_dtype=F32, name=f"{tag}_d_h", **WIDE_K)
    dx, dxb, dgain = _rmsnorm_bwd(x, gain + zero, dh, dy, f"{tag}_norm_bwd")
    return (dx, dxb), dict(grads, gain=dgain)


def _gdn_forward(x, gain, p, tag):
    d = x.shape[1]
    nk = d // HEAD
    hv = 2 * nk
    h, ht = _rmsnorm_fwd(x, gain, f"{tag}_norm")
    proj = _mm(h, p["wmain"], name=f"{tag}_proj")
    ab = _mm(h, p["wab"], out_dtype=F32, name=f"{tag}_proj_ab")
    q = _gdn_conv_fwd(proj, p["cw"], "q", 0, nk, f"{tag}_conv_q")
    k = _gdn_conv_fwd(proj, p["cw"], "k", nk, nk, f"{tag}_conv_k")
    v = _gdn_conv_fwd(proj, p["cw"], "v", 2 * nk, hv, f"{tag}_conv_v")
    gcb, btb = _gates_fwd(ab, p["a_log"], p["dt_bias"], hv, f"{tag}_gates")
    u, wd, attn, tinv = _delta_prep_fwd(q, k, v, gcb, btb, f"{tag}_delta_prep")
    o, states = _delta_scan_fwd(u, wd, attn, q, k, gcb, f"{tag}_delta_scan")
    og = _gdn_out_fwd(o, proj, 2, p["o_gain"], f"{tag}_onorm")
    y = _mm(og, p["wo"], res=x, out_dtype=F32, name=f"{tag}_out")
    return y, (x, ht, proj, ab, q, k, v, gcb, btb, u, wd, attn, tinv, o, states, og)


def _gdn_backward(saved, dys, gain, p, tag, early=None):
    x, ht, proj, ab, q, k, v, gcb, btb, u, wd, attn, tinv, o, states, og = saved
    dy, dyb = dys
    d = x.shape[1]
    nk = d // HEAD
    hv = 2 * nk
    dog = _mm(dyb, p["wo"], tb=True, name=f"{tag}_d_og")
    dwo = _mm(og, dyb, ta=True, out_dtype=F32, name=f"{tag}_d_wo")
    do, dgate, dogain = _gdn_out_bwd(o, proj, 2, p["o_gain"], dog, f"{tag}_onorm_bwd")
    du, dw, dattn, dq_s, dk_s, dg_s = _delta_scan_bwd(u, wd, attn, q, k, gcb, states, do, f"{tag}_delta_scan_bwd")
    dq, dk, dv, dgcb, dbtb = _delta_prep_bwd(q, k, v, gcb, btb, tinv, du, dw, dattn, dq_s, dk_s, dg_s, f"{tag}_delta_prep_bwd")
    dab, dalog, ddt = _gates_bwd(ab, p["a_log"], p["dt_bias"], dgcb, dbtb, hv, f"{tag}_gates_bwd")
    dpq, dcq = _gdn_conv_bwd(proj, p["cw"], dq, "q", 0, nk, f"{tag}_conv_q_bwd")
    dpk, dck = _gdn_conv_bwd(proj, p["cw"], dk, "k", nk, nk, f"{tag}_conv_k_bwd")
    dpv, dcv = _gdn_conv_bwd(proj, p["cw"], dv, "v", 2 * nk, hv, f"{tag}_conv_v_bwd")
    dproj = jnp.concatenate([dpq, dpk, dpv, dgate], axis=1)
    dcw = jnp.concatenate([dcq, dck, dcv], axis=1)
    dwmain = _mm(ht, dproj, out_dtype=F32, name=f"{tag}_d_wmain", tk=DEEP_K)
    dwab = _mm(ht, dab, out_dtype=F32, name=f"{tag}_d_wab")
    grads = dict(wmain=dwmain, wab=dwab, cw=dcw, a_log=dalog, dt_bias=ddt, o_gain=dogain, wo=dwo)
    zero = early(grads) if early is not None else 0.0
    dh = _mm(dproj, p["wmain"], tb=True, out_dtype=F32, name=f"{tag}_d_h_main", **WIDE_K)
    dh = _mm(dab, p["wab"], tb=True, res=dh, out_dtype=F32, name=f"{tag}_d_h_ab")
    dx, dxb, dgain = _rmsnorm_bwd(x, gain + zero, dh, dy, f"{tag}_norm_bwd")
    return (dx, dxb), dict(grads, gain=dgain)


def _resolve(entry, after):
    return entry(after) if callable(entry) else entry


def _local_step(x, target, w, sink=None):
    depth = len(w["ffn"])
    tape = []
    for i in range(depth):
        kind, j = i % 3, i // 3
        gain = w["mix_norm"][i:i + 1]
        if kind == 0:
            x, saved = _att_forward(x, gain, _resolve(w["att"][j], x), f"l{i}_att")
        elif kind == 1:
            x_in = x
            pw = _resolve(w["pool"][j], x)
            x = _pool_fwd(x_in, gain, pw["w"], pw["scale"], f"l{i}_pool")
            saved = x_in
        else:
            x, saved = _gdn_forward(x, gain, _resolve(w["gdn"][j], x), f"l{i}_gdn")
        f = _resolve(w["ffn"][i], x)
        x, fsaved = _ffn_forward(x, w["ffn_norm"][i:i + 1], f["wu"], f["wg"], f["cw"], f["wd"], f"l{i}_ffn")
        tape.append((saved, fsaved))
    dy, dyb, loss_row = _loss_head(x, target, "loss_head")
    dy = (dy, dyb)
    grads = dict(mix=[None] * depth, ffn=[None] * depth)
    zero = 0.0
    for i in reversed(range(depth)):
        kind, j = i % 3, i // 3
        saved, fsaved = tape[i]
        early = (lambda name, layer: functools.partial(sink, name, layer)) if sink is not None else (lambda name, layer: None)
        f = _resolve(w["ffn"][i], dy[0])
        dy, grads["ffn"][i] = _ffn_backward(fsaved, dy, w["ffn_norm"][i:i + 1] + zero, f["wu"], f["wg"], f["cw"], f["wd"],
                                            f"l{i}_ffn", early("ffn", i))
        gain = w["mix_norm"][i:i + 1] + zero
        if kind == 0:
            dy, grads["mix"][i] = _att_backward(saved, dy, gain, _resolve(w["att"][j], dy[0]), f"l{i}_att", early("att", i))
        elif kind == 1:
            pw = _resolve(w["pool"][j], dy[0])
            dx, dgain, dw4, dscale = _pool_bwd(saved, gain, pw["w"], pw["scale"], dy[0], f"l{i}_pool_bwd")
            dy = (dx, dx.astype(BF))
            grads["mix"][i] = dict(gain=dgain, w=dw4, scale=dscale)
            if sink is not None:
                zero = zero + sink("pool", i, grads["mix"][i])
        else:
            dy, grads["mix"][i] = _gdn_backward(saved, dy, gain, _resolve(w["gdn"][j], dy[0]), f"l{i}_gdn", early("gdn", i))
    return loss_row, dy[0], grads


MESH = pl.DeviceIdType.MESH
ANY = pl.BlockSpec(memory_space=pl.ANY)


def _position():
    return tuple(lax.axis_index(a) for a in AXES)


def _flip(pos, rel):
    return tuple(1 - p if (rel >> (2 - i)) & 1 else p for i, p in enumerate(pos))


def _index(pos):
    return 4 * pos[0] + 2 * pos[1] + pos[2]


def _all_gather(arr, name):
    def body(x_ref, o_ref, send, recv, local):
        me = _position()
        mine = pltpu.make_async_copy(x_ref, o_ref.at[_index(me)], local)
        mine.start()
        copies = []
        for rel in range(1, N_DEV):
            cp = pltpu.make_async_remote_copy(
                src_ref=x_ref, dst_ref=o_ref.at[_index(me)], send_sem=send.at[rel - 1], recv_sem=recv.at[rel - 1],
                device_id=_flip(me, rel), device_id_type=MESH)
            cp.start()
            copies.append(cp)
        for cp in copies:
            cp.wait()
        mine.wait()

    return pl.pallas_call(
        body, name=name, in_specs=[ANY], out_specs=ANY, out_shape=SDS((N_DEV,) + arr.shape, arr.dtype),
        scratch_shapes=[pltpu.SemaphoreType.DMA((N_DEV - 1,)), pltpu.SemaphoreType.DMA((N_DEV - 1,)),
                        pltpu.SemaphoreType.DMA(())])(arr)


def _exchange(arr, name):
    def body(x_ref, o_ref, send, recv, local):
        me = _position()
        mine = pltpu.make_async_copy(x_ref.at[_index(me)], o_ref.at[_index(me)], local)
        mine.start()
        copies = []
        for rel in range(1, N_DEV):
            peer = _flip(me, rel)
            cp = pltpu.make_async_remote_copy(
                src_ref=x_ref.at[_index(peer)], dst_ref=o_ref.at[_index(me)], send_sem=send.at[rel - 1],
                recv_sem=recv.at[rel - 1], device_id=peer, device_id_type=MESH)
            cp.start()
            copies.append(cp)
        for cp in copies:
            cp.wait()
        mine.wait()

    return pl.pallas_call(
        body, name=name, in_specs=[ANY], out_specs=ANY, out_shape=SDS(arr.shape, arr.dtype),
        scratch_shapes=[pltpu.SemaphoreType.DMA((N_DEV - 1,)), pltpu.SemaphoreType.DMA((N_DEV - 1,)),
                        pltpu.SemaphoreType.DMA(())])(arr)


HBM = pl.BlockSpec(memory_space=pltpu.HBM)
SEM = pl.BlockSpec(memory_space=pltpu.SEMAPHORE)
EFFECT = pltpu.SideEffectType.DATAFLOW_SIDE_EFFECTING


def _split_copies(x_ref, land_ref, send, recv, scatter):
    me = _position()
    copies = []
    for rel in range(1, N_DEV):
        peer = _flip(me, rel)
        copies.append(pltpu.make_async_remote_copy(
            src_ref=x_ref.at[_index(peer)] if scatter else x_ref, dst_ref=land_ref.at[_index(me)],
            send_sem=send.at[rel - 1], recv_sem=recv.at[rel - 1], device_id=peer, device_id_type=MESH))
    return copies


def _copies_start(arr, scatter, name):
    shape = arr.shape if scatter else (N_DEV,) + arr.shape

    def body(x_ref, land_ref, send, recv, x_thru, land_thru, token):
        for cp in _split_copies(x_ref, land_ref, send, recv, scatter):
            cp.start()
        token[...] = jnp.zeros_like(token)

    sems = pltpu.SemaphoreType.DMA((N_DEV - 1,))
    send, recv, x_thru, land_thru, token = pl.pallas_call(
        body, name=name,
        out_shape=(sems, sems, pltpu.HBM(arr.shape, arr.dtype), pltpu.HBM(shape, arr.dtype), SDS((8, LANE), F32)),
        in_specs=(HBM, HBM), out_specs=(SEM, SEM, HBM, HBM, pl.BlockSpec(memory_space=pltpu.VMEM)),
        input_output_aliases={0: 2, 1: 3}, compiler_params=pltpu.CompilerParams(has_side_effects=EFFECT),
    )(pltpu.with_memory_space_constraint(arr, pltpu.HBM), pltpu.with_memory_space_constraint(lax.empty(shape, arr.dtype), pltpu.HBM))
    return (send, recv, x_thru, land_thru, scatter, name), token[0, 0]


def _copies_wait(handle, after):
    send, recv, x_thru, land_thru, scatter, name = handle

    def body(x_ref, land_ref, send_ref, recv_ref, after_ref, x_dead, got_ref):
        for cp in _split_copies(x_ref, land_ref, send_ref, recv_ref, scatter):
            cp.wait_send()
            cp.wait_recv()

    return pl.pallas_call(
        body, name=name + "_wait",
        out_shape=(pltpu.HBM(x_thru.shape, x_thru.dtype), pltpu.HBM(land_thru.shape, land_thru.dtype)),
        in_specs=(HBM, HBM, SEM, SEM, ANY), out_specs=(HBM, HBM), input_output_aliases={0: 0, 1: 1},
        compiler_params=pltpu.CompilerParams(has_side_effects=EFFECT),
    )(x_thru, land_thru, send, recv, after)[1]


def _with_own(got, own):
    return lax.dynamic_update_index_in_dim(got, own.astype(got.dtype), _index(_position()), 0)


def _adamw(parts, w, m, v, name):
    if not isinstance(parts, (list, tuple)):
        parts = [parts]
    layers = len(parts)
    r, c = parts[0].shape[1:]
    assert w.shape == (layers * r, c), (w.shape, parts[0].shape, layers)
    tr = _tile(r, 128, 16)
    per = r // tr
    c1 = 1.0 / (1.0 - ADAM_B1 ** ADAM_STEP)
    c2 = 1.0 / (1.0 - ADAM_B2 ** ADAM_STEP)

    def body(*refs):
        p_refs = refs[:layers]
        w_ref, m_ref, v_ref, g_ref, d_ref, nm_ref, nv_ref = refs[layers:]

        def update(p_ref):
            g = p_ref[0].astype(F32)
            for s in range(1, N_DEV):
                g = g + p_ref[s].astype(F32)
            nm = ADAM_B1 * m_ref[...] + (1.0 - ADAM_B1) * g
            nv = ADAM_B2 * v_ref[...] + (1.0 - ADAM_B2) * (g * g)
            g_ref[...] = g
            nm_ref[...] = nm
            nv_ref[...] = nv
            d_ref[...] = -ADAM_LR * ((nm * c1) / (jnp.sqrt(nv * c2) + ADAM_EPS) + ADAM_WD * w_ref[...])

        if layers == 1:
            update(p_refs[0])
        else:
            for j in range(layers):
                pl.when(pl.program_id(0) == j)(functools.partial(update, p_refs[j]))

    p_specs = [pl.BlockSpec((N_DEV, tr, c), functools.partial(lambda l, i, j: (0, jnp.where(l == j, i, 0), 0), j=j))
               for j in range(layers)]
    row = pl.BlockSpec((tr, c), lambda l, i: (l * per + i, 0))
    return pl.pallas_call(
        body, name=name, grid=(layers, per), in_specs=p_specs + [row, row, row],
        out_specs=[row] * 4, out_shape=[SDS(w.shape, F32)] * 4, compiler_params=_params("arbitrary", "arbitrary"))(*parts, w, m, v)


PACK = 8 * LANE


def _pack(arrs):
    flat = []
    for a in arrs:
        a = a.reshape(-1).astype(F32)
        flat.append(jnp.pad(a, (0, (-a.shape[0]) % PACK)))
    return jnp.concatenate(flat).reshape(-1, LANE)


def _unpack(packed, shapes, lead=()):
    flat = packed.reshape(lead + (-1,))
    out, off = [], 0
    for shp in shapes:
        n = int(np.prod(shp))
        out.append(flat[..., off:off + n].reshape(lead + tuple(shp)))
        off += n + (-n) % PACK
    return out


def _pad_to(a, axis, size):
    pad = [(0, 0)] * a.ndim
    pad[axis] = (0, size - a.shape[axis])
    return jnp.pad(a, pad)


def _cols_from_shards(g):
    return jnp.transpose(g, (1, 0, 2)).reshape(g.shape[1], -1)


def _cols_to_shards(a):
    c = a.shape[-1] // N_DEV
    a = a.reshape(a.shape[:-1] + (N_DEV, c))
    return jnp.moveaxis(a, -2, 0)


def _rows_to_shards(a):
    r = a.shape[-2] // N_DEV
    a = a.reshape(a.shape[:-2] + (N_DEV, r, a.shape[-1]))
    return jnp.moveaxis(a, -3, 0)


WEIGHTS = ("mix_norm", "ffn_norm", "att_w_qkv", "att_q_gain", "att_k_gain", "att_rel_bias", "att_w_o", "pool_w",
           "pool_scale", "gdn_w_in", "gdn_conv", "gdn_a_log", "gdn_dt_bias", "gdn_o_gain", "gdn_w_o", "ffn_w_up",
           "ffn_conv", "ffn_w_down")
REPLICATED = ("mix_norm", "ffn_norm", "att_q_gain", "att_k_gain", "pool_scale", "gdn_a_log", "gdn_dt_bias", "gdn_o_gain")
SMALL_SHARDED = ("att_rel_bias", "gdn_conv", "ffn_conv")
BIG = ("att_w_qkv", "att_w_o", "pool_w", "gdn_w_in", "gdn_w_o", "ffn_w_up", "ffn_w_down")
KEEP_F32 = ("pool_w",)


def _memo(build):
    cache = []

    def entry(after):
        if not cache:
            cache.append(build(after))
        return cache[0]

    return entry


def _assemble_weights(w, get, small, f):
    d = w["mix_norm"].shape[1]
    nk = d // HEAD
    hv = 2 * nk
    fp = -(-f // FF_PAD) * FF_PAD
    out = dict(mix_norm=w["mix_norm"], ffn_norm=w["ffn_norm"], att=[], pool=[], gdn=[], ffn=[])

    def att(j, after):
        return dict(wqkv=_cols_from_shards(get("att_w_qkv", j, after)), wo=get("att_w_o", j, after).reshape(d, d),
                    qg=w["att_q_gain"][j:j + 1], kg=w["att_k_gain"][j:j + 1], rel=small["att_rel_bias"][j])

    def pool(j, after):
        g = get("pool_w", j, after)
        return dict(w=jnp.transpose(g, (1, 0, 2, 3)).reshape(g.shape[1], g.shape[3], g.shape[3]),
                    scale=w["pool_scale"][j:j + 1])

    def gdn(j, after):
        win = _cols_from_shards(get("gdn_w_in", j, after))
        nm = 6 * d
        wab = jnp.concatenate([_pad_to(win[:, nm:nm + hv], 1, LANE), _pad_to(win[:, nm + hv:], 1, LANE)], axis=1)
        return dict(wmain=win[:, :nm], wab=wab, cw=_pad_to(small["gdn_conv"][j], 0, 8),
                    a_log=_pad_to(w["gdn_a_log"][j:j + 1], 1, LANE), dt_bias=_pad_to(w["gdn_dt_bias"][j:j + 1], 1, LANE),
                    o_gain=w["gdn_o_gain"][j:j + 1], wo=get("gdn_w_o", j, after).reshape(2 * d, d))

    def ffn(i, after):
        g = get("ffn_w_up", i, after)
        half = N_DEV // 2
        tail = [jnp.zeros((d, fp - f), g.dtype)] if fp > f else []
        return dict(wu=jnp.concatenate([g[k] for k in range(half)] + tail, axis=1),
                    wg=jnp.concatenate([g[k] for k in range(half, N_DEV)] + tail, axis=1),
                    cw=_pad_to(_pad_to(small["ffn_conv"][i], 0, 8), 1, fp),
                    wd=_pad_to(get("ffn_w_down", i, after).reshape(f, d), 0, fp))

    for key, build, count in (("att", att, w["att_w_qkv"].shape[0]), ("pool", pool, w["pool_w"].shape[0]),
                              ("gdn", gdn, w["gdn_w_in"].shape[0]), ("ffn", ffn, w["ffn_w_up"].shape[0])):
        out[key] = [_memo(functools.partial(build, j)) for j in range(count)]
    return out


def _full_gradients(grads, w, f):
    d = w["mix_norm"].shape[1]
    nk = d // HEAD
    hv = 2 * nk
    depth = len(grads["ffn"])
    att = [grads["mix"][i] for i in range(depth) if i % 3 == 0]
    pool = [grads["mix"][i] for i in range(depth) if i % 3 == 1]
    gdn = [grads["mix"][i] for i in range(depth) if i % 3 == 2]
    ffn = grads["ffn"]
    win = [jnp.concatenate([g["wmain"], g["wab"][:, :hv], g["wab"][:, LANE:LANE + hv]], axis=1) for g in gdn]
    return dict(
        mix_norm=jnp.concatenate([g["gain"] for g in grads["mix"]], axis=0),
        ffn_norm=jnp.concatenate([g["gain"] for g in ffn], axis=0),
        att_w_qkv=jnp.stack([g["wqkv"] for g in att]),
        att_q_gain=jnp.concatenate([g["qg"] for g in att], axis=0),
        att_k_gain=jnp.concatenate([g["kg"] for g in att], axis=0),
        att_rel_bias=jnp.stack([g["rel"] for g in att]),
        att_w_o=jnp.stack([g["wo"] for g in att]),
        pool_w=jnp.stack([g["w"] for g in pool]),
        pool_scale=jnp.concatenate([g["scale"] for g in pool], axis=0),
        gdn_w_in=jnp.stack(win),
        gdn_conv=jnp.stack([g["cw"][:GDN_CONV] for g in gdn]),
        gdn_a_log=jnp.concatenate([g["a_log"][:, :hv] for g in gdn], axis=0),
        gdn_dt_bias=jnp.concatenate([g["dt_bias"][:, :hv] for g in gdn], axis=0),
        gdn_o_gain=jnp.concatenate([g["o_gain"] for g in gdn], axis=0),
        gdn_w_o=jnp.stack([g["wo"] for g in gdn]),
        ffn_w_up=jnp.stack([jnp.concatenate([g["wu"][:, :f], g["wg"][:, :f]], axis=1) for g in ffn]),
        ffn_conv=jnp.stack([g["cw"][:FFN_CONV, :f] for g in ffn]),
        ffn_w_down=jnp.stack([g["wd"][:f] for g in ffn]))


ROW_SHARDED = ("att_w_o", "gdn_w_o", "ffn_w_down")


def _to_shards(name, full):
    if name == "pool_w":
        r = full.shape[2] // N_DEV
        a = full.reshape(full.shape[:2] + (N_DEV, r, full.shape[3]))
        return jnp.moveaxis(a, 2, 0)
    return _rows_to_shards(full) if name in ROW_SHARDED else _cols_to_shards(full)


def kernel(x, mix_norm, ffn_norm, att_w_qkv, att_q_gain, att_k_gain, att_rel_bias, att_w_o, pool_w, pool_scale, gdn_w_in, gdn_conv, gdn_a_log, gdn_dt_bias, gdn_o_gain, gdn_w_o, ffn_w_up, ffn_conv, ffn_w_down, loss_target, m_mix_norm, m_ffn_norm, m_att_w_qkv, m_att_q_gain, m_att_k_gain, m_att_rel_bias, m_att_w_o, m_pool_w, m_pool_scale, m_gdn_w_in, m_gdn_conv, m_gdn_a_log, m_gdn_dt_bias, m_gdn_o_gain, m_gdn_w_o, m_ffn_w_up, m_ffn_conv, m_ffn_w_down, v_mix_norm, v_ffn_norm, v_att_w_qkv, v_att_q_gain, v_att_k_gain, v_att_rel_bias, v_att_w_o, v_pool_w, v_pool_scale, v_gdn_w_in, v_gdn_conv, v_gdn_a_log, v_gdn_dt_bias, v_gdn_o_gain, v_gdn_w_o, v_ffn_w_up, v_ffn_conv, v_ffn_w_down):
    w = dict(zip(WEIGHTS, (mix_norm, ffn_norm, att_w_qkv, att_q_gain, att_k_gain, att_rel_bias, att_w_o, pool_w, pool_scale, gdn_w_in, gdn_conv, gdn_a_log, gdn_dt_bias, gdn_o_gain, gdn_w_o, ffn_w_up, ffn_conv, ffn_w_down)))
    m = dict(zip(WEIGHTS, (m_mix_norm, m_ffn_norm, m_att_w_qkv, m_att_q_gain, m_att_k_gain, m_att_rel_bias, m_att_w_o, m_pool_w, m_pool_scale, m_gdn_w_in, m_gdn_conv, m_gdn_a_log, m_gdn_dt_bias, m_gdn_o_gain, m_gdn_w_o, m_ffn_w_up, m_ffn_conv, m_ffn_w_down)))
    v = dict(zip(WEIGHTS, (v_mix_norm, v_ffn_norm, v_att_w_qkv, v_att_q_gain, v_att_k_gain, v_att_rel_bias, v_att_w_o, v_pool_w, v_pool_scale, v_gdn_w_in, v_gdn_conv, v_gdn_a_log, v_gdn_dt_bias, v_gdn_o_gain, v_gdn_w_o, v_ffn_w_up, v_ffn_conv, v_ffn_w_down)))

    me = _index(_position())
    d = mix_norm.shape[1]
    hv = 2 * (d // HEAD)
    f = ffn_w_down.shape[1] * N_DEV
    depth = ffn_w_up.shape[0]

    small_shapes = [w[n].shape for n in SMALL_SHARDED]
    small_g = _all_gather(_pack([w[n] for n in SMALL_SHARDED]), "gather_small")
    small = {}
    for n, a in zip(SMALL_SHARDED, _unpack(small_g, small_shapes, lead=(N_DEV,))):
        small[n] = jnp.moveaxis(a, 0, -2).reshape(a.shape[1:-1] + (N_DEV * a.shape[-1],))
    order = []
    for i in range(depth):
        order += [[("att_w_qkv", i // 3), ("att_w_o", i // 3)], [("pool_w", i // 3)], [("gdn_w_in", i // 3), ("gdn_w_o", i // 3)]][i % 3]
        order += [("ffn_w_up", i), ("ffn_w_down", i)]
    after_small = small_g[0, 0, 0] * 0.0
    local = {(n, j): (w[n][j] + after_small if n in KEEP_F32 else (w[n][j] + after_small).astype(BF)) for n, j in order}
    arriving, zero = {}, 0.0
    for n, j in order:
        arriving[(n, j)], tok = _copies_start(local[(n, j)], False, f"gather_{n}_{j}")
        zero = zero + tok

    def get(n, j, after):
        return _with_own(_copies_wait(arriving[(n, j)], after), local[(n, j)])

    ordered = dict(w, mix_norm=mix_norm + zero, ffn_norm=ffn_norm + zero)
    full = _assemble_weights(ordered, get, small, f)

    leaving = {}

    def sink(kind, i, g):
        j = i // 3
        if kind == "ffn":
            c = 2 * f // N_DEV
            up = [g[key][:, k * c:(k + 1) * c] for key in ("wu", "wg") for k in range(N_DEV // 2)]
            pieces = [("ffn_w_up", i, jnp.stack(up)), ("ffn_w_down", i, _rows_to_shards(g["wd"][:f]))]
        elif kind == "att":
            pieces = [("att_w_qkv", j, _cols_to_shards(g["wqkv"])), ("att_w_o", j, _rows_to_shards(g["wo"]))]
        elif kind == "pool":
            pieces = [("pool_w", j, _to_shards("pool_w", g["w"][None])[:, 0])]
        else:
            win = jnp.concatenate([g["wmain"], g["wab"][:, :hv], g["wab"][:, LANE:LANE + hv]], axis=1)
            pieces = [("gdn_w_in", j, _cols_to_shards(win)), ("gdn_w_o", j, _rows_to_shards(g["wo"]))]
        tok = 0.0
        for n, l, shards in pieces:
            shards = shards if n in KEEP_F32 else shards.astype(BF)
            handle, t = _copies_start(shards, True, f"exchange_{n}_{l}")
            leaving[(n, l)] = (handle, lax.dynamic_index_in_dim(shards, me, 0, keepdims=False))
            tok = tok + t
        return tok

    loss_row, dx, grads = _local_step(x[0], loss_target[0], full, sink)
    gfull = _full_gradients(grads, w, f)

    out = {}
    for n in BIG:
        c = w[n].shape[-1]
        parts = []
        for l in range(w[n].shape[0]):
            handle, own = leaving[(n, l)]
            parts.append(_with_own(_copies_wait(handle, dx), own).reshape(N_DEV, -1, c))
        res = _adamw(parts, w[n].reshape(-1, c), m[n].reshape(-1, c), v[n].reshape(-1, c), f"adamw_{n}")
        out[n] = [a.reshape(w[n].shape) for a in res]
    sparts = _exchange(jnp.stack([_pack([_to_shards(n, gfull[n])[k] for n in SMALL_SHARDED]) for k in range(N_DEV)]),
                       "exchange_small")
    res = _adamw(sparts, *[_pack([t[n] for n in SMALL_SHARDED]) for t in (w, m, v)], "adamw_small")
    for n, *vals in zip(SMALL_SHARDED, *[_unpack(a, small_shapes) for a in res]):
        out[n] = vals
    rep_shapes = [w[n].shape for n in REPLICATED]
    rparts = _all_gather(_pack([gfull[n] for n in REPLICATED] + [loss_row[:, 0:1]]), "gather_replicated_grads")
    pad = jnp.zeros((1, 1), F32)
    res = _adamw(rparts, *[_pack([t[n] for n in REPLICATED] + [pad]) for t in (w, m, v)], "adamw_replicated")
    for n, *vals in zip(REPLICATED, *[_unpack(a, rep_shapes) for a in res]):
        out[n] = vals
    loss = jnp.sum(_unpack(rparts, rep_shapes + [(1, 1)], lead=(N_DEV,))[-1])
    return (loss, dx[None], *[out[n][0] for n in WEIGHTS], *[out[n][1] for n in WEIGHTS],
            *[out[n][2] for n in WEIGHTS], *[out[n][3] for n in WEIGHTS])
```

```python
import functools

import numpy as np
import jax
import jax.numpy as jnp
from jax import lax
from jax.experimental import pallas as pl
from jax.experimental.pallas import tpu as pltpu

F32 = jnp.float32
BF = jnp.bfloat16
SDS = jax.ShapeDtypeStruct

EPS = 1e-6
MASK_VALUE = -1e30
CHUNK = 64
LEFT_CHUNKS = 8
BAND_LEFT = LEFT_CHUNKS * CHUNK
BAND = BAND_LEFT + CHUNK
MAX_REL = 256
NUM_REL = (CHUNK - 1) + MAX_REL + 1
HEAD = 128
LANE = 128
HALO = 16
POOL_WINDOWS = (2, 4, 8, 16)
GDN_CONV = 4
FFN_CONV = 3
FF_PAD = 512
N_DEV = 8
AXES = ("x", "y", "c")
VMEM_LIMIT = 56 * 1024 * 1024

ADAM_LR = 0.001
ADAM_B1 = 0.9
ADAM_B2 = 0.999
ADAM_EPS = 1e-08
ADAM_WD = 0.01
ADAM_STEP = 10

HI = lax.Precision.HIGHEST
NT = (((1,), (1,)), ((), ()))
TN = (((0,), (0,)), ((), ()))


def _tile(n, target, mult=LANE):
    if n <= target:
        return n
    t = (target // mult) * mult
    while t >= mult:
        if n % t == 0:
            return t
        t -= mult
    return n


def _params(*sem):
    return pltpu.CompilerParams(dimension_semantics=sem, vmem_limit_bytes=VMEM_LIMIT)


def _silu(x):
    return x / (1.0 + jnp.exp(-x))


@functools.partial(jax.custom_vjp, nondiff_argnums=(1,))
def _shift(x, k):
    return pltpu.roll(x, k % x.shape[0], axis=0)


def _shift_fwd(x, k):
    return _shift(x, k), None


def _shift_bwd(k, _, g):
    return (pltpu.roll(g, (-k) % g.shape[0], axis=0),)


_shift.defvjp(_shift_fwd, _shift_bwd)


def _mm(a, b, *, name, ta=False, tb=False, out_dtype=BF, res=None, tm=1024, tn=1024, tk=2048, n_outer=False):
    m, k = (a.shape[1], a.shape[0]) if ta else a.shape
    n, kb = (b.shape[0], b.shape[1]) if tb else (b.shape[1], b.shape[0])
    assert k == kb, (a.shape, b.shape, ta, tb)
    if ta or a.dtype != BF:
        tk = min(tk, 1024)
    tm, tn, tk = _tile(m, tm), _tile(n, tn), _tile(k, tk)
    nk = k // tk
    dims = (((0 if ta else 1,), (1 if tb else 0,)), ((), ()))

    def body(*refs):
        if res is None:
            a_ref, b_ref, o_ref, acc = refs
        else:
            a_ref, b_ref, r_ref, o_ref, acc = refs
        prod = lax.dot_general(a_ref[...].astype(BF), b_ref[...].astype(BF), dims, preferred_element_type=F32)
        if nk == 1:
            if res is not None:
                prod = prod + r_ref[...].astype(F32)
            o_ref[...] = prod.astype(out_dtype)
            return
        kk = pl.program_id(2)

        @pl.when(kk == 0)
        def _():
            acc[...] = prod

        @pl.when(kk > 0)
        def _():
            acc[...] += prod

        @pl.when(kk == nk - 1)
        def _():
            r = acc[...]
            if res is not None:
                r = r + r_ref[...].astype(F32)
            o_ref[...] = r.astype(out_dtype)

    def order(fn):
        return (lambda j, i, q: fn(i, j, q)) if n_outer else fn

    a_spec = pl.BlockSpec((tk, tm), order(lambda i, j, q: (q, i))) if ta else pl.BlockSpec((tm, tk), order(lambda i, j, q: (i, q)))
    b_spec = pl.BlockSpec((tn, tk), order(lambda i, j, q: (j, q))) if tb else pl.BlockSpec((tk, tn), order(lambda i, j, q: (q, j)))
    o_spec = pl.BlockSpec((tm, tn), order(lambda i, j, q: (i, j)))
    ins, specs = [a, b], [a_spec, b_spec]
    if res is not None:
        ins.append(res)
        specs.append(o_spec)
    return pl.pallas_call(
        body, name=name, grid=(n // tn, m // tm, nk) if n_outer else (m // tm, n // tn, nk), in_specs=specs, out_specs=o_spec,
        out_shape=SDS((m, n), out_dtype), scratch_shapes=[pltpu.VMEM((tm, tn), F32)],
        compiler_params=_params("parallel", "parallel", "arbitrary"))(*ins)


def _rms(x, gain):
    return x * lax.rsqrt(jnp.mean(x * x, axis=-1, keepdims=True) + EPS) * gain


def _rmsnorm_fwd(x, gain, name):
    s, d = x.shape
    ts = _tile(s, 256, LANE)

    def body(x_ref, g_ref, o_ref, t_ref):
        h = _rms(x_ref[...], g_ref[...])
        o_ref[...] = h.astype(BF)
        t_ref[...] = h.T.astype(BF)

    return pl.pallas_call(
        body, name=name, grid=(s // ts,),
        in_specs=[pl.BlockSpec((ts, d), lambda i: (i, 0)), pl.BlockSpec((1, d), lambda i: (0, 0))],
        out_specs=[pl.BlockSpec((ts, d), lambda i: (i, 0)), pl.BlockSpec((d, ts), lambda i: (0, i))],
        out_shape=[SDS((s, d), BF), SDS((d, s), BF)], compiler_params=_params("parallel"))(x, gain)


def _rmsnorm_bwd(x, gain, dh, dres, name):
    s, d = x.shape
    ts = _tile(s, 256, 16)

    grp = 16

    def body(x_ref, g_ref, dh_ref, dr_ref, dx_ref, dxb_ref, dg_ref):
        i = pl.program_id(0)
        gain_row = g_ref[...]

        def piece(t, dg_rows):
            rows = pl.ds(pl.multiple_of(t * grp, grp), grp)
            xv = x_ref[rows, :]
            dh = dh_ref[rows, :].astype(F32)
            r = lax.rsqrt(jnp.mean(xv * xv, axis=-1, keepdims=True) + EPS)
            xh = xv * r
            dxh = dh * gain_row
            dx = r * (dxh - xh * jnp.mean(dxh * xh, axis=-1, keepdims=True)) + dr_ref[rows, :]
            dx_ref[rows, :] = dx
            dxb_ref[rows, :] = dx.astype(BF)
            return dg_rows + dh * xh

        def pieces(t, dg_rows):
            for q in range(together):
                dg_rows = piece(t * together + q, dg_rows)
            return dg_rows

        together = 4
        dg = jnp.sum(lax.fori_loop(0, ts // (grp * together), pieces, jnp.zeros((grp, d), F32)), axis=0, keepdims=True)

        @pl.when(i == 0)
        def _():
            dg_ref[...] = dg

        @pl.when(i > 0)
        def _():
            dg_ref[...] += dg

    row = pl.BlockSpec((ts, d), lambda i: (i, 0))
    vec = pl.BlockSpec((1, d), lambda i: (0, 0))
    return pl.pallas_call(
        body, name=name, grid=(s // ts,), in_specs=[row, vec, row, row], out_specs=[row, row, vec],
        out_shape=[SDS((s, d), F32), SDS((s, d), BF), SDS((1, d), F32)], compiler_params=_params("arbitrary"))(x, gain, dh, dres)


def _halo_index(rows_per_block):
    per = rows_per_block // HALO
    return lambda rb: jnp.maximum(rb * per - 1, 0)


FFN_SUB = 256


def _ext_rows(cur, prev, nxt, r0, rows, tile, keep_prev=1.0, keep_next=1.0):
    n = cur.shape[0]
    parts = []
    if r0 > 0:
        parts.append(cur[pl.ds(r0 - HALO, HALO + rows), tile].astype(F32))
    else:
        head = jnp.zeros((HALO, LANE), F32) if prev is None else prev[:, tile].astype(F32) * keep_prev
        parts += [head, cur[pl.ds(0, rows), tile].astype(F32)]
    if nxt is not False:
        if r0 + rows < n:
            parts.append(cur[pl.ds(r0 + rows, HALO), tile].astype(F32))
        else:
            parts.append(jnp.zeros((HALO, LANE), F32) if nxt is None else nxt[:, tile].astype(F32) * keep_next)
    return jnp.concatenate(parts, axis=0)


def _ffn_act_fwd(u, g, cw, name):
    s, f = u.shape
    r, tc = _tile(s, 512, HALO), _tile(f, 512)
    sub = _tile(r, FFN_SUB, HALO)
    hidx = _halo_index(r)

    def body(uc, uh, gc, cw_ref, o_ref, t_ref):
        keep = jnp.where(pl.program_id(1) == 0, 0.0, 1.0)
        for lt in range(tc // LANE):
            tile = pl.ds(lt * LANE, LANE)
            c0, c1, c2 = cw_ref[0:1, tile], cw_ref[1:2, tile], cw_ref[2:3, tile]
            for r0 in range(0, r, sub):
                u_ext = _ext_rows(uc, uh, False, r0, sub, tile, keep)
                acc = u_ext * c2 + pltpu.roll(u_ext, 1, axis=0) * c1 + pltpu.roll(u_ext, 2, axis=0) * c0
                a = _silu(acc)[HALO:] * gc[pl.ds(r0, sub), tile].astype(F32)
                o_ref[pl.ds(r0, sub), tile] = a.astype(BF)
                t_ref[tile, pl.ds(r0, sub)] = a.T.astype(BF)

    cur = pl.BlockSpec((r, tc), lambda j, rb: (rb, j))
    return pl.pallas_call(
        body, name=name, grid=(f // tc, s // r),
        in_specs=[cur, pl.BlockSpec((HALO, tc), lambda j, rb: (hidx(rb), j)), cur,
                  pl.BlockSpec((8, tc), lambda j, rb: (0, j))],
        out_specs=[cur, pl.BlockSpec((tc, r), lambda j, rb: (j, rb))],
        out_shape=[SDS((s, f), BF), SDS((f, s), BF)], compiler_params=_params("parallel", "parallel"))(u, u, g, cw)


def _ffn_act_bwd(u, g, cw, da, name):
    s, f = u.shape
    r, tc = _tile(s, 512, HALO), _tile(f, 512)
    sub = _tile(r, FFN_SUB, HALO)
    nb = s // r
    per = r // HALO
    hidx = _halo_index(r)
    nidx = lambda rb: jnp.minimum((rb + 1) * per, s // HALO - 1)
    n = sub + 2 * HALO

    def body(uc, uh, un, gc, gn, dac, dan, cw_ref, du_ref, dg_ref, dcw_ref):
        rb = pl.program_id(1)
        first = jnp.where(rb == 0, 0.0, 1.0)
        last = jnp.where(rb == nb - 1, 0.0, 1.0)
        wrow = lax.broadcasted_iota(jnp.int32, (8, LANE), 0)

        @pl.when(rb == 0)
        def _():
            dcw_ref[...] = jnp.zeros_like(dcw_ref)

        for lt in range(tc // LANE):
            tile = pl.ds(lt * LANE, LANE)
            c0, c1, c2 = cw_ref[0:1, tile], cw_ref[1:2, tile], cw_ref[2:3, tile]
            dcw = jnp.zeros((8, LANE), F32)
            for r0 in range(0, r, sub):
                u_ext = _ext_rows(uc, uh, un, r0, sub, tile, first, last)
                g_ext = _ext_rows(gc, None, gn, r0, sub, tile, 1.0, last)
                da_ext = _ext_rows(dac, None, dan, r0, sub, tile, 1.0, last)
                sh1, sh2 = pltpu.roll(u_ext, 1, axis=0), pltpu.roll(u_ext, 2, axis=0)
                acc = u_ext * c2 + sh1 * c1 + sh2 * c0
                sg = 1.0 / (1.0 + jnp.exp(-acc))
                dg_ref[pl.ds(r0, sub), tile] = (da_ext * (acc * sg))[HALO:HALO + sub].astype(BF)
                dacc = da_ext * g_ext * (sg * (1.0 + acc * (1.0 - sg)))
                du = dacc * c2 + pltpu.roll(dacc, n - 1, axis=0) * c1 + pltpu.roll(dacc, n - 2, axis=0) * c0
                du_ref[pl.ds(r0, sub), tile] = du[HALO:HALO + sub].astype(BF)
                for j, tap in enumerate((sh2, sh1, u_ext)):
                    dcw = dcw + jnp.where(wrow == j, jnp.sum((dacc * tap)[HALO:HALO + sub], axis=0, keepdims=True), 0.0)
            dcw_ref[:, tile] += dcw

    cur = pl.BlockSpec((r, tc), lambda j, rb: (rb, j))
    prev = pl.BlockSpec((HALO, tc), lambda j, rb: (hidx(rb), j))
    nxt = pl.BlockSpec((HALO, tc), lambda j, rb: (nidx(rb), j))
    wspec = pl.BlockSpec((8, tc), lambda j, rb: (0, j))
    return pl.pallas_call(
        body, name=name, grid=(f // tc, nb),
        in_specs=[cur, prev, nxt, cur, nxt, cur, nxt, wspec],
        out_specs=[cur, cur, wspec], out_shape=[SDS((s, f), BF), SDS((s, f), BF), SDS((8, f), F32)],
        compiler_params=_params("parallel", "arbitrary"))(u, u, u, g, g, da, da, cw)


def _loss_head(y, target, name):
    s, d = y.shape
    ts = _tile(s, 256, 16)

    def body(y_ref, t_ref, dy_ref, dyb_ref, l_ref):
        i = pl.program_id(0)
        err = y_ref[...] - t_ref[...]
        dy_ref[...] = err * (1.0 / d)
        dyb_ref[...] = (err * (1.0 / d)).astype(BF)
        part = jnp.zeros((1, LANE), F32) + 0.5 * jnp.sum(jnp.sum(err * err, axis=1, keepdims=True), axis=0, keepdims=True) / d

        @pl.when(i == 0)
        def _():
            l_ref[...] = part

        @pl.when(i > 0)
        def _():
            l_ref[...] += part

    row = pl.BlockSpec((ts, d), lambda i: (i, 0))
    return pl.pallas_call(
        body, name=name, grid=(s // ts,), in_specs=[row, row],
        out_specs=[row, row, pl.BlockSpec((1, LANE), lambda i: (0, 0))],
        out_shape=[SDS((s, d), F32), SDS((s, d), BF), SDS((1, LANE), F32)], compiler_params=_params("arbitrary"))(y, target)


def _rel_index():
    rel = BAND_LEFT + np.arange(CHUNK)[:, None] - np.arange(BAND)[None, :]
    return (np.clip(rel, -(CHUNK - 1), MAX_REL) + (CHUNK - 1)).reshape(1, CHUNK * BAND).astype(np.int32)


def _onehot(idx_row):
    rows = lax.broadcasted_iota(jnp.int32, (NUM_REL, idx_row.shape[1]), 0)
    return jnp.where(rows == idx_row, 1.0, 0.0).astype(F32)


def _bias_expand(rel_bias, name):
    h = rel_bias.shape[0]
    n = CHUNK * BAND
    tn = n // 8

    def body(rb_ref, idx_ref, o_ref):
        o_ref[...] = jnp.dot(rb_ref[...], _onehot(idx_ref[...]), precision=HI, preferred_element_type=F32)

    out = pl.pallas_call(
        body, name=name, grid=(n // tn,),
        in_specs=[pl.BlockSpec((h, NUM_REL), lambda j: (0, 0)), pl.BlockSpec((1, tn), lambda j: (0, j))],
        out_specs=pl.BlockSpec((h, tn), lambda j: (0, j)), out_shape=SDS((h, n), F32),
        compiler_params=_params("parallel"))(rel_bias, jnp.asarray(_rel_index()))
    return out.reshape(h, CHUNK, BAND)


def _bias_reduce(dbias, name):
    h = dbias.shape[0]
    n = CHUNK * BAND
    tn = n // 8

    def body(db_ref, idx_ref, o_ref):
        j = pl.program_id(0)
        part = lax.dot_general(db_ref[...], _onehot(idx_ref[...]), NT, precision=HI, preferred_element_type=F32)

        @pl.when(j == 0)
        def _():
            o_ref[...] = part

        @pl.when(j > 0)
        def _():
            o_ref[...] += part

    return pl.pallas_call(
        body, name=name, grid=(n // tn,),
        in_specs=[pl.BlockSpec((h, tn), lambda j: (0, j)), pl.BlockSpec((1, tn), lambda j: (0, j))],
        out_specs=pl.BlockSpec((h, NUM_REL), lambda j: (0, 0)), out_shape=SDS((h, NUM_REL), F32),
        compiler_params=_params("arbitrary"))(dbias.reshape(h, n), jnp.asarray(_rel_index()))


def _headnorm(x, gain):
    outs = []
    for hh in range(x.shape[1] // HEAD):
        xh = x[:, hh * HEAD:(hh + 1) * HEAD]
        outs.append(xh * lax.rsqrt(jnp.mean(xh * xh, axis=-1, keepdims=True) + EPS) * gain)
    return jnp.concatenate(outs, axis=1)


def _qkv_post_fwd(qkv, qg, kg, name):
    s, d3 = qkv.shape
    d = d3 // 3
    r = BAND_LEFT
    nb = s // r

    def body(x_ref, qg_ref, kg_ref, q_ref, k_ref, v_ref):
        i = pl.program_id(0)
        keep = jnp.where(i == 0, 0.0, 1.0)
        q_ref[...] = _headnorm(x_ref[:, 0:d].astype(F32), qg_ref[...]).astype(BF)
        k_ref[...] = (_headnorm(x_ref[:, d:2 * d].astype(F32), kg_ref[...]) * keep).astype(BF)
        v_ref[...] = (x_ref[:, 2 * d:].astype(F32) * keep).astype(BF)

    prev = lambda i: (jnp.maximum(i - 1, 0), 0)
    vec = pl.BlockSpec((1, HEAD), lambda i: (0, 0))
    return pl.pallas_call(
        body, name=name, grid=(nb + 1,),
        in_specs=[pl.BlockSpec((r, d3), prev), vec, vec],
        out_specs=[pl.BlockSpec((r, d), prev), pl.BlockSpec((r, d), lambda i: (i, 0)), pl.BlockSpec((r, d), lambda i: (i, 0))],
        out_shape=[SDS((s, d), BF), SDS((s + r, d), BF), SDS((s + r, d), BF)],
        compiler_params=_params("arbitrary"))(qkv, qg, kg)


def _qkv_post_bwd(qkv, qg, kg, dq, dkpad, dvpad, name):
    s, d3 = qkv.shape
    d = d3 // 3
    r = _tile(s, 256, 16)
    off = BAND_LEFT // r

    def body(x_ref, qg_ref, kg_ref, dq_ref, dk_ref, dv_ref, o_ref, dqg_ref, dkg_ref):
        i = pl.program_id(0)
        _, vq = jax.vjp(_headnorm, x_ref[:, 0:d].astype(F32), qg_ref[...])
        dxq, dqg = vq(dq_ref[...])
        _, vk = jax.vjp(_headnorm, x_ref[:, d:2 * d].astype(F32), kg_ref[...])
        dxk, dkg = vk(dk_ref[...])
        o_ref[:, 0:d] = dxq.astype(BF)
        o_ref[:, d:2 * d] = dxk.astype(BF)
        o_ref[:, 2 * d:] = dv_ref[...].astype(BF)

        @pl.when(i == 0)
        def _():
            dqg_ref[...] = dqg
            dkg_ref[...] = dkg

        @pl.when(i > 0)
        def _():
            dqg_ref[...] += dqg
            dkg_ref[...] += dkg

    vec = pl.BlockSpec((1, HEAD), lambda i: (0, 0))
    row3 = pl.BlockSpec((r, d3), lambda i: (i, 0))
    row = pl.BlockSpec((r, d), lambda i: (i, 0))
    padrow = pl.BlockSpec((r, d), lambda i: (i + off, 0))
    return pl.pallas_call(
        body, name=name, grid=(s // r,), in_specs=[row3, vec, vec, row, padrow, padrow],
        out_specs=[row3, vec, vec], out_shape=[SDS((s, d3), BF), SDS((1, HEAD), F32), SDS((1, HEAD), F32)],
        compiler_params=_params("arbitrary"))(qkv, qg, kg, dq, dkpad, dvpad)


ATT_QB = 1024


def _att_probs(q, kw, bias, c0):
    sc = _bdot(q, kw, NTB) * (HEAD ** -0.5) + bias
    lane = lax.broadcasted_iota(jnp.int32, sc.shape, 2)
    chunk = lax.broadcasted_iota(jnp.int32, sc.shape, 0) + c0
    sc = jnp.where(lane + chunk * CHUNK >= BAND_LEFT, sc, MASK_VALUE)
    p = jnp.exp(sc - jnp.max(sc, axis=-1, keepdims=True))
    return p / jnp.sum(p, axis=-1, keepdims=True)


def _attn_fwd(q, kpad, vpad, bias, name):
    s, d = q.shape
    h = d // HEAD
    sp = kpad.shape[0]
    qb = _tile(s, ATT_QB, CHUNK)
    per = qb // CHUNK

    def body(q_ref, k_ref, v_ref, b_ref, o_ref):
        c0 = pl.program_id(1) * per
        wins = [pl.ds(pl.multiple_of((c0 + cc) * CHUNK, CHUNK), BAND) for cc in range(per)]
        kw = jnp.stack([k_ref[w, :] for w in wins])
        vw = jnp.stack([v_ref[w, :] for w in wins])
        p = _att_probs(q_ref[...].reshape(per, CHUNK, HEAD), kw, b_ref[0], c0)
        o_ref[...] = _bdot(p.astype(BF), vw, NNB).reshape(qb, HEAD).astype(BF)

    qspec = pl.BlockSpec((qb, HEAD), lambda hh, i: (i, hh))
    kspec = pl.BlockSpec((sp, HEAD), lambda hh, i: (0, hh))
    return pl.pallas_call(
        body, name=name, grid=(h, s // qb),
        in_specs=[qspec, kspec, kspec, pl.BlockSpec((1, CHUNK, BAND), lambda hh, i: (hh, 0, 0))],
        out_specs=qspec, out_shape=SDS((s, d), BF), compiler_params=_params("parallel", "arbitrary"))(q, kpad, vpad, bias)


def _attn_bwd(q, kpad, vpad, bias, do, name):
    s, d = q.shape
    h = d // HEAD
    sp = kpad.shape[0]
    qb = _tile(s, ATT_QB, CHUNK)
    per = qb // CHUNK
    scale = HEAD ** -0.5

    def body(q_ref, k_ref, v_ref, b_ref, do_ref, dq_ref, dk_ref, dv_ref, db_ref):
        i = pl.program_id(1)

        @pl.when(i == 0)
        def _():
            dk_ref[...] = jnp.zeros_like(dk_ref)
            dv_ref[...] = jnp.zeros_like(dv_ref)
            db_ref[...] = jnp.zeros_like(db_ref)

        c0 = i * per
        wins = [pl.ds(pl.multiple_of((c0 + cc) * CHUNK, CHUNK), BAND) for cc in range(per)]
        kw = jnp.stack([k_ref[w, :] for w in wins])
        vw = jnp.stack([v_ref[w, :] for w in wins])
        qc = q_ref[...].reshape(per, CHUNK, HEAD)
        doc = do_ref[...].astype(BF).reshape(per, CHUNK, HEAD)
        p = _att_probs(qc, kw, b_ref[0], c0)
        dp = _bdot(doc, vw, NTB)
        ds = p * (dp - jnp.sum(p * dp, axis=-1, keepdims=True))
        db_ref[0] += jnp.sum(ds, axis=0)
        dsb = (ds * scale).astype(BF)
        dq_ref[...] = _bdot(dsb, kw, NNB).reshape(qb, HEAD)

        def union(x):
            tot = None
            for cc in range(per):
                parts = [x[cc]]
                if cc:
                    parts.insert(0, jnp.zeros((cc * CHUNK, HEAD), F32))
                if cc < per - 1:
                    parts.append(jnp.zeros(((per - 1 - cc) * CHUNK, HEAD), F32))
                piece = jnp.concatenate(parts, axis=0) if len(parts) > 1 else parts[0]
                tot = piece if tot is None else tot + piece
            return tot

        span = pl.ds(pl.multiple_of(c0 * CHUNK, CHUNK), BAND + (per - 1) * CHUNK)
        dk_ref[span, :] += union(_bdot(dsb, qc, TNB))
        dv_ref[span, :] += union(_bdot(p.astype(BF), doc, TNB))

    qspec = pl.BlockSpec((qb, HEAD), lambda hh, i: (i, hh))
    kspec = pl.BlockSpec((sp, HEAD), lambda hh, i: (0, hh))
    bspec = pl.BlockSpec((1, CHUNK, BAND), lambda hh, i: (hh, 0, 0))
    return pl.pallas_call(
        body, name=name, grid=(h, s // qb), in_specs=[qspec, kspec, kspec, bspec, qspec],
        out_specs=[qspec, kspec, kspec, bspec],
        out_shape=[SDS((s, d), F32), SDS((sp, d), F32), SDS((sp, d), F32), SDS((h, CHUNK, BAND), F32)],
        compiler_params=_params("parallel", "arbitrary"))(q, kpad, vpad, bias, do)


def _pool_tile(x_ext, gain, w4, scale, row0):
    n, d = x_ext.shape
    dg = d // len(POOL_WINDOWS)
    pos = lax.broadcasted_iota(jnp.int32, (n, 1), 0) + row0
    hn = _rms(x_ext, gain) * jnp.where(pos >= 0, 1.0, 0.0)
    outs = []
    for gi, w in enumerate(POOL_WINDOWS):
        hg = hn[:, gi * dg:(gi + 1) * dg]
        acc, k = hg, 1
        while k < w:
            acc = acc + _shift(acc, k)
            k *= 2
        inv = 1.0 / jnp.clip(pos + 1, 1, w).astype(F32)
        pooled = acc * inv - hg
        outs.append(jnp.dot(pooled.astype(BF), w4[gi].astype(BF), preferred_element_type=F32))
    return jnp.concatenate(outs, axis=1) * scale


POOL_ROWS = 128


def _pool_fwd(x, gain, w4, scale, name):
    s, d = x.shape
    r = _tile(s, POOL_ROWS, HALO)
    hidx = _halo_index(r)

    def body(xc, xh, g_ref, w_ref, s_ref, o_ref):
        rb = pl.program_id(0)
        x_ext = jnp.concatenate([xh[...], xc[...]], axis=0)
        y = _pool_tile(x_ext, g_ref[...], [w_ref[gi] for gi in range(len(POOL_WINDOWS))], s_ref[...], rb * r - HALO)
        o_ref[...] = xc[...] + y[HALO:]

    cur = pl.BlockSpec((r, d), lambda rb: (rb, 0))
    vec = pl.BlockSpec((1, d), lambda rb: (0, 0))
    return pl.pallas_call(
        body, name=name, grid=(s // r,),
        in_specs=[cur, pl.BlockSpec((HALO, d), lambda rb: (hidx(rb), 0)), vec,
                  pl.BlockSpec(w4.shape, lambda rb: (0, 0, 0)), vec],
        out_specs=cur, out_shape=SDS((s, d), F32), compiler_params=_params("parallel"))(x, x, gain, w4, scale)


def _pool_bwd(x, gain, w4, scale, dy, name):
    s, d = x.shape
    r = _tile(s, POOL_ROWS, HALO)
    nb = s // r
    hidx = _halo_index(r)

    def body(xc, xh, g_ref, w_ref, s_ref, dy_ref, dx_ref, dg_ref, dw_ref, ds_ref, carry):
        step = pl.program_id(0)
        rb = nb - 1 - step
        x_ext = jnp.concatenate([xh[...], xc[...]], axis=0)
        fn = functools.partial(_pool_tile, row0=rb * r - HALO)
        _, vjp = jax.vjp(fn, x_ext, g_ref[...], [w_ref[gi] for gi in range(len(POOL_WINDOWS))], s_ref[...])
        ct = jnp.concatenate([jnp.zeros((HALO, d), F32), dy_ref[...]], axis=0)
        dx_ext, dg, dws, dsc = vjp(ct)

        @pl.when(step == 0)
        def _():
            carry[...] = jnp.zeros_like(carry)
            dg_ref[...] = jnp.zeros_like(dg_ref)
            dw_ref[...] = jnp.zeros_like(dw_ref)
            ds_ref[...] = jnp.zeros_like(ds_ref)

        dx_ref[...] = dy_ref[...] + dx_ext[HALO:]
        dx_ref[pl.ds(r - HALO, HALO), :] += carry[...]
        carry[...] = dx_ext[:HALO]
        dg_ref[...] += dg
        for gi, dw in enumerate(dws):
            dw_ref[gi] += dw
        ds_ref[...] += dsc

    cur = pl.BlockSpec((r, d), lambda t: (nb - 1 - t, 0))
    vec = pl.BlockSpec((1, d), lambda t: (0, 0))
    wspec = pl.BlockSpec(w4.shape, lambda t: (0, 0, 0))
    return pl.pallas_call(
        body, name=name, grid=(nb,),
        in_specs=[cur, pl.BlockSpec((HALO, d), lambda t: (hidx(nb - 1 - t), 0)), vec, wspec, vec, cur],
        out_specs=[cur, vec, wspec, vec],
        out_shape=[SDS((s, d), F32), SDS((1, d), F32), SDS(w4.shape, F32), SDS((1, d), F32)],
        scratch_shapes=[pltpu.VMEM((HALO, d), F32)], compiler_params=_params("arbitrary"))(x, x, gain, w4, scale, dy)


def _gdn_post(acc, kind):
    y = _silu(acc)
    if kind != "v":
        y = y * lax.rsqrt(jnp.sum(y * y, axis=-1, keepdims=True) + EPS)
    if kind == "q":
        y = y * (HEAD ** -0.5)
    return y


GDN_SUB = 128
GDN_CONV_HEADS = 4


def _gdn_conv_fwd(proj, cw, kind, head0, nheads, name):
    s = proj.shape[0]
    r = _tile(s, 512, HALO)
    sub = _tile(r, GDN_SUB, HALO)
    hb = min(GDN_CONV_HEADS, nheads)
    assert head0 % hb == 0 and nheads % hb == 0
    tc = hb * HEAD
    hidx = _halo_index(r)

    def body(uc, uh, cw_ref, o_ref):
        keep = jnp.where(pl.program_id(1) == 0, 0.0, 1.0)
        for hh in range(hb):
            tile = pl.ds(hh * HEAD, HEAD)
            taps = [cw_ref[j:j + 1, tile] for j in range(GDN_CONV)]
            for r0 in range(0, r, sub):
                u_ext = _ext_rows(uc, uh, False, r0, sub, tile, keep)
                acc = u_ext * taps[3]
                for j in range(1, GDN_CONV):
                    acc = acc + pltpu.roll(u_ext, j, axis=0) * taps[3 - j]
                o_ref[pl.ds(r0, sub), tile] = _gdn_post(acc, kind)[HALO:].astype(BF)

    return pl.pallas_call(
        body, name=name, grid=(nheads // hb, s // r),
        in_specs=[pl.BlockSpec((r, tc), lambda j, rb: (rb, head0 // hb + j)),
                  pl.BlockSpec((HALO, tc), lambda j, rb: (hidx(rb), head0 // hb + j)),
                  pl.BlockSpec((8, tc), lambda j, rb: (0, head0 // hb + j))],
        out_specs=pl.BlockSpec((r, tc), lambda j, rb: (rb, j)), out_shape=SDS((s, nheads * HEAD), BF),
        compiler_params=_params("parallel", "parallel"))(proj, proj, cw)


def _gdn_conv_bwd(proj, cw, dy, kind, head0, nheads, name):
    s = proj.shape[0]
    r = _tile(s, 512, HALO)
    sub = _tile(r, GDN_SUB, HALO)
    nb = s // r
    per = r // HALO
    hb = min(GDN_CONV_HEADS, nheads)
    tc = hb * HEAD
    hidx = _halo_index(r)
    nidx = lambda rb: jnp.minimum((rb + 1) * per, s // HALO - 1)
    rep = dy.shape[1] // (nheads * HEAD)
    n = sub + 2 * HALO

    def body(uc, uh, un, cw_ref, dyc, dyn, du_ref, dcw_ref):
        rb = pl.program_id(1)
        first = jnp.where(rb == 0, 0.0, 1.0)
        last = jnp.where(rb == nb - 1, 0.0, 1.0)
        rows = lax.broadcasted_iota(jnp.int32, (n, HEAD), 0)
        own = jnp.where((rows >= HALO) & (rows < HALO + sub), 1.0, 0.0)
        wrow = lax.broadcasted_iota(jnp.int32, (8, HEAD), 0)

        @pl.when(rb == 0)
        def _():
            dcw_ref[...] = jnp.zeros_like(dcw_ref)

        for hh in range(hb):
            tile = pl.ds(hh * HEAD, HEAD)
            taps = [cw_ref[j:j + 1, tile] for j in range(GDN_CONV)]
            dcw = jnp.zeros((8, HEAD), F32)
            for r0 in range(0, r, sub):
                u_ext = _ext_rows(uc, uh, un, r0, sub, tile, first, last)
                dy_ext = _ext_rows(dyc, None, dyn, r0, sub, pl.ds(hh * rep * HEAD, HEAD), 1.0, last)
                for e in range(1, rep):
                    dy_ext = dy_ext + _ext_rows(dyc, None, dyn, r0, sub, pl.ds((hh * rep + e) * HEAD, HEAD), 1.0, last)
                shifted = [u_ext] + [pltpu.roll(u_ext, j, axis=0) for j in range(1, GDN_CONV)]
                acc = shifted[0] * taps[3]
                for j in range(1, GDN_CONV):
                    acc = acc + shifted[j] * taps[3 - j]
                _, vjp = jax.vjp(functools.partial(_gdn_post, kind=kind), acc)
                dacc, = vjp(dy_ext)
                du = dacc * taps[3]
                for j in range(1, GDN_CONV):
                    du = du + pltpu.roll(dacc, n - j, axis=0) * taps[3 - j]
                du_ref[pl.ds(r0, sub), tile] = du[HALO:HALO + sub].astype(BF)
                dm = dacc * own
                for j in range(GDN_CONV):
                    dcw = dcw + jnp.where(wrow == j, jnp.sum(dm * shifted[3 - j], axis=0, keepdims=True), 0.0)
            dcw_ref[:, tile] += dcw

    ucol = lambda j: head0 // hb + j
    return pl.pallas_call(
        body, name=name, grid=(nheads // hb, nb),
        in_specs=[pl.BlockSpec((r, tc), lambda j, rb: (rb, ucol(j))),
                  pl.BlockSpec((HALO, tc), lambda j, rb: (hidx(rb), ucol(j))),
                  pl.BlockSpec((HALO, tc), lambda j, rb: (nidx(rb), ucol(j))),
                  pl.BlockSpec((8, tc), lambda j, rb: (0, ucol(j))),
                  pl.BlockSpec((r, rep * tc), lambda j, rb: (rb, j)),
                  pl.BlockSpec((HALO, rep * tc), lambda j, rb: (nidx(rb), j))],
        out_specs=[pl.BlockSpec((r, tc), lambda j, rb: (rb, j)), pl.BlockSpec((8, tc), lambda j, rb: (0, j))],
        out_shape=[SDS((s, nheads * HEAD), BF), SDS((8, nheads * HEAD), F32)],
        compiler_params=_params("parallel", "arbitrary"))(proj, proj, proj, cw, dy, dy)


GATE_ROWS = 256


def _gates_tile(a, bt, a_log, dt_bias, hv):
    r = a.shape[0]
    z = a + dt_bias
    softplus = jnp.maximum(z, 0.0) + jnp.log(1.0 + jnp.exp(-jnp.abs(z)))
    g = -jnp.exp(a_log) * softplus
    ri = lax.broadcasted_iota(jnp.int32, (r, r), 0)
    ci = lax.broadcasted_iota(jnp.int32, (r, r), 1)
    same_chunk = jnp.right_shift(ri, 6) == jnp.right_shift(ci, 6)
    tri = jnp.where(same_chunk, jnp.where(ri >= ci, 1.0, 0.0), 0.0).astype(F32)
    gc = jnp.dot(tri, g, precision=HI, preferred_element_type=F32)
    beta = 1.0 / (1.0 + jnp.exp(-bt))
    er = lax.broadcasted_iota(jnp.int32, (LANE, hv * HEAD), 0)
    ec = lax.broadcasted_iota(jnp.int32, (LANE, hv * HEAD), 1)
    expand = jnp.where(er == jnp.right_shift(ec, 7), 1.0, 0.0).astype(F32)
    return (jnp.dot(gc, expand, precision=HI, preferred_element_type=F32),
            jnp.dot(beta, expand, precision=HI, preferred_element_type=F32))


def _gates_fwd(ab, a_log, dt_bias, hv, name):
    s = ab.shape[0]
    r = _tile(s, GATE_ROWS, CHUNK)

    def body(a_ref, b_ref, al_ref, dt_ref, gc_ref, bb_ref):
        gcb, btb = _gates_tile(a_ref[...], b_ref[...], al_ref[...], dt_ref[...], hv)
        gc_ref[...] = gcb
        bb_ref[...] = btb

    vec = pl.BlockSpec((1, LANE), lambda i: (0, 0))
    wide = pl.BlockSpec((r, hv * HEAD), lambda i: (i, 0))
    return pl.pallas_call(
        body, name=name, grid=(s // r,),
        in_specs=[pl.BlockSpec((r, LANE), lambda i: (i, 0)), pl.BlockSpec((r, LANE), lambda i: (i, 1)), vec, vec],
        out_specs=[wide, wide], out_shape=[SDS((s, hv * HEAD), F32)] * 2,
        compiler_params=_params("parallel"))(ab, ab, a_log, dt_bias)


def _gates_bwd(ab, a_log, dt_bias, dgcb, dbtb, hv, name):
    s = ab.shape[0]
    r = _tile(s, GATE_ROWS, CHUNK)

    def body(a_ref, b_ref, al_ref, dt_ref, dgc_ref, dbb_ref, dab_ref, dal_ref, ddt_ref):
        i = pl.program_id(0)
        _, vjp = jax.vjp(functools.partial(_gates_tile, hv=hv), a_ref[...], b_ref[...], al_ref[...], dt_ref[...])
        da, dbt, dal, ddt = vjp((dgc_ref[...], dbb_ref[...]))
        dab_ref[:, 0:LANE] = da
        dab_ref[:, LANE:] = dbt

        @pl.when(i == 0)
        def _():
            dal_ref[...] = dal
            ddt_ref[...] = ddt

        @pl.when(i > 0)
        def _():
            dal_ref[...] += dal
            ddt_ref[...] += ddt

    vec = pl.BlockSpec((1, LANE), lambda i: (0, 0))
    wide = pl.BlockSpec((r, hv * HEAD), lambda i: (i, 0))
    return pl.pallas_call(
        body, name=name, grid=(s // r,),
        in_specs=[pl.BlockSpec((r, LANE), lambda i: (i, 0)), pl.BlockSpec((r, LANE), lambda i: (i, 1)), vec, vec, wide, wide],
        out_specs=[pl.BlockSpec((r, 2 * LANE), lambda i: (i, 0)), vec, vec],
        out_shape=[SDS((s, 2 * LANE), F32), SDS((1, LANE), F32), SDS((1, LANE), F32)],
        compiler_params=_params("arbitrary"))(ab, ab, a_log, dt_bias, dgcb, dbtb)


def _split_bf16(a):
    hi = a.astype(BF)
    return hi, (a - hi.astype(F32)).astype(BF)


def _dot3(a, b, dims=(((1,), (0,)), ((), ()))):
    ah, al = _split_bf16(a)
    bh, bl = _split_bf16(b)
    d = lambda x, y: lax.dot_general(x, y, dims, preferred_element_type=F32)
    return d(ah, bh) + (d(ah, bl) + d(al, bh))


NNB = (((2,), (1,)), ((0,), (0,)))
NTB = (((2,), (2,)), ((0,), (0,)))
TNB = (((1,), (1,)), ((0,), (0,)))


def _bdot(a, b, dims):
    return lax.dot_general(a, b, dims, preferred_element_type=F32)


def _unit_lower_inverse(a):
    ri = lax.broadcasted_iota(jnp.int32, a.shape, 1)
    ci = lax.broadcasted_iota(jnp.int32, a.shape, 2)
    p = -a
    t = jnp.where(ri == ci, 1.0, 0.0) + p
    for _ in range(5):
        p = _dot3(p, p, NNB)
        t = t + _dot3(t, p, NNB)
    return t


@jax.custom_vjp
def _known_inverse(a, t):
    return t


def _known_inverse_fwd(a, t):
    return t, t


def _known_inverse_bwd(t, g):
    return -_dot3(_dot3(t, g, TNB), t, NTB), jnp.zeros_like(t)


_known_inverse.defvjp(_known_inverse_fwd, _known_inverse_bwd)


def _delta_decay(gcb):
    c = CHUNK
    shape = (gcb.shape[0], c, c)
    ri = lax.broadcasted_iota(jnp.int32, shape, 1)
    ci = lax.broadcasted_iota(jnp.int32, shape, 2)
    causal = ri >= ci
    grow = jnp.stack([jnp.concatenate([gcb[b], gcb[b]], axis=0).T[:c, :c] for b in range(shape[0])])
    return jnp.where(causal, jnp.exp(jnp.where(causal, gcb[:, :, :c] - grow, 0.0)), 0.0), ri > ci


def _delta_system(k, gcb, btb):
    decay, strict = _delta_decay(gcb)
    return jnp.where(strict, _bdot((k * btb).astype(BF), k.astype(BF), NTB) * decay, 0.0)


def _delta_prep(q, k, v, gcb, btb, tinv):
    decay, strict = _delta_decay(gcb)
    kb = k * btb
    kbf = k.astype(BF)
    a = jnp.where(strict, _bdot(kb.astype(BF), kbf, NTB) * decay, 0.0)
    t = _known_inverse(a, tinv).astype(BF)
    u = _bdot(t, (v * btb).astype(BF), NNB)
    w = _bdot(t, (kb * jnp.exp(gcb)).astype(BF), NNB)
    attn = _bdot(q.astype(BF), kbf, NTB) * decay
    return u, w, attn


def _delta_scan(u, w, attn, q, k, gcb, s_in):
    c = CHUNK
    glast = gcb[:, c - 1:c, :]
    sb = s_in.astype(BF)
    v_new = u - _bdot(w.astype(BF), sb, NNB)
    vnb = v_new.astype(BF)
    o = _bdot((q * jnp.exp(gcb)).astype(BF), sb, NNB) + _bdot(attn.astype(BF), vnb, NNB)
    ks = (k * jnp.exp(glast - gcb)).astype(BF)
    s_out = s_in * jnp.exp(glast[:, :, 0:1]) + _bdot(ks, vnb, TNB)
    return o, s_out


def _head_stack(ref, rows, width, heads, rep=1):
    return jnp.stack([ref[rows, pl.ds((hh // rep) * width, width)].astype(F32) for hh in range(heads)])


PREP_ROWS = 1024
PREP_HEADS = 2
SCAN_ROWS = 128
SCAN_HEADS = 16


def _delta_prep_fwd(q, k, v, gcb, btb, name):
    s, dv = v.shape
    hv = dv // HEAD
    g = PREP_HEADS
    assert dv // q.shape[1] == g
    r = _tile(s, PREP_ROWS, CHUNK)

    def body(q_ref, k_ref, v_ref, g_ref, b_ref, u_ref, w_ref, a_ref, t_ref):
        nb = r // CHUNK
        qc = q_ref[...].astype(F32).reshape(nb, CHUNK, HEAD)
        kc = k_ref[...].astype(F32).reshape(nb, CHUNK, HEAD)
        for hh in range(g):
            cols = pl.ds(hh * HEAD, HEAD)
            half = pl.ds(hh * CHUNK, CHUNK)
            gc = g_ref[:, cols].reshape(nb, CHUNK, HEAD)
            bc = b_ref[:, cols].reshape(nb, CHUNK, HEAD)
            tinv = _unit_lower_inverse(_delta_system(kc, gc, bc))
            u, w, attn = _delta_prep(qc, kc, v_ref[:, cols].astype(F32).reshape(nb, CHUNK, HEAD), gc, bc, tinv)
            u_ref[:, cols] = u.reshape(r, HEAD)
            w_ref[:, cols] = w.reshape(r, HEAD).astype(BF)
            a_ref[:, half] = attn.reshape(r, CHUNK).astype(BF)
            t_ref[:, half] = tinv.reshape(r, CHUNK)

    kq = pl.BlockSpec((r, HEAD), lambda j, i: (i, j))
    vs = pl.BlockSpec((r, g * HEAD), lambda j, i: (i, j))
    sq = pl.BlockSpec((r, g * CHUNK), lambda j, i: (i, j))
    return pl.pallas_call(
        body, name=name, grid=(hv // g, s // r), in_specs=[kq, kq, vs, vs, vs], out_specs=[vs, vs, sq, sq],
        out_shape=[SDS((s, dv), F32), SDS((s, dv), BF), SDS((s, hv * CHUNK), BF), SDS((s, hv * CHUNK), F32)],
        compiler_params=_params("parallel", "parallel"))(q, k, v, gcb, btb)


def _delta_prep_bwd(q, k, v, gcb, btb, tinv, du, dw, dattn, dq_s, dk_s, dg_s, name):
    s, dv = v.shape
    hv = dv // HEAD
    g = PREP_HEADS
    r = _tile(s, PREP_ROWS, CHUNK)

    def body(q_ref, k_ref, v_ref, g_ref, b_ref, t_ref, du_ref, dw_ref, da_ref, dqs_ref, dks_ref, dgs_ref,
             dq_ref, dk_ref, dv_ref, dg_ref, db_ref):
        nb = r // CHUNK
        wide = lambda ref, cols: ref[:, cols].astype(F32).reshape(nb, CHUNK, HEAD)
        qc = q_ref[...].astype(F32).reshape(nb, CHUNK, HEAD)
        kc = k_ref[...].astype(F32).reshape(nb, CHUNK, HEAD)
        for hh in range(g):
            cols = pl.ds(hh * HEAD, HEAD)
            half = pl.ds(hh * CHUNK, CHUNK)
            fn = functools.partial(_delta_prep, tinv=t_ref[:, half].reshape(nb, CHUNK, CHUNK))
            _, vjp = jax.vjp(fn, qc, kc, wide(v_ref, cols), wide(g_ref, cols), wide(b_ref, cols))
            dq, dk, dvv, dg, db = vjp((wide(du_ref, cols), wide(dw_ref, cols),
                                       da_ref[:, half].astype(F32).reshape(nb, CHUNK, CHUNK)))
            dq_ref[:, cols] = dq.reshape(r, HEAD) + dqs_ref[:, cols]
            dk_ref[:, cols] = dk.reshape(r, HEAD) + dks_ref[:, cols]
            dv_ref[:, cols] = dvv.reshape(r, HEAD)
            dg_ref[:, cols] = dg.reshape(r, HEAD) + dgs_ref[:, cols]
            db_ref[:, cols] = db.reshape(r, HEAD)

    kq = pl.BlockSpec((r, HEAD), lambda j, i: (i, j))
    vs = pl.BlockSpec((r, g * HEAD), lambda j, i: (i, j))
    sq = pl.BlockSpec((r, g * CHUNK), lambda j, i: (i, j))
    return pl.pallas_call(
        body, name=name, grid=(hv // g, s // r), in_specs=[kq, kq, vs, vs, vs, sq, vs, vs, sq, vs, vs, vs],
        out_specs=[vs] * 5, out_shape=[SDS((s, dv), F32)] * 5,
        compiler_params=_params("parallel", "parallel"))(q, k, v, gcb, btb, tinv, du, dw, dattn, dq_s, dk_s, dg_s)


def _delta_scan_fwd(u, w, attn, q, k, gcb, name):
    s, dv = u.shape
    hv = dv // HEAD
    rep = dv // q.shape[1]
    g = min(SCAN_HEADS, hv)
    r = _tile(s, SCAN_ROWS, CHUNK)
    per = r // CHUNK

    def body(u_ref, w_ref, a_ref, q_ref, k_ref, g_ref, o_ref, st_ref, state):
        @pl.when(pl.program_id(1) == 0)
        def _():
            state[...] = jnp.zeros_like(state)

        def chunk(cc, carry):
            rows = pl.ds(pl.multiple_of(cc * CHUNK, CHUNK), CHUNK)
            s_in = state[...]
            o, s_out = _delta_scan(_head_stack(u_ref, rows, HEAD, g), _head_stack(w_ref, rows, HEAD, g),
                                   _head_stack(a_ref, rows, CHUNK, g), _head_stack(q_ref, rows, HEAD, g, rep),
                                   _head_stack(k_ref, rows, HEAD, g, rep), _head_stack(g_ref, rows, HEAD, g), s_in)
            for hh in range(g):
                st_ref[hh, cc] = s_in[hh]
                o_ref[rows, pl.ds(hh * HEAD, HEAD)] = o[hh].astype(BF)
            state[...] = s_out
            return carry

        lax.fori_loop(0, per, chunk, 0)

    kq = pl.BlockSpec((r, g // rep * HEAD), lambda j, i: (i, j))
    vs = pl.BlockSpec((r, g * HEAD), lambda j, i: (i, j))
    sq = pl.BlockSpec((r, g * CHUNK), lambda j, i: (i, j))
    return pl.pallas_call(
        body, name=name, grid=(hv // g, s // r), in_specs=[vs, vs, sq, kq, kq, vs],
        out_specs=[vs, pl.BlockSpec((g, per, HEAD, HEAD), lambda j, i: (j, i, 0, 0))],
        out_shape=[SDS((s, dv), BF), SDS((hv, s // CHUNK, HEAD, HEAD), F32)],
        scratch_shapes=[pltpu.VMEM((g, HEAD, HEAD), F32)],
        compiler_params=_params("parallel", "arbitrary"))(u, w, attn, q, k, gcb)


def _delta_scan_bwd(u, w, attn, q, k, gcb, states, do, name):
    s, dv = u.shape
    hv = dv // HEAD
    rep = dv // q.shape[1]
    g = min(SCAN_HEADS, hv)
    r = _tile(s, SCAN_ROWS, CHUNK)
    per = r // CHUNK
    nb = s // r

    def body(u_ref, w_ref, a_ref, q_ref, k_ref, g_ref, st_ref, do_ref, du_ref, dw_ref, da_ref, dq_ref, dk_ref, dg_ref, dstate):
        @pl.when(pl.program_id(1) == 0)
        def _():
            dstate[...] = jnp.zeros_like(dstate)

        def chunk(t, carry):
            cc = per - 1 - t
            rows = pl.ds(pl.multiple_of(cc * CHUNK, CHUNK), CHUNK)
            s_in = jnp.stack([st_ref[hh, cc] for hh in range(g)])
            _, vjp = jax.vjp(_delta_scan, _head_stack(u_ref, rows, HEAD, g), _head_stack(w_ref, rows, HEAD, g),
                             _head_stack(a_ref, rows, CHUNK, g), _head_stack(q_ref, rows, HEAD, g, rep),
                             _head_stack(k_ref, rows, HEAD, g, rep), _head_stack(g_ref, rows, HEAD, g), s_in)
            du, dw, da, dq, dk, dg, ds_in = vjp((_head_stack(do_ref, rows, HEAD, g), dstate[...]))
            for hh in range(g):
                cols = pl.ds(hh * HEAD, HEAD)
                du_ref[rows, cols] = du[hh].astype(BF)
                dw_ref[rows, cols] = dw[hh].astype(BF)
                da_ref[rows, pl.ds(hh * CHUNK, CHUNK)] = da[hh].astype(BF)
                dq_ref[rows, cols] = dq[hh]
                dk_ref[rows, cols] = dk[hh]
                dg_ref[rows, cols] = dg[hh]
            dstate[...] = ds_in
            return carry

        lax.fori_loop(0, per, chunk, 0)

    kq = pl.BlockSpec((r, g // rep * HEAD), lambda j, i: (nb - 1 - i, j))
    vs = pl.BlockSpec((r, g * HEAD), lambda j, i: (nb - 1 - i, j))
    sq = pl.BlockSpec((r, g * CHUNK), lambda j, i: (nb - 1 - i, j))
    return pl.pallas_call(
        body, name=name, grid=(hv // g, nb),
        in_specs=[vs, vs, sq, kq, kq, vs, pl.BlockSpec((g, per, HEAD, HEAD), lambda j, i: (j, nb - 1 - i, 0, 0)), vs],
        out_specs=[vs, vs, sq, vs, vs, vs],
        out_shape=[SDS((s, dv), BF), SDS((s, dv), BF), SDS((s, hv * CHUNK), BF)] + [SDS((s, dv), F32)] * 3,
        scratch_shapes=[pltpu.VMEM((g, HEAD, HEAD), F32)],
        compiler_params=_params("parallel", "arbitrary"))(u, w, attn, q, k, gcb, states, do)


def _gdn_out_tile(o, gate, gain):
    return _headnorm(o, gain) * _silu(gate)


def _gdn_out_fwd(o, proj, gate_col0, gain, name):
    s, dv = o.shape
    r = _tile(s, 128, 16)

    def body(o_ref, g_ref, gain_ref, y_ref):
        y_ref[...] = _gdn_out_tile(o_ref[...].astype(F32), g_ref[...].astype(F32), gain_ref[...]).astype(BF)

    row = pl.BlockSpec((r, dv), lambda i: (i, 0))
    return pl.pallas_call(
        body, name=name, grid=(s // r,),
        in_specs=[row, pl.BlockSpec((r, dv), lambda i: (i, gate_col0)), pl.BlockSpec((1, HEAD), lambda i: (0, 0))],
        out_specs=row, out_shape=SDS((s, dv), BF), compiler_params=_params("parallel"))(o, proj, gain)


def _gdn_out_bwd(o, proj, gate_col0, gain, dy, name):
    s, dv = o.shape
    r = _tile(s, 128, 16)

    def body(o_ref, g_ref, gain_ref, dy_ref, do_ref, dg_ref, dgain_ref):
        i = pl.program_id(0)
        _, vjp = jax.vjp(_gdn_out_tile, o_ref[...].astype(F32), g_ref[...].astype(F32), gain_ref[...])
        do, dg, dgain = vjp(dy_ref[...].astype(F32))
        do_ref[...] = do
        dg_ref[...] = dg.astype(BF)

        @pl.when(i == 0)
        def _():
            dgain_ref[...] = dgain

        @pl.when(i > 0)
        def _():
            dgain_ref[...] += dgain

    row = pl.BlockSpec((r, dv), lambda i: (i, 0))
    vec = pl.BlockSpec((1, HEAD), lambda i: (0, 0))
    return pl.pallas_call(
        body, name=name, grid=(s // r,),
        in_specs=[row, pl.BlockSpec((r, dv), lambda i: (i, gate_col0)), vec, row],
        out_specs=[row, row, vec], out_shape=[SDS((s, dv), F32), SDS((s, dv), BF), SDS((1, HEAD), F32)],
        compiler_params=_params("arbitrary"))(o, proj, gain, dy)


WIDE_K = dict(tm=512, tn=1024, tk=8192, n_outer=True)
DEEP_K = 4096


def _ffn_forward(x, gain, wu, wg, cw, wd, tag):
    h, ht = _rmsnorm_fwd(x, gain, f"{tag}_norm")
    uu = _mm(h, wu, name=f"{tag}_up_u")
    ug = _mm(h, wg, name=f"{tag}_up_g")
    a, at = _ffn_act_fwd(uu, ug, cw, f"{tag}_act")
    y = _mm(a, wd, res=x, out_dtype=F32, name=f"{tag}_down", **WIDE_K)
    return y, (x, ht, uu, ug, at)


def _ffn_backward(saved, dys, gain, wu, wg, cw, wd, tag, early=None):
    x, ht, uu, ug, at = saved
    dy, dyb = dys
    da = _mm(dyb, wd, tb=True, name=f"{tag}_d_act")
    dwd = _mm(at, dyb, out_dtype=F32, name=f"{tag}_d_wd", tk=DEEP_K)
    duu, dug, dcw = _ffn_act_bwd(uu, ug, cw, da, f"{tag}_act_bwd")
    dwu = _mm(ht, duu, out_dtype=F32, name=f"{tag}_d_wu", tk=DEEP_K)
    dwg = _mm(ht, dug, out_dtype=F32, name=f"{tag}_d_wg", tk=DEEP_K)
    grads = dict(wu=dwu, wg=dwg, cw=dcw, wd=dwd)
    zero = early(grads) if early is not None else 0.0
    dh = _mm(duu, wu, tb=True, out_dtype=F32, name=f"{tag}_d_h_u", **WIDE_K)
    dh = _mm(dug, wg, tb=True, res=dh, out_dtype=F32, name=f"{tag}_d_h_g", **WIDE_K)
    dx, dxb, dgain = _rmsnorm_bwd(x, gain + zero, dh, dy, f"{tag}_norm_bwd")
    return (dx, dxb), dict(grads, gain=dgain)


def _att_forward(x, gain, p, tag):
    h, ht = _rmsnorm_fwd(x, gain, f"{tag}_norm")
    qkv = _mm(h, p["wqkv"], name=f"{tag}_qkv")
    q, kpad, vpad = _qkv_post_fwd(qkv, p["qg"], p["kg"], f"{tag}_qknorm")
    bias = _bias_expand(p["rel"], f"{tag}_bias")
    o = _attn_fwd(q, kpad, vpad, bias, f"{tag}_core")
    y = _mm(o, p["wo"], res=x, out_dtype=F32, name=f"{tag}_out")
    return y, (x, ht, qkv, q, kpad, vpad, bias, o)


def _att_backward(saved, dys, gain, p, tag, early=None):
    x, ht, qkv, q, kpad, vpad, bias, o = saved
    dy, dyb = dys
    do = _mm(dyb, p["wo"], tb=True, name=f"{tag}_d_o")
    dwo = _mm(o, dyb, ta=True, out_dtype=F32, name=f"{tag}_d_wo")
    dq, dkpad, dvpad, dbias = _attn_bwd(q, kpad, vpad, bias, do, f"{tag}_core_bwd")
    drel = _bias_reduce(dbias, f"{tag}_bias_bwd")
    dqkv, dqg, dkg = _qkv_post_bwd(qkv, p["qg"], p["kg"], dq, dkpad, dvpad, f"{tag}_qknorm_bwd")
    dwqkv = _mm(ht, dqkv, out_dtype=F32, name=f"{tag}_d_wqkv", tk=DEEP_K)
    grads = dict(wqkv=dwqkv, qg=dqg, kg=dkg, rel=drel, wo=dwo)
    zero = early(grads) if early is not None else 0.0
    dh = _mm(dqkv, p["wqkv"], tb=True, out_dtype=F32, name=f"{tag}_d_h", **WIDE_K)
    dx, dxb, dgain = _rmsnorm_bwd(x, gain + zero, dh, dy, f"{tag}_norm_bwd")
    return (dx, dxb), dict(grads, gain=dgain)


def _gdn_forward(x, gain, p, tag):
    d = x.shape[1]
    nk = d // HEAD
    hv = 2 * nk
    h, ht = _rmsnorm_fwd(x, gain, f"{tag}_norm")
    proj = _mm(h, p["wmain"], name=f"{tag}_proj")
    ab = _mm(h, p["wab"], out_dtype=F32, name=f"{tag}_proj_ab")
    q = _gdn_conv_fwd(proj, p["cw"], "q", 0, nk, f"{tag}_conv_q")
    k = _gdn_conv_fwd(proj, p["cw"], "k", nk, nk, f"{tag}_conv_k")
    v = _gdn_conv_fwd(proj, p["cw"], "v", 2 * nk, hv, f"{tag}_conv_v")
    gcb, btb = _gates_fwd(ab, p["a_log"], p["dt_bias"], hv, f"{tag}_gates")
    u, wd, attn, tinv = _delta_prep_fwd(q, k, v, gcb, btb, f"{tag}_delta_prep")
    o, states = _delta_scan_fwd(u, wd, attn, q, k, gcb, f"{tag}_delta_scan")
    og = _gdn_out_fwd(o, proj, 2, p["o_gain"], f"{tag}_onorm")
    y = _mm(og, p["wo"], res=x, out_dtype=F32, name=f"{tag}_out")
    return y, (x, ht, proj, ab, q, k, v, gcb, btb, u, wd, attn, tinv, o, states, og)


def _gdn_backward(saved, dys, gain, p, tag, early=None):
    x, ht, proj, ab, q, k, v, gcb, btb, u, wd, attn, tinv, o, states, og = saved
    dy, dyb = dys
    d = x.shape[1]
    nk = d // HEAD
    hv = 2 * nk
    dog = _mm(dyb, p["wo"], tb=True, name=f"{tag}_d_og")
    dwo = _mm(og, dyb, ta=True, out_dtype=F32, name=f"{tag}_d_wo")
    do, dgate, dogain = _gdn_out_bwd(o, proj, 2, p["o_gain"], dog, f"{tag}_onorm_bwd")
    du, dw, dattn, dq_s, dk_s, dg_s = _delta_scan_bwd(u, wd, attn, q, k, gcb, states, do, f"{tag}_delta_scan_bwd")
    dq, dk, dv, dgcb, dbtb = _delta_prep_bwd(q, k, v, gcb, btb, tinv, du, dw, dattn, dq_s, dk_s, dg_s, f"{tag}_delta_prep_bwd")
    dab, dalog, ddt = _gates_bwd(ab, p["a_log"], p["dt_bias"], dgcb, dbtb, hv, f"{tag}_gates_bwd")
    dpq, dcq = _gdn_conv_bwd(proj, p["cw"], dq, "q", 0, nk, f"{tag}_conv_q_bwd")
    dpk, dck = _gdn_conv_bwd(proj, p["cw"], dk, "k", nk, nk, f"{tag}_conv_k_bwd")
    dpv, dcv = _gdn_conv_bwd(proj, p["cw"], dv, "v", 2 * nk, hv, f"{tag}_conv_v_bwd")
    dproj = jnp.concatenate([dpq, dpk, dpv, dgate], axis=1)
    dcw = jnp.concatenate([dcq, dck, dcv], axis=1)
    dwmain = _mm(ht, dproj, out_dtype=F32, name=f"{tag}_d_wmain", tk=DEEP_K)
    dwab = _mm(ht, dab, out_dtype=F32, name=f"{tag}_d_wab")
    grads = dict(wmain=dwmain, wab=dwab, cw=dcw, a_log=dalog, dt_bias=ddt, o_gain=dogain, wo=dwo)
    zero = early(grads) if early is not None else 0.0
    dh = _mm(dproj, p["wmain"], tb=True, out_dtype=F32, name=f"{tag}_d_h_main", **WIDE_K)
    dh = _mm(dab, p["wab"], tb=True, res=dh, out_dtype=F32, name=f"{tag}_d_h_ab")
    dx, dxb, dgain = _rmsnorm_bwd(x, gain + zero, dh, dy, f"{tag}_norm_bwd")
    return (dx, dxb), dict(grads, gain=dgain)


def _resolve(entry, after):
    return entry(after) if callable(entry) else entry


def _local_step(x, target, w, sink=None):
    depth = len(w["ffn"])
    tape = []
    for i in range(depth):
        kind, j = i % 3, i // 3
        gain = w["mix_norm"][i:i + 1]
        if kind == 0:
            x, saved = _att_forward(x, gain, _resolve(w["att"][j], x), f"l{i}_att")
        elif kind == 1:
            x_in = x
            pw = _resolve(w["pool"][j], x)
            x = _pool_fwd(x_in, gain, pw["w"], pw["scale"], f"l{i}_pool")
            saved = x_in
        else:
            x, saved = _gdn_forward(x, gain, _resolve(w["gdn"][j], x), f"l{i}_gdn")
        f = _resolve(w["ffn"][i], x)
        x, fsaved = _ffn_forward(x, w["ffn_norm"][i:i + 1], f["wu"], f["wg"], f["cw"], f["wd"], f"l{i}_ffn")
        tape.append((saved, fsaved))
    dy, dyb, loss_row = _loss_head(x, target, "loss_head")
    dy = (dy, dyb)
    grads = dict(mix=[None] * depth, ffn=[None] * depth)
    zero = 0.0
    for i in reversed(range(depth)):
        kind, j = i % 3, i // 3
        saved, fsaved = tape[i]
        early = (lambda name, layer: functools.partial(sink, name, layer)) if sink is not None else (lambda name, layer: None)
        f = _resolve(w["ffn"][i], dy[0])
        dy, grads["ffn"][i] = _ffn_backward(fsaved, dy, w["ffn_norm"][i:i + 1] + zero, f["wu"], f["wg"], f["cw"], f["wd"],
                                            f"l{i}_ffn", early("ffn", i))
        gain = w["mix_norm"][i:i + 1] + zero
        if kind == 0:
            dy, grads["mix"][i] = _att_backward(saved, dy, gain, _resolve(w["att"][j], dy[0]), f"l{i}_att", early("att", i))
        elif kind == 1:
            pw = _resolve(w["pool"][j], dy[0])
            dx, dgain, dw4, dscale = _pool_bwd(saved, gain, pw["w"], pw["scale"], dy[0], f"l{i}_pool_bwd")
            dy = (dx, dx.astype(BF))
            grads["mix"][i] = dict(gain=dgain, w=dw4, scale=dscale)
            if sink is not None:
                zero = zero + sink("pool", i, grads["mix"][i])
        else:
            dy, grads["mix"][i] = _gdn_backward(saved, dy, gain, _resolve(w["gdn"][j], dy[0]), f"l{i}_gdn", early("gdn", i))
    return loss_row, dy[0], grads


MESH = pl.DeviceIdType.MESH
ANY = pl.BlockSpec(memory_space=pl.ANY)


def _position():
    return tuple(lax.axis_index(a) for a in AXES)


def _flip(pos, rel):
    return tuple(1 - p if (rel >> (2 - i)) & 1 else p for i, p in enumerate(pos))


def _index(pos):
    return 4 * pos[0] + 2 * pos[1] + pos[2]


def _all_gather(arr, name):
    def body(x_ref, o_ref, send, recv, local):
        me = _position()
        mine = pltpu.make_async_copy(x_ref, o_ref.at[_index(me)], local)
        mine.start()
        copies = []
        for rel in range(1, N_DEV):
            cp = pltpu.make_async_remote_copy(
                src_ref=x_ref, dst_ref=o_ref.at[_index(me)], send_sem=send.at[rel - 1], recv_sem=recv.at[rel - 1],
                device_id=_flip(me, rel), device_id_type=MESH)
            cp.start()
            copies.append(cp)
        for cp in copies:
            cp.wait()
        mine.wait()

    return pl.pallas_call(
        body, name=name, in_specs=[ANY], out_specs=ANY, out_shape=SDS((N_DEV,) + arr.shape, arr.dtype),
        scratch_shapes=[pltpu.SemaphoreType.DMA((N_DEV - 1,)), pltpu.SemaphoreType.DMA((N_DEV - 1,)),
                        pltpu.SemaphoreType.DMA(())])(arr)


def _exchange(arr, name):
    def body(x_ref, o_ref, send, recv, local):
        me = _position()
        mine = pltpu.make_async_copy(x_ref.at[_index(me)], o_ref.at[_index(me)], local)
        mine.start()
        copies = []
        for rel in range(1, N_DEV):
            peer = _flip(me, rel)
            cp = pltpu.make_async_remote_copy(
                src_ref=x_ref.at[_index(peer)], dst_ref=o_ref.at[_index(me)], send_sem=send.at[rel - 1],
                recv_sem=recv.at[rel - 1], device_id=peer, device_id_type=MESH)
            cp.start()
            copies.append(cp)
        for cp in copies:
            cp.wait()
        mine.wait()

    return pl.pallas_call(
        body, name=name, in_specs=[ANY], out_specs=ANY, out_shape=SDS(arr.shape, arr.dtype),
        scratch_shapes=[pltpu.SemaphoreType.DMA((N_DEV - 1,)), pltpu.SemaphoreType.DMA((N_DEV - 1,)),
                        pltpu.SemaphoreType.DMA(())])(arr)


HBM = pl.BlockSpec(memory_space=pltpu.HBM)
SEM = pl.BlockSpec(memory_space=pltpu.SEMAPHORE)
EFFECT = pltpu.SideEffectType.DATAFLOW_SIDE_EFFECTING


def _split_copies(x_ref, land_ref, send, recv, scatter):
    me = _position()
    copies = []
    for rel in range(1, N_DEV):
        peer = _flip(me, rel)
        copies.append(pltpu.make_async_remote_copy(
            src_ref=x_ref.at[_index(peer)] if scatter else x_ref, dst_ref=land_ref.at[_index(me)],
            send_sem=send.at[rel - 1], recv_sem=recv.at[rel - 1], device_id=peer, device_id_type=MESH))
    return copies


def _copies_start(arr, scatter, name):
    shape = arr.shape if scatter else (N_DEV,) + arr.shape

    def body(x_ref, land_ref, send, recv, x_thru, land_thru, token):
        for cp in _split_copies(x_ref, land_ref, send, recv, scatter):
            cp.start()
        token[...] = jnp.zeros_like(token)

    sems = pltpu.SemaphoreType.DMA((N_DEV - 1,))
    send, recv, x_thru, land_thru, token = pl.pallas_call(
        body, name=name,
        out_shape=(sems, sems, pltpu.HBM(arr.shape, arr.dtype), pltpu.HBM(shape, arr.dtype), SDS((8, LANE), F32)),
        in_specs=(HBM, HBM), out_specs=(SEM, SEM, HBM, HBM, pl.BlockSpec(memory_space=pltpu.VMEM)),
        input_output_aliases={0: 2, 1: 3}, compiler_params=pltpu.CompilerParams(has_side_effects=EFFECT),
    )(pltpu.with_memory_space_constraint(arr, pltpu.HBM), pltpu.with_memory_space_constraint(lax.empty(shape, arr.dtype), pltpu.HBM))
    return (send, recv, x_thru, land_thru, scatter, name), token[0, 0]


def _copies_wait(handle, after):
    send, recv, x_thru, land_thru, scatter, name = handle

    def body(x_ref, land_ref, send_ref, recv_ref, after_ref, x_dead, got_ref):
        for cp in _split_copies(x_ref, land_ref, send_ref, recv_ref, scatter):
            cp.wait_send()
            cp.wait_recv()

    return pl.pallas_call(
        body, name=name + "_wait",
        out_shape=(pltpu.HBM(x_thru.shape, x_thru.dtype), pltpu.HBM(land_thru.shape, land_thru.dtype)),
        in_specs=(HBM, HBM, SEM, SEM, ANY), out_specs=(HBM, HBM), input_output_aliases={0: 0, 1: 1},
        compiler_params=pltpu.CompilerParams(has_side_effects=EFFECT),
    )(x_thru, land_thru, send, recv, after)[1]


def _with_own(got, own):
    return lax.dynamic_update_index_in_dim(got, own.astype(got.dtype), _index(_position()), 0)


def _adamw(parts, w, m, v, name):
    if not isinstance(parts, (list, tuple)):
        parts = [parts]
    layers = len(parts)
    r, c = parts[0].shape[1:]
    assert w.shape == (layers * r, c), (w.shape, parts[0].shape, layers)
    tr = _tile(r, 128, 16)
    per = r // tr
    c1 = 1.0 / (1.0 - ADAM_B1 ** ADAM_STEP)
    c2 = 1.0 / (1.0 - ADAM_B2 ** ADAM_STEP)

    def body(*refs):
        p_refs = refs[:layers]
        w_ref, m_ref, v_ref, g_ref, d_ref, nm_ref, nv_ref = refs[layers:]

        def update(p_ref):
            g = p_ref[0].astype(F32)
            for s in range(1, N_DEV):
                g = g + p_ref[s].astype(F32)
            nm = ADAM_B1 * m_ref[...] + (1.0 - ADAM_B1) * g
            nv = ADAM_B2 * v_ref[...] + (1.0 - ADAM_B2) * (g * g)
            g_ref[...] = g
            nm_ref[...] = nm
            nv_ref[...] = nv
            d_ref[...] = -ADAM_LR * ((nm * c1) / (jnp.sqrt(nv * c2) + ADAM_EPS) + ADAM_WD * w_ref[...])

        if layers == 1:
            update(p_refs[0])
        else:
            for j in range(layers):
                pl.when(pl.program_id(0) == j)(functools.partial(update, p_refs[j]))

    p_specs = [pl.BlockSpec((N_DEV, tr, c), functools.partial(lambda l, i, j: (0, jnp.where(l == j, i, 0), 0), j=j))
               for j in range(layers)]
    row = pl.BlockSpec((tr, c), lambda l, i: (l * per + i, 0))
    return pl.pallas_call(
        body, name=name, grid=(layers, per), in_specs=p_specs + [row, row, row],
        out_specs=[row] * 4, out_shape=[SDS(w.shape, F32)] * 4, compiler_params=_params("arbitrary", "arbitrary"))(*parts, w, m, v)


PACK = 8 * LANE


def _pack(arrs):
    flat = []
    for a in arrs:
        a = a.reshape(-1).astype(F32)
        flat.append(jnp.pad(a, (0, (-a.shape[0]) % PACK)))
    return jnp.concatenate(flat).reshape(-1, LANE)


def _unpack(packed, shapes, lead=()):
    flat = packed.reshape(lead + (-1,))
    out, off = [], 0
    for shp in shapes:
        n = int(np.prod(shp))
        out.append(flat[..., off:off + n].reshape(lead + tuple(shp)))
        off += n + (-n) % PACK
    return out


def _pad_to(a, axis, size):
    pad = [(0, 0)] * a.ndim
    pad[axis] = (0, size - a.shape[axis])
    return jnp.pad(a, pad)


def _cols_from_shards(g):
    return jnp.transpose(g, (1, 0, 2)).reshape(g.shape[1], -1)


def _cols_to_shards(a):
    c = a.shape[-1] // N_DEV
    a = a.reshape(a.shape[:-1] + (N_DEV, c))
    return jnp.moveaxis(a, -2, 0)


def _rows_to_shards(a):
    r = a.shape[-2] // N_DEV
    a = a.reshape(a.shape[:-2] + (N_DEV, r, a.shape[-1]))
    return jnp.moveaxis(a, -3, 0)


WEIGHTS = ("mix_norm", "ffn_norm", "att_w_qkv", "att_q_gain", "att_k_gain", "att_rel_bias", "att_w_o", "pool_w",
           "pool_scale", "gdn_w_in", "gdn_conv", "gdn_a_log", "gdn_dt_bias", "gdn_o_gain", "gdn_w_o", "ffn_w_up",
           "ffn_conv", "ffn_w_down")
REPLICATED = ("mix_norm", "ffn_norm", "att_q_gain", "att_k_gain", "pool_scale", "gdn_a_log", "gdn_dt_bias", "gdn_o_gain")
SMALL_SHARDED = ("att_rel_bias", "gdn_conv", "ffn_conv")
BIG = ("att_w_qkv", "att_w_o", "pool_w", "gdn_w_in", "gdn_w_o", "ffn_w_up", "ffn_w_down")
KEEP_F32 = ("pool_w",)


def _memo(build):
    cache = []

    def entry(after):
        if not cache:
            cache.append(build(after))
        return cache[0]

    return entry


def _assemble_weights(w, get, small, f):
    d = w["mix_norm"].shape[1]
    nk = d // HEAD
    hv = 2 * nk
    fp = -(-f // FF_PAD) * FF_PAD
    out = dict(mix_norm=w["mix_norm"], ffn_norm=w["ffn_norm"], att=[], pool=[], gdn=[], ffn=[])

    def att(j, after):
        return dict(wqkv=_cols_from_shards(get("att_w_qkv", j, after)), wo=get("att_w_o", j, after).reshape(d, d),
                    qg=w["att_q_gain"][j:j + 1], kg=w["att_k_gain"][j:j + 1], rel=small["att_rel_bias"][j])

    def pool(j, after):
        g = get("pool_w", j, after)
        return dict(w=jnp.transpose(g, (1, 0, 2, 3)).reshape(g.shape[1], g.shape[3], g.shape[3]),
                    scale=w["pool_scale"][j:j + 1])

    def gdn(j, after):
        win = _cols_from_shards(get("gdn_w_in", j, after))
        nm = 6 * d
        wab = jnp.concatenate([_pad_to(win[:, nm:nm + hv], 1, LANE), _pad_to(win[:, nm + hv:], 1, LANE)], axis=1)
        return dict(wmain=win[:, :nm], wab=wab, cw=_pad_to(small["gdn_conv"][j], 0, 8),
                    a_log=_pad_to(w["gdn_a_log"][j:j + 1], 1, LANE), dt_bias=_pad_to(w["gdn_dt_bias"][j:j + 1], 1, LANE),
                    o_gain=w["gdn_o_gain"][j:j + 1], wo=get("gdn_w_o", j, after).reshape(2 * d, d))

    def ffn(i, after):
        g = get("ffn_w_up", i, after)
        half = N_DEV // 2
        tail = [jnp.zeros((d, fp - f), g.dtype)] if fp > f else []
        return dict(wu=jnp.concatenate([g[k] for k in range(half)] + tail, axis=1),
                    wg=jnp.concatenate([g[k] for k in range(half, N_DEV)] + tail, axis=1),
                    cw=_pad_to(_pad_to(small["ffn_conv"][i], 0, 8), 1, fp),
                    wd=_pad_to(get("ffn_w_down", i, after).reshape(f, d), 0, fp))

    for key, build, count in (("att", att, w["att_w_qkv"].shape[0]), ("pool", pool, w["pool_w"].shape[0]),
                              ("gdn", gdn, w["gdn_w_in"].shape[0]), ("ffn", ffn, w["ffn_w_up"].shape[0])):
        out[key] = [_memo(functools.partial(build, j)) for j in range(count)]
    return out


def _full_gradients(grads, w, f):
    d = w["mix_norm"].shape[1]
    nk = d // HEAD
    hv = 2 * nk
    depth = len(grads["ffn"])
    att = [grads["mix"][i] for i in range(depth) if i % 3 == 0]
    pool = [grads["mix"][i] for i in range(depth) if i % 3 == 1]
    gdn = [grads["mix"][i] for i in range(depth) if i % 3 == 2]
    ffn = grads["ffn"]
    win = [jnp.concatenate([g["wmain"], g["wab"][:, :hv], g["wab"][:, LANE:LANE + hv]], axis=1) for g in gdn]
    return dict(
        mix_norm=jnp.concatenate([g["gain"] for g in grads["mix"]], axis=0),
        ffn_norm=jnp.concatenate([g["gain"] for g in ffn], axis=0),
        att_w_qkv=jnp.stack([g["wqkv"] for g in att]),
        att_q_gain=jnp.concatenate([g["qg"] for g in att], axis=0),
        att_k_gain=jnp.concatenate([g["kg"] for g in att], axis=0),
        att_rel_bias=jnp.stack([g["rel"] for g in att]),
        att_w_o=jnp.stack([g["wo"] for g in att]),
        pool_w=jnp.stack([g["w"] for g in pool]),
        pool_scale=jnp.concatenate([g["scale"] for g in pool], axis=0),
        gdn_w_in=jnp.stack(win),
        gdn_conv=jnp.stack([g["cw"][:GDN_CONV] for g in gdn]),
        gdn_a_log=jnp.concatenate([g["a_log"][:, :hv] for g in gdn], axis=0),
        gdn_dt_bias=jnp.concatenate([g["dt_bias"][:, :hv] for g in gdn], axis=0),
        gdn_o_gain=jnp.concatenate([g["o_gain"] for g in gdn], axis=0),
        gdn_w_o=jnp.stack([g["wo"] for g in gdn]),
        ffn_w_up=jnp.stack([jnp.concatenate([g["wu"][:, :f], g["wg"][:, :f]], axis=1) for g in ffn]),
        ffn_conv=jnp.stack([g["cw"][:FFN_CONV, :f] for g in ffn]),
        ffn_w_down=jnp.stack([g["wd"][:f] for g in ffn]))


ROW_SHARDED = ("att_w_o", "gdn_w_o", "ffn_w_down")


def _to_shards(name, full):
    if name == "pool_w":
        r = full.shape[2] // N_DEV
        a = full.reshape(full.shape[:2] + (N_DEV, r, full.shape[3]))
        return jnp.moveaxis(a, 2, 0)
    return _rows_to_shards(full) if name in ROW_SHARDED else _cols_to_shards(full)


def kernel(x, mix_norm, ffn_norm, att_w_qkv, att_q_gain, att_k_gain, att_rel_bias, att_w_o, pool_w, pool_scale, gdn_w_in, gdn_conv, gdn_a_log, gdn_dt_bias, gdn_o_gain, gdn_w_o, ffn_w_up, ffn_conv, ffn_w_down, loss_target, m_mix_norm, m_ffn_norm, m_att_w_qkv, m_att_q_gain, m_att_k_gain, m_att_rel_bias, m_att_w_o, m_pool_w, m_pool_scale, m_gdn_w_in, m_gdn_conv, m_gdn_a_log, m_gdn_dt_bias, m_gdn_o_gain, m_gdn_w_o, m_ffn_w_up, m_ffn_conv, m_ffn_w_down, v_mix_norm, v_ffn_norm, v_att_w_qkv, v_att_q_gain, v_att_k_gain, v_att_rel_bias, v_att_w_o, v_pool_w, v_pool_scale, v_gdn_w_in, v_gdn_conv, v_gdn_a_log, v_gdn_dt_bias, v_gdn_o_gain, v_gdn_w_o, v_ffn_w_up, v_ffn_conv, v_ffn_w_down):
    w = dict(zip(WEIGHTS, (mix_norm, ffn_norm, att_w_qkv, att_q_gain, att_k_gain, att_rel_bias, att_w_o, pool_w, pool_scale, gdn_w_in, gdn_conv, gdn_a_log, gdn_dt_bias, gdn_o_gain, gdn_w_o, ffn_w_up, ffn_conv, ffn_w_down)))
    m = dict(zip(WEIGHTS, (m_mix_norm, m_ffn_norm, m_att_w_qkv, m_att_q_gain, m_att_k_gain, m_att_rel_bias, m_att_w_o, m_pool_w, m_pool_scale, m_gdn_w_in, m_gdn_conv, m_gdn_a_log, m_gdn_dt_bias, m_gdn_o_gain, m_gdn_w_o, m_ffn_w_up, m_ffn_conv, m_ffn_w_down)))
    v = dict(zip(WEIGHTS, (v_mix_norm, v_ffn_norm, v_att_w_qkv, v_att_q_gain, v_att_k_gain, v_att_rel_bias, v_att_w_o, v_pool_w, v_pool_scale, v_gdn_w_in, v_gdn_conv, v_gdn_a_log, v_gdn_dt_bias, v_gdn_o_gain, v_gdn_w_o, v_ffn_w_up, v_ffn_conv, v_ffn_w_down)))

    me = _index(_position())
    d = mix_norm.shape[1]
    hv = 2 * (d // HEAD)
    f = ffn_w_down.shape[1] * N_DEV
    depth = ffn_w_up.shape[0]

    small_shapes = [w[n].shape for n in SMALL_SHARDED]
    small_g = _all_gather(_pack([w[n] for n in SMALL_SHARDED]), "gather_small")
    small = {}
    for n, a in zip(SMALL_SHARDED, _unpack(small_g, small_shapes, lead=(N_DEV,))):
        small[n] = jnp.moveaxis(a, 0, -2).reshape(a.shape[1:-1] + (N_DEV * a.shape[-1],))
    order = []
    for i in range(depth):
        order += [[("att_w_qkv", i // 3), ("att_w_o", i // 3)], [("pool_w", i // 3)], [("gdn_w_in", i // 3), ("gdn_w_o", i // 3)]][i % 3]
        order += [("ffn_w_up", i), ("ffn_w_down", i)]
    after_small = small_g[0, 0, 0] * 0.0
    local = {(n, j): (w[n][j] + after_small if n in KEEP_F32 else (w[n][j] + after_small).astype(BF)) for n, j in order}
    arriving, zero = {}, 0.0
    for n, j in order:
        arriving[(n, j)], tok = _copies_start(local[(n, j)], False, f"gather_{n}_{j}")
        zero = zero + tok

    def get(n, j, after):
        return _with_own(_copies_wait(arriving[(n, j)], after), local[(n, j)])

    ordered = dict(w, mix_norm=mix_norm + zero, ffn_norm=ffn_norm + zero)
    full = _assemble_weights(ordered, get, small, f)

    leaving = {}

    def sink(kind, i, g):
        j = i // 3
        if kind == "ffn":
            c = 2 * f // N_DEV
            up = [g[key][:, k * c:(k + 1) * c] for key in ("wu", "wg") for k in range(N_DEV // 2)]
            pieces = [("ffn_w_up", i, jnp.stack(up)), ("ffn_w_down", i, _rows_to_shards(g["wd"][:f]))]
        elif kind == "att":
            pieces = [("att_w_qkv", j, _cols_to_shards(g["wqkv"])), ("att_w_o", j, _rows_to_shards(g["wo"]))]
        elif kind == "pool":
            pieces = [("pool_w", j, _to_shards("pool_w", g["w"][None])[:, 0])]
        else:
            win = jnp.concatenate([g["wmain"], g["wab"][:, :hv], g["wab"][:, LANE:LANE + hv]], axis=1)
            pieces = [("gdn_w_in", j, _cols_to_shards(win)), ("gdn_w_o", j, _rows_to_shards(g["wo"]))]
        tok = 0.0
        for n, l, shards in pieces:
            shards = shards if n in KEEP_F32 else shards.astype(BF)
            handle, t = _copies_start(shards, True, f"exchange_{n}_{l}")
            leaving[(n, l)] = (handle, lax.dynamic_index_in_dim(shards, me, 0, keepdims=False))
            tok = tok + t
        return tok

    loss_row, dx, grads = _local_step(x[0], loss_target[0], full, sink)
    gfull = _full_gradients(grads, w, f)

    out = {}
    for n in BIG:
        c = w[n].shape[-1]
        parts = []
        for l in range(w[n].shape[0]):
            handle, own = leaving[(n, l)]
            parts.append(_with_own(_copies_wait(handle, dx), own).reshape(N_DEV, -1, c))
        res = _adamw(parts, w[n].reshape(-1, c), m[n].reshape(-1, c), v[n].reshape(-1, c), f"adamw_{n}")
        out[n] = [a.reshape(w[n].shape) for a in res]
    sparts = _exchange(jnp.stack([_pack([_to_shards(n, gfull[n])[k] for n in SMALL_SHARDED]) for k in range(N_DEV)]),
                       "exchange_small")
    res = _adamw(sparts, *[_pack([t[n] for n in SMALL_SHARDED]) for t in (w, m, v)], "adamw_small")
    for n, *vals in zip(SMALL_SHARDED, *[_unpack(a, small_shapes) for a in res]):
        out[n] = vals
    rep_shapes = [w[n].shape for n in REPLICATED]
    rparts = _all_gather(_pack([gfull[n] for n in REPLICATED] + [loss_row[:, 0:1]]), "gather_replicated_grads")
    pad = jnp.zeros((1, 1), F32)
    res = _adamw(rparts, *[_pack([t[n] for n in REPLICATED] + [pad]) for t in (w, m, v)], "adamw_replicated")
    for n, *vals in zip(REPLICATED, *[_unpack(a, rep_shapes) for a in res]):
        out[n] = vals
    loss = jnp.sum(_unpack(rparts, rep_shapes + [(1, 1)], lead=(N_DEV,))[-1])
    return (loss, dx[None], *[out[n][0] for n in WEIGHTS], *[out[n][1] for n in WEIGHTS],
            *[out[n][2] for n in WEIGHTS], *[out[n][3] for n in WEIGHTS])
```

```python
import functools

import numpy as np
import jax
import jax.numpy as jnp
from jax import lax
from jax.experimental import pallas as pl
from jax.experimental.pallas import tpu as pltpu

F32 = jnp.float32
BF = jnp.bfloat16
SDS = jax.ShapeDtypeStruct

EPS = 1e-6
MASK_VALUE = -1e30
CHUNK = 64
LEFT_CHUNKS = 8
BAND_LEFT = LEFT_CHUNKS * CHUNK
BAND = BAND_LEFT + CHUNK
MAX_REL = 256
NUM_REL = (CHUNK - 1) + MAX_REL + 1
HEAD = 128
LANE = 128
HALO = 16
POOL_WINDOWS = (2, 4, 8, 16)
GDN_CONV = 4
FFN_CONV = 3
FF_PAD = 512
N_DEV = 8
AXES = ("x", "y", "c")
VMEM_LIMIT = 56 * 1024 * 1024

ADAM_LR = 0.001
ADAM_B1 = 0.9
ADAM_B2 = 0.999
ADAM_EPS = 1e-08
ADAM_WD = 0.01
ADAM_STEP = 10

HI = lax.Precision.HIGHEST
NT = (((1,), (1,)), ((), ()))
TN = (((0,), (0,)), ((), ()))


def _tile(n, target, mult=LANE):
    if n <= target:
        return n
    t = (target // mult) * mult
    while t >= mult:
        if n % t == 0:
            return t
        t -= mult
    return n


def _params(*sem):
    return pltpu.CompilerParams(dimension_semantics=sem, vmem_limit_bytes=VMEM_LIMIT)


def _silu(x):
    return x / (1.0 + jnp.exp(-x))


@functools.partial(jax.custom_vjp, nondiff_argnums=(1,))
def _shift(x, k):
    return pltpu.roll(x, k % x.shape[0], axis=0)


def _shift_fwd(x, k):
    return _shift(x, k), None


def _shift_bwd(k, _, g):
    return (pltpu.roll(g, (-k) % g.shape[0], axis=0),)


_shift.defvjp(_shift_fwd, _shift_bwd)


def _mm(a, b, *, name, ta=False, tb=False, out_dtype=BF, res=None, tm=1024, tn=1024, tk=2048, n_outer=False):
    m, k = (a.shape[1], a.shape[0]) if ta else a.shape
    n, kb = (b.shape[0], b.shape[1]) if tb else (b.shape[1], b.shape[0])
    assert k == kb, (a.shape, b.shape, ta, tb)
    if ta or a.dtype != BF:
        tk = min(tk, 1024)
    tm, tn, tk = _tile(m, tm), _tile(n, tn), _tile(k, tk)
    nk = k // tk
    dims = (((0 if ta else 1,), (1 if tb else 0,)), ((), ()))

    def body(*refs):
        if res is None:
            a_ref, b_ref, o_ref, acc = refs
        else:
            a_ref, b_ref, r_ref, o_ref, acc = refs
        prod = lax.dot_general(a_ref[...].astype(BF), b_ref[...].astype(BF), dims, preferred_element_type=F32)
        if nk == 1:
            if res is not None:
                prod = prod + r_ref[...].astype(F32)
            o_ref[...] = prod.astype(out_dtype)
            return
        kk = pl.program_id(2)

        @pl.when(kk == 0)
        def _():
            acc[...] = prod

        @pl.when(kk > 0)
        def _():
            acc[...] += prod

        @pl.when(kk == nk - 1)
        def _():
            r = acc[...]
            if res is not None:
                r = r + r_ref[...].astype(F32)
            o_ref[...] = r.astype(out_dtype)

    def order(fn):
        return (lambda j, i, q: fn(i, j, q)) if n_outer else fn

    a_spec = pl.BlockSpec((tk, tm), order(lambda i, j, q: (q, i))) if ta else pl.BlockSpec((tm, tk), order(lambda i, j, q: (i, q)))
    b_spec = pl.BlockSpec((tn, tk), order(lambda i, j, q: (j, q))) if tb else pl.BlockSpec((tk, tn), order(lambda i, j, q: (q, j)))
    o_spec = pl.BlockSpec((tm, tn), order(lambda i, j, q: (i, j)))
    ins, specs = [a, b], [a_spec, b_spec]
    if res is not None:
        ins.append(res)
        specs.append(o_spec)
    return pl.pallas_call(
        body, name=name, grid=(n // tn, m // tm, nk) if n_outer else (m // tm, n // tn, nk), in_specs=specs, out_specs=o_spec,
        out_shape=SDS((m, n), out_dtype), scratch_shapes=[pltpu.VMEM((tm, tn), F32)],
        compiler_params=_params("parallel", "parallel", "arbitrary"))(*ins)


def _mm_pair_nt(a1, b1, a2, b2, name, tm=512, tn=512):
    m, k = a1.shape
    n = b1.shape[0]
    assert a2.shape == a1.shape and b2.shape == b1.shape and b1.shape[1] == k
    tm, tn = _tile(m, tm), _tile(n, tn)

    def body(a1_ref, b1_ref, a2_ref, b2_ref, o_ref):
        o_ref[...] = (lax.dot_general(a1_ref[...], b1_ref[...], NT, preferred_element_type=F32)
                      + lax.dot_general(a2_ref[...], b2_ref[...], NT, preferred_element_type=F32))

    a_spec = pl.BlockSpec((tm, k), lambda j, i: (i, 0))
    b_spec = pl.BlockSpec((tn, k), lambda j, i: (j, 0))
    return pl.pallas_call(
        body, name=name, grid=(n // tn, m // tm), in_specs=[a_spec, b_spec, a_spec, b_spec],
        out_specs=pl.BlockSpec((tm, tn), lambda j, i: (i, j)), out_shape=SDS((m, n), F32),
        compiler_params=_params("parallel", "parallel"))(a1, b1, a2, b2)


def _rms(x, gain):
    return x * lax.rsqrt(jnp.mean(x * x, axis=-1, keepdims=True) + EPS) * gain


def _rmsnorm_fwd(x, gain, name):
    s, d = x.shape
    ts = _tile(s, 256, LANE)

    def body(x_ref, g_ref, o_ref, t_ref):
        h = _rms(x_ref[...], g_ref[...])
        o_ref[...] = h.astype(BF)
        t_ref[...] = h.T.astype(BF)

    return pl.pallas_call(
        body, name=name, grid=(s // ts,),
        in_specs=[pl.BlockSpec((ts, d), lambda i: (i, 0)), pl.BlockSpec((1, d), lambda i: (0, 0))],
        out_specs=[pl.BlockSpec((ts, d), lambda i: (i, 0)), pl.BlockSpec((d, ts), lambda i: (0, i))],
        out_shape=[SDS((s, d), BF), SDS((d, s), BF)], compiler_params=_params("parallel"))(x, gain)


def _rmsnorm_bwd(x, gain, dh, dres, name):
    s, d = x.shape
    ts = _tile(s, 256, 16)

    grp = 16

    def body(x_ref, g_ref, dh_ref, dr_ref, dx_ref, dxb_ref, dg_ref):
        i = pl.program_id(0)
        gain_row = g_ref[...]

        def piece(t, dg_rows):
            rows = pl.ds(pl.multiple_of(t * grp, grp), grp)
            xv = x_ref[rows, :]
            dh = dh_ref[rows, :].astype(F32)
            r = lax.rsqrt(jnp.mean(xv * xv, axis=-1, keepdims=True) + EPS)
            xh = xv * r
            dxh = dh * gain_row
            dx = r * (dxh - xh * jnp.mean(dxh * xh, axis=-1, keepdims=True)) + dr_ref[rows, :]
            dx_ref[rows, :] = dx
            dxb_ref[rows, :] = dx.astype(BF)
            return dg_rows + dh * xh

        def pieces(t, dg_rows):
            for q in range(together):
                dg_rows = piece(t * together + q, dg_rows)
            return dg_rows

        together = 4
        dg = jnp.sum(lax.fori_loop(0, ts // (grp * together), pieces, jnp.zeros((grp, d), F32)), axis=0, keepdims=True)

        @pl.when(i == 0)
        def _():
            dg_ref[...] = dg

        @pl.when(i > 0)
        def _():
            dg_ref[...] += dg

    row = pl.BlockSpec((ts, d), lambda i: (i, 0))
    vec = pl.BlockSpec((1, d), lambda i: (0, 0))
    return pl.pallas_call(
        body, name=name, grid=(s // ts,), in_specs=[row, vec, row, row], out_specs=[row, row, vec],
        out_shape=[SDS((s, d), F32), SDS((s, d), BF), SDS((1, d), F32)], compiler_params=_params("arbitrary"))(x, gain, dh, dres)


def _halo_index(rows_per_block):
    per = rows_per_block // HALO
    return lambda rb: jnp.maximum(rb * per - 1, 0)


FFN_SUB = 256


def _ext_rows(cur, prev, nxt, r0, rows, tile, keep_prev=1.0, keep_next=1.0):
    n = cur.shape[0]
    parts = []
    if r0 > 0:
        parts.append(cur[pl.ds(r0 - HALO, HALO + rows), tile].astype(F32))
    else:
        head = jnp.zeros((HALO, LANE), F32) if prev is None else prev[:, tile].astype(F32) * keep_prev
        parts += [head, cur[pl.ds(0, rows), tile].astype(F32)]
    if nxt is not False:
        if r0 + rows < n:
            parts.append(cur[pl.ds(r0 + rows, HALO), tile].astype(F32))
        else:
            parts.append(jnp.zeros((HALO, LANE), F32) if nxt is None else nxt[:, tile].astype(F32) * keep_next)
    return jnp.concatenate(parts, axis=0)


def _ffn_act_fwd(u, g, cw, name):
    s, f = u.shape
    r, tc = _tile(s, 512, HALO), _tile(f, 512)
    sub = _tile(r, FFN_SUB, HALO)
    hidx = _halo_index(r)

    def body(uc, uh, gc, cw_ref, o_ref, t_ref):
        keep = jnp.where(pl.program_id(1) == 0, 0.0, 1.0)
        for lt in range(tc // LANE):
            tile = pl.ds(lt * LANE, LANE)
            c0, c1, c2 = cw_ref[0:1, tile], cw_ref[1:2, tile], cw_ref[2:3, tile]
            for r0 in range(0, r, sub):
                u_ext = _ext_rows(uc, uh, False, r0, sub, tile, keep)
                acc = u_ext * c2 + pltpu.roll(u_ext, 1, axis=0) * c1 + pltpu.roll(u_ext, 2, axis=0) * c0
                a = _silu(acc)[HALO:] * gc[pl.ds(r0, sub), tile].astype(F32)
                o_ref[pl.ds(r0, sub), tile] = a.astype(BF)
                t_ref[tile, pl.ds(r0, sub)] = a.T.astype(BF)

    cur = pl.BlockSpec((r, tc), lambda j, rb: (rb, j))
    return pl.pallas_call(
        body, name=name, grid=(f // tc, s // r),
        in_specs=[cur, pl.BlockSpec((HALO, tc), lambda j, rb: (hidx(rb), j)), cur,
                  pl.BlockSpec((8, tc), lambda j, rb: (0, j))],
        out_specs=[cur, pl.BlockSpec((tc, r), lambda j, rb: (j, rb))],
        out_shape=[SDS((s, f), BF), SDS((f, s), BF)], compiler_params=_params("parallel", "parallel"))(u, u, g, cw)


def _ffn_act_bwd(u, g, cw, da, name):
    s, f = u.shape
    r, tc = _tile(s, 512, HALO), _tile(f, 512)
    sub = _tile(r, FFN_SUB, HALO)
    nb = s // r
    per = r // HALO
    hidx = _halo_index(r)
    nidx = lambda rb: jnp.minimum((rb + 1) * per, s // HALO - 1)
    n = sub + 2 * HALO

    def body(uc, uh, un, gc, gn, dac, dan, cw_ref, du_ref, dg_ref, dcw_ref):
        rb = pl.program_id(1)
        first = jnp.where(rb == 0, 0.0, 1.0)
        last = jnp.where(rb == nb - 1, 0.0, 1.0)
        wrow = lax.broadcasted_iota(jnp.int32, (8, LANE), 0)

        @pl.when(rb == 0)
        def _():
            dcw_ref[...] = jnp.zeros_like(dcw_ref)

        for lt in range(tc // LANE):
            tile = pl.ds(lt * LANE, LANE)
            c0, c1, c2 = cw_ref[0:1, tile], cw_ref[1:2, tile], cw_ref[2:3, tile]
            dcw = jnp.zeros((8, LANE), F32)
            for r0 in range(0, r, sub):
                u_ext = _ext_rows(uc, uh, un, r0, sub, tile, first, last)
                g_ext = _ext_rows(gc, None, gn, r0, sub, tile, 1.0, last)
                da_ext = _ext_rows(dac, None, dan, r0, sub, tile, 1.0, last)
                sh1, sh2 = pltpu.roll(u_ext, 1, axis=0), pltpu.roll(u_ext, 2, axis=0)
                acc = u_ext * c2 + sh1 * c1 + sh2 * c0
                sg = 1.0 / (1.0 + jnp.exp(-acc))
                dg_ref[pl.ds(r0, sub), tile] = (da_ext * (acc * sg))[HALO:HALO + sub].astype(BF)
                dacc = da_ext * g_ext * (sg * (1.0 + acc * (1.0 - sg)))
                du = dacc * c2 + pltpu.roll(dacc, n - 1, axis=0) * c1 + pltpu.roll(dacc, n - 2, axis=0) * c0
                du_ref[pl.ds(r0, sub), tile] = du[HALO:HALO + sub].astype(BF)
                for j, tap in enumerate((sh2, sh1, u_ext)):
                    dcw = dcw + jnp.where(wrow == j, jnp.sum((dacc * tap)[HALO:HALO + sub], axis=0, keepdims=True), 0.0)
            dcw_ref[:, tile] += dcw

    cur = pl.BlockSpec((r, tc), lambda j, rb: (rb, j))
    prev = pl.BlockSpec((HALO, tc), lambda j, rb: (hidx(rb), j))
    nxt = pl.BlockSpec((HALO, tc), lambda j, rb: (nidx(rb), j))
    wspec = pl.BlockSpec((8, tc), lambda j, rb: (0, j))
    return pl.pallas_call(
        body, name=name, grid=(f // tc, nb),
        in_specs=[cur, prev, nxt, cur, nxt, cur, nxt, wspec],
        out_specs=[cur, cur, wspec], out_shape=[SDS((s, f), BF), SDS((s, f), BF), SDS((8, f), F32)],
        compiler_params=_params("parallel", "arbitrary"))(u, u, u, g, g, da, da, cw)


def _loss_head(y, target, name):
    s, d = y.shape
    ts = _tile(s, 256, 16)

    def body(y_ref, t_ref, dy_ref, dyb_ref, l_ref):
        i = pl.program_id(0)
        err = y_ref[...] - t_ref[...]
        dy_ref[...] = err * (1.0 / d)
        dyb_ref[...] = (err * (1.0 / d)).astype(BF)
        part = jnp.zeros((1, LANE), F32) + 0.5 * jnp.sum(jnp.sum(err * err, axis=1, keepdims=True), axis=0, keepdims=True) / d

        @pl.when(i == 0)
        def _():
            l_ref[...] = part

        @pl.when(i > 0)
        def _():
            l_ref[...] += part

    row = pl.BlockSpec((ts, d), lambda i: (i, 0))
    return pl.pallas_call(
        body, name=name, grid=(s // ts,), in_specs=[row, row],
        out_specs=[row, row, pl.BlockSpec((1, LANE), lambda i: (0, 0))],
        out_shape=[SDS((s, d), F32), SDS((s, d), BF), SDS((1, LANE), F32)], compiler_params=_params("arbitrary"))(y, target)


def _rel_index():
    rel = BAND_LEFT + np.arange(CHUNK)[:, None] - np.arange(BAND)[None, :]
    return (np.clip(rel, -(CHUNK - 1), MAX_REL) + (CHUNK - 1)).reshape(1, CHUNK * BAND).astype(np.int32)


def _onehot(idx_row):
    rows = lax.broadcasted_iota(jnp.int32, (NUM_REL, idx_row.shape[1]), 0)
    return jnp.where(rows == idx_row, 1.0, 0.0).astype(F32)


def _bias_expand(rel_bias, name):
    h = rel_bias.shape[0]
    n = CHUNK * BAND
    tn = n // 8

    def body(rb_ref, idx_ref, o_ref):
        o_ref[...] = jnp.dot(rb_ref[...], _onehot(idx_ref[...]), precision=HI, preferred_element_type=F32)

    out = pl.pallas_call(
        body, name=name, grid=(n // tn,),
        in_specs=[pl.BlockSpec((h, NUM_REL), lambda j: (0, 0)), pl.BlockSpec((1, tn), lambda j: (0, j))],
        out_specs=pl.BlockSpec((h, tn), lambda j: (0, j)), out_shape=SDS((h, n), F32),
        compiler_params=_params("parallel"))(rel_bias, jnp.asarray(_rel_index()))
    return out.reshape(h, CHUNK, BAND)


def _bias_reduce(dbias, name):
    h = dbias.shape[0]
    n = CHUNK * BAND
    tn = n // 8

    def body(db_ref, idx_ref, o_ref):
        j = pl.program_id(0)
        part = lax.dot_general(db_ref[...], _onehot(idx_ref[...]), NT, precision=HI, preferred_element_type=F32)

        @pl.when(j == 0)
        def _():
            o_ref[...] = part

        @pl.when(j > 0)
        def _():
            o_ref[...] += part

    return pl.pallas_call(
        body, name=name, grid=(n // tn,),
        in_specs=[pl.BlockSpec((h, tn), lambda j: (0, j)), pl.BlockSpec((1, tn), lambda j: (0, j))],
        out_specs=pl.BlockSpec((h, NUM_REL), lambda j: (0, 0)), out_shape=SDS((h, NUM_REL), F32),
        compiler_params=_params("arbitrary"))(dbias.reshape(h, n), jnp.asarray(_rel_index()))


def _headnorm(x, gain):
    outs = []
    for hh in range(x.shape[1] // HEAD):
        xh = x[:, hh * HEAD:(hh + 1) * HEAD]
        outs.append(xh * lax.rsqrt(jnp.mean(xh * xh, axis=-1, keepdims=True) + EPS) * gain)
    return jnp.concatenate(outs, axis=1)


def _qkv_post_fwd(qkv, qg, kg, name):
    s, d3 = qkv.shape
    d = d3 // 3
    r = BAND_LEFT
    nb = s // r

    def body(x_ref, qg_ref, kg_ref, q_ref, k_ref, v_ref):
        i = pl.program_id(0)
        keep = jnp.where(i == 0, 0.0, 1.0)
        q_ref[...] = _headnorm(x_ref[:, 0:d].astype(F32), qg_ref[...]).astype(BF)
        k_ref[...] = (_headnorm(x_ref[:, d:2 * d].astype(F32), kg_ref[...]) * keep).astype(BF)
        v_ref[...] = (x_ref[:, 2 * d:].astype(F32) * keep).astype(BF)

    prev = lambda i: (jnp.maximum(i - 1, 0), 0)
    vec = pl.BlockSpec((1, HEAD), lambda i: (0, 0))
    return pl.pallas_call(
        body, name=name, grid=(nb + 1,),
        in_specs=[pl.BlockSpec((r, d3), prev), vec, vec],
        out_specs=[pl.BlockSpec((r, d), prev), pl.BlockSpec((r, d), lambda i: (i, 0)), pl.BlockSpec((r, d), lambda i: (i, 0))],
        out_shape=[SDS((s, d), BF), SDS((s + r, d), BF), SDS((s + r, d), BF)],
        compiler_params=_params("arbitrary"))(qkv, qg, kg)


def _qkv_post_bwd(qkv, qg, kg, dq, dkpad, dvpad, name):
    s, d3 = qkv.shape
    d = d3 // 3
    r = _tile(s, 256, 16)
    off = BAND_LEFT // r

    def body(x_ref, qg_ref, kg_ref, dq_ref, dk_ref, dv_ref, o_ref, dqg_ref, dkg_ref):
        i = pl.program_id(0)
        _, vq = jax.vjp(_headnorm, x_ref[:, 0:d].astype(F32), qg_ref[...])
        dxq, dqg = vq(dq_ref[...])
        _, vk = jax.vjp(_headnorm, x_ref[:, d:2 * d].astype(F32), kg_ref[...])
        dxk, dkg = vk(dk_ref[...])
        o_ref[:, 0:d] = dxq.astype(BF)
        o_ref[:, d:2 * d] = dxk.astype(BF)
        o_ref[:, 2 * d:] = dv_ref[...].astype(BF)

        @pl.when(i == 0)
        def _():
            dqg_ref[...] = dqg
            dkg_ref[...] = dkg

        @pl.when(i > 0)
        def _():
            dqg_ref[...] += dqg
            dkg_ref[...] += dkg

    vec = pl.BlockSpec((1, HEAD), lambda i: (0, 0))
    row3 = pl.BlockSpec((r, d3), lambda i: (i, 0))
    row = pl.BlockSpec((r, d), lambda i: (i, 0))
    padrow = pl.BlockSpec((r, d), lambda i: (i + off, 0))
    return pl.pallas_call(
        body, name=name, grid=(s // r,), in_specs=[row3, vec, vec, row, padrow, padrow],
        out_specs=[row3, vec, vec], out_shape=[SDS((s, d3), BF), SDS((1, HEAD), F32), SDS((1, HEAD), F32)],
        compiler_params=_params("arbitrary"))(qkv, qg, kg, dq, dkpad, dvpad)


ATT_QB = 1024


def _att_probs(q, kw, bias, c0):
    sc = _bdot(q, kw, NTB) * (HEAD ** -0.5) + bias
    lane = lax.broadcasted_iota(jnp.int32, sc.shape, 2)
    chunk = lax.broadcasted_iota(jnp.int32, sc.shape, 0) + c0
    sc = jnp.where(lane + chunk * CHUNK >= BAND_LEFT, sc, MASK_VALUE)
    p = jnp.exp(sc - jnp.max(sc, axis=-1, keepdims=True))
    return p / jnp.sum(p, axis=-1, keepdims=True)


def _attn_fwd(q, kpad, vpad, bias, name):
    s, d = q.shape
    h = d // HEAD
    sp = kpad.shape[0]
    qb = _tile(s, ATT_QB, CHUNK)
    per = qb // CHUNK

    def body(q_ref, k_ref, v_ref, b_ref, o_ref):
        c0 = pl.program_id(1) * per
        wins = [pl.ds(pl.multiple_of((c0 + cc) * CHUNK, CHUNK), BAND) for cc in range(per)]
        kw = jnp.stack([k_ref[w, :] for w in wins])
        vw = jnp.stack([v_ref[w, :] for w in wins])
        p = _att_probs(q_ref[...].reshape(per, CHUNK, HEAD), kw, b_ref[0], c0)
        o_ref[...] = _bdot(p.astype(BF), vw, NNB).reshape(qb, HEAD).astype(BF)

    qspec = pl.BlockSpec((qb, HEAD), lambda hh, i: (i, hh))
    kspec = pl.BlockSpec((sp, HEAD), lambda hh, i: (0, hh))
    return pl.pallas_call(
        body, name=name, grid=(h, s // qb),
        in_specs=[qspec, kspec, kspec, pl.BlockSpec((1, CHUNK, BAND), lambda hh, i: (hh, 0, 0))],
        out_specs=qspec, out_shape=SDS((s, d), BF), compiler_params=_params("parallel", "arbitrary"))(q, kpad, vpad, bias)


def _attn_bwd(q, kpad, vpad, bias, do, name):
    s, d = q.shape
    h = d // HEAD
    sp = kpad.shape[0]
    qb = _tile(s, ATT_QB, CHUNK)
    per = qb // CHUNK
    scale = HEAD ** -0.5

    def body(q_ref, k_ref, v_ref, b_ref, do_ref, dq_ref, dk_ref, dv_ref, db_ref):
        i = pl.program_id(1)

        @pl.when(i == 0)
        def _():
            dk_ref[...] = jnp.zeros_like(dk_ref)
            dv_ref[...] = jnp.zeros_like(dv_ref)
            db_ref[...] = jnp.zeros_like(db_ref)

        c0 = i * per
        wins = [pl.ds(pl.multiple_of((c0 + cc) * CHUNK, CHUNK), BAND) for cc in range(per)]
        kw = jnp.stack([k_ref[w, :] for w in wins])
        vw = jnp.stack([v_ref[w, :] for w in wins])
        qc = q_ref[...].reshape(per, CHUNK, HEAD)
        doc = do_ref[...].astype(BF).reshape(per, CHUNK, HEAD)
        p = _att_probs(qc, kw, b_ref[0], c0)
        dp = _bdot(doc, vw, NTB)
        ds = p * (dp - jnp.sum(p * dp, axis=-1, keepdims=True))
        db_ref[0] += jnp.sum(ds, axis=0)
        dsb = (ds * scale).astype(BF)
        dq_ref[...] = _bdot(dsb, kw, NNB).reshape(qb, HEAD)

        def union(x):
            tot = None
            for cc in range(per):
                parts = [x[cc]]
                if cc:
                    parts.insert(0, jnp.zeros((cc * CHUNK, HEAD), F32))
                if cc < per - 1:
                    parts.append(jnp.zeros(((per - 1 - cc) * CHUNK, HEAD), F32))
                piece = jnp.concatenate(parts, axis=0) if len(parts) > 1 else parts[0]
                tot = piece if tot is None else tot + piece
            return tot

        span = pl.ds(pl.multiple_of(c0 * CHUNK, CHUNK), BAND + (per - 1) * CHUNK)
        dk_ref[span, :] += union(_bdot(dsb, qc, TNB))
        dv_ref[span, :] += union(_bdot(p.astype(BF), doc, TNB))

    qspec = pl.BlockSpec((qb, HEAD), lambda hh, i: (i, hh))
    kspec = pl.BlockSpec((sp, HEAD), lambda hh, i: (0, hh))
    bspec = pl.BlockSpec((1, CHUNK, BAND), lambda hh, i: (hh, 0, 0))
    return pl.pallas_call(
        body, name=name, grid=(h, s // qb), in_specs=[qspec, kspec, kspec, bspec, qspec],
        out_specs=[qspec, kspec, kspec, bspec],
        out_shape=[SDS((s, d), F32), SDS((sp, d), F32), SDS((sp, d), F32), SDS((h, CHUNK, BAND), F32)],
        compiler_params=_params("parallel", "arbitrary"))(q, kpad, vpad, bias, do)


def _pool_tile(x_ext, gain, w4, scale, row0):
    n, d = x_ext.shape
    dg = d // len(POOL_WINDOWS)
    pos = lax.broadcasted_iota(jnp.int32, (n, 1), 0) + row0
    hn = _rms(x_ext, gain) * jnp.where(pos >= 0, 1.0, 0.0)
    outs = []
    for gi, w in enumerate(POOL_WINDOWS):
        hg = hn[:, gi * dg:(gi + 1) * dg]
        acc, k = hg, 1
        while k < w:
            acc = acc + _shift(acc, k)
            k *= 2
        inv = 1.0 / jnp.clip(pos + 1, 1, w).astype(F32)
        pooled = acc * inv - hg
        outs.append(jnp.dot(pooled.astype(BF), w4[gi].astype(BF), preferred_element_type=F32))
    return jnp.concatenate(outs, axis=1) * scale


POOL_ROWS = 128


def _pool_fwd(x, gain, w4, scale, name):
    s, d = x.shape
    r = _tile(s, POOL_ROWS, HALO)
    hidx = _halo_index(r)

    def body(xc, xh, g_ref, w_ref, s_ref, o_ref):
        rb = pl.program_id(0)
        x_ext = jnp.concatenate([xh[...], xc[...]], axis=0)
        y = _pool_tile(x_ext, g_ref[...], [w_ref[gi] for gi in range(len(POOL_WINDOWS))], s_ref[...], rb * r - HALO)
        o_ref[...] = xc[...] + y[HALO:]

    cur = pl.BlockSpec((r, d), lambda rb: (rb, 0))
    vec = pl.BlockSpec((1, d), lambda rb: (0, 0))
    return pl.pallas_call(
        body, name=name, grid=(s // r,),
        in_specs=[cur, pl.BlockSpec((HALO, d), lambda rb: (hidx(rb), 0)), vec,
                  pl.BlockSpec(w4.shape, lambda rb: (0, 0, 0)), vec],
        out_specs=cur, out_shape=SDS((s, d), F32), compiler_params=_params("parallel"))(x, x, gain, w4, scale)


def _pool_bwd(x, gain, w4, scale, dy, name):
    s, d = x.shape
    r = _tile(s, POOL_ROWS, HALO)
    nb = s // r
    hidx = _halo_index(r)

    def body(xc, xh, g_ref, w_ref, s_ref, dy_ref, dx_ref, dg_ref, dw_ref, ds_ref, carry):
        step = pl.program_id(0)
        rb = nb - 1 - step
        x_ext = jnp.concatenate([xh[...], xc[...]], axis=0)
        fn = functools.partial(_pool_tile, row0=rb * r - HALO)
        _, vjp = jax.vjp(fn, x_ext, g_ref[...], [w_ref[gi] for gi in range(len(POOL_WINDOWS))], s_ref[...])
        ct = jnp.concatenate([jnp.zeros((HALO, d), F32), dy_ref[...]], axis=0)
        dx_ext, dg, dws, dsc = vjp(ct)

        @pl.when(step == 0)
        def _():
            carry[...] = jnp.zeros_like(carry)
            dg_ref[...] = jnp.zeros_like(dg_ref)
            dw_ref[...] = jnp.zeros_like(dw_ref)
            ds_ref[...] = jnp.zeros_like(ds_ref)

        dx_ref[...] = dy_ref[...] + dx_ext[HALO:]
        dx_ref[pl.ds(r - HALO, HALO), :] += carry[...]
        carry[...] = dx_ext[:HALO]
        dg_ref[...] += dg
        for gi, dw in enumerate(dws):
            dw_ref[gi] += dw
        ds_ref[...] += dsc

    cur = pl.BlockSpec((r, d), lambda t: (nb - 1 - t, 0))
    vec = pl.BlockSpec((1, d), lambda t: (0, 0))
    wspec = pl.BlockSpec(w4.shape, lambda t: (0, 0, 0))
    return pl.pallas_call(
        body, name=name, grid=(nb,),
        in_specs=[cur, pl.BlockSpec((HALO, d), lambda t: (hidx(nb - 1 - t), 0)), vec, wspec, vec, cur],
        out_specs=[cur, vec, wspec, vec],
        out_shape=[SDS((s, d), F32), SDS((1, d), F32), SDS(w4.shape, F32), SDS((1, d), F32)],
        scratch_shapes=[pltpu.VMEM((HALO, d), F32)], compiler_params=_params("arbitrary"))(x, x, gain, w4, scale, dy)


def _gdn_post(acc, kind):
    y = _silu(acc)
    if kind != "v":
        y = y * lax.rsqrt(jnp.sum(y * y, axis=-1, keepdims=True) + EPS)
    if kind == "q":
        y = y * (HEAD ** -0.5)
    return y


GDN_SUB = 128
GDN_CONV_HEADS = 4


def _gdn_conv_fwd(proj, cw, kind, head0, nheads, name):
    s = proj.shape[0]
    r = _tile(s, 512, HALO)
    sub = _tile(r, GDN_SUB, HALO)
    hb = min(GDN_CONV_HEADS, nheads)
    assert head0 % hb == 0 and nheads % hb == 0
    tc = hb * HEAD
    hidx = _halo_index(r)

    def body(uc, uh, cw_ref, o_ref):
        keep = jnp.where(pl.program_id(1) == 0, 0.0, 1.0)
        for hh in range(hb):
            tile = pl.ds(hh * HEAD, HEAD)
            taps = [cw_ref[j:j + 1, tile] for j in range(GDN_CONV)]
            for r0 in range(0, r, sub):
                u_ext = _ext_rows(uc, uh, False, r0, sub, tile, keep)
                acc = u_ext * taps[3]
                for j in range(1, GDN_CONV):
                    acc = acc + pltpu.roll(u_ext, j, axis=0) * taps[3 - j]
                o_ref[pl.ds(r0, sub), tile] = _gdn_post(acc, kind)[HALO:].astype(BF)

    return pl.pallas_call(
        body, name=name, grid=(nheads // hb, s // r),
        in_specs=[pl.BlockSpec((r, tc), lambda j, rb: (rb, head0 // hb + j)),
                  pl.BlockSpec((HALO, tc), lambda j, rb: (hidx(rb), head0 // hb + j)),
                  pl.BlockSpec((8, tc), lambda j, rb: (0, head0 // hb + j))],
        out_specs=pl.BlockSpec((r, tc), lambda j, rb: (rb, j)), out_shape=SDS((s, nheads * HEAD), BF),
        compiler_params=_params("parallel", "parallel"))(proj, proj, cw)


def _gdn_conv_bwd(proj, cw, dy, kind, head0, nheads, name):
    s = proj.shape[0]
    r = _tile(s, 512, HALO)
    sub = _tile(r, GDN_SUB, HALO)
    nb = s // r
    per = r // HALO
    hb = min(GDN_CONV_HEADS, nheads)
    tc = hb * HEAD
    hidx = _halo_index(r)
    nidx = lambda rb: jnp.minimum((rb + 1) * per, s // HALO - 1)
    rep = dy.shape[1] // (nheads * HEAD)
    n = sub + 2 * HALO

    def body(uc, uh, un, cw_ref, dyc, dyn, du_ref, dcw_ref):
        rb = pl.program_id(1)
        first = jnp.where(rb == 0, 0.0, 1.0)
        last = jnp.where(rb == nb - 1, 0.0, 1.0)
        rows = lax.broadcasted_iota(jnp.int32, (n, HEAD), 0)
        own = jnp.where((rows >= HALO) & (rows < HALO + sub), 1.0, 0.0)
        wrow = lax.broadcasted_iota(jnp.int32, (8, HEAD), 0)

        @pl.when(rb == 0)
        def _():
            dcw_ref[...] = jnp.zeros_like(dcw_ref)

        for hh in range(hb):
            tile = pl.ds(hh * HEAD, HEAD)
            taps = [cw_ref[j:j + 1, tile] for j in range(GDN_CONV)]
            dcw = jnp.zeros((8, HEAD), F32)
            for r0 in range(0, r, sub):
                u_ext = _ext_rows(uc, uh, un, r0, sub, tile, first, last)
                dy_ext = _ext_rows(dyc, None, dyn, r0, sub, pl.ds(hh * rep * HEAD, HEAD), 1.0, last)
                for e in range(1, rep):
                    dy_ext = dy_ext + _ext_rows(dyc, None, dyn, r0, sub, pl.ds((hh * rep + e) * HEAD, HEAD), 1.0, last)
                shifted = [u_ext] + [pltpu.roll(u_ext, j, axis=0) for j in range(1, GDN_CONV)]
                acc = shifted[0] * taps[3]
                for j in range(1, GDN_CONV):
                    acc = acc + shifted[j] * taps[3 - j]
                _, vjp = jax.vjp(functools.partial(_gdn_post, kind=kind), acc)
                dacc, = vjp(dy_ext)
                du = dacc * taps[3]
                for j in range(1, GDN_CONV):
                    du = du + pltpu.roll(dacc, n - j, axis=0) * taps[3 - j]
                du_ref[pl.ds(r0, sub), tile] = du[HALO:HALO + sub].astype(BF)
                dm = dacc * own
                for j in range(GDN_CONV):
                    dcw = dcw + jnp.where(wrow == j, jnp.sum(dm * shifted[3 - j], axis=0, keepdims=True), 0.0)
            dcw_ref[:, tile] += dcw

    ucol = lambda j: head0 // hb + j
    return pl.pallas_call(
        body, name=name, grid=(nheads // hb, nb),
        in_specs=[pl.BlockSpec((r, tc), lambda j, rb: (rb, ucol(j))),
                  pl.BlockSpec((HALO, tc), lambda j, rb: (hidx(rb), ucol(j))),
                  pl.BlockSpec((HALO, tc), lambda j, rb: (nidx(rb), ucol(j))),
                  pl.BlockSpec((8, tc), lambda j, rb: (0, ucol(j))),
                  pl.BlockSpec((r, rep * tc), lambda j, rb: (rb, j)),
                  pl.BlockSpec((HALO, rep * tc), lambda j, rb: (nidx(rb), j))],
        out_specs=[pl.BlockSpec((r, tc), lambda j, rb: (rb, j)), pl.BlockSpec((8, tc), lambda j, rb: (0, j))],
        out_shape=[SDS((s, nheads * HEAD), BF), SDS((8, nheads * HEAD), F32)],
        compiler_params=_params("parallel", "arbitrary"))(proj, proj, proj, cw, dy, dy)


GATE_ROWS = 256


def _gates_tile(a, bt, a_log, dt_bias, hv):
    r = a.shape[0]
    z = a + dt_bias
    softplus = jnp.maximum(z, 0.0) + jnp.log(1.0 + jnp.exp(-jnp.abs(z)))
    g = -jnp.exp(a_log) * softplus
    ri = lax.broadcasted_iota(jnp.int32, (r, r), 0)
    ci = lax.broadcasted_iota(jnp.int32, (r, r), 1)
    same_chunk = jnp.right_shift(ri, 6) == jnp.right_shift(ci, 6)
    tri = jnp.where(same_chunk, jnp.where(ri >= ci, 1.0, 0.0), 0.0).astype(F32)
    gc = jnp.dot(tri, g, precision=HI, preferred_element_type=F32)
    beta = 1.0 / (1.0 + jnp.exp(-bt))
    er = lax.broadcasted_iota(jnp.int32, (LANE, hv * HEAD), 0)
    ec = lax.broadcasted_iota(jnp.int32, (LANE, hv * HEAD), 1)
    expand = jnp.where(er == jnp.right_shift(ec, 7), 1.0, 0.0).astype(F32)
    return (jnp.dot(gc, expand, precision=HI, preferred_element_type=F32),
            jnp.dot(beta, expand, precision=HI, preferred_element_type=F32))


def _gates_fwd(ab, a_log, dt_bias, hv, name):
    s = ab.shape[0]
    r = _tile(s, GATE_ROWS, CHUNK)

    def body(a_ref, b_ref, al_ref, dt_ref, gc_ref, bb_ref):
        gcb, btb = _gates_tile(a_ref[...], b_ref[...], al_ref[...], dt_ref[...], hv)
        gc_ref[...] = gcb
        bb_ref[...] = btb

    vec = pl.BlockSpec((1, LANE), lambda i: (0, 0))
    wide = pl.BlockSpec((r, hv * HEAD), lambda i: (i, 0))
    return pl.pallas_call(
        body, name=name, grid=(s // r,),
        in_specs=[pl.BlockSpec((r, LANE), lambda i: (i, 0)), pl.BlockSpec((r, LANE), lambda i: (i, 1)), vec, vec],
        out_specs=[wide, wide], out_shape=[SDS((s, hv * HEAD), F32)] * 2,
        compiler_params=_params("parallel"))(ab, ab, a_log, dt_bias)


def _gates_bwd(ab, a_log, dt_bias, dgcb, dbtb, hv, name):
    s = ab.shape[0]
    r = _tile(s, GATE_ROWS, CHUNK)

    def body(a_ref, b_ref, al_ref, dt_ref, dgc_ref, dbb_ref, dab_ref, dal_ref, ddt_ref):
        i = pl.program_id(0)
        _, vjp = jax.vjp(functools.partial(_gates_tile, hv=hv), a_ref[...], b_ref[...], al_ref[...], dt_ref[...])
        da, dbt, dal, ddt = vjp((dgc_ref[...], dbb_ref[...]))
        dab_ref[:, 0:LANE] = da
        dab_ref[:, LANE:] = dbt

        @pl.when(i == 0)
        def _():
            dal_ref[...] = dal
            ddt_ref[...] = ddt

        @pl.when(i > 0)
        def _():
            dal_ref[...] += dal
            ddt_ref[...] += ddt

    vec = pl.BlockSpec((1, LANE), lambda i: (0, 0))
    wide = pl.BlockSpec((r, hv * HEAD), lambda i: (i, 0))
    return pl.pallas_call(
        body, name=name, grid=(s // r,),
        in_specs=[pl.BlockSpec((r, LANE), lambda i: (i, 0)), pl.BlockSpec((r, LANE), lambda i: (i, 1)), vec, vec, wide, wide],
        out_specs=[pl.BlockSpec((r, 2 * LANE), lambda i: (i, 0)), vec, vec],
        out_shape=[SDS((s, 2 * LANE), F32), SDS((1, LANE), F32), SDS((1, LANE), F32)],
        compiler_params=_params("arbitrary"))(ab, ab, a_log, dt_bias, dgcb, dbtb)


def _split_bf16(a):
    hi = a.astype(BF)
    return hi, (a - hi.astype(F32)).astype(BF)


def _dot3(a, b, dims=(((1,), (0,)), ((), ()))):
    ah, al = _split_bf16(a)
    bh, bl = _split_bf16(b)
    d = lambda x, y: lax.dot_general(x, y, dims, preferred_element_type=F32)
    return d(ah, bh) + (d(ah, bl) + d(al, bh))


NNB = (((2,), (1,)), ((0,), (0,)))
NTB = (((2,), (2,)), ((0,), (0,)))
TNB = (((1,), (1,)), ((0,), (0,)))


def _bdot(a, b, dims):
    return lax.dot_general(a, b, dims, preferred_element_type=F32)


def _unit_lower_inverse(a):
    ri = lax.broadcasted_iota(jnp.int32, a.shape, 1)
    ci = lax.broadcasted_iota(jnp.int32, a.shape, 2)
    p = -a
    t = jnp.where(ri == ci, 1.0, 0.0) + p
    for _ in range(5):
        p = _dot3(p, p, NNB)
        t = t + _dot3(t, p, NNB)
    return t


@jax.custom_vjp
def _known_inverse(a, t):
    return t


def _known_inverse_fwd(a, t):
    return t, t


def _known_inverse_bwd(t, g):
    return -_dot3(_dot3(t, g, TNB), t, NTB), jnp.zeros_like(t)


_known_inverse.defvjp(_known_inverse_fwd, _known_inverse_bwd)


def _delta_decay(gcb):
    c = CHUNK
    shape = (gcb.shape[0], c, c)
    ri = lax.broadcasted_iota(jnp.int32, shape, 1)
    ci = lax.broadcasted_iota(jnp.int32, shape, 2)
    causal = ri >= ci
    grow = jnp.stack([jnp.concatenate([gcb[b], gcb[b]], axis=0).T[:c, :c] for b in range(shape[0])])
    return jnp.where(causal, jnp.exp(jnp.where(causal, gcb[:, :, :c] - grow, 0.0)), 0.0), ri > ci


def _delta_system(k, gcb, btb):
    decay, strict = _delta_decay(gcb)
    return jnp.where(strict, _bdot((k * btb).astype(BF), k.astype(BF), NTB) * decay, 0.0)


def _delta_prep(q, k, v, gcb, btb, tinv):
    decay, strict = _delta_decay(gcb)
    kb = k * btb
    kbf = k.astype(BF)
    a = jnp.where(strict, _bdot(kb.astype(BF), kbf, NTB) * decay, 0.0)
    t = _known_inverse(a, tinv).astype(BF)
    u = _bdot(t, (v * btb).astype(BF), NNB)
    w = _bdot(t, (kb * jnp.exp(gcb)).astype(BF), NNB)
    attn = _bdot(q.astype(BF), kbf, NTB) * decay
    return u, w, attn


def _delta_scan(u, w, attn, q, k, gcb, s_in):
    c = CHUNK
    glast = gcb[:, c - 1:c, :]
    sb = s_in.astype(BF)
    v_new = u - _bdot(w.astype(BF), sb, NNB)
    vnb = v_new.astype(BF)
    o = _bdot((q * jnp.exp(gcb)).astype(BF), sb, NNB) + _bdot(attn.astype(BF), vnb, NNB)
    ks = (k * jnp.exp(glast - gcb)).astype(BF)
    s_out = s_in * jnp.exp(glast[:, :, 0:1]) + _bdot(ks, vnb, TNB)
    return o, s_out


def _head_stack(ref, rows, width, heads, rep=1):
    return jnp.stack([ref[rows, pl.ds((hh // rep) * width, width)].astype(F32) for hh in range(heads)])


PREP_ROWS = 1024
PREP_HEADS = 2
SCAN_ROWS = 128
SCAN_HEADS = 16


def _delta_prep_fwd(q, k, v, gcb, btb, name):
    s, dv = v.shape
    hv = dv // HEAD
    g = PREP_HEADS
    assert dv // q.shape[1] == g
    r = _tile(s, PREP_ROWS, CHUNK)

    def body(q_ref, k_ref, v_ref, g_ref, b_ref, u_ref, w_ref, a_ref, t_ref):
        nb = r // CHUNK
        qc = q_ref[...].astype(F32).reshape(nb, CHUNK, HEAD)
        kc = k_ref[...].astype(F32).reshape(nb, CHUNK, HEAD)
        for hh in range(g):
            cols = pl.ds(hh * HEAD, HEAD)
            half = pl.ds(hh * CHUNK, CHUNK)
            gc = g_ref[:, cols].reshape(nb, CHUNK, HEAD)
            bc = b_ref[:, cols].reshape(nb, CHUNK, HEAD)
            tinv = _unit_lower_inverse(_delta_system(kc, gc, bc))
            u, w, attn = _delta_prep(qc, kc, v_ref[:, cols].astype(F32).reshape(nb, CHUNK, HEAD), gc, bc, tinv)
            u_ref[:, cols] = u.reshape(r, HEAD)
            w_ref[:, cols] = w.reshape(r, HEAD).astype(BF)
            a_ref[:, half] = attn.reshape(r, CHUNK).astype(BF)
            t_ref[:, half] = tinv.reshape(r, CHUNK)

    kq = pl.BlockSpec((r, HEAD), lambda j, i: (i, j))
    vs = pl.BlockSpec((r, g * HEAD), lambda j, i: (i, j))
    sq = pl.BlockSpec((r, g * CHUNK), lambda j, i: (i, j))
    return pl.pallas_call(
        body, name=name, grid=(hv // g, s // r), in_specs=[kq, kq, vs, vs, vs], out_specs=[vs, vs, sq, sq],
        out_shape=[SDS((s, dv), F32), SDS((s, dv), BF), SDS((s, hv * CHUNK), BF), SDS((s, hv * CHUNK), F32)],
        compiler_params=_params("parallel", "parallel"))(q, k, v, gcb, btb)


def _delta_prep_bwd(q, k, v, gcb, btb, tinv, du, dw, dattn, dq_s, dk_s, dg_s, name):
    s, dv = v.shape
    hv = dv // HEAD
    g = PREP_HEADS
    r = _tile(s, PREP_ROWS, CHUNK)

    def body(q_ref, k_ref, v_ref, g_ref, b_ref, t_ref, du_ref, dw_ref, da_ref, dqs_ref, dks_ref, dgs_ref,
             dq_ref, dk_ref, dv_ref, dg_ref, db_ref):
        nb = r // CHUNK
        wide = lambda ref, cols: ref[:, cols].astype(F32).reshape(nb, CHUNK, HEAD)
        qc = q_ref[...].astype(F32).reshape(nb, CHUNK, HEAD)
        kc = k_ref[...].astype(F32).reshape(nb, CHUNK, HEAD)
        for hh in range(g):
            cols = pl.ds(hh * HEAD, HEAD)
            half = pl.ds(hh * CHUNK, CHUNK)
            fn = functools.partial(_delta_prep, tinv=t_ref[:, half].reshape(nb, CHUNK, CHUNK))
            _, vjp = jax.vjp(fn, qc, kc, wide(v_ref, cols), wide(g_ref, cols), wide(b_ref, cols))
            dq, dk, dvv, dg, db = vjp((wide(du_ref, cols), wide(dw_ref, cols),
                                       da_ref[:, half].astype(F32).reshape(nb, CHUNK, CHUNK)))
            dq_ref[:, cols] = dq.reshape(r, HEAD) + dqs_ref[:, cols]
            dk_ref[:, cols] = dk.reshape(r, HEAD) + dks_ref[:, cols]
            dv_ref[:, cols] = dvv.reshape(r, HEAD)
            dg_ref[:, cols] = dg.reshape(r, HEAD) + dgs_ref[:, cols]
            db_ref[:, cols] = db.reshape(r, HEAD)

    kq = pl.BlockSpec((r, HEAD), lambda j, i: (i, j))
    vs = pl.BlockSpec((r, g * HEAD), lambda j, i: (i, j))
    sq = pl.BlockSpec((r, g * CHUNK), lambda j, i: (i, j))
    return pl.pallas_call(
        body, name=name, grid=(hv // g, s // r), in_specs=[kq, kq, vs, vs, vs, sq, vs, vs, sq, vs, vs, vs],
        out_specs=[vs] * 5, out_shape=[SDS((s, dv), F32)] * 5,
        compiler_params=_params("parallel", "parallel"))(q, k, v, gcb, btb, tinv, du, dw, dattn, dq_s, dk_s, dg_s)


def _delta_scan_fwd(u, w, attn, q, k, gcb, name):
    s, dv = u.shape
    hv = dv // HEAD
    rep = dv // q.shape[1]
    g = min(SCAN_HEADS, hv)
    r = _tile(s, SCAN_ROWS, CHUNK)
    per = r // CHUNK

    def body(u_ref, w_ref, a_ref, q_ref, k_ref, g_ref, o_ref, st_ref, state):
        @pl.when(pl.program_id(1) == 0)
        def _():
            state[...] = jnp.zeros_like(state)

        def chunk(cc, carry):
            rows = pl.ds(pl.multiple_of(cc * CHUNK, CHUNK), CHUNK)
            s_in = state[...]
            o, s_out = _delta_scan(_head_stack(u_ref, rows, HEAD, g), _head_stack(w_ref, rows, HEAD, g),
                                   _head_stack(a_ref, rows, CHUNK, g), _head_stack(q_ref, rows, HEAD, g, rep),
                                   _head_stack(k_ref, rows, HEAD, g, rep), _head_stack(g_ref, rows, HEAD, g), s_in)
            for hh in range(g):
                st_ref[hh, cc] = s_in[hh]
                o_ref[rows, pl.ds(hh * HEAD, HEAD)] = o[hh].astype(BF)
            state[...] = s_out
            return carry

        lax.fori_loop(0, per, chunk, 0)

    kq = pl.BlockSpec((r, g // rep * HEAD), lambda j, i: (i, j))
    vs = pl.BlockSpec((r, g * HEAD), lambda j, i: (i, j))
    sq = pl.BlockSpec((r, g * CHUNK), lambda j, i: (i, j))
    return pl.pallas_call(
        body, name=name, grid=(hv // g, s // r), in_specs=[vs, vs, sq, kq, kq, vs],
        out_specs=[vs, pl.BlockSpec((g, per, HEAD, HEAD), lambda j, i: (j, i, 0, 0))],
        out_shape=[SDS((s, dv), BF), SDS((hv, s // CHUNK, HEAD, HEAD), F32)],
        scratch_shapes=[pltpu.VMEM((g, HEAD, HEAD), F32)],
        compiler_params=_params("parallel", "arbitrary"))(u, w, attn, q, k, gcb)


def _delta_scan_bwd(u, w, attn, q, k, gcb, states, do, name):
    s, dv = u.shape
    hv = dv // HEAD
    rep = dv // q.shape[1]
    g = min(SCAN_HEADS, hv)
    r = _tile(s, SCAN_ROWS, CHUNK)
    per = r // CHUNK
    nb = s // r

    def body(u_ref, w_ref, a_ref, q_ref, k_ref, g_ref, st_ref, do_ref, du_ref, dw_ref, da_ref, dq_ref, dk_ref, dg_ref, dstate):
        @pl.when(pl.program_id(1) == 0)
        def _():
            dstate[...] = jnp.zeros_like(dstate)

        def chunk(t, carry):
            cc = per - 1 - t
            rows = pl.ds(pl.multiple_of(cc * CHUNK, CHUNK), CHUNK)
            s_in = jnp.stack([st_ref[hh, cc] for hh in range(g)])
            _, vjp = jax.vjp(_delta_scan, _head_stack(u_ref, rows, HEAD, g), _head_stack(w_ref, rows, HEAD, g),
                             _head_stack(a_ref, rows, CHUNK, g), _head_stack(q_ref, rows, HEAD, g, rep),
                             _head_stack(k_ref, rows, HEAD, g, rep), _head_stack(g_ref, rows, HEAD, g), s_in)
            du, dw, da, dq, dk, dg, ds_in = vjp((_head_stack(do_ref, rows, HEAD, g), dstate[...]))
            for hh in range(g):
                cols = pl.ds(hh * HEAD, HEAD)
                du_ref[rows, cols] = du[hh].astype(BF)
                dw_ref[rows, cols] = dw[hh].astype(BF)
                da_ref[rows, pl.ds(hh * CHUNK, CHUNK)] = da[hh].astype(BF)
                dq_ref[rows, cols] = dq[hh]
                dk_ref[rows, cols] = dk[hh]
                dg_ref[rows, cols] = dg[hh]
            dstate[...] = ds_in
            return carry

        lax.fori_loop(0, per, chunk, 0)

    kq = pl.BlockSpec((r, g // rep * HEAD), lambda j, i: (nb - 1 - i, j))
    vs = pl.BlockSpec((r, g * HEAD), lambda j, i: (nb - 1 - i, j))
    sq = pl.BlockSpec((r, g * CHUNK), lambda j, i: (nb - 1 - i, j))
    return pl.pallas_call(
        body, name=name, grid=(hv // g, nb),
        in_specs=[vs, vs, sq, kq, kq, vs, pl.BlockSpec((g, per, HEAD, HEAD), lambda j, i: (j, nb - 1 - i, 0, 0)), vs],
        out_specs=[vs, vs, sq, vs, vs, vs],
        out_shape=[SDS((s, dv), BF), SDS((s, dv), BF), SDS((s, hv * CHUNK), BF)] + [SDS((s, dv), F32)] * 3,
        scratch_shapes=[pltpu.VMEM((g, HEAD, HEAD), F32)],
        compiler_params=_params("parallel", "arbitrary"))(u, w, attn, q, k, gcb, states, do)


def _gdn_out_tile(o, gate, gain):
    return _headnorm(o, gain) * _silu(gate)


def _gdn_out_fwd(o, proj, gate_col0, gain, name):
    s, dv = o.shape
    r = _tile(s, 128, 16)

    def body(o_ref, g_ref, gain_ref, y_ref):
        y_ref[...] = _gdn_out_tile(o_ref[...].astype(F32), g_ref[...].astype(F32), gain_ref[...]).astype(BF)

    row = pl.BlockSpec((r, dv), lambda i: (i, 0))
    return pl.pallas_call(
        body, name=name, grid=(s // r,),
        in_specs=[row, pl.BlockSpec((r, dv), lambda i: (i, gate_col0)), pl.BlockSpec((1, HEAD), lambda i: (0, 0))],
        out_specs=row, out_shape=SDS((s, dv), BF), compiler_params=_params("parallel"))(o, proj, gain)


def _gdn_out_bwd(o, proj, gate_col0, gain, dy, name):
    s, dv = o.shape
    r = _tile(s, 128, 16)

    def body(o_ref, g_ref, gain_ref, dy_ref, do_ref, dg_ref, dgain_ref):
        i = pl.program_id(0)
        _, vjp = jax.vjp(_gdn_out_tile, o_ref[...].astype(F32), g_ref[...].astype(F32), gain_ref[...])
        do, dg, dgain = vjp(dy_ref[...].astype(F32))
        do_ref[...] = do
        dg_ref[...] = dg.astype(BF)

        @pl.when(i == 0)
        def _():
            dgain_ref[...] = dgain

        @pl.when(i > 0)
        def _():
            dgain_ref[...] += dgain

    row = pl.BlockSpec((r, dv), lambda i: (i, 0))
    vec = pl.BlockSpec((1, HEAD), lambda i: (0, 0))
    return pl.pallas_call(
        body, name=name, grid=(s // r,),
        in_specs=[row, pl.BlockSpec((r, dv), lambda i: (i, gate_col0)), vec, row],
        out_specs=[row, row, vec], out_shape=[SDS((s, dv), F32), SDS((s, dv), BF), SDS((1, HEAD), F32)],
        compiler_params=_params("arbitrary"))(o, proj, gain, dy)


WIDE_K = dict(tm=512, tn=1024, tk=8192, n_outer=True)
DEEP_K = 4096


def _ffn_forward(x, gain, wu, wg, cw, wd, tag):
    h, ht = _rmsnorm_fwd(x, gain, f"{tag}_norm")
    uu = _mm(h, wu, name=f"{tag}_up_u")
    ug = _mm(h, wg, name=f"{tag}_up_g")
    a, at = _ffn_act_fwd(uu, ug, cw, f"{tag}_act")
    y = _mm(a, wd, res=x, out_dtype=F32, name=f"{tag}_down", **WIDE_K)
    return y, (x, ht, uu, ug, at)


def _ffn_backward(saved, dys, gain, wu, wg, cw, wd, tag, early=None):
    x, ht, uu, ug, at = saved
    dy, dyb = dys
    da = _mm(dyb, wd, tb=True, name=f"{tag}_d_act")
    dwd = _mm(at, dyb, out_dtype=F32, name=f"{tag}_d_wd", tk=DEEP_K)
    duu, dug, dcw = _ffn_act_bwd(uu, ug, cw, da, f"{tag}_act_bwd")
    dwu = _mm(ht, duu, out_dtype=F32, name=f"{tag}_d_wu", tk=DEEP_K)
    dwg = _mm(ht, dug, out_dtype=F32, name=f"{tag}_d_wg", tk=DEEP_K)
    grads = dict(wu=dwu, wg=dwg, cw=dcw, wd=dwd)
    zero = early(grads) if early is not None else 0.0
    dh = _mm_pair_nt(duu, wu, dug, wg, f"{tag}_d_h")
    dx, dxb, dgain = _rmsnorm_bwd(x, gain + zero, dh, dy, f"{tag}_norm_bwd")
    return (dx, dxb), dict(grads, gain=dgain)


def _att_forward(x, gain, p, tag):
    h, ht = _rmsnorm_fwd(x, gain, f"{tag}_norm")
    qkv = _mm(h, p["wqkv"], name=f"{tag}_qkv")
    q, kpad, vpad = _qkv_post_fwd(qkv, p["qg"], p["kg"], f"{tag}_qknorm")
    bias = _bias_expand(p["rel"], f"{tag}_bias")
    o = _attn_fwd(q, kpad, vpad, bias, f"{tag}_core")
    y = _mm(o, p["wo"], res=x, out_dtype=F32, name=f"{tag}_out")
    return y, (x, ht, qkv, q, kpad, vpad, bias, o)


def _att_backward(saved, dys, gain, p, tag, early=None):
    x, ht, qkv, q, kpad, vpad, bias, o = saved
    dy, dyb = dys
    do = _mm(dyb, p["wo"], tb=True, name=f"{tag}_d_o")
    dwo = _mm(o, dyb, ta=True, out_dtype=F32, name=f"{tag}_d_wo")
    dq, dkpad, dvpad, dbias = _attn_bwd(q, kpad, vpad, bias, do, f"{tag}_core_bwd")
    drel = _bias_reduce(dbias, f"{tag}_bias_bwd")
    dqkv, dqg, dkg = _qkv_post_bwd(qkv, p["qg"], p["kg"], dq, dkpad, dvpad, f"{tag}_qknorm_bwd")
    dwqkv = _mm(ht, dqkv, out_dtype=F32, name=f"{tag}_d_wqkv", tk=DEEP_K)
    grads = dict(wqkv=dwqkv, qg=dqg, kg=dkg, rel=drel, wo=dwo)
    zero = early(grads) if early is not None else 0.0
    dh = _mm(dqkv, p["wqkv"], tb=True, out_dtype=F32, name=f"{tag}_d_h", **WIDE_K)
    dx, dxb, dgain = _rmsnorm_bwd(x, gain + zero, dh, dy, f"{tag}_norm_bwd")
    return (dx, dxb), dict(grads, gain=dgain)


def _gdn_forward(x, gain, p, tag):
    d = x.shape[1]
    nk = d // HEAD
    hv = 2 * nk
    h, ht = _rmsnorm_fwd(x, gain, f"{tag}_norm")
    proj = _mm(h, p["wmain"], name=f"{tag}_proj")
    ab = _mm(h, p["wab"], out_dtype=F32, name=f"{tag}_proj_ab")
    q = _gdn_conv_fwd(proj, p["cw"], "q", 0, nk, f"{tag}_conv_q")
    k = _gdn_conv_fwd(proj, p["cw"], "k", nk, nk, f"{tag}_conv_k")
    v = _gdn_conv_fwd(proj, p["cw"], "v", 2 * nk, hv, f"{tag}_conv_v")
    gcb, btb = _gates_fwd(ab, p["a_log"], p["dt_bias"], hv, f"{tag}_gates")
    u, wd, attn, tinv = _delta_prep_fwd(q, k, v, gcb, btb, f"{tag}_delta_prep")
    o, states = _delta_scan_fwd(u, wd, attn, q, k, gcb, f"{tag}_delta_scan")
    og = _gdn_out_fwd(o, proj, 2, p["o_gain"], f"{tag}_onorm")
    y = _mm(og, p["wo"], res=x, out_dtype=F32, name=f"{tag}_out")
    return y, (x, ht, proj, ab, q, k, v, gcb, btb, u, wd, attn, tinv, o, states, og)


def _gdn_backward(saved, dys, gain, p, tag, early=None):
    x, ht, proj, ab, q, k, v, gcb, btb, u, wd, attn, tinv, o, states, og = saved
    dy, dyb = dys
    d = x.shape[1]
    nk = d // HEAD
    hv = 2 * nk
    dog = _mm(dyb, p["wo"], tb=True, name=f"{tag}_d_og")
    dwo = _mm(og, dyb, ta=True, out_dtype=F32, name=f"{tag}_d_wo")
    do, dgate, dogain = _gdn_out_bwd(o, proj, 2, p["o_gain"], dog, f"{tag}_onorm_bwd")
    du, dw, dattn, dq_s, dk_s, dg_s = _delta_scan_bwd(u, wd, attn, q, k, gcb, states, do, f"{tag}_delta_scan_bwd")
    dq, dk, dv, dgcb, dbtb = _delta_prep_bwd(q, k, v, gcb, btb, tinv, du, dw, dattn, dq_s, dk_s, dg_s, f"{tag}_delta_prep_bwd")
    dab, dalog, ddt = _gates_bwd(ab, p["a_log"], p["dt_bias"], dgcb, dbtb, hv, f"{tag}_gates_bwd")
    dpq, dcq = _gdn_conv_bwd(proj, p["cw"], dq, "q", 0, nk, f"{tag}_conv_q_bwd")
    dpk, dck = _gdn_conv_bwd(proj, p["cw"], dk, "k", nk, nk, f"{tag}_conv_k_bwd")
    dpv, dcv = _gdn_conv_bwd(proj, p["cw"], dv, "v", 2 * nk, hv, f"{tag}_conv_v_bwd")
    dproj = jnp.concatenate([dpq, dpk, dpv, dgate], axis=1)
    dcw = jnp.concatenate([dcq, dck, dcv], axis=1)
    dwmain = _mm(ht, dproj, out_dtype=F32, name=f"{tag}_d_wmain", tk=DEEP_K)
    dwab = _mm(ht, dab, out_dtype=F32, name=f"{tag}_d_wab")
    grads = dict(wmain=dwmain, wab=dwab, cw=dcw, a_log=dalog, dt_bias=ddt, o_gain=dogain, wo=dwo)
    zero = early(grads) if early is not None else 0.0
    dh = _mm(dproj, p["wmain"], tb=True, out_dtype=F32, name=f"{tag}_d_h_main", **WIDE_K)
    dh = _mm(dab, p["wab"], tb=True, res=dh, out_dtype=F32, name=f"{tag}_d_h_ab")
    dx, dxb, dgain = _rmsnorm_bwd(x, gain + zero, dh, dy, f"{tag}_norm_bwd")
    return (dx, dxb), dict(grads, gain=dgain)


def _resolve(entry, after):
    return entry(after) if callable(entry) else entry


def _local_step(x, target, w, sink=None):
    depth = len(w["ffn"])
    tape = []
    for i in range(depth):
        kind, j = i % 3, i // 3
        gain = w["mix_norm"][i:i + 1]
        if kind == 0:
            x, saved = _att_forward(x, gain, _resolve(w["att"][j], x), f"l{i}_att")
        elif kind == 1:
            x_in = x
            pw = _resolve(w["pool"][j], x)
            x = _pool_fwd(x_in, gain, pw["w"], pw["scale"], f"l{i}_pool")
            saved = x_in
        else:
            x, saved = _gdn_forward(x, gain, _resolve(w["gdn"][j], x), f"l{i}_gdn")
        f = _resolve(w["ffn"][i], x)
        x, fsaved = _ffn_forward(x, w["ffn_norm"][i:i + 1], f["wu"], f["wg"], f["cw"], f["wd"], f"l{i}_ffn")
        tape.append((saved, fsaved))
    dy, dyb, loss_row = _loss_head(x, target, "loss_head")
    dy = (dy, dyb)
    grads = dict(mix=[None] * depth, ffn=[None] * depth)
    zero = 0.0
    for i in reversed(range(depth)):
        kind, j = i % 3, i // 3
        saved, fsaved = tape[i]
        early = (lambda name, layer: functools.partial(sink, name, layer)) if sink is not None else (lambda name, layer: None)
        f = _resolve(w["ffn"][i], dy[0])
        dy, grads["ffn"][i] = _ffn_backward(fsaved, dy, w["ffn_norm"][i:i + 1] + zero, f["wu"], f["wg"], f["cw"], f["wd"],
                                            f"l{i}_ffn", early("ffn", i))
        gain = w["mix_norm"][i:i + 1] + zero
        if kind == 0:
            dy, grads["mix"][i] = _att_backward(saved, dy, gain, _resolve(w["att"][j], dy[0]), f"l{i}_att", early("att", i))
        elif kind == 1:
            pw = _resolve(w["pool"][j], dy[0])
            dx, dgain, dw4, dscale = _pool_bwd(saved, gain, pw["w"], pw["scale"], dy[0], f"l{i}_pool_bwd")
            dy = (dx, dx.astype(BF))
            grads["mix"][i] = dict(gain=dgain, w=dw4, scale=dscale)
            if sink is not None:
                zero = zero + sink("pool", i, grads["mix"][i])
        else:
            dy, grads["mix"][i] = _gdn_backward(saved, dy, gain, _resolve(w["gdn"][j], dy[0]), f"l{i}_gdn", early("gdn", i))
    return loss_row, dy[0], grads


MESH = pl.DeviceIdType.MESH
ANY = pl.BlockSpec(memory_space=pl.ANY)


def _position():
    return tuple(lax.axis_index(a) for a in AXES)


def _flip(pos, rel):
    return tuple(1 - p if (rel >> (2 - i)) & 1 else p for i, p in enumerate(pos))


def _index(pos):
    return 4 * pos[0] + 2 * pos[1] + pos[2]


def _all_gather(arr, name):
    def body(x_ref, o_ref, send, recv, local):
        me = _position()
        mine = pltpu.make_async_copy(x_ref, o_ref.at[_index(me)], local)
        mine.start()
        copies = []
        for rel in range(1, N_DEV):
            cp = pltpu.make_async_remote_copy(
                src_ref=x_ref, dst_ref=o_ref.at[_index(me)], send_sem=send.at[rel - 1], recv_sem=recv.at[rel - 1],
                device_id=_flip(me, rel), device_id_type=MESH)
            cp.start()
            copies.append(cp)
        for cp in copies:
            cp.wait()
        mine.wait()

    return pl.pallas_call(
        body, name=name, in_specs=[ANY], out_specs=ANY, out_shape=SDS((N_DEV,) + arr.shape, arr.dtype),
        scratch_shapes=[pltpu.SemaphoreType.DMA((N_DEV - 1,)), pltpu.SemaphoreType.DMA((N_DEV - 1,)),
                        pltpu.SemaphoreType.DMA(())])(arr)


def _exchange(arr, name):
    def body(x_ref, o_ref, send, recv, local):
        me = _position()
        mine = pltpu.make_async_copy(x_ref.at[_index(me)], o_ref.at[_index(me)], local)
        mine.start()
        copies = []
        for rel in range(1, N_DEV):
            peer = _flip(me, rel)
            cp = pltpu.make_async_remote_copy(
                src_ref=x_ref.at[_index(peer)], dst_ref=o_ref.at[_index(me)], send_sem=send.at[rel - 1],
                recv_sem=recv.at[rel - 1], device_id=peer, device_id_type=MESH)
            cp.start()
            copies.append(cp)
        for cp in copies:
            cp.wait()
        mine.wait()

    return pl.pallas_call(
        body, name=name, in_specs=[ANY], out_specs=ANY, out_shape=SDS(arr.shape, arr.dtype),
        scratch_shapes=[pltpu.SemaphoreType.DMA((N_DEV - 1,)), pltpu.SemaphoreType.DMA((N_DEV - 1,)),
                        pltpu.SemaphoreType.DMA(())])(arr)


HBM = pl.BlockSpec(memory_space=pltpu.HBM)
SEM = pl.BlockSpec(memory_space=pltpu.SEMAPHORE)
EFFECT = pltpu.SideEffectType.DATAFLOW_SIDE_EFFECTING


def _split_copies(x_ref, land_ref, send, recv, scatter):
    me = _position()
    copies = []
    for rel in range(1, N_DEV):
        peer = _flip(me, rel)
        copies.append(pltpu.make_async_remote_copy(
            src_ref=x_ref.at[_index(peer)] if scatter else x_ref, dst_ref=land_ref.at[_index(me)],
            send_sem=send.at[rel - 1], recv_sem=recv.at[rel - 1], device_id=peer, device_id_type=MESH))
    return copies


def _copies_start(arr, scatter, name):
    shape = arr.shape if scatter else (N_DEV,) + arr.shape

    def body(x_ref, land_ref, send, recv, x_thru, land_thru, token):
        for cp in _split_copies(x_ref, land_ref, send, recv, scatter):
            cp.start()
        token[...] = jnp.zeros_like(token)

    sems = pltpu.SemaphoreType.DMA((N_DEV - 1,))
    send, recv, x_thru, land_thru, token = pl.pallas_call(
        body, name=name,
        out_shape=(sems, sems, pltpu.HBM(arr.shape, arr.dtype), pltpu.HBM(shape, arr.dtype), SDS((8, LANE), F32)),
        in_specs=(HBM, HBM), out_specs=(SEM, SEM, HBM, HBM, pl.BlockSpec(memory_space=pltpu.VMEM)),
        input_output_aliases={0: 2, 1: 3}, compiler_params=pltpu.CompilerParams(has_side_effects=EFFECT),
    )(pltpu.with_memory_space_constraint(arr, pltpu.HBM), pltpu.with_memory_space_constraint(lax.empty(shape, arr.dtype), pltpu.HBM))
    return (send, recv, x_thru, land_thru, scatter, name), token[0, 0]


def _copies_wait(handle, after):
    send, recv, x_thru, land_thru, scatter, name = handle

    def body(x_ref, land_ref, send_ref, recv_ref, after_ref, x_dead, got_ref):
        for cp in _split_copies(x_ref, land_ref, send_ref, recv_ref, scatter):
            cp.wait_send()
            cp.wait_recv()

    return pl.pallas_call(
        body, name=name + "_wait",
        out_shape=(pltpu.HBM(x_thru.shape, x_thru.dtype), pltpu.HBM(land_thru.shape, land_thru.dtype)),
        in_specs=(HBM, HBM, SEM, SEM, ANY), out_specs=(HBM, HBM), input_output_aliases={0: 0, 1: 1},
        compiler_params=pltpu.CompilerParams(has_side_effects=EFFECT),
    )(x_thru, land_thru, send, recv, after)[1]


def _with_own(got, own):
    return lax.dynamic_update_index_in_dim(got, own.astype(got.dtype), _index(_position()), 0)


def _adamw(parts, w, m, v, name):
    if not isinstance(parts, (list, tuple)):
        parts = [parts]
    layers = len(parts)
    r, c = parts[0].shape[1:]
    assert w.shape == (layers * r, c), (w.shape, parts[0].shape, layers)
    tr = _tile(r, 128, 16)
    per = r // tr
    c1 = 1.0 / (1.0 - ADAM_B1 ** ADAM_STEP)
    c2 = 1.0 / (1.0 - ADAM_B2 ** ADAM_STEP)

    def body(*refs):
        p_refs = refs[:layers]
        w_ref, m_ref, v_ref, g_ref, d_ref, nm_ref, nv_ref = refs[layers:]

        def update(p_ref):
            g = p_ref[0].astype(F32)
            for s in range(1, N_DEV):
                g = g + p_ref[s].astype(F32)
            nm = ADAM_B1 * m_ref[...] + (1.0 - ADAM_B1) * g
            nv = ADAM_B2 * v_ref[...] + (1.0 - ADAM_B2) * (g * g)
            g_ref[...] = g
            nm_ref[...] = nm
            nv_ref[...] = nv
            d_ref[...] = -ADAM_LR * ((nm * c1) / (jnp.sqrt(nv * c2) + ADAM_EPS) + ADAM_WD * w_ref[...])

        if layers == 1:
            update(p_refs[0])
        else:
            for j in range(layers):
                pl.when(pl.program_id(0) == j)(functools.partial(update, p_refs[j]))

    p_specs = [pl.BlockSpec((N_DEV, tr, c), functools.partial(lambda l, i, j: (0, jnp.where(l == j, i, 0), 0), j=j))
               for j in range(layers)]
    row = pl.BlockSpec((tr, c), lambda l, i: (l * per + i, 0))
    return pl.pallas_call(
        body, name=name, grid=(layers, per), in_specs=p_specs + [row, row, row],
        out_specs=[row] * 4, out_shape=[SDS(w.shape, F32)] * 4, compiler_params=_params("arbitrary", "arbitrary"))(*parts, w, m, v)


PACK = 8 * LANE


def _pack(arrs):
    flat = []
    for a in arrs:
        a = a.reshape(-1).astype(F32)
        flat.append(jnp.pad(a, (0, (-a.shape[0]) % PACK)))
    return jnp.concatenate(flat).reshape(-1, LANE)


def _unpack(packed, shapes, lead=()):
    flat = packed.reshape(lead + (-1,))
    out, off = [], 0
    for shp in shapes:
        n = int(np.prod(shp))
        out.append(flat[..., off:off + n].reshape(lead + tuple(shp)))
        off += n + (-n) % PACK
    return out


def _pad_to(a, axis, size):
    pad = [(0, 0)] * a.ndim
    pad[axis] = (0, size - a.shape[axis])
    return jnp.pad(a, pad)


def _cols_from_shards(g):
    return jnp.transpose(g, (1, 0, 2)).reshape(g.shape[1], -1)


def _cols_to_shards(a):
    c = a.shape[-1] // N_DEV
    a = a.reshape(a.shape[:-1] + (N_DEV, c))
    return jnp.moveaxis(a, -2, 0)


def _rows_to_shards(a):
    r = a.shape[-2] // N_DEV
    a = a.reshape(a.shape[:-2] + (N_DEV, r, a.shape[-1]))
    return jnp.moveaxis(a, -3, 0)


WEIGHTS = ("mix_norm", "ffn_norm", "att_w_qkv", "att_q_gain", "att_k_gain", "att_rel_bias", "att_w_o", "pool_w",
           "pool_scale", "gdn_w_in", "gdn_conv", "gdn_a_log", "gdn_dt_bias", "gdn_o_gain", "gdn_w_o", "ffn_w_up",
           "ffn_conv", "ffn_w_down")
REPLICATED = ("mix_norm", "ffn_norm", "att_q_gain", "att_k_gain", "pool_scale", "gdn_a_log", "gdn_dt_bias", "gdn_o_gain")
SMALL_SHARDED = ("att_rel_bias", "gdn_conv", "ffn_conv")
BIG = ("att_w_qkv", "att_w_o", "pool_w", "gdn_w_in", "gdn_w_o", "ffn_w_up", "ffn_w_down")
KEEP_F32 = ("pool_w",)


def _memo(build):
    cache = []

    def entry(after):
        if not cache:
            cache.append(build(after))
        return cache[0]

    return entry


def _assemble_weights(w, get, small, f):
    d = w["mix_norm"].shape[1]
    nk = d // HEAD
    hv = 2 * nk
    fp = -(-f // FF_PAD) * FF_PAD
    out = dict(mix_norm=w["mix_norm"], ffn_norm=w["ffn_norm"], att=[], pool=[], gdn=[], ffn=[])

    def att(j, after):
        return dict(wqkv=_cols_from_shards(get("att_w_qkv", j, after)), wo=get("att_w_o", j, after).reshape(d, d),
                    qg=w["att_q_gain"][j:j + 1], kg=w["att_k_gain"][j:j + 1], rel=small["att_rel_bias"][j])

    def pool(j, after):
        g = get("pool_w", j, after)
        return dict(w=jnp.transpose(g, (1, 0, 2, 3)).reshape(g.shape[1], g.shape[3], g.shape[3]),
                    scale=w["pool_scale"][j:j + 1])

    def gdn(j, after):
        win = _cols_from_shards(get("gdn_w_in", j, after))
        nm = 6 * d
        wab = jnp.concatenate([_pad_to(win[:, nm:nm + hv], 1, LANE), _pad_to(win[:, nm + hv:], 1, LANE)], axis=1)
        return dict(wmain=win[:, :nm], wab=wab, cw=_pad_to(small["gdn_conv"][j], 0, 8),
                    a_log=_pad_to(w["gdn_a_log"][j:j + 1], 1, LANE), dt_bias=_pad_to(w["gdn_dt_bias"][j:j + 1], 1, LANE),
                    o_gain=w["gdn_o_gain"][j:j + 1], wo=get("gdn_w_o", j, after).reshape(2 * d, d))

    def ffn(i, after):
        g = get("ffn_w_up", i, after)
        half = N_DEV // 2
        tail = [jnp.zeros((d, fp - f), g.dtype)] if fp > f else []
        return dict(wu=jnp.concatenate([g[k] for k in range(half)] + tail, axis=1),
                    wg=jnp.concatenate([g[k] for k in range(half, N_DEV)] + tail, axis=1),
                    cw=_pad_to(_pad_to(small["ffn_conv"][i], 0, 8), 1, fp),
                    wd=_pad_to(get("ffn_w_down", i, after).reshape(f, d), 0, fp))

    for key, build, count in (("att", att, w["att_w_qkv"].shape[0]), ("pool", pool, w["pool_w"].shape[0]),
                              ("gdn", gdn, w["gdn_w_in"].shape[0]), ("ffn", ffn, w["ffn_w_up"].shape[0])):
        out[key] = [_memo(functools.partial(build, j)) for j in range(count)]
    return out


def _full_gradients(grads, w, f):
    d = w["mix_norm"].shape[1]
    nk = d // HEAD
    hv = 2 * nk
    depth = len(grads["ffn"])
    att = [grads["mix"][i] for i in range(depth) if i % 3 == 0]
    pool = [grads["mix"][i] for i in range(depth) if i % 3 == 1]
    gdn = [grads["mix"][i] for i in range(depth) if i % 3 == 2]
    ffn = grads["ffn"]
    win = [jnp.concatenate([g["wmain"], g["wab"][:, :hv], g["wab"][:, LANE:LANE + hv]], axis=1) for g in gdn]
    return dict(
        mix_norm=jnp.concatenate([g["gain"] for g in grads["mix"]], axis=0),
        ffn_norm=jnp.concatenate([g["gain"] for g in ffn], axis=0),
        att_w_qkv=jnp.stack([g["wqkv"] for g in att]),
        att_q_gain=jnp.concatenate([g["qg"] for g in att], axis=0),
        att_k_gain=jnp.concatenate([g["kg"] for g in att], axis=0),
        att_rel_bias=jnp.stack([g["rel"] for g in att]),
        att_w_o=jnp.stack([g["wo"] for g in att]),
        pool_w=jnp.stack([g["w"] for g in pool]),
        pool_scale=jnp.concatenate([g["scale"] for g in pool], axis=0),
        gdn_w_in=jnp.stack(win),
        gdn_conv=jnp.stack([g["cw"][:GDN_CONV] for g in gdn]),
        gdn_a_log=jnp.concatenate([g["a_log"][:, :hv] for g in gdn], axis=0),
        gdn_dt_bias=jnp.concatenate([g["dt_bias"][:, :hv] for g in gdn], axis=0),
        gdn_o_gain=jnp.concatenate([g["o_gain"] for g in gdn], axis=0),
        gdn_w_o=jnp.stack([g["wo"] for g in gdn]),
        ffn_w_up=jnp.stack([jnp.concatenate([g["wu"][:, :f], g["wg"][:, :f]], axis=1) for g in ffn]),
        ffn_conv=jnp.stack([g["cw"][:FFN_CONV, :f] for g in ffn]),
        ffn_w_down=jnp.stack([g["wd"][:f] for g in ffn]))


ROW_SHARDED = ("att_w_o", "gdn_w_o", "ffn_w_down")


def _to_shards(name, full):
    if name == "pool_w":
        r = full.shape[2] // N_DEV
        a = full.reshape(full.shape[:2] + (N_DEV, r, full.shape[3]))
        return jnp.moveaxis(a, 2, 0)
    return _rows_to_shards(full) if name in ROW_SHARDED else _cols_to_shards(full)


def kernel(x, mix_norm, ffn_norm, att_w_qkv, att_q_gain, att_k_gain, att_rel_bias, att_w_o, pool_w, pool_scale, gdn_w_in, gdn_conv, gdn_a_log, gdn_dt_bias, gdn_o_gain, gdn_w_o, ffn_w_up, ffn_conv, ffn_w_down, loss_target, m_mix_norm, m_ffn_norm, m_att_w_qkv, m_att_q_gain, m_att_k_gain, m_att_rel_bias, m_att_w_o, m_pool_w, m_pool_scale, m_gdn_w_in, m_gdn_conv, m_gdn_a_log, m_gdn_dt_bias, m_gdn_o_gain, m_gdn_w_o, m_ffn_w_up, m_ffn_conv, m_ffn_w_down, v_mix_norm, v_ffn_norm, v_att_w_qkv, v_att_q_gain, v_att_k_gain, v_att_rel_bias, v_att_w_o, v_pool_w, v_pool_scale, v_gdn_w_in, v_gdn_conv, v_gdn_a_log, v_gdn_dt_bias, v_gdn_o_gain, v_gdn_w_o, v_ffn_w_up, v_ffn_conv, v_ffn_w_down):
    w = dict(zip(WEIGHTS, (mix_norm, ffn_norm, att_w_qkv, att_q_gain, att_k_gain, att_rel_bias, att_w_o, pool_w, pool_scale, gdn_w_in, gdn_conv, gdn_a_log, gdn_dt_bias, gdn_o_gain, gdn_w_o, ffn_w_up, ffn_conv, ffn_w_down)))
    m = dict(zip(WEIGHTS, (m_mix_norm, m_ffn_norm, m_att_w_qkv, m_att_q_gain, m_att_k_gain, m_att_rel_bias, m_att_w_o, m_pool_w, m_pool_scale, m_gdn_w_in, m_gdn_conv, m_gdn_a_log, m_gdn_dt_bias, m_gdn_o_gain, m_gdn_w_o, m_ffn_w_up, m_ffn_conv, m_ffn_w_down)))
    v = dict(zip(WEIGHTS, (v_mix_norm, v_ffn_norm, v_att_w_qkv, v_att_q_gain, v_att_k_gain, v_att_rel_bias, v_att_w_o, v_pool_w, v_pool_scale, v_gdn_w_in, v_gdn_conv, v_gdn_a_log, v_gdn_dt_bias, v_gdn_o_gain, v_gdn_w_o, v_ffn_w_up, v_ffn_conv, v_ffn_w_down)))

    me = _index(_position())
    d = mix_norm.shape[1]
    hv = 2 * (d // HEAD)
    f = ffn_w_down.shape[1] * N_DEV
    depth = ffn_w_up.shape[0]

    small_shapes = [w[n].shape for n in SMALL_SHARDED]
    small_g = _all_gather(_pack([w[n] for n in SMALL_SHARDED]), "gather_small")
    small = {}
    for n, a in zip(SMALL_SHARDED, _unpack(small_g, small_shapes, lead=(N_DEV,))):
        small[n] = jnp.moveaxis(a, 0, -2).reshape(a.shape[1:-1] + (N_DEV * a.shape[-1],))
    order = []
    for i in range(depth):
        order += [[("att_w_qkv", i // 3), ("att_w_o", i // 3)], [("pool_w", i // 3)], [("gdn_w_in", i // 3), ("gdn_w_o", i // 3)]][i % 3]
        order += [("ffn_w_up", i), ("ffn_w_down", i)]
    after_small = small_g[0, 0, 0] * 0.0
    local = {(n, j): (w[n][j] + after_small if n in KEEP_F32 else (w[n][j] + after_small).astype(BF)) for n, j in order}
    arriving, zero = {}, 0.0
    for n, j in order:
        arriving[(n, j)], tok = _copies_start(local[(n, j)], False, f"gather_{n}_{j}")
        zero = zero + tok

    def get(n, j, after):
        return _with_own(_copies_wait(arriving[(n, j)], after), local[(n, j)])

    ordered = dict(w, mix_norm=mix_norm + zero, ffn_norm=ffn_norm + zero)
    full = _assemble_weights(ordered, get, small, f)

    leaving = {}

    def sink(kind, i, g):
        j = i // 3
        if kind == "ffn":
            c = 2 * f // N_DEV
            up = [g[key][:, k * c:(k + 1) * c] for key in ("wu", "wg") for k in range(N_DEV // 2)]
            pieces = [("ffn_w_up", i, jnp.stack(up)), ("ffn_w_down", i, _rows_to_shards(g["wd"][:f]))]
        elif kind == "att":
            pieces = [("att_w_qkv", j, _cols_to_shards(g["wqkv"])), ("att_w_o", j, _rows_to_shards(g["wo"]))]
        elif kind == "pool":
            pieces = [("pool_w", j, _to_shards("pool_w", g["w"][None])[:, 0])]
        else:
            win = jnp.concatenate([g["wmain"], g["wab"][:, :hv], g["wab"][:, LANE:LANE + hv]], axis=1)
            pieces = [("gdn_w_in", j, _cols_to_shards(win)), ("gdn_w_o", j, _rows_to_shards(g["wo"]))]
        tok = 0.0
        for n, l, shards in pieces:
            shards = shards if n in KEEP_F32 else shards.astype(BF)
            handle, t = _copies_start(shards, True, f"exchange_{n}_{l}")
            leaving[(n, l)] = (handle, lax.dynamic_index_in_dim(shards, me, 0, keepdims=False))
            tok = tok + t
        return tok

    loss_row, dx, grads = _local_step(x[0], loss_target[0], full, sink)
    gfull = _full_gradients(grads, w, f)

    out = {}
    for n in BIG:
        c = w[n].shape[-1]
        parts = []
        for l in range(w[n].shape[0]):
            handle, own = leaving[(n, l)]
            parts.append(_with_own(_copies_wait(handle, dx), own).reshape(N_DEV, -1, c))
        res = _adamw(parts, w[n].reshape(-1, c), m[n].reshape(-1, c), v[n].reshape(-1, c), f"adamw_{n}")
        out[n] = [a.reshape(w[n].shape) for a in res]
    sparts = _exchange(jnp.stack([_pack([_to_shards(n, gfull[n])[k] for n in SMALL_SHARDED]) for k in range(N_DEV)]),
                       "exchange_small")
    res = _adamw(sparts, *[_pack([t[n] for n in SMALL_SHARDED]) for t in (w, m, v)], "adamw_small")
    for n, *vals in zip(SMALL_SHARDED, *[_unpack(a, small_shapes) for a in res]):
        out[n] = vals
    rep_shapes = [w[n].shape for n in REPLICATED]
    rparts = _all_gather(_pack([gfull[n] for n in REPLICATED] + [loss_row[:, 0:1]]), "gather_replicated_grads")
    pad = jnp.zeros((1, 1), F32)
    res = _adamw(rparts, *[_pack([t[n] for n in REPLICATED] + [pad]) for t in (w, m, v)], "adamw_replicated")
    for n, *vals in zip(REPLICATED, *[_unpack(a, rep_shapes) for a in res]):
        out[n] = vals
    loss = jnp.sum(_unpack(rparts, rep_shapes + [(1, 1)], lead=(N_DEV,))[-1])
    return (loss, dx[None], *[out[n][0] for n in WEIGHTS], *[out[n][1] for n in WEIGHTS],
            *[out[n][2] for n in WEIGHTS], *[out[n][3] for n in WEIGHTS])
```
